```python
import math
import jax, jax.numpy as jnp
from jax import lax
import numpy as np

D_MODEL = 1024
BATCH = 4
SEQ = 4096
DEPTH = 1
DEC_BATCH = 32
DEC_SEQ = 4
PAST_LEN = 16384
PAGE_SIZE = 128

HEAD_DIM = 64
N_MLSTM_HEADS = 8
N_NSA_HEADS = 8
N_KV_HEADS = 2
GQ = N_NSA_HEADS // N_KV_HEADS
MLSTM_W = N_MLSTM_HEADS * HEAD_DIM
NSA_W = N_NSA_HEADS * HEAD_DIM
MIX_WIDTH = MLSTM_W + NSA_W
KV_W = N_KV_HEADS * HEAD_DIM
MLSTM_CHUNK = 64
MLSTM_CONV = 4
CMP_BLOCK = 32
SEL_BLOCK = 64
SEL_TOPK = 16
WINDOW = 512
NSA_QBLOCK = 64
N_BUCKETS = 32
MAX_DISTANCE = 128
D_FF = 2688
FFN_CONV = 3
ALPHA = (2.0 * DEPTH) ** 0.25
BETA = (8.0 * DEPTH) ** -0.25
LN_EPS = 1e-5
IN_SIZES = (2 * MLSTM_W, MLSTM_W, MLSTM_W, N_MLSTM_HEADS, N_MLSTM_HEADS,
            NSA_W, KV_W, KV_W, KV_W, KV_W, KV_W, KV_W, 3 * N_NSA_HEADS)
IN_WIDTH = sum(IN_SIZES)

kernel_name = "hymba_mlstm_nsa_convffn_decode_step"


def split_columns(z):
    out, start = [], 0
    for s in IN_SIZES:
        out.append(z[..., start:start + s])
        start += s
    return out


def layer_norm(x, w, b):
    xf = x.astype(jnp.float32)
    mu = jnp.mean(xf, -1, keepdims=True)
    var = jnp.mean(jnp.square(xf - mu), -1, keepdims=True)
    return ((xf - mu) * lax.rsqrt(var + LN_EPS) * w + b).astype(x.dtype)


def head_norm(h, w):
    mu = jnp.mean(h, -1, keepdims=True)
    var = jnp.mean(jnp.square(h - mu), -1, keepdims=True)
    hn = (h - mu) * lax.rsqrt(var + LN_EPS)
    return hn.reshape(h.shape[:2] + (-1,)) * w


def causal_dwconv(u, prev, w, b):
    k_w = w.shape[0]
    t_len = u.shape[1]
    full = jnp.concatenate([prev.astype(u.dtype), u], axis=1)
    out = b
    for j in range(k_w):
        out = out + full[:, j:j + t_len] * w[j]
    return out, full[:, full.shape[1] - (k_w - 1):]


def masked_softmax(s, mask, axes):
    s = jnp.where(mask, s.astype(jnp.float32), -jnp.inf)
    mx = jnp.max(s, axis=axes, keepdims=True)
    mx = jnp.where(jnp.isfinite(mx), mx, 0.0)
    e = jnp.exp(s - mx)
    den = jnp.sum(e, axis=axes, keepdims=True)
    return e / jnp.maximum(den, 1e-30)


def t5_bucket(dist):
    n = jnp.maximum(dist, 0)
    max_exact = N_BUCKETS // 2
    large = max_exact + (jnp.log(jnp.maximum(n, 1).astype(jnp.float32) / max_exact)
                         / math.log(MAX_DISTANCE / max_exact) * (N_BUCKETS - max_exact)).astype(jnp.int32)
    large = jnp.minimum(large, N_BUCKETS - 1)
    return jnp.where(n < max_exact, n, large)


def head_bias(rel_bias, dist):
    b = rel_bias[t5_bucket(dist)].astype(jnp.float32)
    b = b.reshape(dist.shape + (N_KV_HEADS, GQ))
    return jnp.transpose(b, (2, 3, 0, 1))


def group_bias(rel_bias, dist):
    table = jnp.transpose(rel_bias.reshape(N_BUCKETS, N_KV_HEADS, GQ), (1, 0, 2))
    buckets = t5_bucket(dist)
    b = jax.vmap(lambda tb, bk: tb[bk], in_axes=(0, 1), out_axes=1)(table, buckets)
    return jnp.moveaxis(b, -1, 2).astype(jnp.float32)


def mlstm_chunkwise(q, k, v, ig, lf, c0, n0, m0):
    f32 = jnp.float32
    bsz, t_len, n_h, d = q.shape
    q, k, v, ig, lf = (a.astype(f32) for a in (q, k, v, ig, lf))
    k = k * (d ** -0.5)
    c_len = MLSTM_CHUNK if t_len % MLSTM_CHUNK == 0 else t_len
    n_c = t_len // c_len

    def to_chunks(a):
        return jnp.moveaxis(a.reshape((bsz, n_c, c_len) + a.shape[2:]), 1, 0)

    causal = jnp.tril(jnp.ones((c_len, c_len), bool))

    def step(carry, xs):
        c_st, n_st, m_st = carry
        qc, kc, vc, ic, fc = xs
        b = jnp.moveaxis(jnp.cumsum(fc, axis=1), 2, 1)
        ih = jnp.moveaxis(ic, 2, 1)
        dmat = jnp.where(causal, b[..., :, None] - b[..., None, :] + ih[..., None, :], -jnp.inf)
        inter = b + m_st[..., None]
        m_t = jnp.maximum(inter, jnp.max(dmat, -1))
        w_intra = jnp.exp(dmat - m_t[..., None])
        w_inter = jnp.exp(inter - m_t)
        a = w_intra * jnp.einsum('blhd,bshd->bhls', qc, kc)
        num = (jnp.einsum('bhls,bshd->blhd', a, vc)
               + jnp.einsum('bhed,blhd->blhe', c_st, qc) * jnp.moveaxis(w_inter, 1, 2)[..., None])
        den = jnp.sum(a, -1) + w_inter * jnp.einsum('bhd,blhd->bhl', n_st, qc)
        den = jnp.maximum(jnp.abs(den), jnp.exp(-m_t))
        h = num / jnp.moveaxis(den, 1, 2)[..., None]
        b_end = b[..., -1]
        dec = b_end[..., None] - b + ih
        m_new = jnp.maximum(b_end + m_st, jnp.max(dec, -1))
        wk = jnp.exp(dec - m_new[..., None])
        sc = jnp.exp(b_end + m_st - m_new)
        c_new = sc[..., None, None] * c_st + jnp.einsum('bhl,blhe,blhd->bhed', wk, vc, kc)
        n_new = sc[..., None] * n_st + jnp.einsum('bhl,blhd->bhd', wk, kc)
        return (c_new, n_new, m_new), h

    carry0 = (c0.astype(f32), n0.astype(f32), m0.astype(f32))
    (c_f, n_f, m_f), hs = lax.scan(step, carry0, tuple(to_chunks(a) for a in (q, k, v, ig, lf)))
    h = jnp.moveaxis(hs, 0, 1).reshape(bsz, t_len, n_h, d)
    return h, c_f, n_f, m_f


def nsa_attend(q, q_pos0, kv_all, win_all, w_pos0, pool_w, rel_bias):
    bsz, t_q = q.shape[:2]
    t_k = kv_all.shape[1]
    qg = (q.astype(jnp.float32) * HEAD_DIM ** -0.5).reshape(bsz, t_q, N_KV_HEADS, GQ, HEAD_DIM)
    n_cmp = t_k // CMP_BLOCK
    cmp_rows = kv_all[:, :n_cmp * CMP_BLOCK, :2].reshape(bsz, n_cmp, CMP_BLOCK, 2, N_KV_HEADS, HEAD_DIM)
    kv_cmp = jnp.einsum('bcjegd,ejg->ebcgd', cmp_rows, pool_w)
    k_cmp, v_cmp = kv_cmp[0], kv_cmp[1]
    cmp_end = jnp.arange(n_cmp) * CMP_BLOCK + (CMP_BLOCK - 1)
    n_sel = -(-t_k // SEL_BLOCK)
    sel_rows = jnp.pad(kv_all[:, :, 2:], ((0, 0), (0, n_sel * SEL_BLOCK - t_k), (0, 0), (0, 0), (0, 0)))
    sel_blocks = jnp.transpose(sel_rows.reshape(bsz, n_sel, SEL_BLOCK, 2, N_KV_HEADS, HEAD_DIM),
                               (0, 4, 1, 2, 3, 5))
    top_k = min(SEL_TOPK, n_sel)
    ratio = SEL_BLOCK // CMP_BLOCK
    win_pad = jnp.pad(win_all, ((0, 0), (WINDOW, 0), (0, 0), (0, 0), (0, 0)))
    qb_len = NSA_QBLOCK if t_q % NSA_QBLOCK == 0 else t_q
    gather = jax.vmap(jax.vmap(lambda blocks, idx: blocks[idx]))

    def one_block(i):
        qs = i * qb_len
        qb = lax.dynamic_slice_in_dim(qg, qs, qb_len, axis=1)
        t0 = q_pos0 + qs
        t = t0 + jnp.arange(qb_len)
        dist_c = t[:, None] - cmp_end[None, :]
        s_c = jnp.einsum('bqgjd,bcgd->bgjqc', qb, k_cmp) + head_bias(rel_bias, dist_c)
        p_c = masked_softmax(s_c, dist_c >= 0, (-1,))
        o_c = jnp.einsum('bgjqc,bcgd->bqgjd', p_c, v_cmp)
        imp = jnp.pad(jnp.sum(p_c, 2), ((0, 0), (0, 0), (0, 0), (0, n_sel * ratio - n_cmp)))
        imp = imp.reshape(bsz, N_KV_HEADS, qb_len, n_sel, ratio).sum(-1)
        blk = jnp.arange(n_sel)[None, :]
        cur = (t // SEL_BLOCK)[:, None]
        score = jnp.where(blk == cur, jnp.inf, jnp.where(blk < cur, imp, -jnp.inf))
        top_s, top_i = lax.top_k(score, top_k)
        kv_sel = gather(sel_blocks, top_i)
        key_pos = top_i[..., None] * SEL_BLOCK + jnp.arange(SEL_BLOCK)
        dist_s = t[:, None, None] - key_pos
        mask_s = (top_s > -jnp.inf)[..., None] & (dist_s >= 0)
        s_s = jnp.einsum('bqgjd,bgqksd->bgjqks', qb, kv_sel[..., 0, :]) + group_bias(rel_bias, dist_s)
        p_s = masked_softmax(s_s, mask_s[:, :, None], (-2, -1))
        o_s = jnp.einsum('bgjqks,bgqksd->bqgjd', p_s, kv_sel[..., 1, :])
        kv_w = lax.dynamic_slice_in_dim(win_pad, t0 - w_pos0, WINDOW + qb_len, axis=1)
        wpos = t0 - WINDOW + jnp.arange(WINDOW + qb_len)
        dist_w = t[:, None] - wpos[None, :]
        mask_w = (dist_w >= 0) & (dist_w <= WINDOW) & (wpos[None, :] >= w_pos0)
        s_w = jnp.einsum('bqgjd,bsgd->bgjqs', qb, kv_w[:, :, 0]) + head_bias(rel_bias, dist_w)
        p_w = masked_softmax(s_w, mask_w, (-1,))
        o_w = jnp.einsum('bgjqs,bsgd->bqgjd', p_w, kv_w[:, :, 1])
        return o_c, o_s, o_w

    outs = lax.map(one_block, jnp.arange(t_q // qb_len))

    def unblock(o):
        return jnp.moveaxis(o, 0, 1).reshape(bsz, t_q, N_NSA_HEADS, HEAD_DIM)

    return unblock(outs[0]), unblock(outs[1]), unblock(outs[2])


def hybrid_layer(x, pos0, kv_past, win_past, c0, n0, m0, mconv_prev, fconv_prev,
                 w_in, b_in, mlstm_conv_w, mlstm_conv_b, mlstm_norm_w, nsa_pool_w, rel_bias,
                 w_out, ln1_w, ln1_b, w_up, b_up, ffn_conv_w, ffn_conv_b, w_down, b_down, ln2_w, ln2_b):
    bsz, t_len, _ = x.shape
    z = jnp.einsum('btd,de->bte', x, w_in) + b_in
    m_qk, m_v, m_o, m_i, m_f, n_q, c_k, c_v, s_k, s_v, w_k, w_v, n_g = split_columns(z)
    qk, mconv_new = causal_dwconv(m_qk, mconv_prev, mlstm_conv_w, mlstm_conv_b)
    qk = jax.nn.silu(qk)
    hm = lambda a: a.reshape(bsz, t_len, N_MLSTM_HEADS, HEAD_DIM)
    h, c_new, n_new, m_new = mlstm_chunkwise(hm(qk[..., :MLSTM_W]), hm(qk[..., MLSTM_W:]), hm(m_v),
                                             m_i, jax.nn.log_sigmoid(m_f.astype(jnp.float32)), c0, n0, m0)
    mix_m = jax.nn.sigmoid(m_o.astype(jnp.float32)) * head_norm(h, mlstm_norm_w)
    hk = lambda a: a.reshape(bsz, t_len, N_KV_HEADS, HEAD_DIM)
    kv_new = jnp.stack([hk(c_k), hk(c_v), hk(s_k), hk(s_v)], axis=2)
    win_new = jnp.stack([hk(w_k), hk(w_v)], axis=2)
    kv_all = kv_new if kv_past is None else jnp.concatenate([kv_past.astype(kv_new.dtype), kv_new], axis=1)
    win_all = win_new if win_past is None else jnp.concatenate([win_past.astype(win_new.dtype), win_new], axis=1)
    w_pos0 = pos0 + t_len - win_all.shape[1]
    o_c, o_s, o_w = nsa_attend(n_q.reshape(bsz, t_len, N_NSA_HEADS, HEAD_DIM), pos0, kv_all, win_all,
                               w_pos0, nsa_pool_w, rel_bias)
    g = jax.nn.sigmoid(n_g.astype(jnp.float32)).reshape(bsz, t_len, 3, N_NSA_HEADS)[..., None]
    mix_n = (g[:, :, 0] * o_c + g[:, :, 1] * o_s + g[:, :, 2] * o_w).reshape(bsz, t_len, NSA_W)
    mix = jnp.concatenate([mix_m, mix_n], axis=-1).astype(x.dtype)
    x1 = layer_norm(ALPHA * x + mix @ w_out, ln1_w, ln1_b)
    u, fconv_new = causal_dwconv(x1 @ w_up + b_up, fconv_prev, ffn_conv_w, ffn_conv_b)
    ff = (jax.nn.gelu(u[..., :D_FF]) * u[..., D_FF:]) @ w_down + b_down
    y = layer_norm(ALPHA * x1 + ff, ln2_w, ln2_b)
    win_state = win_all[:, win_all.shape[1] - min(WINDOW, pos0 + t_len):]
    return y, kv_new, win_state, c_new, n_new, m_new, mconv_new, fconv_new


def setup_inputs(seed: int = 0) -> dict:
    key = jax.random.key(seed)
    ks = jax.random.split(key, 32)
    nrm = lambda i, shape: jax.random.normal(ks[i], shape, jnp.float32)
    n_pages = PAST_LEN // PAGE_SIZE
    n_used = DEC_BATCH * n_pages
    n_pool = n_used + max(n_used // 4, 1)
    win_buf = min(WINDOW, PAST_LEN)
    page_table = jax.random.permutation(ks[0], n_pool)[:n_used].reshape(DEC_BATCH, n_pages).astype(jnp.int32)
    f_start = 2 * MLSTM_W + 2 * MLSTM_W + N_MLSTM_HEADS
    b_in = 0.02 * nrm(1, (IN_WIDTH,))
    b_in = b_in.at[f_start:f_start + N_MLSTM_HEADS].add(jnp.linspace(3.0, 6.0, N_MLSTM_HEADS))
    return {
        'x_prompt': nrm(2, (BATCH, SEQ, D_MODEL)),
        'x_sample': nrm(3, (DEC_BATCH, DEC_SEQ, D_MODEL)),
        'cache_nsa_kv': nrm(4, (n_pool, PAGE_SIZE, 4, N_KV_HEADS, HEAD_DIM)),
        'cache_win_kv': nrm(5, (DEC_BATCH, win_buf, 2, N_KV_HEADS, HEAD_DIM)),
        'state_mlstm_c': 0.3 * nrm(6, (DEC_BATCH, N_MLSTM_HEADS, HEAD_DIM, HEAD_DIM)),
        'state_mlstm_n': 0.3 * nrm(7, (DEC_BATCH, N_MLSTM_HEADS, HEAD_DIM)),
        'state_mlstm_m': nrm(8, (DEC_BATCH, N_MLSTM_HEADS)),
        'state_mlstm_conv': nrm(9, (DEC_BATCH, MLSTM_CONV - 1, 2 * MLSTM_W)),
        'state_ffn_conv': nrm(10, (DEC_BATCH, FFN_CONV - 1, 2 * D_FF)),
        'page_table': page_table,
        'w_in': nrm(11, (D_MODEL, IN_WIDTH)) * D_MODEL ** -0.5,
        'b_in': b_in,
        'mlstm_conv_w': nrm(12, (MLSTM_CONV, 2 * MLSTM_W)) * MLSTM_CONV ** -0.5,
        'mlstm_conv_b': 0.02 * nrm(13, (2 * MLSTM_W,)),
        'mlstm_norm_w': 1.0 + 0.02 * nrm(14, (MLSTM_W,)),
        'nsa_pool_w': CMP_BLOCK ** -0.5 * (1.0 + 0.1 * nrm(15, (2, CMP_BLOCK, N_KV_HEADS))),
        'rel_bias': 0.1 * nrm(16, (N_BUCKETS, N_NSA_HEADS)),
        'w_out': nrm(17, (MIX_WIDTH, D_MODEL)) * MIX_WIDTH ** -0.5 * BETA,
        'ln1_w': 1.0 + 0.02 * nrm(18, (D_MODEL,)),
        'ln1_b': 0.02 * nrm(19, (D_MODEL,)),
        'w_up': nrm(20, (D_MODEL, 2 * D_FF)) * D_MODEL ** -0.5,
        'b_up': 0.02 * nrm(21, (2 * D_FF,)),
        'ffn_conv_w': nrm(22, (FFN_CONV, 2 * D_FF)) * FFN_CONV ** -0.5,
        'ffn_conv_b': 0.02 * nrm(23, (2 * D_FF,)),
        'w_down': nrm(24, (D_FF, D_MODEL)) * D_FF ** -0.5 * BETA,
        'b_down': 0.02 * nrm(25, (D_MODEL,)),
        'ln2_w': 1.0 + 0.02 * nrm(26, (D_MODEL,)),
        'ln2_b': 0.02 * nrm(27, (D_MODEL,)),
    }


def reference(x_prompt, x_sample, cache_nsa_kv, cache_win_kv, state_mlstm_c, state_mlstm_n, state_mlstm_m,
              state_mlstm_conv, state_ffn_conv, page_table,
              w_in, b_in, mlstm_conv_w, mlstm_conv_b, mlstm_norm_w, nsa_pool_w, rel_bias, w_out,
              ln1_w, ln1_b, w_up, b_up, ffn_conv_w, ffn_conv_b, w_down, b_down, ln2_w, ln2_b):
    weights = (w_in, b_in, mlstm_conv_w, mlstm_conv_b, mlstm_norm_w, nsa_pool_w, rel_bias, w_out,
               ln1_w, ln1_b, w_up, b_up, ffn_conv_w, ffn_conv_b, w_down, b_down, ln2_w, ln2_b)
    bsz = x_prompt.shape[0]
    f32 = jnp.float32
    for _layer in range(DEPTH):
        (y_p, kv_p, win_p, c_p, n_p, m_p, mconv_p, fconv_p) = hybrid_layer(
            x_prompt, 0, None, None,
            jnp.zeros((bsz, N_MLSTM_HEADS, HEAD_DIM, HEAD_DIM), f32),
            jnp.zeros((bsz, N_MLSTM_HEADS, HEAD_DIM), f32),
            jnp.zeros((bsz, N_MLSTM_HEADS), f32),
            jnp.zeros((bsz, MLSTM_CONV - 1, 2 * MLSTM_W), x_prompt.dtype),
            jnp.zeros((bsz, FFN_CONV - 1, 2 * D_FF), x_prompt.dtype),
            *weights)
    dec_b, n_pages = page_table.shape
    kv_past = cache_nsa_kv[page_table].reshape(dec_b, n_pages * PAGE_SIZE, 4, N_KV_HEADS, HEAD_DIM)
    past_len = n_pages * PAGE_SIZE
    for _layer in range(DEPTH):
        (y_s, kv_s, win_s, c_s, n_s, m_s, mconv_s, fconv_s) = hybrid_layer(
            x_sample, past_len, kv_past, cache_win_kv, state_mlstm_c, state_mlstm_n, state_mlstm_m,
            state_mlstm_conv, state_ffn_conv, *weights)
    return (y_p, y_s, kv_p, kv_s, win_p, win_s, c_p, c_s, n_p, n_s, m_p, m_s, mconv_p, mconv_s, fconv_p, fconv_s)
```

```python
import functools
import math

import jax
import jax.numpy as jnp
from jax import lax
from jax.experimental import pallas as pl
from jax.experimental.pallas import tpu as pltpu

F32 = jnp.float32
BF16 = jnp.bfloat16
I32 = jnp.int32

D_MODEL = 1024
HEAD_DIM = 64
N_MLSTM_HEADS = 8
N_NSA_HEADS = 8
N_KV_HEADS = 2
GQ = N_NSA_HEADS // N_KV_HEADS
MLSTM_W = N_MLSTM_HEADS * HEAD_DIM
NSA_W = N_NSA_HEADS * HEAD_DIM
KV_W = N_KV_HEADS * HEAD_DIM
MLSTM_CHUNK = 64
MLSTM_CONV = 4
CMP_BLOCK = 32
SEL_BLOCK = 64
SEL_TOPK = 16
WINDOW = 512
NSA_QBLOCK = 64
N_BUCKETS = 32
MAX_DISTANCE = 128
D_FF = 2688
FFN_CONV = 3
DEPTH = 1
ALPHA = (2.0 * DEPTH) ** 0.25
LN_EPS = 1e-5
PAGE_SIZE = 128

LANES = 128
VMEM_LIMIT = 56 * 1024 * 1024
NEG_INF = float("-inf")

SEQ_PAD = 8
N_NEW = 4
N_QROWS = N_NSA_HEADS * SEQ_PAD


def _bucket_thresholds():
    max_exact = N_BUCKETS // 2
    thr = list(range(max_exact + 1))
    for k in range(1, N_BUCKETS - max_exact):
        thr.append(int(math.ceil(max_exact * (MAX_DISTANCE / max_exact) ** (k / (N_BUCKETS - max_exact)))))
    return tuple(thr)


BUCKET_THR = _bucket_thresholds()
FAR_DIST = BUCKET_THR[N_BUCKETS - 1]


def _dot(a, b, precision=None):
    return jnp.dot(a, b, preferred_element_type=F32, precision=precision)


def _dot_nt(a, b, precision=None):
    return lax.dot_general(a, b, (((1,), (1,)), ((), ())), preferred_element_type=F32, precision=precision)


def _dot_tn(a, b):
    return lax.dot_general(a, b, (((0,), (0,)), ((), ())), preferred_element_type=F32)


def _layer_norm(x, w, b):
    mu = jnp.mean(x, -1, keepdims=True)
    xc = x - mu
    var = jnp.mean(xc * xc, -1, keepdims=True)
    return xc * lax.rsqrt(var + LN_EPS) * w + b


def _log_sigmoid(x):
    return jnp.minimum(x, 0.0) - jnp.log1p(jnp.exp(-jnp.abs(x)))


def _bias_from_dist(dist, rb_ref, head):
    acc = jnp.full(dist.shape, rb_ref[0, head], F32)
    for k in range(1, N_BUCKETS):
        acc = jnp.where(dist >= BUCKET_THR[k], rb_ref[k, head], acc)
    return acc


def _masked_softmax(s):
    mx = jnp.max(s, -1, keepdims=True)
    mx = jnp.where(mx > NEG_INF, mx, 0.0)
    e = jnp.exp(s - mx)
    den = jnp.sum(e, -1, keepdims=True)
    return e / jnp.maximum(den, 1e-30)


_C_QK, _C_V, _C_O, _C_NQ, _C_KV, _C_WIN, _C_G = 0, 1024, 1536, 2048, 2560, 3072, 3328
IN_PAD = 3456
G_IGATE, G_FGATE, G_NSA = 0, N_MLSTM_HEADS, 2 * N_MLSTM_HEADS


def _proj_in_body(x_ref, w_ref, b_ref, qk_ref, v_ref, o_ref, nq_ref, kv_ref, win_ref, g_ref):
    z = _dot(x_ref[...].astype(BF16), w_ref[...]) + b_ref[...]
    qk_ref[...] = z[:, _C_QK:_C_V]
    v_ref[...] = z[:, _C_V:_C_O]
    o_ref[...] = z[:, _C_O:_C_NQ]
    nq_ref[...] = z[:, _C_NQ:_C_KV]
    kv_ref[...] = z[:, _C_KV:_C_WIN]
    win_ref[...] = z[:, _C_WIN:_C_G]
    g_ref[...] = z[:, _C_G:IN_PAD]


def _proj_in(x2d, w_p, b_p):
    rows = x2d.shape[0]
    tm = min(rows, 256)
    widths = (1024, 512, 512, 512, 512, 256, 128)
    const = lambda i: (0, 0)
    return pl.pallas_call(
        _proj_in_body,
        grid=(rows // tm,),
        in_specs=[pl.BlockSpec((tm, D_MODEL), lambda i: (i, 0)),
                  pl.BlockSpec((D_MODEL, IN_PAD), const, pipeline_mode=pl.Buffered(1)),
                  pl.BlockSpec((1, IN_PAD), const, pipeline_mode=pl.Buffered(1))],
        out_specs=[pl.BlockSpec((tm, w), lambda i: (i, 0)) for w in widths],
        out_shape=[jax.ShapeDtypeStruct((rows, w), F32) for w in widths],
        compiler_params=pltpu.CompilerParams(dimension_semantics=("arbitrary",),
                                             vmem_limit_bytes=VMEM_LIMIT),
        name="proj_in",
    )(x2d, w_p, b_p)


def _mlstm_body(*refs, tb, n_dummy, has_state):
    if has_state:
        (zqk_ref, zv_ref, zo_ref, zg_ref, c0_ref, n0_ref, m0_ref, cw_ref, cb_ref, nw_ref,
         mix_ref, c_out, n_out, m_out, ubuf, qk_sc, c_sc, n_sc, m_sc) = refs
    else:
        (zqk_ref, zv_ref, zo_ref, zg_ref, cw_ref, cb_ref, nw_ref,
         mix_ref, c_out, n_out, m_out, ubuf, qk_sc, c_sc, n_sc, m_sc) = refs
    L = MLSTM_CHUNK
    D = HEAD_DIM
    j = pl.program_id(1)

    @pl.when(j == 0)
    def _init():
        ubuf[0:8, :] = jnp.zeros((8, 2 * MLSTM_W), F32)
        if has_state:
            c_sc[...] = c0_ref[0]
            n_sc[...] = n0_ref[0]
            m_sc[...] = m0_ref[0]
        else:
            c_sc[...] = jnp.zeros(c_sc.shape, F32)
            n_sc[...] = jnp.zeros(n_sc.shape, F32)
            m_sc[...] = jnp.zeros(m_sc.shape, F32)

    u = zqk_ref[0]
    ubuf[8:8 + tb, :] = u
    conv = cb_ref[...] + ubuf[5:5 + tb, :] * cw_ref[0:1, :]
    conv = conv + ubuf[6:6 + tb, :] * cw_ref[1:2, :]
    conv = conv + ubuf[7:7 + tb, :] * cw_ref[2:3, :]
    conv = conv + u * cw_ref[3:4, :]
    ubuf[5:8, :] = u[tb - 3:tb, :]
    qk_sc[...] = conv * jax.nn.sigmoid(conv)

    row_i = lax.broadcasted_iota(I32, (L, L), 0)
    col_i = lax.broadcasted_iota(I32, (L, L), 1)
    causal = col_i <= row_i
    tri = causal.astype(F32)
    eye_g = (lax.broadcasted_iota(I32, (2 * N_MLSTM_HEADS, LANES), 0)
             == lax.broadcasted_iota(I32, (2 * N_MLSTM_HEADS, LANES), 1)).astype(F32)
    hi = lax.Precision.HIGHEST

    def chunk(ci, carry):
        r0 = pl.multiple_of(ci * L, L)
        g = zg_ref[0, pl.ds(r0, L), :]
        if n_dummy:
            lane = lax.broadcasted_iota(I32, (L, LANES), 1)
            dummy = (lax.broadcasted_iota(I32, (L, LANES), 0) + ci * L) < n_dummy
            g = jnp.where(dummy, jnp.where(lane < G_FGATE, -1e30, 1e30), g)
        g_t = _dot_nt(eye_g, g, hi)
        lf = _log_sigmoid(g)
        lf_t = _log_sigmoid(g_t[G_FGATE:G_FGATE + N_MLSTM_HEADS, :])
        ig_t = g_t[G_IGATE:G_IGATE + N_MLSTM_HEADS, :]
        b_all = _dot(tri, lf, hi)
        b_t = _dot_nt(lf_t, tri, hi)
        for h in range(N_MLSTM_HEADS):
            hs = slice(h * D, (h + 1) * D)
            q_h = qk_sc[pl.ds(r0, L), hs]
            k_h = qk_sc[pl.ds(r0, L), MLSTM_W + h * D:MLSTM_W + (h + 1) * D] * (D ** -0.5)
            v_h = zv_ref[0, pl.ds(r0, L), hs]
            bc = b_all[:, G_FGATE + h:G_FGATE + h + 1]
            ic = g[:, G_IGATE + h:G_IGATE + h + 1]
            br = b_t[h:h + 1, :]
            ir = ig_t[h:h + 1, :]
            m_h = m_sc[0:1, h:h + 1]
            c_h = c_sc[h]
            n_h = n_sc[h:h + 1, :]
            dmat = jnp.where(causal, bc - br + ir, NEG_INF)
            inter = bc + m_h
            m_t = jnp.maximum(inter, jnp.max(dmat, -1, keepdims=True))
            w_intra = jnp.exp(dmat - m_t)
            w_inter = jnp.exp(inter - m_t)
            qb = q_h.astype(BF16)
            kb = k_h.astype(BF16)
            vb = v_h.astype(BF16)
            a = w_intra * _dot_nt(qb, kb)
            num = _dot(a.astype(BF16), vb) + _dot_nt(qb, c_h.astype(BF16)) * w_inter
            den = jnp.sum(a, -1, keepdims=True) + w_inter * jnp.sum(q_h * n_h, -1, keepdims=True)
            den = jnp.maximum(jnp.abs(den), jnp.exp(-m_t))
            hh = num / den
            b_end = bc[L - 1:L, :]
            m_new = jnp.maximum(b_end + m_h, jnp.max(b_end - br + ir, -1, keepdims=True))
            wk = jnp.exp(b_end - bc + ic - m_new)
            sc = jnp.exp(b_end + m_h - m_new)
            c_sc[h] = sc * c_h + _dot_tn((v_h * wk).astype(BF16), kb)
            n_sc[h:h + 1, :] = sc * n_h + jnp.sum(wk * k_h, 0, keepdims=True)
            m_sc[0:1, h:h + 1] = m_new
            mu = jnp.mean(hh, -1, keepdims=True)
            hc = hh - mu
            var = jnp.mean(hc * hc, -1, keepdims=True)
            hn = hc * lax.rsqrt(var + LN_EPS)
            og = jax.nn.sigmoid(zo_ref[0, pl.ds(r0, L), hs])
            mix_ref[0, pl.ds(r0, L), hs] = og * (hn * nw_ref[:, hs])
        return carry

    lax.fori_loop(0, tb // L, chunk, 0)

    @pl.when(j == pl.num_programs(1) - 1)
    def _final():
        c_out[0] = c_sc[...]
        n_out[0] = n_sc[...]
        m_out[0] = m_sc[...]


def _mlstm(zqk, zv, zo, zg, state, conv_w, conv_b, norm_w, n_dummy):
    bsz, t_len, _ = zqk.shape
    tb = min(t_len, 256)
    has_state = state is not None
    row = lambda w: pl.BlockSpec((1, tb, w), lambda b, j: (b, j, 0))
    per_b = lambda shape: pl.BlockSpec((1,) + shape, lambda b, j: (b,) + (0,) * len(shape))
    const = lambda shape: pl.BlockSpec(shape, lambda b, j: (0,) * len(shape))
    in_specs = [row(2 * MLSTM_W), row(MLSTM_W), row(MLSTM_W), row(LANES)]
    args = [zqk, zv, zo, zg]
    if has_state:
        c0, n0, m0 = state
        in_specs += [per_b((N_MLSTM_HEADS, HEAD_DIM, HEAD_DIM)), per_b((N_MLSTM_HEADS, HEAD_DIM)),
                     per_b((1, N_MLSTM_HEADS))]
        args += [c0, n0, m0.reshape(bsz, 1, N_MLSTM_HEADS)]
    in_specs += [const((MLSTM_CONV, 2 * MLSTM_W)), const((1, 2 * MLSTM_W)), const((1, MLSTM_W))]
    args += [conv_w, conv_b.reshape(1, -1), norm_w.reshape(1, -1)]
    mix, c_new, n_new, m_new = pl.pallas_call(
        functools.partial(_mlstm_body, tb=tb, n_dummy=n_dummy, has_state=has_state),
        grid=(bsz, t_len // tb),
        in_specs=in_specs,
        out_specs=[row(MLSTM_W), per_b((N_MLSTM_HEADS, HEAD_DIM, HEAD_DIM)),
                   per_b((N_MLSTM_HEADS, HEAD_DIM)), per_b((1, N_MLSTM_HEADS))],
        out_shape=[jax.ShapeDtypeStruct((bsz, t_len, MLSTM_W), F32),
                   jax.ShapeDtypeStruct((bsz, N_MLSTM_HEADS, HEAD_DIM, HEAD_DIM), F32),
                   jax.ShapeDtypeStruct((bsz, N_MLSTM_HEADS, HEAD_DIM), F32),
                   jax.ShapeDtypeStruct((bsz, 1, N_MLSTM_HEADS), F32)],
        scratch_shapes=[pltpu.VMEM((8 + tb, 2 * MLSTM_W), F32),
                        pltpu.VMEM((tb, 2 * MLSTM_W), F32),
                        pltpu.VMEM((N_MLSTM_HEADS, HEAD_DIM, HEAD_DIM), F32),
                        pltpu.VMEM((N_MLSTM_HEADS, HEAD_DIM), F32),
                        pltpu.VMEM((1, N_MLSTM_HEADS), F32)],
        compiler_params=pltpu.CompilerParams(dimension_semantics=("arbitrary", "arbitrary"),
                                             vmem_limit_bytes=VMEM_LIMIT),
        name="mlstm",
    )(*args)
    return mix, c_new, n_new, m_new.reshape(bsz, N_MLSTM_HEADS)


def _block_diag_queries(x, n_rows):
    lane = lax.broadcasted_iota(I32, (n_rows, LANES), 1)
    low = lane < HEAD_DIM
    pieces = []
    for h in range(N_NSA_HEADS):
        pair = x[:, (h // 2) * LANES:(h // 2 + 1) * LANES]
        src_low = h % 2 == 0
        dst_low = h // GQ == 0
        if src_low != dst_low:
            pair = pltpu.roll(pair, HEAD_DIM, 1)
        pieces.append(jnp.where(low if dst_low else ~low, pair, 0.0))
    return (jnp.concatenate(pieces, axis=0) * (HEAD_DIM ** -0.5)).astype(BF16)


def _group_lanes(o, g):
    return o[:, g * HEAD_DIM:(g + 1) * HEAD_DIM]


def _gated_mix(gates, o_c, o_s, o_w, h, n_rows):
    g = h // GQ
    rs = slice(h * n_rows, (h + 1) * n_rows)
    col = lambda branch: gates[:, G_NSA + branch * N_NSA_HEADS + h:G_NSA + branch * N_NSA_HEADS + h + 1]
    return (col(0) * _group_lanes(o_c[rs], g) + col(1) * _group_lanes(o_s[rs], g)
            + col(2) * _group_lanes(o_w[rs], g))


SEL_TILE = 256
SEL_FRONT = SEL_TILE - SEL_BLOCK
WIN_BAND = WINDOW + NSA_QBLOCK


def _nsa_prompt_body(rb_ref, nq_ref, kv_ref, win_ref, zg_ref, pw_ref, out_ref,
                     ks_sc, vs_sc, wk_sc, wv_sc, kc_sc, vc_sc, bs_sc, bw_sc, fc_sc, far_sc,
                     m_sc, l_sc, acc_sc, *, t_len):
    QB = NSA_QBLOCK
    NR = N_NSA_HEADS * QB
    n_cmp = t_len // CMP_BLOCK
    i = pl.program_id(1)
    hi = lax.Precision.HIGHEST

    @pl.when(i == 0)
    def _init():
        ks_sc[0:SEL_FRONT, :] = jnp.zeros((SEL_FRONT, LANES), BF16)
        vs_sc[0:SEL_FRONT, :] = jnp.zeros((SEL_FRONT, LANES), BF16)
        wk_sc[0:WINDOW, :] = jnp.zeros((WINDOW, LANES), BF16)
        wv_sc[0:WINDOW, :] = jnp.zeros((WINDOW, LANES), BF16)
        rows_per = 512
        blocks_per = rows_per // CMP_BLOCK

        def fill(c, carry):
            r0 = pl.multiple_of(c * rows_per, rows_per)
            kv = kv_ref[0, pl.ds(r0, rows_per), :]
            ks_sc[pl.ds(SEL_FRONT + r0, rows_per), :] = kv[:, 2 * LANES:3 * LANES].astype(BF16)
            vs_sc[pl.ds(SEL_FRONT + r0, rows_per), :] = kv[:, 3 * LANES:4 * LANES].astype(BF16)
            w = win_ref[0, pl.ds(r0, rows_per), :]
            wk_sc[pl.ds(WINDOW + r0, rows_per), :] = w[:, 0:LANES].astype(BF16)
            wv_sc[pl.ds(WINDOW + r0, rows_per), :] = w[:, LANES:2 * LANES].astype(BF16)
            pooled = jnp.sum(kv[:, 0:2 * LANES].reshape(blocks_per, CMP_BLOCK, 2 * LANES) * pw_ref[...][None],
                             axis=1)
            c0 = pl.multiple_of(c * blocks_per, blocks_per)
            kc_sc[pl.ds(c0, blocks_per), :] = pooled[:, 0:LANES].astype(BF16)
            vc_sc[pl.ds(c0, blocks_per), :] = pooled[:, LANES:2 * LANES].astype(BF16)
            return carry

        lax.fori_loop(0, t_len // rows_per, fill, 0)

        for h in range(N_NSA_HEADS):
            rs = slice(h * QB, (h + 1) * QB)
            q_i = lax.broadcasted_iota(I32, (QB, SEL_TILE), 0)
            s_i = lax.broadcasted_iota(I32, (QB, SEL_TILE), 1)
            dist = q_i + SEL_FRONT - s_i
            bs_sc[rs, :] = jnp.where(dist >= 0, _bias_from_dist(dist, rb_ref, h), NEG_INF)
            q_i = lax.broadcasted_iota(I32, (QB, WIN_BAND), 0)
            s_i = lax.broadcasted_iota(I32, (QB, WIN_BAND), 1)
            dist = q_i + WINDOW - s_i
            bw_sc[rs, :] = jnp.where((dist >= 0) & (dist <= WINDOW), _bias_from_dist(dist, rb_ref, h), NEG_INF)
            q_i = lax.broadcasted_iota(I32, (QB, n_cmp), 0)
            c_i = lax.broadcasted_iota(I32, (QB, n_cmp), 1)
            rel = jnp.where(c_i < n_cmp // 2, c_i, c_i - n_cmp)
            dist = q_i - CMP_BLOCK * rel - (CMP_BLOCK - 1)
            dist = jnp.where((c_i >= 2) & (c_i < n_cmp // 2), FAR_DIST, dist)
            fc_sc[rs, :] = _bias_from_dist(dist, rb_ref, h)
            far_sc[rs, :] = jnp.full((QB, 1), rb_ref[N_BUCKETS - 1, h], F32)

    qbd = _block_diag_queries(nq_ref[0], QB)

    s_c = _dot_nt(qbd, kc_sc[...]) + pltpu.roll(fc_sc[...], 2 * i, 1)
    q_i = lax.broadcasted_iota(I32, (QB, n_cmp), 0)
    c_i = lax.broadcasted_iota(I32, (QB, n_cmp), 1)
    valid_c = (c_i * CMP_BLOCK + (CMP_BLOCK - 1)) <= (i * QB + q_i)
    s_c = jnp.where(valid_c[None], s_c.reshape(N_NSA_HEADS, QB, n_cmp), NEG_INF).reshape(NR, n_cmp)
    p_c = _masked_softmax(s_c)
    o_c = _dot(p_c.astype(BF16), vc_sc[...])

    p_sum = jnp.sum(p_c.reshape(N_KV_HEADS, GQ, QB, n_cmp), axis=1).reshape(N_KV_HEADS * QB, n_cmp)
    ratio = SEL_BLOCK // CMP_BLOCK
    pair = (lax.broadcasted_iota(I32, (n_cmp, LANES), 0) // ratio
            == lax.broadcasted_iota(I32, (n_cmp, LANES), 1)).astype(F32)
    imp = _dot(p_sum, pair, hi)
    imp_t = jnp.transpose(imp)
    n_sel = t_len // SEL_BLOCK
    blk = lax.broadcasted_iota(I32, (n_sel, LANES), 0)
    score = jnp.where(blk == i, jnp.inf, jnp.where(blk < i, imp_t[0:n_sel, :], NEG_INF))

    def pick(_, carry):
        work, chosen = carry
        best = jnp.max(work, 0, keepdims=True)
        first = jnp.min(jnp.where(work == best, blk, n_sel), 0, keepdims=True)
        hit = blk == first
        return jnp.where(hit, NEG_INF, work), jnp.where(hit, 1.0, chosen)

    _, chosen = lax.fori_loop(0, SEL_TOPK, pick, (score, jnp.zeros((n_sel, LANES), F32)), unroll=True)
    chosen = jnp.where(score > NEG_INF, chosen, 0.0)
    if n_sel < LANES:
        chosen = jnp.concatenate([chosen, jnp.zeros((LANES - n_sel, LANES), F32)], axis=0)
    sel = jnp.transpose(chosen).astype(BF16)

    def sel_tile(r):
        start = pl.multiple_of(QB * (i + 1) + SEL_FRONT - SEL_TILE * (r + 1), SEL_BLOCK)
        k_t = ks_sc[pl.ds(start, SEL_TILE), :]
        v_t = vs_sc[pl.ds(start, SEL_TILE), :]
        first_blk = i + 1 - (SEL_TILE // SEL_BLOCK) * (r + 1)
        expand = (lax.broadcasted_iota(I32, (LANES, SEL_TILE), 0)
                  == first_blk + lax.broadcasted_iota(I32, (LANES, SEL_TILE), 1) // SEL_BLOCK)
        picked = _dot(sel, jnp.where(expand, 1.0, 0.0).astype(BF16)) > 0.5
        return _dot_nt(qbd, k_t), v_t, picked

    def masked(s, picked):
        s4 = s.reshape(N_KV_HEADS, GQ, QB, SEL_TILE)
        p4 = picked.reshape(N_KV_HEADS, 1, QB, SEL_TILE)
        return jnp.where(p4, s4, NEG_INF).reshape(NR, SEL_TILE)

    s0, v0, picked0 = sel_tile(0)
    s0 = masked(s0 + bs_sc[...], picked0)
    m0 = jnp.max(s0, -1, keepdims=True)
    e0 = jnp.exp(s0 - m0)
    m_sc[...] = m0
    l_sc[...] = jnp.sum(e0, -1, keepdims=True)
    acc_sc[...] = _dot(e0.astype(BF16), v0)

    def flash(r, carry):
        s_r, v_r, picked_r = sel_tile(r)
        s_r = masked(s_r + far_sc[...], picked_r)
        m_old = m_sc[...]
        m_new = jnp.maximum(m_old, jnp.max(s_r, -1, keepdims=True))
        scale = jnp.exp(m_old - m_new)
        e_r = jnp.exp(s_r - m_new)
        m_sc[...] = m_new
        l_sc[...] = scale * l_sc[...] + jnp.sum(e_r, -1, keepdims=True)
        acc_sc[...] = scale * acc_sc[...] + _dot(e_r.astype(BF16), v_r)
        return carry

    n_tiles = (i + SEL_TILE // SEL_BLOCK) // (SEL_TILE // SEL_BLOCK)
    lax.fori_loop(1, n_tiles, flash, 0)
    o_s = acc_sc[...] / jnp.maximum(l_sc[...], 1e-30)

    w0 = pl.multiple_of(i * QB, QB)
    s_w = _dot_nt(qbd, wk_sc[pl.ds(w0, WIN_BAND), :]) + bw_sc[...]
    in_seq = lax.broadcasted_iota(I32, (NR, WIN_BAND), 1) >= WINDOW - i * QB
    p_w = _masked_softmax(jnp.where(in_seq, s_w, NEG_INF))
    o_w = _dot(p_w.astype(BF16), wv_sc[pl.ds(w0, WIN_BAND), :])

    gates = jax.nn.sigmoid(zg_ref[0])
    for h in range(N_NSA_HEADS):
        out_ref[0, :, h * HEAD_DIM:(h + 1) * HEAD_DIM] = _gated_mix(gates, o_c, o_s, o_w, h, QB)


def _nsa_prompt(nq, kv, win, zg, pool_w2, rel_bias):
    bsz, t_len, _ = nq.shape
    assert t_len // CMP_BLOCK == LANES and t_len % SEL_TILE == 0
    n_cmp = t_len // CMP_BLOCK
    nr = N_NSA_HEADS * NSA_QBLOCK
    per_b = lambda w: pl.BlockSpec((1, t_len, w), lambda b, i: (b, 0, 0))
    blk = lambda w: pl.BlockSpec((1, NSA_QBLOCK, w), lambda b, i: (b, i, 0))
    return pl.pallas_call(
        functools.partial(_nsa_prompt_body, t_len=t_len),
        grid=(bsz, t_len // NSA_QBLOCK),
        in_specs=[pl.BlockSpec(memory_space=pltpu.SMEM),
                  blk(NSA_W), per_b(4 * KV_W), per_b(2 * KV_W), blk(LANES),
                  pl.BlockSpec((CMP_BLOCK, 2 * LANES), lambda b, i: (0, 0))],
        out_specs=blk(NSA_W),
        out_shape=jax.ShapeDtypeStruct((bsz, t_len, NSA_W), F32),
        scratch_shapes=[pltpu.VMEM((SEL_FRONT + t_len, LANES), BF16),
                        pltpu.VMEM((SEL_FRONT + t_len, LANES), BF16),
                        pltpu.VMEM((WINDOW + t_len, LANES), BF16),
                        pltpu.VMEM((WINDOW + t_len, LANES), BF16),
                        pltpu.VMEM((n_cmp, LANES), BF16),
                        pltpu.VMEM((n_cmp, LANES), BF16),
                        pltpu.VMEM((nr, SEL_TILE), F32),
                        pltpu.VMEM((nr, WIN_BAND), F32),
                        pltpu.VMEM((nr, n_cmp), F32),
                        pltpu.VMEM((nr, 1), F32),
                        pltpu.VMEM((nr, 1), F32),
                        pltpu.VMEM((nr, 1), F32),
                        pltpu.VMEM((nr, LANES), F32)],
        compiler_params=pltpu.CompilerParams(dimension_semantics=("arbitrary", "arbitrary"),
                                             vmem_limit_bytes=VMEM_LIMIT),
        name="nsa_prompt",
    )(rel_bias, nq, kv, win, zg, pool_w2)


CMP_PAGES = 16
N_PICK = SEL_TOPK - 1
N_OWNERS = N_KV_HEADS * N_NEW
N_FETCH = N_OWNERS * N_PICK
OWNER_KEYS = N_PICK * SEL_BLOCK


def _sample_row_ids():
    r = lax.broadcasted_iota(I32, (N_QROWS, 1), 0)
    return r % SEQ_PAD - (SEQ_PAD - N_NEW), r // SEQ_PAD


def _sample_bias(dist, rb_ref):
    _, head = _sample_row_ids()
    col = lambda k: sum(jnp.where(head == h, rb_ref[k, h], 0.0) for h in range(N_NSA_HEADS))
    acc = jnp.broadcast_to(col(0), dist.shape)
    for k in range(1, N_BUCKETS):
        acc = jnp.where(dist >= BUCKET_THR[k], col(k), acc)
    return acc


def _softmax_two(s1, s2):
    mx = jnp.maximum(jnp.max(s1, -1, keepdims=True), jnp.max(s2, -1, keepdims=True))
    mx = jnp.where(mx > NEG_INF, mx, 0.0)
    e1 = jnp.exp(s1 - mx)
    e2 = jnp.exp(s2 - mx)
    den = jnp.maximum(jnp.sum(e1, -1, keepdims=True) + jnp.sum(e2, -1, keepdims=True), 1e-30)
    return e1 / den, e2 / den


def _nsa_cmp_body(pt_ref, rb_ref, nq_ref, pw_ref, cache_ref, oc_ref, idx_ref,
                  buf, sem, kc_sc, vc_sc, *, n_pages, past_len):
    b = pl.program_id(0)
    nb = pl.num_programs(0)
    n_chunks = n_pages // CMP_PAGES
    blocks_per = CMP_PAGES * PAGE_SIZE // CMP_BLOCK
    n_cmp = past_len // CMP_BLOCK
    n_sel_past = past_len // SEL_BLOCK
    n_prob = N_KV_HEADS * SEQ_PAD
    hi = lax.Precision.HIGHEST

    def page_copy(bb, chunk, p, slot):
        page = pt_ref[bb * n_pages + chunk * CMP_PAGES + p]
        return pltpu.make_async_copy(cache_ref.at[page, :, pl.ds(0, 2 * LANES)], buf.at[slot, p], sem.at[slot])

    def start_chunk(bb, chunk, slot):
        for p in range(CMP_PAGES):
            page_copy(bb, chunk, p, slot).start()

    def wait_chunk(bb, chunk, slot):
        for p in range(CMP_PAGES):
            page_copy(bb, chunk, p, slot).wait()

    @pl.when(b == 0)
    def _prologue():
        start_chunk(0, 0, 0)

    def chunk_step(c, carry):
        slot = (b * n_chunks + c) % 2

        @pl.when(c + 1 < n_chunks)
        def _next_same():
            start_chunk(b, c + 1, 1 - slot)

        @pl.when((c + 1 == n_chunks) & (b + 1 < nb))
        def _next_seq():
            start_chunk(b + 1, 0, 1 - slot)

        wait_chunk(b, c, slot)
        rows = buf[slot].reshape(blocks_per, CMP_BLOCK, 2 * LANES)
        pooled = jnp.sum(rows * pw_ref[...][None], axis=1)
        c0 = pl.multiple_of(c * blocks_per, blocks_per)
        kc_sc[pl.ds(c0, blocks_per), :] = pooled[:, 0:LANES].astype(BF16)
        vc_sc[pl.ds(c0, blocks_per), :] = pooled[:, LANES:2 * LANES].astype(BF16)
        return carry

    lax.fori_loop(0, n_chunks, chunk_step, 0)

    qbd = _block_diag_queries(nq_ref[0], SEQ_PAD)
    tok, _ = _sample_row_ids()
    c_i = lax.broadcasted_iota(I32, (N_QROWS, n_cmp), 1)
    dist = past_len + tok - (c_i * CMP_BLOCK + CMP_BLOCK - 1)
    s_c = _dot_nt(qbd, kc_sc[...]) + _sample_bias(dist, rb_ref)
    p_c = _masked_softmax(jnp.where(dist >= 0, s_c, NEG_INF))
    oc_ref[0] = _dot(p_c.astype(BF16), vc_sc[...])

    p_sum = jnp.sum(p_c.reshape(N_KV_HEADS, GQ, SEQ_PAD, n_cmp), axis=1).reshape(n_prob, n_cmp)
    ratio = SEL_BLOCK // CMP_BLOCK
    pair = (lax.broadcasted_iota(I32, (n_cmp, n_sel_past), 0) // ratio
            == lax.broadcasted_iota(I32, (n_cmp, n_sel_past), 1)).astype(F32)
    imp = _dot(p_sum, pair, hi)
    blk = lax.broadcasted_iota(I32, imp.shape, 1)
    slot_i = lax.broadcasted_iota(I32, (n_prob, LANES), 1)

    def pick(k, carry):
        work, ids = carry
        best = jnp.max(work, 1, keepdims=True)
        first = jnp.min(jnp.where(work == best, blk, n_sel_past), 1, keepdims=True)
        return jnp.where(blk == first, NEG_INF, work), jnp.where(slot_i == k, first, ids)

    _, ids = lax.fori_loop(0, N_PICK, pick, (imp, jnp.zeros((n_prob, LANES), I32)), unroll=True)
    idx_ref[0] = ids


def _nsa_cmp(page_table, rel_bias, nq_s, pool_w2, cache3):
    dec_b, n_pages = page_table.shape
    past_len = n_pages * PAGE_SIZE
    assert n_pages % CMP_PAGES == 0 and past_len // SEL_BLOCK >= N_PICK
    n_cmp = past_len // CMP_BLOCK
    n_prob = N_KV_HEADS * SEQ_PAD
    grid_spec = pltpu.PrefetchScalarGridSpec(
        num_scalar_prefetch=1,
        grid=(dec_b,),
        in_specs=[pl.BlockSpec(memory_space=pltpu.SMEM),
                  pl.BlockSpec((1, SEQ_PAD, NSA_W), lambda b, pt: (b, 0, 0)),
                  pl.BlockSpec((CMP_BLOCK, 2 * LANES), lambda b, pt: (0, 0)),
                  pl.BlockSpec(memory_space=pl.ANY)],
        out_specs=[pl.BlockSpec((1, N_QROWS, LANES), lambda b, pt: (b, 0, 0)),
                   pl.BlockSpec((1, n_prob, LANES), lambda b, pt: (b, 0, 0))],
        scratch_shapes=[pltpu.VMEM((2, CMP_PAGES, PAGE_SIZE, 2 * LANES), F32),
                        pltpu.SemaphoreType.DMA((2,)),
                        pltpu.VMEM((n_cmp, LANES), BF16),
                        pltpu.VMEM((n_cmp, LANES), BF16)],
    )
    return pl.pallas_call(
        functools.partial(_nsa_cmp_body, n_pages=n_pages, past_len=past_len),
        grid_spec=grid_spec,
        out_shape=[jax.ShapeDtypeStruct((dec_b, N_QROWS, LANES), F32),
                   jax.ShapeDtypeStruct((dec_b, n_prob, LANES), I32)],
        compiler_params=pltpu.CompilerParams(dimension_semantics=("arbitrary",),
                                             vmem_limit_bytes=VMEM_LIMIT),
        name="nsa_sample_cmp",
    )(page_table.reshape(-1), rel_bias, nq_s, pool_w2, cache3)


def _nsa_sel_body(pt_ref, ids_ref, rb_ref, nq_ref, kvn_ref, winc_ref, winn_ref, zg_ref, oc_ref, idv_ref,
                  cache_ref, out_ref, buf, sem, *, n_pages, past_len):
    b = pl.program_id(0)
    nb = pl.num_programs(0)
    hi = lax.Precision.HIGHEST

    def block_copy(bb, n, slot):
        blk = ids_ref[bb * N_FETCH + n]
        page = pt_ref[bb * n_pages + blk // 2]
        src = cache_ref.at[page, pl.ds(pl.multiple_of((blk % 2) * SEL_BLOCK, SEL_BLOCK), SEL_BLOCK),
                           pl.ds(2 * LANES, 2 * LANES)]
        dst = buf.at[slot, pl.ds(pl.multiple_of(n * SEL_BLOCK, SEL_BLOCK), SEL_BLOCK), :]
        return pltpu.make_async_copy(src, dst, sem.at[slot])

    def start_all(bb, slot):
        def go(n, carry):
            block_copy(bb, n, slot).start()
            return carry
        lax.fori_loop(0, N_FETCH, go, 0)

    def wait_all(bb, slot):
        def go(n, carry):
            block_copy(bb, n, slot).wait()
            return carry
        lax.fori_loop(0, N_FETCH, go, 0)

    slot = b % 2

    @pl.when(b == 0)
    def _prologue():
        start_all(0, 0)

    @pl.when(b + 1 < nb)
    def _prefetch():
        start_all(b + 1, 1 - slot)

    qbd = _block_diag_queries(nq_ref[0], SEQ_PAD)
    tok, head = _sample_row_ids()
    owner = (head // GQ) * N_NEW + tok

    wc = winc_ref[0]
    wn = winn_ref[0]
    r_i = lax.broadcasted_iota(I32, (N_QROWS, WINDOW), 1)
    dist1 = WINDOW + tok - r_i
    s1 = _dot_nt(qbd, wc[:, 0:LANES].astype(BF16)) + _sample_bias(dist1, rb_ref)
    s1 = jnp.where((dist1 >= 0) & (dist1 <= WINDOW), s1, NEG_INF)
    n_i = lax.broadcasted_iota(I32, (N_QROWS, SEQ_PAD), 1) - (SEQ_PAD - N_NEW)
    dist2 = tok - n_i
    new_ok = (n_i >= 0) & (dist2 >= 0)
    bias2 = _sample_bias(dist2, rb_ref)
    s2 = jnp.where(new_ok, _dot_nt(qbd, wn[:, 0:LANES].astype(BF16)) + bias2, NEG_INF)
    p1, p2 = _softmax_two(s1, s2)
    o_w = (_dot(p1.astype(BF16), wc[:, LANES:2 * LANES].astype(BF16))
           + _dot(p2.astype(BF16), wn[:, LANES:2 * LANES].astype(BF16)))

    wait_all(b, slot)
    s_f = jnp.full((N_QROWS, OWNER_KEYS), NEG_INF, F32)
    for o in range(N_OWNERS):
        k_o = buf[slot, o * OWNER_KEYS:(o + 1) * OWNER_KEYS, 0:LANES].astype(BF16)
        s_f = jnp.where(owner == o, _dot_nt(qbd, k_o), s_f)
    spread = (lax.broadcasted_iota(I32, (LANES, OWNER_KEYS), 0)
              == lax.broadcasted_iota(I32, (LANES, OWNER_KEYS), 1) // SEL_BLOCK).astype(F32)
    blk_of_key = _dot(idv_ref[0].astype(F32), spread, hi)
    blk_rows = jnp.concatenate([blk_of_key[(h // GQ) * SEQ_PAD:(h // GQ + 1) * SEQ_PAD]
                                for h in range(N_NSA_HEADS)], axis=0)
    key_pos = blk_rows.astype(I32) * SEL_BLOCK + lax.broadcasted_iota(I32, (N_QROWS, OWNER_KEYS), 1) % SEL_BLOCK
    dist_s = past_len + tok - key_pos
    s_f = jnp.where((tok >= 0) & (dist_s >= 0), s_f + _sample_bias(dist_s, rb_ref), NEG_INF)
    kvn = kvn_ref[0]
    s_n = jnp.where(new_ok, _dot_nt(qbd, kvn[:, 2 * LANES:3 * LANES].astype(BF16)) + bias2, NEG_INF)
    p_f, p_n = _softmax_two(s_f, s_n)
    o_s = _dot(p_n.astype(BF16), kvn[:, 3 * LANES:4 * LANES].astype(BF16))
    for o in range(N_OWNERS):
        v_o = buf[slot, o * OWNER_KEYS:(o + 1) * OWNER_KEYS, LANES:2 * LANES].astype(BF16)
        o_s = o_s + _dot(jnp.where(owner == o, p_f, 0.0).astype(BF16), v_o)

    o_c = oc_ref[0]
    gates = jax.nn.sigmoid(zg_ref[0])
    is_new = lax.broadcasted_iota(I32, (SEQ_PAD, HEAD_DIM), 0) >= SEQ_PAD - N_NEW
    for h in range(N_NSA_HEADS):
        mix = _gated_mix(gates, o_c, o_s, o_w, h, SEQ_PAD)
        out_ref[0, :, h * HEAD_DIM:(h + 1) * HEAD_DIM] = jnp.where(is_new, mix, 0.0)


def _nsa_sel(page_table, ids, ids_rows, rel_bias, nq_s, kv_s, win_cache, win_s, zg_s, o_c, cache3):
    dec_b, n_pages = page_table.shape
    past_len = n_pages * PAGE_SIZE
    seq = lambda w: pl.BlockSpec((1, SEQ_PAD, w), lambda b, pt, ix: (b, 0, 0))
    grid_spec = pltpu.PrefetchScalarGridSpec(
        num_scalar_prefetch=2,
        grid=(dec_b,),
        in_specs=[pl.BlockSpec(memory_space=pltpu.SMEM),
                  seq(NSA_W), seq(4 * KV_W),
                  pl.BlockSpec((1, WINDOW, 2 * KV_W), lambda b, pt, ix: (b, 0, 0)),
                  seq(2 * KV_W), seq(LANES),
                  pl.BlockSpec((1, N_QROWS, LANES), lambda b, pt, ix: (b, 0, 0)),
                  pl.BlockSpec((1, N_KV_HEADS * SEQ_PAD, LANES), lambda b, pt, ix: (b, 0, 0)),
                  pl.BlockSpec(memory_space=pl.ANY)],
        out_specs=seq(NSA_W),
        scratch_shapes=[pltpu.VMEM((2, N_FETCH * SEL_BLOCK, 2 * LANES), F32),
                        pltpu.SemaphoreType.DMA((2,))],
    )
    return pl.pallas_call(
        functools.partial(_nsa_sel_body, n_pages=n_pages, past_len=past_len),
        grid_spec=grid_spec,
        out_shape=jax.ShapeDtypeStruct((dec_b, SEQ_PAD, NSA_W), F32),
        compiler_params=pltpu.CompilerParams(dimension_semantics=("arbitrary",),
                                             vmem_limit_bytes=VMEM_LIMIT),
        name="nsa_sample_sel",
    )(page_table.reshape(-1), ids.reshape(-1), rel_bias, nq_s, kv_s, win_cache, win_s, zg_s, o_c, ids_rows, cache3)


FF_CHUNK = 896


def _ffn_body(*refs, tm, has_override):
    if has_override:
        (x_ref, mm_ref, mn_ref, ov_ref, woa_ref, wob_ref, l1w_ref, l1b_ref, wup_ref, bup_ref, cw_ref, cb_ref,
         wdn_ref, bdn_ref, l2w_ref, l2b_ref, y_ref, tail_ref, carry_sc, ubuf) = refs
    else:
        (x_ref, mm_ref, mn_ref, woa_ref, wob_ref, l1w_ref, l1b_ref, wup_ref, bup_ref, cw_ref, cb_ref,
         wdn_ref, bdn_ref, l2w_ref, l2b_ref, y_ref, tail_ref, carry_sc, ubuf) = refs
    j = pl.program_id(1)

    @pl.when(j == 0)
    def _init():
        carry_sc[...] = jnp.zeros(carry_sc.shape, F32)

    x = x_ref[0]
    h = ALPHA * x + (_dot(mm_ref[0].astype(BF16), woa_ref[...]) + _dot(mn_ref[0].astype(BF16), wob_ref[...]))
    x1 = _layer_norm(h, l1w_ref[...], l1b_ref[...])
    x1b = x1.astype(BF16)
    if has_override:
        state_row = lax.broadcasted_iota(I32, (tm, FF_CHUNK), 0) % SEQ_PAD < SEQ_PAD - N_NEW

    def conv_half(c0):
        cols = slice(c0, c0 + FF_CHUNK)
        u = _dot(x1b, wup_ref[:, cols]) + bup_ref[:, cols]
        if has_override:
            u = jnp.where(state_row, ov_ref[0, :, cols], u)
        ubuf[0:8, :] = carry_sc[:, cols]
        ubuf[8:8 + tm, :] = u
        out = cb_ref[:, cols] + ubuf[6:6 + tm, :] * cw_ref[0:1, cols]
        out = out + ubuf[7:7 + tm, :] * cw_ref[1:2, cols]
        out = out + u * cw_ref[2:3, cols]
        carry_sc[:, cols] = u[tm - 8:tm, :]
        tail_ref[0, 0, :, cols] = u if has_override else u[tm - 8:tm, :]
        return out

    ff = jnp.zeros((tm, D_MODEL), F32)
    for c in range(D_FF // FF_CHUNK):
        ga = conv_half(c * FF_CHUNK)
        gb = conv_half(D_FF + c * FF_CHUNK)
        gelu = ga * (0.5 * (1.0 + jnp.tanh(math.sqrt(2.0 / math.pi) * (ga + 0.044715 * (ga * ga * ga)))))
        ff = ff + _dot((gelu * gb).astype(BF16), wdn_ref[c * FF_CHUNK:(c + 1) * FF_CHUNK, :])
    ff = ff + bdn_ref[...]
    y_ref[0] = _layer_norm(ALPHA * x1 + ff, l2w_ref[...], l2b_ref[...])


def _ffn(x, mix_m, mix_n, override, wts):
    bsz, t_len, _ = x.shape
    tm = min(t_len, 256)
    nt = t_len // tm
    has_override = override is not None
    tail_rows = tm if has_override else 8
    row = lambda w: pl.BlockSpec((1, tm, w), lambda b, j: (b, j, 0))
    const = lambda shape: pl.BlockSpec(shape, lambda b, j: (0,) * len(shape), pipeline_mode=pl.Buffered(1))
    in_specs = [row(D_MODEL), row(MLSTM_W), row(NSA_W)]
    args = [x, mix_m, mix_n]
    if has_override:
        in_specs.append(row(2 * D_FF))
        args.append(override)
    in_specs += [const(w.shape) for w in wts]
    args += list(wts)
    y, tail = pl.pallas_call(
        functools.partial(_ffn_body, tm=tm, has_override=has_override),
        grid=(bsz, nt),
        in_specs=in_specs,
        out_specs=[row(D_MODEL), pl.BlockSpec((1, 1, tail_rows, 2 * D_FF), lambda b, j: (b, j, 0, 0))],
        out_shape=[jax.ShapeDtypeStruct((bsz, t_len, D_MODEL), F32),
                   jax.ShapeDtypeStruct((bsz, nt, tail_rows, 2 * D_FF), F32)],
        scratch_shapes=[pltpu.VMEM((8, 2 * D_FF), F32), pltpu.VMEM((8 + tm, FF_CHUNK), F32)],
        compiler_params=pltpu.CompilerParams(dimension_semantics=("arbitrary", "arbitrary"),
                                             vmem_limit_bytes=VMEM_LIMIT),
        name="ffn",
    )(*args)
    return y, tail


def _permute_in_proj(w_in, b_in):
    gates_a = slice(2 * MLSTM_W + 2 * MLSTM_W, 2 * MLSTM_W + 2 * MLSTM_W + 2 * N_MLSTM_HEADS)
    nsa0 = gates_a.stop
    nsa1 = nsa0 + NSA_W + 6 * KV_W

    def perm(a):
        pad = IN_PAD - a.shape[-1]
        parts = [a[..., :gates_a.start], a[..., nsa0:nsa1], a[..., gates_a], a[..., nsa1:]]
        parts.append(jnp.zeros(a.shape[:-1] + (pad,), a.dtype))
        return jnp.concatenate(parts, axis=-1)

    return perm(w_in).astype(BF16), perm(b_in).reshape(1, IN_PAD)


def _front_pad(a, n):
    return jnp.pad(a, ((0, 0), (n, 0), (0, 0)))


def kernel(x_prompt, x_sample, cache_nsa_kv, cache_win_kv, state_mlstm_c, state_mlstm_n, state_mlstm_m,
           state_mlstm_conv, state_ffn_conv, page_table,
           w_in, b_in, mlstm_conv_w, mlstm_conv_b, mlstm_norm_w, nsa_pool_w, rel_bias, w_out,
           ln1_w, ln1_b, w_up, b_up, ffn_conv_w, ffn_conv_b, w_down, b_down, ln2_w, ln2_b):
    bsz, t_len, _ = x_prompt.shape
    dec_b, n_new, _ = x_sample.shape
    assert n_new == N_NEW and cache_win_kv.shape[1] == WINDOW

    w_p, b_p = _permute_in_proj(w_in, b_in)
    pool_w2 = jnp.concatenate([jnp.repeat(nsa_pool_w[0], HEAD_DIM, axis=-1),
                               jnp.repeat(nsa_pool_w[1], HEAD_DIM, axis=-1)], axis=-1)
    row = lambda v: v.reshape(1, -1)
    ffn_w = (w_out[:MLSTM_W].astype(BF16), w_out[MLSTM_W:].astype(BF16), row(ln1_w), row(ln1_b),
             w_up.astype(BF16), row(b_up), ffn_conv_w, row(ffn_conv_b), w_down.astype(BF16), row(b_down),
             row(ln2_w), row(ln2_b))

    zqk, zv, zo, nq, kv, win, zg = _proj_in(x_prompt.reshape(bsz * t_len, D_MODEL), w_p, b_p)
    seq3 = lambda a: a.reshape(bsz, t_len, a.shape[-1])
    zqk, zv, zo, nq, kv, win, zg = map(seq3, (zqk, zv, zo, nq, kv, win, zg))
    mix_m, c_p, n_p, m_p = _mlstm(zqk, zv, zo, zg, None, mlstm_conv_w, mlstm_conv_b, mlstm_norm_w, 0)
    mix_n = _nsa_prompt(nq, kv, win, zg, pool_w2, rel_bias)
    y_p, tail_p = _ffn(x_prompt, mix_m, mix_n, None, ffn_w)
    kv_p = kv.reshape(bsz, t_len, 4, N_KV_HEADS, HEAD_DIM)
    win_p = win[:, t_len - min(WINDOW, t_len):].reshape(bsz, -1, 2, N_KV_HEADS, HEAD_DIM)
    mconv_p = zqk[:, t_len - (MLSTM_CONV - 1):]
    fconv_p = tail_p[:, -1, 8 - (FFN_CONV - 1):]

    pad = SEQ_PAD - N_NEW
    xs = _front_pad(x_sample, pad)
    szqk, szv, szo, snq, skv, swin, szg = _proj_in(xs.reshape(dec_b * SEQ_PAD, D_MODEL), w_p, b_p)
    sseq = lambda a: a.reshape(dec_b, SEQ_PAD, a.shape[-1])
    szqk, szv, szo, snq, skv, swin, szg = map(sseq, (szqk, szv, szo, snq, skv, swin, szg))
    lead = MLSTM_CHUNK - N_NEW
    new = lambda a: a[:, pad:]
    m_qk = jnp.concatenate([jnp.zeros((dec_b, lead - (MLSTM_CONV - 1), 2 * MLSTM_W), F32),
                            state_mlstm_conv.astype(F32), new(szqk)], axis=1)
    mix_ms, c_s, n_s, m_s = _mlstm(m_qk, _front_pad(new(szv), lead), _front_pad(new(szo), lead),
                                   _front_pad(new(szg), lead),
                                   (state_mlstm_c, state_mlstm_n, state_mlstm_m),
                                   mlstm_conv_w, mlstm_conv_b, mlstm_norm_w, lead)
    mix_ms = mix_ms[:, MLSTM_CHUNK - SEQ_PAD:]
    cache3 = cache_nsa_kv.reshape(cache_nsa_kv.shape[0], PAGE_SIZE, 4 * KV_W)
    o_c, ids_rows = _nsa_cmp(page_table, rel_bias, snq, pool_w2, cache3)
    ids = ids_rows.reshape(dec_b, N_KV_HEADS, SEQ_PAD, LANES)[:, :, pad:, :N_PICK]
    win_cache = cache_win_kv.reshape(dec_b, WINDOW, 2 * KV_W)
    mix_ns = _nsa_sel(page_table, ids, ids_rows, rel_bias, snq, skv, win_cache, swin, szg, o_c, cache3)
    override = jnp.concatenate([jnp.zeros((dec_b, pad - (FFN_CONV - 1), 2 * D_FF), F32),
                                state_ffn_conv.astype(F32),
                                jnp.zeros((dec_b, N_NEW, 2 * D_FF), F32)], axis=1)
    flat = lambda a: a.reshape(1, dec_b * SEQ_PAD, a.shape[-1])
    y_s8, u_s = _ffn(flat(xs), flat(mix_ms), flat(mix_ns), flat(override), ffn_w)
    y_s = y_s8.reshape(dec_b, SEQ_PAD, D_MODEL)[:, pad:]
    kv_s = new(skv).reshape(dec_b, N_NEW, 4, N_KV_HEADS, HEAD_DIM)
    win_s = jnp.concatenate([win_cache[:, N_NEW:], new(swin)], axis=1).reshape(dec_b, WINDOW, 2, N_KV_HEADS, HEAD_DIM)
    mconv_s = szqk[:, SEQ_PAD - (MLSTM_CONV - 1):]
    fconv_s = u_s.reshape(dec_b, SEQ_PAD, 2 * D_FF)[:, SEQ_PAD - (FFN_CONV - 1):]
    return (y_p, y_s, kv_p, kv_s, win_p, win_s, c_p, c_s, n_p, n_s, m_p, m_s, mconv_p, mconv_s, fconv_p, fconv_s)
```

```python
import functools
import math

import jax
import jax.numpy as jnp
from jax import lax
from jax.experimental import pallas as pl
from jax.experimental.pallas import tpu as pltpu

F32 = jnp.float32
BF16 = jnp.bfloat16
I32 = jnp.int32

D_MODEL = 1024
HEAD_DIM = 64
N_MLSTM_HEADS = 8
N_NSA_HEADS = 8
N_KV_HEADS = 2
GQ = N_NSA_HEADS // N_KV_HEADS
MLSTM_W = N_MLSTM_HEADS * HEAD_DIM
NSA_W = N_NSA_HEADS * HEAD_DIM
KV_W = N_KV_HEADS * HEAD_DIM
MLSTM_CHUNK = 64
MLSTM_CONV = 4
CMP_BLOCK = 32
SEL_BLOCK = 64
SEL_TOPK = 16
WINDOW = 512
NSA_QBLOCK = 64
N_BUCKETS = 32
MAX_DISTANCE = 128
D_FF = 2688
FFN_CONV = 3
DEPTH = 1
ALPHA = (2.0 * DEPTH) ** 0.25
LN_EPS = 1e-5
PAGE_SIZE = 128

LANES = 128
VMEM_LIMIT = 56 * 1024 * 1024
NEG_INF = float("-inf")

SEQ_PAD = 8
N_NEW = 4
N_QROWS = N_NSA_HEADS * SEQ_PAD


def _bucket_thresholds():
    max_exact = N_BUCKETS // 2
    thr = list(range(max_exact + 1))
    for k in range(1, N_BUCKETS - max_exact):
        thr.append(int(math.ceil(max_exact * (MAX_DISTANCE / max_exact) ** (k / (N_BUCKETS - max_exact)))))
    return tuple(thr)


BUCKET_THR = _bucket_thresholds()
FAR_DIST = BUCKET_THR[N_BUCKETS - 1]


def _dot(a, b, precision=None):
    return jnp.dot(a, b, preferred_element_type=F32, precision=precision)


def _dot_nt(a, b, precision=None):
    return lax.dot_general(a, b, (((1,), (1,)), ((), ())), preferred_element_type=F32, precision=precision)


def _dot_tn(a, b):
    return lax.dot_general(a, b, (((0,), (0,)), ((), ())), preferred_element_type=F32)


def _layer_norm(x, w, b):
    mu = jnp.mean(x, -1, keepdims=True)
    xc = x - mu
    var = jnp.mean(xc * xc, -1, keepdims=True)
    return xc * lax.rsqrt(var + LN_EPS) * w + b


def _log_sigmoid(x):
    return jnp.minimum(x, 0.0) - jnp.log1p(jnp.exp(-jnp.abs(x)))


def _bias_from_dist(dist, rb_ref, head):
    acc = jnp.full(dist.shape, rb_ref[0, head], F32)
    for k in range(1, N_BUCKETS):
        acc = jnp.where(dist >= BUCKET_THR[k], rb_ref[k, head], acc)
    return acc


def _masked_softmax(s):
    mx = jnp.max(s, -1, keepdims=True)
    mx = jnp.where(mx > NEG_INF, mx, 0.0)
    e = jnp.exp(s - mx)
    den = jnp.sum(e, -1, keepdims=True)
    return e / jnp.maximum(den, 1e-30)


_C_QK, _C_V, _C_O, _C_NQ, _C_KV, _C_WIN, _C_G = 0, 1024, 1536, 2048, 2560, 3072, 3328
IN_PAD = 3456
G_IGATE, G_FGATE, G_NSA = 0, N_MLSTM_HEADS, 2 * N_MLSTM_HEADS


def _proj_in_body(x_ref, w_ref, b_ref, qk_ref, v_ref, o_ref, nq_ref, kv_ref, win_ref, g_ref):
    z = _dot(x_ref[...].astype(BF16), w_ref[...]) + b_ref[...]
    qk_ref[...] = z[:, _C_QK:_C_V]
    v_ref[...] = z[:, _C_V:_C_O]
    o_ref[...] = z[:, _C_O:_C_NQ]
    nq_ref[...] = z[:, _C_NQ:_C_KV]
    kv_ref[...] = z[:, _C_KV:_C_WIN]
    win_ref[...] = z[:, _C_WIN:_C_G]
    g_ref[...] = z[:, _C_G:IN_PAD]


def _proj_in(x2d, w_p, b_p):
    rows = x2d.shape[0]
    tm = min(rows, 256)
    widths = (1024, 512, 512, 512, 512, 256, 128)
    const = lambda i: (0, 0)
    return pl.pallas_call(
        _proj_in_body,
        grid=(rows // tm,),
        in_specs=[pl.BlockSpec((tm, D_MODEL), lambda i: (i, 0)),
                  pl.BlockSpec((D_MODEL, IN_PAD), const, pipeline_mode=pl.Buffered(1)),
                  pl.BlockSpec((1, IN_PAD), const, pipeline_mode=pl.Buffered(1))],
        out_specs=[pl.BlockSpec((tm, w), lambda i: (i, 0)) for w in widths],
        out_shape=[jax.ShapeDtypeStruct((rows, w), F32) for w in widths],
        compiler_params=pltpu.CompilerParams(dimension_semantics=("arbitrary",),
                                             vmem_limit_bytes=VMEM_LIMIT),
        name="proj_in",
    )(x2d, w_p, b_p)


def _mlstm_body(*refs, tb, n_dummy, has_state):
    if has_state:
        (zqk_ref, zv_ref, zo_ref, zg_ref, c0_ref, n0_ref, m0_ref, cw_ref, cb_ref, nw_ref,
         mix_ref, c_out, n_out, m_out, ubuf, qk_sc, c_sc, n_sc, m_sc) = refs
    else:
        (zqk_ref, zv_ref, zo_ref, zg_ref, cw_ref, cb_ref, nw_ref,
         mix_ref, c_out, n_out, m_out, ubuf, qk_sc, c_sc, n_sc, m_sc) = refs
    L = MLSTM_CHUNK
    D = HEAD_DIM
    j = pl.program_id(1)

    @pl.when(j == 0)
    def _init():
        ubuf[0:8, :] = jnp.zeros((8, 2 * MLSTM_W), F32)
        if has_state:
            c_sc[...] = c0_ref[0]
            n_sc[...] = n0_ref[0]
            m_sc[...] = m0_ref[0]
        else:
            c_sc[...] = jnp.zeros(c_sc.shape, F32)
            n_sc[...] = jnp.zeros(n_sc.shape, F32)
            m_sc[...] = jnp.zeros(m_sc.shape, F32)

    u = zqk_ref[0]
    ubuf[8:8 + tb, :] = u
    conv = cb_ref[...] + ubuf[5:5 + tb, :] * cw_ref[0:1, :]
    conv = conv + ubuf[6:6 + tb, :] * cw_ref[1:2, :]
    conv = conv + ubuf[7:7 + tb, :] * cw_ref[2:3, :]
    conv = conv + u * cw_ref[3:4, :]
    ubuf[5:8, :] = u[tb - 3:tb, :]
    qk_sc[...] = conv * jax.nn.sigmoid(conv)

    row_i = lax.broadcasted_iota(I32, (L, L), 0)
    col_i = lax.broadcasted_iota(I32, (L, L), 1)
    causal = col_i <= row_i
    tri = causal.astype(F32)
    eye_g = (lax.broadcasted_iota(I32, (2 * N_MLSTM_HEADS, LANES), 0)
             == lax.broadcasted_iota(I32, (2 * N_MLSTM_HEADS, LANES), 1)).astype(F32)
    hi = lax.Precision.HIGHEST

    def chunk(ci, carry):
        r0 = pl.multiple_of(ci * L, L)
        g = zg_ref[0, pl.ds(r0, L), :]
        if n_dummy:
            lane = lax.broadcasted_iota(I32, (L, LANES), 1)
            dummy = (lax.broadcasted_iota(I32, (L, LANES), 0) + ci * L) < n_dummy
            g = jnp.where(dummy, jnp.where(lane < G_FGATE, -1e30, 1e30), g)
        g_t = _dot_nt(eye_g, g, hi)
        lf = _log_sigmoid(g)
        lf_t = _log_sigmoid(g_t[G_FGATE:G_FGATE + N_MLSTM_HEADS, :])
        ig_t = g_t[G_IGATE:G_IGATE + N_MLSTM_HEADS, :]
        b_all = _dot(tri, lf, hi)
        b_t = _dot_nt(lf_t, tri, hi)
        for h in range(N_MLSTM_HEADS):
            hs = slice(h * D, (h + 1) * D)
            q_h = qk_sc[pl.ds(r0, L), hs]
            k_h = qk_sc[pl.ds(r0, L), MLSTM_W + h * D:MLSTM_W + (h + 1) * D] * (D ** -0.5)
            v_h = zv_ref[0, pl.ds(r0, L), hs]
            bc = b_all[:, G_FGATE + h:G_FGATE + h + 1]
            ic = g[:, G_IGATE + h:G_IGATE + h + 1]
            br = b_t[h:h + 1, :]
            ir = ig_t[h:h + 1, :]
            m_h = m_sc[0:1, h:h + 1]
            c_h = c_sc[h]
            n_h = n_sc[h:h + 1, :]
            dmat = jnp.where(causal, bc - br + ir, NEG_INF)
            inter = bc + m_h
            m_t = jnp.maximum(inter, jnp.max(dmat, -1, keepdims=True))
            w_intra = jnp.exp(dmat - m_t)
            w_inter = jnp.exp(inter - m_t)
            qb = q_h.astype(BF16)
            kb = k_h.astype(BF16)
            vb = v_h.astype(BF16)
            a = w_intra * _dot_nt(qb, kb)
            num = _dot(a.astype(BF16), vb) + _dot_nt(qb, c_h.astype(BF16)) * w_inter
            den = jnp.sum(a, -1, keepdims=True) + w_inter * jnp.sum(q_h * n_h, -1, keepdims=True)
            den = jnp.maximum(jnp.abs(den), jnp.exp(-m_t))
            hh = num / den
            b_end = bc[L - 1:L, :]
            m_new = jnp.maximum(b_end + m_h, jnp.max(b_end - br + ir, -1, keepdims=True))
            wk = jnp.exp(b_end - bc + ic - m_new)
            sc = jnp.exp(b_end + m_h - m_new)
            c_sc[h] = sc * c_h + _dot_tn((v_h * wk).astype(BF16), kb)
            n_sc[h:h + 1, :] = sc * n_h + jnp.sum(wk * k_h, 0, keepdims=True)
            m_sc[0:1, h:h + 1] = m_new
            mu = jnp.mean(hh, -1, keepdims=True)
            hc = hh - mu
            var = jnp.mean(hc * hc, -1, keepdims=True)
            hn = hc * lax.rsqrt(var + LN_EPS)
            og = jax.nn.sigmoid(zo_ref[0, pl.ds(r0, L), hs])
            mix_ref[0, pl.ds(r0, L), hs] = og * (hn * nw_ref[:, hs])
        return carry

    lax.fori_loop(0, tb // L, chunk, 0)

    @pl.when(j == pl.num_programs(1) - 1)
    def _final():
        c_out[0] = c_sc[...]
        n_out[0] = n_sc[...]
        m_out[0] = m_sc[...]


def _mlstm(zqk, zv, zo, zg, state, conv_w, conv_b, norm_w, n_dummy):
    bsz, t_len, _ = zqk.shape
    tb = min(t_len, 256)
    has_state = state is not None
    row = lambda w: pl.BlockSpec((1, tb, w), lambda b, j: (b, j, 0))
    per_b = lambda shape: pl.BlockSpec((1,) + shape, lambda b, j: (b,) + (0,) * len(shape))
    const = lambda shape: pl.BlockSpec(shape, lambda b, j: (0,) * len(shape))
    in_specs = [row(2 * MLSTM_W), row(MLSTM_W), row(MLSTM_W), row(LANES)]
    args = [zqk, zv, zo, zg]
    if has_state:
        c0, n0, m0 = state
        in_specs += [per_b((N_MLSTM_HEADS, HEAD_DIM, HEAD_DIM)), per_b((N_MLSTM_HEADS, HEAD_DIM)),
                     per_b((1, N_MLSTM_HEADS))]
        args += [c0, n0, m0.reshape(bsz, 1, N_MLSTM_HEADS)]
    in_specs += [const((MLSTM_CONV, 2 * MLSTM_W)), const((1, 2 * MLSTM_W)), const((1, MLSTM_W))]
    args += [conv_w, conv_b.reshape(1, -1), norm_w.reshape(1, -1)]
    mix, c_new, n_new, m_new = pl.pallas_call(
        functools.partial(_mlstm_body, tb=tb, n_dummy=n_dummy, has_state=has_state),
        grid=(bsz, t_len // tb),
        in_specs=in_specs,
        out_specs=[row(MLSTM_W), per_b((N_MLSTM_HEADS, HEAD_DIM, HEAD_DIM)),
                   per_b((N_MLSTM_HEADS, HEAD_DIM)), per_b((1, N_MLSTM_HEADS))],
        out_shape=[jax.ShapeDtypeStruct((bsz, t_len, MLSTM_W), F32),
                   jax.ShapeDtypeStruct((bsz, N_MLSTM_HEADS, HEAD_DIM, HEAD_DIM), F32),
                   jax.ShapeDtypeStruct((bsz, N_MLSTM_HEADS, HEAD_DIM), F32),
                   jax.ShapeDtypeStruct((bsz, 1, N_MLSTM_HEADS), F32)],
        scratch_shapes=[pltpu.VMEM((8 + tb, 2 * MLSTM_W), F32),
                        pltpu.VMEM((tb, 2 * MLSTM_W), F32),
                        pltpu.VMEM((N_MLSTM_HEADS, HEAD_DIM, HEAD_DIM), F32),
                        pltpu.VMEM((N_MLSTM_HEADS, HEAD_DIM), F32),
                        pltpu.VMEM((1, N_MLSTM_HEADS), F32)],
        compiler_params=pltpu.CompilerParams(dimension_semantics=("arbitrary", "arbitrary"),
                                             vmem_limit_bytes=VMEM_LIMIT),
        name="mlstm",
    )(*args)
    return mix, c_new, n_new, m_new.reshape(bsz, N_MLSTM_HEADS)


def _block_diag_queries(x, n_rows, heads=tuple(range(N_NSA_HEADS))):
    lane = lax.broadcasted_iota(I32, (n_rows, LANES), 1)
    low = lane < HEAD_DIM
    pieces = []
    for h in heads:
        pair = x[:, (h // 2) * LANES:(h // 2 + 1) * LANES]
        src_low = h % 2 == 0
        dst_low = h // GQ == 0
        if src_low != dst_low:
            pair = pltpu.roll(pair, HEAD_DIM, 1)
        pieces.append(jnp.where(low if dst_low else ~low, pair, 0.0))
    return (jnp.concatenate(pieces, axis=0) * (HEAD_DIM ** -0.5)).astype(BF16)


def _group_lanes(o, g):
    return o[:, g * HEAD_DIM:(g + 1) * HEAD_DIM]


def _gated_mix(gates, o_c, o_s, o_w, h, n_rows):
    g = h // GQ
    rs = slice(h * n_rows, (h + 1) * n_rows)
    col = lambda branch: gates[:, G_NSA + branch * N_NSA_HEADS + h:G_NSA + branch * N_NSA_HEADS + h + 1]
    return (col(0) * _group_lanes(o_c[rs], g) + col(1) * _group_lanes(o_s[rs], g)
            + col(2) * _group_lanes(o_w[rs], g))


SEL_TILE = 256
SEL_TILE_BLOCKS = SEL_TILE // SEL_BLOCK
WIN_BAND = WINDOW + NSA_QBLOCK
N_COLS = N_NSA_HEADS * NSA_QBLOCK
PROMPT_HEADS = tuple(GQ * (jj % N_KV_HEADS) + jj // N_KV_HEADS for jj in range(N_NSA_HEADS))
SEL_REL0 = 2 * SEL_TILE - SEL_BLOCK
SEL_BIAS_ROWS = SEL_REL0 + SEL_TILE


def _softmax_rows(s):
    mx = jnp.max(s, 0, keepdims=True)
    mx = jnp.where(mx > NEG_INF, mx, 0.0)
    e = jnp.exp(s - mx)
    return e / jnp.maximum(jnp.sum(e, 0, keepdims=True), 1e-30)


def _nsa_prompt_body(rb_ref, nq_ref, kv_ref, win_ref, zg_ref, pw_ref, out_ref,
                     ks_sc, vst_sc, wk_sc, wv_sc, kc_sc, pv_sc, vct_sc, bs_sc, bw_sc, fc_sc, far_sc,
                     m_sc, l_sc, acc_sc, *, t_len):
    QB = NSA_QBLOCK
    n_cmp = t_len // CMP_BLOCK
    n_sel = t_len // SEL_BLOCK
    i = pl.program_id(1)
    hi = lax.Precision.HIGHEST

    @pl.when(i == 0)
    def _init():
        ks_sc[0:SEL_TILE, :] = jnp.zeros((SEL_TILE, LANES), BF16)
        vst_sc[0] = jnp.zeros((LANES, SEL_TILE), BF16)
        wk_sc[0:WINDOW, :] = jnp.zeros((WINDOW, LANES), BF16)
        wv_sc[0:WINDOW, :] = jnp.zeros((WINDOW, LANES), BF16)
        rows_per = 2 * SEL_TILE
        blocks_per = rows_per // CMP_BLOCK

        def fill(c, carry):
            r0 = pl.multiple_of(c * rows_per, rows_per)
            kv = kv_ref[0, pl.ds(r0, rows_per), :]
            ks_sc[pl.ds(SEL_TILE + r0, rows_per), :] = kv[:, 2 * LANES:3 * LANES].astype(BF16)
            for half in range(2):
                v_t = kv[half * SEL_TILE:(half + 1) * SEL_TILE, 3 * LANES:4 * LANES]
                vst_sc[2 * c + half + 1] = jnp.transpose(v_t).astype(BF16)
            w = win_ref[0, pl.ds(r0, rows_per), :]
            wk_sc[pl.ds(WINDOW + r0, rows_per), :] = w[:, 0:LANES].astype(BF16)
            wv_sc[pl.ds(WINDOW + r0, rows_per), :] = w[:, LANES:2 * LANES].astype(BF16)
            pooled = jnp.sum(kv[:, 0:2 * LANES].reshape(blocks_per, CMP_BLOCK, 2 * LANES) * pw_ref[...][None],
                             axis=1)
            c0 = pl.multiple_of(c * blocks_per, blocks_per)
            kc_sc[pl.ds(c0, blocks_per), :] = pooled[:, 0:LANES].astype(BF16)
            pv_sc[pl.ds(c0, blocks_per), :] = pooled[:, LANES:2 * LANES]
            return carry

        lax.fori_loop(0, t_len // rows_per, fill, 0)
        vct_sc[...] = jnp.transpose(pv_sc[...]).astype(BF16)

        for jj, h in enumerate(PROMPT_HEADS):
            cs = slice(jj * QB, (jj + 1) * QB)
            u_i = lax.broadcasted_iota(I32, (SEL_BIAS_ROWS, QB), 0)
            q_i = lax.broadcasted_iota(I32, (SEL_BIAS_ROWS, QB), 1)
            dist = q_i - (u_i - SEL_REL0)
            bs_sc[:, cs] = jnp.where(dist >= 0, _bias_from_dist(dist, rb_ref, h), NEG_INF)
            s_i = lax.broadcasted_iota(I32, (WIN_BAND, QB), 0)
            q_i = lax.broadcasted_iota(I32, (WIN_BAND, QB), 1)
            dist = q_i + WINDOW - s_i
            bw_sc[:, cs] = jnp.where((dist >= 0) & (dist <= WINDOW), _bias_from_dist(dist, rb_ref, h), NEG_INF)
            c_i = lax.broadcasted_iota(I32, (n_cmp, QB), 0)
            q_i = lax.broadcasted_iota(I32, (n_cmp, QB), 1)
            rel = jnp.where(c_i < n_cmp // 2, c_i, c_i - n_cmp)
            dist = q_i - CMP_BLOCK * rel - (CMP_BLOCK - 1)
            dist = jnp.where((c_i >= 2) & (c_i < n_cmp // 2), FAR_DIST, dist)
            fc_sc[:, cs] = _bias_from_dist(dist, rb_ref, h)
            far_sc[:, cs] = jnp.full((1, QB), rb_ref[N_BUCKETS - 1, h], F32)

    qbd = _block_diag_queries(nq_ref[0], QB, PROMPT_HEADS)

    s_c = _dot_nt(kc_sc[...], qbd) + pltpu.roll(fc_sc[...], 2 * i, 0)
    c_i = lax.broadcasted_iota(I32, (n_cmp, N_COLS), 0)
    q_i = lax.broadcasted_iota(I32, (n_cmp, N_COLS), 1) % QB
    p_c = _softmax_rows(jnp.where(c_i * CMP_BLOCK + (CMP_BLOCK - 1) <= i * QB + q_i, s_c, NEG_INF))
    o_c = _dot(vct_sc[...], p_c.astype(BF16))

    p_sum = p_c[:, 0:LANES]
    for j in range(1, GQ):
        p_sum = p_sum + p_c[:, j * LANES:(j + 1) * LANES]
    ratio = SEL_BLOCK // CMP_BLOCK
    pair = (lax.broadcasted_iota(I32, (n_sel, n_cmp), 1) // ratio
            == lax.broadcasted_iota(I32, (n_sel, n_cmp), 0)).astype(F32)
    imp = _dot(pair, p_sum, hi)
    blk = lax.broadcasted_iota(I32, (n_sel, LANES), 0)
    score = jnp.where(blk == i, jnp.inf, jnp.where(blk < i, imp, NEG_INF))

    def pick(_, carry):
        work, chosen = carry
        best = jnp.max(work, 0, keepdims=True)
        first = jnp.min(jnp.where(work == best, blk, n_sel), 0, keepdims=True)
        hit = blk == first
        return jnp.where(hit, NEG_INF, work), jnp.where(hit, 1.0, chosen)

    _, chosen = lax.fori_loop(0, SEL_TOPK, pick, (score, jnp.zeros((n_sel, LANES), F32)), unroll=True)
    chosen = jnp.where(score > NEG_INF, chosen, 0.0).astype(BF16)

    def sel_tile(slot, bias, first):
        k_t = ks_sc[pl.ds(pl.multiple_of(slot * SEL_TILE, SEL_TILE), SEL_TILE), :]
        expand = ((slot - 1) * SEL_TILE_BLOCKS + lax.broadcasted_iota(I32, (SEL_TILE, n_sel), 0) // SEL_BLOCK
                  == lax.broadcasted_iota(I32, (SEL_TILE, n_sel), 1))
        picked = _dot(jnp.where(expand, 1.0, 0.0).astype(BF16), chosen)
        picked = jnp.concatenate([picked] * GQ, axis=1) > 0.5
        s = jnp.where(picked, _dot_nt(k_t, qbd) + bias, NEG_INF)
        t_max = jnp.max(s, 0, keepdims=True)
        if first:
            m_new = t_max
            e = jnp.exp(s - m_new)
            l_sc[...] = jnp.sum(e, 0, keepdims=True)
            acc_sc[...] = _dot(vst_sc[slot], e.astype(BF16))
        else:
            m_old = m_sc[...]
            m_new = jnp.maximum(m_old, t_max)
            scale = jnp.exp(m_old - m_new)
            e = jnp.exp(s - m_new)
            l_sc[...] = scale * l_sc[...] + jnp.sum(e, 0, keepdims=True)
            acc_sc[...] = scale * acc_sc[...] + _dot(vst_sc[slot], e.astype(BF16))
        m_sc[...] = m_new

    last = i // SEL_TILE_BLOCKS
    u0 = pl.multiple_of(SEL_REL0 - SEL_TILE - SEL_BLOCK * (i % SEL_TILE_BLOCKS), SEL_BLOCK)
    sel_tile(last + 1, bs_sc[pl.ds(u0 + SEL_TILE, SEL_TILE), :], True)
    sel_tile(last, bs_sc[pl.ds(u0, SEL_TILE), :], False)

    def far_tile(slot, carry):
        sel_tile(slot, far_sc[...], False)
        return carry

    lax.fori_loop(1, last, far_tile, 0)
    o_s = acc_sc[...] / jnp.maximum(l_sc[...], 1e-30)

    w0 = pl.multiple_of(i * QB, QB)
    s_w = _dot_nt(wk_sc[pl.ds(w0, WIN_BAND), :], qbd) + bw_sc[...]
    in_seq = lax.broadcasted_iota(I32, (WIN_BAND, N_COLS), 0) >= WINDOW - i * QB
    p_w = _softmax_rows(jnp.where(in_seq, s_w, NEG_INF))
    o_w = _dot_tn(wv_sc[pl.ds(w0, WIN_BAND), :], p_w.astype(BF16))

    eye = (lax.broadcasted_iota(I32, (3 * N_NSA_HEADS + 8, LANES), 0) + G_NSA
           == lax.broadcasted_iota(I32, (3 * N_NSA_HEADS + 8, LANES), 1)).astype(F32)
    gates = jax.nn.sigmoid(_dot_nt(eye, zg_ref[0], hi))
    pieces = []
    for h in range(N_NSA_HEADS):
        g = h // GQ
        jj = PROMPT_HEADS.index(h)
        part = lambda o: o[g * HEAD_DIM:(g + 1) * HEAD_DIM, jj * QB:(jj + 1) * QB]
        pieces.append(gates[h:h + 1, :] * part(o_c)
                      + gates[N_NSA_HEADS + h:N_NSA_HEADS + h + 1, :] * part(o_s)
                      + gates[2 * N_NSA_HEADS + h:2 * N_NSA_HEADS + h + 1, :] * part(o_w))
    mix_t = jnp.concatenate(pieces, axis=0)
    mix_t = jnp.concatenate([mix_t, jnp.zeros((NSA_W, LANES - QB), F32)], axis=1)
    out_ref[0] = jnp.transpose(mix_t)[0:QB, :]


def _nsa_prompt(nq, kv, win, zg, pool_w2, rel_bias):
    bsz, t_len, _ = nq.shape
    assert t_len // CMP_BLOCK == LANES and t_len % (2 * SEL_TILE) == 0
    n_cmp = t_len // CMP_BLOCK
    per_b = lambda w: pl.BlockSpec((1, t_len, w), lambda b, i: (b, 0, 0))
    blk = lambda w: pl.BlockSpec((1, NSA_QBLOCK, w), lambda b, i: (b, i, 0))
    return pl.pallas_call(
        functools.partial(_nsa_prompt_body, t_len=t_len),
        grid=(bsz, t_len // NSA_QBLOCK),
        in_specs=[pl.BlockSpec(memory_space=pltpu.SMEM),
                  blk(NSA_W), per_b(4 * KV_W), per_b(2 * KV_W), blk(LANES),
                  pl.BlockSpec((CMP_BLOCK, 2 * LANES), lambda b, i: (0, 0))],
        out_specs=blk(NSA_W),
        out_shape=jax.ShapeDtypeStruct((bsz, t_len, NSA_W), F32),
        scratch_shapes=[pltpu.VMEM((SEL_TILE + t_len, LANES), BF16),
                        pltpu.VMEM((1 + t_len // SEL_TILE, LANES, SEL_TILE), BF16),
                        pltpu.VMEM((WINDOW + t_len, LANES), BF16),
                        pltpu.VMEM((WINDOW + t_len, LANES), BF16),
                        pltpu.VMEM((n_cmp, LANES), BF16),
                        pltpu.VMEM((n_cmp, LANES), F32),
                        pltpu.VMEM((LANES, n_cmp), BF16),
                        pltpu.VMEM((SEL_BIAS_ROWS, N_COLS), F32),
                        pltpu.VMEM((WIN_BAND, N_COLS), F32),
                        pltpu.VMEM((n_cmp, N_COLS), F32),
                        pltpu.VMEM((1, N_COLS), F32),
                        pltpu.VMEM((1, N_COLS), F32),
                        pltpu.VMEM((1, N_COLS), F32),
                        pltpu.VMEM((LANES, N_COLS), F32)],
        compiler_params=pltpu.CompilerParams(dimension_semantics=("arbitrary", "arbitrary"),
                                             vmem_limit_bytes=VMEM_LIMIT),
        name="nsa_prompt",
    )(rel_bias, nq, kv, win, zg, pool_w2)


CMP_PAGES = 16
N_PICK = SEL_TOPK - 1
N_OWNERS = N_KV_HEADS * N_NEW
N_FETCH = N_OWNERS * N_PICK
OWNER_KEYS = N_PICK * PAGE_SIZE


def _sample_row_ids():
    r = lax.broadcasted_iota(I32, (N_QROWS, 1), 0)
    return r % SEQ_PAD - (SEQ_PAD - N_NEW), r // SEQ_PAD


def _sample_bias(dist, rb_ref):
    _, head = _sample_row_ids()
    col = lambda k: sum(jnp.where(head == h, rb_ref[k, h], 0.0) for h in range(N_NSA_HEADS))
    acc = jnp.broadcast_to(col(0), dist.shape)
    for k in range(1, N_BUCKETS):
        acc = jnp.where(dist >= BUCKET_THR[k], col(k), acc)
    return acc


def _softmax_two(s1, s2):
    mx = jnp.maximum(jnp.max(s1, -1, keepdims=True), jnp.max(s2, -1, keepdims=True))
    mx = jnp.where(mx > NEG_INF, mx, 0.0)
    e1 = jnp.exp(s1 - mx)
    e2 = jnp.exp(s2 - mx)
    den = jnp.maximum(jnp.sum(e1, -1, keepdims=True) + jnp.sum(e2, -1, keepdims=True), 1e-30)
    return e1 / den, e2 / den


def _nsa_cmp_body(pt_ref, rb_ref, nq_ref, pw_ref, cache_ref, oc_ref, idx_ref,
                  buf, sem, kc_sc, vc_sc, *, n_pages, past_len):
    b = pl.program_id(0)
    nb = pl.num_programs(0)
    n_chunks = n_pages // CMP_PAGES
    blocks_per = CMP_PAGES * PAGE_SIZE // CMP_BLOCK
    n_cmp = past_len // CMP_BLOCK
    n_sel_past = past_len // SEL_BLOCK
    n_prob = N_KV_HEADS * SEQ_PAD
    hi = lax.Precision.HIGHEST

    def page_copy(bb, chunk, p, slot):
        page = pt_ref[bb * n_pages + chunk * CMP_PAGES + p]
        return pltpu.make_async_copy(cache_ref.at[page, pl.ds(0, 2 * LANES), :], buf.at[slot, p], sem.at[slot])

    def start_chunk(bb, chunk, slot):
        for p in range(CMP_PAGES):
            page_copy(bb, chunk, p, slot).start()

    def wait_chunk(bb, chunk, slot):
        for p in range(CMP_PAGES):
            page_copy(bb, chunk, p, slot).wait()

    @pl.when(b == 0)
    def _prologue():
        start_chunk(0, 0, 0)

    def chunk_step(c, carry):
        slot = (b * n_chunks + c) % 2

        @pl.when(c + 1 < n_chunks)
        def _next_same():
            start_chunk(b, c + 1, 1 - slot)

        @pl.when((c + 1 == n_chunks) & (b + 1 < nb))
        def _next_seq():
            start_chunk(b + 1, 0, 1 - slot)

        wait_chunk(b, c, slot)
        rows = jnp.concatenate([jnp.transpose(buf[slot, p]) for p in range(CMP_PAGES)], axis=0)
        pooled = jnp.sum(rows.reshape(blocks_per, CMP_BLOCK, 2 * LANES) * pw_ref[...][None], axis=1)
        c0 = pl.multiple_of(c * blocks_per, blocks_per)
        kc_sc[pl.ds(c0, blocks_per), :] = pooled[:, 0:LANES].astype(BF16)
        vc_sc[pl.ds(c0, blocks_per), :] = pooled[:, LANES:2 * LANES].astype(BF16)
        return carry

    lax.fori_loop(0, n_chunks, chunk_step, 0)

    qbd = _block_diag_queries(nq_ref[0], SEQ_PAD)
    tok, _ = _sample_row_ids()
    c_i = lax.broadcasted_iota(I32, (N_QROWS, n_cmp), 1)
    dist = past_len + tok - (c_i * CMP_BLOCK + CMP_BLOCK - 1)
    s_c = _dot_nt(qbd, kc_sc[...]) + _sample_bias(dist, rb_ref)
    p_c = _masked_softmax(jnp.where(dist >= 0, s_c, NEG_INF))
    oc_ref[0] = _dot(p_c.astype(BF16), vc_sc[...])

    p_sum = jnp.sum(p_c.reshape(N_KV_HEADS, GQ, SEQ_PAD, n_cmp), axis=1).reshape(n_prob, n_cmp)
    ratio = SEL_BLOCK // CMP_BLOCK
    pair = (lax.broadcasted_iota(I32, (n_cmp, n_sel_past), 0) // ratio
            == lax.broadcasted_iota(I32, (n_cmp, n_sel_past), 1)).astype(F32)
    imp = _dot(p_sum, pair, hi)
    blk = lax.broadcasted_iota(I32, imp.shape, 1)
    slot_i = lax.broadcasted_iota(I32, (n_prob, LANES), 1)

    def pick(k, carry):
        work, ids = carry
        best = jnp.max(work, 1, keepdims=True)
        first = jnp.min(jnp.where(work == best, blk, n_sel_past), 1, keepdims=True)
        return jnp.where(blk == first, NEG_INF, work), jnp.where(slot_i == k, first, ids)

    _, ids = lax.fori_loop(0, N_PICK, pick, (imp, jnp.zeros((n_prob, LANES), I32)), unroll=True)
    idx_ref[0] = ids


def _nsa_cmp(page_table, rel_bias, nq_s, pool_w2, cache3):
    dec_b, n_pages = page_table.shape
    past_len = n_pages * PAGE_SIZE
    assert n_pages % CMP_PAGES == 0 and past_len // SEL_BLOCK >= N_PICK
    n_cmp = past_len // CMP_BLOCK
    n_prob = N_KV_HEADS * SEQ_PAD
    grid_spec = pltpu.PrefetchScalarGridSpec(
        num_scalar_prefetch=1,
        grid=(dec_b,),
        in_specs=[pl.BlockSpec(memory_space=pltpu.SMEM),
                  pl.BlockSpec((1, SEQ_PAD, NSA_W), lambda b, pt: (b, 0, 0)),
                  pl.BlockSpec((CMP_BLOCK, 2 * LANES), lambda b, pt: (0, 0)),
                  pl.BlockSpec(memory_space=pl.ANY)],
        out_specs=[pl.BlockSpec((1, N_QROWS, LANES), lambda b, pt: (b, 0, 0)),
                   pl.BlockSpec((1, n_prob, LANES), lambda b, pt: (b, 0, 0))],
        scratch_shapes=[pltpu.VMEM((2, CMP_PAGES, 2 * LANES, PAGE_SIZE), F32),
                        pltpu.SemaphoreType.DMA((2,)),
                        pltpu.VMEM((n_cmp, LANES), BF16),
                        pltpu.VMEM((n_cmp, LANES), BF16)],
    )
    return pl.pallas_call(
        functools.partial(_nsa_cmp_body, n_pages=n_pages, past_len=past_len),
        grid_spec=grid_spec,
        out_shape=[jax.ShapeDtypeStruct((dec_b, N_QROWS, LANES), F32),
                   jax.ShapeDtypeStruct((dec_b, n_prob, LANES), I32)],
        compiler_params=pltpu.CompilerParams(dimension_semantics=("arbitrary",),
                                             vmem_limit_bytes=VMEM_LIMIT),
        name="nsa_sample_cmp",
    )(page_table.reshape(-1), rel_bias, nq_s, pool_w2, cache3)


def _nsa_sel_body(pt_ref, ids_ref, rb_ref, nq_ref, kvn_ref, winc_ref, winn_ref, zg_ref, oc_ref, idv_ref,
                  cache_ref, out_ref, buf, sem, *, n_pages, past_len):
    b = pl.program_id(0)
    nb = pl.num_programs(0)
    hi = lax.Precision.HIGHEST

    def block_copy(bb, n, slot):
        blk = ids_ref[bb * N_FETCH + n]
        page = pt_ref[bb * n_pages + blk // 2]
        return pltpu.make_async_copy(cache_ref.at[page, pl.ds(2 * LANES, 2 * LANES), :], buf.at[slot, n],
                                     sem.at[slot])

    def start_all(bb, slot):
        def go(n, carry):
            block_copy(bb, n, slot).start()
            return carry
        lax.fori_loop(0, N_FETCH, go, 0)

    def wait_all(bb, slot):
        def go(n, carry):
            block_copy(bb, n, slot).wait()
            return carry
        lax.fori_loop(0, N_FETCH, go, 0)

    slot = b % 2

    @pl.when(b == 0)
    def _prologue():
        start_all(0, 0)

    @pl.when(b + 1 < nb)
    def _prefetch():
        start_all(b + 1, 1 - slot)

    qbd = _block_diag_queries(nq_ref[0], SEQ_PAD)
    tok, head = _sample_row_ids()
    owner = (head // GQ) * N_NEW + tok

    wc = winc_ref[0]
    wn = winn_ref[0]
    r_i = lax.broadcasted_iota(I32, (N_QROWS, WINDOW), 1)
    dist1 = WINDOW + tok - r_i
    s1 = _dot(qbd, wc[0:LANES, :].astype(BF16)) + _sample_bias(dist1, rb_ref)
    s1 = jnp.where((dist1 >= 0) & (dist1 <= WINDOW), s1, NEG_INF)
    n_i = lax.broadcasted_iota(I32, (N_QROWS, SEQ_PAD), 1) - (SEQ_PAD - N_NEW)
    dist2 = tok - n_i
    new_ok = (n_i >= 0) & (dist2 >= 0)
    bias2 = _sample_bias(dist2, rb_ref)
    s2 = jnp.where(new_ok, _dot_nt(qbd, wn[:, 0:LANES].astype(BF16)) + bias2, NEG_INF)
    p1, p2 = _softmax_two(s1, s2)
    o_w = (_dot_nt(p1.astype(BF16), wc[LANES:2 * LANES, :].astype(BF16))
           + _dot(p2.astype(BF16), wn[:, LANES:2 * LANES].astype(BF16)))

    wait_all(b, slot)

    def owner_tiles(o, r0):
        return jnp.concatenate([buf[slot, o * N_PICK + k, r0:r0 + LANES, :] for k in range(N_PICK)],
                               axis=1).astype(BF16)

    s_f = jnp.full((N_QROWS, OWNER_KEYS), NEG_INF, F32)
    for o in range(N_OWNERS):
        s_f = jnp.where(owner == o, _dot(qbd, owner_tiles(o, 0)), s_f)
    spread = (lax.broadcasted_iota(I32, (LANES, OWNER_KEYS), 0)
              == lax.broadcasted_iota(I32, (LANES, OWNER_KEYS), 1) // PAGE_SIZE).astype(F32)
    blk_of_key = _dot(idv_ref[0].astype(F32), spread, hi)
    blk_rows = jnp.concatenate([blk_of_key[(h // GQ) * SEQ_PAD:(h // GQ + 1) * SEQ_PAD]
                                for h in range(N_NSA_HEADS)], axis=0).astype(I32)
    blocks_per_page = PAGE_SIZE // SEL_BLOCK
    in_page = lax.broadcasted_iota(I32, (N_QROWS, OWNER_KEYS), 1) % PAGE_SIZE
    in_block = in_page // SEL_BLOCK == blk_rows % blocks_per_page
    dist_s = past_len + tok - ((blk_rows // blocks_per_page) * PAGE_SIZE + in_page)
    s_f = jnp.where((tok >= 0) & in_block & (dist_s >= 0), s_f + _sample_bias(dist_s, rb_ref), NEG_INF)
    kvn = kvn_ref[0]
    s_n = jnp.where(new_ok, _dot_nt(qbd, kvn[:, 2 * LANES:3 * LANES].astype(BF16)) + bias2, NEG_INF)
    p_f, p_n = _softmax_two(s_f, s_n)
    o_s = _dot(p_n.astype(BF16), kvn[:, 3 * LANES:4 * LANES].astype(BF16))
    for o in range(N_OWNERS):
        o_s = o_s + _dot_nt(jnp.where(owner == o, p_f, 0.0).astype(BF16), owner_tiles(o, LANES))

    o_c = oc_ref[0]
    gates = jax.nn.sigmoid(zg_ref[0])
    is_new = lax.broadcasted_iota(I32, (SEQ_PAD, HEAD_DIM), 0) >= SEQ_PAD - N_NEW
    for h in range(N_NSA_HEADS):
        mix = _gated_mix(gates, o_c, o_s, o_w, h, SEQ_PAD)
        out_ref[0, :, h * HEAD_DIM:(h + 1) * HEAD_DIM] = jnp.where(is_new, mix, 0.0)


def _nsa_sel(page_table, ids, ids_rows, rel_bias, nq_s, kv_s, win_cache, win_s, zg_s, o_c, cache3):
    dec_b, n_pages = page_table.shape
    past_len = n_pages * PAGE_SIZE
    seq = lambda w: pl.BlockSpec((1, SEQ_PAD, w), lambda b, pt, ix: (b, 0, 0))
    grid_spec = pltpu.PrefetchScalarGridSpec(
        num_scalar_prefetch=2,
        grid=(dec_b,),
        in_specs=[pl.BlockSpec(memory_space=pltpu.SMEM),
                  seq(NSA_W), seq(4 * KV_W),
                  pl.BlockSpec((1, 2 * KV_W, WINDOW), lambda b, pt, ix: (b, 0, 0)),
                  seq(2 * KV_W), seq(LANES),
                  pl.BlockSpec((1, N_QROWS, LANES), lambda b, pt, ix: (b, 0, 0)),
                  pl.BlockSpec((1, N_KV_HEADS * SEQ_PAD, LANES), lambda b, pt, ix: (b, 0, 0)),
                  pl.BlockSpec(memory_space=pl.ANY)],
        out_specs=seq(NSA_W),
        scratch_shapes=[pltpu.VMEM((2, N_FETCH, 2 * LANES, PAGE_SIZE), F32),
                        pltpu.SemaphoreType.DMA((2,))],
    )
    return pl.pallas_call(
        functools.partial(_nsa_sel_body, n_pages=n_pages, past_len=past_len),
        grid_spec=grid_spec,
        out_shape=jax.ShapeDtypeStruct((dec_b, SEQ_PAD, NSA_W), F32),
        compiler_params=pltpu.CompilerParams(dimension_semantics=("arbitrary",),
                                             vmem_limit_bytes=VMEM_LIMIT),
        name="nsa_sample_sel",
    )(page_table.reshape(-1), ids.reshape(-1), rel_bias, nq_s, kv_s, win_cache, win_s, zg_s, o_c, ids_rows, cache3)


FF_CHUNK = 896


def _ffn_body(*refs, tm, has_override):
    if has_override:
        (x_ref, mm_ref, mn_ref, ov_ref, woa_ref, wob_ref, l1w_ref, l1b_ref, wup_ref, bup_ref, cw_ref, cb_ref,
         wdn_ref, bdn_ref, l2w_ref, l2b_ref, y_ref, tail_ref, carry_sc, ubuf) = refs
    else:
        (x_ref, mm_ref, mn_ref, woa_ref, wob_ref, l1w_ref, l1b_ref, wup_ref, bup_ref, cw_ref, cb_ref,
         wdn_ref, bdn_ref, l2w_ref, l2b_ref, y_ref, tail_ref, carry_sc, ubuf) = refs
    j = pl.program_id(1)

    @pl.when(j == 0)
    def _init():
        carry_sc[...] = jnp.zeros(carry_sc.shape, F32)

    x = x_ref[0]
    h = ALPHA * x + (_dot(mm_ref[0].astype(BF16), woa_ref[...]) + _dot(mn_ref[0].astype(BF16), wob_ref[...]))
    x1 = _layer_norm(h, l1w_ref[...], l1b_ref[...])
    x1b = x1.astype(BF16)
    if has_override:
        state_row = lax.broadcasted_iota(I32, (tm, FF_CHUNK), 0) % SEQ_PAD < SEQ_PAD - N_NEW

    def conv_half(c0):
        cols = slice(c0, c0 + FF_CHUNK)
        u = _dot(x1b, wup_ref[:, cols]) + bup_ref[:, cols]
        if has_override:
            u = jnp.where(state_row, ov_ref[0, :, cols], u)
        ubuf[0:8, :] = carry_sc[:, cols]
        ubuf[8:8 + tm, :] = u
        out = cb_ref[:, cols] + ubuf[6:6 + tm, :] * cw_ref[0:1, cols]
        out = out + ubuf[7:7 + tm, :] * cw_ref[1:2, cols]
        out = out + u * cw_ref[2:3, cols]
        carry_sc[:, cols] = u[tm - 8:tm, :]
        tail_ref[0, 0, :, cols] = u if has_override else u[tm - 8:tm, :]
        return out

    ff = jnp.zeros((tm, D_MODEL), F32)
    for c in range(D_FF // FF_CHUNK):
        ga = conv_half(c * FF_CHUNK)
        gb = conv_half(D_FF + c * FF_CHUNK)
        gelu = ga * (0.5 * (1.0 + jnp.tanh(math.sqrt(2.0 / math.pi) * (ga + 0.044715 * (ga * ga * ga)))))
        ff = ff + _dot((gelu * gb).astype(BF16), wdn_ref[c * FF_CHUNK:(c + 1) * FF_CHUNK, :])
    ff = ff + bdn_ref[...]
    y_ref[0] = _layer_norm(ALPHA * x1 + ff, l2w_ref[...], l2b_ref[...])


def _ffn(x, mix_m, mix_n, override, wts):
    bsz, t_len, _ = x.shape
    tm = min(t_len, 256)
    nt = t_len // tm
    has_override = override is not None
    tail_rows = tm if has_override else 8
    row = lambda w: pl.BlockSpec((1, tm, w), lambda b, j: (b, j, 0))
    const = lambda shape: pl.BlockSpec(shape, lambda b, j: (0,) * len(shape), pipeline_mode=pl.Buffered(1))
    in_specs = [row(D_MODEL), row(MLSTM_W), row(NSA_W)]
    args = [x, mix_m, mix_n]
    if has_override:
        in_specs.append(row(2 * D_FF))
        args.append(override)
    in_specs += [const(w.shape) for w in wts]
    args += list(wts)
    y, tail = pl.pallas_call(
        functools.partial(_ffn_body, tm=tm, has_override=has_override),
        grid=(bsz, nt),
        in_specs=in_specs,
        out_specs=[row(D_MODEL), pl.BlockSpec((1, 1, tail_rows, 2 * D_FF), lambda b, j: (b, j, 0, 0))],
        out_shape=[jax.ShapeDtypeStruct((bsz, t_len, D_MODEL), F32),
                   jax.ShapeDtypeStruct((bsz, nt, tail_rows, 2 * D_FF), F32)],
        scratch_shapes=[pltpu.VMEM((8, 2 * D_FF), F32), pltpu.VMEM((8 + tm, FF_CHUNK), F32)],
        compiler_params=pltpu.CompilerParams(dimension_semantics=("arbitrary", "arbitrary"),
                                             vmem_limit_bytes=VMEM_LIMIT),
        name="ffn",
    )(*args)
    return y, tail


def _permute_in_proj(w_in, b_in):
    gates_a = slice(2 * MLSTM_W + 2 * MLSTM_W, 2 * MLSTM_W + 2 * MLSTM_W + 2 * N_MLSTM_HEADS)
    nsa0 = gates_a.stop
    nsa1 = nsa0 + NSA_W + 6 * KV_W

    def perm(a):
        pad = IN_PAD - a.shape[-1]
        parts = [a[..., :gates_a.start], a[..., nsa0:nsa1], a[..., gates_a], a[..., nsa1:]]
        parts.append(jnp.zeros(a.shape[:-1] + (pad,), a.dtype))
        return jnp.concatenate(parts, axis=-1)

    return perm(w_in).astype(BF16), perm(b_in).reshape(1, IN_PAD)


def _front_pad(a, n):
    return jnp.pad(a, ((0, 0), (n, 0), (0, 0)))


def kernel(x_prompt, x_sample, cache_nsa_kv, cache_win_kv, state_mlstm_c, state_mlstm_n, state_mlstm_m,
           state_mlstm_conv, state_ffn_conv, page_table,
           w_in, b_in, mlstm_conv_w, mlstm_conv_b, mlstm_norm_w, nsa_pool_w, rel_bias, w_out,
           ln1_w, ln1_b, w_up, b_up, ffn_conv_w, ffn_conv_b, w_down, b_down, ln2_w, ln2_b):
    bsz, t_len, _ = x_prompt.shape
    dec_b, n_new, _ = x_sample.shape
    assert n_new == N_NEW and cache_win_kv.shape[1] == WINDOW

    w_p, b_p = _permute_in_proj(w_in, b_in)
    pool_w2 = jnp.concatenate([jnp.repeat(nsa_pool_w[0], HEAD_DIM, axis=-1),
                               jnp.repeat(nsa_pool_w[1], HEAD_DIM, axis=-1)], axis=-1)
    row = lambda v: v.reshape(1, -1)
    ffn_w = (w_out[:MLSTM_W].astype(BF16), w_out[MLSTM_W:].astype(BF16), row(ln1_w), row(ln1_b),
             w_up.astype(BF16), row(b_up), ffn_conv_w, row(ffn_conv_b), w_down.astype(BF16), row(b_down),
             row(ln2_w), row(ln2_b))

    zqk, zv, zo, nq, kv, win, zg = _proj_in(x_prompt.reshape(bsz * t_len, D_MODEL), w_p, b_p)
    seq3 = lambda a: a.reshape(bsz, t_len, a.shape[-1])
    zqk, zv, zo, nq, kv, win, zg = map(seq3, (zqk, zv, zo, nq, kv, win, zg))
    mix_m, c_p, n_p, m_p = _mlstm(zqk, zv, zo, zg, None, mlstm_conv_w, mlstm_conv_b, mlstm_norm_w, 0)
    mix_n = _nsa_prompt(nq, kv, win, zg, pool_w2, rel_bias)
    y_p, tail_p = _ffn(x_prompt, mix_m, mix_n, None, ffn_w)
    kv_p = kv.reshape(bsz, t_len, 4, N_KV_HEADS, HEAD_DIM)
    win_p = win[:, t_len - min(WINDOW, t_len):].reshape(bsz, -1, 2, N_KV_HEADS, HEAD_DIM)
    mconv_p = zqk[:, t_len - (MLSTM_CONV - 1):]
    fconv_p = tail_p[:, -1, 8 - (FFN_CONV - 1):]

    pad = SEQ_PAD - N_NEW
    xs = _front_pad(x_sample, pad)
    szqk, szv, szo, snq, skv, swin, szg = _proj_in(xs.reshape(dec_b * SEQ_PAD, D_MODEL), w_p, b_p)
    sseq = lambda a: a.reshape(dec_b, SEQ_PAD, a.shape[-1])
    szqk, szv, szo, snq, skv, swin, szg = map(sseq, (szqk, szv, szo, snq, skv, swin, szg))
    lead = MLSTM_CHUNK - N_NEW
    new = lambda a: a[:, pad:]
    m_qk = jnp.concatenate([jnp.zeros((dec_b, lead - (MLSTM_CONV - 1), 2 * MLSTM_W), F32),
                            state_mlstm_conv.astype(F32), new(szqk)], axis=1)
    mix_ms, c_s, n_s, m_s = _mlstm(m_qk, _front_pad(new(szv), lead), _front_pad(new(szo), lead),
                                   _front_pad(new(szg), lead),
                                   (state_mlstm_c, state_mlstm_n, state_mlstm_m),
                                   mlstm_conv_w, mlstm_conv_b, mlstm_norm_w, lead)
    mix_ms = mix_ms[:, MLSTM_CHUNK - SEQ_PAD:]
    cache3 = jnp.transpose(cache_nsa_kv, (0, 2, 3, 4, 1)).reshape(cache_nsa_kv.shape[0], 4 * KV_W, PAGE_SIZE)
    o_c, ids_rows = _nsa_cmp(page_table, rel_bias, snq, pool_w2, cache3)
    ids = ids_rows.reshape(dec_b, N_KV_HEADS, SEQ_PAD, LANES)[:, :, pad:, :N_PICK]
    win_cache = jnp.transpose(cache_win_kv, (0, 2, 3, 4, 1)).reshape(dec_b, 2 * KV_W, WINDOW)
    mix_ns = _nsa_sel(page_table, ids, ids_rows, rel_bias, snq, skv, win_cache, swin, szg, o_c, cache3)
    override = jnp.concatenate([jnp.zeros((dec_b, pad - (FFN_CONV - 1), 2 * D_FF), F32),
                                state_ffn_conv.astype(F32),
                                jnp.zeros((dec_b, N_NEW, 2 * D_FF), F32)], axis=1)
    flat = lambda a: a.reshape(1, dec_b * SEQ_PAD, a.shape[-1])
    y_s8, u_s = _ffn(flat(xs), flat(mix_ms), flat(mix_ns), flat(override), ffn_w)
    y_s = y_s8.reshape(dec_b, SEQ_PAD, D_MODEL)[:, pad:]
    kv_s = new(skv).reshape(dec_b, N_NEW, 4, N_KV_HEADS, HEAD_DIM)
    win_s = jnp.concatenate([cache_win_kv[:, N_NEW:], new(swin).reshape(dec_b, N_NEW, 2, N_KV_HEADS, HEAD_DIM)],
                            axis=1)
    mconv_s = szqk[:, SEQ_PAD - (MLSTM_CONV - 1):]
    fconv_s = u_s.reshape(dec_b, SEQ_PAD, 2 * D_FF)[:, SEQ_PAD - (FFN_CONV - 1):]
    return (y_p, y_s, kv_p, kv_s, win_p, win_s, c_p, c_s, n_p, n_s, m_p, m_s, mconv_p, mconv_s, fconv_p, fconv_s)
```

```python
import functools
import math

import jax
import jax.numpy as jnp
from jax import lax
from jax.experimental import pallas as pl
from jax.experimental.pallas import tpu as pltpu

F32 = jnp.float32
BF16 = jnp.bfloat16
I32 = jnp.int32

D_MODEL = 1024
HEAD_DIM = 64
N_MLSTM_HEADS = 8
N_NSA_HEADS = 8
N_KV_HEADS = 2
GQ = N_NSA_HEADS // N_KV_HEADS
MLSTM_W = N_MLSTM_HEADS * HEAD_DIM
NSA_W = N_NSA_HEADS * HEAD_DIM
KV_W = N_KV_HEADS * HEAD_DIM
MLSTM_CHUNK = 64
MLSTM_CONV = 4
CMP_BLOCK = 32
SEL_BLOCK = 64
SEL_TOPK = 16
WINDOW = 512
NSA_QBLOCK = 64
N_BUCKETS = 32
MAX_DISTANCE = 128
D_FF = 2688
FFN_CONV = 3
DEPTH = 1
ALPHA = (2.0 * DEPTH) ** 0.25
LN_EPS = 1e-5
PAGE_SIZE = 128

LANES = 128
VMEM_LIMIT = 56 * 1024 * 1024
NEG_INF = float("-inf")

SEQ_PAD = 8
N_NEW = 4
N_QROWS = N_NSA_HEADS * SEQ_PAD


def _bucket_thresholds():
    max_exact = N_BUCKETS // 2
    thr = list(range(max_exact + 1))
    for k in range(1, N_BUCKETS - max_exact):
        thr.append(int(math.ceil(max_exact * (MAX_DISTANCE / max_exact) ** (k / (N_BUCKETS - max_exact)))))
    return tuple(thr)


BUCKET_THR = _bucket_thresholds()
FAR_DIST = BUCKET_THR[N_BUCKETS - 1]


def _dot(a, b, precision=None):
    return jnp.dot(a, b, preferred_element_type=F32, precision=precision)


def _dot_nt(a, b, precision=None):
    return lax.dot_general(a, b, (((1,), (1,)), ((), ())), preferred_element_type=F32, precision=precision)


def _dot_tn(a, b):
    return lax.dot_general(a, b, (((0,), (0,)), ((), ())), preferred_element_type=F32)


def _layer_norm(x, w, b):
    mu = jnp.mean(x, -1, keepdims=True)
    xc = x - mu
    var = jnp.mean(xc * xc, -1, keepdims=True)
    return xc * lax.rsqrt(var + LN_EPS) * w + b


def _log_sigmoid(x):
    return jnp.minimum(x, 0.0) - jnp.log1p(jnp.exp(-jnp.abs(x)))


def _bias_from_dist(dist, rb_ref, head):
    acc = jnp.full(dist.shape, rb_ref[0, head], F32)
    for k in range(1, N_BUCKETS):
        acc = jnp.where(dist >= BUCKET_THR[k], rb_ref[k, head], acc)
    return acc


def _masked_softmax(s):
    mx = jnp.max(s, -1, keepdims=True)
    mx = jnp.where(mx > NEG_INF, mx, 0.0)
    e = jnp.exp(s - mx)
    den = jnp.sum(e, -1, keepdims=True)
    return e / jnp.maximum(den, 1e-30)


_C_QK, _C_V, _C_O, _C_NQ, _C_KV, _C_WIN, _C_G = 0, 1024, 1536, 2048, 2560, 3072, 3328
IN_PAD = 3456
G_IGATE, G_FGATE, G_NSA = 0, N_MLSTM_HEADS, 2 * N_MLSTM_HEADS


def _proj_in_body(x_ref, w_ref, b_ref, qk_ref, v_ref, o_ref, nq_ref, kv_ref, win_ref, g_ref):
    z = _dot(x_ref[...].astype(BF16), w_ref[...]) + b_ref[...]
    qk_ref[...] = z[:, _C_QK:_C_V]
    v_ref[...] = z[:, _C_V:_C_O]
    o_ref[...] = z[:, _C_O:_C_NQ]
    nq_ref[...] = z[:, _C_NQ:_C_KV]
    kv_ref[...] = z[:, _C_KV:_C_WIN]
    win_ref[...] = z[:, _C_WIN:_C_G]
    g_ref[...] = z[:, _C_G:IN_PAD]


def _proj_in(x2d, w_p, b_p):
    rows = x2d.shape[0]
    tm = min(rows, 256)
    widths = (1024, 512, 512, 512, 512, 256, 128)
    const = lambda i: (0, 0)
    return pl.pallas_call(
        _proj_in_body,
        grid=(rows // tm,),
        in_specs=[pl.BlockSpec((tm, D_MODEL), lambda i: (i, 0)),
                  pl.BlockSpec((D_MODEL, IN_PAD), const, pipeline_mode=pl.Buffered(1)),
                  pl.BlockSpec((1, IN_PAD), const, pipeline_mode=pl.Buffered(1))],
        out_specs=[pl.BlockSpec((tm, w), lambda i: (i, 0)) for w in widths],
        out_shape=[jax.ShapeDtypeStruct((rows, w), F32) for w in widths],
        compiler_params=pltpu.CompilerParams(dimension_semantics=("arbitrary",),
                                             vmem_limit_bytes=VMEM_LIMIT),
        name="proj_in",
    )(x2d, w_p, b_p)


def _mlstm_body(*refs, tb, n_dummy, has_state):
    if has_state:
        (zqk_ref, zv_ref, zo_ref, zg_ref, c0_ref, n0_ref, m0_ref, cw_ref, cb_ref, nw_ref,
         mix_ref, c_out, n_out, m_out, ubuf, qk_sc, c_sc, n_sc, m_sc) = refs
    else:
        (zqk_ref, zv_ref, zo_ref, zg_ref, cw_ref, cb_ref, nw_ref,
         mix_ref, c_out, n_out, m_out, ubuf, qk_sc, c_sc, n_sc, m_sc) = refs
    L = MLSTM_CHUNK
    D = HEAD_DIM
    j = pl.program_id(1)

    @pl.when(j == 0)
    def _init():
        ubuf[0:8, :] = jnp.zeros((8, 2 * MLSTM_W), F32)
        if has_state:
            c_sc[...] = c0_ref[0]
            n_sc[...] = n0_ref[0]
            m_sc[...] = m0_ref[0]
        else:
            c_sc[...] = jnp.zeros(c_sc.shape, F32)
            n_sc[...] = jnp.zeros(n_sc.shape, F32)
            m_sc[...] = jnp.zeros(m_sc.shape, F32)

    u = zqk_ref[0]
    ubuf[8:8 + tb, :] = u
    conv = cb_ref[...] + ubuf[5:5 + tb, :] * cw_ref[0:1, :]
    conv = conv + ubuf[6:6 + tb, :] * cw_ref[1:2, :]
    conv = conv + ubuf[7:7 + tb, :] * cw_ref[2:3, :]
    conv = conv + u * cw_ref[3:4, :]
    ubuf[5:8, :] = u[tb - 3:tb, :]
    qk_sc[...] = conv * jax.nn.sigmoid(conv)

    row_i = lax.broadcasted_iota(I32, (L, L), 0)
    col_i = lax.broadcasted_iota(I32, (L, L), 1)
    causal = col_i <= row_i
    tri = causal.astype(F32)
    eye_g = (lax.broadcasted_iota(I32, (2 * N_MLSTM_HEADS, LANES), 0)
             == lax.broadcasted_iota(I32, (2 * N_MLSTM_HEADS, LANES), 1)).astype(F32)
    hi = lax.Precision.HIGHEST

    def chunk(ci, carry):
        r0 = pl.multiple_of(ci * L, L)
        g = zg_ref[0, pl.ds(r0, L), :]
        if n_dummy:
            lane = lax.broadcasted_iota(I32, (L, LANES), 1)
            dummy = (lax.broadcasted_iota(I32, (L, LANES), 0) + ci * L) < n_dummy
            g = jnp.where(dummy, jnp.where(lane < G_FGATE, -1e30, 1e30), g)
        g_t = _dot_nt(eye_g, g, hi)
        lf = _log_sigmoid(g)
        lf_t = _log_sigmoid(g_t[G_FGATE:G_FGATE + N_MLSTM_HEADS, :])
        ig_t = g_t[G_IGATE:G_IGATE + N_MLSTM_HEADS, :]
        b_all = _dot(tri, lf, hi)
        b_t = _dot_nt(lf_t, tri, hi)
        for h in range(N_MLSTM_HEADS):
            hs = slice(h * D, (h + 1) * D)
            q_h = qk_sc[pl.ds(r0, L), hs]
            k_h = qk_sc[pl.ds(r0, L), MLSTM_W + h * D:MLSTM_W + (h + 1) * D] * (D ** -0.5)
            v_h = zv_ref[0, pl.ds(r0, L), hs]
            bc = b_all[:, G_FGATE + h:G_FGATE + h + 1]
            ic = g[:, G_IGATE + h:G_IGATE + h + 1]
            br = b_t[h:h + 1, :]
            ir = ig_t[h:h + 1, :]
            m_h = m_sc[0:1, h:h + 1]
            c_h = c_sc[h]
            n_h = n_sc[h:h + 1, :]
            dmat = jnp.where(causal, bc - br + ir, NEG_INF)
            inter = bc + m_h
            m_t = jnp.maximum(inter, jnp.max(dmat, -1, keepdims=True))
            w_intra = jnp.exp(dmat - m_t)
            w_inter = jnp.exp(inter - m_t)
            qb = q_h.astype(BF16)
            kb = k_h.astype(BF16)
            vb = v_h.astype(BF16)
            a = w_intra * _dot_nt(qb, kb)
            num = _dot(a.astype(BF16), vb) + _dot_nt(qb, c_h.astype(BF16)) * w_inter
            den = jnp.sum(a, -1, keepdims=True) + w_inter * jnp.sum(q_h * n_h, -1, keepdims=True)
            den = jnp.maximum(jnp.abs(den), jnp.exp(-m_t))
            hh = num / den
            b_end = bc[L - 1:L, :]
            m_new = jnp.maximum(b_end + m_h, jnp.max(b_end - br + ir, -1, keepdims=True))
            wk = jnp.exp(b_end - bc + ic - m_new)
            sc = jnp.exp(b_end + m_h - m_new)
            c_sc[h] = sc * c_h + _dot_tn((v_h * wk).astype(BF16), kb)
            n_sc[h:h + 1, :] = sc * n_h + jnp.sum(wk * k_h, 0, keepdims=True)
            m_sc[0:1, h:h + 1] = m_new
            mu = jnp.mean(hh, -1, keepdims=True)
            hc = hh - mu
            var = jnp.mean(hc * hc, -1, keepdims=True)
            hn = hc * lax.rsqrt(var + LN_EPS)
            og = jax.nn.sigmoid(zo_ref[0, pl.ds(r0, L), hs])
            mix_ref[0, pl.ds(r0, L), hs] = og * (hn * nw_ref[:, hs])
        return carry

    lax.fori_loop(0, tb // L, chunk, 0)

    @pl.when(j == pl.num_programs(1) - 1)
    def _final():
        c_out[0] = c_sc[...]
        n_out[0] = n_sc[...]
        m_out[0] = m_sc[...]


def _mlstm(zqk, zv, zo, zg, state, conv_w, conv_b, norm_w, n_dummy):
    bsz, t_len, _ = zqk.shape
    tb = min(t_len, 256)
    has_state = state is not None
    row = lambda w: pl.BlockSpec((1, tb, w), lambda b, j: (b, j, 0))
    per_b = lambda shape: pl.BlockSpec((1,) + shape, lambda b, j: (b,) + (0,) * len(shape))
    const = lambda shape: pl.BlockSpec(shape, lambda b, j: (0,) * len(shape))
    in_specs = [row(2 * MLSTM_W), row(MLSTM_W), row(MLSTM_W), row(LANES)]
    args = [zqk, zv, zo, zg]
    if has_state:
        c0, n0, m0 = state
        in_specs += [per_b((N_MLSTM_HEADS, HEAD_DIM, HEAD_DIM)), per_b((N_MLSTM_HEADS, HEAD_DIM)),
                     per_b((1, N_MLSTM_HEADS))]
        args += [c0, n0, m0.reshape(bsz, 1, N_MLSTM_HEADS)]
    in_specs += [const((MLSTM_CONV, 2 * MLSTM_W)), const((1, 2 * MLSTM_W)), const((1, MLSTM_W))]
    args += [conv_w, conv_b.reshape(1, -1), norm_w.reshape(1, -1)]
    mix, c_new, n_new, m_new = pl.pallas_call(
        functools.partial(_mlstm_body, tb=tb, n_dummy=n_dummy, has_state=has_state),
        grid=(bsz, t_len // tb),
        in_specs=in_specs,
        out_specs=[row(MLSTM_W), per_b((N_MLSTM_HEADS, HEAD_DIM, HEAD_DIM)),
                   per_b((N_MLSTM_HEADS, HEAD_DIM)), per_b((1, N_MLSTM_HEADS))],
        out_shape=[jax.ShapeDtypeStruct((bsz, t_len, MLSTM_W), F32),
                   jax.ShapeDtypeStruct((bsz, N_MLSTM_HEADS, HEAD_DIM, HEAD_DIM), F32),
                   jax.ShapeDtypeStruct((bsz, N_MLSTM_HEADS, HEAD_DIM), F32),
                   jax.ShapeDtypeStruct((bsz, 1, N_MLSTM_HEADS), F32)],
        scratch_shapes=[pltpu.VMEM((8 + tb, 2 * MLSTM_W), F32),
                        pltpu.VMEM((tb, 2 * MLSTM_W), F32),
                        pltpu.VMEM((N_MLSTM_HEADS, HEAD_DIM, HEAD_DIM), F32),
                        pltpu.VMEM((N_MLSTM_HEADS, HEAD_DIM), F32),
                        pltpu.VMEM((1, N_MLSTM_HEADS), F32)],
        compiler_params=pltpu.CompilerParams(dimension_semantics=("arbitrary", "arbitrary"),
                                             vmem_limit_bytes=VMEM_LIMIT),
        name="mlstm",
    )(*args)
    return mix, c_new, n_new, m_new.reshape(bsz, N_MLSTM_HEADS)


def _block_diag_queries(x, n_rows, heads=tuple(range(N_NSA_HEADS)), scale=HEAD_DIM ** -0.5):
    lane = lax.broadcasted_iota(I32, (n_rows, LANES), 1)
    low = lane < HEAD_DIM
    pieces = []
    for h in heads:
        pair = x[:, (h // 2) * LANES:(h // 2 + 1) * LANES]
        src_low = h % 2 == 0
        dst_low = h // GQ == 0
        if src_low != dst_low:
            pair = pltpu.roll(pair, HEAD_DIM, 1)
        pieces.append(jnp.where(low if dst_low else ~low, pair, 0.0))
    return (jnp.concatenate(pieces, axis=0) * scale).astype(BF16)


def _group_lanes(o, g):
    return o[:, g * HEAD_DIM:(g + 1) * HEAD_DIM]


def _gated_mix(gates, o_c, o_s, o_w, h, n_rows):
    g = h // GQ
    rs = slice(h * n_rows, (h + 1) * n_rows)
    col = lambda branch: gates[:, G_NSA + branch * N_NSA_HEADS + h:G_NSA + branch * N_NSA_HEADS + h + 1]
    return (col(0) * _group_lanes(o_c[rs], g) + col(1) * _group_lanes(o_s[rs], g)
            + col(2) * _group_lanes(o_w[rs], g))


SEL_TILE = 256
SEL_TILE_BLOCKS = SEL_TILE // SEL_BLOCK
WIN_BAND = WINDOW + NSA_QBLOCK
N_COLS = N_NSA_HEADS * NSA_QBLOCK
PROMPT_HEADS = tuple(GQ * (jj % N_KV_HEADS) + jj // N_KV_HEADS for jj in range(N_NSA_HEADS))
SEL_REL0 = 2 * SEL_TILE - SEL_BLOCK
SEL_BIAS_ROWS = SEL_REL0 + SEL_TILE


LOG2E = math.log2(math.e)


def _softmax2_rows(s):
    mx = jnp.max(s, 0, keepdims=True)
    mx = jnp.where(mx > NEG_INF, mx, 0.0)
    e = jnp.exp2(s - mx)
    return e / jnp.maximum(jnp.sum(e, 0, keepdims=True), 1e-30)


def _nsa_prompt_body(rb_ref, nq_ref, kv_ref, win_ref, zg_ref, pw_ref, out_ref,
                     ks_sc, vst_sc, wk_sc, wv_sc, kc_sc, pv_sc, vct_sc, bs_sc, bw_sc, fc_sc,
                     m_sc, l_sc, acc_sc, *, t_len):
    QB = NSA_QBLOCK
    n_cmp = t_len // CMP_BLOCK
    n_sel = t_len // SEL_BLOCK
    i = pl.program_id(1)
    hi = lax.Precision.HIGHEST

    @pl.when(i == 0)
    def _init():
        ks_sc[0:SEL_TILE, :] = jnp.zeros((SEL_TILE, LANES), BF16)
        vst_sc[0] = jnp.zeros((LANES, SEL_TILE), BF16)
        wk_sc[0:WINDOW, :] = jnp.zeros((WINDOW, LANES), BF16)
        wv_sc[0:WINDOW, :] = jnp.zeros((WINDOW, LANES), BF16)
        rows_per = 2 * SEL_TILE
        blocks_per = rows_per // CMP_BLOCK

        def fill(c, carry):
            r0 = pl.multiple_of(c * rows_per, rows_per)
            kv = kv_ref[0, pl.ds(r0, rows_per), :]
            ks_sc[pl.ds(SEL_TILE + r0, rows_per), :] = kv[:, 2 * LANES:3 * LANES].astype(BF16)
            for half in range(2):
                v_t = kv[half * SEL_TILE:(half + 1) * SEL_TILE, 3 * LANES:4 * LANES]
                vst_sc[2 * c + half + 1] = jnp.transpose(v_t).astype(BF16)
            w = win_ref[0, pl.ds(r0, rows_per), :]
            wk_sc[pl.ds(WINDOW + r0, rows_per), :] = w[:, 0:LANES].astype(BF16)
            wv_sc[pl.ds(WINDOW + r0, rows_per), :] = w[:, LANES:2 * LANES].astype(BF16)
            pooled = jnp.sum(kv[:, 0:2 * LANES].reshape(blocks_per, CMP_BLOCK, 2 * LANES) * pw_ref[...][None],
                             axis=1)
            c0 = pl.multiple_of(c * blocks_per, blocks_per)
            kc_sc[pl.ds(c0, blocks_per), :] = pooled[:, 0:LANES].astype(BF16)
            pv_sc[pl.ds(c0, blocks_per), :] = pooled[:, LANES:2 * LANES]
            return carry

        lax.fori_loop(0, t_len // rows_per, fill, 0)
        vct_sc[...] = jnp.transpose(pv_sc[...]).astype(BF16)

        for jj, h in enumerate(PROMPT_HEADS):
            cs = slice(jj * QB, (jj + 1) * QB)
            u_i = lax.broadcasted_iota(I32, (SEL_BIAS_ROWS, QB), 0)
            q_i = lax.broadcasted_iota(I32, (SEL_BIAS_ROWS, QB), 1)
            dist = q_i - (u_i - SEL_REL0)
            near = (_bias_from_dist(dist, rb_ref, h) - rb_ref[N_BUCKETS - 1, h]) * LOG2E
            bs_sc[:, cs] = jnp.where(dist >= 0, near, NEG_INF)
            s_i = lax.broadcasted_iota(I32, (WIN_BAND, QB), 0)
            q_i = lax.broadcasted_iota(I32, (WIN_BAND, QB), 1)
            dist = q_i + WINDOW - s_i
            bw_sc[:, cs] = jnp.where((dist >= 0) & (dist <= WINDOW), _bias_from_dist(dist, rb_ref, h) * LOG2E,
                                     NEG_INF)
            c_i = lax.broadcasted_iota(I32, (n_cmp, QB), 0)
            q_i = lax.broadcasted_iota(I32, (n_cmp, QB), 1)
            rel = jnp.where(c_i < n_cmp // 2, c_i, c_i - n_cmp)
            dist = q_i - CMP_BLOCK * rel - (CMP_BLOCK - 1)
            dist = jnp.where((c_i >= 2) & (c_i < n_cmp // 2), FAR_DIST, dist)
            fc_sc[:, cs] = _bias_from_dist(dist, rb_ref, h) * LOG2E

    qbd = _block_diag_queries(nq_ref[0], QB, PROMPT_HEADS, HEAD_DIM ** -0.5 * LOG2E)

    w0 = pl.multiple_of(i * QB, QB)
    s_w = _dot_nt(wk_sc[pl.ds(w0, WIN_BAND), :], qbd) + bw_sc[...]
    in_seq = lax.broadcasted_iota(I32, (WIN_BAND, N_COLS), 0) >= WINDOW - i * QB
    p_w = _softmax2_rows(jnp.where(in_seq, s_w, NEG_INF))
    o_w = _dot_tn(wv_sc[pl.ds(w0, WIN_BAND), :], p_w.astype(BF16))

    s_c = _dot_nt(kc_sc[...], qbd) + pltpu.roll(fc_sc[...], 2 * i, 0)
    c_i = lax.broadcasted_iota(I32, (n_cmp, N_COLS), 0)
    q_i = lax.broadcasted_iota(I32, (n_cmp, N_COLS), 1) % QB
    p_c = _softmax2_rows(jnp.where(c_i * CMP_BLOCK + (CMP_BLOCK - 1) <= i * QB + q_i, s_c, NEG_INF))
    o_c = _dot(vct_sc[...], p_c.astype(BF16))

    p_sum = p_c[:, 0:LANES]
    for j in range(1, GQ):
        p_sum = p_sum + p_c[:, j * LANES:(j + 1) * LANES]
    ratio = SEL_BLOCK // CMP_BLOCK
    pair = (lax.broadcasted_iota(I32, (n_sel, n_cmp), 1) // ratio
            == lax.broadcasted_iota(I32, (n_sel, n_cmp), 0)).astype(F32)
    imp = _dot(pair, p_sum, hi)
    blk = lax.broadcasted_iota(I32, (n_sel, LANES), 0)
    score = jnp.where(blk == i, jnp.inf, jnp.where(blk < i, imp, NEG_INF))

    def pick(_, carry):
        work, chosen = carry
        best = jnp.max(work, 0, keepdims=True)
        first = jnp.min(jnp.where(work == best, blk, n_sel), 0, keepdims=True)
        hit = blk == first
        return jnp.where(hit, NEG_INF, work), jnp.where(hit, 1.0, chosen)

    _, chosen = lax.fori_loop(0, SEL_TOPK, pick, (score, jnp.zeros((n_sel, LANES), F32)), unroll=True)
    chosen = jnp.where(score > NEG_INF, chosen, 0.0).astype(BF16)

    def tile_scores(slot, bias):
        k_t = ks_sc[pl.ds(pl.multiple_of(slot * SEL_TILE, SEL_TILE), SEL_TILE), :]
        expand = ((slot - 1) * SEL_TILE_BLOCKS + lax.broadcasted_iota(I32, (SEL_TILE, n_sel), 0) // SEL_BLOCK
                  == lax.broadcasted_iota(I32, (SEL_TILE, n_sel), 1))
        picked = _dot(jnp.where(expand, 1.0, 0.0).astype(BF16), chosen)
        picked = jnp.concatenate([picked] * GQ, axis=1) > 0.5
        s = _dot_nt(k_t, qbd)
        return jnp.where(picked, s if bias is None else s + bias, NEG_INF)

    def sel_pair(slot_a, slot_b, bias_a, bias_b, first):
        s_a = tile_scores(slot_a, bias_a)
        s_b = tile_scores(slot_b, bias_b)
        t_max = jnp.maximum(jnp.max(s_a, 0, keepdims=True), jnp.max(s_b, 0, keepdims=True))
        m_new = t_max if first else jnp.maximum(m_sc[...], t_max)
        e_a = jnp.exp2(s_a - m_new)
        e_b = jnp.exp2(s_b - m_new)
        l_new = jnp.sum(e_a, 0, keepdims=True) + jnp.sum(e_b, 0, keepdims=True)
        acc_new = _dot(vst_sc[slot_a], e_a.astype(BF16)) + _dot(vst_sc[slot_b], e_b.astype(BF16))
        if first:
            l_sc[...] = l_new
            acc_sc[...] = acc_new
        else:
            scale = jnp.exp2(m_sc[...] - m_new)
            l_sc[...] = scale * l_sc[...] + l_new
            acc_sc[...] = scale * acc_sc[...] + acc_new
        m_sc[...] = m_new

    last = i // SEL_TILE_BLOCKS
    u0 = pl.multiple_of(SEL_REL0 - SEL_TILE - SEL_BLOCK * (i % SEL_TILE_BLOCKS), SEL_BLOCK)
    sel_pair(last + 1, last, bs_sc[pl.ds(u0 + SEL_TILE, SEL_TILE), :], bs_sc[pl.ds(u0, SEL_TILE), :], True)

    def far_pair(p, carry):
        slot_b = 2 * p + 2
        sel_pair(2 * p + 1, jnp.where(slot_b < last, slot_b, 0), None, None, False)
        return carry

    lax.fori_loop(0, last // 2, far_pair, 0)
    o_s = acc_sc[...] / jnp.maximum(l_sc[...], 1e-30)

    eye = (lax.broadcasted_iota(I32, (3 * N_NSA_HEADS + 8, LANES), 0) + G_NSA
           == lax.broadcasted_iota(I32, (3 * N_NSA_HEADS + 8, LANES), 1)).astype(F32)
    gates = jax.nn.sigmoid(_dot_nt(eye, zg_ref[0], hi))
    pieces = []
    for h in range(N_NSA_HEADS):
        g = h // GQ
        jj = PROMPT_HEADS.index(h)
        part = lambda o: o[g * HEAD_DIM:(g + 1) * HEAD_DIM, jj * QB:(jj + 1) * QB]
        pieces.append(gates[h:h + 1, :] * part(o_c)
                      + gates[N_NSA_HEADS + h:N_NSA_HEADS + h + 1, :] * part(o_s)
                      + gates[2 * N_NSA_HEADS + h:2 * N_NSA_HEADS + h + 1, :] * part(o_w))
    mix_t = jnp.concatenate(pieces, axis=0)
    mix_t = jnp.concatenate([mix_t, jnp.zeros((NSA_W, LANES - QB), F32)], axis=1)
    out_ref[0] = jnp.transpose(mix_t)[0:QB, :]


def _nsa_prompt(nq, kv, win, zg, pool_w2, rel_bias):
    bsz, t_len, _ = nq.shape
    assert t_len // CMP_BLOCK == LANES and t_len % (2 * SEL_TILE) == 0
    n_cmp = t_len // CMP_BLOCK
    per_b = lambda w: pl.BlockSpec((1, t_len, w), lambda b, i: (b, 0, 0))
    blk = lambda w: pl.BlockSpec((1, NSA_QBLOCK, w), lambda b, i: (b, i, 0))
    return pl.pallas_call(
        functools.partial(_nsa_prompt_body, t_len=t_len),
        grid=(bsz, t_len // NSA_QBLOCK),
        in_specs=[pl.BlockSpec(memory_space=pltpu.SMEM),
                  blk(NSA_W), per_b(4 * KV_W), per_b(2 * KV_W), blk(LANES),
                  pl.BlockSpec((CMP_BLOCK, 2 * LANES), lambda b, i: (0, 0))],
        out_specs=blk(NSA_W),
        out_shape=jax.ShapeDtypeStruct((bsz, t_len, NSA_W), F32),
        scratch_shapes=[pltpu.VMEM((SEL_TILE + t_len, LANES), BF16),
                        pltpu.VMEM((1 + t_len // SEL_TILE, LANES, SEL_TILE), BF16),
                        pltpu.VMEM((WINDOW + t_len, LANES), BF16),
                        pltpu.VMEM((WINDOW + t_len, LANES), BF16),
                        pltpu.VMEM((n_cmp, LANES), BF16),
                        pltpu.VMEM((n_cmp, LANES), F32),
                        pltpu.VMEM((LANES, n_cmp), BF16),
                        pltpu.VMEM((SEL_BIAS_ROWS, N_COLS), F32),
                        pltpu.VMEM((WIN_BAND, N_COLS), F32),
                        pltpu.VMEM((n_cmp, N_COLS), F32),
                        pltpu.VMEM((1, N_COLS), F32),
                        pltpu.VMEM((1, N_COLS), F32),
                        pltpu.VMEM((LANES, N_COLS), F32)],
        compiler_params=pltpu.CompilerParams(dimension_semantics=("arbitrary", "arbitrary"),
                                             vmem_limit_bytes=VMEM_LIMIT),
        name="nsa_prompt",
    )(rel_bias, nq, kv, win, zg, pool_w2)


CMP_PAGES = 16
N_PICK = SEL_TOPK - 1
N_OWNERS = N_KV_HEADS * N_NEW
N_FETCH = N_OWNERS * N_PICK
OWNER_KEYS = N_PICK * PAGE_SIZE


def _sample_row_ids():
    r = lax.broadcasted_iota(I32, (N_QROWS, 1), 0)
    return r % SEQ_PAD - (SEQ_PAD - N_NEW), r // SEQ_PAD


def _sample_bias(dist, rb_ref):
    _, head = _sample_row_ids()
    col = lambda k: sum(jnp.where(head == h, rb_ref[k, h], 0.0) for h in range(N_NSA_HEADS))
    acc = jnp.broadcast_to(col(0), dist.shape)
    for k in range(1, N_BUCKETS):
        acc = jnp.where(dist >= BUCKET_THR[k], col(k), acc)
    return acc


def _softmax_two(s1, s2):
    mx = jnp.maximum(jnp.max(s1, -1, keepdims=True), jnp.max(s2, -1, keepdims=True))
    mx = jnp.where(mx > NEG_INF, mx, 0.0)
    e1 = jnp.exp(s1 - mx)
    e2 = jnp.exp(s2 - mx)
    den = jnp.maximum(jnp.sum(e1, -1, keepdims=True) + jnp.sum(e2, -1, keepdims=True), 1e-30)
    return e1 / den, e2 / den


def _nsa_cmp_body(pt_ref, rb_ref, nq_ref, pw_ref, cache_ref, oc_ref, idx_ref,
                  buf, sem, kc_sc, vc_sc, *, n_pages, past_len):
    b = pl.program_id(0)
    nb = pl.num_programs(0)
    n_chunks = n_pages // CMP_PAGES
    blocks_per = CMP_PAGES * PAGE_SIZE // CMP_BLOCK
    n_cmp = past_len // CMP_BLOCK
    n_sel_past = past_len // SEL_BLOCK
    n_prob = N_KV_HEADS * SEQ_PAD
    hi = lax.Precision.HIGHEST

    def page_copy(bb, chunk, p, slot):
        page = pt_ref[bb * n_pages + chunk * CMP_PAGES + p]
        return pltpu.make_async_copy(cache_ref.at[page, pl.ds(0, 2 * LANES), :], buf.at[slot, p], sem.at[slot])

    def start_chunk(bb, chunk, slot):
        for p in range(CMP_PAGES):
            page_copy(bb, chunk, p, slot).start()

    def wait_chunk(bb, chunk, slot):
        for p in range(CMP_PAGES):
            page_copy(bb, chunk, p, slot).wait()

    @pl.when(b == 0)
    def _prologue():
        start_chunk(0, 0, 0)

    def chunk_step(c, carry):
        slot = (b * n_chunks + c) % 2

        @pl.when(c + 1 < n_chunks)
        def _next_same():
            start_chunk(b, c + 1, 1 - slot)

        @pl.when((c + 1 == n_chunks) & (b + 1 < nb))
        def _next_seq():
            start_chunk(b + 1, 0, 1 - slot)

        wait_chunk(b, c, slot)
        rows = jnp.concatenate([jnp.transpose(buf[slot, p]) for p in range(CMP_PAGES)], axis=0)
        pooled = jnp.sum(rows.reshape(blocks_per, CMP_BLOCK, 2 * LANES) * pw_ref[...][None], axis=1)
        c0 = pl.multiple_of(c * blocks_per, blocks_per)
        kc_sc[pl.ds(c0, blocks_per), :] = pooled[:, 0:LANES].astype(BF16)
        vc_sc[pl.ds(c0, blocks_per), :] = pooled[:, LANES:2 * LANES].astype(BF16)
        return carry

    lax.fori_loop(0, n_chunks, chunk_step, 0)

    qbd = _block_diag_queries(nq_ref[0], SEQ_PAD)
    tok, _ = _sample_row_ids()
    c_i = lax.broadcasted_iota(I32, (N_QROWS, n_cmp), 1)
    dist = past_len + tok - (c_i * CMP_BLOCK + CMP_BLOCK - 1)
    s_c = _dot_nt(qbd, kc_sc[...]) + _sample_bias(dist, rb_ref)
    p_c = _masked_softmax(jnp.where(dist >= 0, s_c, NEG_INF))
    oc_ref[0] = _dot(p_c.astype(BF16), vc_sc[...])

    p_sum = jnp.sum(p_c.reshape(N_KV_HEADS, GQ, SEQ_PAD, n_cmp), axis=1).reshape(n_prob, n_cmp)
    ratio = SEL_BLOCK // CMP_BLOCK
    pair = (lax.broadcasted_iota(I32, (n_cmp, n_sel_past), 0) // ratio
            == lax.broadcasted_iota(I32, (n_cmp, n_sel_past), 1)).astype(F32)
    imp = _dot(p_sum, pair, hi)
    blk = lax.broadcasted_iota(I32, imp.shape, 1)
    slot_i = lax.broadcasted_iota(I32, (n_prob, LANES), 1)

    def pick(k, carry):
        work, ids = carry
        best = jnp.max(work, 1, keepdims=True)
        first = jnp.min(jnp.where(work == best, blk, n_sel_past), 1, keepdims=True)
        return jnp.where(blk == first, NEG_INF, work), jnp.where(slot_i == k, first, ids)

    _, ids = lax.fori_loop(0, N_PICK, pick, (imp, jnp.zeros((n_prob, LANES), I32)), unroll=True)
    idx_ref[0] = ids


def _nsa_cmp(page_table, rel_bias, nq_s, pool_w2, cache3):
    dec_b, n_pages = page_table.shape
    past_len = n_pages * PAGE_SIZE
    assert n_pages % CMP_PAGES == 0 and past_len // SEL_BLOCK >= N_PICK
    n_cmp = past_len // CMP_BLOCK
    n_prob = N_KV_HEADS * SEQ_PAD
    grid_spec = pltpu.PrefetchScalarGridSpec(
        num_scalar_prefetch=1,
        grid=(dec_b,),
        in_specs=[pl.BlockSpec(memory_space=pltpu.SMEM),
                  pl.BlockSpec((1, SEQ_PAD, NSA_W), lambda b, pt: (b, 0, 0)),
                  pl.BlockSpec((CMP_BLOCK, 2 * LANES), lambda b, pt: (0, 0)),
                  pl.BlockSpec(memory_space=pl.ANY)],
        out_specs=[pl.BlockSpec((1, N_QROWS, LANES), lambda b, pt: (b, 0, 0)),
                   pl.BlockSpec((1, n_prob, LANES), lambda b, pt: (b, 0, 0))],
        scratch_shapes=[pltpu.VMEM((2, CMP_PAGES, 2 * LANES, PAGE_SIZE), F32),
                        pltpu.SemaphoreType.DMA((2,)),
                        pltpu.VMEM((n_cmp, LANES), BF16),
                        pltpu.VMEM((n_cmp, LANES), BF16)],
    )
    return pl.pallas_call(
        functools.partial(_nsa_cmp_body, n_pages=n_pages, past_len=past_len),
        grid_spec=grid_spec,
        out_shape=[jax.ShapeDtypeStruct((dec_b, N_QROWS, LANES), F32),
                   jax.ShapeDtypeStruct((dec_b, n_prob, LANES), I32)],
        compiler_params=pltpu.CompilerParams(dimension_semantics=("arbitrary",),
                                             vmem_limit_bytes=VMEM_LIMIT),
        name="nsa_sample_cmp",
    )(page_table.reshape(-1), rel_bias, nq_s, pool_w2, cache3)


def _nsa_sel_body(pt_ref, ids_ref, rb_ref, nq_ref, kvn_ref, winc_ref, winn_ref, zg_ref, oc_ref, idv_ref,
                  cache_ref, out_ref, buf, sem, *, n_pages, past_len):
    b = pl.program_id(0)
    nb = pl.num_programs(0)
    hi = lax.Precision.HIGHEST

    def block_copy(bb, n, slot):
        blk = ids_ref[bb * N_FETCH + n]
        page = pt_ref[bb * n_pages + blk // 2]
        return pltpu.make_async_copy(cache_ref.at[page, pl.ds(2 * LANES, 2 * LANES), :], buf.at[slot, n],
                                     sem.at[slot])

    def start_all(bb, slot):
        def go(n, carry):
            block_copy(bb, n, slot).start()
            return carry
        lax.fori_loop(0, N_FETCH, go, 0)

    def wait_all(bb, slot):
        def go(n, carry):
            block_copy(bb, n, slot).wait()
            return carry
        lax.fori_loop(0, N_FETCH, go, 0)

    slot = b % 2

    @pl.when(b == 0)
    def _prologue():
        start_all(0, 0)

    @pl.when(b + 1 < nb)
    def _prefetch():
        start_all(b + 1, 1 - slot)

    qbd = _block_diag_queries(nq_ref[0], SEQ_PAD)
    tok, head = _sample_row_ids()
    owner = (head // GQ) * N_NEW + tok

    wc = winc_ref[0]
    wn = winn_ref[0]
    r_i = lax.broadcasted_iota(I32, (N_QROWS, WINDOW), 1)
    dist1 = WINDOW + tok - r_i
    s1 = _dot(qbd, wc[0:LANES, :].astype(BF16)) + _sample_bias(dist1, rb_ref)
    s1 = jnp.where((dist1 >= 0) & (dist1 <= WINDOW), s1, NEG_INF)
    n_i = lax.broadcasted_iota(I32, (N_QROWS, SEQ_PAD), 1) - (SEQ_PAD - N_NEW)
    dist2 = tok - n_i
    new_ok = (n_i >= 0) & (dist2 >= 0)
    bias2 = _sample_bias(dist2, rb_ref)
    s2 = jnp.where(new_ok, _dot_nt(qbd, wn[:, 0:LANES].astype(BF16)) + bias2, NEG_INF)
    p1, p2 = _softmax_two(s1, s2)
    o_w = (_dot_nt(p1.astype(BF16), wc[LANES:2 * LANES, :].astype(BF16))
           + _dot(p2.astype(BF16), wn[:, LANES:2 * LANES].astype(BF16)))

    wait_all(b, slot)

    def owner_tiles(o, r0):
        return jnp.concatenate([buf[slot, o * N_PICK + k, r0:r0 + LANES, :] for k in range(N_PICK)],
                               axis=1).astype(BF16)

    s_f = jnp.full((N_QROWS, OWNER_KEYS), NEG_INF, F32)
    for o in range(N_OWNERS):
        s_f = jnp.where(owner == o, _dot(qbd, owner_tiles(o, 0)), s_f)
    spread = (lax.broadcasted_iota(I32, (LANES, OWNER_KEYS), 0)
              == lax.broadcasted_iota(I32, (LANES, OWNER_KEYS), 1) // PAGE_SIZE).astype(F32)
    blk_of_key = _dot(idv_ref[0].astype(F32), spread, hi)
    blk_rows = jnp.concatenate([blk_of_key[(h // GQ) * SEQ_PAD:(h // GQ + 1) * SEQ_PAD]
                                for h in range(N_NSA_HEADS)], axis=0).astype(I32)
    blocks_per_page = PAGE_SIZE // SEL_BLOCK
    in_page = lax.broadcasted_iota(I32, (N_QROWS, OWNER_KEYS), 1) % PAGE_SIZE
    in_block = in_page // SEL_BLOCK == blk_rows % blocks_per_page
    dist_s = past_len + tok - ((blk_rows // blocks_per_page) * PAGE_SIZE + in_page)
    s_f = jnp.where((tok >= 0) & in_block & (dist_s >= 0), s_f + _sample_bias(dist_s, rb_ref), NEG_INF)
    kvn = kvn_ref[0]
    s_n = jnp.where(new_ok, _dot_nt(qbd, kvn[:, 2 * LANES:3 * LANES].astype(BF16)) + bias2, NEG_INF)
    p_f, p_n = _softmax_two(s_f, s_n)
    o_s = _dot(p_n.astype(BF16), kvn[:, 3 * LANES:4 * LANES].astype(BF16))
    for o in range(N_OWNERS):
        o_s = o_s + _dot_nt(jnp.where(owner == o, p_f, 0.0).astype(BF16), owner_tiles(o, LANES))

    o_c = oc_ref[0]
    gates = jax.nn.sigmoid(zg_ref[0])
    is_new = lax.broadcasted_iota(I32, (SEQ_PAD, HEAD_DIM), 0) >= SEQ_PAD - N_NEW
    for h in range(N_NSA_HEADS):
        mix = _gated_mix(gates, o_c, o_s, o_w, h, SEQ_PAD)
        out_ref[0, :, h * HEAD_DIM:(h + 1) * HEAD_DIM] = jnp.where(is_new, mix, 0.0)


def _nsa_sel(page_table, ids, ids_rows, rel_bias, nq_s, kv_s, win_cache, win_s, zg_s, o_c, cache3):
    dec_b, n_pages = page_table.shape
    past_len = n_pages * PAGE_SIZE
    seq = lambda w: pl.BlockSpec((1, SEQ_PAD, w), lambda b, pt, ix: (b, 0, 0))
    grid_spec = pltpu.PrefetchScalarGridSpec(
        num_scalar_prefetch=2,
        grid=(dec_b,),
        in_specs=[pl.BlockSpec(memory_space=pltpu.SMEM),
                  seq(NSA_W), seq(4 * KV_W),
                  pl.BlockSpec((1, 2 * KV_W, WINDOW), lambda b, pt, ix: (b, 0, 0)),
                  seq(2 * KV_W), seq(LANES),
                  pl.BlockSpec((1, N_QROWS, LANES), lambda b, pt, ix: (b, 0, 0)),
                  pl.BlockSpec((1, N_KV_HEADS * SEQ_PAD, LANES), lambda b, pt, ix: (b, 0, 0)),
                  pl.BlockSpec(memory_space=pl.ANY)],
        out_specs=seq(NSA_W),
        scratch_shapes=[pltpu.VMEM((2, N_FETCH, 2 * LANES, PAGE_SIZE), F32),
                        pltpu.SemaphoreType.DMA((2,))],
    )
    return pl.pallas_call(
        functools.partial(_nsa_sel_body, n_pages=n_pages, past_len=past_len),
        grid_spec=grid_spec,
        out_shape=jax.ShapeDtypeStruct((dec_b, SEQ_PAD, NSA_W), F32),
        compiler_params=pltpu.CompilerParams(dimension_semantics=("arbitrary",),
                                             vmem_limit_bytes=VMEM_LIMIT),
        name="nsa_sample_sel",
    )(page_table.reshape(-1), ids.reshape(-1), rel_bias, nq_s, kv_s, win_cache, win_s, zg_s, o_c, ids_rows, cache3)


FF_CHUNK = 896


def _ffn_body(*refs, tm, has_override):
    if has_override:
        (x_ref, mm_ref, mn_ref, ov_ref, woa_ref, wob_ref, l1w_ref, l1b_ref, wup_ref, bup_ref, cw_ref, cb_ref,
         wdn_ref, bdn_ref, l2w_ref, l2b_ref, y_ref, tail_ref, carry_sc, ubuf) = refs
    else:
        (x_ref, mm_ref, mn_ref, woa_ref, wob_ref, l1w_ref, l1b_ref, wup_ref, bup_ref, cw_ref, cb_ref,
         wdn_ref, bdn_ref, l2w_ref, l2b_ref, y_ref, tail_ref, carry_sc, ubuf) = refs
    j = pl.program_id(1)

    @pl.when(j == 0)
    def _init():
        carry_sc[...] = jnp.zeros(carry_sc.shape, F32)

    x = x_ref[0]
    h = ALPHA * x + (_dot(mm_ref[0].astype(BF16), woa_ref[...]) + _dot(mn_ref[0].astype(BF16), wob_ref[...]))
    x1 = _layer_norm(h, l1w_ref[...], l1b_ref[...])
    x1b = x1.astype(BF16)
    if has_override:
        state_row = lax.broadcasted_iota(I32, (tm, FF_CHUNK), 0) % SEQ_PAD < SEQ_PAD - N_NEW

    def conv_half(c0):
        cols = slice(c0, c0 + FF_CHUNK)
        u = _dot(x1b, wup_ref[:, cols]) + bup_ref[:, cols]
        if has_override:
            u = jnp.where(state_row, ov_ref[0, :, cols], u)
        ubuf[0:8, :] = carry_sc[:, cols]
        ubuf[8:8 + tm, :] = u
        out = cb_ref[:, cols] + ubuf[6:6 + tm, :] * cw_ref[0:1, cols]
        out = out + ubuf[7:7 + tm, :] * cw_ref[1:2, cols]
        out = out + u * cw_ref[2:3, cols]
        carry_sc[:, cols] = u[tm - 8:tm, :]
        tail_ref[0, 0, :, cols] = u if has_override else u[tm - 8:tm, :]
        return out

    ff = jnp.zeros((tm, D_MODEL), F32)
    for c in range(D_FF // FF_CHUNK):
        ga = conv_half(c * FF_CHUNK)
        gb = conv_half(D_FF + c * FF_CHUNK)
        gelu = ga * (0.5 * (1.0 + jnp.tanh(math.sqrt(2.0 / math.pi) * (ga + 0.044715 * (ga * ga * ga)))))
        ff = ff + _dot((gelu * gb).astype(BF16), wdn_ref[c * FF_CHUNK:(c + 1) * FF_CHUNK, :])
    ff = ff + bdn_ref[...]
    y_ref[0] = _layer_norm(ALPHA * x1 + ff, l2w_ref[...], l2b_ref[...])


def _ffn(x, mix_m, mix_n, override, wts):
    bsz, t_len, _ = x.shape
    tm = min(t_len, 256)
    nt = t_len // tm
    has_override = override is not None
    tail_rows = tm if has_override else 8
    row = lambda w: pl.BlockSpec((1, tm, w), lambda b, j: (b, j, 0))
    const = lambda shape: pl.BlockSpec(shape, lambda b, j: (0,) * len(shape), pipeline_mode=pl.Buffered(1))
    in_specs = [row(D_MODEL), row(MLSTM_W), row(NSA_W)]
    args = [x, mix_m, mix_n]
    if has_override:
        in_specs.append(row(2 * D_FF))
        args.append(override)
    in_specs += [const(w.shape) for w in wts]
    args += list(wts)
    y, tail = pl.pallas_call(
        functools.partial(_ffn_body, tm=tm, has_override=has_override),
        grid=(bsz, nt),
        in_specs=in_specs,
        out_specs=[row(D_MODEL), pl.BlockSpec((1, 1, tail_rows, 2 * D_FF), lambda b, j: (b, j, 0, 0))],
        out_shape=[jax.ShapeDtypeStruct((bsz, t_len, D_MODEL), F32),
                   jax.ShapeDtypeStruct((bsz, nt, tail_rows, 2 * D_FF), F32)],
        scratch_shapes=[pltpu.VMEM((8, 2 * D_FF), F32), pltpu.VMEM((8 + tm, FF_CHUNK), F32)],
        compiler_params=pltpu.CompilerParams(dimension_semantics=("arbitrary", "arbitrary"),
                                             vmem_limit_bytes=VMEM_LIMIT),
        name="ffn",
    )(*args)
    return y, tail


def _permute_in_proj(w_in, b_in):
    gates_a = slice(2 * MLSTM_W + 2 * MLSTM_W, 2 * MLSTM_W + 2 * MLSTM_W + 2 * N_MLSTM_HEADS)
    nsa0 = gates_a.stop
    nsa1 = nsa0 + NSA_W + 6 * KV_W

    def perm(a):
        pad = IN_PAD - a.shape[-1]
        parts = [a[..., :gates_a.start], a[..., nsa0:nsa1], a[..., gates_a], a[..., nsa1:]]
        parts.append(jnp.zeros(a.shape[:-1] + (pad,), a.dtype))
        return jnp.concatenate(parts, axis=-1)

    return perm(w_in).astype(BF16), perm(b_in).reshape(1, IN_PAD)


def _front_pad(a, n):
    return jnp.pad(a, ((0, 0), (n, 0), (0, 0)))


def kernel(x_prompt, x_sample, cache_nsa_kv, cache_win_kv, state_mlstm_c, state_mlstm_n, state_mlstm_m,
           state_mlstm_conv, state_ffn_conv, page_table,
           w_in, b_in, mlstm_conv_w, mlstm_conv_b, mlstm_norm_w, nsa_pool_w, rel_bias, w_out,
           ln1_w, ln1_b, w_up, b_up, ffn_conv_w, ffn_conv_b, w_down, b_down, ln2_w, ln2_b):
    bsz, t_len, _ = x_prompt.shape
    dec_b, n_new, _ = x_sample.shape
    assert n_new == N_NEW and cache_win_kv.shape[1] == WINDOW

    w_p, b_p = _permute_in_proj(w_in, b_in)
    pool_w2 = jnp.concatenate([jnp.repeat(nsa_pool_w[0], HEAD_DIM, axis=-1),
                               jnp.repeat(nsa_pool_w[1], HEAD_DIM, axis=-1)], axis=-1)
    row = lambda v: v.reshape(1, -1)
    ffn_w = (w_out[:MLSTM_W].astype(BF16), w_out[MLSTM_W:].astype(BF16), row(ln1_w), row(ln1_b),
             w_up.astype(BF16), row(b_up), ffn_conv_w, row(ffn_conv_b), w_down.astype(BF16), row(b_down),
             row(ln2_w), row(ln2_b))

    zqk, zv, zo, nq, kv, win, zg = _proj_in(x_prompt.reshape(bsz * t_len, D_MODEL), w_p, b_p)
    seq3 = lambda a: a.reshape(bsz, t_len, a.shape[-1])
    zqk, zv, zo, nq, kv, win, zg = map(seq3, (zqk, zv, zo, nq, kv, win, zg))
    mix_m, c_p, n_p, m_p = _mlstm(zqk, zv, zo, zg, None, mlstm_conv_w, mlstm_conv_b, mlstm_norm_w, 0)
    mix_n = _nsa_prompt(nq, kv, win, zg, pool_w2, rel_bias)
    y_p, tail_p = _ffn(x_prompt, mix_m, mix_n, None, ffn_w)
    kv_p = kv.reshape(bsz, t_len, 4, N_KV_HEADS, HEAD_DIM)
    win_p = win[:, t_len - min(WINDOW, t_len):].reshape(bsz, -1, 2, N_KV_HEADS, HEAD_DIM)
    mconv_p = zqk[:, t_len - (MLSTM_CONV - 1):]
    fconv_p = tail_p[:, -1, 8 - (FFN_CONV - 1):]

    pad = SEQ_PAD - N_NEW
    xs = _front_pad(x_sample, pad)
    szqk, szv, szo, snq, skv, swin, szg = _proj_in(xs.reshape(dec_b * SEQ_PAD, D_MODEL), w_p, b_p)
    sseq = lambda a: a.reshape(dec_b, SEQ_PAD, a.shape[-1])
    szqk, szv, szo, snq, skv, swin, szg = map(sseq, (szqk, szv, szo, snq, skv, swin, szg))
    lead = MLSTM_CHUNK - N_NEW
    new = lambda a: a[:, pad:]
    m_qk = jnp.concatenate([jnp.zeros((dec_b, lead - (MLSTM_CONV - 1), 2 * MLSTM_W), F32),
                            state_mlstm_conv.astype(F32), new(szqk)], axis=1)
    mix_ms, c_s, n_s, m_s = _mlstm(m_qk, _front_pad(new(szv), lead), _front_pad(new(szo), lead),
                                   _front_pad(new(szg), lead),
                                   (state_mlstm_c, state_mlstm_n, state_mlstm_m),
                                   mlstm_conv_w, mlstm_conv_b, mlstm_norm_w, lead)
    mix_ms = mix_ms[:, MLSTM_CHUNK - SEQ_PAD:]
    cache3 = jnp.transpose(cache_nsa_kv, (0, 2, 3, 4, 1)).reshape(cache_nsa_kv.shape[0], 4 * KV_W, PAGE_SIZE)
    o_c, ids_rows = _nsa_cmp(page_table, rel_bias, snq, pool_w2, cache3)
    ids = ids_rows.reshape(dec_b, N_KV_HEADS, SEQ_PAD, LANES)[:, :, pad:, :N_PICK]
    win_cache = jnp.transpose(cache_win_kv, (0, 2, 3, 4, 1)).reshape(dec_b, 2 * KV_W, WINDOW)
    mix_ns = _nsa_sel(page_table, ids, ids_rows, rel_bias, snq, skv, win_cache, swin, szg, o_c, cache3)
    override = jnp.concatenate([jnp.zeros((dec_b, pad - (FFN_CONV - 1), 2 * D_FF), F32),
                                state_ffn_conv.astype(F32),
                                jnp.zeros((dec_b, N_NEW, 2 * D_FF), F32)], axis=1)
    flat = lambda a: a.reshape(1, dec_b * SEQ_PAD, a.shape[-1])
    y_s8, u_s = _ffn(flat(xs), flat(mix_ms), flat(mix_ns), flat(override), ffn_w)
    y_s = y_s8.reshape(dec_b, SEQ_PAD, D_MODEL)[:, pad:]
    kv_s = new(skv).reshape(dec_b, N_NEW, 4, N_KV_HEADS, HEAD_DIM)
    win_s = jnp.concatenate([cache_win_kv[:, N_NEW:], new(swin).reshape(dec_b, N_NEW, 2, N_KV_HEADS, HEAD_DIM)],
                            axis=1)
    mconv_s = szqk[:, SEQ_PAD - (MLSTM_CONV - 1):]
    fconv_s = u_s.reshape(dec_b, SEQ_PAD, 2 * D_FF)[:, SEQ_PAD - (FFN_CONV - 1):]
    return (y_p, y_s, kv_p, kv_s, win_p, win_s, c_p, c_s, n_p, n_s, m_p, m_s, mconv_p, mconv_s, fconv_p, fconv_s)
```

```python
import functools
import math

import jax
import jax.numpy as jnp
from jax import lax
from jax.experimental import pallas as pl
from jax.experimental.pallas import tpu as pltpu

F32 = jnp.float32
BF16 = jnp.bfloat16
I32 = jnp.int32

D_MODEL = 1024
HEAD_DIM = 64
N_MLSTM_HEADS = 8
N_NSA_HEADS = 8
N_KV_HEADS = 2
GQ = N_NSA_HEADS // N_KV_HEADS
MLSTM_W = N_MLSTM_HEADS * HEAD_DIM
NSA_W = N_NSA_HEADS * HEAD_DIM
KV_W = N_KV_HEADS * HEAD_DIM
MLSTM_CHUNK = 64
MLSTM_CONV = 4
CMP_BLOCK = 32
SEL_BLOCK = 64
SEL_TOPK = 16
WINDOW = 512
NSA_QBLOCK = 64
N_BUCKETS = 32
MAX_DISTANCE = 128
D_FF = 2688
FFN_CONV = 3
DEPTH = 1
ALPHA = (2.0 * DEPTH) ** 0.25
LN_EPS = 1e-5
PAGE_SIZE = 128

LANES = 128
VMEM_LIMIT = 56 * 1024 * 1024
NEG_INF = float("-inf")

SEQ_PAD = 8
N_NEW = 4
N_QROWS = N_NSA_HEADS * SEQ_PAD


def _bucket_thresholds():
    max_exact = N_BUCKETS // 2
    thr = list(range(max_exact + 1))
    for k in range(1, N_BUCKETS - max_exact):
        thr.append(int(math.ceil(max_exact * (MAX_DISTANCE / max_exact) ** (k / (N_BUCKETS - max_exact)))))
    return tuple(thr)


BUCKET_THR = _bucket_thresholds()
FAR_DIST = BUCKET_THR[N_BUCKETS - 1]


def _dot(a, b, precision=None):
    return jnp.dot(a, b, preferred_element_type=F32, precision=precision)


def _dot_nt(a, b, precision=None):
    return lax.dot_general(a, b, (((1,), (1,)), ((), ())), preferred_element_type=F32, precision=precision)


def _dot_tn(a, b):
    return lax.dot_general(a, b, (((0,), (0,)), ((), ())), preferred_element_type=F32)


def _layer_norm(x, w, b):
    mu = jnp.mean(x, -1, keepdims=True)
    xc = x - mu
    var = jnp.mean(xc * xc, -1, keepdims=True)
    return xc * lax.rsqrt(var + LN_EPS) * w + b


def _log_sigmoid(x):
    return jnp.minimum(x, 0.0) - jnp.log1p(jnp.exp(-jnp.abs(x)))


def _bias_from_dist(dist, rb_ref, head):
    acc = jnp.full(dist.shape, rb_ref[0, head], F32)
    for k in range(1, N_BUCKETS):
        acc = jnp.where(dist >= BUCKET_THR[k], rb_ref[k, head], acc)
    return acc


def _masked_softmax(s):
    mx = jnp.max(s, -1, keepdims=True)
    mx = jnp.where(mx > NEG_INF, mx, 0.0)
    e = jnp.exp(s - mx)
    den = jnp.sum(e, -1, keepdims=True)
    return e / jnp.maximum(den, 1e-30)


_C_QK, _C_V, _C_O, _C_NQ, _C_KV, _C_WIN, _C_G = 0, 1024, 1536, 2048, 2560, 3072, 3328
IN_PAD = 3456
G_IGATE, G_FGATE, G_NSA = 0, N_MLSTM_HEADS, 2 * N_MLSTM_HEADS


def _proj_in_body(x_ref, w_ref, b_ref, qk_ref, v_ref, o_ref, nq_ref, kv_ref, win_ref, g_ref):
    z = _dot(x_ref[...].astype(BF16), w_ref[...]) + b_ref[...]
    qk_ref[...] = z[:, _C_QK:_C_V]
    v_ref[...] = z[:, _C_V:_C_O]
    o_ref[...] = z[:, _C_O:_C_NQ]
    nq_ref[...] = z[:, _C_NQ:_C_KV]
    kv_ref[...] = z[:, _C_KV:_C_WIN]
    win_ref[...] = z[:, _C_WIN:_C_G]
    g_ref[...] = z[:, _C_G:IN_PAD]


def _proj_in(x2d, w_p, b_p):
    rows = x2d.shape[0]
    tm = min(rows, 256)
    widths = (1024, 512, 512, 512, 512, 256, 128)
    const = lambda i: (0, 0)
    return pl.pallas_call(
        _proj_in_body,
        grid=(rows // tm,),
        in_specs=[pl.BlockSpec((tm, D_MODEL), lambda i: (i, 0)),
                  pl.BlockSpec((D_MODEL, IN_PAD), const, pipeline_mode=pl.Buffered(1)),
                  pl.BlockSpec((1, IN_PAD), const, pipeline_mode=pl.Buffered(1))],
        out_specs=[pl.BlockSpec((tm, w), lambda i: (i, 0)) for w in widths],
        out_shape=[jax.ShapeDtypeStruct((rows, w), F32) for w in widths],
        compiler_params=pltpu.CompilerParams(dimension_semantics=("arbitrary",),
                                             vmem_limit_bytes=VMEM_LIMIT),
        name="proj_in",
    )(x2d, w_p, b_p)


N_PAIRS = N_MLSTM_HEADS // 2
GATE_LANE0 = N_MLSTM_HEADS
MLSTM_SEQS_PER_STEP = 4


def _dot_split(x, onehot_bf16, pieces):
    acc = None
    rem = x
    for n in range(pieces):
        part = rem.astype(BF16)
        d = _dot(part, onehot_bf16)
        acc = d if acc is None else acc + d
        if n + 1 < pieces:
            rem = rem - part.astype(F32)
    return acc


def _mlstm_body(*refs, tb, nbg, n_dummy, has_state):
    if has_state:
        (zqk_ref, zv_ref, zo_ref, zg_ref, c0_ref, n0_ref, m0_ref, cw_ref, cb_ref, nw_ref,
         mix_ref, c_out, n_out, m_out, ubuf, qk_sc, ct_sc, n_sc, m_sc) = refs
    else:
        (zqk_ref, zv_ref, zo_ref, zg_ref, cw_ref, cb_ref, nw_ref,
         mix_ref, c_out, n_out, m_out, ubuf, qk_sc, ct_sc, n_sc, m_sc) = refs
    L = MLSTM_CHUNK
    D = HEAD_DIM
    j = pl.program_id(1)
    hi = lax.Precision.HIGHEST
    head_lanes = slice(GATE_LANE0, GATE_LANE0 + N_MLSTM_HEADS)
    pairs = [(bi, p) for bi in range(nbg) for p in range(N_PAIRS)]

    @pl.when(j == 0)
    def _init():
        ubuf[:, 0:8, :] = jnp.zeros((nbg, 8, 2 * MLSTM_W), F32)
        ct_sc[...] = jnp.zeros(ct_sc.shape, F32)
        n_sc[...] = jnp.zeros(n_sc.shape, F32)
        m_sc[...] = jnp.zeros(m_sc.shape, F32)
        if has_state:
            for bi in range(nbg):
                for h in range(N_MLSTM_HEADS):
                    p, hh = divmod(h, 2)
                    hs = slice(hh * D, (hh + 1) * D)
                    ct_sc[bi, p, hs, hs] = c0_ref[bi, h]
                    n_sc[bi, p, :, hs] = n0_ref[bi, h:h + 1, :]
                m_sc[bi, :, head_lanes] = m0_ref[bi]

    for bi in range(nbg):
        u = zqk_ref[bi]
        ubuf[bi, 8:8 + tb, :] = u
        conv = cb_ref[...] + ubuf[bi, 5:5 + tb, :] * cw_ref[0:1, :]
        conv = conv + ubuf[bi, 6:6 + tb, :] * cw_ref[1:2, :]
        conv = conv + ubuf[bi, 7:7 + tb, :] * cw_ref[2:3, :]
        conv = conv + u * cw_ref[3:4, :]
        ubuf[bi, 5:8, :] = u[tb - 3:tb, :]
        qk_sc[bi] = conv * jax.nn.sigmoid(conv)

    tri = (lax.broadcasted_iota(I32, (L, L), 1) <= lax.broadcasted_iota(I32, (L, L), 0)).astype(F32)
    row_l = lax.broadcasted_iota(I32, (L, LANES), 0)
    lane = lax.broadcasted_iota(I32, (L, LANES), 1)
    low = lane < D
    key_tok = lane % D
    causal_pair = key_tok <= row_l
    eye_pair = key_tok == row_l
    spread = (lax.broadcasted_iota(I32, (LANES, MLSTM_W), 0) - GATE_LANE0
              == lax.broadcasted_iota(I32, (LANES, MLSTM_W), 1) // D).astype(BF16)
    same_head = (lax.broadcasted_iota(I32, (LANES, LANES), 0) // D
                 == lax.broadcasted_iota(I32, (LANES, LANES), 1) // D)
    ones_bd = same_head.astype(BF16)
    slab_rows = 3 * L + 8

    def chunk(ci, carry):
        r0 = pl.multiple_of(ci * L, L)
        rows = pl.ds(r0, L)
        gates = []
        for bi in range(nbg):
            g = zg_ref[bi, rows, :]
            if n_dummy:
                dummy = (row_l + ci * L) < n_dummy
                g = jnp.where(dummy, jnp.where(lane < G_FGATE, -1e30, 1e30), g)
            gates.append(g)
        b_all = _dot(tri, jnp.concatenate([_log_sigmoid(g) for g in gates], axis=1), hi)
        stacks = []
        for bi in range(nbg):
            b = b_all[:, bi * LANES:(bi + 1) * LANES]
            r = pltpu.roll(gates[bi], GATE_LANE0 - G_IGATE, 1) - b
            cm = r
            for sh in (1, 2, 4, 8, 16, 32):
                cm = jnp.where(row_l >= sh, jnp.maximum(cm, pltpu.roll(cm, sh, 0)), cm)
            m_st = m_sc[bi]
            mx = jnp.maximum(m_st, cm)
            mx_end = mx[L - 1:L, :]
            m_sc[bi] = b[L - 1:L, :] + mx_end
            stacks += [mx, r, b + mx, m_st, mx_end, jnp.zeros((6, LANES), F32)]
        x_all = _dot_split(jnp.concatenate(stacks, axis=0), spread, 2)

        work = []
        den_terms = []
        for bi, p in pairs:
            cs = slice(p * LANES, (p + 1) * LANES)
            x0 = bi * slab_rows
            mx_p, rs_p, mt_p = x_all[x0:x0 + L, cs], x_all[x0 + L:x0 + 2 * L, cs], x_all[x0 + 2 * L:x0 + 3 * L, cs]
            mst_p, mxe_p = x_all[x0 + 3 * L:x0 + 3 * L + 1, cs], x_all[x0 + 3 * L + 1:x0 + 3 * L + 2, cs]
            wi_p = jnp.exp(mst_p - mx_p)
            fl_p = jnp.exp(-mt_p)
            wk_p = jnp.exp(rs_p - mxe_p)
            sc_p = jnp.exp(mst_p - mxe_p)
            r_row = jnp.sum(jnp.where(eye_pair, rs_p, 0.0), axis=0, keepdims=True)
            w_intra = jnp.exp(jnp.where(causal_pair, r_row - mx_p, NEG_INF))
            q_p = qk_sc[bi, rows, cs]
            k_p = qk_sc[bi, rows, MLSTM_W + p * LANES:MLSTM_W + (p + 1) * LANES] * (D ** -0.5)
            v_p = zv_ref[bi, rows, cs]
            qb = q_p.astype(BF16)
            k_bd = jnp.concatenate([jnp.where(low, k_p, 0.0), jnp.where(low, 0.0, k_p)], axis=0)
            v_bd = jnp.concatenate([jnp.where(low, v_p, 0.0), jnp.where(low, 0.0, v_p)], axis=0).astype(BF16)
            c_t = ct_sc[bi, p]
            both = _dot_nt(qb, jnp.concatenate([k_bd, c_t], axis=0).astype(BF16))
            a = w_intra * both[:, 0:LANES]
            num = _dot(a.astype(BF16), v_bd) + both[:, LANES:2 * LANES] * wi_p
            den_terms.append(a + wi_p * (q_p * n_sc[bi, p]))
            work.append((num, fl_p, k_p * wk_p, v_p, sc_p, c_t))
        den_all = _dot_split(jnp.concatenate(den_terms, axis=0), ones_bd, 2)

        hhs = []
        for n, (num, fl_p, _, _, _, _) in enumerate(work):
            hhs.append(num / jnp.maximum(jnp.abs(den_all[n * L:(n + 1) * L, :]), fl_p))
        mu_all = _dot_split(jnp.concatenate(hhs, axis=0), ones_bd, 1) * (1.0 / D)
        hcs = [hh - mu_all[n * L:(n + 1) * L, :] for n, hh in enumerate(hhs)]
        var_all = _dot_split(jnp.concatenate([hc * hc for hc in hcs], axis=0), ones_bd, 1) * (1.0 / D)
        for n, (bi, p) in enumerate(pairs):
            cs = slice(p * LANES, (p + 1) * LANES)
            hn = hcs[n] * lax.rsqrt(var_all[n * L:(n + 1) * L, :] + LN_EPS)
            og = jax.nn.sigmoid(zo_ref[bi, rows, cs])
            mix_ref[bi, rows, cs] = og * (hn * nw_ref[:, cs])
            _, _, kw, v_p, sc_p, c_t = work[n]
            upd = _dot_tn(v_p.astype(BF16), kw.astype(BF16))
            ct_sc[bi, p] = sc_p * c_t + jnp.where(same_head, upd, 0.0)
            n_sc[bi, p] = sc_p * n_sc[bi, p] + jnp.sum(kw, axis=0, keepdims=True)
        return carry

    lax.fori_loop(0, tb // L, chunk, 0)

    @pl.when(j == pl.num_programs(1) - 1)
    def _final():
        for bi in range(nbg):
            for h in range(N_MLSTM_HEADS):
                p, hh = divmod(h, 2)
                hs = slice(hh * D, (hh + 1) * D)
                c_out[bi, h] = ct_sc[bi, p, hs, hs]
                n_out[bi, h:h + 1, :] = n_sc[bi, p, :, hs]
            m_out[bi] = m_sc[bi, :, head_lanes]


def _mlstm(zqk, zv, zo, zg, state, conv_w, conv_b, norm_w, n_dummy):
    bsz, t_len, _ = zqk.shape
    tb = min(t_len, 256)
    nbg = MLSTM_SEQS_PER_STEP
    assert bsz % nbg == 0
    has_state = state is not None
    row = lambda w: pl.BlockSpec((nbg, tb, w), lambda b, j: (b, j, 0))
    per_b = lambda shape: pl.BlockSpec((nbg,) + shape, lambda b, j: (b,) + (0,) * len(shape))
    const = lambda shape: pl.BlockSpec(shape, lambda b, j: (0,) * len(shape))
    in_specs = [row(2 * MLSTM_W), row(MLSTM_W), row(MLSTM_W), row(LANES)]
    args = [zqk, zv, zo, zg]
    if has_state:
        c0, n0, m0 = state
        in_specs += [per_b((N_MLSTM_HEADS, HEAD_DIM, HEAD_DIM)), per_b((N_MLSTM_HEADS, HEAD_DIM)),
                     per_b((1, N_MLSTM_HEADS))]
        args += [c0, n0, m0.reshape(bsz, 1, N_MLSTM_HEADS)]
    in_specs += [const((MLSTM_CONV, 2 * MLSTM_W)), const((1, 2 * MLSTM_W)), const((1, MLSTM_W))]
    args += [conv_w, conv_b.reshape(1, -1), norm_w.reshape(1, -1)]
    mix, c_new, n_new, m_new = pl.pallas_call(
        functools.partial(_mlstm_body, tb=tb, nbg=nbg, n_dummy=n_dummy, has_state=has_state),
        grid=(bsz // nbg, t_len // tb),
        in_specs=in_specs,
        out_specs=[row(MLSTM_W), per_b((N_MLSTM_HEADS, HEAD_DIM, HEAD_DIM)),
                   per_b((N_MLSTM_HEADS, HEAD_DIM)), per_b((1, N_MLSTM_HEADS))],
        out_shape=[jax.ShapeDtypeStruct((bsz, t_len, MLSTM_W), F32),
                   jax.ShapeDtypeStruct((bsz, N_MLSTM_HEADS, HEAD_DIM, HEAD_DIM), F32),
                   jax.ShapeDtypeStruct((bsz, N_MLSTM_HEADS, HEAD_DIM), F32),
                   jax.ShapeDtypeStruct((bsz, 1, N_MLSTM_HEADS), F32)],
        scratch_shapes=[pltpu.VMEM((nbg, 8 + tb, 2 * MLSTM_W), F32),
                        pltpu.VMEM((nbg, tb, 2 * MLSTM_W), F32),
                        pltpu.VMEM((nbg, N_PAIRS, LANES, LANES), F32),
                        pltpu.VMEM((nbg, N_PAIRS, 1, LANES), F32),
                        pltpu.VMEM((nbg, 1, LANES), F32)],
        compiler_params=pltpu.CompilerParams(dimension_semantics=("arbitrary", "arbitrary"),
                                             vmem_limit_bytes=VMEM_LIMIT),
        name="mlstm",
    )(*args)
    return mix, c_new, n_new, m_new.reshape(bsz, N_MLSTM_HEADS)


def _block_diag_queries(x, n_rows, heads=tuple(range(N_NSA_HEADS)), scale=HEAD_DIM ** -0.5):
    lane = lax.broadcasted_iota(I32, (n_rows, LANES), 1)
    low = lane < HEAD_DIM
    pieces = []
    for h in heads:
        pair = x[:, (h // 2) * LANES:(h // 2 + 1) * LANES]
        src_low = h % 2 == 0
        dst_low = h // GQ == 0
        if src_low != dst_low:
            pair = pltpu.roll(pair, HEAD_DIM, 1)
        pieces.append(jnp.where(low if dst_low else ~low, pair, 0.0))
    return (jnp.concatenate(pieces, axis=0) * scale).astype(BF16)


def _group_lanes(o, g):
    return o[:, g * HEAD_DIM:(g + 1) * HEAD_DIM]


def _gated_mix(gates, o_c, o_s, o_w, h, n_rows):
    g = h // GQ
    rs = slice(h * n_rows, (h + 1) * n_rows)
    col = lambda branch: gates[:, G_NSA + branch * N_NSA_HEADS + h:G_NSA + branch * N_NSA_HEADS + h + 1]
    return (col(0) * _group_lanes(o_c[rs], g) + col(1) * _group_lanes(o_s[rs], g)
            + col(2) * _group_lanes(o_w[rs], g))


SEL_TILE = 256
SEL_TILE_BLOCKS = SEL_TILE // SEL_BLOCK
WIN_BAND = WINDOW + NSA_QBLOCK
N_COLS = N_NSA_HEADS * NSA_QBLOCK
PROMPT_HEADS = tuple(GQ * (jj % N_KV_HEADS) + jj // N_KV_HEADS for jj in range(N_NSA_HEADS))
SEL_REL0 = 2 * SEL_TILE - SEL_BLOCK
SEL_BIAS_ROWS = SEL_REL0 + SEL_TILE


LOG2E = math.log2(math.e)


def _softmax2_rows(s):
    mx = jnp.max(s, 0, keepdims=True)
    mx = jnp.where(mx > NEG_INF, mx, 0.0)
    e = jnp.exp2(s - mx)
    return e / jnp.maximum(jnp.sum(e, 0, keepdims=True), 1e-30)


def _nsa_prompt_body(rb_ref, nq_ref, kv_ref, win_ref, zg_ref, pw_ref, out_ref,
                     ks_sc, vst_sc, wk_sc, wv_sc, kc_sc, pv_sc, vct_sc, bs_sc, bw_sc, fc_sc,
                     m_sc, l_sc, acc_sc, *, t_len):
    QB = NSA_QBLOCK
    n_cmp = t_len // CMP_BLOCK
    n_sel = t_len // SEL_BLOCK
    i = pl.program_id(1)
    hi = lax.Precision.HIGHEST

    @pl.when(i == 0)
    def _init():
        ks_sc[0:SEL_TILE, :] = jnp.zeros((SEL_TILE, LANES), BF16)
        vst_sc[0] = jnp.zeros((LANES, SEL_TILE), BF16)
        wk_sc[0:WINDOW, :] = jnp.zeros((WINDOW, LANES), BF16)
        wv_sc[0:WINDOW, :] = jnp.zeros((WINDOW, LANES), BF16)
        rows_per = 2 * SEL_TILE
        blocks_per = rows_per // CMP_BLOCK

        def fill(c, carry):
            r0 = pl.multiple_of(c * rows_per, rows_per)
            kv = kv_ref[0, pl.ds(r0, rows_per), :]
            ks_sc[pl.ds(SEL_TILE + r0, rows_per), :] = kv[:, 2 * LANES:3 * LANES].astype(BF16)
            for half in range(2):
                v_t = kv[half * SEL_TILE:(half + 1) * SEL_TILE, 3 * LANES:4 * LANES]
                vst_sc[2 * c + half + 1] = jnp.transpose(v_t).astype(BF16)
            w = win_ref[0, pl.ds(r0, rows_per), :]
            wk_sc[pl.ds(WINDOW + r0, rows_per), :] = w[:, 0:LANES].astype(BF16)
            wv_sc[pl.ds(WINDOW + r0, rows_per), :] = w[:, LANES:2 * LANES].astype(BF16)
            pooled = jnp.sum(kv[:, 0:2 * LANES].reshape(blocks_per, CMP_BLOCK, 2 * LANES) * pw_ref[...][None],
                             axis=1)
            c0 = pl.multiple_of(c * blocks_per, blocks_per)
            kc_sc[pl.ds(c0, blocks_per), :] = pooled[:, 0:LANES].astype(BF16)
            pv_sc[pl.ds(c0, blocks_per), :] = pooled[:, LANES:2 * LANES]
            return carry

        lax.fori_loop(0, t_len // rows_per, fill, 0)
        vct_sc[...] = jnp.transpose(pv_sc[...]).astype(BF16)

        for jj, h in enumerate(PROMPT_HEADS):
            cs = slice(jj * QB, (jj + 1) * QB)
            u_i = lax.broadcasted_iota(I32, (SEL_BIAS_ROWS, QB), 0)
            q_i = lax.broadcasted_iota(I32, (SEL_BIAS_ROWS, QB), 1)
            dist = q_i - (u_i - SEL_REL0)
            near = (_bias_from_dist(dist, rb_ref, h) - rb_ref[N_BUCKETS - 1, h]) * LOG2E
            bs_sc[:, cs] = jnp.where(dist >= 0, near, NEG_INF)
            s_i = lax.broadcasted_iota(I32, (WIN_BAND, QB), 0)
            q_i = lax.broadcasted_iota(I32, (WIN_BAND, QB), 1)
            dist = q_i + WINDOW - s_i
            bw_sc[:, cs] = jnp.where((dist >= 0) & (dist <= WINDOW), _bias_from_dist(dist, rb_ref, h) * LOG2E,
                                     NEG_INF)
            c_i = lax.broadcasted_iota(I32, (n_cmp, QB), 0)
            q_i = lax.broadcasted_iota(I32, (n_cmp, QB), 1)
            rel = jnp.where(c_i < n_cmp // 2, c_i, c_i - n_cmp)
            dist = q_i - CMP_BLOCK * rel - (CMP_BLOCK - 1)
            dist = jnp.where((c_i >= 2) & (c_i < n_cmp // 2), FAR_DIST, dist)
            fc_sc[:, cs] = _bias_from_dist(dist, rb_ref, h) * LOG2E

    qbd = _block_diag_queries(nq_ref[0], QB, PROMPT_HEADS, HEAD_DIM ** -0.5 * LOG2E)

    w0 = pl.multiple_of(i * QB, QB)
    s_w = _dot_nt(wk_sc[pl.ds(w0, WIN_BAND), :], qbd) + bw_sc[...]
    in_seq = lax.broadcasted_iota(I32, (WIN_BAND, N_COLS), 0) >= WINDOW - i * QB
    p_w = _softmax2_rows(jnp.where(in_seq, s_w, NEG_INF))
    o_w = _dot_tn(wv_sc[pl.ds(w0, WIN_BAND), :], p_w.astype(BF16))

    s_c = _dot_nt(kc_sc[...], qbd) + pltpu.roll(fc_sc[...], 2 * i, 0)
    c_i = lax.broadcasted_iota(I32, (n_cmp, N_COLS), 0)
    q_i = lax.broadcasted_iota(I32, (n_cmp, N_COLS), 1) % QB
    p_c = _softmax2_rows(jnp.where(c_i * CMP_BLOCK + (CMP_BLOCK - 1) <= i * QB + q_i, s_c, NEG_INF))
    o_c = _dot(vct_sc[...], p_c.astype(BF16))

    p_sum = p_c[:, 0:LANES]
    for j in range(1, GQ):
        p_sum = p_sum + p_c[:, j * LANES:(j + 1) * LANES]
    ratio = SEL_BLOCK // CMP_BLOCK
    pair = (lax.broadcasted_iota(I32, (n_sel, n_cmp), 1) // ratio
            == lax.broadcasted_iota(I32, (n_sel, n_cmp), 0)).astype(F32)
    imp = _dot(pair, p_sum, hi)
    blk = lax.broadcasted_iota(I32, (n_sel, LANES), 0)
    score = jnp.where(blk == i, jnp.inf, jnp.where(blk < i, imp, NEG_INF))

    def pick(_, carry):
        work, chosen = carry
        best = jnp.max(work, 0, keepdims=True)
        first = jnp.min(jnp.where(work == best, blk, n_sel), 0, keepdims=True)
        hit = blk == first
        return jnp.where(hit, NEG_INF, work), jnp.where(hit, 1.0, chosen)

    _, chosen = lax.fori_loop(0, SEL_TOPK, pick, (score, jnp.zeros((n_sel, LANES), F32)), unroll=True)
    chosen = jnp.where(score > NEG_INF, chosen, 0.0).astype(BF16)

    def tile_scores(slot, bias):
        k_t = ks_sc[pl.ds(pl.multiple_of(slot * SEL_TILE, SEL_TILE), SEL_TILE), :]
        expand = ((slot - 1) * SEL_TILE_BLOCKS + lax.broadcasted_iota(I32, (SEL_TILE, n_sel), 0) // SEL_BLOCK
                  == lax.broadcasted_iota(I32, (SEL_TILE, n_sel), 1))
        picked = _dot(jnp.where(expand, 1.0, 0.0).astype(BF16), chosen)
        picked = jnp.concatenate([picked] * GQ, axis=1) > 0.5
        s = _dot_nt(k_t, qbd)
        return jnp.where(picked, s if bias is None else s + bias, NEG_INF)

    def sel_pair(slot_a, slot_b, bias_a, bias_b, first):
        s_a = tile_scores(slot_a, bias_a)
        s_b = tile_scores(slot_b, bias_b)
        t_max = jnp.maximum(jnp.max(s_a, 0, keepdims=True), jnp.max(s_b, 0, keepdims=True))
        m_new = t_max if first else jnp.maximum(m_sc[...], t_max)
        e_a = jnp.exp2(s_a - m_new)
        e_b = jnp.exp2(s_b - m_new)
        l_new = jnp.sum(e_a, 0, keepdims=True) + jnp.sum(e_b, 0, keepdims=True)
        acc_new = _dot(vst_sc[slot_a], e_a.astype(BF16)) + _dot(vst_sc[slot_b], e_b.astype(BF16))
        if first:
            l_sc[...] = l_new
            acc_sc[...] = acc_new
        else:
            scale = jnp.exp2(m_sc[...] - m_new)
            l_sc[...] = scale * l_sc[...] + l_new
            acc_sc[...] = scale * acc_sc[...] + acc_new
        m_sc[...] = m_new

    last = i // SEL_TILE_BLOCKS
    u0 = pl.multiple_of(SEL_REL0 - SEL_TILE - SEL_BLOCK * (i % SEL_TILE_BLOCKS), SEL_BLOCK)
    sel_pair(last + 1, last, bs_sc[pl.ds(u0 + SEL_TILE, SEL_TILE), :], bs_sc[pl.ds(u0, SEL_TILE), :], True)

    def far_pair(p, carry):
        slot_b = 2 * p + 2
        sel_pair(2 * p + 1, jnp.where(slot_b < last, slot_b, 0), None, None, False)
        return carry

    lax.fori_loop(0, last // 2, far_pair, 0)
    o_s = acc_sc[...] / jnp.maximum(l_sc[...], 1e-30)

    eye = (lax.broadcasted_iota(I32, (3 * N_NSA_HEADS + 8, LANES), 0) + G_NSA
           == lax.broadcasted_iota(I32, (3 * N_NSA_HEADS + 8, LANES), 1)).astype(F32)
    gates = jax.nn.sigmoid(_dot_nt(eye, zg_ref[0], hi))
    pieces = []
    for h in range(N_NSA_HEADS):
        g = h // GQ
        jj = PROMPT_HEADS.index(h)
        part = lambda o: o[g * HEAD_DIM:(g + 1) * HEAD_DIM, jj * QB:(jj + 1) * QB]
        pieces.append(gates[h:h + 1, :] * part(o_c)
                      + gates[N_NSA_HEADS + h:N_NSA_HEADS + h + 1, :] * part(o_s)
                      + gates[2 * N_NSA_HEADS + h:2 * N_NSA_HEADS + h + 1, :] * part(o_w))
    mix_t = jnp.concatenate(pieces, axis=0)
    mix_t = jnp.concatenate([mix_t, jnp.zeros((NSA_W, LANES - QB), F32)], axis=1)
    out_ref[0] = jnp.transpose(mix_t)[0:QB, :]


def _nsa_prompt(nq, kv, win, zg, pool_w2, rel_bias):
    bsz, t_len, _ = nq.shape
    assert t_len // CMP_BLOCK == LANES and t_len % (2 * SEL_TILE) == 0
    n_cmp = t_len // CMP_BLOCK
    per_b = lambda w: pl.BlockSpec((1, t_len, w), lambda b, i: (b, 0, 0))
    blk = lambda w: pl.BlockSpec((1, NSA_QBLOCK, w), lambda b, i: (b, i, 0))
    return pl.pallas_call(
        functools.partial(_nsa_prompt_body, t_len=t_len),
        grid=(bsz, t_len // NSA_QBLOCK),
        in_specs=[pl.BlockSpec(memory_space=pltpu.SMEM),
                  blk(NSA_W), per_b(4 * KV_W), per_b(2 * KV_W), blk(LANES),
                  pl.BlockSpec((CMP_BLOCK, 2 * LANES), lambda b, i: (0, 0))],
        out_specs=blk(NSA_W),
        out_shape=jax.ShapeDtypeStruct((bsz, t_len, NSA_W), F32),
        scratch_shapes=[pltpu.VMEM((SEL_TILE + t_len, LANES), BF16),
                        pltpu.VMEM((1 + t_len // SEL_TILE, LANES, SEL_TILE), BF16),
                        pltpu.VMEM((WINDOW + t_len, LANES), BF16),
                        pltpu.VMEM((WINDOW + t_len, LANES), BF16),
                        pltpu.VMEM((n_cmp, LANES), BF16),
                        pltpu.VMEM((n_cmp, LANES), F32),
                        pltpu.VMEM((LANES, n_cmp), BF16),
                        pltpu.VMEM((SEL_BIAS_ROWS, N_COLS), F32),
                        pltpu.VMEM((WIN_BAND, N_COLS), F32),
                        pltpu.VMEM((n_cmp, N_COLS), F32),
                        pltpu.VMEM((1, N_COLS), F32),
                        pltpu.VMEM((1, N_COLS), F32),
                        pltpu.VMEM((LANES, N_COLS), F32)],
        compiler_params=pltpu.CompilerParams(dimension_semantics=("arbitrary", "arbitrary"),
                                             vmem_limit_bytes=VMEM_LIMIT),
        name="nsa_prompt",
    )(rel_bias, nq, kv, win, zg, pool_w2)


CMP_PAGES = 16
N_PICK = SEL_TOPK - 1
N_OWNERS = N_KV_HEADS * N_NEW
N_FETCH = N_OWNERS * N_PICK
OWNER_KEYS = N_PICK * PAGE_SIZE


def _sample_row_ids():
    r = lax.broadcasted_iota(I32, (N_QROWS, 1), 0)
    return r % SEQ_PAD - (SEQ_PAD - N_NEW), r // SEQ_PAD


def _sample_bias(dist, rb_ref):
    _, head = _sample_row_ids()
    col = lambda k: sum(jnp.where(head == h, rb_ref[k, h], 0.0) for h in range(N_NSA_HEADS))
    acc = jnp.broadcast_to(col(0), dist.shape)
    for k in range(1, N_BUCKETS):
        acc = jnp.where(dist >= BUCKET_THR[k], col(k), acc)
    return acc


def _softmax_two(s1, s2):
    mx = jnp.maximum(jnp.max(s1, -1, keepdims=True), jnp.max(s2, -1, keepdims=True))
    mx = jnp.where(mx > NEG_INF, mx, 0.0)
    e1 = jnp.exp(s1 - mx)
    e2 = jnp.exp(s2 - mx)
    den = jnp.maximum(jnp.sum(e1, -1, keepdims=True) + jnp.sum(e2, -1, keepdims=True), 1e-30)
    return e1 / den, e2 / den


def _nsa_cmp_body(pt_ref, rb_ref, nq_ref, pw_ref, cache_ref, oc_ref, idx_ref,
                  buf, sem, kc_sc, vc_sc, *, n_pages, past_len):
    b = pl.program_id(0)
    nb = pl.num_programs(0)
    n_chunks = n_pages // CMP_PAGES
    blocks_per = CMP_PAGES * PAGE_SIZE // CMP_BLOCK
    n_cmp = past_len // CMP_BLOCK
    n_sel_past = past_len // SEL_BLOCK
    n_prob = N_KV_HEADS * SEQ_PAD
    hi = lax.Precision.HIGHEST

    def page_copy(bb, chunk, p, slot):
        page = pt_ref[bb * n_pages + chunk * CMP_PAGES + p]
        return pltpu.make_async_copy(cache_ref.at[page, pl.ds(0, 2 * LANES), :], buf.at[slot, p], sem.at[slot])

    def start_chunk(bb, chunk, slot):
        for p in range(CMP_PAGES):
            page_copy(bb, chunk, p, slot).start()

    def wait_chunk(bb, chunk, slot):
        for p in range(CMP_PAGES):
            page_copy(bb, chunk, p, slot).wait()

    @pl.when(b == 0)
    def _prologue():
        start_chunk(0, 0, 0)

    def chunk_step(c, carry):
        slot = (b * n_chunks + c) % 2

        @pl.when(c + 1 < n_chunks)
        def _next_same():
            start_chunk(b, c + 1, 1 - slot)

        @pl.when((c + 1 == n_chunks) & (b + 1 < nb))
        def _next_seq():
            start_chunk(b + 1, 0, 1 - slot)

        wait_chunk(b, c, slot)
        rows = jnp.concatenate([jnp.transpose(buf[slot, p]) for p in range(CMP_PAGES)], axis=0)
        pooled = jnp.sum(rows.reshape(blocks_per, CMP_BLOCK, 2 * LANES) * pw_ref[...][None], axis=1)
        c0 = pl.multiple_of(c * blocks_per, blocks_per)
        kc_sc[pl.ds(c0, blocks_per), :] = pooled[:, 0:LANES].astype(BF16)
        vc_sc[pl.ds(c0, blocks_per), :] = pooled[:, LANES:2 * LANES].astype(BF16)
        return carry

    lax.fori_loop(0, n_chunks, chunk_step, 0)

    qbd = _block_diag_queries(nq_ref[0], SEQ_PAD)
    tok, _ = _sample_row_ids()
    c_i = lax.broadcasted_iota(I32, (N_QROWS, n_cmp), 1)
    dist = past_len + tok - (c_i * CMP_BLOCK + CMP_BLOCK - 1)
    s_c = _dot_nt(qbd, kc_sc[...]) + _sample_bias(dist, rb_ref)
    p_c = _masked_softmax(jnp.where(dist >= 0, s_c, NEG_INF))
    oc_ref[0] = _dot(p_c.astype(BF16), vc_sc[...])

    p_sum = jnp.sum(p_c.reshape(N_KV_HEADS, GQ, SEQ_PAD, n_cmp), axis=1).reshape(n_prob, n_cmp)
    ratio = SEL_BLOCK // CMP_BLOCK
    pair = (lax.broadcasted_iota(I32, (n_cmp, n_sel_past), 0) // ratio
            == lax.broadcasted_iota(I32, (n_cmp, n_sel_past), 1)).astype(F32)
    imp = _dot(p_sum, pair, hi)
    blk = lax.broadcasted_iota(I32, imp.shape, 1)
    slot_i = lax.broadcasted_iota(I32, (n_prob, LANES), 1)

    def pick(k, carry):
        work, ids = carry
        best = jnp.max(work, 1, keepdims=True)
        first = jnp.min(jnp.where(work == best, blk, n_sel_past), 1, keepdims=True)
        return jnp.where(blk == first, NEG_INF, work), jnp.where(slot_i == k, first, ids)

    _, ids = lax.fori_loop(0, N_PICK, pick, (imp, jnp.zeros((n_prob, LANES), I32)), unroll=True)
    idx_ref[0] = ids


def _nsa_cmp(page_table, rel_bias, nq_s, pool_w2, cache3):
    dec_b, n_pages = page_table.shape
    past_len = n_pages * PAGE_SIZE
    assert n_pages % CMP_PAGES == 0 and past_len // SEL_BLOCK >= N_PICK
    n_cmp = past_len // CMP_BLOCK
    n_prob = N_KV_HEADS * SEQ_PAD
    grid_spec = pltpu.PrefetchScalarGridSpec(
        num_scalar_prefetch=1,
        grid=(dec_b,),
        in_specs=[pl.BlockSpec(memory_space=pltpu.SMEM),
                  pl.BlockSpec((1, SEQ_PAD, NSA_W), lambda b, pt: (b, 0, 0)),
                  pl.BlockSpec((CMP_BLOCK, 2 * LANES), lambda b, pt: (0, 0)),
                  pl.BlockSpec(memory_space=pl.ANY)],
        out_specs=[pl.BlockSpec((1, N_QROWS, LANES), lambda b, pt: (b, 0, 0)),
                   pl.BlockSpec((1, n_prob, LANES), lambda b, pt: (b, 0, 0))],
        scratch_shapes=[pltpu.VMEM((2, CMP_PAGES, 2 * LANES, PAGE_SIZE), F32),
                        pltpu.SemaphoreType.DMA((2,)),
                        pltpu.VMEM((n_cmp, LANES), BF16),
                        pltpu.VMEM((n_cmp, LANES), BF16)],
    )
    return pl.pallas_call(
        functools.partial(_nsa_cmp_body, n_pages=n_pages, past_len=past_len),
        grid_spec=grid_spec,
        out_shape=[jax.ShapeDtypeStruct((dec_b, N_QROWS, LANES), F32),
                   jax.ShapeDtypeStruct((dec_b, n_prob, LANES), I32)],
        compiler_params=pltpu.CompilerParams(dimension_semantics=("arbitrary",),
                                             vmem_limit_bytes=VMEM_LIMIT),
        name="nsa_sample_cmp",
    )(page_table.reshape(-1), rel_bias, nq_s, pool_w2, cache3)


def _nsa_sel_body(pt_ref, ids_ref, rb_ref, nq_ref, kvn_ref, winc_ref, winn_ref, zg_ref, oc_ref, idv_ref,
                  cache_ref, out_ref, buf, sem, *, n_pages, past_len):
    b = pl.program_id(0)
    nb = pl.num_programs(0)
    hi = lax.Precision.HIGHEST

    def block_copy(bb, n, slot):
        blk = ids_ref[bb * N_FETCH + n]
        page = pt_ref[bb * n_pages + blk // 2]
        return pltpu.make_async_copy(cache_ref.at[page, pl.ds(2 * LANES, 2 * LANES), :], buf.at[slot, n],
                                     sem.at[slot])

    def start_all(bb, slot):
        def go(n, carry):
            block_copy(bb, n, slot).start()
            return carry
        lax.fori_loop(0, N_FETCH, go, 0)

    def wait_all(bb, slot):
        def go(n, carry):
            block_copy(bb, n, slot).wait()
            return carry
        lax.fori_loop(0, N_FETCH, go, 0)

    slot = b % 2

    @pl.when(b == 0)
    def _prologue():
        start_all(0, 0)

    @pl.when(b + 1 < nb)
    def _prefetch():
        start_all(b + 1, 1 - slot)

    qbd = _block_diag_queries(nq_ref[0], SEQ_PAD)
    tok, head = _sample_row_ids()
    owner = (head // GQ) * N_NEW + tok

    wc = winc_ref[0]
    wn = winn_ref[0]
    r_i = lax.broadcasted_iota(I32, (N_QROWS, WINDOW), 1)
    dist1 = WINDOW + tok - r_i
    s1 = _dot(qbd, wc[0:LANES, :].astype(BF16)) + _sample_bias(dist1, rb_ref)
    s1 = jnp.where((dist1 >= 0) & (dist1 <= WINDOW), s1, NEG_INF)
    n_i = lax.broadcasted_iota(I32, (N_QROWS, SEQ_PAD), 1) - (SEQ_PAD - N_NEW)
    dist2 = tok - n_i
    new_ok = (n_i >= 0) & (dist2 >= 0)
    bias2 = _sample_bias(dist2, rb_ref)
    s2 = jnp.where(new_ok, _dot_nt(qbd, wn[:, 0:LANES].astype(BF16)) + bias2, NEG_INF)
    p1, p2 = _softmax_two(s1, s2)
    o_w = (_dot_nt(p1.astype(BF16), wc[LANES:2 * LANES, :].astype(BF16))
           + _dot(p2.astype(BF16), wn[:, LANES:2 * LANES].astype(BF16)))

    wait_all(b, slot)

    def owner_tiles(o, r0):
        return jnp.concatenate([buf[slot, o * N_PICK + k, r0:r0 + LANES, :] for k in range(N_PICK)],
                               axis=1).astype(BF16)

    s_f = jnp.full((N_QROWS, OWNER_KEYS), NEG_INF, F32)
    for o in range(N_OWNERS):
        s_f = jnp.where(owner == o, _dot(qbd, owner_tiles(o, 0)), s_f)
    spread = (lax.broadcasted_iota(I32, (LANES, OWNER_KEYS), 0)
              == lax.broadcasted_iota(I32, (LANES, OWNER_KEYS), 1) // PAGE_SIZE).astype(F32)
    blk_of_key = _dot(idv_ref[0].astype(F32), spread, hi)
    blk_rows = jnp.concatenate([blk_of_key[(h // GQ) * SEQ_PAD:(h // GQ + 1) * SEQ_PAD]
                                for h in range(N_NSA_HEADS)], axis=0).astype(I32)
    blocks_per_page = PAGE_SIZE // SEL_BLOCK
    in_page = lax.broadcasted_iota(I32, (N_QROWS, OWNER_KEYS), 1) % PAGE_SIZE
    in_block = in_page // SEL_BLOCK == blk_rows % blocks_per_page
    dist_s = past_len + tok - ((blk_rows // blocks_per_page) * PAGE_SIZE + in_page)
    s_f = jnp.where((tok >= 0) & in_block & (dist_s >= 0), s_f + _sample_bias(dist_s, rb_ref), NEG_INF)
    kvn = kvn_ref[0]
    s_n = jnp.where(new_ok, _dot_nt(qbd, kvn[:, 2 * LANES:3 * LANES].astype(BF16)) + bias2, NEG_INF)
    p_f, p_n = _softmax_two(s_f, s_n)
    o_s = _dot(p_n.astype(BF16), kvn[:, 3 * LANES:4 * LANES].astype(BF16))
    for o in range(N_OWNERS):
        o_s = o_s + _dot_nt(jnp.where(owner == o, p_f, 0.0).astype(BF16), owner_tiles(o, LANES))

    o_c = oc_ref[0]
    gates = jax.nn.sigmoid(zg_ref[0])
    is_new = lax.broadcasted_iota(I32, (SEQ_PAD, HEAD_DIM), 0) >= SEQ_PAD - N_NEW
    for h in range(N_NSA_HEADS):
        mix = _gated_mix(gates, o_c, o_s, o_w, h, SEQ_PAD)
        out_ref[0, :, h * HEAD_DIM:(h + 1) * HEAD_DIM] = jnp.where(is_new, mix, 0.0)


def _nsa_sel(page_table, ids, ids_rows, rel_bias, nq_s, kv_s, win_cache, win_s, zg_s, o_c, cache3):
    dec_b, n_pages = page_table.shape
    past_len = n_pages * PAGE_SIZE
    seq = lambda w: pl.BlockSpec((1, SEQ_PAD, w), lambda b, pt, ix: (b, 0, 0))
    grid_spec = pltpu.PrefetchScalarGridSpec(
        num_scalar_prefetch=2,
        grid=(dec_b,),
        in_specs=[pl.BlockSpec(memory_space=pltpu.SMEM),
                  seq(NSA_W), seq(4 * KV_W),
                  pl.BlockSpec((1, 2 * KV_W, WINDOW), lambda b, pt, ix: (b, 0, 0)),
                  seq(2 * KV_W), seq(LANES),
                  pl.BlockSpec((1, N_QROWS, LANES), lambda b, pt, ix: (b, 0, 0)),
                  pl.BlockSpec((1, N_KV_HEADS * SEQ_PAD, LANES), lambda b, pt, ix: (b, 0, 0)),
                  pl.BlockSpec(memory_space=pl.ANY)],
        out_specs=seq(NSA_W),
        scratch_shapes=[pltpu.VMEM((2, N_FETCH, 2 * LANES, PAGE_SIZE), F32),
                        pltpu.SemaphoreType.DMA((2,))],
    )
    return pl.pallas_call(
        functools.partial(_nsa_sel_body, n_pages=n_pages, past_len=past_len),
        grid_spec=grid_spec,
        out_shape=jax.ShapeDtypeStruct((dec_b, SEQ_PAD, NSA_W), F32),
        compiler_params=pltpu.CompilerParams(dimension_semantics=("arbitrary",),
                                             vmem_limit_bytes=VMEM_LIMIT),
        name="nsa_sample_sel",
    )(page_table.reshape(-1), ids.reshape(-1), rel_bias, nq_s, kv_s, win_cache, win_s, zg_s, o_c, ids_rows, cache3)


FF_CHUNK = 896


def _ffn_body(*refs, tm, has_override):
    if has_override:
        (x_ref, mm_ref, mn_ref, ov_ref, woa_ref, wob_ref, l1w_ref, l1b_ref, wup_ref, bup_ref, cw_ref, cb_ref,
         wdn_ref, bdn_ref, l2w_ref, l2b_ref, y_ref, tail_ref, carry_sc, ubuf) = refs
    else:
        (x_ref, mm_ref, mn_ref, woa_ref, wob_ref, l1w_ref, l1b_ref, wup_ref, bup_ref, cw_ref, cb_ref,
         wdn_ref, bdn_ref, l2w_ref, l2b_ref, y_ref, tail_ref, carry_sc, ubuf) = refs
    j = pl.program_id(1)

    @pl.when(j == 0)
    def _init():
        carry_sc[...] = jnp.zeros(carry_sc.shape, F32)

    x = x_ref[0]
    h = ALPHA * x + (_dot(mm_ref[0].astype(BF16), woa_ref[...]) + _dot(mn_ref[0].astype(BF16), wob_ref[...]))
    x1 = _layer_norm(h, l1w_ref[...], l1b_ref[...])
    x1b = x1.astype(BF16)
    if has_override:
        state_row = lax.broadcasted_iota(I32, (tm, FF_CHUNK), 0) % SEQ_PAD < SEQ_PAD - N_NEW

    def conv_half(c0):
        cols = slice(c0, c0 + FF_CHUNK)
        u = _dot(x1b, wup_ref[:, cols]) + bup_ref[:, cols]
        if has_override:
            u = jnp.where(state_row, ov_ref[0, :, cols], u)
        ubuf[0:8, :] = carry_sc[:, cols]
        ubuf[8:8 + tm, :] = u
        out = cb_ref[:, cols] + ubuf[6:6 + tm, :] * cw_ref[0:1, cols]
        out = out + ubuf[7:7 + tm, :] * cw_ref[1:2, cols]
        out = out + u * cw_ref[2:3, cols]
        carry_sc[:, cols] = u[tm - 8:tm, :]
        tail_ref[0, 0, :, cols] = u if has_override else u[tm - 8:tm, :]
        return out

    ff = jnp.zeros((tm, D_MODEL), F32)
    for c in range(D_FF // FF_CHUNK):
        ga = conv_half(c * FF_CHUNK)
        gb = conv_half(D_FF + c * FF_CHUNK)
        gelu = ga * (0.5 * (1.0 + jnp.tanh(math.sqrt(2.0 / math.pi) * (ga + 0.044715 * (ga * ga * ga)))))
        ff = ff + _dot((gelu * gb).astype(BF16), wdn_ref[c * FF_CHUNK:(c + 1) * FF_CHUNK, :])
    ff = ff + bdn_ref[...]
    y_ref[0] = _layer_norm(ALPHA * x1 + ff, l2w_ref[...], l2b_ref[...])


def _ffn(x, mix_m, mix_n, override, wts):
    bsz, t_len, _ = x.shape
    tm = min(t_len, 256)
    nt = t_len // tm
    has_override = override is not None
    tail_rows = tm if has_override else 8
    row = lambda w: pl.BlockSpec((1, tm, w), lambda b, j: (b, j, 0))
    const = lambda shape: pl.BlockSpec(shape, lambda b, j: (0,) * len(shape), pipeline_mode=pl.Buffered(1))
    in_specs = [row(D_MODEL), row(MLSTM_W), row(NSA_W)]
    args = [x, mix_m, mix_n]
    if has_override:
        in_specs.append(row(2 * D_FF))
        args.append(override)
    in_specs += [const(w.shape) for w in wts]
    args += list(wts)
    y, tail = pl.pallas_call(
        functools.partial(_ffn_body, tm=tm, has_override=has_override),
        grid=(bsz, nt),
        in_specs=in_specs,
        out_specs=[row(D_MODEL), pl.BlockSpec((1, 1, tail_rows, 2 * D_FF), lambda b, j: (b, j, 0, 0))],
        out_shape=[jax.ShapeDtypeStruct((bsz, t_len, D_MODEL), F32),
                   jax.ShapeDtypeStruct((bsz, nt, tail_rows, 2 * D_FF), F32)],
        scratch_shapes=[pltpu.VMEM((8, 2 * D_FF), F32), pltpu.VMEM((8 + tm, FF_CHUNK), F32)],
        compiler_params=pltpu.CompilerParams(dimension_semantics=("arbitrary", "arbitrary"),
                                             vmem_limit_bytes=VMEM_LIMIT),
        name="ffn",
    )(*args)
    return y, tail


def _permute_in_proj(w_in, b_in):
    gates_a = slice(2 * MLSTM_W + 2 * MLSTM_W, 2 * MLSTM_W + 2 * MLSTM_W + 2 * N_MLSTM_HEADS)
    nsa0 = gates_a.stop
    nsa1 = nsa0 + NSA_W + 6 * KV_W

    def perm(a):
        pad = IN_PAD - a.shape[-1]
        parts = [a[..., :gates_a.start], a[..., nsa0:nsa1], a[..., gates_a], a[..., nsa1:]]
        parts.append(jnp.zeros(a.shape[:-1] + (pad,), a.dtype))
        return jnp.concatenate(parts, axis=-1)

    return perm(w_in).astype(BF16), perm(b_in).reshape(1, IN_PAD)


def _front_pad(a, n):
    return jnp.pad(a, ((0, 0), (n, 0), (0, 0)))


def kernel(x_prompt, x_sample, cache_nsa_kv, cache_win_kv, state_mlstm_c, state_mlstm_n, state_mlstm_m,
           state_mlstm_conv, state_ffn_conv, page_table,
           w_in, b_in, mlstm_conv_w, mlstm_conv_b, mlstm_norm_w, nsa_pool_w, rel_bias, w_out,
           ln1_w, ln1_b, w_up, b_up, ffn_conv_w, ffn_conv_b, w_down, b_down, ln2_w, ln2_b):
    bsz, t_len, _ = x_prompt.shape
    dec_b, n_new, _ = x_sample.shape
    assert n_new == N_NEW and cache_win_kv.shape[1] == WINDOW

    w_p, b_p = _permute_in_proj(w_in, b_in)
    pool_w2 = jnp.concatenate([jnp.repeat(nsa_pool_w[0], HEAD_DIM, axis=-1),
                               jnp.repeat(nsa_pool_w[1], HEAD_DIM, axis=-1)], axis=-1)
    row = lambda v: v.reshape(1, -1)
    ffn_w = (w_out[:MLSTM_W].astype(BF16), w_out[MLSTM_W:].astype(BF16), row(ln1_w), row(ln1_b),
             w_up.astype(BF16), row(b_up), ffn_conv_w, row(ffn_conv_b), w_down.astype(BF16), row(b_down),
             row(ln2_w), row(ln2_b))

    zqk, zv, zo, nq, kv, win, zg = _proj_in(x_prompt.reshape(bsz * t_len, D_MODEL), w_p, b_p)
    seq3 = lambda a: a.reshape(bsz, t_len, a.shape[-1])
    zqk, zv, zo, nq, kv, win, zg = map(seq3, (zqk, zv, zo, nq, kv, win, zg))
    mix_m, c_p, n_p, m_p = _mlstm(zqk, zv, zo, zg, None, mlstm_conv_w, mlstm_conv_b, mlstm_norm_w, 0)
    mix_n = _nsa_prompt(nq, kv, win, zg, pool_w2, rel_bias)
    y_p, tail_p = _ffn(x_prompt, mix_m, mix_n, None, ffn_w)
    kv_p = kv.reshape(bsz, t_len, 4, N_KV_HEADS, HEAD_DIM)
    win_p = win[:, t_len - min(WINDOW, t_len):].reshape(bsz, -1, 2, N_KV_HEADS, HEAD_DIM)
    mconv_p = zqk[:, t_len - (MLSTM_CONV - 1):]
    fconv_p = tail_p[:, -1, 8 - (FFN_CONV - 1):]

    pad = SEQ_PAD - N_NEW
    xs = _front_pad(x_sample, pad)
    szqk, szv, szo, snq, skv, swin, szg = _proj_in(xs.reshape(dec_b * SEQ_PAD, D_MODEL), w_p, b_p)
    sseq = lambda a: a.reshape(dec_b, SEQ_PAD, a.shape[-1])
    szqk, szv, szo, snq, skv, swin, szg = map(sseq, (szqk, szv, szo, snq, skv, swin, szg))
    lead = MLSTM_CHUNK - N_NEW
    new = lambda a: a[:, pad:]
    m_qk = jnp.concatenate([jnp.zeros((dec_b, lead - (MLSTM_CONV - 1), 2 * MLSTM_W), F32),
                            state_mlstm_conv.astype(F32), new(szqk)], axis=1)
    mix_ms, c_s, n_s, m_s = _mlstm(m_qk, _front_pad(new(szv), lead), _front_pad(new(szo), lead),
                                   _front_pad(new(szg), lead),
                                   (state_mlstm_c, state_mlstm_n, state_mlstm_m),
                                   mlstm_conv_w, mlstm_conv_b, mlstm_norm_w, lead)
    mix_ms = mix_ms[:, MLSTM_CHUNK - SEQ_PAD:]
    cache3 = jnp.transpose(cache_nsa_kv, (0, 2, 3, 4, 1)).reshape(cache_nsa_kv.shape[0], 4 * KV_W, PAGE_SIZE)
    o_c, ids_rows = _nsa_cmp(page_table, rel_bias, snq, pool_w2, cache3)
    ids = ids_rows.reshape(dec_b, N_KV_HEADS, SEQ_PAD, LANES)[:, :, pad:, :N_PICK]
    win_cache = jnp.transpose(cache_win_kv, (0, 2, 3, 4, 1)).reshape(dec_b, 2 * KV_W, WINDOW)
    mix_ns = _nsa_sel(page_table, ids, ids_rows, rel_bias, snq, skv, win_cache, swin, szg, o_c, cache3)
    override = jnp.concatenate([jnp.zeros((dec_b, pad - (FFN_CONV - 1), 2 * D_FF), F32),
                                state_ffn_conv.astype(F32),
                                jnp.zeros((dec_b, N_NEW, 2 * D_FF), F32)], axis=1)
    flat = lambda a: a.reshape(1, dec_b * SEQ_PAD, a.shape[-1])
    y_s8, u_s = _ffn(flat(xs), flat(mix_ms), flat(mix_ns), flat(override), ffn_w)
    y_s = y_s8.reshape(dec_b, SEQ_PAD, D_MODEL)[:, pad:]
    kv_s = new(skv).reshape(dec_b, N_NEW, 4, N_KV_HEADS, HEAD_DIM)
    win_s = jnp.concatenate([cache_win_kv[:, N_NEW:], new(swin).reshape(dec_b, N_NEW, 2, N_KV_HEADS, HEAD_DIM)],
                            axis=1)
    mconv_s = szqk[:, SEQ_PAD - (MLSTM_CONV - 1):]
    fconv_s = u_s.reshape(dec_b, SEQ_PAD, 2 * D_FF)[:, SEQ_PAD - (FFN_CONV - 1):]
    return (y_p, y_s, kv_p, kv_s, win_p, win_s, c_p, c_s, n_p, n_s, m_p, m_s, mconv_p, mconv_s, fconv_p, fconv_s)
```

```python
import functools
import math

import jax
import jax.numpy as jnp
from jax import lax
from jax.experimental import pallas as pl
from jax.experimental.pallas import tpu as pltpu

F32 = jnp.float32
BF16 = jnp.bfloat16
I32 = jnp.int32

D_MODEL = 1024
HEAD_DIM = 64
N_MLSTM_HEADS = 8
N_NSA_HEADS = 8
N_KV_HEADS = 2
GQ = N_NSA_HEADS // N_KV_HEADS
MLSTM_W = N_MLSTM_HEADS * HEAD_DIM
NSA_W = N_NSA_HEADS * HEAD_DIM
KV_W = N_KV_HEADS * HEAD_DIM
MLSTM_CHUNK = 64
MLSTM_CONV = 4
CMP_BLOCK = 32
SEL_BLOCK = 64
SEL_TOPK = 16
WINDOW = 512
NSA_QBLOCK = 64
N_BUCKETS = 32
MAX_DISTANCE = 128
D_FF = 2688
FFN_CONV = 3
DEPTH = 1
ALPHA = (2.0 * DEPTH) ** 0.25
LN_EPS = 1e-5
PAGE_SIZE = 128

LANES = 128
VMEM_LIMIT = 56 * 1024 * 1024
NEG_INF = float("-inf")

SEQ_PAD = 8
N_NEW = 4
N_QROWS = N_NSA_HEADS * SEQ_PAD


def _bucket_thresholds():
    max_exact = N_BUCKETS // 2
    thr = list(range(max_exact + 1))
    for k in range(1, N_BUCKETS - max_exact):
        thr.append(int(math.ceil(max_exact * (MAX_DISTANCE / max_exact) ** (k / (N_BUCKETS - max_exact)))))
    return tuple(thr)


BUCKET_THR = _bucket_thresholds()
FAR_DIST = BUCKET_THR[N_BUCKETS - 1]


def _dot(a, b, precision=None):
    return jnp.dot(a, b, preferred_element_type=F32, precision=precision)


def _dot_nt(a, b, precision=None):
    return lax.dot_general(a, b, (((1,), (1,)), ((), ())), preferred_element_type=F32, precision=precision)


def _dot_tn(a, b):
    return lax.dot_general(a, b, (((0,), (0,)), ((), ())), preferred_element_type=F32)


def _layer_norm(x, w, b):
    mu = jnp.mean(x, -1, keepdims=True)
    xc = x - mu
    var = jnp.mean(xc * xc, -1, keepdims=True)
    return xc * lax.rsqrt(var + LN_EPS) * w + b


def _log_sigmoid(x):
    return jnp.minimum(x, 0.0) - jnp.log1p(jnp.exp(-jnp.abs(x)))


def _bias_from_dist(dist, rb_ref, head):
    acc = jnp.full(dist.shape, rb_ref[0, head], F32)
    for k in range(1, N_BUCKETS):
        acc = jnp.where(dist >= BUCKET_THR[k], rb_ref[k, head], acc)
    return acc


def _masked_softmax(s):
    mx = jnp.max(s, -1, keepdims=True)
    mx = jnp.where(mx > NEG_INF, mx, 0.0)
    e = jnp.exp(s - mx)
    den = jnp.sum(e, -1, keepdims=True)
    return e / jnp.maximum(den, 1e-30)


_C_QK, _C_V, _C_O, _C_NQ, _C_KV, _C_WIN, _C_G = 0, 1024, 1536, 2048, 2560, 3072, 3328
IN_PAD = 3456
G_IGATE, G_FGATE, G_NSA = 0, N_MLSTM_HEADS, 2 * N_MLSTM_HEADS


def _proj_in_body(*refs, kv_transposed):
    if kv_transposed:
        x_ref, w_ref, b_ref, wkvt_ref, bkv_ref, qk_ref, v_ref, o_ref, nq_ref, kv_ref, win_ref, g_ref, kvt_ref = refs
    else:
        x_ref, w_ref, b_ref, qk_ref, v_ref, o_ref, nq_ref, kv_ref, win_ref, g_ref = refs
    xb = x_ref[...].astype(BF16)
    z = _dot(xb, w_ref[...]) + b_ref[...]
    qk_ref[...] = z[:, _C_QK:_C_V]
    v_ref[...] = z[:, _C_V:_C_O]
    o_ref[...] = z[:, _C_O:_C_NQ]
    nq_ref[...] = z[:, _C_NQ:_C_KV]
    kv_ref[...] = z[:, _C_KV:_C_WIN]
    win_ref[...] = z[:, _C_WIN:_C_G]
    g_ref[...] = z[:, _C_G:IN_PAD]
    if kv_transposed:
        kvt_ref[0] = _dot_nt(wkvt_ref[...], xb) + bkv_ref[...]


def _proj_in(x2d, w_p, b_p, seq_len=None):
    rows = x2d.shape[0]
    tm = min(rows, 256)
    widths = (1024, 512, 512, 512, 512, 256, 128)
    const = lambda i: (0, 0)
    const_spec = lambda shape: pl.BlockSpec(shape, const, pipeline_mode=pl.Buffered(1))
    in_specs = [pl.BlockSpec((tm, D_MODEL), lambda i: (i, 0)), const_spec((D_MODEL, IN_PAD)), const_spec((1, IN_PAD))]
    args = [x2d, w_p, b_p]
    out_specs = [pl.BlockSpec((tm, w), lambda i: (i, 0)) for w in widths]
    out_shape = [jax.ShapeDtypeStruct((rows, w), F32) for w in widths]
    if seq_len is not None:
        tiles = seq_len // tm
        in_specs += [const_spec((4 * KV_W, D_MODEL)), const_spec((4 * KV_W, 1))]
        args += [jnp.transpose(w_p[:, _C_KV:_C_WIN]), b_p[0, _C_KV:_C_WIN].reshape(4 * KV_W, 1)]
        out_specs.append(pl.BlockSpec((1, 4 * KV_W, tm), lambda i: (i // tiles, 0, i % tiles)))
        out_shape.append(jax.ShapeDtypeStruct((rows // seq_len, 4 * KV_W, seq_len), F32))
    return pl.pallas_call(
        functools.partial(_proj_in_body, kv_transposed=seq_len is not None),
        grid=(rows // tm,),
        in_specs=in_specs,
        out_specs=out_specs,
        out_shape=out_shape,
        compiler_params=pltpu.CompilerParams(dimension_semantics=("arbitrary",),
                                             vmem_limit_bytes=VMEM_LIMIT),
        name="proj_in",
    )(*args)


N_PAIRS = N_MLSTM_HEADS // 2
GATE_LANE0 = N_MLSTM_HEADS
MLSTM_SEQS_PER_STEP = 4


def _dot_split(x, onehot_bf16, pieces):
    acc = None
    rem = x
    for n in range(pieces):
        part = rem.astype(BF16)
        d = _dot(part, onehot_bf16)
        acc = d if acc is None else acc + d
        if n + 1 < pieces:
            rem = rem - part.astype(F32)
    return acc


def _mlstm_body(*refs, tb, nbg, n_dummy, has_state):
    if has_state:
        (zqk_ref, zv_ref, zo_ref, zg_ref, c0_ref, n0_ref, m0_ref, cw_ref, cb_ref, nw_ref,
         mix_ref, c_out, n_out, m_out, ubuf, qk_sc, ct_sc, n_sc, m_sc) = refs
    else:
        (zqk_ref, zv_ref, zo_ref, zg_ref, cw_ref, cb_ref, nw_ref,
         mix_ref, c_out, n_out, m_out, ubuf, qk_sc, ct_sc, n_sc, m_sc) = refs
    L = MLSTM_CHUNK
    D = HEAD_DIM
    j = pl.program_id(1)
    hi = lax.Precision.HIGHEST
    head_lanes = slice(GATE_LANE0, GATE_LANE0 + N_MLSTM_HEADS)
    pairs = [(bi, p) for bi in range(nbg) for p in range(N_PAIRS)]

    @pl.when(j == 0)
    def _init():
        ubuf[:, 0:8, :] = jnp.zeros((nbg, 8, 2 * MLSTM_W), F32)
        ct_sc[...] = jnp.zeros(ct_sc.shape, F32)
        n_sc[...] = jnp.zeros(n_sc.shape, F32)
        m_sc[...] = jnp.zeros(m_sc.shape, F32)
        if has_state:
            for bi in range(nbg):
                for h in range(N_MLSTM_HEADS):
                    p, hh = divmod(h, 2)
                    hs = slice(hh * D, (hh + 1) * D)
                    ct_sc[bi, p, hs, hs] = c0_ref[bi, h]
                    n_sc[bi, p, :, hs] = n0_ref[bi, h:h + 1, :]
                m_sc[bi, :, head_lanes] = m0_ref[bi]

    for bi in range(nbg):
        u = zqk_ref[bi]
        ubuf[bi, 8:8 + tb, :] = u
        conv = cb_ref[...] + ubuf[bi, 5:5 + tb, :] * cw_ref[0:1, :]
        conv = conv + ubuf[bi, 6:6 + tb, :] * cw_ref[1:2, :]
        conv = conv + ubuf[bi, 7:7 + tb, :] * cw_ref[2:3, :]
        conv = conv + u * cw_ref[3:4, :]
        ubuf[bi, 5:8, :] = u[tb - 3:tb, :]
        qk_sc[bi] = conv * jax.nn.sigmoid(conv)

    tri = (lax.broadcasted_iota(I32, (L, L), 1) <= lax.broadcasted_iota(I32, (L, L), 0)).astype(F32)
    row_l = lax.broadcasted_iota(I32, (L, LANES), 0)
    lane = lax.broadcasted_iota(I32, (L, LANES), 1)
    low = lane < D
    key_tok = lane % D
    causal_pair = key_tok <= row_l
    eye_pair = key_tok == row_l
    spread = (lax.broadcasted_iota(I32, (LANES, MLSTM_W), 0) - GATE_LANE0
              == lax.broadcasted_iota(I32, (LANES, MLSTM_W), 1) // D).astype(BF16)
    same_head = (lax.broadcasted_iota(I32, (LANES, LANES), 0) // D
                 == lax.broadcasted_iota(I32, (LANES, LANES), 1) // D)
    ones_bd = same_head.astype(BF16)
    slab_rows = 3 * L + 8

    def chunk(ci, carry):
        r0 = pl.multiple_of(ci * L, L)
        rows = pl.ds(r0, L)
        gates = []
        for bi in range(nbg):
            g = zg_ref[bi, rows, :]
            if n_dummy:
                dummy = (row_l + ci * L) < n_dummy
                g = jnp.where(dummy, jnp.where(lane < G_FGATE, -1e30, 1e30), g)
            gates.append(g)
        b_all = _dot(tri, jnp.concatenate([_log_sigmoid(g) for g in gates], axis=1), hi)
        stacks = []
        for bi in range(nbg):
            b = b_all[:, bi * LANES:(bi + 1) * LANES]
            r = pltpu.roll(gates[bi], GATE_LANE0 - G_IGATE, 1) - b
            cm = r
            for sh in (1, 2, 4, 8, 16, 32):
                cm = jnp.where(row_l >= sh, jnp.maximum(cm, pltpu.roll(cm, sh, 0)), cm)
            m_st = m_sc[bi]
            mx = jnp.maximum(m_st, cm)
            mx_end = mx[L - 1:L, :]
            m_sc[bi] = b[L - 1:L, :] + mx_end
            stacks += [mx, r, b + mx, m_st, mx_end, jnp.zeros((6, LANES), F32)]
        x_all = _dot_split(jnp.concatenate(stacks, axis=0), spread, 2)

        work = []
        den_terms = []
        for bi, p in pairs:
            cs = slice(p * LANES, (p + 1) * LANES)
            x0 = bi * slab_rows
            mx_p, rs_p, mt_p = x_all[x0:x0 + L, cs], x_all[x0 + L:x0 + 2 * L, cs], x_all[x0 + 2 * L:x0 + 3 * L, cs]
            mst_p, mxe_p = x_all[x0 + 3 * L:x0 + 3 * L + 1, cs], x_all[x0 + 3 * L + 1:x0 + 3 * L + 2, cs]
            wi_p = jnp.exp(mst_p - mx_p)
            fl_p = jnp.exp(-mt_p)
            wk_p = jnp.exp(rs_p - mxe_p)
            sc_p = jnp.exp(mst_p - mxe_p)
            r_row = jnp.sum(jnp.where(eye_pair, rs_p, 0.0), axis=0, keepdims=True)
            w_intra = jnp.exp(jnp.where(causal_pair, r_row - mx_p, NEG_INF))
            q_p = qk_sc[bi, rows, cs]
            k_p = qk_sc[bi, rows, MLSTM_W + p * LANES:MLSTM_W + (p + 1) * LANES] * (D ** -0.5)
            v_p = zv_ref[bi, rows, cs]
            qb = q_p.astype(BF16)
            k_bd = jnp.concatenate([jnp.where(low, k_p, 0.0), jnp.where(low, 0.0, k_p)], axis=0)
            v_bd = jnp.concatenate([jnp.where(low, v_p, 0.0), jnp.where(low, 0.0, v_p)], axis=0).astype(BF16)
            c_t = ct_sc[bi, p]
            both = _dot_nt(qb, jnp.concatenate([k_bd, c_t], axis=0).astype(BF16))
            a = w_intra * both[:, 0:LANES]
            num = _dot(a.astype(BF16), v_bd) + both[:, LANES:2 * LANES] * wi_p
            den_terms.append(a + wi_p * (q_p * n_sc[bi, p]))
            work.append((num, fl_p, k_p * wk_p, v_p, sc_p, c_t))
        den_all = _dot_split(jnp.concatenate(den_terms, axis=0), ones_bd, 2)

        hhs = []
        for n, (num, fl_p, _, _, _, _) in enumerate(work):
            hhs.append(num / jnp.maximum(jnp.abs(den_all[n * L:(n + 1) * L, :]), fl_p))
        mu_all = _dot_split(jnp.concatenate(hhs, axis=0), ones_bd, 1) * (1.0 / D)
        hcs = [hh - mu_all[n * L:(n + 1) * L, :] for n, hh in enumerate(hhs)]
        var_all = _dot_split(jnp.concatenate([hc * hc for hc in hcs], axis=0), ones_bd, 1) * (1.0 / D)
        for n, (bi, p) in enumerate(pairs):
            cs = slice(p * LANES, (p + 1) * LANES)
            hn = hcs[n] * lax.rsqrt(var_all[n * L:(n + 1) * L, :] + LN_EPS)
            og = jax.nn.sigmoid(zo_ref[bi, rows, cs])
            mix_ref[bi, rows, cs] = og * (hn * nw_ref[:, cs])
            _, _, kw, v_p, sc_p, c_t = work[n]
            upd = _dot_tn(v_p.astype(BF16), kw.astype(BF16))
            ct_sc[bi, p] = sc_p * c_t + jnp.where(same_head, upd, 0.0)
            n_sc[bi, p] = sc_p * n_sc[bi, p] + jnp.sum(kw, axis=0, keepdims=True)
        return carry

    lax.fori_loop(0, tb // L, chunk, 0)

    @pl.when(j == pl.num_programs(1) - 1)
    def _final():
        for bi in range(nbg):
            for h in range(N_MLSTM_HEADS):
                p, hh = divmod(h, 2)
                hs = slice(hh * D, (hh + 1) * D)
                c_out[bi, h] = ct_sc[bi, p, hs, hs]
                n_out[bi, h:h + 1, :] = n_sc[bi, p, :, hs]
            m_out[bi] = m_sc[bi, :, head_lanes]


def _mlstm(zqk, zv, zo, zg, state, conv_w, conv_b, norm_w, n_dummy):
    bsz, t_len, _ = zqk.shape
    tb = min(t_len, 256)
    nbg = MLSTM_SEQS_PER_STEP
    assert bsz % nbg == 0
    has_state = state is not None
    row = lambda w: pl.BlockSpec((nbg, tb, w), lambda b, j: (b, j, 0))
    per_b = lambda shape: pl.BlockSpec((nbg,) + shape, lambda b, j: (b,) + (0,) * len(shape))
    const = lambda shape: pl.BlockSpec(shape, lambda b, j: (0,) * len(shape))
    in_specs = [row(2 * MLSTM_W), row(MLSTM_W), row(MLSTM_W), row(LANES)]
    args = [zqk, zv, zo, zg]
    if has_state:
        c0, n0, m0 = state
        in_specs += [per_b((N_MLSTM_HEADS, HEAD_DIM, HEAD_DIM)), per_b((N_MLSTM_HEADS, HEAD_DIM)),
                     per_b((1, N_MLSTM_HEADS))]
        args += [c0, n0, m0.reshape(bsz, 1, N_MLSTM_HEADS)]
    in_specs += [const((MLSTM_CONV, 2 * MLSTM_W)), const((1, 2 * MLSTM_W)), const((1, MLSTM_W))]
    args += [conv_w, conv_b.reshape(1, -1), norm_w.reshape(1, -1)]
    mix, c_new, n_new, m_new = pl.pallas_call(
        functools.partial(_mlstm_body, tb=tb, nbg=nbg, n_dummy=n_dummy, has_state=has_state),
        grid=(bsz // nbg, t_len // tb),
        in_specs=in_specs,
        out_specs=[row(MLSTM_W), per_b((N_MLSTM_HEADS, HEAD_DIM, HEAD_DIM)),
                   per_b((N_MLSTM_HEADS, HEAD_DIM)), per_b((1, N_MLSTM_HEADS))],
        out_shape=[jax.ShapeDtypeStruct((bsz, t_len, MLSTM_W), F32),
                   jax.ShapeDtypeStruct((bsz, N_MLSTM_HEADS, HEAD_DIM, HEAD_DIM), F32),
                   jax.ShapeDtypeStruct((bsz, N_MLSTM_HEADS, HEAD_DIM), F32),
                   jax.ShapeDtypeStruct((bsz, 1, N_MLSTM_HEADS), F32)],
        scratch_shapes=[pltpu.VMEM((nbg, 8 + tb, 2 * MLSTM_W), F32),
                        pltpu.VMEM((nbg, tb, 2 * MLSTM_W), F32),
                        pltpu.VMEM((nbg, N_PAIRS, LANES, LANES), F32),
                        pltpu.VMEM((nbg, N_PAIRS, 1, LANES), F32),
                        pltpu.VMEM((nbg, 1, LANES), F32)],
        compiler_params=pltpu.CompilerParams(dimension_semantics=("arbitrary", "arbitrary"),
                                             vmem_limit_bytes=VMEM_LIMIT),
        name="mlstm",
    )(*args)
    return mix, c_new, n_new, m_new.reshape(bsz, N_MLSTM_HEADS)


def _block_diag_queries(x, n_rows, heads=tuple(range(N_NSA_HEADS)), scale=HEAD_DIM ** -0.5):
    lane = lax.broadcasted_iota(I32, (n_rows, LANES), 1)
    low = lane < HEAD_DIM
    pieces = []
    for h in heads:
        pair = x[:, (h // 2) * LANES:(h // 2 + 1) * LANES]
        src_low = h % 2 == 0
        dst_low = h // GQ == 0
        if src_low != dst_low:
            pair = pltpu.roll(pair, HEAD_DIM, 1)
        pieces.append(jnp.where(low if dst_low else ~low, pair, 0.0))
    return (jnp.concatenate(pieces, axis=0) * scale).astype(BF16)


def _group_lanes(o, g):
    return o[:, g * HEAD_DIM:(g + 1) * HEAD_DIM]


def _gated_mix(gates, o_c, o_s, o_w, h, n_rows):
    g = h // GQ
    rs = slice(h * n_rows, (h + 1) * n_rows)
    col = lambda branch: gates[:, G_NSA + branch * N_NSA_HEADS + h:G_NSA + branch * N_NSA_HEADS + h + 1]
    return (col(0) * _group_lanes(o_c[rs], g) + col(1) * _group_lanes(o_s[rs], g)
            + col(2) * _group_lanes(o_w[rs], g))


SEL_TILE = 256
SEL_TILE_BLOCKS = SEL_TILE // SEL_BLOCK
WIN_BAND = WINDOW + NSA_QBLOCK
N_COLS = N_NSA_HEADS * NSA_QBLOCK
PROMPT_HEADS = tuple(GQ * (jj % N_KV_HEADS) + jj // N_KV_HEADS for jj in range(N_NSA_HEADS))
SEL_REL0 = 2 * SEL_TILE - SEL_BLOCK
SEL_BIAS_ROWS = SEL_REL0 + SEL_TILE


LOG2E = math.log2(math.e)


def _softmax2_rows(s):
    mx = jnp.max(s, 0, keepdims=True)
    mx = jnp.where(mx > NEG_INF, mx, 0.0)
    e = jnp.exp2(s - mx)
    return e / jnp.maximum(jnp.sum(e, 0, keepdims=True), 1e-30)


def _nsa_prompt_body(rb_ref, nq_ref, kv_ref, win_ref, zg_ref, pw_ref, out_ref,
                     ks_sc, vst_sc, wk_sc, wv_sc, kc_sc, pv_sc, vct_sc, bs_sc, bw_sc, fc_sc,
                     m_sc, l_sc, acc_sc, *, t_len):
    QB = NSA_QBLOCK
    n_cmp = t_len // CMP_BLOCK
    n_sel = t_len // SEL_BLOCK
    i = pl.program_id(1)
    hi = lax.Precision.HIGHEST

    @pl.when(i == 0)
    def _init():
        ks_sc[0:SEL_TILE, :] = jnp.zeros((SEL_TILE, LANES), BF16)
        vst_sc[0] = jnp.zeros((LANES, SEL_TILE), BF16)
        wk_sc[0:WINDOW, :] = jnp.zeros((WINDOW, LANES), BF16)
        wv_sc[0:WINDOW, :] = jnp.zeros((WINDOW, LANES), BF16)
        rows_per = 2 * SEL_TILE
        blocks_per = rows_per // CMP_BLOCK

        def fill(c, carry):
            r0 = pl.multiple_of(c * rows_per, rows_per)
            kv = kv_ref[0, pl.ds(r0, rows_per), :]
            ks_sc[pl.ds(SEL_TILE + r0, rows_per), :] = kv[:, 2 * LANES:3 * LANES].astype(BF16)
            for half in range(2):
                v_t = kv[half * SEL_TILE:(half + 1) * SEL_TILE, 3 * LANES:4 * LANES]
                vst_sc[2 * c + half + 1] = jnp.transpose(v_t).astype(BF16)
            w = win_ref[0, pl.ds(r0, rows_per), :]
            wk_sc[pl.ds(WINDOW + r0, rows_per), :] = w[:, 0:LANES].astype(BF16)
            wv_sc[pl.ds(WINDOW + r0, rows_per), :] = w[:, LANES:2 * LANES].astype(BF16)
            pooled = jnp.sum(kv[:, 0:2 * LANES].reshape(blocks_per, CMP_BLOCK, 2 * LANES) * pw_ref[...][None],
                             axis=1)
            c0 = pl.multiple_of(c * blocks_per, blocks_per)
            kc_sc[pl.ds(c0, blocks_per), :] = pooled[:, 0:LANES].astype(BF16)
            pv_sc[pl.ds(c0, blocks_per), :] = pooled[:, LANES:2 * LANES]
            return carry

        lax.fori_loop(0, t_len // rows_per, fill, 0)
        vct_sc[...] = jnp.transpose(pv_sc[...]).astype(BF16)

        for jj, h in enumerate(PROMPT_HEADS):
            cs = slice(jj * QB, (jj + 1) * QB)
            u_i = lax.broadcasted_iota(I32, (SEL_BIAS_ROWS, QB), 0)
            q_i = lax.broadcasted_iota(I32, (SEL_BIAS_ROWS, QB), 1)
            dist = q_i - (u_i - SEL_REL0)
            near = (_bias_from_dist(dist, rb_ref, h) - rb_ref[N_BUCKETS - 1, h]) * LOG2E
            bs_sc[:, cs] = jnp.where(dist >= 0, near, NEG_INF)
            s_i = lax.broadcasted_iota(I32, (WIN_BAND, QB), 0)
            q_i = lax.broadcasted_iota(I32, (WIN_BAND, QB), 1)
            dist = q_i + WINDOW - s_i
            bw_sc[:, cs] = jnp.where((dist >= 0) & (dist <= WINDOW), _bias_from_dist(dist, rb_ref, h) * LOG2E,
                                     NEG_INF)
            c_i = lax.broadcasted_iota(I32, (n_cmp, QB), 0)
            q_i = lax.broadcasted_iota(I32, (n_cmp, QB), 1)
            rel = jnp.where(c_i < n_cmp // 2, c_i, c_i - n_cmp)
            dist = q_i - CMP_BLOCK * rel - (CMP_BLOCK - 1)
            dist = jnp.where((c_i >= 2) & (c_i < n_cmp // 2), FAR_DIST, dist)
            fc_sc[:, cs] = _bias_from_dist(dist, rb_ref, h) * LOG2E

    qbd = _block_diag_queries(nq_ref[0], QB, PROMPT_HEADS, HEAD_DIM ** -0.5 * LOG2E)

    w0 = pl.multiple_of(i * QB, QB)
    s_w = _dot_nt(wk_sc[pl.ds(w0, WIN_BAND), :], qbd) + bw_sc[...]
    in_seq = lax.broadcasted_iota(I32, (WIN_BAND, N_COLS), 0) >= WINDOW - i * QB
    p_w = _softmax2_rows(jnp.where(in_seq, s_w, NEG_INF))
    o_w = _dot_tn(wv_sc[pl.ds(w0, WIN_BAND), :], p_w.astype(BF16))

    s_c = _dot_nt(kc_sc[...], qbd) + pltpu.roll(fc_sc[...], 2 * i, 0)
    c_i = lax.broadcasted_iota(I32, (n_cmp, N_COLS), 0)
    q_i = lax.broadcasted_iota(I32, (n_cmp, N_COLS), 1) % QB
    p_c = _softmax2_rows(jnp.where(c_i * CMP_BLOCK + (CMP_BLOCK - 1) <= i * QB + q_i, s_c, NEG_INF))
    o_c = _dot(vct_sc[...], p_c.astype(BF16))

    p_sum = p_c[:, 0:LANES]
    for j in range(1, GQ):
        p_sum = p_sum + p_c[:, j * LANES:(j + 1) * LANES]
    ratio = SEL_BLOCK // CMP_BLOCK
    pair = (lax.broadcasted_iota(I32, (n_sel, n_cmp), 1) // ratio
            == lax.broadcasted_iota(I32, (n_sel, n_cmp), 0)).astype(F32)
    imp = _dot(pair, p_sum, hi)
    blk = lax.broadcasted_iota(I32, (n_sel, LANES), 0)
    score = jnp.where(blk == i, jnp.inf, jnp.where(blk < i, imp, NEG_INF))

    def pick(_, carry):
        work, chosen = carry
        best = jnp.max(work, 0, keepdims=True)
        first = jnp.min(jnp.where(work == best, blk, n_sel), 0, keepdims=True)
        hit = blk == first
        return jnp.where(hit, NEG_INF, work), jnp.where(hit, 1.0, chosen)

    _, chosen = lax.fori_loop(0, SEL_TOPK, pick, (score, jnp.zeros((n_sel, LANES), F32)), unroll=True)
    chosen = jnp.where(score > NEG_INF, chosen, 0.0).astype(BF16)

    def tile_scores(slot, bias):
        k_t = ks_sc[pl.ds(pl.multiple_of(slot * SEL_TILE, SEL_TILE), SEL_TILE), :]
        expand = ((slot - 1) * SEL_TILE_BLOCKS + lax.broadcasted_iota(I32, (SEL_TILE, n_sel), 0) // SEL_BLOCK
                  == lax.broadcasted_iota(I32, (SEL_TILE, n_sel), 1))
        picked = _dot(jnp.where(expand, 1.0, 0.0).astype(BF16), chosen)
        picked = jnp.concatenate([picked] * GQ, axis=1) > 0.5
        s = _dot_nt(k_t, qbd)
        return jnp.where(picked, s if bias is None else s + bias, NEG_INF)

    def sel_pair(slot_a, slot_b, bias_a, bias_b, first):
        s_a = tile_scores(slot_a, bias_a)
        s_b = tile_scores(slot_b, bias_b)
        t_max = jnp.maximum(jnp.max(s_a, 0, keepdims=True), jnp.max(s_b, 0, keepdims=True))
        m_new = t_max if first else jnp.maximum(m_sc[...], t_max)
        e_a = jnp.exp2(s_a - m_new)
        e_b = jnp.exp2(s_b - m_new)
        l_new = jnp.sum(e_a, 0, keepdims=True) + jnp.sum(e_b, 0, keepdims=True)
        acc_new = _dot(vst_sc[slot_a], e_a.astype(BF16)) + _dot(vst_sc[slot_b], e_b.astype(BF16))
        if first:
            l_sc[...] = l_new
            acc_sc[...] = acc_new
        else:
            scale = jnp.exp2(m_sc[...] - m_new)
            l_sc[...] = scale * l_sc[...] + l_new
            acc_sc[...] = scale * acc_sc[...] + acc_new
        m_sc[...] = m_new

    last = i // SEL_TILE_BLOCKS
    u0 = pl.multiple_of(SEL_REL0 - SEL_TILE - SEL_BLOCK * (i % SEL_TILE_BLOCKS), SEL_BLOCK)
    sel_pair(last + 1, last, bs_sc[pl.ds(u0 + SEL_TILE, SEL_TILE), :], bs_sc[pl.ds(u0, SEL_TILE), :], True)

    def far_pair(p, carry):
        slot_b = 2 * p + 2
        sel_pair(2 * p + 1, jnp.where(slot_b < last, slot_b, 0), None, None, False)
        return carry

    lax.fori_loop(0, last // 2, far_pair, 0)
    o_s = acc_sc[...] / jnp.maximum(l_sc[...], 1e-30)

    eye = (lax.broadcasted_iota(I32, (3 * N_NSA_HEADS + 8, LANES), 0) + G_NSA
           == lax.broadcasted_iota(I32, (3 * N_NSA_HEADS + 8, LANES), 1)).astype(F32)
    gates = jax.nn.sigmoid(_dot_nt(eye, zg_ref[0], hi))
    pieces = []
    for h in range(N_NSA_HEADS):
        g = h // GQ
        jj = PROMPT_HEADS.index(h)
        part = lambda o: o[g * HEAD_DIM:(g + 1) * HEAD_DIM, jj * QB:(jj + 1) * QB]
        pieces.append(gates[h:h + 1, :] * part(o_c)
                      + gates[N_NSA_HEADS + h:N_NSA_HEADS + h + 1, :] * part(o_s)
                      + gates[2 * N_NSA_HEADS + h:2 * N_NSA_HEADS + h + 1, :] * part(o_w))
    mix_t = jnp.concatenate(pieces, axis=0)
    mix_t = jnp.concatenate([mix_t, jnp.zeros((NSA_W, LANES - QB), F32)], axis=1)
    out_ref[0] = jnp.transpose(mix_t)[0:QB, :]


def _nsa_prompt(nq, kv, win, zg, pool_w2, rel_bias):
    bsz, t_len, _ = nq.shape
    assert t_len // CMP_BLOCK == LANES and t_len % (2 * SEL_TILE) == 0
    n_cmp = t_len // CMP_BLOCK
    per_b = lambda w: pl.BlockSpec((1, t_len, w), lambda b, i: (b, 0, 0))
    blk = lambda w: pl.BlockSpec((1, NSA_QBLOCK, w), lambda b, i: (b, i, 0))
    return pl.pallas_call(
        functools.partial(_nsa_prompt_body, t_len=t_len),
        grid=(bsz, t_len // NSA_QBLOCK),
        in_specs=[pl.BlockSpec(memory_space=pltpu.SMEM),
                  blk(NSA_W), per_b(4 * KV_W), per_b(2 * KV_W), blk(LANES),
                  pl.BlockSpec((CMP_BLOCK, 2 * LANES), lambda b, i: (0, 0))],
        out_specs=blk(NSA_W),
        out_shape=jax.ShapeDtypeStruct((bsz, t_len, NSA_W), F32),
        scratch_shapes=[pltpu.VMEM((SEL_TILE + t_len, LANES), BF16),
                        pltpu.VMEM((1 + t_len // SEL_TILE, LANES, SEL_TILE), BF16),
                        pltpu.VMEM((WINDOW + t_len, LANES), BF16),
                        pltpu.VMEM((WINDOW + t_len, LANES), BF16),
                        pltpu.VMEM((n_cmp, LANES), BF16),
                        pltpu.VMEM((n_cmp, LANES), F32),
                        pltpu.VMEM((LANES, n_cmp), BF16),
                        pltpu.VMEM((SEL_BIAS_ROWS, N_COLS), F32),
                        pltpu.VMEM((WIN_BAND, N_COLS), F32),
                        pltpu.VMEM((n_cmp, N_COLS), F32),
                        pltpu.VMEM((1, N_COLS), F32),
                        pltpu.VMEM((1, N_COLS), F32),
                        pltpu.VMEM((LANES, N_COLS), F32)],
        compiler_params=pltpu.CompilerParams(dimension_semantics=("arbitrary", "arbitrary"),
                                             vmem_limit_bytes=VMEM_LIMIT),
        name="nsa_prompt",
    )(rel_bias, nq, kv, win, zg, pool_w2)


CMP_PAGES = 16
N_PICK = SEL_TOPK - 1
N_OWNERS = N_KV_HEADS * N_NEW
N_FETCH = N_OWNERS * N_PICK
OWNER_KEYS = N_PICK * PAGE_SIZE


def _sample_row_ids():
    r = lax.broadcasted_iota(I32, (N_QROWS, 1), 0)
    return r % SEQ_PAD - (SEQ_PAD - N_NEW), r // SEQ_PAD


def _sample_bias(dist, rb_ref):
    _, head = _sample_row_ids()
    col = lambda k: sum(jnp.where(head == h, rb_ref[k, h], 0.0) for h in range(N_NSA_HEADS))
    acc = jnp.broadcast_to(col(0), dist.shape)
    for k in range(1, N_BUCKETS):
        acc = jnp.where(dist >= BUCKET_THR[k], col(k), acc)
    return acc


def _softmax_two(s1, s2):
    mx = jnp.maximum(jnp.max(s1, -1, keepdims=True), jnp.max(s2, -1, keepdims=True))
    mx = jnp.where(mx > NEG_INF, mx, 0.0)
    e1 = jnp.exp(s1 - mx)
    e2 = jnp.exp(s2 - mx)
    den = jnp.maximum(jnp.sum(e1, -1, keepdims=True) + jnp.sum(e2, -1, keepdims=True), 1e-30)
    return e1 / den, e2 / den


def _nsa_cmp_body(pt_ref, rb_ref, nq_ref, pw_ref, cache_ref, oc_ref, idx_ref,
                  buf, sem, kc_sc, vc_sc, *, n_pages, past_len):
    b = pl.program_id(0)
    nb = pl.num_programs(0)
    n_chunks = n_pages // CMP_PAGES
    blocks_per = CMP_PAGES * PAGE_SIZE // CMP_BLOCK
    n_cmp = past_len // CMP_BLOCK
    n_sel_past = past_len // SEL_BLOCK
    n_prob = N_KV_HEADS * SEQ_PAD
    hi = lax.Precision.HIGHEST

    def page_copy(bb, chunk, p, slot):
        page = pt_ref[bb * n_pages + chunk * CMP_PAGES + p]
        return pltpu.make_async_copy(cache_ref.at[page, pl.ds(0, 2 * LANES), :], buf.at[slot, p], sem.at[slot])

    def start_chunk(bb, chunk, slot):
        for p in range(CMP_PAGES):
            page_copy(bb, chunk, p, slot).start()

    def wait_chunk(bb, chunk, slot):
        for p in range(CMP_PAGES):
            page_copy(bb, chunk, p, slot).wait()

    @pl.when(b == 0)
    def _prologue():
        start_chunk(0, 0, 0)

    def chunk_step(c, carry):
        slot = (b * n_chunks + c) % 2

        @pl.when(c + 1 < n_chunks)
        def _next_same():
            start_chunk(b, c + 1, 1 - slot)

        @pl.when((c + 1 == n_chunks) & (b + 1 < nb))
        def _next_seq():
            start_chunk(b + 1, 0, 1 - slot)

        wait_chunk(b, c, slot)
        rows = jnp.concatenate([jnp.transpose(buf[slot, p]) for p in range(CMP_PAGES)], axis=0)
        pooled = jnp.sum(rows.reshape(blocks_per, CMP_BLOCK, 2 * LANES) * pw_ref[...][None], axis=1)
        c0 = pl.multiple_of(c * blocks_per, blocks_per)
        kc_sc[pl.ds(c0, blocks_per), :] = pooled[:, 0:LANES].astype(BF16)
        vc_sc[pl.ds(c0, blocks_per), :] = pooled[:, LANES:2 * LANES].astype(BF16)
        return carry

    lax.fori_loop(0, n_chunks, chunk_step, 0)

    qbd = _block_diag_queries(nq_ref[0], SEQ_PAD)
    tok, _ = _sample_row_ids()
    c_i = lax.broadcasted_iota(I32, (N_QROWS, n_cmp), 1)
    dist = past_len + tok - (c_i * CMP_BLOCK + CMP_BLOCK - 1)
    s_c = _dot_nt(qbd, kc_sc[...]) + _sample_bias(dist, rb_ref)
    p_c = _masked_softmax(jnp.where(dist >= 0, s_c, NEG_INF))
    oc_ref[0] = _dot(p_c.astype(BF16), vc_sc[...])

    p_sum = jnp.sum(p_c.reshape(N_KV_HEADS, GQ, SEQ_PAD, n_cmp), axis=1).reshape(n_prob, n_cmp)
    ratio = SEL_BLOCK // CMP_BLOCK
    pair = (lax.broadcasted_iota(I32, (n_cmp, n_sel_past), 0) // ratio
            == lax.broadcasted_iota(I32, (n_cmp, n_sel_past), 1)).astype(F32)
    imp = _dot(p_sum, pair, hi)
    blk = lax.broadcasted_iota(I32, imp.shape, 1)
    slot_i = lax.broadcasted_iota(I32, (n_prob, LANES), 1)

    def pick(k, carry):
        work, ids = carry
        best = jnp.max(work, 1, keepdims=True)
        first = jnp.min(jnp.where(work == best, blk, n_sel_past), 1, keepdims=True)
        return jnp.where(blk == first, NEG_INF, work), jnp.where(slot_i == k, first, ids)

    _, ids = lax.fori_loop(0, N_PICK, pick, (imp, jnp.zeros((n_prob, LANES), I32)), unroll=True)
    idx_ref[0] = ids


def _nsa_cmp(page_table, rel_bias, nq_s, pool_w2, cache3):
    dec_b, n_pages = page_table.shape
    past_len = n_pages * PAGE_SIZE
    assert n_pages % CMP_PAGES == 0 and past_len // SEL_BLOCK >= N_PICK
    n_cmp = past_len // CMP_BLOCK
    n_prob = N_KV_HEADS * SEQ_PAD
    grid_spec = pltpu.PrefetchScalarGridSpec(
        num_scalar_prefetch=1,
        grid=(dec_b,),
        in_specs=[pl.BlockSpec(memory_space=pltpu.SMEM),
                  pl.BlockSpec((1, SEQ_PAD, NSA_W), lambda b, pt: (b, 0, 0)),
                  pl.BlockSpec((CMP_BLOCK, 2 * LANES), lambda b, pt: (0, 0)),
                  pl.BlockSpec(memory_space=pl.ANY)],
        out_specs=[pl.BlockSpec((1, N_QROWS, LANES), lambda b, pt: (b, 0, 0)),
                   pl.BlockSpec((1, n_prob, LANES), lambda b, pt: (b, 0, 0))],
        scratch_shapes=[pltpu.VMEM((2, CMP_PAGES, 2 * LANES, PAGE_SIZE), F32),
                        pltpu.SemaphoreType.DMA((2,)),
                        pltpu.VMEM((n_cmp, LANES), BF16),
                        pltpu.VMEM((n_cmp, LANES), BF16)],
    )
    return pl.pallas_call(
        functools.partial(_nsa_cmp_body, n_pages=n_pages, past_len=past_len),
        grid_spec=grid_spec,
        out_shape=[jax.ShapeDtypeStruct((dec_b, N_QROWS, LANES), F32),
                   jax.ShapeDtypeStruct((dec_b, n_prob, LANES), I32)],
        compiler_params=pltpu.CompilerParams(dimension_semantics=("arbitrary",),
                                             vmem_limit_bytes=VMEM_LIMIT),
        name="nsa_sample_cmp",
    )(page_table.reshape(-1), rel_bias, nq_s, pool_w2, cache3)


def _nsa_sel_body(pt_ref, ids_ref, rb_ref, nq_ref, kvn_ref, winc_ref, winn_ref, zg_ref, oc_ref, idv_ref,
                  cache_ref, out_ref, buf, sem, *, n_pages, past_len):
    b = pl.program_id(0)
    nb = pl.num_programs(0)
    hi = lax.Precision.HIGHEST

    def block_copy(bb, n, slot):
        blk = ids_ref[bb * N_FETCH + n]
        page = pt_ref[bb * n_pages + blk // 2]
        return pltpu.make_async_copy(cache_ref.at[page, pl.ds(2 * LANES, 2 * LANES), :], buf.at[slot, n],
                                     sem.at[slot])

    def start_all(bb, slot):
        def go(n, carry):
            block_copy(bb, n, slot).start()
            return carry
        lax.fori_loop(0, N_FETCH, go, 0)

    def wait_all(bb, slot):
        def go(n, carry):
            block_copy(bb, n, slot).wait()
            return carry
        lax.fori_loop(0, N_FETCH, go, 0)

    slot = b % 2

    @pl.when(b == 0)
    def _prologue():
        start_all(0, 0)

    @pl.when(b + 1 < nb)
    def _prefetch():
        start_all(b + 1, 1 - slot)

    qbd = _block_diag_queries(nq_ref[0], SEQ_PAD)
    tok, head = _sample_row_ids()
    owner = (head // GQ) * N_NEW + tok

    wc = winc_ref[0]
    wn = winn_ref[0]
    r_i = lax.broadcasted_iota(I32, (N_QROWS, WINDOW), 1)
    dist1 = WINDOW + tok - r_i
    s1 = _dot(qbd, wc[0:LANES, :].astype(BF16)) + _sample_bias(dist1, rb_ref)
    s1 = jnp.where((dist1 >= 0) & (dist1 <= WINDOW), s1, NEG_INF)
    n_i = lax.broadcasted_iota(I32, (N_QROWS, SEQ_PAD), 1) - (SEQ_PAD - N_NEW)
    dist2 = tok - n_i
    new_ok = (n_i >= 0) & (dist2 >= 0)
    bias2 = _sample_bias(dist2, rb_ref)
    s2 = jnp.where(new_ok, _dot_nt(qbd, wn[:, 0:LANES].astype(BF16)) + bias2, NEG_INF)
    p1, p2 = _softmax_two(s1, s2)
    o_w = (_dot_nt(p1.astype(BF16), wc[LANES:2 * LANES, :].astype(BF16))
           + _dot(p2.astype(BF16), wn[:, LANES:2 * LANES].astype(BF16)))

    wait_all(b, slot)

    def owner_tiles(o, r0):
        return jnp.concatenate([buf[slot, o * N_PICK + k, r0:r0 + LANES, :] for k in range(N_PICK)],
                               axis=1).astype(BF16)

    s_f = jnp.full((N_QROWS, OWNER_KEYS), NEG_INF, F32)
    for o in range(N_OWNERS):
        s_f = jnp.where(owner == o, _dot(qbd, owner_tiles(o, 0)), s_f)
    spread = (lax.broadcasted_iota(I32, (LANES, OWNER_KEYS), 0)
              == lax.broadcasted_iota(I32, (LANES, OWNER_KEYS), 1) // PAGE_SIZE).astype(F32)
    blk_of_key = _dot(idv_ref[0].astype(F32), spread, hi)
    blk_rows = jnp.concatenate([blk_of_key[(h // GQ) * SEQ_PAD:(h // GQ + 1) * SEQ_PAD]
                                for h in range(N_NSA_HEADS)], axis=0).astype(I32)
    blocks_per_page = PAGE_SIZE // SEL_BLOCK
    in_page = lax.broadcasted_iota(I32, (N_QROWS, OWNER_KEYS), 1) % PAGE_SIZE
    in_block = in_page // SEL_BLOCK == blk_rows % blocks_per_page
    dist_s = past_len + tok - ((blk_rows // blocks_per_page) * PAGE_SIZE + in_page)
    s_f = jnp.where((tok >= 0) & in_block & (dist_s >= 0), s_f + _sample_bias(dist_s, rb_ref), NEG_INF)
    kvn = kvn_ref[0]
    s_n = jnp.where(new_ok, _dot_nt(qbd, kvn[:, 2 * LANES:3 * LANES].astype(BF16)) + bias2, NEG_INF)
    p_f, p_n = _softmax_two(s_f, s_n)
    o_s = _dot(p_n.astype(BF16), kvn[:, 3 * LANES:4 * LANES].astype(BF16))
    for o in range(N_OWNERS):
        o_s = o_s + _dot_nt(jnp.where(owner == o, p_f, 0.0).astype(BF16), owner_tiles(o, LANES))

    o_c = oc_ref[0]
    gates = jax.nn.sigmoid(zg_ref[0])
    is_new = lax.broadcasted_iota(I32, (SEQ_PAD, HEAD_DIM), 0) >= SEQ_PAD - N_NEW
    for h in range(N_NSA_HEADS):
        mix = _gated_mix(gates, o_c, o_s, o_w, h, SEQ_PAD)
        out_ref[0, :, h * HEAD_DIM:(h + 1) * HEAD_DIM] = jnp.where(is_new, mix, 0.0)


def _nsa_sel(page_table, ids, ids_rows, rel_bias, nq_s, kv_s, win_cache, win_s, zg_s, o_c, cache3):
    dec_b, n_pages = page_table.shape
    past_len = n_pages * PAGE_SIZE
    seq = lambda w: pl.BlockSpec((1, SEQ_PAD, w), lambda b, pt, ix: (b, 0, 0))
    grid_spec = pltpu.PrefetchScalarGridSpec(
        num_scalar_prefetch=2,
        grid=(dec_b,),
        in_specs=[pl.BlockSpec(memory_space=pltpu.SMEM),
                  seq(NSA_W), seq(4 * KV_W),
                  pl.BlockSpec((1, 2 * KV_W, WINDOW), lambda b, pt, ix: (b, 0, 0)),
                  seq(2 * KV_W), seq(LANES),
                  pl.BlockSpec((1, N_QROWS, LANES), lambda b, pt, ix: (b, 0, 0)),
                  pl.BlockSpec((1, N_KV_HEADS * SEQ_PAD, LANES), lambda b, pt, ix: (b, 0, 0)),
                  pl.BlockSpec(memory_space=pl.ANY)],
        out_specs=seq(NSA_W),
        scratch_shapes=[pltpu.VMEM((2, N_FETCH, 2 * LANES, PAGE_SIZE), F32),
                        pltpu.SemaphoreType.DMA((2,))],
    )
    return pl.pallas_call(
        functools.partial(_nsa_sel_body, n_pages=n_pages, past_len=past_len),
        grid_spec=grid_spec,
        out_shape=jax.ShapeDtypeStruct((dec_b, SEQ_PAD, NSA_W), F32),
        compiler_params=pltpu.CompilerParams(dimension_semantics=("arbitrary",),
                                             vmem_limit_bytes=VMEM_LIMIT),
        name="nsa_sample_sel",
    )(page_table.reshape(-1), ids.reshape(-1), rel_bias, nq_s, kv_s, win_cache, win_s, zg_s, o_c, ids_rows, cache3)


FF_CHUNK = 896


def _ffn_body(*refs, tm, has_override):
    if has_override:
        (x_ref, mm_ref, mn_ref, ov_ref, woa_ref, wob_ref, l1w_ref, l1b_ref, wup_ref, bup_ref, cw_ref, cb_ref,
         wdn_ref, bdn_ref, l2w_ref, l2b_ref, y_ref, tail_ref, carry_sc, ubuf) = refs
    else:
        (x_ref, mm_ref, mn_ref, woa_ref, wob_ref, l1w_ref, l1b_ref, wup_ref, bup_ref, cw_ref, cb_ref,
         wdn_ref, bdn_ref, l2w_ref, l2b_ref, y_ref, tail_ref, carry_sc, ubuf) = refs
    j = pl.program_id(1)

    @pl.when(j == 0)
    def _init():
        carry_sc[...] = jnp.zeros(carry_sc.shape, F32)

    x = x_ref[0]
    h = ALPHA * x + (_dot(mm_ref[0].astype(BF16), woa_ref[...]) + _dot(mn_ref[0].astype(BF16), wob_ref[...]))
    x1 = _layer_norm(h, l1w_ref[...], l1b_ref[...])
    x1b = x1.astype(BF16)
    if has_override:
        state_row = lax.broadcasted_iota(I32, (tm, FF_CHUNK), 0) % SEQ_PAD < SEQ_PAD - N_NEW

    def conv_half(c0):
        cols = slice(c0, c0 + FF_CHUNK)
        u = _dot(x1b, wup_ref[:, cols]) + bup_ref[:, cols]
        if has_override:
            u = jnp.where(state_row, ov_ref[0, :, cols], u)
        ubuf[0:8, :] = carry_sc[:, cols]
        ubuf[8:8 + tm, :] = u
        out = cb_ref[:, cols] + ubuf[6:6 + tm, :] * cw_ref[0:1, cols]
        out = out + ubuf[7:7 + tm, :] * cw_ref[1:2, cols]
        out = out + u * cw_ref[2:3, cols]
        carry_sc[:, cols] = u[tm - 8:tm, :]
        tail_ref[0, 0, :, cols] = u if has_override else u[tm - 8:tm, :]
        return out

    ff = jnp.zeros((tm, D_MODEL), F32)
    for c in range(D_FF // FF_CHUNK):
        ga = conv_half(c * FF_CHUNK)
        gb = conv_half(D_FF + c * FF_CHUNK)
        gelu = ga * (0.5 * (1.0 + jnp.tanh(math.sqrt(2.0 / math.pi) * (ga + 0.044715 * (ga * ga * ga)))))
        ff = ff + _dot((gelu * gb).astype(BF16), wdn_ref[c * FF_CHUNK:(c + 1) * FF_CHUNK, :])
    ff = ff + bdn_ref[...]
    y_ref[0] = _layer_norm(ALPHA * x1 + ff, l2w_ref[...], l2b_ref[...])


def _ffn(x, mix_m, mix_n, override, wts):
    bsz, t_len, _ = x.shape
    tm = min(t_len, 512)
    nt = t_len // tm
    has_override = override is not None
    tail_rows = tm if has_override else 8
    row = lambda w: pl.BlockSpec((1, tm, w), lambda b, j: (b, j, 0))
    const = lambda shape: pl.BlockSpec(shape, lambda b, j: (0,) * len(shape), pipeline_mode=pl.Buffered(1))
    in_specs = [row(D_MODEL), row(MLSTM_W), row(NSA_W)]
    args = [x, mix_m, mix_n]
    if has_override:
        in_specs.append(row(2 * D_FF))
        args.append(override)
    in_specs += [const(w.shape) for w in wts]
    args += list(wts)
    y, tail = pl.pallas_call(
        functools.partial(_ffn_body, tm=tm, has_override=has_override),
        grid=(bsz, nt),
        in_specs=in_specs,
        out_specs=[row(D_MODEL), pl.BlockSpec((1, 1, tail_rows, 2 * D_FF), lambda b, j: (b, j, 0, 0))],
        out_shape=[jax.ShapeDtypeStruct((bsz, t_len, D_MODEL), F32),
                   jax.ShapeDtypeStruct((bsz, nt, tail_rows, 2 * D_FF), F32)],
        scratch_shapes=[pltpu.VMEM((8, 2 * D_FF), F32), pltpu.VMEM((8 + tm, FF_CHUNK), F32)],
        compiler_params=pltpu.CompilerParams(dimension_semantics=("arbitrary", "arbitrary"),
                                             vmem_limit_bytes=VMEM_LIMIT),
        name="ffn",
    )(*args)
    return y, tail


def _permute_in_proj(w_in, b_in):
    gates_a = slice(2 * MLSTM_W + 2 * MLSTM_W, 2 * MLSTM_W + 2 * MLSTM_W + 2 * N_MLSTM_HEADS)
    nsa0 = gates_a.stop
    nsa1 = nsa0 + NSA_W + 6 * KV_W

    def perm(a):
        pad = IN_PAD - a.shape[-1]
        parts = [a[..., :gates_a.start], a[..., nsa0:nsa1], a[..., gates_a], a[..., nsa1:]]
        parts.append(jnp.zeros(a.shape[:-1] + (pad,), a.dtype))
        return jnp.concatenate(parts, axis=-1)

    return perm(w_in).astype(BF16), perm(b_in).reshape(1, IN_PAD)


def _front_pad(a, n):
    return jnp.pad(a, ((0, 0), (n, 0), (0, 0)))


def kernel(x_prompt, x_sample, cache_nsa_kv, cache_win_kv, state_mlstm_c, state_mlstm_n, state_mlstm_m,
           state_mlstm_conv, state_ffn_conv, page_table,
           w_in, b_in, mlstm_conv_w, mlstm_conv_b, mlstm_norm_w, nsa_pool_w, rel_bias, w_out,
           ln1_w, ln1_b, w_up, b_up, ffn_conv_w, ffn_conv_b, w_down, b_down, ln2_w, ln2_b):
    bsz, t_len, _ = x_prompt.shape
    dec_b, n_new, _ = x_sample.shape
    assert n_new == N_NEW and cache_win_kv.shape[1] == WINDOW

    w_p, b_p = _permute_in_proj(w_in, b_in)
    pool_w2 = jnp.concatenate([jnp.repeat(nsa_pool_w[0], HEAD_DIM, axis=-1),
                               jnp.repeat(nsa_pool_w[1], HEAD_DIM, axis=-1)], axis=-1)
    row = lambda v: v.reshape(1, -1)
    ffn_w = (w_out[:MLSTM_W].astype(BF16), w_out[MLSTM_W:].astype(BF16), row(ln1_w), row(ln1_b),
             w_up.astype(BF16), row(b_up), ffn_conv_w, row(ffn_conv_b), w_down.astype(BF16), row(b_down),
             row(ln2_w), row(ln2_b))

    zqk, zv, zo, nq, kv, win, zg, kv_t = _proj_in(x_prompt.reshape(bsz * t_len, D_MODEL), w_p, b_p, seq_len=t_len)
    seq3 = lambda a: a.reshape(bsz, t_len, a.shape[-1])
    zqk, zv, zo, nq, kv, win, zg = map(seq3, (zqk, zv, zo, nq, kv, win, zg))
    mix_m, c_p, n_p, m_p = _mlstm(zqk, zv, zo, zg, None, mlstm_conv_w, mlstm_conv_b, mlstm_norm_w, 0)
    mix_n = _nsa_prompt(nq, kv, win, zg, pool_w2, rel_bias)
    y_p, tail_p = _ffn(x_prompt, mix_m, mix_n, None, ffn_w)
    kv_p = jnp.transpose(kv_t.reshape(bsz, 4, N_KV_HEADS, HEAD_DIM, t_len), (0, 4, 1, 2, 3))
    win_p = win[:, t_len - min(WINDOW, t_len):].reshape(bsz, -1, 2, N_KV_HEADS, HEAD_DIM)
    mconv_p = zqk[:, t_len - (MLSTM_CONV - 1):]
    fconv_p = tail_p[:, -1, 8 - (FFN_CONV - 1):]

    pad = SEQ_PAD - N_NEW
    xs = _front_pad(x_sample, pad)
    szqk, szv, szo, snq, skv, swin, szg = _proj_in(xs.reshape(dec_b * SEQ_PAD, D_MODEL), w_p, b_p)
    sseq = lambda a: a.reshape(dec_b, SEQ_PAD, a.shape[-1])
    szqk, szv, szo, snq, skv, swin, szg = map(sseq, (szqk, szv, szo, snq, skv, swin, szg))
    lead = MLSTM_CHUNK - N_NEW
    new = lambda a: a[:, pad:]
    m_qk = jnp.concatenate([jnp.zeros((dec_b, lead - (MLSTM_CONV - 1), 2 * MLSTM_W), F32),
                            state_mlstm_conv.astype(F32), new(szqk)], axis=1)
    mix_ms, c_s, n_s, m_s = _mlstm(m_qk, _front_pad(new(szv), lead), _front_pad(new(szo), lead),
                                   _front_pad(new(szg), lead),
                                   (state_mlstm_c, state_mlstm_n, state_mlstm_m),
                                   mlstm_conv_w, mlstm_conv_b, mlstm_norm_w, lead)
    mix_ms = mix_ms[:, MLSTM_CHUNK - SEQ_PAD:]
    cache3 = jnp.transpose(cache_nsa_kv, (0, 2, 3, 4, 1)).reshape(cache_nsa_kv.shape[0], 4 * KV_W, PAGE_SIZE)
    o_c, ids_rows = _nsa_cmp(page_table, rel_bias, snq, pool_w2, cache3)
    ids = ids_rows.reshape(dec_b, N_KV_HEADS, SEQ_PAD, LANES)[:, :, pad:, :N_PICK]
    win_cache = jnp.transpose(cache_win_kv, (0, 2, 3, 4, 1)).reshape(dec_b, 2 * KV_W, WINDOW)
    mix_ns = _nsa_sel(page_table, ids, ids_rows, rel_bias, snq, skv, win_cache, swin, szg, o_c, cache3)
    override = jnp.concatenate([jnp.zeros((dec_b, pad - (FFN_CONV - 1), 2 * D_FF), F32),
                                state_ffn_conv.astype(F32),
                                jnp.zeros((dec_b, N_NEW, 2 * D_FF), F32)], axis=1)
    flat = lambda a: a.reshape(1, dec_b * SEQ_PAD, a.shape[-1])
    y_s8, u_s = _ffn(flat(xs), flat(mix_ms), flat(mix_ns), flat(override), ffn_w)
    y_s = y_s8.reshape(dec_b, SEQ_PAD, D_MODEL)[:, pad:]
    kv_s = new(skv).reshape(dec_b, N_NEW, 4, N_KV_HEADS, HEAD_DIM)
    win_s = jnp.concatenate([cache_win_kv[:, N_NEW:], new(swin).reshape(dec_b, N_NEW, 2, N_KV_HEADS, HEAD_DIM)],
                            axis=1)
    mconv_s = szqk[:, SEQ_PAD - (MLSTM_CONV - 1):]
    fconv_s = u_s.reshape(dec_b, SEQ_PAD, 2 * D_FF)[:, SEQ_PAD - (FFN_CONV - 1):]
    return (y_p, y_s, kv_p, kv_s, win_p, win_s, c_p, c_s, n_p, n_s, m_p, m_s, mconv_p, mconv_s, fconv_p, fconv_s)
```

```python
import functools
import math

import jax
import jax.numpy as jnp
from jax import lax
from jax.experimental import pallas as pl
from jax.experimental.pallas import tpu as pltpu

F32 = jnp.float32
BF16 = jnp.bfloat16
I32 = jnp.int32

D_MODEL = 1024
HEAD_DIM = 64
N_MLSTM_HEADS = 8
N_NSA_HEADS = 8
N_KV_HEADS = 2
GQ = N_NSA_HEADS // N_KV_HEADS
MLSTM_W = N_MLSTM_HEADS * HEAD_DIM
NSA_W = N_NSA_HEADS * HEAD_DIM
KV_W = N_KV_HEADS * HEAD_DIM
MLSTM_CHUNK = 64
MLSTM_CONV = 4
CMP_BLOCK = 32
SEL_BLOCK = 64
SEL_TOPK = 16
WINDOW = 512
NSA_QBLOCK = 64
N_BUCKETS = 32
MAX_DISTANCE = 128
D_FF = 2688
FFN_CONV = 3
DEPTH = 1
ALPHA = (2.0 * DEPTH) ** 0.25
LN_EPS = 1e-5
PAGE_SIZE = 128

LANES = 128
VMEM_LIMIT = 56 * 1024 * 1024
NEG_INF = float("-inf")

SEQ_PAD = 8
N_NEW = 4
N_QROWS = N_NSA_HEADS * SEQ_PAD


def _bucket_thresholds():
    max_exact = N_BUCKETS // 2
    thr = list(range(max_exact + 1))
    for k in range(1, N_BUCKETS - max_exact):
        thr.append(int(math.ceil(max_exact * (MAX_DISTANCE / max_exact) ** (k / (N_BUCKETS - max_exact)))))
    return tuple(thr)


BUCKET_THR = _bucket_thresholds()
FAR_DIST = BUCKET_THR[N_BUCKETS - 1]


def _dot(a, b, precision=None):
    return jnp.dot(a, b, preferred_element_type=F32, precision=precision)


def _dot_nt(a, b, precision=None):
    return lax.dot_general(a, b, (((1,), (1,)), ((), ())), preferred_element_type=F32, precision=precision)


def _dot_tn(a, b, precision=None):
    return lax.dot_general(a, b, (((0,), (0,)), ((), ())), preferred_element_type=F32, precision=precision)


def _layer_norm(x, w, b):
    mu = jnp.mean(x, -1, keepdims=True)
    xc = x - mu
    var = jnp.mean(xc * xc, -1, keepdims=True)
    return xc * lax.rsqrt(var + LN_EPS) * w + b


def _log_sigmoid(x):
    return jnp.minimum(x, 0.0) - jnp.log1p(jnp.exp(-jnp.abs(x)))


def _bias_from_dist(dist, rb_ref, head):
    acc = jnp.full(dist.shape, rb_ref[0, head], F32)
    for k in range(1, N_BUCKETS):
        acc = jnp.where(dist >= BUCKET_THR[k], rb_ref[k, head], acc)
    return acc


def _masked_softmax(s):
    mx = jnp.max(s, -1, keepdims=True)
    mx = jnp.where(mx > NEG_INF, mx, 0.0)
    e = jnp.exp(s - mx)
    den = jnp.sum(e, -1, keepdims=True)
    return e / jnp.maximum(den, 1e-30)


_C_QK, _C_V, _C_O, _C_NQ, _C_KV, _C_WIN, _C_G = 0, 1024, 1536, 2048, 2560, 3072, 3328
IN_PAD = 3456
G_IGATE, G_FGATE, G_NSA = 0, N_MLSTM_HEADS, 2 * N_MLSTM_HEADS


def _proj_in_body(*refs, kv_transposed):
    if kv_transposed:
        x_ref, w_ref, b_ref, wkvt_ref, bkv_ref, qk_ref, v_ref, o_ref, nq_ref, kv_ref, win_ref, g_ref, kvt_ref = refs
    else:
        x_ref, w_ref, b_ref, qk_ref, v_ref, o_ref, nq_ref, kv_ref, win_ref, g_ref = refs
    xb = x_ref[...].astype(BF16)
    z = _dot(xb, w_ref[...]) + b_ref[...]
    qk_ref[...] = z[:, _C_QK:_C_V]
    v_ref[...] = z[:, _C_V:_C_O]
    o_ref[...] = z[:, _C_O:_C_NQ]
    nq_ref[...] = z[:, _C_NQ:_C_KV]
    kv_ref[...] = z[:, _C_KV:_C_WIN]
    win_ref[...] = z[:, _C_WIN:_C_G]
    g_ref[...] = z[:, _C_G:IN_PAD]
    if kv_transposed:
        kvt_ref[0] = _dot_nt(wkvt_ref[...], xb) + bkv_ref[...]


def _proj_in(x2d, w_p, b_p, seq_len=None):
    rows = x2d.shape[0]
    tm = min(rows, 256)
    widths = (1024, 512, 512, 512, 512, 256, 128)
    const = lambda i: (0, 0)
    const_spec = lambda shape: pl.BlockSpec(shape, const, pipeline_mode=pl.Buffered(1))
    in_specs = [pl.BlockSpec((tm, D_MODEL), lambda i: (i, 0)), const_spec((D_MODEL, IN_PAD)), const_spec((1, IN_PAD))]
    args = [x2d, w_p, b_p]
    out_specs = [pl.BlockSpec((tm, w), lambda i: (i, 0)) for w in widths]
    out_shape = [jax.ShapeDtypeStruct((rows, w), F32) for w in widths]
    if seq_len is not None:
        tiles = seq_len // tm
        in_specs += [const_spec((4 * KV_W, D_MODEL)), const_spec((4 * KV_W, 1))]
        args += [jnp.transpose(w_p[:, _C_KV:_C_WIN]), b_p[0, _C_KV:_C_WIN].reshape(4 * KV_W, 1)]
        out_specs.append(pl.BlockSpec((1, 4 * KV_W, tm), lambda i: (i // tiles, 0, i % tiles)))
        out_shape.append(jax.ShapeDtypeStruct((rows // seq_len, 4 * KV_W, seq_len), F32))
    return pl.pallas_call(
        functools.partial(_proj_in_body, kv_transposed=seq_len is not None),
        grid=(rows // tm,),
        in_specs=in_specs,
        out_specs=out_specs,
        out_shape=out_shape,
        compiler_params=pltpu.CompilerParams(dimension_semantics=("arbitrary",),
                                             vmem_limit_bytes=VMEM_LIMIT),
        name="proj_in",
    )(*args)


N_PAIRS = N_MLSTM_HEADS // 2
GATE_LANE0 = N_MLSTM_HEADS
MLSTM_SEQS_PER_STEP = 4


def _dot_split(x, onehot_bf16, pieces):
    acc = None
    rem = x
    for n in range(pieces):
        part = rem.astype(BF16)
        d = _dot(part, onehot_bf16)
        acc = d if acc is None else acc + d
        if n + 1 < pieces:
            rem = rem - part.astype(F32)
    return acc


def _mlstm_body(*refs, tb, nbg, n_dummy, has_state):
    if has_state:
        (zqk_ref, zv_ref, zo_ref, zg_ref, c0_ref, n0_ref, m0_ref, cw_ref, cb_ref, nw_ref,
         mix_ref, c_out, n_out, m_out, ubuf, qk_sc, ct_sc, n_sc, m_sc) = refs
    else:
        (zqk_ref, zv_ref, zo_ref, zg_ref, cw_ref, cb_ref, nw_ref,
         mix_ref, c_out, n_out, m_out, ubuf, qk_sc, ct_sc, n_sc, m_sc) = refs
    L = MLSTM_CHUNK
    D = HEAD_DIM
    j = pl.program_id(1)
    hi = lax.Precision.HIGHEST
    head_lanes = slice(GATE_LANE0, GATE_LANE0 + N_MLSTM_HEADS)
    pairs = [(bi, p) for bi in range(nbg) for p in range(N_PAIRS)]

    @pl.when(j == 0)
    def _init():
        ubuf[:, 0:8, :] = jnp.zeros((nbg, 8, 2 * MLSTM_W), F32)
        ct_sc[...] = jnp.zeros(ct_sc.shape, F32)
        n_sc[...] = jnp.zeros(n_sc.shape, F32)
        m_sc[...] = jnp.zeros(m_sc.shape, F32)
        if has_state:
            for bi in range(nbg):
                for h in range(N_MLSTM_HEADS):
                    p, hh = divmod(h, 2)
                    hs = slice(hh * D, (hh + 1) * D)
                    ct_sc[bi, p, hs, hs] = c0_ref[bi, h]
                    n_sc[bi, p, :, hs] = n0_ref[bi, h:h + 1, :]
                m_sc[bi, :, head_lanes] = m0_ref[bi]

    for bi in range(nbg):
        u = zqk_ref[bi]
        ubuf[bi, 8:8 + tb, :] = u
        conv = cb_ref[...] + ubuf[bi, 5:5 + tb, :] * cw_ref[0:1, :]
        conv = conv + ubuf[bi, 6:6 + tb, :] * cw_ref[1:2, :]
        conv = conv + ubuf[bi, 7:7 + tb, :] * cw_ref[2:3, :]
        conv = conv + u * cw_ref[3:4, :]
        ubuf[bi, 5:8, :] = u[tb - 3:tb, :]
        qk_sc[bi] = conv * jax.nn.sigmoid(conv)

    tri = (lax.broadcasted_iota(I32, (L, L), 1) <= lax.broadcasted_iota(I32, (L, L), 0)).astype(F32)
    row_l = lax.broadcasted_iota(I32, (L, LANES), 0)
    lane = lax.broadcasted_iota(I32, (L, LANES), 1)
    low = lane < D
    key_tok = lane % D
    causal_pair = key_tok <= row_l
    eye_pair = key_tok == row_l
    spread = (lax.broadcasted_iota(I32, (LANES, MLSTM_W), 0) - GATE_LANE0
              == lax.broadcasted_iota(I32, (LANES, MLSTM_W), 1) // D).astype(BF16)
    same_head = (lax.broadcasted_iota(I32, (LANES, LANES), 0) // D
                 == lax.broadcasted_iota(I32, (LANES, LANES), 1) // D)
    ones_bd = same_head.astype(BF16)
    slab_rows = 3 * L + 8

    def chunk(ci, carry):
        r0 = pl.multiple_of(ci * L, L)
        rows = pl.ds(r0, L)
        gates = []
        for bi in range(nbg):
            g = zg_ref[bi, rows, :]
            if n_dummy:
                dummy = (row_l + ci * L) < n_dummy
                g = jnp.where(dummy, jnp.where(lane < G_FGATE, -1e30, 1e30), g)
            gates.append(g)
        b_all = _dot(tri, jnp.concatenate([_log_sigmoid(g) for g in gates], axis=1), hi)
        stacks = []
        for bi in range(nbg):
            b = b_all[:, bi * LANES:(bi + 1) * LANES]
            r = pltpu.roll(gates[bi], GATE_LANE0 - G_IGATE, 1) - b
            cm = r
            for sh in (1, 2, 4, 8, 16, 32):
                cm = jnp.where(row_l >= sh, jnp.maximum(cm, pltpu.roll(cm, sh, 0)), cm)
            m_st = m_sc[bi]
            mx = jnp.maximum(m_st, cm)
            mx_end = mx[L - 1:L, :]
            m_sc[bi] = b[L - 1:L, :] + mx_end
            stacks += [mx, r, b + mx, m_st, mx_end, jnp.zeros((6, LANES), F32)]
        x_all = _dot_split(jnp.concatenate(stacks, axis=0), spread, 2)

        work = []
        den_terms = []
        for bi, p in pairs:
            cs = slice(p * LANES, (p + 1) * LANES)
            x0 = bi * slab_rows
            mx_p, rs_p, mt_p = x_all[x0:x0 + L, cs], x_all[x0 + L:x0 + 2 * L, cs], x_all[x0 + 2 * L:x0 + 3 * L, cs]
            mst_p, mxe_p = x_all[x0 + 3 * L:x0 + 3 * L + 1, cs], x_all[x0 + 3 * L + 1:x0 + 3 * L + 2, cs]
            wi_p = jnp.exp(mst_p - mx_p)
            fl_p = jnp.exp(-mt_p)
            wk_p = jnp.exp(rs_p - mxe_p)
            sc_p = jnp.exp(mst_p - mxe_p)
            r_row = jnp.sum(jnp.where(eye_pair, rs_p, 0.0), axis=0, keepdims=True)
            w_intra = jnp.exp(jnp.where(causal_pair, r_row - mx_p, NEG_INF))
            q_p = qk_sc[bi, rows, cs]
            k_p = qk_sc[bi, rows, MLSTM_W + p * LANES:MLSTM_W + (p + 1) * LANES] * (D ** -0.5)
            v_p = zv_ref[bi, rows, cs]
            qb = q_p.astype(BF16)
            k_bd = jnp.concatenate([jnp.where(low, k_p, 0.0), jnp.where(low, 0.0, k_p)], axis=0)
            v_bd = jnp.concatenate([jnp.where(low, v_p, 0.0), jnp.where(low, 0.0, v_p)], axis=0).astype(BF16)
            c_t = ct_sc[bi, p]
            both = _dot_nt(qb, jnp.concatenate([k_bd, c_t], axis=0).astype(BF16))
            a = w_intra * both[:, 0:LANES]
            num = _dot(a.astype(BF16), v_bd) + both[:, LANES:2 * LANES] * wi_p
            den_terms.append(a + wi_p * (q_p * n_sc[bi, p]))
            work.append((num, fl_p, k_p * wk_p, v_p, sc_p, c_t))
        den_all = _dot_split(jnp.concatenate(den_terms, axis=0), ones_bd, 2)

        hhs = []
        for n, (num, fl_p, _, _, _, _) in enumerate(work):
            hhs.append(num / jnp.maximum(jnp.abs(den_all[n * L:(n + 1) * L, :]), fl_p))
        mu_all = _dot_split(jnp.concatenate(hhs, axis=0), ones_bd, 1) * (1.0 / D)
        hcs = [hh - mu_all[n * L:(n + 1) * L, :] for n, hh in enumerate(hhs)]
        var_all = _dot_split(jnp.concatenate([hc * hc for hc in hcs], axis=0), ones_bd, 1) * (1.0 / D)
        for n, (bi, p) in enumerate(pairs):
            cs = slice(p * LANES, (p + 1) * LANES)
            hn = hcs[n] * lax.rsqrt(var_all[n * L:(n + 1) * L, :] + LN_EPS)
            og = jax.nn.sigmoid(zo_ref[bi, rows, cs])
            mix_ref[bi, rows, cs] = og * (hn * nw_ref[:, cs])
            _, _, kw, v_p, sc_p, c_t = work[n]
            upd = _dot_tn(v_p.astype(BF16), kw.astype(BF16))
            ct_sc[bi, p] = sc_p * c_t + jnp.where(same_head, upd, 0.0)
            n_sc[bi, p] = sc_p * n_sc[bi, p] + jnp.sum(kw, axis=0, keepdims=True)
        return carry

    lax.fori_loop(0, tb // L, chunk, 0)

    @pl.when(j == pl.num_programs(1) - 1)
    def _final():
        for bi in range(nbg):
            for h in range(N_MLSTM_HEADS):
                p, hh = divmod(h, 2)
                hs = slice(hh * D, (hh + 1) * D)
                c_out[bi, h] = ct_sc[bi, p, hs, hs]
                n_out[bi, h:h + 1, :] = n_sc[bi, p, :, hs]
            m_out[bi] = m_sc[bi, :, head_lanes]


def _mlstm(zqk, zv, zo, zg, state, conv_w, conv_b, norm_w, n_dummy):
    bsz, t_len, _ = zqk.shape
    tb = min(t_len, 256)
    nbg = MLSTM_SEQS_PER_STEP
    assert bsz % nbg == 0
    has_state = state is not None
    row = lambda w: pl.BlockSpec((nbg, tb, w), lambda b, j: (b, j, 0))
    per_b = lambda shape: pl.BlockSpec((nbg,) + shape, lambda b, j: (b,) + (0,) * len(shape))
    const = lambda shape: pl.BlockSpec(shape, lambda b, j: (0,) * len(shape))
    in_specs = [row(2 * MLSTM_W), row(MLSTM_W), row(MLSTM_W), row(LANES)]
    args = [zqk, zv, zo, zg]
    if has_state:
        c0, n0, m0 = state
        in_specs += [per_b((N_MLSTM_HEADS, HEAD_DIM, HEAD_DIM)), per_b((N_MLSTM_HEADS, HEAD_DIM)),
                     per_b((1, N_MLSTM_HEADS))]
        args += [c0, n0, m0.reshape(bsz, 1, N_MLSTM_HEADS)]
    in_specs += [const((MLSTM_CONV, 2 * MLSTM_W)), const((1, 2 * MLSTM_W)), const((1, MLSTM_W))]
    args += [conv_w, conv_b.reshape(1, -1), norm_w.reshape(1, -1)]
    mix, c_new, n_new, m_new = pl.pallas_call(
        functools.partial(_mlstm_body, tb=tb, nbg=nbg, n_dummy=n_dummy, has_state=has_state),
        grid=(bsz // nbg, t_len // tb),
        in_specs=in_specs,
        out_specs=[row(MLSTM_W), per_b((N_MLSTM_HEADS, HEAD_DIM, HEAD_DIM)),
                   per_b((N_MLSTM_HEADS, HEAD_DIM)), per_b((1, N_MLSTM_HEADS))],
        out_shape=[jax.ShapeDtypeStruct((bsz, t_len, MLSTM_W), F32),
                   jax.ShapeDtypeStruct((bsz, N_MLSTM_HEADS, HEAD_DIM, HEAD_DIM), F32),
                   jax.ShapeDtypeStruct((bsz, N_MLSTM_HEADS, HEAD_DIM), F32),
                   jax.ShapeDtypeStruct((bsz, 1, N_MLSTM_HEADS), F32)],
        scratch_shapes=[pltpu.VMEM((nbg, 8 + tb, 2 * MLSTM_W), F32),
                        pltpu.VMEM((nbg, tb, 2 * MLSTM_W), F32),
                        pltpu.VMEM((nbg, N_PAIRS, LANES, LANES), F32),
                        pltpu.VMEM((nbg, N_PAIRS, 1, LANES), F32),
                        pltpu.VMEM((nbg, 1, LANES), F32)],
        compiler_params=pltpu.CompilerParams(dimension_semantics=("arbitrary", "arbitrary"),
                                             vmem_limit_bytes=VMEM_LIMIT),
        name="mlstm",
    )(*args)
    return mix, c_new, n_new, m_new.reshape(bsz, N_MLSTM_HEADS)


def _block_diag_queries(x, n_rows, heads=tuple(range(N_NSA_HEADS)), scale=HEAD_DIM ** -0.5):
    lane = lax.broadcasted_iota(I32, (n_rows, LANES), 1)
    low = lane < HEAD_DIM
    pieces = []
    for h in heads:
        pair = x[:, (h // 2) * LANES:(h // 2 + 1) * LANES]
        src_low = h % 2 == 0
        dst_low = h // GQ == 0
        if src_low != dst_low:
            pair = pltpu.roll(pair, HEAD_DIM, 1)
        pieces.append(jnp.where(low if dst_low else ~low, pair, 0.0))
    return (jnp.concatenate(pieces, axis=0) * scale).astype(BF16)


def _group_lanes(o, g):
    return o[:, g * HEAD_DIM:(g + 1) * HEAD_DIM]


def _gated_mix(gates, o_c, o_s, o_w, h, n_rows):
    g = h // GQ
    rs = slice(h * n_rows, (h + 1) * n_rows)
    col = lambda branch: gates[:, G_NSA + branch * N_NSA_HEADS + h:G_NSA + branch * N_NSA_HEADS + h + 1]
    return (col(0) * _group_lanes(o_c[rs], g) + col(1) * _group_lanes(o_s[rs], g)
            + col(2) * _group_lanes(o_w[rs], g))


SEL_TILE = 256
SEL_TILE_BLOCKS = SEL_TILE // SEL_BLOCK
WIN_BAND = WINDOW + NSA_QBLOCK
N_COLS = N_NSA_HEADS * NSA_QBLOCK
PROMPT_HEADS = tuple(GQ * (jj % N_KV_HEADS) + jj // N_KV_HEADS for jj in range(N_NSA_HEADS))
SEL_REL0 = 2 * SEL_TILE - SEL_BLOCK
SEL_BIAS_ROWS = SEL_REL0 + SEL_TILE


LOG2E = math.log2(math.e)


def _softmax2_rows(s):
    mx = jnp.max(s, 0, keepdims=True)
    mx = jnp.where(mx > NEG_INF, mx, 0.0)
    e = jnp.exp2(s - mx)
    return e / jnp.maximum(jnp.sum(e, 0, keepdims=True), 1e-30)


def _nsa_prompt_body(rb_ref, nq_ref, kv_ref, win_ref, zg_ref, pw_ref, out_ref,
                     ks_sc, vst_sc, wk_sc, wv_sc, kc_sc, pv_sc, vct_sc, bs_sc, bw_sc, fc_sc,
                     m_sc, l_sc, acc_sc, *, t_len):
    QB = NSA_QBLOCK
    n_cmp = t_len // CMP_BLOCK
    n_sel = t_len // SEL_BLOCK
    i = pl.program_id(1)
    hi = lax.Precision.HIGHEST

    @pl.when(i == 0)
    def _init():
        ks_sc[0:SEL_TILE, :] = jnp.zeros((SEL_TILE, LANES), BF16)
        vst_sc[0] = jnp.zeros((LANES, SEL_TILE), BF16)
        wk_sc[0:WINDOW, :] = jnp.zeros((WINDOW, LANES), BF16)
        wv_sc[0:WINDOW, :] = jnp.zeros((WINDOW, LANES), BF16)
        rows_per = 2 * SEL_TILE
        blocks_per = rows_per // CMP_BLOCK

        def fill(c, carry):
            r0 = pl.multiple_of(c * rows_per, rows_per)
            kv = kv_ref[0, pl.ds(r0, rows_per), :]
            ks_sc[pl.ds(SEL_TILE + r0, rows_per), :] = kv[:, 2 * LANES:3 * LANES].astype(BF16)
            for half in range(2):
                v_t = kv[half * SEL_TILE:(half + 1) * SEL_TILE, 3 * LANES:4 * LANES]
                vst_sc[2 * c + half + 1] = jnp.transpose(v_t).astype(BF16)
            w = win_ref[0, pl.ds(r0, rows_per), :]
            wk_sc[pl.ds(WINDOW + r0, rows_per), :] = w[:, 0:LANES].astype(BF16)
            wv_sc[pl.ds(WINDOW + r0, rows_per), :] = w[:, LANES:2 * LANES].astype(BF16)
            pooled = jnp.sum(kv[:, 0:2 * LANES].reshape(blocks_per, CMP_BLOCK, 2 * LANES) * pw_ref[...][None],
                             axis=1)
            c0 = pl.multiple_of(c * blocks_per, blocks_per)
            kc_sc[pl.ds(c0, blocks_per), :] = pooled[:, 0:LANES].astype(BF16)
            pv_sc[pl.ds(c0, blocks_per), :] = pooled[:, LANES:2 * LANES]
            return carry

        lax.fori_loop(0, t_len // rows_per, fill, 0)
        vct_sc[...] = jnp.transpose(pv_sc[...]).astype(BF16)

        for jj, h in enumerate(PROMPT_HEADS):
            cs = slice(jj * QB, (jj + 1) * QB)
            u_i = lax.broadcasted_iota(I32, (SEL_BIAS_ROWS, QB), 0)
            q_i = lax.broadcasted_iota(I32, (SEL_BIAS_ROWS, QB), 1)
            dist = q_i - (u_i - SEL_REL0)
            near = (_bias_from_dist(dist, rb_ref, h) - rb_ref[N_BUCKETS - 1, h]) * LOG2E
            bs_sc[:, cs] = jnp.where(dist >= 0, near, NEG_INF)
            s_i = lax.broadcasted_iota(I32, (WIN_BAND, QB), 0)
            q_i = lax.broadcasted_iota(I32, (WIN_BAND, QB), 1)
            dist = q_i + WINDOW - s_i
            bw_sc[:, cs] = jnp.where((dist >= 0) & (dist <= WINDOW), _bias_from_dist(dist, rb_ref, h) * LOG2E,
                                     NEG_INF)
            c_i = lax.broadcasted_iota(I32, (n_cmp, QB), 0)
            q_i = lax.broadcasted_iota(I32, (n_cmp, QB), 1)
            rel = jnp.where(c_i < n_cmp // 2, c_i, c_i - n_cmp)
            dist = q_i - CMP_BLOCK * rel - (CMP_BLOCK - 1)
            dist = jnp.where((c_i >= 2) & (c_i < n_cmp // 2), FAR_DIST, dist)
            fc_sc[:, cs] = _bias_from_dist(dist, rb_ref, h) * LOG2E

    qbd = _block_diag_queries(nq_ref[0], QB, PROMPT_HEADS, HEAD_DIM ** -0.5 * LOG2E)

    w0 = pl.multiple_of(i * QB, QB)
    s_w = _dot_nt(wk_sc[pl.ds(w0, WIN_BAND), :], qbd) + bw_sc[...]
    in_seq = lax.broadcasted_iota(I32, (WIN_BAND, N_COLS), 0) >= WINDOW - i * QB
    p_w = _softmax2_rows(jnp.where(in_seq, s_w, NEG_INF))
    o_w = _dot_tn(wv_sc[pl.ds(w0, WIN_BAND), :], p_w.astype(BF16))

    s_c = _dot_nt(kc_sc[...], qbd) + pltpu.roll(fc_sc[...], 2 * i, 0)
    c_i = lax.broadcasted_iota(I32, (n_cmp, N_COLS), 0)
    q_i = lax.broadcasted_iota(I32, (n_cmp, N_COLS), 1) % QB
    p_c = _softmax2_rows(jnp.where(c_i * CMP_BLOCK + (CMP_BLOCK - 1) <= i * QB + q_i, s_c, NEG_INF))
    o_c = _dot(vct_sc[...], p_c.astype(BF16))

    p_sum = p_c[:, 0:LANES]
    for j in range(1, GQ):
        p_sum = p_sum + p_c[:, j * LANES:(j + 1) * LANES]
    ratio = SEL_BLOCK // CMP_BLOCK
    pair = (lax.broadcasted_iota(I32, (n_sel, n_cmp), 1) // ratio
            == lax.broadcasted_iota(I32, (n_sel, n_cmp), 0)).astype(F32)
    imp = _dot(pair, p_sum, hi)
    blk = lax.broadcasted_iota(I32, (n_sel, LANES), 0)
    score = jnp.where(blk == i, jnp.inf, jnp.where(blk < i, imp, NEG_INF))

    def pick(_, carry):
        work, chosen = carry
        best = jnp.max(work, 0, keepdims=True)
        first = jnp.min(jnp.where(work == best, blk, n_sel), 0, keepdims=True)
        hit = blk == first
        return jnp.where(hit, NEG_INF, work), jnp.where(hit, 1.0, chosen)

    _, chosen = lax.fori_loop(0, SEL_TOPK, pick, (score, jnp.zeros((n_sel, LANES), F32)), unroll=True)
    chosen = jnp.where(score > NEG_INF, chosen, 0.0).astype(BF16)

    def tile_scores(slot, bias):
        k_t = ks_sc[pl.ds(pl.multiple_of(slot * SEL_TILE, SEL_TILE), SEL_TILE), :]
        expand = ((slot - 1) * SEL_TILE_BLOCKS + lax.broadcasted_iota(I32, (SEL_TILE, n_sel), 0) // SEL_BLOCK
                  == lax.broadcasted_iota(I32, (SEL_TILE, n_sel), 1))
        picked = _dot(jnp.where(expand, 1.0, 0.0).astype(BF16), chosen)
        picked = jnp.concatenate([picked] * GQ, axis=1) > 0.5
        s = _dot_nt(k_t, qbd)
        return jnp.where(picked, s if bias is None else s + bias, NEG_INF)

    def sel_pair(slot_a, slot_b, bias_a, bias_b, first):
        s_a = tile_scores(slot_a, bias_a)
        s_b = tile_scores(slot_b, bias_b)
        t_max = jnp.maximum(jnp.max(s_a, 0, keepdims=True), jnp.max(s_b, 0, keepdims=True))
        m_new = t_max if first else jnp.maximum(m_sc[...], t_max)
        e_a = jnp.exp2(s_a - m_new)
        e_b = jnp.exp2(s_b - m_new)
        l_new = jnp.sum(e_a, 0, keepdims=True) + jnp.sum(e_b, 0, keepdims=True)
        acc_new = _dot(vst_sc[slot_a], e_a.astype(BF16)) + _dot(vst_sc[slot_b], e_b.astype(BF16))
        if first:
            l_sc[...] = l_new
            acc_sc[...] = acc_new
        else:
            scale = jnp.exp2(m_sc[...] - m_new)
            l_sc[...] = scale * l_sc[...] + l_new
            acc_sc[...] = scale * acc_sc[...] + acc_new
        m_sc[...] = m_new

    last = i // SEL_TILE_BLOCKS
    u0 = pl.multiple_of(SEL_REL0 - SEL_TILE - SEL_BLOCK * (i % SEL_TILE_BLOCKS), SEL_BLOCK)
    sel_pair(last + 1, last, bs_sc[pl.ds(u0 + SEL_TILE, SEL_TILE), :], bs_sc[pl.ds(u0, SEL_TILE), :], True)

    def far_pair(p, carry):
        slot_b = 2 * p + 2
        sel_pair(2 * p + 1, jnp.where(slot_b < last, slot_b, 0), None, None, False)
        return carry

    lax.fori_loop(0, last // 2, far_pair, 0)
    o_s = acc_sc[...] / jnp.maximum(l_sc[...], 1e-30)

    eye = (lax.broadcasted_iota(I32, (3 * N_NSA_HEADS + 8, LANES), 0) + G_NSA
           == lax.broadcasted_iota(I32, (3 * N_NSA_HEADS + 8, LANES), 1)).astype(F32)
    gates = jax.nn.sigmoid(_dot_nt(eye, zg_ref[0], hi))
    pieces = []
    for h in range(N_NSA_HEADS):
        g = h // GQ
        jj = PROMPT_HEADS.index(h)
        part = lambda o: o[g * HEAD_DIM:(g + 1) * HEAD_DIM, jj * QB:(jj + 1) * QB]
        pieces.append(gates[h:h + 1, :] * part(o_c)
                      + gates[N_NSA_HEADS + h:N_NSA_HEADS + h + 1, :] * part(o_s)
                      + gates[2 * N_NSA_HEADS + h:2 * N_NSA_HEADS + h + 1, :] * part(o_w))
    mix_t = jnp.concatenate(pieces, axis=0)
    mix_t = jnp.concatenate([mix_t, jnp.zeros((NSA_W, LANES - QB), F32)], axis=1)
    out_ref[0] = jnp.transpose(mix_t)[0:QB, :]


def _nsa_prompt(nq, kv, win, zg, pool_w2, rel_bias):
    bsz, t_len, _ = nq.shape
    assert t_len // CMP_BLOCK == LANES and t_len % (2 * SEL_TILE) == 0
    n_cmp = t_len // CMP_BLOCK
    per_b = lambda w: pl.BlockSpec((1, t_len, w), lambda b, i: (b, 0, 0))
    blk = lambda w: pl.BlockSpec((1, NSA_QBLOCK, w), lambda b, i: (b, i, 0))
    return pl.pallas_call(
        functools.partial(_nsa_prompt_body, t_len=t_len),
        grid=(bsz, t_len // NSA_QBLOCK),
        in_specs=[pl.BlockSpec(memory_space=pltpu.SMEM),
                  blk(NSA_W), per_b(4 * KV_W), per_b(2 * KV_W), blk(LANES),
                  pl.BlockSpec((CMP_BLOCK, 2 * LANES), lambda b, i: (0, 0))],
        out_specs=blk(NSA_W),
        out_shape=jax.ShapeDtypeStruct((bsz, t_len, NSA_W), F32),
        scratch_shapes=[pltpu.VMEM((SEL_TILE + t_len, LANES), BF16),
                        pltpu.VMEM((1 + t_len // SEL_TILE, LANES, SEL_TILE), BF16),
                        pltpu.VMEM((WINDOW + t_len, LANES), BF16),
                        pltpu.VMEM((WINDOW + t_len, LANES), BF16),
                        pltpu.VMEM((n_cmp, LANES), BF16),
                        pltpu.VMEM((n_cmp, LANES), F32),
                        pltpu.VMEM((LANES, n_cmp), BF16),
                        pltpu.VMEM((SEL_BIAS_ROWS, N_COLS), F32),
                        pltpu.VMEM((WIN_BAND, N_COLS), F32),
                        pltpu.VMEM((n_cmp, N_COLS), F32),
                        pltpu.VMEM((1, N_COLS), F32),
                        pltpu.VMEM((1, N_COLS), F32),
                        pltpu.VMEM((LANES, N_COLS), F32)],
        compiler_params=pltpu.CompilerParams(dimension_semantics=("arbitrary", "arbitrary"),
                                             vmem_limit_bytes=VMEM_LIMIT),
        name="nsa_prompt",
    )(rel_bias, nq, kv, win, zg, pool_w2)


CMP_PAGES = 32
CMP_SLOTS = 3
N_PICK = SEL_TOPK - 1
N_OWNERS = N_KV_HEADS * N_NEW
N_FETCH = N_OWNERS * N_PICK
OWNER_KEYS = N_PICK * PAGE_SIZE


def _sample_row_ids():
    r = lax.broadcasted_iota(I32, (N_QROWS, 1), 0)
    return r % SEQ_PAD - (SEQ_PAD - N_NEW), r // SEQ_PAD


def _sample_bias(dist, rb_ref):
    _, head = _sample_row_ids()
    col = lambda k: sum(jnp.where(head == h, rb_ref[k, h], 0.0) for h in range(N_NSA_HEADS))
    acc = jnp.broadcast_to(col(0), dist.shape)
    for k in range(1, N_BUCKETS):
        acc = jnp.where(dist >= BUCKET_THR[k], col(k), acc)
    return acc


def _softmax_two(s1, s2):
    mx = jnp.maximum(jnp.max(s1, -1, keepdims=True), jnp.max(s2, -1, keepdims=True))
    mx = jnp.where(mx > NEG_INF, mx, 0.0)
    e1 = jnp.exp(s1 - mx)
    e2 = jnp.exp(s2 - mx)
    den = jnp.maximum(jnp.sum(e1, -1, keepdims=True) + jnp.sum(e2, -1, keepdims=True), 1e-30)
    return e1 / den, e2 / den


def _nsa_cmp_body(pt_ref, rb_ref, nq_ref, pwt_ref, cache_ref, oc_ref, idx_ref,
                  buf, sem, seg_sc, kct_sc, vct_sc, *, n_pages, past_len):
    b = pl.program_id(0)
    nb = pl.num_programs(0)
    n_chunks = n_pages // CMP_PAGES
    chunk_rows = CMP_PAGES * PAGE_SIZE
    chunk_blocks = chunk_rows // CMP_BLOCK
    n_cmp = past_len // CMP_BLOCK
    n_sel_past = past_len // SEL_BLOCK
    n_prob = N_KV_HEADS * SEQ_PAD
    total = nb * n_chunks
    hi = lax.Precision.HIGHEST

    def page_copy(gi, p):
        page = pt_ref[(gi // n_chunks) * n_pages + (gi % n_chunks) * CMP_PAGES + p]
        slot = gi % CMP_SLOTS
        return pltpu.make_async_copy(cache_ref.at[page, pl.ds(0, 2 * LANES), :], buf.at[slot, p], sem.at[slot])

    def start_chunk(gi):
        for p in range(CMP_PAGES):
            page_copy(gi, p).start()

    def wait_chunk(gi):
        for p in range(CMP_PAGES):
            page_copy(gi, p).wait()

    @pl.when(b == 0)
    def _prologue():
        seg_sc[...] = (lax.broadcasted_iota(I32, (chunk_rows, chunk_blocks), 0) // CMP_BLOCK
                       == lax.broadcasted_iota(I32, (chunk_rows, chunk_blocks), 1)).astype(BF16)
        for gi in range(min(CMP_SLOTS - 1, total)):
            start_chunk(gi)

    def chunk_step(c, carry):
        gi = b * n_chunks + c
        ahead = gi + (CMP_SLOTS - 1)

        @pl.when(ahead < total)
        def _prefetch():
            start_chunk(ahead)

        wait_chunk(gi)
        slot = gi % CMP_SLOTS
        y = jnp.concatenate([(buf[slot, p] * pwt_ref[...]).astype(BF16) for p in range(CMP_PAGES)], axis=1)
        pooled = _dot(y, seg_sc[...])
        kct_sc[c] = pooled[0:LANES].astype(BF16)
        vct_sc[c] = pooled[LANES:2 * LANES].astype(BF16)
        return carry

    lax.fori_loop(0, n_chunks, chunk_step, 0)

    qbd = _block_diag_queries(nq_ref[0], SEQ_PAD)
    tok, _ = _sample_row_ids()
    s_c = jnp.concatenate([_dot(qbd, kct_sc[c]) for c in range(n_chunks)], axis=1)
    near0 = n_cmp - LANES
    assert past_len - (SEQ_PAD - N_NEW) - (CMP_BLOCK * (near0 - 1) + CMP_BLOCK - 1) >= FAR_DIST
    dist = past_len + tok - ((lax.broadcasted_iota(I32, (N_QROWS, LANES), 1) + near0) * CMP_BLOCK + CMP_BLOCK - 1)
    near = jnp.where(dist >= 0, s_c[:, near0:] + _sample_bias(dist, rb_ref), NEG_INF)
    far = s_c[:, :near0] + _sample_bias(jnp.full((N_QROWS, 1), FAR_DIST, I32), rb_ref)
    p_c = _masked_softmax(jnp.concatenate([far, near], axis=1))
    o_c = _dot_nt(p_c[:, 0:chunk_blocks].astype(BF16), vct_sc[0])
    for c in range(1, n_chunks):
        o_c = o_c + _dot_nt(p_c[:, c * chunk_blocks:(c + 1) * chunk_blocks].astype(BF16), vct_sc[c])
    oc_ref[0] = o_c

    p_sum = jnp.sum(p_c.reshape(N_KV_HEADS, GQ, SEQ_PAD, n_cmp), axis=1).reshape(n_prob, n_cmp)
    p_sum = jnp.concatenate([p_sum, jnp.zeros((LANES - n_prob, n_cmp), F32)], axis=0)
    ratio = SEL_BLOCK // CMP_BLOCK
    pair = (lax.broadcasted_iota(I32, (n_sel_past, n_cmp), 1) // ratio
            == lax.broadcasted_iota(I32, (n_sel_past, n_cmp), 0)).astype(F32)
    imp = _dot_nt(pair, p_sum, hi)
    blk = lax.broadcasted_iota(I32, (n_sel_past, LANES), 0)
    pick_i = lax.broadcasted_iota(I32, (SEL_TOPK, LANES), 0)

    def pick(k, carry):
        work, ids = carry
        best = jnp.max(work, 0, keepdims=True)
        first = jnp.min(jnp.where(work == best, blk, n_sel_past), 0, keepdims=True)
        return jnp.where(blk == first, NEG_INF, work), jnp.where(pick_i == k, first, ids)

    _, ids = lax.fori_loop(0, N_PICK, pick, (imp, jnp.zeros((SEL_TOPK, LANES), I32)), unroll=True)
    idx_ref[0] = ids


def _nsa_cmp(page_table, rel_bias, nq_s, pool_wt, cache3):
    dec_b, n_pages = page_table.shape
    past_len = n_pages * PAGE_SIZE
    assert n_pages % CMP_PAGES == 0 and past_len // SEL_BLOCK >= N_PICK
    n_chunks = n_pages // CMP_PAGES
    chunk_rows = CMP_PAGES * PAGE_SIZE
    chunk_blocks = chunk_rows // CMP_BLOCK
    assert chunk_blocks == LANES
    grid_spec = pltpu.PrefetchScalarGridSpec(
        num_scalar_prefetch=1,
        grid=(dec_b,),
        in_specs=[pl.BlockSpec(memory_space=pltpu.SMEM),
                  pl.BlockSpec((1, SEQ_PAD, NSA_W), lambda b, pt: (b, 0, 0)),
                  pl.BlockSpec((2 * LANES, PAGE_SIZE), lambda b, pt: (0, 0)),
                  pl.BlockSpec(memory_space=pl.ANY)],
        out_specs=[pl.BlockSpec((1, N_QROWS, LANES), lambda b, pt: (b, 0, 0)),
                   pl.BlockSpec((1, SEL_TOPK, LANES), lambda b, pt: (b, 0, 0))],
        scratch_shapes=[pltpu.VMEM((CMP_SLOTS, CMP_PAGES, 2 * LANES, PAGE_SIZE), F32),
                        pltpu.SemaphoreType.DMA((CMP_SLOTS,)),
                        pltpu.VMEM((chunk_rows, chunk_blocks), BF16),
                        pltpu.VMEM((n_chunks, LANES, chunk_blocks), BF16),
                        pltpu.VMEM((n_chunks, LANES, chunk_blocks), BF16)],
    )
    return pl.pallas_call(
        functools.partial(_nsa_cmp_body, n_pages=n_pages, past_len=past_len),
        grid_spec=grid_spec,
        out_shape=[jax.ShapeDtypeStruct((dec_b, N_QROWS, LANES), F32),
                   jax.ShapeDtypeStruct((dec_b, SEL_TOPK, LANES), I32)],
        compiler_params=pltpu.CompilerParams(dimension_semantics=("arbitrary",),
                                             vmem_limit_bytes=VMEM_LIMIT),
        name="nsa_sample_cmp",
    )(page_table.reshape(-1), rel_bias, nq_s, pool_wt, cache3)


def _nsa_sel_body(pt_ref, ids_ref, rb_ref, nq_ref, kvn_ref, winc_ref, winn_ref, zg_ref, oc_ref, idv_ref,
                  cache_ref, out_ref, buf, sem, *, n_pages, past_len):
    b = pl.program_id(0)
    nb = pl.num_programs(0)
    hi = lax.Precision.HIGHEST

    def block_copy(bb, n, slot):
        blk = ids_ref[bb * N_FETCH + n]
        page = pt_ref[bb * n_pages + blk // 2]
        return pltpu.make_async_copy(cache_ref.at[page, pl.ds(2 * LANES, 2 * LANES), :], buf.at[slot, n],
                                     sem.at[slot])

    def start_all(bb, slot):
        def go(n, carry):
            block_copy(bb, n, slot).start()
            return carry
        lax.fori_loop(0, N_FETCH, go, 0)

    def wait_all(bb, slot):
        def go(n, carry):
            block_copy(bb, n, slot).wait()
            return carry
        lax.fori_loop(0, N_FETCH, go, 0)

    slot = b % 2

    @pl.when(b == 0)
    def _prologue():
        start_all(0, 0)

    @pl.when(b + 1 < nb)
    def _prefetch():
        start_all(b + 1, 1 - slot)

    qbd = _block_diag_queries(nq_ref[0], SEQ_PAD)
    tok, head = _sample_row_ids()
    owner = (head // GQ) * N_NEW + tok

    wc = winc_ref[0]
    wn = winn_ref[0]
    r_i = lax.broadcasted_iota(I32, (N_QROWS, WINDOW), 1)
    dist1 = WINDOW + tok - r_i
    s1 = _dot(qbd, wc[0:LANES, :].astype(BF16)) + _sample_bias(dist1, rb_ref)
    s1 = jnp.where((dist1 >= 0) & (dist1 <= WINDOW), s1, NEG_INF)
    n_i = lax.broadcasted_iota(I32, (N_QROWS, SEQ_PAD), 1) - (SEQ_PAD - N_NEW)
    dist2 = tok - n_i
    new_ok = (n_i >= 0) & (dist2 >= 0)
    bias2 = _sample_bias(dist2, rb_ref)
    s2 = jnp.where(new_ok, _dot_nt(qbd, wn[:, 0:LANES].astype(BF16)) + bias2, NEG_INF)
    p1, p2 = _softmax_two(s1, s2)
    o_w = (_dot_nt(p1.astype(BF16), wc[LANES:2 * LANES, :].astype(BF16))
           + _dot(p2.astype(BF16), wn[:, LANES:2 * LANES].astype(BF16)))

    wait_all(b, slot)

    def owner_tiles(o, r0):
        return jnp.concatenate([buf[slot, o * N_PICK + k, r0:r0 + LANES, :] for k in range(N_PICK)],
                               axis=1).astype(BF16)

    s_f = jnp.full((N_QROWS, OWNER_KEYS), NEG_INF, F32)
    for o in range(N_OWNERS):
        s_f = jnp.where(owner == o, _dot(qbd, owner_tiles(o, 0)), s_f)
    spread = (lax.broadcasted_iota(I32, (SEL_TOPK, OWNER_KEYS), 0)
              == lax.broadcasted_iota(I32, (SEL_TOPK, OWNER_KEYS), 1) // PAGE_SIZE).astype(F32)
    blk_of_key = _dot_tn(idv_ref[0].astype(F32), spread, hi)
    blk_rows = jnp.concatenate([blk_of_key[(h // GQ) * SEQ_PAD:(h // GQ + 1) * SEQ_PAD]
                                for h in range(N_NSA_HEADS)], axis=0).astype(I32)
    blocks_per_page = PAGE_SIZE // SEL_BLOCK
    in_page = lax.broadcasted_iota(I32, (N_QROWS, OWNER_KEYS), 1) % PAGE_SIZE
    in_block = in_page // SEL_BLOCK == blk_rows % blocks_per_page
    dist_s = past_len + tok - ((blk_rows // blocks_per_page) * PAGE_SIZE + in_page)
    s_f = jnp.where((tok >= 0) & in_block & (dist_s >= 0), s_f + _sample_bias(dist_s, rb_ref), NEG_INF)
    kvn = kvn_ref[0]
    s_n = jnp.where(new_ok, _dot_nt(qbd, kvn[:, 2 * LANES:3 * LANES].astype(BF16)) + bias2, NEG_INF)
    p_f, p_n = _softmax_two(s_f, s_n)
    o_s = _dot(p_n.astype(BF16), kvn[:, 3 * LANES:4 * LANES].astype(BF16))
    for o in range(N_OWNERS):
        o_s = o_s + _dot_nt(jnp.where(owner == o, p_f, 0.0).astype(BF16), owner_tiles(o, LANES))

    o_c = oc_ref[0]
    gates = jax.nn.sigmoid(zg_ref[0])
    is_new = lax.broadcasted_iota(I32, (SEQ_PAD, HEAD_DIM), 0) >= SEQ_PAD - N_NEW
    for h in range(N_NSA_HEADS):
        mix = _gated_mix(gates, o_c, o_s, o_w, h, SEQ_PAD)
        out_ref[0, :, h * HEAD_DIM:(h + 1) * HEAD_DIM] = jnp.where(is_new, mix, 0.0)


def _nsa_sel(page_table, ids, ids_rows, rel_bias, nq_s, kv_s, win_cache, win_s, zg_s, o_c, cache3):
    dec_b, n_pages = page_table.shape
    past_len = n_pages * PAGE_SIZE
    seq = lambda w: pl.BlockSpec((1, SEQ_PAD, w), lambda b, pt, ix: (b, 0, 0))
    grid_spec = pltpu.PrefetchScalarGridSpec(
        num_scalar_prefetch=2,
        grid=(dec_b,),
        in_specs=[pl.BlockSpec(memory_space=pltpu.SMEM),
                  seq(NSA_W), seq(4 * KV_W),
                  pl.BlockSpec((1, 2 * KV_W, WINDOW), lambda b, pt, ix: (b, 0, 0)),
                  seq(2 * KV_W), seq(LANES),
                  pl.BlockSpec((1, N_QROWS, LANES), lambda b, pt, ix: (b, 0, 0)),
                  pl.BlockSpec((1, SEL_TOPK, LANES), lambda b, pt, ix: (b, 0, 0)),
                  pl.BlockSpec(memory_space=pl.ANY)],
        out_specs=seq(NSA_W),
        scratch_shapes=[pltpu.VMEM((2, N_FETCH, 2 * LANES, PAGE_SIZE), F32),
                        pltpu.SemaphoreType.DMA((2,))],
    )
    return pl.pallas_call(
        functools.partial(_nsa_sel_body, n_pages=n_pages, past_len=past_len),
        grid_spec=grid_spec,
        out_shape=jax.ShapeDtypeStruct((dec_b, SEQ_PAD, NSA_W), F32),
        compiler_params=pltpu.CompilerParams(dimension_semantics=("arbitrary",),
                                             vmem_limit_bytes=VMEM_LIMIT),
        name="nsa_sample_sel",
    )(page_table.reshape(-1), ids.reshape(-1), rel_bias, nq_s, kv_s, win_cache, win_s, zg_s, o_c, ids_rows, cache3)


FF_CHUNK = 896


def _ffn_body(*refs, tm, has_override):
    if has_override:
        (x_ref, mm_ref, mn_ref, ov_ref, woa_ref, wob_ref, l1w_ref, l1b_ref, wup_ref, bup_ref, cw_ref, cb_ref,
         wdn_ref, bdn_ref, l2w_ref, l2b_ref, y_ref, tail_ref, carry_sc, ubuf) = refs
    else:
        (x_ref, mm_ref, mn_ref, woa_ref, wob_ref, l1w_ref, l1b_ref, wup_ref, bup_ref, cw_ref, cb_ref,
         wdn_ref, bdn_ref, l2w_ref, l2b_ref, y_ref, tail_ref, carry_sc, ubuf) = refs
    j = pl.program_id(1)

    @pl.when(j == 0)
    def _init():
        carry_sc[...] = jnp.zeros(carry_sc.shape, F32)

    x = x_ref[0]
    h = ALPHA * x + (_dot(mm_ref[0].astype(BF16), woa_ref[...]) + _dot(mn_ref[0].astype(BF16), wob_ref[...]))
    x1 = _layer_norm(h, l1w_ref[...], l1b_ref[...])
    x1b = x1.astype(BF16)
    if has_override:
        state_row = lax.broadcasted_iota(I32, (tm, FF_CHUNK), 0) % SEQ_PAD < SEQ_PAD - N_NEW

    def conv_half(c0):
        cols = slice(c0, c0 + FF_CHUNK)
        u = _dot(x1b, wup_ref[:, cols]) + bup_ref[:, cols]
        if has_override:
            u = jnp.where(state_row, ov_ref[0, :, cols], u)
        ubuf[0:8, :] = carry_sc[:, cols]
        ubuf[8:8 + tm, :] = u
        out = cb_ref[:, cols] + ubuf[6:6 + tm, :] * cw_ref[0:1, cols]
        out = out + ubuf[7:7 + tm, :] * cw_ref[1:2, cols]
        out = out + u * cw_ref[2:3, cols]
        carry_sc[:, cols] = u[tm - 8:tm, :]
        tail_ref[0, 0, :, cols] = u if has_override else u[tm - 8:tm, :]
        return out

    ff = jnp.zeros((tm, D_MODEL), F32)
    for c in range(D_FF // FF_CHUNK):
        ga = conv_half(c * FF_CHUNK)
        gb = conv_half(D_FF + c * FF_CHUNK)
        gelu = ga * (0.5 * (1.0 + jnp.tanh(math.sqrt(2.0 / math.pi) * (ga + 0.044715 * (ga * ga * ga)))))
        ff = ff + _dot((gelu * gb).astype(BF16), wdn_ref[c * FF_CHUNK:(c + 1) * FF_CHUNK, :])
    ff = ff + bdn_ref[...]
    y_ref[0] = _layer_norm(ALPHA * x1 + ff, l2w_ref[...], l2b_ref[...])


def _ffn(x, mix_m, mix_n, override, wts):
    bsz, t_len, _ = x.shape
    tm = min(t_len, 512)
    nt = t_len // tm
    has_override = override is not None
    tail_rows = tm if has_override else 8
    row = lambda w: pl.BlockSpec((1, tm, w), lambda b, j: (b, j, 0))
    const = lambda shape: pl.BlockSpec(shape, lambda b, j: (0,) * len(shape), pipeline_mode=pl.Buffered(1))
    in_specs = [row(D_MODEL), row(MLSTM_W), row(NSA_W)]
    args = [x, mix_m, mix_n]
    if has_override:
        in_specs.append(row(2 * D_FF))
        args.append(override)
    in_specs += [const(w.shape) for w in wts]
    args += list(wts)
    y, tail = pl.pallas_call(
        functools.partial(_ffn_body, tm=tm, has_override=has_override),
        grid=(bsz, nt),
        in_specs=in_specs,
        out_specs=[row(D_MODEL), pl.BlockSpec((1, 1, tail_rows, 2 * D_FF), lambda b, j: (b, j, 0, 0))],
        out_shape=[jax.ShapeDtypeStruct((bsz, t_len, D_MODEL), F32),
                   jax.ShapeDtypeStruct((bsz, nt, tail_rows, 2 * D_FF), F32)],
        scratch_shapes=[pltpu.VMEM((8, 2 * D_FF), F32), pltpu.VMEM((8 + tm, FF_CHUNK), F32)],
        compiler_params=pltpu.CompilerParams(dimension_semantics=("arbitrary", "arbitrary"),
                                             vmem_limit_bytes=VMEM_LIMIT),
        name="ffn",
    )(*args)
    return y, tail


def _permute_in_proj(w_in, b_in):
    gates_a = slice(2 * MLSTM_W + 2 * MLSTM_W, 2 * MLSTM_W + 2 * MLSTM_W + 2 * N_MLSTM_HEADS)
    nsa0 = gates_a.stop
    nsa1 = nsa0 + NSA_W + 6 * KV_W

    def perm(a):
        pad = IN_PAD - a.shape[-1]
        parts = [a[..., :gates_a.start], a[..., nsa0:nsa1], a[..., gates_a], a[..., nsa1:]]
        parts.append(jnp.zeros(a.shape[:-1] + (pad,), a.dtype))
        return jnp.concatenate(parts, axis=-1)

    return perm(w_in).astype(BF16), perm(b_in).reshape(1, IN_PAD)


def _front_pad(a, n):
    return jnp.pad(a, ((0, 0), (n, 0), (0, 0)))


def kernel(x_prompt, x_sample, cache_nsa_kv, cache_win_kv, state_mlstm_c, state_mlstm_n, state_mlstm_m,
           state_mlstm_conv, state_ffn_conv, page_table,
           w_in, b_in, mlstm_conv_w, mlstm_conv_b, mlstm_norm_w, nsa_pool_w, rel_bias, w_out,
           ln1_w, ln1_b, w_up, b_up, ffn_conv_w, ffn_conv_b, w_down, b_down, ln2_w, ln2_b):
    bsz, t_len, _ = x_prompt.shape
    dec_b, n_new, _ = x_sample.shape
    assert n_new == N_NEW and cache_win_kv.shape[1] == WINDOW

    w_p, b_p = _permute_in_proj(w_in, b_in)
    pool_w2 = jnp.concatenate([jnp.repeat(nsa_pool_w[0], HEAD_DIM, axis=-1),
                               jnp.repeat(nsa_pool_w[1], HEAD_DIM, axis=-1)], axis=-1)
    row = lambda v: v.reshape(1, -1)
    ffn_w = (w_out[:MLSTM_W].astype(BF16), w_out[MLSTM_W:].astype(BF16), row(ln1_w), row(ln1_b),
             w_up.astype(BF16), row(b_up), ffn_conv_w, row(ffn_conv_b), w_down.astype(BF16), row(b_down),
             row(ln2_w), row(ln2_b))

    zqk, zv, zo, nq, kv, win, zg, kv_t = _proj_in(x_prompt.reshape(bsz * t_len, D_MODEL), w_p, b_p, seq_len=t_len)
    seq3 = lambda a: a.reshape(bsz, t_len, a.shape[-1])
    zqk, zv, zo, nq, kv, win, zg = map(seq3, (zqk, zv, zo, nq, kv, win, zg))
    mix_m, c_p, n_p, m_p = _mlstm(zqk, zv, zo, zg, None, mlstm_conv_w, mlstm_conv_b, mlstm_norm_w, 0)
    mix_n = _nsa_prompt(nq, kv, win, zg, pool_w2, rel_bias)
    y_p, tail_p = _ffn(x_prompt, mix_m, mix_n, None, ffn_w)
    kv_p = jnp.transpose(kv_t.reshape(bsz, 4, N_KV_HEADS, HEAD_DIM, t_len), (0, 4, 1, 2, 3))
    win_p = win[:, t_len - min(WINDOW, t_len):].reshape(bsz, -1, 2, N_KV_HEADS, HEAD_DIM)
    mconv_p = zqk[:, t_len - (MLSTM_CONV - 1):]
    fconv_p = tail_p[:, -1, 8 - (FFN_CONV - 1):]

    pad = SEQ_PAD - N_NEW
    xs = _front_pad(x_sample, pad)
    szqk, szv, szo, snq, skv, swin, szg = _proj_in(xs.reshape(dec_b * SEQ_PAD, D_MODEL), w_p, b_p)
    sseq = lambda a: a.reshape(dec_b, SEQ_PAD, a.shape[-1])
    szqk, szv, szo, snq, skv, swin, szg = map(sseq, (szqk, szv, szo, snq, skv, swin, szg))
    lead = MLSTM_CHUNK - N_NEW
    new = lambda a: a[:, pad:]
    m_qk = jnp.concatenate([jnp.zeros((dec_b, lead - (MLSTM_CONV - 1), 2 * MLSTM_W), F32),
                            state_mlstm_conv.astype(F32), new(szqk)], axis=1)
    mix_ms, c_s, n_s, m_s = _mlstm(m_qk, _front_pad(new(szv), lead), _front_pad(new(szo), lead),
                                   _front_pad(new(szg), lead),
                                   (state_mlstm_c, state_mlstm_n, state_mlstm_m),
                                   mlstm_conv_w, mlstm_conv_b, mlstm_norm_w, lead)
    mix_ms = mix_ms[:, MLSTM_CHUNK - SEQ_PAD:]
    cache3 = jnp.transpose(cache_nsa_kv, (0, 2, 3, 4, 1)).reshape(cache_nsa_kv.shape[0], 4 * KV_W, PAGE_SIZE)
    pool_wt = jnp.tile(jnp.repeat(jnp.transpose(nsa_pool_w, (0, 2, 1)).reshape(2 * N_KV_HEADS, CMP_BLOCK),
                                  HEAD_DIM, axis=0), (1, PAGE_SIZE // CMP_BLOCK))
    o_c, ids_rows = _nsa_cmp(page_table, rel_bias, snq, pool_wt, cache3)
    ids = jnp.transpose(ids_rows[:, :N_PICK, :N_KV_HEADS * SEQ_PAD], (0, 2, 1))
    ids = ids.reshape(dec_b, N_KV_HEADS, SEQ_PAD, N_PICK)[:, :, pad:]
    win_cache = jnp.transpose(cache_win_kv, (0, 2, 3, 4, 1)).reshape(dec_b, 2 * KV_W, WINDOW)
    mix_ns = _nsa_sel(page_table, ids, ids_rows, rel_bias, snq, skv, win_cache, swin, szg, o_c, cache3)
    override = jnp.concatenate([jnp.zeros((dec_b, pad - (FFN_CONV - 1), 2 * D_FF), F32),
                                state_ffn_conv.astype(F32),
                                jnp.zeros((dec_b, N_NEW, 2 * D_FF), F32)], axis=1)
    flat = lambda a: a.reshape(1, dec_b * SEQ_PAD, a.shape[-1])
    y_s8, u_s = _ffn(flat(xs), flat(mix_ms), flat(mix_ns), flat(override), ffn_w)
    y_s = y_s8.reshape(dec_b, SEQ_PAD, D_MODEL)[:, pad:]
    kv_s = new(skv).reshape(dec_b, N_NEW, 4, N_KV_HEADS, HEAD_DIM)
    win_s = jnp.concatenate([cache_win_kv[:, N_NEW:], new(swin).reshape(dec_b, N_NEW, 2, N_KV_HEADS, HEAD_DIM)],
                            axis=1)
    mconv_s = szqk[:, SEQ_PAD - (MLSTM_CONV - 1):]
    fconv_s = u_s.reshape(dec_b, SEQ_PAD, 2 * D_FF)[:, SEQ_PAD - (FFN_CONV - 1):]
    return (y_p, y_s, kv_p, kv_s, win_p, win_s, c_p, c_s, n_p, n_s, m_p, m_s, mconv_p, mconv_s, fconv_p, fconv_s)
```

```python
import functools
import math

import jax
import jax.numpy as jnp
from jax import lax
from jax.experimental import pallas as pl
from jax.experimental.pallas import tpu as pltpu

F32 = jnp.float32
BF16 = jnp.bfloat16
I32 = jnp.int32

D_MODEL = 1024
HEAD_DIM = 64
N_MLSTM_HEADS = 8
N_NSA_HEADS = 8
N_KV_HEADS = 2
GQ = N_NSA_HEADS // N_KV_HEADS
MLSTM_W = N_MLSTM_HEADS * HEAD_DIM
NSA_W = N_NSA_HEADS * HEAD_DIM
KV_W = N_KV_HEADS * HEAD_DIM
MLSTM_CHUNK = 64
MLSTM_CONV = 4
CMP_BLOCK = 32
SEL_BLOCK = 64
SEL_TOPK = 16
WINDOW = 512
NSA_QBLOCK = 64
N_BUCKETS = 32
MAX_DISTANCE = 128
D_FF = 2688
FFN_CONV = 3
DEPTH = 1
ALPHA = (2.0 * DEPTH) ** 0.25
LN_EPS = 1e-5
PAGE_SIZE = 128

LANES = 128
VMEM_LIMIT = 56 * 1024 * 1024
NEG_INF = float("-inf")

SEQ_PAD = 8
N_NEW = 4
N_QROWS = N_NSA_HEADS * SEQ_PAD


def _bucket_thresholds():
    max_exact = N_BUCKETS // 2
    thr = list(range(max_exact + 1))
    for k in range(1, N_BUCKETS - max_exact):
        thr.append(int(math.ceil(max_exact * (MAX_DISTANCE / max_exact) ** (k / (N_BUCKETS - max_exact)))))
    return tuple(thr)


BUCKET_THR = _bucket_thresholds()
FAR_DIST = BUCKET_THR[N_BUCKETS - 1]


def _dot(a, b, precision=None):
    return jnp.dot(a, b, preferred_element_type=F32, precision=precision)


def _dot_nt(a, b, precision=None):
    return lax.dot_general(a, b, (((1,), (1,)), ((), ())), preferred_element_type=F32, precision=precision)


def _dot_tn(a, b, precision=None):
    return lax.dot_general(a, b, (((0,), (0,)), ((), ())), preferred_element_type=F32, precision=precision)


def _layer_norm(x, w, b):
    mu = jnp.mean(x, -1, keepdims=True)
    xc = x - mu
    var = jnp.mean(xc * xc, -1, keepdims=True)
    return xc * lax.rsqrt(var + LN_EPS) * w + b


def _log_sigmoid(x):
    return jnp.minimum(x, 0.0) - jnp.log1p(jnp.exp(-jnp.abs(x)))


def _bias_from_dist(dist, rb_ref, head):
    acc = jnp.full(dist.shape, rb_ref[0, head], F32)
    for k in range(1, N_BUCKETS):
        acc = jnp.where(dist >= BUCKET_THR[k], rb_ref[k, head], acc)
    return acc


def _masked_softmax(s):
    mx = jnp.max(s, -1, keepdims=True)
    mx = jnp.where(mx > NEG_INF, mx, 0.0)
    e = jnp.exp(s - mx)
    den = jnp.sum(e, -1, keepdims=True)
    return e / jnp.maximum(den, 1e-30)


_C_QK, _C_V, _C_O, _C_NQ, _C_KV, _C_WIN, _C_G = 0, 1024, 1536, 2048, 2560, 3072, 3328
IN_PAD = 3456
G_IGATE, G_FGATE, G_NSA = 0, N_MLSTM_HEADS, 2 * N_MLSTM_HEADS


def _proj_in_body(*refs, kv_transposed):
    if kv_transposed:
        x_ref, w_ref, b_ref, wkvt_ref, bkv_ref, qk_ref, v_ref, o_ref, nq_ref, kv_ref, win_ref, g_ref, kvt_ref = refs
    else:
        x_ref, w_ref, b_ref, qk_ref, v_ref, o_ref, nq_ref, kv_ref, win_ref, g_ref = refs
    xb = x_ref[...].astype(BF16)
    z = _dot(xb, w_ref[...]) + b_ref[...]
    qk_ref[...] = z[:, _C_QK:_C_V]
    v_ref[...] = z[:, _C_V:_C_O]
    o_ref[...] = z[:, _C_O:_C_NQ]
    nq_ref[...] = z[:, _C_NQ:_C_KV]
    kv_ref[...] = z[:, _C_KV:_C_WIN]
    win_ref[...] = z[:, _C_WIN:_C_G]
    g_ref[...] = z[:, _C_G:IN_PAD]
    if kv_transposed:
        kvt_ref[0] = _dot_nt(wkvt_ref[...], xb) + bkv_ref[...]


def _proj_in(x2d, w_p, b_p, seq_len=None):
    rows = x2d.shape[0]
    tm = min(rows, 256)
    widths = (1024, 512, 512, 512, 512, 256, 128)
    const = lambda i: (0, 0)
    const_spec = lambda shape: pl.BlockSpec(shape, const, pipeline_mode=pl.Buffered(1))
    in_specs = [pl.BlockSpec((tm, D_MODEL), lambda i: (i, 0)), const_spec((D_MODEL, IN_PAD)), const_spec((1, IN_PAD))]
    args = [x2d, w_p, b_p]
    out_specs = [pl.BlockSpec((tm, w), lambda i: (i, 0)) for w in widths]
    out_shape = [jax.ShapeDtypeStruct((rows, w), F32) for w in widths]
    if seq_len is not None:
        tiles = seq_len // tm
        in_specs += [const_spec((4 * KV_W, D_MODEL)), const_spec((4 * KV_W, 1))]
        args += [jnp.transpose(w_p[:, _C_KV:_C_WIN]), b_p[0, _C_KV:_C_WIN].reshape(4 * KV_W, 1)]
        out_specs.append(pl.BlockSpec((1, 4 * KV_W, tm), lambda i: (i // tiles, 0, i % tiles)))
        out_shape.append(jax.ShapeDtypeStruct((rows // seq_len, 4 * KV_W, seq_len), F32))
    return pl.pallas_call(
        functools.partial(_proj_in_body, kv_transposed=seq_len is not None),
        grid=(rows // tm,),
        in_specs=in_specs,
        out_specs=out_specs,
        out_shape=out_shape,
        compiler_params=pltpu.CompilerParams(dimension_semantics=("arbitrary",),
                                             vmem_limit_bytes=VMEM_LIMIT),
        name="proj_in",
    )(*args)


N_PAIRS = N_MLSTM_HEADS // 2
GATE_LANE0 = N_MLSTM_HEADS
MLSTM_SEQS_PER_STEP = 4


def _dot_split(x, onehot_bf16, pieces):
    acc = None
    rem = x
    for n in range(pieces):
        part = rem.astype(BF16)
        d = _dot(part, onehot_bf16)
        acc = d if acc is None else acc + d
        if n + 1 < pieces:
            rem = rem - part.astype(F32)
    return acc


def _mlstm_body(*refs, tb, nbg, n_dummy, has_state):
    if has_state:
        (zqk_ref, zv_ref, zo_ref, zg_ref, c0_ref, n0_ref, m0_ref, cw_ref, cb_ref, nw_ref,
         mix_ref, c_out, n_out, m_out, ubuf, qk_sc, ct_sc, n_sc, m_sc) = refs
    else:
        (zqk_ref, zv_ref, zo_ref, zg_ref, cw_ref, cb_ref, nw_ref,
         mix_ref, c_out, n_out, m_out, ubuf, qk_sc, ct_sc, n_sc, m_sc) = refs
    L = MLSTM_CHUNK
    D = HEAD_DIM
    j = pl.program_id(1)
    hi = lax.Precision.HIGHEST
    head_lanes = slice(GATE_LANE0, GATE_LANE0 + N_MLSTM_HEADS)
    pairs = [(bi, p) for bi in range(nbg) for p in range(N_PAIRS)]

    @pl.when(j == 0)
    def _init():
        ubuf[:, 0:8, :] = jnp.zeros((nbg, 8, 2 * MLSTM_W), F32)
        ct_sc[...] = jnp.zeros(ct_sc.shape, F32)
        n_sc[...] = jnp.zeros(n_sc.shape, F32)
        m_sc[...] = jnp.zeros(m_sc.shape, F32)
        if has_state:
            for bi in range(nbg):
                for h in range(N_MLSTM_HEADS):
                    p, hh = divmod(h, 2)
                    hs = slice(hh * D, (hh + 1) * D)
                    ct_sc[bi, p, hs, hs] = c0_ref[bi, h]
                    n_sc[bi, p, :, hs] = n0_ref[bi, h:h + 1, :]
                m_sc[bi, :, head_lanes] = m0_ref[bi]

    for bi in range(nbg):
        u = zqk_ref[bi]
        ubuf[bi, 8:8 + tb, :] = u
        conv = cb_ref[...] + ubuf[bi, 5:5 + tb, :] * cw_ref[0:1, :]
        conv = conv + ubuf[bi, 6:6 + tb, :] * cw_ref[1:2, :]
        conv = conv + ubuf[bi, 7:7 + tb, :] * cw_ref[2:3, :]
        conv = conv + u * cw_ref[3:4, :]
        ubuf[bi, 5:8, :] = u[tb - 3:tb, :]
        qk_sc[bi] = conv * jax.nn.sigmoid(conv)

    tri = (lax.broadcasted_iota(I32, (L, L), 1) <= lax.broadcasted_iota(I32, (L, L), 0)).astype(F32)
    row_l = lax.broadcasted_iota(I32, (L, LANES), 0)
    lane = lax.broadcasted_iota(I32, (L, LANES), 1)
    low = lane < D
    key_tok = lane % D
    causal_pair = key_tok <= row_l
    eye_pair = key_tok == row_l
    spread = (lax.broadcasted_iota(I32, (LANES, MLSTM_W), 0) - GATE_LANE0
              == lax.broadcasted_iota(I32, (LANES, MLSTM_W), 1) // D).astype(BF16)
    same_head = (lax.broadcasted_iota(I32, (LANES, LANES), 0) // D
                 == lax.broadcasted_iota(I32, (LANES, LANES), 1) // D)
    ones_bd = same_head.astype(BF16)
    slab_rows = 3 * L + 8

    def chunk(ci, carry):
        r0 = pl.multiple_of(ci * L, L)
        rows = pl.ds(r0, L)
        gates = []
        for bi in range(nbg):
            g = zg_ref[bi, rows, :]
            if n_dummy:
                dummy = (row_l + ci * L) < n_dummy
                g = jnp.where(dummy, jnp.where(lane < G_FGATE, -1e30, 1e30), g)
            gates.append(g)
        b_all = _dot(tri, jnp.concatenate([_log_sigmoid(g) for g in gates], axis=1), hi)
        stacks = []
        for bi in range(nbg):
            b = b_all[:, bi * LANES:(bi + 1) * LANES]
            r = pltpu.roll(gates[bi], GATE_LANE0 - G_IGATE, 1) - b
            cm = r
            for sh in (1, 2, 4, 8, 16, 32):
                cm = jnp.where(row_l >= sh, jnp.maximum(cm, pltpu.roll(cm, sh, 0)), cm)
            m_st = m_sc[bi]
            mx = jnp.maximum(m_st, cm)
            mx_end = mx[L - 1:L, :]
            m_sc[bi] = b[L - 1:L, :] + mx_end
            stacks += [mx, r, b + mx, m_st, mx_end, jnp.zeros((6, LANES), F32)]
        x_all = _dot_split(jnp.concatenate(stacks, axis=0), spread, 2)

        work = []
        den_terms = []
        for bi, p in pairs:
            cs = slice(p * LANES, (p + 1) * LANES)
            x0 = bi * slab_rows
            mx_p, rs_p, mt_p = x_all[x0:x0 + L, cs], x_all[x0 + L:x0 + 2 * L, cs], x_all[x0 + 2 * L:x0 + 3 * L, cs]
            mst_p, mxe_p = x_all[x0 + 3 * L:x0 + 3 * L + 1, cs], x_all[x0 + 3 * L + 1:x0 + 3 * L + 2, cs]
            wi_p = jnp.exp(mst_p - mx_p)
            fl_p = jnp.exp(-mt_p)
            wk_p = jnp.exp(rs_p - mxe_p)
            sc_p = jnp.exp(mst_p - mxe_p)
            r_row = jnp.sum(jnp.where(eye_pair, rs_p, 0.0), axis=0, keepdims=True)
            w_intra = jnp.exp(jnp.where(causal_pair, r_row - mx_p, NEG_INF))
            q_p = qk_sc[bi, rows, cs]
            k_p = qk_sc[bi, rows, MLSTM_W + p * LANES:MLSTM_W + (p + 1) * LANES] * (D ** -0.5)
            v_p = zv_ref[bi, rows, cs]
            qb = q_p.astype(BF16)
            k_bd = jnp.concatenate([jnp.where(low, k_p, 0.0), jnp.where(low, 0.0, k_p)], axis=0)
            v_bd = jnp.concatenate([jnp.where(low, v_p, 0.0), jnp.where(low, 0.0, v_p)], axis=0).astype(BF16)
            c_t = ct_sc[bi, p]
            both = _dot_nt(qb, jnp.concatenate([k_bd, c_t], axis=0).astype(BF16))
            a = w_intra * both[:, 0:LANES]
            num = _dot(a.astype(BF16), v_bd) + both[:, LANES:2 * LANES] * wi_p
            den_terms.append(a + wi_p * (q_p * n_sc[bi, p]))
            work.append((num, fl_p, k_p * wk_p, v_p, sc_p, c_t))
        den_all = _dot_split(jnp.concatenate(den_terms, axis=0), ones_bd, 2)

        hhs = []
        for n, (num, fl_p, _, _, _, _) in enumerate(work):
            hhs.append(num / jnp.maximum(jnp.abs(den_all[n * L:(n + 1) * L, :]), fl_p))
        mu_all = _dot_split(jnp.concatenate(hhs, axis=0), ones_bd, 1) * (1.0 / D)
        hcs = [hh - mu_all[n * L:(n + 1) * L, :] for n, hh in enumerate(hhs)]
        var_all = _dot_split(jnp.concatenate([hc * hc for hc in hcs], axis=0), ones_bd, 1) * (1.0 / D)
        for n, (bi, p) in enumerate(pairs):
            cs = slice(p * LANES, (p + 1) * LANES)
            hn = hcs[n] * lax.rsqrt(var_all[n * L:(n + 1) * L, :] + LN_EPS)
            og = jax.nn.sigmoid(zo_ref[bi, rows, cs])
            mix_ref[bi, rows, cs] = og * (hn * nw_ref[:, cs])
            _, _, kw, v_p, sc_p, c_t = work[n]
            upd = _dot_tn(v_p.astype(BF16), kw.astype(BF16))
            ct_sc[bi, p] = sc_p * c_t + jnp.where(same_head, upd, 0.0)
            n_sc[bi, p] = sc_p * n_sc[bi, p] + jnp.sum(kw, axis=0, keepdims=True)
        return carry

    lax.fori_loop(0, tb // L, chunk, 0)

    @pl.when(j == pl.num_programs(1) - 1)
    def _final():
        for bi in range(nbg):
            for h in range(N_MLSTM_HEADS):
                p, hh = divmod(h, 2)
                hs = slice(hh * D, (hh + 1) * D)
                c_out[bi, h] = ct_sc[bi, p, hs, hs]
                n_out[bi, h:h + 1, :] = n_sc[bi, p, :, hs]
            m_out[bi] = m_sc[bi, :, head_lanes]


def _mlstm(zqk, zv, zo, zg, state, conv_w, conv_b, norm_w, n_dummy):
    bsz, t_len, _ = zqk.shape
    tb = min(t_len, 256)
    nbg = MLSTM_SEQS_PER_STEP
    assert bsz % nbg == 0
    has_state = state is not None
    row = lambda w: pl.BlockSpec((nbg, tb, w), lambda b, j: (b, j, 0))
    per_b = lambda shape: pl.BlockSpec((nbg,) + shape, lambda b, j: (b,) + (0,) * len(shape))
    const = lambda shape: pl.BlockSpec(shape, lambda b, j: (0,) * len(shape))
    in_specs = [row(2 * MLSTM_W), row(MLSTM_W), row(MLSTM_W), row(LANES)]
    args = [zqk, zv, zo, zg]
    if has_state:
        c0, n0, m0 = state
        in_specs += [per_b((N_MLSTM_HEADS, HEAD_DIM, HEAD_DIM)), per_b((N_MLSTM_HEADS, HEAD_DIM)),
                     per_b((1, N_MLSTM_HEADS))]
        args += [c0, n0, m0.reshape(bsz, 1, N_MLSTM_HEADS)]
    in_specs += [const((MLSTM_CONV, 2 * MLSTM_W)), const((1, 2 * MLSTM_W)), const((1, MLSTM_W))]
    args += [conv_w, conv_b.reshape(1, -1), norm_w.reshape(1, -1)]
    mix, c_new, n_new, m_new = pl.pallas_call(
        functools.partial(_mlstm_body, tb=tb, nbg=nbg, n_dummy=n_dummy, has_state=has_state),
        grid=(bsz // nbg, t_len // tb),
        in_specs=in_specs,
        out_specs=[row(MLSTM_W), per_b((N_MLSTM_HEADS, HEAD_DIM, HEAD_DIM)),
                   per_b((N_MLSTM_HEADS, HEAD_DIM)), per_b((1, N_MLSTM_HEADS))],
        out_shape=[jax.ShapeDtypeStruct((bsz, t_len, MLSTM_W), F32),
                   jax.ShapeDtypeStruct((bsz, N_MLSTM_HEADS, HEAD_DIM, HEAD_DIM), F32),
                   jax.ShapeDtypeStruct((bsz, N_MLSTM_HEADS, HEAD_DIM), F32),
                   jax.ShapeDtypeStruct((bsz, 1, N_MLSTM_HEADS), F32)],
        scratch_shapes=[pltpu.VMEM((nbg, 8 + tb, 2 * MLSTM_W), F32),
                        pltpu.VMEM((nbg, tb, 2 * MLSTM_W), F32),
                        pltpu.VMEM((nbg, N_PAIRS, LANES, LANES), F32),
                        pltpu.VMEM((nbg, N_PAIRS, 1, LANES), F32),
                        pltpu.VMEM((nbg, 1, LANES), F32)],
        compiler_params=pltpu.CompilerParams(dimension_semantics=("arbitrary", "arbitrary"),
                                             vmem_limit_bytes=VMEM_LIMIT),
        name="mlstm",
    )(*args)
    return mix, c_new, n_new, m_new.reshape(bsz, N_MLSTM_HEADS)


def _block_diag_queries(x, n_rows, heads=tuple(range(N_NSA_HEADS)), scale=HEAD_DIM ** -0.5):
    lane = lax.broadcasted_iota(I32, (n_rows, LANES), 1)
    low = lane < HEAD_DIM
    pieces = []
    for h in heads:
        pair = x[:, (h // 2) * LANES:(h // 2 + 1) * LANES]
        src_low = h % 2 == 0
        dst_low = h // GQ == 0
        if src_low != dst_low:
            pair = pltpu.roll(pair, HEAD_DIM, 1)
        pieces.append(jnp.where(low if dst_low else ~low, pair, 0.0))
    return (jnp.concatenate(pieces, axis=0) * scale).astype(BF16)


def _group_lanes(o, g):
    return o[:, g * HEAD_DIM:(g + 1) * HEAD_DIM]


def _gated_mix(gates, o_c, o_s, o_w, h, n_rows):
    g = h // GQ
    rs = slice(h * n_rows, (h + 1) * n_rows)
    col = lambda branch: gates[:, G_NSA + branch * N_NSA_HEADS + h:G_NSA + branch * N_NSA_HEADS + h + 1]
    return (col(0) * _group_lanes(o_c[rs], g) + col(1) * _group_lanes(o_s[rs], g)
            + col(2) * _group_lanes(o_w[rs], g))


SEL_TILE = 256
SEL_TILE_BLOCKS = SEL_TILE // SEL_BLOCK
FAR_GROUP = 2
WIN_BAND = WINDOW + NSA_QBLOCK
N_COLS = N_NSA_HEADS * NSA_QBLOCK
PROMPT_HEADS = tuple(GQ * (jj % N_KV_HEADS) + jj // N_KV_HEADS for jj in range(N_NSA_HEADS))
SEL_REL0 = 2 * SEL_TILE - SEL_BLOCK
SEL_BIAS_ROWS = SEL_REL0 + SEL_TILE


LOG2E = math.log2(math.e)


def _softmax2_rows(s):
    mx = jnp.max(s, 0, keepdims=True)
    mx = jnp.where(mx > NEG_INF, mx, 0.0)
    e = jnp.exp2(s - mx)
    return e / jnp.maximum(jnp.sum(e, 0, keepdims=True), 1e-30)


def _nsa_prompt_body(rb_ref, nq_ref, kv_ref, win_ref, zg_ref, pw_ref, out_ref,
                     ks_sc, vst_sc, wk_sc, wv_sc, kc_sc, pv_sc, vct_sc, bs_sc, bw_sc, fc_sc,
                     m_sc, l_sc, acc_sc, ch_sc, fs_sc, ft_sc, *, t_len):
    QB = NSA_QBLOCK
    n_cmp = t_len // CMP_BLOCK
    n_sel = t_len // SEL_BLOCK
    i = pl.program_id(1)
    hi = lax.Precision.HIGHEST

    @pl.when(i == 0)
    def _init():
        ks_sc[0:SEL_TILE, :] = jnp.zeros((SEL_TILE, LANES), BF16)
        vst_sc[0] = jnp.zeros((LANES, SEL_TILE), BF16)
        wk_sc[0:WINDOW, :] = jnp.zeros((WINDOW, LANES), BF16)
        wv_sc[0:WINDOW, :] = jnp.zeros((WINDOW, LANES), BF16)
        rows_per = 2 * SEL_TILE
        blocks_per = rows_per // CMP_BLOCK

        def fill(c, carry):
            r0 = pl.multiple_of(c * rows_per, rows_per)
            kv = kv_ref[0, pl.ds(r0, rows_per), :]
            ks_sc[pl.ds(SEL_TILE + r0, rows_per), :] = kv[:, 2 * LANES:3 * LANES].astype(BF16)
            for half in range(2):
                v_t = kv[half * SEL_TILE:(half + 1) * SEL_TILE, 3 * LANES:4 * LANES]
                vst_sc[2 * c + half + 1] = jnp.transpose(v_t).astype(BF16)
            w = win_ref[0, pl.ds(r0, rows_per), :]
            wk_sc[pl.ds(WINDOW + r0, rows_per), :] = w[:, 0:LANES].astype(BF16)
            wv_sc[pl.ds(WINDOW + r0, rows_per), :] = w[:, LANES:2 * LANES].astype(BF16)
            pooled = jnp.sum(kv[:, 0:2 * LANES].reshape(blocks_per, CMP_BLOCK, 2 * LANES) * pw_ref[...][None],
                             axis=1)
            c0 = pl.multiple_of(c * blocks_per, blocks_per)
            kc_sc[pl.ds(c0, blocks_per), :] = pooled[:, 0:LANES].astype(BF16)
            pv_sc[pl.ds(c0, blocks_per), :] = pooled[:, LANES:2 * LANES]
            return carry

        lax.fori_loop(0, t_len // rows_per, fill, 0)
        vct_sc[...] = jnp.transpose(pv_sc[...]).astype(BF16)

        for jj, h in enumerate(PROMPT_HEADS):
            cs = slice(jj * QB, (jj + 1) * QB)
            u_i = lax.broadcasted_iota(I32, (SEL_BIAS_ROWS, QB), 0)
            q_i = lax.broadcasted_iota(I32, (SEL_BIAS_ROWS, QB), 1)
            dist = q_i - (u_i - SEL_REL0)
            near = (_bias_from_dist(dist, rb_ref, h) - rb_ref[N_BUCKETS - 1, h]) * LOG2E
            bs_sc[:, cs] = jnp.where(dist >= 0, near, NEG_INF)
            s_i = lax.broadcasted_iota(I32, (WIN_BAND, QB), 0)
            q_i = lax.broadcasted_iota(I32, (WIN_BAND, QB), 1)
            dist = q_i + WINDOW - s_i
            bw_sc[:, cs] = jnp.where((dist >= 0) & (dist <= WINDOW), _bias_from_dist(dist, rb_ref, h) * LOG2E,
                                     NEG_INF)
            c_i = lax.broadcasted_iota(I32, (n_cmp, QB), 0)
            q_i = lax.broadcasted_iota(I32, (n_cmp, QB), 1)
            rel = jnp.where(c_i < n_cmp // 2, c_i, c_i - n_cmp)
            dist = q_i - CMP_BLOCK * rel - (CMP_BLOCK - 1)
            dist = jnp.where((c_i >= 2) & (c_i < n_cmp // 2), FAR_DIST, dist)
            fc_sc[:, cs] = _bias_from_dist(dist, rb_ref, h) * LOG2E

    qbd = _block_diag_queries(nq_ref[0], QB, PROMPT_HEADS, HEAD_DIM ** -0.5 * LOG2E)

    w0 = pl.multiple_of(i * QB, QB)
    s_w = _dot_nt(wk_sc[pl.ds(w0, WIN_BAND), :], qbd) + bw_sc[...]
    in_seq = lax.broadcasted_iota(I32, (WIN_BAND, N_COLS), 0) >= WINDOW - i * QB
    p_w = _softmax2_rows(jnp.where(in_seq, s_w, NEG_INF))
    o_w = _dot_tn(wv_sc[pl.ds(w0, WIN_BAND), :], p_w.astype(BF16))

    s_c = _dot_nt(kc_sc[...], qbd) + pltpu.roll(fc_sc[...], 2 * i, 0)
    c_i = lax.broadcasted_iota(I32, (n_cmp, N_COLS), 0)
    q_i = lax.broadcasted_iota(I32, (n_cmp, N_COLS), 1) % QB
    p_c = _softmax2_rows(jnp.where(c_i * CMP_BLOCK + (CMP_BLOCK - 1) <= i * QB + q_i, s_c, NEG_INF))
    o_c = _dot(vct_sc[...], p_c.astype(BF16))

    p_sum = p_c[:, 0:LANES]
    for j in range(1, GQ):
        p_sum = p_sum + p_c[:, j * LANES:(j + 1) * LANES]
    ratio = SEL_BLOCK // CMP_BLOCK
    pair = (lax.broadcasted_iota(I32, (n_sel, n_cmp), 1) // ratio
            == lax.broadcasted_iota(I32, (n_sel, n_cmp), 0)).astype(F32)
    imp = _dot(pair, p_sum, hi)
    blk = lax.broadcasted_iota(I32, (n_sel, LANES), 0)
    score = jnp.where(blk == i, jnp.inf, jnp.where(blk < i, imp, NEG_INF))

    def pick(_, carry):
        work, chosen = carry
        best = jnp.max(work, 0, keepdims=True)
        first = jnp.min(jnp.where(work == best, blk, n_sel), 0, keepdims=True)
        hit = blk == first
        return jnp.where(hit, NEG_INF, work), jnp.where(hit, 1.0, chosen)

    _, chosen = lax.fori_loop(0, SEL_TOPK, pick, (score, jnp.zeros((n_sel, LANES), F32)), unroll=True)
    ch_sc[0:8, :] = jnp.full((8, LANES), NEG_INF, F32)
    ch_sc[8:8 + n_sel, :] = jnp.where((score > NEG_INF) & (chosen > 0.5), 0.0, NEG_INF)

    def tile_scores(slot, bias):
        k_t = ks_sc[pl.ds(pl.multiple_of(slot * SEL_TILE, SEL_TILE), SEL_TILE), :]
        s = _dot_nt(k_t, qbd)
        if bias is not None:
            s = s + bias
        parts = []
        for r in range(SEL_TILE_BLOCKS):
            row = ch_sc[pl.ds(8 - SEL_TILE_BLOCKS + slot * SEL_TILE_BLOCKS + r, 1), :]
            parts.append(s[r * SEL_BLOCK:(r + 1) * SEL_BLOCK, :] + jnp.concatenate([row] * GQ, axis=1))
        return jnp.concatenate(parts, axis=0)

    def group_max(scores):
        return functools.reduce(jnp.maximum, [jnp.max(s, 0, keepdims=True) for s in scores])

    def sel_group(slots, scores, t_max, first):
        m_new = t_max if first else jnp.maximum(m_sc[...], t_max)
        exps = [jnp.exp2(s - m_new) for s in scores]
        l_new = functools.reduce(jnp.add, [jnp.sum(e, 0, keepdims=True) for e in exps])
        acc_new = functools.reduce(jnp.add, [_dot(vst_sc[slot], e.astype(BF16)) for slot, e in zip(slots, exps)])
        if first:
            l_sc[...] = l_new
            acc_sc[...] = acc_new
        else:
            scale = jnp.exp2(m_sc[...] - m_new)
            l_sc[...] = scale * l_sc[...] + l_new
            acc_sc[...] = scale * acc_sc[...] + acc_new
        m_sc[...] = m_new

    last = i // SEL_TILE_BLOCKS
    u0 = pl.multiple_of(SEL_REL0 - SEL_TILE - SEL_BLOCK * (i % SEL_TILE_BLOCKS), SEL_BLOCK)
    near = [tile_scores(last + 1, bs_sc[pl.ds(u0 + SEL_TILE, SEL_TILE), :]),
            tile_scores(last, bs_sc[pl.ds(u0, SEL_TILE), :])]
    sel_group((last + 1, last), near, group_max(near), True)

    def far_slots(p):
        return [jnp.where(FAR_GROUP * p + 1 + n < last, FAR_GROUP * p + 1 + n, 0) for n in range(FAR_GROUP)]

    def far_scores(p):
        buf = p % 2
        scores = [tile_scores(slot, None) for slot in far_slots(p)]
        for n, s in enumerate(scores):
            fs_sc[buf, n] = s
        ft_sc[buf] = group_max(scores)

    def far_group(p, carry):
        buf = p % 2
        sel_group(far_slots(p), [fs_sc[buf, n] for n in range(FAR_GROUP)], ft_sc[buf], False)
        far_scores(p + 1)
        return carry

    far_scores(0)
    lax.fori_loop(0, (last - 1 + FAR_GROUP - 1) // FAR_GROUP, far_group, 0)
    o_s = acc_sc[...] / jnp.maximum(l_sc[...], 1e-30)

    eye = (lax.broadcasted_iota(I32, (3 * N_NSA_HEADS + 8, LANES), 0) + G_NSA
           == lax.broadcasted_iota(I32, (3 * N_NSA_HEADS + 8, LANES), 1)).astype(F32)
    gates = jax.nn.sigmoid(_dot_nt(eye, zg_ref[0], hi))
    pieces = []
    for h in range(N_NSA_HEADS):
        g = h // GQ
        jj = PROMPT_HEADS.index(h)
        part = lambda o: o[g * HEAD_DIM:(g + 1) * HEAD_DIM, jj * QB:(jj + 1) * QB]
        pieces.append(gates[h:h + 1, :] * part(o_c)
                      + gates[N_NSA_HEADS + h:N_NSA_HEADS + h + 1, :] * part(o_s)
                      + gates[2 * N_NSA_HEADS + h:2 * N_NSA_HEADS + h + 1, :] * part(o_w))
    mix_t = jnp.concatenate(pieces, axis=0)
    mix_t = jnp.concatenate([mix_t, jnp.zeros((NSA_W, LANES - QB), F32)], axis=1)
    out_ref[0] = jnp.transpose(mix_t)[0:QB, :]


def _nsa_prompt(nq, kv, win, zg, pool_w2, rel_bias):
    bsz, t_len, _ = nq.shape
    assert t_len // CMP_BLOCK == LANES and t_len % (2 * SEL_TILE) == 0
    n_cmp = t_len // CMP_BLOCK
    per_b = lambda w: pl.BlockSpec((1, t_len, w), lambda b, i: (b, 0, 0))
    blk = lambda w: pl.BlockSpec((1, NSA_QBLOCK, w), lambda b, i: (b, i, 0))
    return pl.pallas_call(
        functools.partial(_nsa_prompt_body, t_len=t_len),
        grid=(bsz, t_len // NSA_QBLOCK),
        in_specs=[pl.BlockSpec(memory_space=pltpu.SMEM),
                  blk(NSA_W), per_b(4 * KV_W), per_b(2 * KV_W), blk(LANES),
                  pl.BlockSpec((CMP_BLOCK, 2 * LANES), lambda b, i: (0, 0))],
        out_specs=blk(NSA_W),
        out_shape=jax.ShapeDtypeStruct((bsz, t_len, NSA_W), F32),
        scratch_shapes=[pltpu.VMEM((SEL_TILE + t_len, LANES), BF16),
                        pltpu.VMEM((1 + t_len // SEL_TILE, LANES, SEL_TILE), BF16),
                        pltpu.VMEM((WINDOW + t_len, LANES), BF16),
                        pltpu.VMEM((WINDOW + t_len, LANES), BF16),
                        pltpu.VMEM((n_cmp, LANES), BF16),
                        pltpu.VMEM((n_cmp, LANES), F32),
                        pltpu.VMEM((LANES, n_cmp), BF16),
                        pltpu.VMEM((SEL_BIAS_ROWS, N_COLS), F32),
                        pltpu.VMEM((WIN_BAND, N_COLS), F32),
                        pltpu.VMEM((n_cmp, N_COLS), F32),
                        pltpu.VMEM((1, N_COLS), F32),
                        pltpu.VMEM((1, N_COLS), F32),
                        pltpu.VMEM((LANES, N_COLS), F32),
                        pltpu.VMEM((8 + t_len // SEL_BLOCK, LANES), F32),
                        pltpu.VMEM((2, FAR_GROUP, SEL_TILE, N_COLS), F32),
                        pltpu.VMEM((2, 1, N_COLS), F32)],
        compiler_params=pltpu.CompilerParams(dimension_semantics=("arbitrary", "arbitrary"),
                                             vmem_limit_bytes=VMEM_LIMIT),
        name="nsa_prompt",
    )(rel_bias, nq, kv, win, zg, pool_w2)


CMP_PAGES = 32
CMP_SLOTS = 3
N_PICK = SEL_TOPK - 1
N_OWNERS = N_KV_HEADS * N_NEW
N_FETCH = N_OWNERS * N_PICK
OWNER_KEYS = N_PICK * PAGE_SIZE


def _sample_row_ids():
    r = lax.broadcasted_iota(I32, (N_QROWS, 1), 0)
    return r % SEQ_PAD - (SEQ_PAD - N_NEW), r // SEQ_PAD


def _sample_bias(dist, rb_ref):
    _, head = _sample_row_ids()
    col = lambda k: sum(jnp.where(head == h, rb_ref[k, h], 0.0) for h in range(N_NSA_HEADS))
    acc = jnp.broadcast_to(col(0), dist.shape)
    for k in range(1, N_BUCKETS):
        acc = jnp.where(dist >= BUCKET_THR[k], col(k), acc)
    return acc


def _softmax_two(s1, s2):
    mx = jnp.maximum(jnp.max(s1, -1, keepdims=True), jnp.max(s2, -1, keepdims=True))
    mx = jnp.where(mx > NEG_INF, mx, 0.0)
    e1 = jnp.exp(s1 - mx)
    e2 = jnp.exp(s2 - mx)
    den = jnp.maximum(jnp.sum(e1, -1, keepdims=True) + jnp.sum(e2, -1, keepdims=True), 1e-30)
    return e1 / den, e2 / den


def _nsa_cmp_body(pt_ref, rb_ref, nq_ref, pwt_ref, cache_ref, oc_ref, idx_ref,
                  buf, sem, seg_sc, kct_sc, vct_sc, *, n_pages, past_len):
    b = pl.program_id(0)
    nb = pl.num_programs(0)
    n_chunks = n_pages // CMP_PAGES
    chunk_rows = CMP_PAGES * PAGE_SIZE
    chunk_blocks = chunk_rows // CMP_BLOCK
    n_cmp = past_len // CMP_BLOCK
    n_sel_past = past_len // SEL_BLOCK
    n_prob = N_KV_HEADS * SEQ_PAD
    total = nb * n_chunks
    hi = lax.Precision.HIGHEST

    def page_copy(gi, p):
        page = pt_ref[(gi // n_chunks) * n_pages + (gi % n_chunks) * CMP_PAGES + p]
        slot = gi % CMP_SLOTS
        return pltpu.make_async_copy(cache_ref.at[page, pl.ds(0, 2 * LANES), :], buf.at[slot, p], sem.at[slot])

    def start_chunk(gi):
        for p in range(CMP_PAGES):
            page_copy(gi, p).start()

    def wait_chunk(gi):
        for p in range(CMP_PAGES):
            page_copy(gi, p).wait()

    @pl.when(b == 0)
    def _prologue():
        seg_sc[...] = (lax.broadcasted_iota(I32, (chunk_rows, chunk_blocks), 0) // CMP_BLOCK
                       == lax.broadcasted_iota(I32, (chunk_rows, chunk_blocks), 1)).astype(BF16)
        for gi in range(min(CMP_SLOTS - 1, total)):
            start_chunk(gi)

    def chunk_step(c, carry):
        gi = b * n_chunks + c
        ahead = gi + (CMP_SLOTS - 1)

        @pl.when(ahead < total)
        def _prefetch():
            start_chunk(ahead)

        wait_chunk(gi)
        slot = gi % CMP_SLOTS
        y = jnp.concatenate([(buf[slot, p] * pwt_ref[...]).astype(BF16) for p in range(CMP_PAGES)], axis=1)
        pooled = _dot(y, seg_sc[...])
        kct_sc[c] = pooled[0:LANES].astype(BF16)
        vct_sc[c] = pooled[LANES:2 * LANES].astype(BF16)
        return carry

    lax.fori_loop(0, n_chunks, chunk_step, 0)

    qbd = _block_diag_queries(nq_ref[0], SEQ_PAD)
    tok, _ = _sample_row_ids()
    s_c = jnp.concatenate([_dot(qbd, kct_sc[c]) for c in range(n_chunks)], axis=1)
    near0 = n_cmp - LANES
    assert past_len - (SEQ_PAD - N_NEW) - (CMP_BLOCK * (near0 - 1) + CMP_BLOCK - 1) >= FAR_DIST
    dist = past_len + tok - ((lax.broadcasted_iota(I32, (N_QROWS, LANES), 1) + near0) * CMP_BLOCK + CMP_BLOCK - 1)
    near = jnp.where(dist >= 0, s_c[:, near0:] + _sample_bias(dist, rb_ref), NEG_INF)
    far = s_c[:, :near0] + _sample_bias(jnp.full((N_QROWS, 1), FAR_DIST, I32), rb_ref)
    p_c = _masked_softmax(jnp.concatenate([far, near], axis=1))
    o_c = _dot_nt(p_c[:, 0:chunk_blocks].astype(BF16), vct_sc[0])
    for c in range(1, n_chunks):
        o_c = o_c + _dot_nt(p_c[:, c * chunk_blocks:(c + 1) * chunk_blocks].astype(BF16), vct_sc[c])
    oc_ref[0] = o_c

    p_sum = jnp.sum(p_c.reshape(N_KV_HEADS, GQ, SEQ_PAD, n_cmp), axis=1).reshape(n_prob, n_cmp)
    p_sum = jnp.concatenate([p_sum, jnp.zeros((LANES - n_prob, n_cmp), F32)], axis=0)
    ratio = SEL_BLOCK // CMP_BLOCK
    pair = (lax.broadcasted_iota(I32, (n_sel_past, n_cmp), 1) // ratio
            == lax.broadcasted_iota(I32, (n_sel_past, n_cmp), 0)).astype(F32)
    imp = _dot_nt(pair, p_sum, hi)
    blk = lax.broadcasted_iota(I32, (n_sel_past, LANES), 0)
    pick_i = lax.broadcasted_iota(I32, (SEL_TOPK, LANES), 0)

    def pick(k, carry):
        work, ids = carry
        best = jnp.max(work, 0, keepdims=True)
        first = jnp.min(jnp.where(work == best, blk, n_sel_past), 0, keepdims=True)
        return jnp.where(blk == first, NEG_INF, work), jnp.where(pick_i == k, first, ids)

    _, ids = lax.fori_loop(0, N_PICK, pick, (imp, jnp.zeros((SEL_TOPK, LANES), I32)), unroll=True)
    idx_ref[0] = ids


def _nsa_cmp(page_table, rel_bias, nq_s, pool_wt, cache3):
    dec_b, n_pages = page_table.shape
    past_len = n_pages * PAGE_SIZE
    assert n_pages % CMP_PAGES == 0 and past_len // SEL_BLOCK >= N_PICK
    n_chunks = n_pages // CMP_PAGES
    chunk_rows = CMP_PAGES * PAGE_SIZE
    chunk_blocks = chunk_rows // CMP_BLOCK
    assert chunk_blocks == LANES
    grid_spec = pltpu.PrefetchScalarGridSpec(
        num_scalar_prefetch=1,
        grid=(dec_b,),
        in_specs=[pl.BlockSpec(memory_space=pltpu.SMEM),
                  pl.BlockSpec((1, SEQ_PAD, NSA_W), lambda b, pt: (b, 0, 0)),
                  pl.BlockSpec((2 * LANES, PAGE_SIZE), lambda b, pt: (0, 0)),
                  pl.BlockSpec(memory_space=pl.ANY)],
        out_specs=[pl.BlockSpec((1, N_QROWS, LANES), lambda b, pt: (b, 0, 0)),
                   pl.BlockSpec((1, SEL_TOPK, LANES), lambda b, pt: (b, 0, 0))],
        scratch_shapes=[pltpu.VMEM((CMP_SLOTS, CMP_PAGES, 2 * LANES, PAGE_SIZE), F32),
                        pltpu.SemaphoreType.DMA((CMP_SLOTS,)),
                        pltpu.VMEM((chunk_rows, chunk_blocks), BF16),
                        pltpu.VMEM((n_chunks, LANES, chunk_blocks), BF16),
                        pltpu.VMEM((n_chunks, LANES, chunk_blocks), BF16)],
    )
    return pl.pallas_call(
        functools.partial(_nsa_cmp_body, n_pages=n_pages, past_len=past_len),
        grid_spec=grid_spec,
        out_shape=[jax.ShapeDtypeStruct((dec_b, N_QROWS, LANES), F32),
                   jax.ShapeDtypeStruct((dec_b, SEL_TOPK, LANES), I32)],
        compiler_params=pltpu.CompilerParams(dimension_semantics=("arbitrary",),
                                             vmem_limit_bytes=VMEM_LIMIT),
        name="nsa_sample_cmp",
    )(page_table.reshape(-1), rel_bias, nq_s, pool_wt, cache3)


def _nsa_sel_body(pt_ref, ids_ref, rb_ref, nq_ref, kvn_ref, winc_ref, winn_ref, zg_ref, oc_ref, idv_ref,
                  cache_ref, out_ref, buf, sem, *, n_pages, past_len):
    b = pl.program_id(0)
    nb = pl.num_programs(0)
    hi = lax.Precision.HIGHEST

    def block_copy(bb, n, slot):
        blk = ids_ref[bb * N_FETCH + n]
        page = pt_ref[bb * n_pages + blk // 2]
        return pltpu.make_async_copy(cache_ref.at[page, pl.ds(2 * LANES, 2 * LANES), :], buf.at[slot, n],
                                     sem.at[slot])

    def start_all(bb, slot):
        def go(n, carry):
            block_copy(bb, n, slot).start()
            return carry
        lax.fori_loop(0, N_FETCH, go, 0)

    def wait_all(bb, slot):
        def go(n, carry):
            block_copy(bb, n, slot).wait()
            return carry
        lax.fori_loop(0, N_FETCH, go, 0)

    slot = b % 2

    @pl.when(b == 0)
    def _prologue():
        start_all(0, 0)

    @pl.when(b + 1 < nb)
    def _prefetch():
        start_all(b + 1, 1 - slot)

    qbd = _block_diag_queries(nq_ref[0], SEQ_PAD)
    tok, head = _sample_row_ids()
    owner = (head // GQ) * N_NEW + tok

    wc = winc_ref[0]
    wn = winn_ref[0]
    r_i = lax.broadcasted_iota(I32, (N_QROWS, WINDOW), 1)
    dist1 = WINDOW + tok - r_i
    s1 = _dot(qbd, wc[0:LANES, :].astype(BF16)) + _sample_bias(dist1, rb_ref)
    s1 = jnp.where((dist1 >= 0) & (dist1 <= WINDOW), s1, NEG_INF)
    n_i = lax.broadcasted_iota(I32, (N_QROWS, SEQ_PAD), 1) - (SEQ_PAD - N_NEW)
    dist2 = tok - n_i
    new_ok = (n_i >= 0) & (dist2 >= 0)
    bias2 = _sample_bias(dist2, rb_ref)
    s2 = jnp.where(new_ok, _dot_nt(qbd, wn[:, 0:LANES].astype(BF16)) + bias2, NEG_INF)
    p1, p2 = _softmax_two(s1, s2)
    o_w = (_dot_nt(p1.astype(BF16), wc[LANES:2 * LANES, :].astype(BF16))
           + _dot(p2.astype(BF16), wn[:, LANES:2 * LANES].astype(BF16)))

    wait_all(b, slot)

    def owner_tiles(o, r0):
        return jnp.concatenate([buf[slot, o * N_PICK + k, r0:r0 + LANES, :] for k in range(N_PICK)],
                               axis=1).astype(BF16)

    s_f = jnp.full((N_QROWS, OWNER_KEYS), NEG_INF, F32)
    for o in range(N_OWNERS):
        s_f = jnp.where(owner == o, _dot(qbd, owner_tiles(o, 0)), s_f)
    spread = (lax.broadcasted_iota(I32, (SEL_TOPK, OWNER_KEYS), 0)
              == lax.broadcasted_iota(I32, (SEL_TOPK, OWNER_KEYS), 1) // PAGE_SIZE).astype(F32)
    blk_of_key = _dot_tn(idv_ref[0].astype(F32), spread, hi)
    blk_rows = jnp.concatenate([blk_of_key[(h // GQ) * SEQ_PAD:(h // GQ + 1) * SEQ_PAD]
                                for h in range(N_NSA_HEADS)], axis=0).astype(I32)
    blocks_per_page = PAGE_SIZE // SEL_BLOCK
    in_page = lax.broadcasted_iota(I32, (N_QROWS, OWNER_KEYS), 1) % PAGE_SIZE
    in_block = in_page // SEL_BLOCK == blk_rows % blocks_per_page
    dist_s = past_len + tok - ((blk_rows // blocks_per_page) * PAGE_SIZE + in_page)
    s_f = jnp.where((tok >= 0) & in_block & (dist_s >= 0), s_f + _sample_bias(dist_s, rb_ref), NEG_INF)
    kvn = kvn_ref[0]
    s_n = jnp.where(new_ok, _dot_nt(qbd, kvn[:, 2 * LANES:3 * LANES].astype(BF16)) + bias2, NEG_INF)
    p_f, p_n = _softmax_two(s_f, s_n)
    o_s = _dot(p_n.astype(BF16), kvn[:, 3 * LANES:4 * LANES].astype(BF16))
    for o in range(N_OWNERS):
        o_s = o_s + _dot_nt(jnp.where(owner == o, p_f, 0.0).astype(BF16), owner_tiles(o, LANES))

    o_c = oc_ref[0]
    gates = jax.nn.sigmoid(zg_ref[0])
    is_new = lax.broadcasted_iota(I32, (SEQ_PAD, HEAD_DIM), 0) >= SEQ_PAD - N_NEW
    for h in range(N_NSA_HEADS):
        mix = _gated_mix(gates, o_c, o_s, o_w, h, SEQ_PAD)
        out_ref[0, :, h * HEAD_DIM:(h + 1) * HEAD_DIM] = jnp.where(is_new, mix, 0.0)


def _nsa_sel(page_table, ids, ids_rows, rel_bias, nq_s, kv_s, win_cache, win_s, zg_s, o_c, cache3):
    dec_b, n_pages = page_table.shape
    past_len = n_pages * PAGE_SIZE
    seq = lambda w: pl.BlockSpec((1, SEQ_PAD, w), lambda b, pt, ix: (b, 0, 0))
    grid_spec = pltpu.PrefetchScalarGridSpec(
        num_scalar_prefetch=2,
        grid=(dec_b,),
        in_specs=[pl.BlockSpec(memory_space=pltpu.SMEM),
                  seq(NSA_W), seq(4 * KV_W),
                  pl.BlockSpec((1, 2 * KV_W, WINDOW), lambda b, pt, ix: (b, 0, 0)),
                  seq(2 * KV_W), seq(LANES),
                  pl.BlockSpec((1, N_QROWS, LANES), lambda b, pt, ix: (b, 0, 0)),
                  pl.BlockSpec((1, SEL_TOPK, LANES), lambda b, pt, ix: (b, 0, 0)),
                  pl.BlockSpec(memory_space=pl.ANY)],
        out_specs=seq(NSA_W),
        scratch_shapes=[pltpu.VMEM((2, N_FETCH, 2 * LANES, PAGE_SIZE), F32),
                        pltpu.SemaphoreType.DMA((2,))],
    )
    return pl.pallas_call(
        functools.partial(_nsa_sel_body, n_pages=n_pages, past_len=past_len),
        grid_spec=grid_spec,
        out_shape=jax.ShapeDtypeStruct((dec_b, SEQ_PAD, NSA_W), F32),
        compiler_params=pltpu.CompilerParams(dimension_semantics=("arbitrary",),
                                             vmem_limit_bytes=VMEM_LIMIT),
        name="nsa_sample_sel",
    )(page_table.reshape(-1), ids.reshape(-1), rel_bias, nq_s, kv_s, win_cache, win_s, zg_s, o_c, ids_rows, cache3)


FF_CHUNK = 896


def _ffn_body(*refs, tm, has_override):
    if has_override:
        (x_ref, mm_ref, mn_ref, ov_ref, woa_ref, wob_ref, l1w_ref, l1b_ref, wup_ref, bup_ref, cw_ref, cb_ref,
         wdn_ref, bdn_ref, l2w_ref, l2b_ref, y_ref, tail_ref, carry_sc, ubuf) = refs
    else:
        (x_ref, mm_ref, mn_ref, woa_ref, wob_ref, l1w_ref, l1b_ref, wup_ref, bup_ref, cw_ref, cb_ref,
         wdn_ref, bdn_ref, l2w_ref, l2b_ref, y_ref, tail_ref, carry_sc, ubuf) = refs
    j = pl.program_id(1)

    @pl.when(j == 0)
    def _init():
        carry_sc[...] = jnp.zeros(carry_sc.shape, F32)

    x = x_ref[0]
    h = ALPHA * x + (_dot(mm_ref[0].astype(BF16), woa_ref[...]) + _dot(mn_ref[0].astype(BF16), wob_ref[...]))
    x1 = _layer_norm(h, l1w_ref[...], l1b_ref[...])
    x1b = x1.astype(BF16)
    if has_override:
        state_row = lax.broadcasted_iota(I32, (tm, FF_CHUNK), 0) % SEQ_PAD < SEQ_PAD - N_NEW

    def conv_half(c0):
        cols = slice(c0, c0 + FF_CHUNK)
        u = _dot(x1b, wup_ref[:, cols]) + bup_ref[:, cols]
        if has_override:
            u = jnp.where(state_row, ov_ref[0, :, cols], u)
        ubuf[0:8, :] = carry_sc[:, cols]
        ubuf[8:8 + tm, :] = u
        out = cb_ref[:, cols] + ubuf[6:6 + tm, :] * cw_ref[0:1, cols]
        out = out + ubuf[7:7 + tm, :] * cw_ref[1:2, cols]
        out = out + u * cw_ref[2:3, cols]
        carry_sc[:, cols] = u[tm - 8:tm, :]
        tail_ref[0, 0, :, cols] = u if has_override else u[tm - 8:tm, :]
        return out

    ff = jnp.zeros((tm, D_MODEL), F32)
    for c in range(D_FF // FF_CHUNK):
        ga = conv_half(c * FF_CHUNK)
        gb = conv_half(D_FF + c * FF_CHUNK)
        gelu = ga * (0.5 * (1.0 + jnp.tanh(math.sqrt(2.0 / math.pi) * (ga + 0.044715 * (ga * ga * ga)))))
        ff = ff + _dot((gelu * gb).astype(BF16), wdn_ref[c * FF_CHUNK:(c + 1) * FF_CHUNK, :])
    ff = ff + bdn_ref[...]
    y_ref[0] = _layer_norm(ALPHA * x1 + ff, l2w_ref[...], l2b_ref[...])


def _ffn(x, mix_m, mix_n, override, wts):
    bsz, t_len, _ = x.shape
    tm = min(t_len, 512)
    nt = t_len // tm
    has_override = override is not None
    tail_rows = tm if has_override else 8
    row = lambda w: pl.BlockSpec((1, tm, w), lambda b, j: (b, j, 0))
    const = lambda shape: pl.BlockSpec(shape, lambda b, j: (0,) * len(shape), pipeline_mode=pl.Buffered(1))
    in_specs = [row(D_MODEL), row(MLSTM_W), row(NSA_W)]
    args = [x, mix_m, mix_n]
    if has_override:
        in_specs.append(row(2 * D_FF))
        args.append(override)
    in_specs += [const(w.shape) for w in wts]
    args += list(wts)
    y, tail = pl.pallas_call(
        functools.partial(_ffn_body, tm=tm, has_override=has_override),
        grid=(bsz, nt),
        in_specs=in_specs,
        out_specs=[row(D_MODEL), pl.BlockSpec((1, 1, tail_rows, 2 * D_FF), lambda b, j: (b, j, 0, 0))],
        out_shape=[jax.ShapeDtypeStruct((bsz, t_len, D_MODEL), F32),
                   jax.ShapeDtypeStruct((bsz, nt, tail_rows, 2 * D_FF), F32)],
        scratch_shapes=[pltpu.VMEM((8, 2 * D_FF), F32), pltpu.VMEM((8 + tm, FF_CHUNK), F32)],
        compiler_params=pltpu.CompilerParams(dimension_semantics=("arbitrary", "arbitrary"),
                                             vmem_limit_bytes=VMEM_LIMIT),
        name="ffn",
    )(*args)
    return y, tail


def _permute_in_proj(w_in, b_in):
    gates_a = slice(2 * MLSTM_W + 2 * MLSTM_W, 2 * MLSTM_W + 2 * MLSTM_W + 2 * N_MLSTM_HEADS)
    nsa0 = gates_a.stop
    nsa1 = nsa0 + NSA_W + 6 * KV_W

    def perm(a):
        pad = IN_PAD - a.shape[-1]
        parts = [a[..., :gates_a.start], a[..., nsa0:nsa1], a[..., gates_a], a[..., nsa1:]]
        parts.append(jnp.zeros(a.shape[:-1] + (pad,), a.dtype))
        return jnp.concatenate(parts, axis=-1)

    return perm(w_in).astype(BF16), perm(b_in).reshape(1, IN_PAD)


def _front_pad(a, n):
    return jnp.pad(a, ((0, 0), (n, 0), (0, 0)))


def kernel(x_prompt, x_sample, cache_nsa_kv, cache_win_kv, state_mlstm_c, state_mlstm_n, state_mlstm_m,
           state_mlstm_conv, state_ffn_conv, page_table,
           w_in, b_in, mlstm_conv_w, mlstm_conv_b, mlstm_norm_w, nsa_pool_w, rel_bias, w_out,
           ln1_w, ln1_b, w_up, b_up, ffn_conv_w, ffn_conv_b, w_down, b_down, ln2_w, ln2_b):
    bsz, t_len, _ = x_prompt.shape
    dec_b, n_new, _ = x_sample.shape
    assert n_new == N_NEW and cache_win_kv.shape[1] == WINDOW

    w_p, b_p = _permute_in_proj(w_in, b_in)
    pool_w2 = jnp.concatenate([jnp.repeat(nsa_pool_w[0], HEAD_DIM, axis=-1),
                               jnp.repeat(nsa_pool_w[1], HEAD_DIM, axis=-1)], axis=-1)
    row = lambda v: v.reshape(1, -1)
    ffn_w = (w_out[:MLSTM_W].astype(BF16), w_out[MLSTM_W:].astype(BF16), row(ln1_w), row(ln1_b),
             w_up.astype(BF16), row(b_up), ffn_conv_w, row(ffn_conv_b), w_down.astype(BF16), row(b_down),
             row(ln2_w), row(ln2_b))

    zqk, zv, zo, nq, kv, win, zg, kv_t = _proj_in(x_prompt.reshape(bsz * t_len, D_MODEL), w_p, b_p, seq_len=t_len)
    seq3 = lambda a: a.reshape(bsz, t_len, a.shape[-1])
    zqk, zv, zo, nq, kv, win, zg = map(seq3, (zqk, zv, zo, nq, kv, win, zg))
    mix_m, c_p, n_p, m_p = _mlstm(zqk, zv, zo, zg, None, mlstm_conv_w, mlstm_conv_b, mlstm_norm_w, 0)
    mix_n = _nsa_prompt(nq, kv, win, zg, pool_w2, rel_bias)
    y_p, tail_p = _ffn(x_prompt, mix_m, mix_n, None, ffn_w)
    kv_p = jnp.transpose(kv_t.reshape(bsz, 4, N_KV_HEADS, HEAD_DIM, t_len), (0, 4, 1, 2, 3))
    win_p = win[:, t_len - min(WINDOW, t_len):].reshape(bsz, -1, 2, N_KV_HEADS, HEAD_DIM)
    mconv_p = zqk[:, t_len - (MLSTM_CONV - 1):]
    fconv_p = tail_p[:, -1, 8 - (FFN_CONV - 1):]

    pad = SEQ_PAD - N_NEW
    xs = _front_pad(x_sample, pad)
    szqk, szv, szo, snq, skv, swin, szg = _proj_in(xs.reshape(dec_b * SEQ_PAD, D_MODEL), w_p, b_p)
    sseq = lambda a: a.reshape(dec_b, SEQ_PAD, a.shape[-1])
    szqk, szv, szo, snq, skv, swin, szg = map(sseq, (szqk, szv, szo, snq, skv, swin, szg))
    lead = MLSTM_CHUNK - N_NEW
    new = lambda a: a[:, pad:]
    m_qk = jnp.concatenate([jnp.zeros((dec_b, lead - (MLSTM_CONV - 1), 2 * MLSTM_W), F32),
                            state_mlstm_conv.astype(F32), new(szqk)], axis=1)
    mix_ms, c_s, n_s, m_s = _mlstm(m_qk, _front_pad(new(szv), lead), _front_pad(new(szo), lead),
                                   _front_pad(new(szg), lead),
                                   (state_mlstm_c, state_mlstm_n, state_mlstm_m),
                                   mlstm_conv_w, mlstm_conv_b, mlstm_norm_w, lead)
    mix_ms = mix_ms[:, MLSTM_CHUNK - SEQ_PAD:]
    cache3 = jnp.transpose(cache_nsa_kv, (0, 2, 3, 4, 1)).reshape(cache_nsa_kv.shape[0], 4 * KV_W, PAGE_SIZE)
    pool_wt = jnp.tile(jnp.repeat(jnp.transpose(nsa_pool_w, (0, 2, 1)).reshape(2 * N_KV_HEADS, CMP_BLOCK),
                                  HEAD_DIM, axis=0), (1, PAGE_SIZE // CMP_BLOCK))
    o_c, ids_rows = _nsa_cmp(page_table, rel_bias, snq, pool_wt, cache3)
    ids = jnp.transpose(ids_rows[:, :N_PICK, :N_KV_HEADS * SEQ_PAD], (0, 2, 1))
    ids = ids.reshape(dec_b, N_KV_HEADS, SEQ_PAD, N_PICK)[:, :, pad:]
    win_cache = jnp.transpose(cache_win_kv, (0, 2, 3, 4, 1)).reshape(dec_b, 2 * KV_W, WINDOW)
    mix_ns = _nsa_sel(page_table, ids, ids_rows, rel_bias, snq, skv, win_cache, swin, szg, o_c, cache3)
    override = jnp.concatenate([jnp.zeros((dec_b, pad - (FFN_CONV - 1), 2 * D_FF), F32),
                                state_ffn_conv.astype(F32),
                                jnp.zeros((dec_b, N_NEW, 2 * D_FF), F32)], axis=1)
    flat = lambda a: a.reshape(1, dec_b * SEQ_PAD, a.shape[-1])
    y_s8, u_s = _ffn(flat(xs), flat(mix_ms), flat(mix_ns), flat(override), ffn_w)
    y_s = y_s8.reshape(dec_b, SEQ_PAD, D_MODEL)[:, pad:]
    kv_s = new(skv).reshape(dec_b, N_NEW, 4, N_KV_HEADS, HEAD_DIM)
    win_s = jnp.concatenate([cache_win_kv[:, N_NEW:], new(swin).reshape(dec_b, N_NEW, 2, N_KV_HEADS, HEAD_DIM)],
                            axis=1)
    mconv_s = szqk[:, SEQ_PAD - (MLSTM_CONV - 1):]
    fconv_s = u_s.reshape(dec_b, SEQ_PAD, 2 * D_FF)[:, SEQ_PAD - (FFN_CONV - 1):]
    return (y_p, y_s, kv_p, kv_s, win_p, win_s, c_p, c_s, n_p, n_s, m_p, m_s, mconv_p, mconv_s, fconv_p, fconv_s)
```

```python
import functools
import math

import jax
import jax.numpy as jnp
from jax import lax
from jax.experimental import pallas as pl
from jax.experimental.pallas import tpu as pltpu

F32 = jnp.float32
BF16 = jnp.bfloat16
I32 = jnp.int32

D_MODEL = 1024
HEAD_DIM = 64
N_MLSTM_HEADS = 8
N_NSA_HEADS = 8
N_KV_HEADS = 2
GQ = N_NSA_HEADS // N_KV_HEADS
MLSTM_W = N_MLSTM_HEADS * HEAD_DIM
NSA_W = N_NSA_HEADS * HEAD_DIM
KV_W = N_KV_HEADS * HEAD_DIM
MLSTM_CHUNK = 64
MLSTM_CONV = 4
CMP_BLOCK = 32
SEL_BLOCK = 64
SEL_TOPK = 16
WINDOW = 512
NSA_QBLOCK = 64
N_BUCKETS = 32
MAX_DISTANCE = 128
D_FF = 2688
FFN_CONV = 3
DEPTH = 1
ALPHA = (2.0 * DEPTH) ** 0.25
LN_EPS = 1e-5
PAGE_SIZE = 128

LANES = 128
VMEM_LIMIT = 56 * 1024 * 1024
NEG_INF = float("-inf")

SEQ_PAD = 8
N_NEW = 4
N_QROWS = N_NSA_HEADS * SEQ_PAD


def _bucket_thresholds():
    max_exact = N_BUCKETS // 2
    thr = list(range(max_exact + 1))
    for k in range(1, N_BUCKETS - max_exact):
        thr.append(int(math.ceil(max_exact * (MAX_DISTANCE / max_exact) ** (k / (N_BUCKETS - max_exact)))))
    return tuple(thr)


BUCKET_THR = _bucket_thresholds()
FAR_DIST = BUCKET_THR[N_BUCKETS - 1]


def _dot(a, b, precision=None):
    return jnp.dot(a, b, preferred_element_type=F32, precision=precision)


def _dot_nt(a, b, precision=None):
    return lax.dot_general(a, b, (((1,), (1,)), ((), ())), preferred_element_type=F32, precision=precision)


def _dot_tn(a, b, precision=None):
    return lax.dot_general(a, b, (((0,), (0,)), ((), ())), preferred_element_type=F32, precision=precision)


def _layer_norm(x, w, b):
    mu = jnp.mean(x, -1, keepdims=True)
    xc = x - mu
    var = jnp.mean(xc * xc, -1, keepdims=True)
    return xc * lax.rsqrt(var + LN_EPS) * w + b


def _log_sigmoid(x):
    return jnp.minimum(x, 0.0) - jnp.log1p(jnp.exp(-jnp.abs(x)))


def _bias_from_dist(dist, rb_ref, head):
    acc = jnp.full(dist.shape, rb_ref[0, head], F32)
    for k in range(1, N_BUCKETS):
        acc = jnp.where(dist >= BUCKET_THR[k], rb_ref[k, head], acc)
    return acc


def _masked_softmax(s):
    mx = jnp.max(s, -1, keepdims=True)
    mx = jnp.where(mx > NEG_INF, mx, 0.0)
    e = jnp.exp(s - mx)
    den = jnp.sum(e, -1, keepdims=True)
    return e / jnp.maximum(den, 1e-30)


_C_QK, _C_V, _C_O, _C_NQ, _C_KV, _C_WIN, _C_G = 0, 1024, 1536, 2048, 2560, 3072, 3328
IN_PAD = 3456
G_IGATE, G_FGATE, G_NSA = 0, N_MLSTM_HEADS, 2 * N_MLSTM_HEADS


def _proj_in_body(*refs, kv_transposed):
    if kv_transposed:
        x_ref, w_ref, b_ref, wkvt_ref, bkv_ref, qk_ref, v_ref, o_ref, nq_ref, kv_ref, win_ref, g_ref, kvt_ref = refs
    else:
        x_ref, w_ref, b_ref, qk_ref, v_ref, o_ref, nq_ref, kv_ref, win_ref, g_ref = refs
    xb = x_ref[...].astype(BF16)
    z = _dot(xb, w_ref[...]) + b_ref[...]
    qk_ref[...] = z[:, _C_QK:_C_V]
    v_ref[...] = z[:, _C_V:_C_O]
    o_ref[...] = z[:, _C_O:_C_NQ]
    nq_ref[...] = z[:, _C_NQ:_C_KV]
    kv_ref[...] = z[:, _C_KV:_C_WIN]
    win_ref[...] = z[:, _C_WIN:_C_G]
    g_ref[...] = z[:, _C_G:IN_PAD]
    if kv_transposed:
        kvt_ref[0] = _dot_nt(wkvt_ref[...], xb) + bkv_ref[...]


def _proj_in(x2d, w_p, b_p, seq_len=None):
    rows = x2d.shape[0]
    tm = min(rows, 256)
    widths = (1024, 512, 512, 512, 512, 256, 128)
    const = lambda i: (0, 0)
    const_spec = lambda shape: pl.BlockSpec(shape, const, pipeline_mode=pl.Buffered(1))
    in_specs = [pl.BlockSpec((tm, D_MODEL), lambda i: (i, 0)), const_spec((D_MODEL, IN_PAD)), const_spec((1, IN_PAD))]
    args = [x2d, w_p, b_p]
    out_specs = [pl.BlockSpec((tm, w), lambda i: (i, 0)) for w in widths]
    out_shape = [jax.ShapeDtypeStruct((rows, w), F32) for w in widths]
    if seq_len is not None:
        tiles = seq_len // tm
        in_specs += [const_spec((4 * KV_W, D_MODEL)), const_spec((4 * KV_W, 1))]
        args += [jnp.transpose(w_p[:, _C_KV:_C_WIN]), b_p[0, _C_KV:_C_WIN].reshape(4 * KV_W, 1)]
        out_specs.append(pl.BlockSpec((1, 4 * KV_W, tm), lambda i: (i // tiles, 0, i % tiles)))
        out_shape.append(jax.ShapeDtypeStruct((rows // seq_len, 4 * KV_W, seq_len), F32))
    return pl.pallas_call(
        functools.partial(_proj_in_body, kv_transposed=seq_len is not None),
        grid=(rows // tm,),
        in_specs=in_specs,
        out_specs=out_specs,
        out_shape=out_shape,
        compiler_params=pltpu.CompilerParams(dimension_semantics=("arbitrary",),
                                             vmem_limit_bytes=VMEM_LIMIT),
        name="proj_in",
    )(*args)


N_PAIRS = N_MLSTM_HEADS // 2
GATE_LANE0 = N_MLSTM_HEADS
MLSTM_SEQS_PER_STEP = 4


def _dot_split(x, onehot_bf16, pieces):
    acc = None
    rem = x
    for n in range(pieces):
        part = rem.astype(BF16)
        d = _dot(part, onehot_bf16)
        acc = d if acc is None else acc + d
        if n + 1 < pieces:
            rem = rem - part.astype(F32)
    return acc


def _mlstm_body(*refs, tb, nbg, n_dummy, has_state):
    if has_state:
        (zqk_ref, zv_ref, zo_ref, zg_ref, c0_ref, n0_ref, m0_ref, cw_ref, cb_ref, nw_ref,
         mix_ref, c_out, n_out, m_out, ubuf, qk_sc, ct_sc, n_sc, m_sc) = refs
    else:
        (zqk_ref, zv_ref, zo_ref, zg_ref, cw_ref, cb_ref, nw_ref,
         mix_ref, c_out, n_out, m_out, ubuf, qk_sc, ct_sc, n_sc, m_sc) = refs
    L = MLSTM_CHUNK
    D = HEAD_DIM
    j = pl.program_id(1)
    hi = lax.Precision.HIGHEST
    head_lanes = slice(GATE_LANE0, GATE_LANE0 + N_MLSTM_HEADS)
    pairs = [(bi, p) for bi in range(nbg) for p in range(N_PAIRS)]

    @pl.when(j == 0)
    def _init():
        ubuf[:, 0:8, :] = jnp.zeros((nbg, 8, 2 * MLSTM_W), F32)
        ct_sc[...] = jnp.zeros(ct_sc.shape, F32)
        n_sc[...] = jnp.zeros(n_sc.shape, F32)
        m_sc[...] = jnp.zeros(m_sc.shape, F32)
        if has_state:
            for bi in range(nbg):
                for h in range(N_MLSTM_HEADS):
                    p, hh = divmod(h, 2)
                    hs = slice(hh * D, (hh + 1) * D)
                    ct_sc[bi, p, hs, hs] = c0_ref[bi, h]
                    n_sc[bi, p, :, hs] = n0_ref[bi, h:h + 1, :]
                m_sc[bi, :, head_lanes] = m0_ref[bi]

    for bi in range(nbg):
        u = zqk_ref[bi]
        ubuf[bi, 8:8 + tb, :] = u
        conv = cb_ref[...] + ubuf[bi, 5:5 + tb, :] * cw_ref[0:1, :]
        conv = conv + ubuf[bi, 6:6 + tb, :] * cw_ref[1:2, :]
        conv = conv + ubuf[bi, 7:7 + tb, :] * cw_ref[2:3, :]
        conv = conv + u * cw_ref[3:4, :]
        ubuf[bi, 5:8, :] = u[tb - 3:tb, :]
        qk_sc[bi] = conv * jax.nn.sigmoid(conv)

    tri = (lax.broadcasted_iota(I32, (L, L), 1) <= lax.broadcasted_iota(I32, (L, L), 0)).astype(F32)
    row_l = lax.broadcasted_iota(I32, (L, LANES), 0)
    lane = lax.broadcasted_iota(I32, (L, LANES), 1)
    low = lane < D
    key_tok = lane % D
    causal_pair = key_tok <= row_l
    eye_pair = key_tok == row_l
    spread = (lax.broadcasted_iota(I32, (LANES, MLSTM_W), 0) - GATE_LANE0
              == lax.broadcasted_iota(I32, (LANES, MLSTM_W), 1) // D).astype(BF16)
    same_head = (lax.broadcasted_iota(I32, (LANES, LANES), 0) // D
                 == lax.broadcasted_iota(I32, (LANES, LANES), 1) // D)
    ones_bd = same_head.astype(BF16)
    slab_rows = 3 * L + 8

    def chunk(ci, carry):
        r0 = pl.multiple_of(ci * L, L)
        rows = pl.ds(r0, L)
        gates = []
        for bi in range(nbg):
            g = zg_ref[bi, rows, :]
            if n_dummy:
                dummy = (row_l + ci * L) < n_dummy
                g = jnp.where(dummy, jnp.where(lane < G_FGATE, -1e30, 1e30), g)
            gates.append(g)
        b_all = _dot(tri, jnp.concatenate([_log_sigmoid(g) for g in gates], axis=1), hi)
        stacks = []
        for bi in range(nbg):
            b = b_all[:, bi * LANES:(bi + 1) * LANES]
            r = pltpu.roll(gates[bi], GATE_LANE0 - G_IGATE, 1) - b
            cm = r
            for sh in (1, 2, 4, 8, 16, 32):
                cm = jnp.where(row_l >= sh, jnp.maximum(cm, pltpu.roll(cm, sh, 0)), cm)
            m_st = m_sc[bi]
            mx = jnp.maximum(m_st, cm)
            mx_end = mx[L - 1:L, :]
            m_sc[bi] = b[L - 1:L, :] + mx_end
            stacks += [mx, r, b + mx, m_st, mx_end, jnp.zeros((6, LANES), F32)]
        x_all = _dot_split(jnp.concatenate(stacks, axis=0), spread, 2)

        work = []
        den_terms = []
        for bi, p in pairs:
            cs = slice(p * LANES, (p + 1) * LANES)
            x0 = bi * slab_rows
            mx_p, rs_p, mt_p = x_all[x0:x0 + L, cs], x_all[x0 + L:x0 + 2 * L, cs], x_all[x0 + 2 * L:x0 + 3 * L, cs]
            mst_p, mxe_p = x_all[x0 + 3 * L:x0 + 3 * L + 1, cs], x_all[x0 + 3 * L + 1:x0 + 3 * L + 2, cs]
            wi_p = jnp.exp(mst_p - mx_p)
            fl_p = jnp.exp(-mt_p)
            wk_p = jnp.exp(rs_p - mxe_p)
            sc_p = jnp.exp(mst_p - mxe_p)
            r_row = jnp.sum(jnp.where(eye_pair, rs_p, 0.0), axis=0, keepdims=True)
            w_intra = jnp.exp(jnp.where(causal_pair, r_row - mx_p, NEG_INF))
            q_p = qk_sc[bi, rows, cs]
            k_p = qk_sc[bi, rows, MLSTM_W + p * LANES:MLSTM_W + (p + 1) * LANES] * (D ** -0.5)
            v_p = zv_ref[bi, rows, cs]
            qb = q_p.astype(BF16)
            k_bd = jnp.concatenate([jnp.where(low, k_p, 0.0), jnp.where(low, 0.0, k_p)], axis=0)
            v_bd = jnp.concatenate([jnp.where(low, v_p, 0.0), jnp.where(low, 0.0, v_p)], axis=0).astype(BF16)
            c_t = ct_sc[bi, p]
            both = _dot_nt(qb, jnp.concatenate([k_bd, c_t], axis=0).astype(BF16))
            a = w_intra * both[:, 0:LANES]
            num = _dot(a.astype(BF16), v_bd) + both[:, LANES:2 * LANES] * wi_p
            den_terms.append(a + wi_p * (q_p * n_sc[bi, p]))
            work.append((num, fl_p, k_p * wk_p, v_p, sc_p, c_t))
        den_all = _dot_split(jnp.concatenate(den_terms, axis=0), ones_bd, 2)

        hhs = []
        for n, (num, fl_p, _, _, _, _) in enumerate(work):
            hhs.append(num / jnp.maximum(jnp.abs(den_all[n * L:(n + 1) * L, :]), fl_p))
        mu_all = _dot_split(jnp.concatenate(hhs, axis=0), ones_bd, 1) * (1.0 / D)
        hcs = [hh - mu_all[n * L:(n + 1) * L, :] for n, hh in enumerate(hhs)]
        var_all = _dot_split(jnp.concatenate([hc * hc for hc in hcs], axis=0), ones_bd, 1) * (1.0 / D)
        for n, (bi, p) in enumerate(pairs):
            cs = slice(p * LANES, (p + 1) * LANES)
            hn = hcs[n] * lax.rsqrt(var_all[n * L:(n + 1) * L, :] + LN_EPS)
            og = jax.nn.sigmoid(zo_ref[bi, rows, cs])
            mix_ref[bi, rows, cs] = og * (hn * nw_ref[:, cs])
            _, _, kw, v_p, sc_p, c_t = work[n]
            upd = _dot_tn(v_p.astype(BF16), kw.astype(BF16))
            ct_sc[bi, p] = sc_p * c_t + jnp.where(same_head, upd, 0.0)
            n_sc[bi, p] = sc_p * n_sc[bi, p] + jnp.sum(kw, axis=0, keepdims=True)
        return carry

    lax.fori_loop(0, tb // L, chunk, 0)

    @pl.when(j == pl.num_programs(1) - 1)
    def _final():
        for bi in range(nbg):
            for h in range(N_MLSTM_HEADS):
                p, hh = divmod(h, 2)
                hs = slice(hh * D, (hh + 1) * D)
                c_out[bi, h] = ct_sc[bi, p, hs, hs]
                n_out[bi, h:h + 1, :] = n_sc[bi, p, :, hs]
            m_out[bi] = m_sc[bi, :, head_lanes]


def _mlstm(zqk, zv, zo, zg, state, conv_w, conv_b, norm_w, n_dummy):
    bsz, t_len, _ = zqk.shape
    tb = min(t_len, 256)
    nbg = MLSTM_SEQS_PER_STEP
    assert bsz % nbg == 0
    has_state = state is not None
    row = lambda w: pl.BlockSpec((nbg, tb, w), lambda b, j: (b, j, 0))
    per_b = lambda shape: pl.BlockSpec((nbg,) + shape, lambda b, j: (b,) + (0,) * len(shape))
    const = lambda shape: pl.BlockSpec(shape, lambda b, j: (0,) * len(shape))
    in_specs = [row(2 * MLSTM_W), row(MLSTM_W), row(MLSTM_W), row(LANES)]
    args = [zqk, zv, zo, zg]
    if has_state:
        c0, n0, m0 = state
        in_specs += [per_b((N_MLSTM_HEADS, HEAD_DIM, HEAD_DIM)), per_b((N_MLSTM_HEADS, HEAD_DIM)),
                     per_b((1, N_MLSTM_HEADS))]
        args += [c0, n0, m0.reshape(bsz, 1, N_MLSTM_HEADS)]
    in_specs += [const((MLSTM_CONV, 2 * MLSTM_W)), const((1, 2 * MLSTM_W)), const((1, MLSTM_W))]
    args += [conv_w, conv_b.reshape(1, -1), norm_w.reshape(1, -1)]
    mix, c_new, n_new, m_new = pl.pallas_call(
        functools.partial(_mlstm_body, tb=tb, nbg=nbg, n_dummy=n_dummy, has_state=has_state),
        grid=(bsz // nbg, t_len // tb),
        in_specs=in_specs,
        out_specs=[row(MLSTM_W), per_b((N_MLSTM_HEADS, HEAD_DIM, HEAD_DIM)),
                   per_b((N_MLSTM_HEADS, HEAD_DIM)), per_b((1, N_MLSTM_HEADS))],
        out_shape=[jax.ShapeDtypeStruct((bsz, t_len, MLSTM_W), F32),
                   jax.ShapeDtypeStruct((bsz, N_MLSTM_HEADS, HEAD_DIM, HEAD_DIM), F32),
                   jax.ShapeDtypeStruct((bsz, N_MLSTM_HEADS, HEAD_DIM), F32),
                   jax.ShapeDtypeStruct((bsz, 1, N_MLSTM_HEADS), F32)],
        scratch_shapes=[pltpu.VMEM((nbg, 8 + tb, 2 * MLSTM_W), F32),
                        pltpu.VMEM((nbg, tb, 2 * MLSTM_W), F32),
                        pltpu.VMEM((nbg, N_PAIRS, LANES, LANES), F32),
                        pltpu.VMEM((nbg, N_PAIRS, 1, LANES), F32),
                        pltpu.VMEM((nbg, 1, LANES), F32)],
        compiler_params=pltpu.CompilerParams(dimension_semantics=("arbitrary", "arbitrary"),
                                             vmem_limit_bytes=VMEM_LIMIT),
        name="mlstm",
    )(*args)
    return mix, c_new, n_new, m_new.reshape(bsz, N_MLSTM_HEADS)


def _block_diag_queries(x, n_rows, heads=tuple(range(N_NSA_HEADS)), scale=HEAD_DIM ** -0.5):
    lane = lax.broadcasted_iota(I32, (n_rows, LANES), 1)
    low = lane < HEAD_DIM
    pieces = []
    for h in heads:
        pair = x[:, (h // 2) * LANES:(h // 2 + 1) * LANES]
        src_low = h % 2 == 0
        dst_low = h // GQ == 0
        if src_low != dst_low:
            pair = pltpu.roll(pair, HEAD_DIM, 1)
        pieces.append(jnp.where(low if dst_low else ~low, pair, 0.0))
    return (jnp.concatenate(pieces, axis=0) * scale).astype(BF16)


def _group_lanes(o, g):
    return o[:, g * HEAD_DIM:(g + 1) * HEAD_DIM]


def _gated_mix(gates, o_c, o_s, o_w, h, n_rows):
    g = h // GQ
    rs = slice(h * n_rows, (h + 1) * n_rows)
    col = lambda branch: gates[:, G_NSA + branch * N_NSA_HEADS + h:G_NSA + branch * N_NSA_HEADS + h + 1]
    return (col(0) * _group_lanes(o_c[rs], g) + col(1) * _group_lanes(o_s[rs], g)
            + col(2) * _group_lanes(o_w[rs], g))


SEL_TILE = 256
SEL_TILE_BLOCKS = SEL_TILE // SEL_BLOCK
FAR_GROUP = 2
WIN_BAND = WINDOW + NSA_QBLOCK
N_COLS = N_NSA_HEADS * NSA_QBLOCK
PROMPT_HEADS = tuple(GQ * (jj % N_KV_HEADS) + jj // N_KV_HEADS for jj in range(N_NSA_HEADS))
SEL_REL0 = 2 * SEL_TILE - SEL_BLOCK
SEL_BIAS_ROWS = SEL_REL0 + SEL_TILE


LOG2E = math.log2(math.e)


def _softmax2_rows(s):
    mx = jnp.max(s, 0, keepdims=True)
    mx = jnp.where(mx > NEG_INF, mx, 0.0)
    e = jnp.exp2(s - mx)
    return e / jnp.maximum(jnp.sum(e, 0, keepdims=True), 1e-30)


def _nsa_prompt_body(rb_ref, nq_ref, nqn_ref, kv_ref, win_ref, zg_ref, pw_ref, out_ref,
                     ks_sc, vst_sc, wk_sc, wv_sc, kc_sc, pv_sc, vct_sc, bs_sc, bw_sc, fc_sc,
                     m_sc, l_sc, acc_sc, ch_sc, oc_sc, fs_sc, ft_sc, *, t_len):
    QB = NSA_QBLOCK
    n_cmp = t_len // CMP_BLOCK
    n_sel = t_len // SEL_BLOCK
    i = pl.program_id(1)
    hi = lax.Precision.HIGHEST

    def cmp_scores(x_q):
        return _dot_nt(kc_sc[...], _block_diag_queries(x_q, QB, PROMPT_HEADS, HEAD_DIM ** -0.5 * LOG2E))

    def cmp_and_select(raw, qi, out_buf):
        s_c = raw + pltpu.roll(fc_sc[...], 2 * qi, 0)
        c_i = lax.broadcasted_iota(I32, (n_cmp, N_COLS), 0)
        q_i = lax.broadcasted_iota(I32, (n_cmp, N_COLS), 1) % QB
        p_c = _softmax2_rows(jnp.where(c_i * CMP_BLOCK + (CMP_BLOCK - 1) <= qi * QB + q_i, s_c, NEG_INF))
        oc_sc[out_buf] = _dot(vct_sc[...], p_c.astype(BF16))
        p_sum = p_c[:, 0:LANES]
        for j in range(1, GQ):
            p_sum = p_sum + p_c[:, j * LANES:(j + 1) * LANES]
        ratio = SEL_BLOCK // CMP_BLOCK
        pair = (lax.broadcasted_iota(I32, (n_sel, n_cmp), 1) // ratio
                == lax.broadcasted_iota(I32, (n_sel, n_cmp), 0)).astype(F32)
        imp = _dot(pair, p_sum, hi)
        blk = lax.broadcasted_iota(I32, (n_sel, LANES), 0)
        score = jnp.where(blk == qi, jnp.inf, jnp.where(blk < qi, imp, NEG_INF))

        def pick(_, carry):
            work, chosen = carry
            best = jnp.max(work, 0, keepdims=True)
            first = jnp.min(jnp.where(work == best, blk, n_sel), 0, keepdims=True)
            hit = blk == first
            return jnp.where(hit, NEG_INF, work), jnp.where(hit, 1.0, chosen)

        _, chosen = lax.fori_loop(0, SEL_TOPK, pick, (score, jnp.zeros((n_sel, LANES), F32)), unroll=True)
        ch_sc[out_buf, 0:8, :] = jnp.full((8, LANES), NEG_INF, F32)
        ch_sc[out_buf, 8:8 + n_sel, :] = jnp.where((score > NEG_INF) & (chosen > 0.5), 0.0, NEG_INF)

    @pl.when(i == 0)
    def _init():
        ks_sc[0:SEL_TILE, :] = jnp.zeros((SEL_TILE, LANES), BF16)
        vst_sc[0] = jnp.zeros((LANES, SEL_TILE), BF16)
        wk_sc[0:WINDOW, :] = jnp.zeros((WINDOW, LANES), BF16)
        wv_sc[0:WINDOW, :] = jnp.zeros((WINDOW, LANES), BF16)
        rows_per = 2 * SEL_TILE
        blocks_per = rows_per // CMP_BLOCK

        def fill(c, carry):
            r0 = pl.multiple_of(c * rows_per, rows_per)
            kv = kv_ref[0, pl.ds(r0, rows_per), :]
            ks_sc[pl.ds(SEL_TILE + r0, rows_per), :] = kv[:, 2 * LANES:3 * LANES].astype(BF16)
            for half in range(2):
                v_t = kv[half * SEL_TILE:(half + 1) * SEL_TILE, 3 * LANES:4 * LANES]
                vst_sc[2 * c + half + 1] = jnp.transpose(v_t).astype(BF16)
            w = win_ref[0, pl.ds(r0, rows_per), :]
            wk_sc[pl.ds(WINDOW + r0, rows_per), :] = w[:, 0:LANES].astype(BF16)
            wv_sc[pl.ds(WINDOW + r0, rows_per), :] = w[:, LANES:2 * LANES].astype(BF16)
            pooled = jnp.sum(kv[:, 0:2 * LANES].reshape(blocks_per, CMP_BLOCK, 2 * LANES) * pw_ref[...][None],
                             axis=1)
            c0 = pl.multiple_of(c * blocks_per, blocks_per)
            kc_sc[pl.ds(c0, blocks_per), :] = pooled[:, 0:LANES].astype(BF16)
            pv_sc[pl.ds(c0, blocks_per), :] = pooled[:, LANES:2 * LANES]
            return carry

        lax.fori_loop(0, t_len // rows_per, fill, 0)
        vct_sc[...] = jnp.transpose(pv_sc[...]).astype(BF16)

        for jj, h in enumerate(PROMPT_HEADS):
            cs = slice(jj * QB, (jj + 1) * QB)
            u_i = lax.broadcasted_iota(I32, (SEL_BIAS_ROWS, QB), 0)
            q_i = lax.broadcasted_iota(I32, (SEL_BIAS_ROWS, QB), 1)
            dist = q_i - (u_i - SEL_REL0)
            near = (_bias_from_dist(dist, rb_ref, h) - rb_ref[N_BUCKETS - 1, h]) * LOG2E
            bs_sc[:, cs] = jnp.where(dist >= 0, near, NEG_INF)
            s_i = lax.broadcasted_iota(I32, (WIN_BAND, QB), 0)
            q_i = lax.broadcasted_iota(I32, (WIN_BAND, QB), 1)
            dist = q_i + WINDOW - s_i
            bw_sc[:, cs] = jnp.where((dist >= 0) & (dist <= WINDOW), _bias_from_dist(dist, rb_ref, h) * LOG2E,
                                     NEG_INF)
            c_i = lax.broadcasted_iota(I32, (n_cmp, QB), 0)
            q_i = lax.broadcasted_iota(I32, (n_cmp, QB), 1)
            rel = jnp.where(c_i < n_cmp // 2, c_i, c_i - n_cmp)
            dist = q_i - CMP_BLOCK * rel - (CMP_BLOCK - 1)
            dist = jnp.where((c_i >= 2) & (c_i < n_cmp // 2), FAR_DIST, dist)
            fc_sc[:, cs] = _bias_from_dist(dist, rb_ref, h) * LOG2E

        cmp_and_select(cmp_scores(nq_ref[0]), 0, 0)

    buf = i % 2
    o_c = oc_sc[buf]
    qbd = _block_diag_queries(nq_ref[0], QB, PROMPT_HEADS, HEAD_DIM ** -0.5 * LOG2E)
    last = i // SEL_TILE_BLOCKS

    def tile_raw(slot):
        return _dot_nt(ks_sc[pl.ds(pl.multiple_of(slot * SEL_TILE, SEL_TILE), SEL_TILE), :], qbd)

    raw_next = cmp_scores(nqn_ref[0])
    w0 = pl.multiple_of(i * QB, QB)
    raw_w = _dot_nt(wk_sc[pl.ds(w0, WIN_BAND), :], qbd)
    raw_near = [tile_raw(last + 1), tile_raw(last)]

    def tile_scores(slot, bias, s=None):
        if s is None:
            s = tile_raw(slot)
        if bias is not None:
            s = s + bias
        parts = []
        for r in range(SEL_TILE_BLOCKS):
            row = ch_sc[buf, pl.ds(8 - SEL_TILE_BLOCKS + slot * SEL_TILE_BLOCKS + r, 1), :]
            parts.append(s[r * SEL_BLOCK:(r + 1) * SEL_BLOCK, :] + jnp.concatenate([row] * GQ, axis=1))
        return jnp.concatenate(parts, axis=0)

    def group_max(scores):
        return functools.reduce(jnp.maximum, [jnp.max(s, 0, keepdims=True) for s in scores])

    def sel_group(slots, scores, t_max, first):
        m_new = t_max if first else jnp.maximum(m_sc[...], t_max)
        exps = [jnp.exp2(s - m_new) for s in scores]
        l_new = functools.reduce(jnp.add, [jnp.sum(e, 0, keepdims=True) for e in exps])
        acc_new = functools.reduce(jnp.add, [_dot(vst_sc[slot], e.astype(BF16)) for slot, e in zip(slots, exps)])
        if first:
            l_sc[...] = l_new
            acc_sc[...] = acc_new
        else:
            scale = jnp.exp2(m_sc[...] - m_new)
            l_sc[...] = scale * l_sc[...] + l_new
            acc_sc[...] = scale * acc_sc[...] + acc_new
        m_sc[...] = m_new

    def far_slots(p):
        return [jnp.where(FAR_GROUP * p + 1 + n < last, FAR_GROUP * p + 1 + n, 0) for n in range(FAR_GROUP)]

    def far_scores(p):
        scores = [tile_scores(slot, None) for slot in far_slots(p)]
        for n, s in enumerate(scores):
            fs_sc[p % 2, n] = s
        ft_sc[p % 2] = group_max(scores)

    far_scores(0)
    cmp_and_select(raw_next, i + 1, 1 - buf)

    in_seq = lax.broadcasted_iota(I32, (WIN_BAND, N_COLS), 0) >= WINDOW - i * QB
    p_w = _softmax2_rows(jnp.where(in_seq, raw_w + bw_sc[...], NEG_INF))
    o_w = _dot_tn(wv_sc[pl.ds(w0, WIN_BAND), :], p_w.astype(BF16))

    u0 = pl.multiple_of(SEL_REL0 - SEL_TILE - SEL_BLOCK * (i % SEL_TILE_BLOCKS), SEL_BLOCK)
    near = [tile_scores(last + 1, bs_sc[pl.ds(u0 + SEL_TILE, SEL_TILE), :], raw_near[0]),
            tile_scores(last, bs_sc[pl.ds(u0, SEL_TILE), :], raw_near[1])]
    sel_group((last + 1, last), near, group_max(near), True)

    def far_group(p, carry):
        sel_group(far_slots(p), [fs_sc[p % 2, n] for n in range(FAR_GROUP)], ft_sc[p % 2], False)
        far_scores(p + 1)
        return carry

    lax.fori_loop(0, (last - 1 + FAR_GROUP - 1) // FAR_GROUP, far_group, 0)
    o_s = acc_sc[...] / jnp.maximum(l_sc[...], 1e-30)

    eye = (lax.broadcasted_iota(I32, (3 * N_NSA_HEADS + 8, LANES), 0) + G_NSA
           == lax.broadcasted_iota(I32, (3 * N_NSA_HEADS + 8, LANES), 1)).astype(F32)
    gates = jax.nn.sigmoid(_dot_nt(eye, zg_ref[0], hi))
    pieces = []
    for h in range(N_NSA_HEADS):
        g = h // GQ
        jj = PROMPT_HEADS.index(h)
        part = lambda o: o[g * HEAD_DIM:(g + 1) * HEAD_DIM, jj * QB:(jj + 1) * QB]
        pieces.append(gates[h:h + 1, :] * part(o_c)
                      + gates[N_NSA_HEADS + h:N_NSA_HEADS + h + 1, :] * part(o_s)
                      + gates[2 * N_NSA_HEADS + h:2 * N_NSA_HEADS + h + 1, :] * part(o_w))
    mix_t = jnp.concatenate(pieces, axis=0)
    mix_t = jnp.concatenate([mix_t, jnp.zeros((NSA_W, LANES - QB), F32)], axis=1)
    out_ref[0] = jnp.transpose(mix_t)[0:QB, :]


def _nsa_prompt(nq, kv, win, zg, pool_w2, rel_bias):
    bsz, t_len, _ = nq.shape
    assert t_len // CMP_BLOCK == LANES and t_len % (2 * SEL_TILE) == 0
    n_cmp = t_len // CMP_BLOCK
    per_b = lambda w: pl.BlockSpec((1, t_len, w), lambda b, i: (b, 0, 0))
    blk = lambda w: pl.BlockSpec((1, NSA_QBLOCK, w), lambda b, i: (b, i, 0))
    n_qb = t_len // NSA_QBLOCK
    next_blk = pl.BlockSpec((1, NSA_QBLOCK, NSA_W), lambda b, i: (b, jnp.minimum(i + 1, n_qb - 1), 0))
    return pl.pallas_call(
        functools.partial(_nsa_prompt_body, t_len=t_len),
        grid=(bsz, n_qb),
        in_specs=[pl.BlockSpec(memory_space=pltpu.SMEM),
                  blk(NSA_W), next_blk, per_b(4 * KV_W), per_b(2 * KV_W), blk(LANES),
                  pl.BlockSpec((CMP_BLOCK, 2 * LANES), lambda b, i: (0, 0))],
        out_specs=blk(NSA_W),
        out_shape=jax.ShapeDtypeStruct((bsz, t_len, NSA_W), F32),
        scratch_shapes=[pltpu.VMEM((SEL_TILE + t_len, LANES), BF16),
                        pltpu.VMEM((1 + t_len // SEL_TILE, LANES, SEL_TILE), BF16),
                        pltpu.VMEM((WINDOW + t_len, LANES), BF16),
                        pltpu.VMEM((WINDOW + t_len, LANES), BF16),
                        pltpu.VMEM((n_cmp, LANES), BF16),
                        pltpu.VMEM((n_cmp, LANES), F32),
                        pltpu.VMEM((LANES, n_cmp), BF16),
                        pltpu.VMEM((SEL_BIAS_ROWS, N_COLS), F32),
                        pltpu.VMEM((WIN_BAND, N_COLS), F32),
                        pltpu.VMEM((n_cmp, N_COLS), F32),
                        pltpu.VMEM((1, N_COLS), F32),
                        pltpu.VMEM((1, N_COLS), F32),
                        pltpu.VMEM((LANES, N_COLS), F32),
                        pltpu.VMEM((2, 8 + t_len // SEL_BLOCK, LANES), F32),
                        pltpu.VMEM((2, LANES, N_COLS), F32),
                        pltpu.VMEM((2, FAR_GROUP, SEL_TILE, N_COLS), F32),
                        pltpu.VMEM((2, 1, N_COLS), F32)],
        compiler_params=pltpu.CompilerParams(dimension_semantics=("arbitrary", "arbitrary"),
                                             vmem_limit_bytes=VMEM_LIMIT),
        name="nsa_prompt",
    )(rel_bias, nq, nq, kv, win, zg, pool_w2)


CMP_PAGES = 32
CMP_SLOTS = 3
N_PICK = SEL_TOPK - 1
N_OWNERS = N_KV_HEADS * N_NEW
N_FETCH = N_OWNERS * N_PICK
OWNER_KEYS = N_PICK * PAGE_SIZE


def _sample_row_ids():
    r = lax.broadcasted_iota(I32, (N_QROWS, 1), 0)
    return r % SEQ_PAD - (SEQ_PAD - N_NEW), r // SEQ_PAD


def _sample_bias(dist, rb_ref):
    _, head = _sample_row_ids()
    col = lambda k: sum(jnp.where(head == h, rb_ref[k, h], 0.0) for h in range(N_NSA_HEADS))
    acc = jnp.broadcast_to(col(0), dist.shape)
    for k in range(1, N_BUCKETS):
        acc = jnp.where(dist >= BUCKET_THR[k], col(k), acc)
    return acc


def _softmax_two(s1, s2):
    mx = jnp.maximum(jnp.max(s1, -1, keepdims=True), jnp.max(s2, -1, keepdims=True))
    mx = jnp.where(mx > NEG_INF, mx, 0.0)
    e1 = jnp.exp(s1 - mx)
    e2 = jnp.exp(s2 - mx)
    den = jnp.maximum(jnp.sum(e1, -1, keepdims=True) + jnp.sum(e2, -1, keepdims=True), 1e-30)
    return e1 / den, e2 / den


def _nsa_cmp_body(pt_ref, rb_ref, nq_ref, pwt_ref, cache_ref, oc_ref, idx_ref,
                  buf, sem, seg_sc, kct_sc, vct_sc, *, n_pages, past_len):
    b = pl.program_id(0)
    nb = pl.num_programs(0)
    n_chunks = n_pages // CMP_PAGES
    chunk_rows = CMP_PAGES * PAGE_SIZE
    chunk_blocks = chunk_rows // CMP_BLOCK
    n_cmp = past_len // CMP_BLOCK
    n_sel_past = past_len // SEL_BLOCK
    n_prob = N_KV_HEADS * SEQ_PAD
    total = nb * n_chunks
    hi = lax.Precision.HIGHEST

    def page_copy(gi, p):
        page = pt_ref[(gi // n_chunks) * n_pages + (gi % n_chunks) * CMP_PAGES + p]
        slot = gi % CMP_SLOTS
        return pltpu.make_async_copy(cache_ref.at[page, pl.ds(0, 2 * LANES), :], buf.at[slot, p], sem.at[slot])

    def start_chunk(gi):
        for p in range(CMP_PAGES):
            page_copy(gi, p).start()

    def wait_chunk(gi):
        for p in range(CMP_PAGES):
            page_copy(gi, p).wait()

    @pl.when(b == 0)
    def _prologue():
        seg_sc[...] = (lax.broadcasted_iota(I32, (chunk_rows, chunk_blocks), 0) // CMP_BLOCK
                       == lax.broadcasted_iota(I32, (chunk_rows, chunk_blocks), 1)).astype(BF16)
        for gi in range(min(CMP_SLOTS - 1, total)):
            start_chunk(gi)

    def chunk_step(c, carry):
        gi = b * n_chunks + c
        ahead = gi + (CMP_SLOTS - 1)

        @pl.when(ahead < total)
        def _prefetch():
            start_chunk(ahead)

        wait_chunk(gi)
        slot = gi % CMP_SLOTS
        y = jnp.concatenate([(buf[slot, p] * pwt_ref[...]).astype(BF16) for p in range(CMP_PAGES)], axis=1)
        pooled = _dot(y, seg_sc[...])
        kct_sc[c] = pooled[0:LANES].astype(BF16)
        vct_sc[c] = pooled[LANES:2 * LANES].astype(BF16)
        return carry

    lax.fori_loop(0, n_chunks, chunk_step, 0)

    qbd = _block_diag_queries(nq_ref[0], SEQ_PAD)
    tok, _ = _sample_row_ids()
    s_c = jnp.concatenate([_dot(qbd, kct_sc[c]) for c in range(n_chunks)], axis=1)
    near0 = n_cmp - LANES
    assert past_len - (SEQ_PAD - N_NEW) - (CMP_BLOCK * (near0 - 1) + CMP_BLOCK - 1) >= FAR_DIST
    dist = past_len + tok - ((lax.broadcasted_iota(I32, (N_QROWS, LANES), 1) + near0) * CMP_BLOCK + CMP_BLOCK - 1)
    near = jnp.where(dist >= 0, s_c[:, near0:] + _sample_bias(dist, rb_ref), NEG_INF)
    far = s_c[:, :near0] + _sample_bias(jnp.full((N_QROWS, 1), FAR_DIST, I32), rb_ref)
    p_c = _masked_softmax(jnp.concatenate([far, near], axis=1))
    o_c = _dot_nt(p_c[:, 0:chunk_blocks].astype(BF16), vct_sc[0])
    for c in range(1, n_chunks):
        o_c = o_c + _dot_nt(p_c[:, c * chunk_blocks:(c + 1) * chunk_blocks].astype(BF16), vct_sc[c])
    oc_ref[0] = o_c

    p_sum = jnp.sum(p_c.reshape(N_KV_HEADS, GQ, SEQ_PAD, n_cmp), axis=1).reshape(n_prob, n_cmp)
    p_sum = jnp.concatenate([p_sum, jnp.zeros((LANES - n_prob, n_cmp), F32)], axis=0)
    ratio = SEL_BLOCK // CMP_BLOCK
    pair = (lax.broadcasted_iota(I32, (n_sel_past, n_cmp), 1) // ratio
            == lax.broadcasted_iota(I32, (n_sel_past, n_cmp), 0)).astype(F32)
    imp = _dot_nt(pair, p_sum, hi)
    blk = lax.broadcasted_iota(I32, (n_sel_past, LANES), 0)
    pick_i = lax.broadcasted_iota(I32, (SEL_TOPK, LANES), 0)

    def pick(k, carry):
        work, ids = carry
        best = jnp.max(work, 0, keepdims=True)
        first = jnp.min(jnp.where(work == best, blk, n_sel_past), 0, keepdims=True)
        return jnp.where(blk == first, NEG_INF, work), jnp.where(pick_i == k, first, ids)

    _, ids = lax.fori_loop(0, N_PICK, pick, (imp, jnp.zeros((SEL_TOPK, LANES), I32)), unroll=True)
    idx_ref[0] = ids


def _nsa_cmp(page_table, rel_bias, nq_s, pool_wt, cache3):
    dec_b, n_pages = page_table.shape
    past_len = n_pages * PAGE_SIZE
    assert n_pages % CMP_PAGES == 0 and past_len // SEL_BLOCK >= N_PICK
    n_chunks = n_pages // CMP_PAGES
    chunk_rows = CMP_PAGES * PAGE_SIZE
    chunk_blocks = chunk_rows // CMP_BLOCK
    assert chunk_blocks == LANES
    grid_spec = pltpu.PrefetchScalarGridSpec(
        num_scalar_prefetch=1,
        grid=(dec_b,),
        in_specs=[pl.BlockSpec(memory_space=pltpu.SMEM),
                  pl.BlockSpec((1, SEQ_PAD, NSA_W), lambda b, pt: (b, 0, 0)),
                  pl.BlockSpec((2 * LANES, PAGE_SIZE), lambda b, pt: (0, 0)),
                  pl.BlockSpec(memory_space=pl.ANY)],
        out_specs=[pl.BlockSpec((1, N_QROWS, LANES), lambda b, pt: (b, 0, 0)),
                   pl.BlockSpec((1, SEL_TOPK, LANES), lambda b, pt: (b, 0, 0))],
        scratch_shapes=[pltpu.VMEM((CMP_SLOTS, CMP_PAGES, 2 * LANES, PAGE_SIZE), F32),
                        pltpu.SemaphoreType.DMA((CMP_SLOTS,)),
                        pltpu.VMEM((chunk_rows, chunk_blocks), BF16),
                        pltpu.VMEM((n_chunks, LANES, chunk_blocks), BF16),
                        pltpu.VMEM((n_chunks, LANES, chunk_blocks), BF16)],
    )
    return pl.pallas_call(
        functools.partial(_nsa_cmp_body, n_pages=n_pages, past_len=past_len),
        grid_spec=grid_spec,
        out_shape=[jax.ShapeDtypeStruct((dec_b, N_QROWS, LANES), F32),
                   jax.ShapeDtypeStruct((dec_b, SEL_TOPK, LANES), I32)],
        compiler_params=pltpu.CompilerParams(dimension_semantics=("arbitrary",),
                                             vmem_limit_bytes=VMEM_LIMIT),
        name="nsa_sample_cmp",
    )(page_table.reshape(-1), rel_bias, nq_s, pool_wt, cache3)


def _nsa_sel_body(pt_ref, ids_ref, rb_ref, nq_ref, kvn_ref, winc_ref, winn_ref, zg_ref, oc_ref, idv_ref,
                  cache_ref, out_ref, buf, sem, *, n_pages, past_len):
    b = pl.program_id(0)
    nb = pl.num_programs(0)
    hi = lax.Precision.HIGHEST

    def block_copy(bb, n, slot):
        blk = ids_ref[bb * N_FETCH + n]
        page = pt_ref[bb * n_pages + blk // 2]
        return pltpu.make_async_copy(cache_ref.at[page, pl.ds(2 * LANES, 2 * LANES), :], buf.at[slot, n],
                                     sem.at[slot])

    def start_all(bb, slot):
        def go(n, carry):
            block_copy(bb, n, slot).start()
            return carry
        lax.fori_loop(0, N_FETCH, go, 0)

    def wait_all(bb, slot):
        def go(n, carry):
            block_copy(bb, n, slot).wait()
            return carry
        lax.fori_loop(0, N_FETCH, go, 0)

    slot = b % 2

    @pl.when(b == 0)
    def _prologue():
        start_all(0, 0)

    @pl.when(b + 1 < nb)
    def _prefetch():
        start_all(b + 1, 1 - slot)

    qbd = _block_diag_queries(nq_ref[0], SEQ_PAD)
    tok, head = _sample_row_ids()
    owner = (head // GQ) * N_NEW + tok

    wc = winc_ref[0]
    wn = winn_ref[0]
    r_i = lax.broadcasted_iota(I32, (N_QROWS, WINDOW), 1)
    dist1 = WINDOW + tok - r_i
    s1 = _dot(qbd, wc[0:LANES, :].astype(BF16)) + _sample_bias(dist1, rb_ref)
    s1 = jnp.where((dist1 >= 0) & (dist1 <= WINDOW), s1, NEG_INF)
    n_i = lax.broadcasted_iota(I32, (N_QROWS, SEQ_PAD), 1) - (SEQ_PAD - N_NEW)
    dist2 = tok - n_i
    new_ok = (n_i >= 0) & (dist2 >= 0)
    bias2 = _sample_bias(dist2, rb_ref)
    s2 = jnp.where(new_ok, _dot_nt(qbd, wn[:, 0:LANES].astype(BF16)) + bias2, NEG_INF)
    p1, p2 = _softmax_two(s1, s2)
    o_w = (_dot_nt(p1.astype(BF16), wc[LANES:2 * LANES, :].astype(BF16))
           + _dot(p2.astype(BF16), wn[:, LANES:2 * LANES].astype(BF16)))

    wait_all(b, slot)

    def owner_tiles(o, r0):
        return jnp.concatenate([buf[slot, o * N_PICK + k, r0:r0 + LANES, :] for k in range(N_PICK)],
                               axis=1).astype(BF16)

    s_f = jnp.full((N_QROWS, OWNER_KEYS), NEG_INF, F32)
    for o in range(N_OWNERS):
        s_f = jnp.where(owner == o, _dot(qbd, owner_tiles(o, 0)), s_f)
    spread = (lax.broadcasted_iota(I32, (SEL_TOPK, OWNER_KEYS), 0)
              == lax.broadcasted_iota(I32, (SEL_TOPK, OWNER_KEYS), 1) // PAGE_SIZE).astype(F32)
    blk_of_key = _dot_tn(idv_ref[0].astype(F32), spread, hi)
    blk_rows = jnp.concatenate([blk_of_key[(h // GQ) * SEQ_PAD:(h // GQ + 1) * SEQ_PAD]
                                for h in range(N_NSA_HEADS)], axis=0).astype(I32)
    blocks_per_page = PAGE_SIZE // SEL_BLOCK
    in_page = lax.broadcasted_iota(I32, (N_QROWS, OWNER_KEYS), 1) % PAGE_SIZE
    in_block = in_page // SEL_BLOCK == blk_rows % blocks_per_page
    dist_s = past_len + tok - ((blk_rows // blocks_per_page) * PAGE_SIZE + in_page)
    s_f = jnp.where((tok >= 0) & in_block & (dist_s >= 0), s_f + _sample_bias(dist_s, rb_ref), NEG_INF)
    kvn = kvn_ref[0]
    s_n = jnp.where(new_ok, _dot_nt(qbd, kvn[:, 2 * LANES:3 * LANES].astype(BF16)) + bias2, NEG_INF)
    p_f, p_n = _softmax_two(s_f, s_n)
    o_s = _dot(p_n.astype(BF16), kvn[:, 3 * LANES:4 * LANES].astype(BF16))
    for o in range(N_OWNERS):
        o_s = o_s + _dot_nt(jnp.where(owner == o, p_f, 0.0).astype(BF16), owner_tiles(o, LANES))

    o_c = oc_ref[0]
    gates = jax.nn.sigmoid(zg_ref[0])
    is_new = lax.broadcasted_iota(I32, (SEQ_PAD, HEAD_DIM), 0) >= SEQ_PAD - N_NEW
    for h in range(N_NSA_HEADS):
        mix = _gated_mix(gates, o_c, o_s, o_w, h, SEQ_PAD)
        out_ref[0, :, h * HEAD_DIM:(h + 1) * HEAD_DIM] = jnp.where(is_new, mix, 0.0)


def _nsa_sel(page_table, ids, ids_rows, rel_bias, nq_s, kv_s, win_cache, win_s, zg_s, o_c, cache3):
    dec_b, n_pages = page_table.shape
    past_len = n_pages * PAGE_SIZE
    seq = lambda w: pl.BlockSpec((1, SEQ_PAD, w), lambda b, pt, ix: (b, 0, 0))
    grid_spec = pltpu.PrefetchScalarGridSpec(
        num_scalar_prefetch=2,
        grid=(dec_b,),
        in_specs=[pl.BlockSpec(memory_space=pltpu.SMEM),
                  seq(NSA_W), seq(4 * KV_W),
                  pl.BlockSpec((1, 2 * KV_W, WINDOW), lambda b, pt, ix: (b, 0, 0)),
                  seq(2 * KV_W), seq(LANES),
                  pl.BlockSpec((1, N_QROWS, LANES), lambda b, pt, ix: (b, 0, 0)),
                  pl.BlockSpec((1, SEL_TOPK, LANES), lambda b, pt, ix: (b, 0, 0)),
                  pl.BlockSpec(memory_space=pl.ANY)],
        out_specs=seq(NSA_W),
        scratch_shapes=[pltpu.VMEM((2, N_FETCH, 2 * LANES, PAGE_SIZE), F32),
                        pltpu.SemaphoreType.DMA((2,))],
    )
    return pl.pallas_call(
        functools.partial(_nsa_sel_body, n_pages=n_pages, past_len=past_len),
        grid_spec=grid_spec,
        out_shape=jax.ShapeDtypeStruct((dec_b, SEQ_PAD, NSA_W), F32),
        compiler_params=pltpu.CompilerParams(dimension_semantics=("arbitrary",),
                                             vmem_limit_bytes=VMEM_LIMIT),
        name="nsa_sample_sel",
    )(page_table.reshape(-1), ids.reshape(-1), rel_bias, nq_s, kv_s, win_cache, win_s, zg_s, o_c, ids_rows, cache3)


FF_CHUNK = 896


def _ffn_body(*refs, tm, has_override):
    if has_override:
        (x_ref, mm_ref, mn_ref, ov_ref, woa_ref, wob_ref, l1w_ref, l1b_ref, wup_ref, bup_ref, cw_ref, cb_ref,
         wdn_ref, bdn_ref, l2w_ref, l2b_ref, y_ref, tail_ref, carry_sc, ubuf) = refs
    else:
        (x_ref, mm_ref, mn_ref, woa_ref, wob_ref, l1w_ref, l1b_ref, wup_ref, bup_ref, cw_ref, cb_ref,
         wdn_ref, bdn_ref, l2w_ref, l2b_ref, y_ref, tail_ref, carry_sc, ubuf) = refs
    j = pl.program_id(1)

    @pl.when(j == 0)
    def _init():
        carry_sc[...] = jnp.zeros(carry_sc.shape, F32)

    x = x_ref[0]
    h = ALPHA * x + (_dot(mm_ref[0].astype(BF16), woa_ref[...]) + _dot(mn_ref[0].astype(BF16), wob_ref[...]))
    x1 = _layer_norm(h, l1w_ref[...], l1b_ref[...])
    x1b = x1.astype(BF16)
    if has_override:
        state_row = lax.broadcasted_iota(I32, (tm, FF_CHUNK), 0) % SEQ_PAD < SEQ_PAD - N_NEW

    def conv_half(c0):
        cols = slice(c0, c0 + FF_CHUNK)
        u = _dot(x1b, wup_ref[:, cols]) + bup_ref[:, cols]
        if has_override:
            u = jnp.where(state_row, ov_ref[0, :, cols], u)
        ubuf[0:8, :] = carry_sc[:, cols]
        ubuf[8:8 + tm, :] = u
        out = cb_ref[:, cols] + ubuf[6:6 + tm, :] * cw_ref[0:1, cols]
        out = out + ubuf[7:7 + tm, :] * cw_ref[1:2, cols]
        out = out + u * cw_ref[2:3, cols]
        carry_sc[:, cols] = u[tm - 8:tm, :]
        tail_ref[0, 0, :, cols] = u if has_override else u[tm - 8:tm, :]
        return out

    ff = jnp.zeros((tm, D_MODEL), F32)
    for c in range(D_FF // FF_CHUNK):
        ga = conv_half(c * FF_CHUNK)
        gb = conv_half(D_FF + c * FF_CHUNK)
        gelu = ga * (0.5 * (1.0 + jnp.tanh(math.sqrt(2.0 / math.pi) * (ga + 0.044715 * (ga * ga * ga)))))
        ff = ff + _dot((gelu * gb).astype(BF16), wdn_ref[c * FF_CHUNK:(c + 1) * FF_CHUNK, :])
    ff = ff + bdn_ref[...]
    y_ref[0] = _layer_norm(ALPHA * x1 + ff, l2w_ref[...], l2b_ref[...])


def _ffn(x, mix_m, mix_n, override, wts):
    bsz, t_len, _ = x.shape
    tm = min(t_len, 512)
    nt = t_len // tm
    has_override = override is not None
    tail_rows = tm if has_override else 8
    row = lambda w: pl.BlockSpec((1, tm, w), lambda b, j: (b, j, 0))
    const = lambda shape: pl.BlockSpec(shape, lambda b, j: (0,) * len(shape), pipeline_mode=pl.Buffered(1))
    in_specs = [row(D_MODEL), row(MLSTM_W), row(NSA_W)]
    args = [x, mix_m, mix_n]
    if has_override:
        in_specs.append(row(2 * D_FF))
        args.append(override)
    in_specs += [const(w.shape) for w in wts]
    args += list(wts)
    y, tail = pl.pallas_call(
        functools.partial(_ffn_body, tm=tm, has_override=has_override),
        grid=(bsz, nt),
        in_specs=in_specs,
        out_specs=[row(D_MODEL), pl.BlockSpec((1, 1, tail_rows, 2 * D_FF), lambda b, j: (b, j, 0, 0))],
        out_shape=[jax.ShapeDtypeStruct((bsz, t_len, D_MODEL), F32),
                   jax.ShapeDtypeStruct((bsz, nt, tail_rows, 2 * D_FF), F32)],
        scratch_shapes=[pltpu.VMEM((8, 2 * D_FF), F32), pltpu.VMEM((8 + tm, FF_CHUNK), F32)],
        compiler_params=pltpu.CompilerParams(dimension_semantics=("arbitrary", "arbitrary"),
                                             vmem_limit_bytes=VMEM_LIMIT),
        name="ffn",
    )(*args)
    return y, tail


def _permute_in_proj(w_in, b_in):
    gates_a = slice(2 * MLSTM_W + 2 * MLSTM_W, 2 * MLSTM_W + 2 * MLSTM_W + 2 * N_MLSTM_HEADS)
    nsa0 = gates_a.stop
    nsa1 = nsa0 + NSA_W + 6 * KV_W

    def perm(a):
        pad = IN_PAD - a.shape[-1]
        parts = [a[..., :gates_a.start], a[..., nsa0:nsa1], a[..., gates_a], a[..., nsa1:]]
        parts.append(jnp.zeros(a.shape[:-1] + (pad,), a.dtype))
        return jnp.concatenate(parts, axis=-1)

    return perm(w_in).astype(BF16), perm(b_in).reshape(1, IN_PAD)


def _front_pad(a, n):
    return jnp.pad(a, ((0, 0), (n, 0), (0, 0)))


def kernel(x_prompt, x_sample, cache_nsa_kv, cache_win_kv, state_mlstm_c, state_mlstm_n, state_mlstm_m,
           state_mlstm_conv, state_ffn_conv, page_table,
           w_in, b_in, mlstm_conv_w, mlstm_conv_b, mlstm_norm_w, nsa_pool_w, rel_bias, w_out,
           ln1_w, ln1_b, w_up, b_up, ffn_conv_w, ffn_conv_b, w_down, b_down, ln2_w, ln2_b):
    bsz, t_len, _ = x_prompt.shape
    dec_b, n_new, _ = x_sample.shape
    assert n_new == N_NEW and cache_win_kv.shape[1] == WINDOW

    w_p, b_p = _permute_in_proj(w_in, b_in)
    pool_w2 = jnp.concatenate([jnp.repeat(nsa_pool_w[0], HEAD_DIM, axis=-1),
                               jnp.repeat(nsa_pool_w[1], HEAD_DIM, axis=-1)], axis=-1)
    row = lambda v: v.reshape(1, -1)
    ffn_w = (w_out[:MLSTM_W].astype(BF16), w_out[MLSTM_W:].astype(BF16), row(ln1_w), row(ln1_b),
             w_up.astype(BF16), row(b_up), ffn_conv_w, row(ffn_conv_b), w_down.astype(BF16), row(b_down),
             row(ln2_w), row(ln2_b))

    zqk, zv, zo, nq, kv, win, zg, kv_t = _proj_in(x_prompt.reshape(bsz * t_len, D_MODEL), w_p, b_p, seq_len=t_len)
    seq3 = lambda a: a.reshape(bsz, t_len, a.shape[-1])
    zqk, zv, zo, nq, kv, win, zg = map(seq3, (zqk, zv, zo, nq, kv, win, zg))
    mix_m, c_p, n_p, m_p = _mlstm(zqk, zv, zo, zg, None, mlstm_conv_w, mlstm_conv_b, mlstm_norm_w, 0)
    mix_n = _nsa_prompt(nq, kv, win, zg, pool_w2, rel_bias)
    y_p, tail_p = _ffn(x_prompt, mix_m, mix_n, None, ffn_w)
    kv_p = jnp.transpose(kv_t.reshape(bsz, 4, N_KV_HEADS, HEAD_DIM, t_len), (0, 4, 1, 2, 3))
    win_p = win[:, t_len - min(WINDOW, t_len):].reshape(bsz, -1, 2, N_KV_HEADS, HEAD_DIM)
    mconv_p = zqk[:, t_len - (MLSTM_CONV - 1):]
    fconv_p = tail_p[:, -1, 8 - (FFN_CONV - 1):]

    pad = SEQ_PAD - N_NEW
    xs = _front_pad(x_sample, pad)
    szqk, szv, szo, snq, skv, swin, szg = _proj_in(xs.reshape(dec_b * SEQ_PAD, D_MODEL), w_p, b_p)
    sseq = lambda a: a.reshape(dec_b, SEQ_PAD, a.shape[-1])
    szqk, szv, szo, snq, skv, swin, szg = map(sseq, (szqk, szv, szo, snq, skv, swin, szg))
    lead = MLSTM_CHUNK - N_NEW
    new = lambda a: a[:, pad:]
    m_qk = jnp.concatenate([jnp.zeros((dec_b, lead - (MLSTM_CONV - 1), 2 * MLSTM_W), F32),
                            state_mlstm_conv.astype(F32), new(szqk)], axis=1)
    mix_ms, c_s, n_s, m_s = _mlstm(m_qk, _front_pad(new(szv), lead), _front_pad(new(szo), lead),
                                   _front_pad(new(szg), lead),
                                   (state_mlstm_c, state_mlstm_n, state_mlstm_m),
                                   mlstm_conv_w, mlstm_conv_b, mlstm_norm_w, lead)
    mix_ms = mix_ms[:, MLSTM_CHUNK - SEQ_PAD:]
    cache3 = jnp.transpose(cache_nsa_kv, (0, 2, 3, 4, 1)).reshape(cache_nsa_kv.shape[0], 4 * KV_W, PAGE_SIZE)
    pool_wt = jnp.tile(jnp.repeat(jnp.transpose(nsa_pool_w, (0, 2, 1)).reshape(2 * N_KV_HEADS, CMP_BLOCK),
                                  HEAD_DIM, axis=0), (1, PAGE_SIZE // CMP_BLOCK))
    o_c, ids_rows = _nsa_cmp(page_table, rel_bias, snq, pool_wt, cache3)
    ids = jnp.transpose(ids_rows[:, :N_PICK, :N_KV_HEADS * SEQ_PAD], (0, 2, 1))
    ids = ids.reshape(dec_b, N_KV_HEADS, SEQ_PAD, N_PICK)[:, :, pad:]
    win_cache = jnp.transpose(cache_win_kv, (0, 2, 3, 4, 1)).reshape(dec_b, 2 * KV_W, WINDOW)
    mix_ns = _nsa_sel(page_table, ids, ids_rows, rel_bias, snq, skv, win_cache, swin, szg, o_c, cache3)
    override = jnp.concatenate([jnp.zeros((dec_b, pad - (FFN_CONV - 1), 2 * D_FF), F32),
                                state_ffn_conv.astype(F32),
                                jnp.zeros((dec_b, N_NEW, 2 * D_FF), F32)], axis=1)
    flat = lambda a: a.reshape(1, dec_b * SEQ_PAD, a.shape[-1])
    y_s8, u_s = _ffn(flat(xs), flat(mix_ms), flat(mix_ns), flat(override), ffn_w)
    y_s = y_s8.reshape(dec_b, SEQ_PAD, D_MODEL)[:, pad:]
    kv_s = new(skv).reshape(dec_b, N_NEW, 4, N_KV_HEADS, HEAD_DIM)
    win_s = jnp.concatenate([cache_win_kv[:, N_NEW:], new(swin).reshape(dec_b, N_NEW, 2, N_KV_HEADS, HEAD_DIM)],
                            axis=1)
    mconv_s = szqk[:, SEQ_PAD - (MLSTM_CONV - 1):]
    fconv_s = u_s.reshape(dec_b, SEQ_PAD, 2 * D_FF)[:, SEQ_PAD - (FFN_CONV - 1):]
    return (y_p, y_s, kv_p, kv_s, win_p, win_s, c_p, c_s, n_p, n_s, m_p, m_s, mconv_p, mconv_s, fconv_p, fconv_s)
```

```python
import functools
import math

import jax
import jax.numpy as jnp
from jax import lax
from jax.experimental import pallas as pl
from jax.experimental.pallas import tpu as pltpu

F32 = jnp.float32
BF16 = jnp.bfloat16
I32 = jnp.int32

D_MODEL = 1024
HEAD_DIM = 64
N_MLSTM_HEADS = 8
N_NSA_HEADS = 8
N_KV_HEADS = 2
GQ = N_NSA_HEADS // N_KV_HEADS
MLSTM_W = N_MLSTM_HEADS * HEAD_DIM
NSA_W = N_NSA_HEADS * HEAD_DIM
KV_W = N_KV_HEADS * HEAD_DIM
MLSTM_CHUNK = 64
MLSTM_CONV = 4
CMP_BLOCK = 32
SEL_BLOCK = 64
SEL_TOPK = 16
WINDOW = 512
NSA_QBLOCK = 64
N_BUCKETS = 32
MAX_DISTANCE = 128
D_FF = 2688
FFN_CONV = 3
DEPTH = 1
ALPHA = (2.0 * DEPTH) ** 0.25
LN_EPS = 1e-5
PAGE_SIZE = 128

LANES = 128
VMEM_LIMIT = 56 * 1024 * 1024
NEG_INF = float("-inf")

SEQ_PAD = 8
N_NEW = 4
N_QROWS = N_NSA_HEADS * SEQ_PAD


def _bucket_thresholds():
    max_exact = N_BUCKETS // 2
    thr = list(range(max_exact + 1))
    for k in range(1, N_BUCKETS - max_exact):
        thr.append(int(math.ceil(max_exact * (MAX_DISTANCE / max_exact) ** (k / (N_BUCKETS - max_exact)))))
    return tuple(thr)


BUCKET_THR = _bucket_thresholds()
FAR_DIST = BUCKET_THR[N_BUCKETS - 1]


def _dot(a, b, precision=None):
    return jnp.dot(a, b, preferred_element_type=F32, precision=precision)


def _dot_nt(a, b, precision=None):
    return lax.dot_general(a, b, (((1,), (1,)), ((), ())), preferred_element_type=F32, precision=precision)


def _dot_tn(a, b, precision=None):
    return lax.dot_general(a, b, (((0,), (0,)), ((), ())), preferred_element_type=F32, precision=precision)


def _layer_norm(x, w, b):
    mu = jnp.mean(x, -1, keepdims=True)
    xc = x - mu
    var = jnp.mean(xc * xc, -1, keepdims=True)
    return xc * lax.rsqrt(var + LN_EPS) * w + b


def _log_sigmoid(x):
    return jnp.minimum(x, 0.0) - jnp.log1p(jnp.exp(-jnp.abs(x)))


def _bias_from_dist(dist, rb_ref, head):
    acc = jnp.full(dist.shape, rb_ref[0, head], F32)
    for k in range(1, N_BUCKETS):
        acc = jnp.where(dist >= BUCKET_THR[k], rb_ref[k, head], acc)
    return acc


def _masked_softmax(s):
    mx = jnp.max(s, -1, keepdims=True)
    mx = jnp.where(mx > NEG_INF, mx, 0.0)
    e = jnp.exp(s - mx)
    den = jnp.sum(e, -1, keepdims=True)
    return e / jnp.maximum(den, 1e-30)


_C_QK, _C_V, _C_O, _C_NQ, _C_KV, _C_WIN, _C_G = 0, 1024, 1536, 2048, 2560, 3072, 3328
IN_PAD = 3456
G_IGATE, G_FGATE, G_NSA = 0, N_MLSTM_HEADS, 2 * N_MLSTM_HEADS


def _proj_in_body(*refs, kv_transposed):
    if kv_transposed:
        x_ref, w_ref, b_ref, wkvt_ref, bkv_ref, qk_ref, v_ref, o_ref, nq_ref, kv_ref, win_ref, g_ref, kvt_ref = refs
    else:
        x_ref, w_ref, b_ref, qk_ref, v_ref, o_ref, nq_ref, kv_ref, win_ref, g_ref = refs
    xb = x_ref[...].astype(BF16)
    z = _dot(xb, w_ref[...]) + b_ref[...]
    qk_ref[...] = z[:, _C_QK:_C_V]
    v_ref[...] = z[:, _C_V:_C_O]
    o_ref[...] = z[:, _C_O:_C_NQ]
    nq_ref[...] = z[:, _C_NQ:_C_KV]
    kv_ref[...] = z[:, _C_KV:_C_WIN]
    win_ref[...] = z[:, _C_WIN:_C_G]
    g_ref[...] = z[:, _C_G:IN_PAD]
    if kv_transposed:
        kvt_ref[0] = _dot_nt(wkvt_ref[...], xb) + bkv_ref[...]


def _proj_in(x2d, w_p, b_p, seq_len=None):
    rows = x2d.shape[0]
    tm = min(rows, 256)
    widths = (1024, 512, 512, 512, 512, 256, 128)
    const = lambda i: (0, 0)
    const_spec = lambda shape: pl.BlockSpec(shape, const, pipeline_mode=pl.Buffered(1))
    in_specs = [pl.BlockSpec((tm, D_MODEL), lambda i: (i, 0)), const_spec((D_MODEL, IN_PAD)), const_spec((1, IN_PAD))]
    args = [x2d, w_p, b_p]
    out_specs = [pl.BlockSpec((tm, w), lambda i: (i, 0)) for w in widths]
    out_shape = [jax.ShapeDtypeStruct((rows, w), F32) for w in widths]
    if seq_len is not None:
        tiles = seq_len // tm
        in_specs += [const_spec((4 * KV_W, D_MODEL)), const_spec((4 * KV_W, 1))]
        args += [jnp.transpose(w_p[:, _C_KV:_C_WIN]), b_p[0, _C_KV:_C_WIN].reshape(4 * KV_W, 1)]
        out_specs.append(pl.BlockSpec((1, 4 * KV_W, tm), lambda i: (i // tiles, 0, i % tiles)))
        out_shape.append(jax.ShapeDtypeStruct((rows // seq_len, 4 * KV_W, seq_len), F32))
    return pl.pallas_call(
        functools.partial(_proj_in_body, kv_transposed=seq_len is not None),
        grid=(rows // tm,),
        in_specs=in_specs,
        out_specs=out_specs,
        out_shape=out_shape,
        compiler_params=pltpu.CompilerParams(dimension_semantics=("arbitrary",),
                                             vmem_limit_bytes=VMEM_LIMIT),
        name="proj_in",
    )(*args)


N_PAIRS = N_MLSTM_HEADS // 2
GATE_LANE0 = N_MLSTM_HEADS
MLSTM_SEQS_PER_STEP = 4


def _dot_split(x, onehot_bf16, pieces):
    acc = None
    rem = x
    for n in range(pieces):
        part = rem.astype(BF16)
        d = _dot(part, onehot_bf16)
        acc = d if acc is None else acc + d
        if n + 1 < pieces:
            rem = rem - part.astype(F32)
    return acc


def _mlstm_body(*refs, tb, nbg, n_dummy, has_state):
    if has_state:
        (zqk_ref, zv_ref, zo_ref, zg_ref, c0_ref, n0_ref, m0_ref, cw_ref, cb_ref, nw_ref,
         mix_ref, c_out, n_out, m_out, ubuf, qk_sc, ct_sc, n_sc, m_sc) = refs
    else:
        (zqk_ref, zv_ref, zo_ref, zg_ref, cw_ref, cb_ref, nw_ref,
         mix_ref, c_out, n_out, m_out, ubuf, qk_sc, ct_sc, n_sc, m_sc) = refs
    L = MLSTM_CHUNK
    D = HEAD_DIM
    j = pl.program_id(1)
    hi = lax.Precision.HIGHEST
    head_lanes = slice(GATE_LANE0, GATE_LANE0 + N_MLSTM_HEADS)
    pairs = [(bi, p) for bi in range(nbg) for p in range(N_PAIRS)]

    @pl.when(j == 0)
    def _init():
        ubuf[:, 0:8, :] = jnp.zeros((nbg, 8, 2 * MLSTM_W), F32)
        ct_sc[...] = jnp.zeros(ct_sc.shape, F32)
        n_sc[...] = jnp.zeros(n_sc.shape, F32)
        m_sc[...] = jnp.zeros(m_sc.shape, F32)
        if has_state:
            for bi in range(nbg):
                for h in range(N_MLSTM_HEADS):
                    p, hh = divmod(h, 2)
                    hs = slice(hh * D, (hh + 1) * D)
                    ct_sc[bi, p, hs, hs] = c0_ref[bi, h]
                    n_sc[bi, p, :, hs] = n0_ref[bi, h:h + 1, :]
                m_sc[bi, :, head_lanes] = m0_ref[bi]

    for bi in range(nbg):
        u = zqk_ref[bi]
        ubuf[bi, 8:8 + tb, :] = u
        conv = cb_ref[...] + ubuf[bi, 5:5 + tb, :] * cw_ref[0:1, :]
        conv = conv + ubuf[bi, 6:6 + tb, :] * cw_ref[1:2, :]
        conv = conv + ubuf[bi, 7:7 + tb, :] * cw_ref[2:3, :]
        conv = conv + u * cw_ref[3:4, :]
        ubuf[bi, 5:8, :] = u[tb - 3:tb, :]
        qk_sc[bi] = conv * jax.nn.sigmoid(conv)

    tri = (lax.broadcasted_iota(I32, (L, L), 1) <= lax.broadcasted_iota(I32, (L, L), 0)).astype(F32)
    row_l = lax.broadcasted_iota(I32, (L, LANES), 0)
    lane = lax.broadcasted_iota(I32, (L, LANES), 1)
    low = lane < D
    key_tok = lane % D
    causal_pair = key_tok <= row_l
    eye_pair = key_tok == row_l
    spread = (lax.broadcasted_iota(I32, (LANES, MLSTM_W), 0) - GATE_LANE0
              == lax.broadcasted_iota(I32, (LANES, MLSTM_W), 1) // D).astype(BF16)
    same_head = (lax.broadcasted_iota(I32, (LANES, LANES), 0) // D
                 == lax.broadcasted_iota(I32, (LANES, LANES), 1) // D)
    ones_bd = same_head.astype(BF16)
    slab_rows = 3 * L + 8

    def chunk(ci, carry):
        r0 = pl.multiple_of(ci * L, L)
        rows = pl.ds(r0, L)
        gates = []
        for bi in range(nbg):
            g = zg_ref[bi, rows, :]
            if n_dummy:
                dummy = (row_l + ci * L) < n_dummy
                g = jnp.where(dummy, jnp.where(lane < G_FGATE, -1e30, 1e30), g)
            gates.append(g)
        b_all = _dot(tri, jnp.concatenate([_log_sigmoid(g) for g in gates], axis=1), hi)
        stacks = []
        for bi in range(nbg):
            b = b_all[:, bi * LANES:(bi + 1) * LANES]
            r = pltpu.roll(gates[bi], GATE_LANE0 - G_IGATE, 1) - b
            cm = r
            for sh in (1, 2, 4, 8, 16, 32):
                cm = jnp.where(row_l >= sh, jnp.maximum(cm, pltpu.roll(cm, sh, 0)), cm)
            m_st = m_sc[bi]
            mx = jnp.maximum(m_st, cm)
            mx_end = mx[L - 1:L, :]
            m_sc[bi] = b[L - 1:L, :] + mx_end
            stacks += [mx, r, b + mx, m_st, mx_end, jnp.zeros((6, LANES), F32)]
        x_all = _dot_split(jnp.concatenate(stacks, axis=0), spread, 2)

        work = []
        den_terms = []
        for bi, p in pairs:
            cs = slice(p * LANES, (p + 1) * LANES)
            x0 = bi * slab_rows
            mx_p, rs_p, mt_p = x_all[x0:x0 + L, cs], x_all[x0 + L:x0 + 2 * L, cs], x_all[x0 + 2 * L:x0 + 3 * L, cs]
            mst_p, mxe_p = x_all[x0 + 3 * L:x0 + 3 * L + 1, cs], x_all[x0 + 3 * L + 1:x0 + 3 * L + 2, cs]
            wi_p = jnp.exp(mst_p - mx_p)
            fl_p = jnp.exp(-mt_p)
            wk_p = jnp.exp(rs_p - mxe_p)
            sc_p = jnp.exp(mst_p - mxe_p)
            r_row = jnp.sum(jnp.where(eye_pair, rs_p, 0.0), axis=0, keepdims=True)
            w_intra = jnp.exp(jnp.where(causal_pair, r_row - mx_p, NEG_INF))
            q_p = qk_sc[bi, rows, cs]
            k_p = qk_sc[bi, rows, MLSTM_W + p * LANES:MLSTM_W + (p + 1) * LANES] * (D ** -0.5)
            v_p = zv_ref[bi, rows, cs]
            qb = q_p.astype(BF16)
            k_bd = jnp.concatenate([jnp.where(low, k_p, 0.0), jnp.where(low, 0.0, k_p)], axis=0)
            v_bd = jnp.concatenate([jnp.where(low, v_p, 0.0), jnp.where(low, 0.0, v_p)], axis=0).astype(BF16)
            c_t = ct_sc[bi, p]
            both = _dot_nt(qb, jnp.concatenate([k_bd, c_t], axis=0).astype(BF16))
            a = w_intra * both[:, 0:LANES]
            num = _dot(a.astype(BF16), v_bd) + both[:, LANES:2 * LANES] * wi_p
            n_row = n_sc[bi, p]
            den_terms.append(a + wi_p * (q_p * n_row))
            work.append((num, fl_p))
            kw = k_p * wk_p
            upd = _dot_tn(v_p.astype(BF16), kw.astype(BF16))
            ct_sc[bi, p] = sc_p * c_t + jnp.where(same_head, upd, 0.0)
            n_sc[bi, p] = sc_p * n_row + jnp.sum(kw, axis=0, keepdims=True)
        den_all = _dot_split(jnp.concatenate(den_terms, axis=0), ones_bd, 2)

        hhs = []
        for n, (num, fl_p) in enumerate(work):
            hhs.append(num / jnp.maximum(jnp.abs(den_all[n * L:(n + 1) * L, :]), fl_p))
        mu_all = _dot_split(jnp.concatenate(hhs, axis=0), ones_bd, 1) * (1.0 / D)
        hcs = [hh - mu_all[n * L:(n + 1) * L, :] for n, hh in enumerate(hhs)]
        var_all = _dot_split(jnp.concatenate([hc * hc for hc in hcs], axis=0), ones_bd, 1) * (1.0 / D)
        for n, (bi, p) in enumerate(pairs):
            cs = slice(p * LANES, (p + 1) * LANES)
            hn = hcs[n] * lax.rsqrt(var_all[n * L:(n + 1) * L, :] + LN_EPS)
            og = jax.nn.sigmoid(zo_ref[bi, rows, cs])
            mix_ref[bi, rows, cs] = og * (hn * nw_ref[:, cs])
        return carry

    lax.fori_loop(0, tb // L, chunk, 0, unroll=2 if (tb // L) % 2 == 0 else 1)

    @pl.when(j == pl.num_programs(1) - 1)
    def _final():
        for bi in range(nbg):
            for h in range(N_MLSTM_HEADS):
                p, hh = divmod(h, 2)
                hs = slice(hh * D, (hh + 1) * D)
                c_out[bi, h] = ct_sc[bi, p, hs, hs]
                n_out[bi, h:h + 1, :] = n_sc[bi, p, :, hs]
            m_out[bi] = m_sc[bi, :, head_lanes]


def _mlstm(zqk, zv, zo, zg, state, conv_w, conv_b, norm_w, n_dummy):
    bsz, t_len, _ = zqk.shape
    tb = min(t_len, 256)
    nbg = MLSTM_SEQS_PER_STEP
    assert bsz % nbg == 0
    has_state = state is not None
    row = lambda w: pl.BlockSpec((nbg, tb, w), lambda b, j: (b, j, 0))
    per_b = lambda shape: pl.BlockSpec((nbg,) + shape, lambda b, j: (b,) + (0,) * len(shape))
    const = lambda shape: pl.BlockSpec(shape, lambda b, j: (0,) * len(shape))
    in_specs = [row(2 * MLSTM_W), row(MLSTM_W), row(MLSTM_W), row(LANES)]
    args = [zqk, zv, zo, zg]
    if has_state:
        c0, n0, m0 = state
        in_specs += [per_b((N_MLSTM_HEADS, HEAD_DIM, HEAD_DIM)), per_b((N_MLSTM_HEADS, HEAD_DIM)),
                     per_b((1, N_MLSTM_HEADS))]
        args += [c0, n0, m0.reshape(bsz, 1, N_MLSTM_HEADS)]
    in_specs += [const((MLSTM_CONV, 2 * MLSTM_W)), const((1, 2 * MLSTM_W)), const((1, MLSTM_W))]
    args += [conv_w, conv_b.reshape(1, -1), norm_w.reshape(1, -1)]
    mix, c_new, n_new, m_new = pl.pallas_call(
        functools.partial(_mlstm_body, tb=tb, nbg=nbg, n_dummy=n_dummy, has_state=has_state),
        grid=(bsz // nbg, t_len // tb),
        in_specs=in_specs,
        out_specs=[row(MLSTM_W), per_b((N_MLSTM_HEADS, HEAD_DIM, HEAD_DIM)),
                   per_b((N_MLSTM_HEADS, HEAD_DIM)), per_b((1, N_MLSTM_HEADS))],
        out_shape=[jax.ShapeDtypeStruct((bsz, t_len, MLSTM_W), F32),
                   jax.ShapeDtypeStruct((bsz, N_MLSTM_HEADS, HEAD_DIM, HEAD_DIM), F32),
                   jax.ShapeDtypeStruct((bsz, N_MLSTM_HEADS, HEAD_DIM), F32),
                   jax.ShapeDtypeStruct((bsz, 1, N_MLSTM_HEADS), F32)],
        scratch_shapes=[pltpu.VMEM((nbg, 8 + tb, 2 * MLSTM_W), F32),
                        pltpu.VMEM((nbg, tb, 2 * MLSTM_W), F32),
                        pltpu.VMEM((nbg, N_PAIRS, LANES, LANES), F32),
                        pltpu.VMEM((nbg, N_PAIRS, 1, LANES), F32),
                        pltpu.VMEM((nbg, 1, LANES), F32)],
        compiler_params=pltpu.CompilerParams(dimension_semantics=("arbitrary", "arbitrary"),
                                             vmem_limit_bytes=VMEM_LIMIT),
        name="mlstm",
    )(*args)
    return mix, c_new, n_new, m_new.reshape(bsz, N_MLSTM_HEADS)


def _block_diag_queries(x, n_rows, heads=tuple(range(N_NSA_HEADS)), scale=HEAD_DIM ** -0.5):
    lane = lax.broadcasted_iota(I32, (n_rows, LANES), 1)
    low = lane < HEAD_DIM
    pieces = []
    for h in heads:
        pair = x[:, (h // 2) * LANES:(h // 2 + 1) * LANES]
        src_low = h % 2 == 0
        dst_low = h // GQ == 0
        if src_low != dst_low:
            pair = pltpu.roll(pair, HEAD_DIM, 1)
        pieces.append(jnp.where(low if dst_low else ~low, pair, 0.0))
    return (jnp.concatenate(pieces, axis=0) * scale).astype(BF16)


def _group_lanes(o, g):
    return o[:, g * HEAD_DIM:(g + 1) * HEAD_DIM]


def _gated_mix(gates, o_c, o_s, o_w, h, n_rows):
    g = h // GQ
    rs = slice(h * n_rows, (h + 1) * n_rows)
    col = lambda branch: gates[:, G_NSA + branch * N_NSA_HEADS + h:G_NSA + branch * N_NSA_HEADS + h + 1]
    return (col(0) * _group_lanes(o_c[rs], g) + col(1) * _group_lanes(o_s[rs], g)
            + col(2) * _group_lanes(o_w[rs], g))


SEL_TILE = 256
SEL_TILE_BLOCKS = SEL_TILE // SEL_BLOCK
FAR_GROUP = 2
WIN_BAND = WINDOW + NSA_QBLOCK
N_COLS = N_NSA_HEADS * NSA_QBLOCK
PROMPT_HEADS = tuple(GQ * (jj % N_KV_HEADS) + jj // N_KV_HEADS for jj in range(N_NSA_HEADS))
SEL_REL0 = 2 * SEL_TILE - SEL_BLOCK
SEL_BIAS_ROWS = SEL_REL0 + SEL_TILE


LOG2E = math.log2(math.e)


def _softmax2_rows(s):
    mx = jnp.max(s, 0, keepdims=True)
    mx = jnp.where(mx > NEG_INF, mx, 0.0)
    e = jnp.exp2(s - mx)
    return e / jnp.maximum(jnp.sum(e, 0, keepdims=True), 1e-30)


def _nsa_prompt_body(rb_ref, nq_ref, nqn_ref, kv_ref, win_ref, zg_ref, pw_ref, out_ref,
                     ks_sc, vst_sc, wk_sc, wv_sc, kc_sc, pv_sc, vct_sc, bs_sc, bw_sc, fc_sc,
                     m_sc, l_sc, acc_sc, ch_sc, oc_sc, fs_sc, ft_sc, *, t_len):
    QB = NSA_QBLOCK
    n_cmp = t_len // CMP_BLOCK
    n_sel = t_len // SEL_BLOCK
    i = pl.program_id(1)
    hi = lax.Precision.HIGHEST

    def cmp_scores(x_q):
        return _dot_nt(kc_sc[...], _block_diag_queries(x_q, QB, PROMPT_HEADS, HEAD_DIM ** -0.5 * LOG2E))

    def cmp_and_select(raw, qi, out_buf):
        s_c = raw + pltpu.roll(fc_sc[...], 2 * qi, 0)
        c_i = lax.broadcasted_iota(I32, (n_cmp, N_COLS), 0)
        q_i = lax.broadcasted_iota(I32, (n_cmp, N_COLS), 1) % QB
        p_c = _softmax2_rows(jnp.where(c_i * CMP_BLOCK + (CMP_BLOCK - 1) <= qi * QB + q_i, s_c, NEG_INF))
        oc_sc[out_buf] = _dot(vct_sc[...], p_c.astype(BF16))
        p_sum = p_c[:, 0:LANES]
        for j in range(1, GQ):
            p_sum = p_sum + p_c[:, j * LANES:(j + 1) * LANES]
        ratio = SEL_BLOCK // CMP_BLOCK
        pair = (lax.broadcasted_iota(I32, (n_sel, n_cmp), 1) // ratio
                == lax.broadcasted_iota(I32, (n_sel, n_cmp), 0)).astype(F32)
        imp = _dot(pair, p_sum, hi)
        blk = lax.broadcasted_iota(I32, (n_sel, LANES), 0)
        score = jnp.where(blk == qi, jnp.inf, jnp.where(blk < qi, imp, NEG_INF))

        def pick(_, carry):
            work, chosen = carry
            best = jnp.max(work, 0, keepdims=True)
            first = jnp.min(jnp.where(work == best, blk, n_sel), 0, keepdims=True)
            hit = blk == first
            return jnp.where(hit, NEG_INF, work), jnp.where(hit, 1.0, chosen)

        _, chosen = lax.fori_loop(0, SEL_TOPK, pick, (score, jnp.zeros((n_sel, LANES), F32)), unroll=True)
        ch_sc[out_buf, 0:8, :] = jnp.full((8, LANES), NEG_INF, F32)
        ch_sc[out_buf, 8:8 + n_sel, :] = jnp.where((score > NEG_INF) & (chosen > 0.5), 0.0, NEG_INF)

    @pl.when(i == 0)
    def _init():
        ks_sc[0:SEL_TILE, :] = jnp.zeros((SEL_TILE, LANES), BF16)
        vst_sc[0] = jnp.zeros((LANES, SEL_TILE), BF16)
        wk_sc[0:WINDOW, :] = jnp.zeros((WINDOW, LANES), BF16)
        wv_sc[0:WINDOW, :] = jnp.zeros((WINDOW, LANES), BF16)
        rows_per = 2 * SEL_TILE
        blocks_per = rows_per // CMP_BLOCK

        def fill(c, carry):
            r0 = pl.multiple_of(c * rows_per, rows_per)
            kv = kv_ref[0, pl.ds(r0, rows_per), :]
            ks_sc[pl.ds(SEL_TILE + r0, rows_per), :] = kv[:, 2 * LANES:3 * LANES].astype(BF16)
            for half in range(2):
                v_t = kv[half * SEL_TILE:(half + 1) * SEL_TILE, 3 * LANES:4 * LANES]
                vst_sc[2 * c + half + 1] = jnp.transpose(v_t).astype(BF16)
            w = win_ref[0, pl.ds(r0, rows_per), :]
            wk_sc[pl.ds(WINDOW + r0, rows_per), :] = w[:, 0:LANES].astype(BF16)
            wv_sc[pl.ds(WINDOW + r0, rows_per), :] = w[:, LANES:2 * LANES].astype(BF16)
            pooled = jnp.sum(kv[:, 0:2 * LANES].reshape(blocks_per, CMP_BLOCK, 2 * LANES) * pw_ref[...][None],
                             axis=1)
            c0 = pl.multiple_of(c * blocks_per, blocks_per)
            kc_sc[pl.ds(c0, blocks_per), :] = pooled[:, 0:LANES].astype(BF16)
            pv_sc[pl.ds(c0, blocks_per), :] = pooled[:, LANES:2 * LANES]
            return carry

        lax.fori_loop(0, t_len // rows_per, fill, 0)
        vct_sc[...] = jnp.transpose(pv_sc[...]).astype(BF16)

        for jj, h in enumerate(PROMPT_HEADS):
            cs = slice(jj * QB, (jj + 1) * QB)
            u_i = lax.broadcasted_iota(I32, (SEL_BIAS_ROWS, QB), 0)
            q_i = lax.broadcasted_iota(I32, (SEL_BIAS_ROWS, QB), 1)
            dist = q_i - (u_i - SEL_REL0)
            near = (_bias_from_dist(dist, rb_ref, h) - rb_ref[N_BUCKETS - 1, h]) * LOG2E
            bs_sc[:, cs] = jnp.where(dist >= 0, near, NEG_INF)
            s_i = lax.broadcasted_iota(I32, (WIN_BAND, QB), 0)
            q_i = lax.broadcasted_iota(I32, (WIN_BAND, QB), 1)
            dist = q_i + WINDOW - s_i
            bw_sc[:, cs] = jnp.where((dist >= 0) & (dist <= WINDOW), _bias_from_dist(dist, rb_ref, h) * LOG2E,
                                     NEG_INF)
            c_i = lax.broadcasted_iota(I32, (n_cmp, QB), 0)
            q_i = lax.broadcasted_iota(I32, (n_cmp, QB), 1)
            rel = jnp.where(c_i < n_cmp // 2, c_i, c_i - n_cmp)
            dist = q_i - CMP_BLOCK * rel - (CMP_BLOCK - 1)
            dist = jnp.where((c_i >= 2) & (c_i < n_cmp // 2), FAR_DIST, dist)
            fc_sc[:, cs] = _bias_from_dist(dist, rb_ref, h) * LOG2E

        cmp_and_select(cmp_scores(nq_ref[0]), 0, 0)

    buf = i % 2
    o_c = oc_sc[buf]
    qbd = _block_diag_queries(nq_ref[0], QB, PROMPT_HEADS, HEAD_DIM ** -0.5 * LOG2E)
    last = i // SEL_TILE_BLOCKS

    def tile_raw(slot):
        return _dot_nt(ks_sc[pl.ds(pl.multiple_of(slot * SEL_TILE, SEL_TILE), SEL_TILE), :], qbd)

    raw_next = cmp_scores(nqn_ref[0])
    w0 = pl.multiple_of(i * QB, QB)
    raw_w = _dot_nt(wk_sc[pl.ds(w0, WIN_BAND), :], qbd)
    raw_near = [tile_raw(last + 1), tile_raw(last)]

    def tile_scores(slot, bias, s=None):
        if s is None:
            s = tile_raw(slot)
        if bias is not None:
            s = s + bias
        parts = []
        for r in range(SEL_TILE_BLOCKS):
            row = ch_sc[buf, pl.ds(8 - SEL_TILE_BLOCKS + slot * SEL_TILE_BLOCKS + r, 1), :]
            parts.append(s[r * SEL_BLOCK:(r + 1) * SEL_BLOCK, :] + jnp.concatenate([row] * GQ, axis=1))
        return jnp.concatenate(parts, axis=0)

    def group_max(scores):
        return functools.reduce(jnp.maximum, [jnp.max(s, 0, keepdims=True) for s in scores])

    def sel_group(slots, scores, t_max, first):
        m_new = t_max if first else jnp.maximum(m_sc[...], t_max)
        exps = [jnp.exp2(s - m_new) for s in scores]
        l_new = functools.reduce(jnp.add, [jnp.sum(e, 0, keepdims=True) for e in exps])
        acc_new = functools.reduce(jnp.add, [_dot(vst_sc[slot], e.astype(BF16)) for slot, e in zip(slots, exps)])
        if first:
            l_sc[...] = l_new
            acc_sc[...] = acc_new
        else:
            scale = jnp.exp2(m_sc[...] - m_new)
            l_sc[...] = scale * l_sc[...] + l_new
            acc_sc[...] = scale * acc_sc[...] + acc_new
        m_sc[...] = m_new

    def far_slots(p):
        return [jnp.where(FAR_GROUP * p + 1 + n < last, FAR_GROUP * p + 1 + n, 0) for n in range(FAR_GROUP)]

    def far_scores(p):
        scores = [tile_scores(slot, None) for slot in far_slots(p)]
        for n, s in enumerate(scores):
            fs_sc[p % 2, n] = s
        ft_sc[p % 2] = group_max(scores)

    far_scores(0)
    cmp_and_select(raw_next, i + 1, 1 - buf)

    in_seq = lax.broadcasted_iota(I32, (WIN_BAND, N_COLS), 0) >= WINDOW - i * QB
    p_w = _softmax2_rows(jnp.where(in_seq, raw_w + bw_sc[...], NEG_INF))
    o_w = _dot_tn(wv_sc[pl.ds(w0, WIN_BAND), :], p_w.astype(BF16))

    u0 = pl.multiple_of(SEL_REL0 - SEL_TILE - SEL_BLOCK * (i % SEL_TILE_BLOCKS), SEL_BLOCK)
    near = [tile_scores(last + 1, bs_sc[pl.ds(u0 + SEL_TILE, SEL_TILE), :], raw_near[0]),
            tile_scores(last, bs_sc[pl.ds(u0, SEL_TILE), :], raw_near[1])]
    sel_group((last + 1, last), near, group_max(near), True)

    def far_group(p, carry):
        sel_group(far_slots(p), [fs_sc[p % 2, n] for n in range(FAR_GROUP)], ft_sc[p % 2], False)
        far_scores(p + 1)
        return carry

    lax.fori_loop(0, (last - 1 + FAR_GROUP - 1) // FAR_GROUP, far_group, 0)
    o_s = acc_sc[...] / jnp.maximum(l_sc[...], 1e-30)

    eye = (lax.broadcasted_iota(I32, (3 * N_NSA_HEADS + 8, LANES), 0) + G_NSA
           == lax.broadcasted_iota(I32, (3 * N_NSA_HEADS + 8, LANES), 1)).astype(F32)
    gates = jax.nn.sigmoid(_dot_nt(eye, zg_ref[0], hi))
    pieces = []
    for h in range(N_NSA_HEADS):
        g = h // GQ
        jj = PROMPT_HEADS.index(h)
        part = lambda o: o[g * HEAD_DIM:(g + 1) * HEAD_DIM, jj * QB:(jj + 1) * QB]
        pieces.append(gates[h:h + 1, :] * part(o_c)
                      + gates[N_NSA_HEADS + h:N_NSA_HEADS + h + 1, :] * part(o_s)
                      + gates[2 * N_NSA_HEADS + h:2 * N_NSA_HEADS + h + 1, :] * part(o_w))
    mix_t = jnp.concatenate(pieces, axis=0)
    mix_t = jnp.concatenate([mix_t, jnp.zeros((NSA_W, LANES - QB), F32)], axis=1)
    out_ref[0] = jnp.transpose(mix_t)[0:QB, :]


def _nsa_prompt(nq, kv, win, zg, pool_w2, rel_bias):
    bsz, t_len, _ = nq.shape
    assert t_len // CMP_BLOCK == LANES and t_len % (2 * SEL_TILE) == 0
    n_cmp = t_len // CMP_BLOCK
    per_b = lambda w: pl.BlockSpec((1, t_len, w), lambda b, i: (b, 0, 0))
    blk = lambda w: pl.BlockSpec((1, NSA_QBLOCK, w), lambda b, i: (b, i, 0))
    n_qb = t_len // NSA_QBLOCK
    next_blk = pl.BlockSpec((1, NSA_QBLOCK, NSA_W), lambda b, i: (b, jnp.minimum(i + 1, n_qb - 1), 0))
    return pl.pallas_call(
        functools.partial(_nsa_prompt_body, t_len=t_len),
        grid=(bsz, n_qb),
        in_specs=[pl.BlockSpec(memory_space=pltpu.SMEM),
                  blk(NSA_W), next_blk, per_b(4 * KV_W), per_b(2 * KV_W), blk(LANES),
                  pl.BlockSpec((CMP_BLOCK, 2 * LANES), lambda b, i: (0, 0))],
        out_specs=blk(NSA_W),
        out_shape=jax.ShapeDtypeStruct((bsz, t_len, NSA_W), F32),
        scratch_shapes=[pltpu.VMEM((SEL_TILE + t_len, LANES), BF16),
                        pltpu.VMEM((1 + t_len // SEL_TILE, LANES, SEL_TILE), BF16),
                        pltpu.VMEM((WINDOW + t_len, LANES), BF16),
                        pltpu.VMEM((WINDOW + t_len, LANES), BF16),
                        pltpu.VMEM((n_cmp, LANES), BF16),
                        pltpu.VMEM((n_cmp, LANES), F32),
                        pltpu.VMEM((LANES, n_cmp), BF16),
                        pltpu.VMEM((SEL_BIAS_ROWS, N_COLS), F32),
                        pltpu.VMEM((WIN_BAND, N_COLS), F32),
                        pltpu.VMEM((n_cmp, N_COLS), F32),
                        pltpu.VMEM((1, N_COLS), F32),
                        pltpu.VMEM((1, N_COLS), F32),
                        pltpu.VMEM((LANES, N_COLS), F32),
                        pltpu.VMEM((2, 8 + t_len // SEL_BLOCK, LANES), F32),
                        pltpu.VMEM((2, LANES, N_COLS), F32),
                        pltpu.VMEM((2, FAR_GROUP, SEL_TILE, N_COLS), F32),
                        pltpu.VMEM((2, 1, N_COLS), F32)],
        compiler_params=pltpu.CompilerParams(dimension_semantics=("arbitrary", "arbitrary"),
                                             vmem_limit_bytes=VMEM_LIMIT),
        name="nsa_prompt",
    )(rel_bias, nq, nq, kv, win, zg, pool_w2)


CMP_PAGES = 32
CMP_SLOTS = 3
N_PICK = SEL_TOPK - 1
N_OWNERS = N_KV_HEADS * N_NEW
N_FETCH = N_OWNERS * N_PICK
OWNER_KEYS = N_PICK * PAGE_SIZE


def _sample_row_ids():
    r = lax.broadcasted_iota(I32, (N_QROWS, 1), 0)
    return r % SEQ_PAD - (SEQ_PAD - N_NEW), r // SEQ_PAD


def _sample_bias(dist, rb_ref):
    _, head = _sample_row_ids()
    col = lambda k: sum(jnp.where(head == h, rb_ref[k, h], 0.0) for h in range(N_NSA_HEADS))
    acc = jnp.broadcast_to(col(0), dist.shape)
    for k in range(1, N_BUCKETS):
        acc = jnp.where(dist >= BUCKET_THR[k], col(k), acc)
    return acc


def _softmax_two(s1, s2):
    mx = jnp.maximum(jnp.max(s1, -1, keepdims=True), jnp.max(s2, -1, keepdims=True))
    mx = jnp.where(mx > NEG_INF, mx, 0.0)
    e1 = jnp.exp(s1 - mx)
    e2 = jnp.exp(s2 - mx)
    den = jnp.maximum(jnp.sum(e1, -1, keepdims=True) + jnp.sum(e2, -1, keepdims=True), 1e-30)
    return e1 / den, e2 / den


def _nsa_cmp_body(pt_ref, rb_ref, nq_ref, pwt_ref, cache_ref, oc_ref, idx_ref,
                  buf, sem, seg_sc, kct_sc, vct_sc, *, n_pages, past_len):
    b = pl.program_id(0)
    nb = pl.num_programs(0)
    n_chunks = n_pages // CMP_PAGES
    chunk_rows = CMP_PAGES * PAGE_SIZE
    chunk_blocks = chunk_rows // CMP_BLOCK
    n_cmp = past_len // CMP_BLOCK
    n_sel_past = past_len // SEL_BLOCK
    n_prob = N_KV_HEADS * SEQ_PAD
    total = nb * n_chunks
    hi = lax.Precision.HIGHEST

    def page_copy(gi, p):
        page = pt_ref[(gi // n_chunks) * n_pages + (gi % n_chunks) * CMP_PAGES + p]
        slot = gi % CMP_SLOTS
        return pltpu.make_async_copy(cache_ref.at[page, pl.ds(0, 2 * LANES), :], buf.at[slot, p], sem.at[slot])

    def start_chunk(gi):
        for p in range(CMP_PAGES):
            page_copy(gi, p).start()

    def wait_chunk(gi):
        for p in range(CMP_PAGES):
            page_copy(gi, p).wait()

    @pl.when(b == 0)
    def _prologue():
        seg_sc[...] = (lax.broadcasted_iota(I32, (chunk_rows, chunk_blocks), 0) // CMP_BLOCK
                       == lax.broadcasted_iota(I32, (chunk_rows, chunk_blocks), 1)).astype(BF16)
        for gi in range(min(CMP_SLOTS - 1, total)):
            start_chunk(gi)

    def chunk_step(c, carry):
        gi = b * n_chunks + c
        ahead = gi + (CMP_SLOTS - 1)

        @pl.when(ahead < total)
        def _prefetch():
            start_chunk(ahead)

        wait_chunk(gi)
        slot = gi % CMP_SLOTS
        y = jnp.concatenate([(buf[slot, p] * pwt_ref[...]).astype(BF16) for p in range(CMP_PAGES)], axis=1)
        pooled = _dot(y, seg_sc[...])
        kct_sc[c] = pooled[0:LANES].astype(BF16)
        vct_sc[c] = pooled[LANES:2 * LANES].astype(BF16)
        return carry

    lax.fori_loop(0, n_chunks, chunk_step, 0)

    qbd = _block_diag_queries(nq_ref[0], SEQ_PAD)
    tok, _ = _sample_row_ids()
    s_c = jnp.concatenate([_dot(qbd, kct_sc[c]) for c in range(n_chunks)], axis=1)
    near0 = n_cmp - LANES
    assert past_len - (SEQ_PAD - N_NEW) - (CMP_BLOCK * (near0 - 1) + CMP_BLOCK - 1) >= FAR_DIST
    dist = past_len + tok - ((lax.broadcasted_iota(I32, (N_QROWS, LANES), 1) + near0) * CMP_BLOCK + CMP_BLOCK - 1)
    near = jnp.where(dist >= 0, s_c[:, near0:] + _sample_bias(dist, rb_ref), NEG_INF)
    far = s_c[:, :near0] + _sample_bias(jnp.full((N_QROWS, 1), FAR_DIST, I32), rb_ref)
    p_c = _masked_softmax(jnp.concatenate([far, near], axis=1))
    o_c = _dot_nt(p_c[:, 0:chunk_blocks].astype(BF16), vct_sc[0])
    for c in range(1, n_chunks):
        o_c = o_c + _dot_nt(p_c[:, c * chunk_blocks:(c + 1) * chunk_blocks].astype(BF16), vct_sc[c])
    oc_ref[0] = o_c

    p_sum = jnp.sum(p_c.reshape(N_KV_HEADS, GQ, SEQ_PAD, n_cmp), axis=1).reshape(n_prob, n_cmp)
    p_sum = jnp.concatenate([p_sum, jnp.zeros((LANES - n_prob, n_cmp), F32)], axis=0)
    ratio = SEL_BLOCK // CMP_BLOCK
    pair = (lax.broadcasted_iota(I32, (n_sel_past, n_cmp), 1) // ratio
            == lax.broadcasted_iota(I32, (n_sel_past, n_cmp), 0)).astype(F32)
    imp = _dot_nt(pair, p_sum, hi)
    blk = lax.broadcasted_iota(I32, (n_sel_past, LANES), 0)
    pick_i = lax.broadcasted_iota(I32, (SEL_TOPK, LANES), 0)

    def pick(k, carry):
        work, ids = carry
        best = jnp.max(work, 0, keepdims=True)
        first = jnp.min(jnp.where(work == best, blk, n_sel_past), 0, keepdims=True)
        return jnp.where(blk == first, NEG_INF, work), jnp.where(pick_i == k, first, ids)

    _, ids = lax.fori_loop(0, N_PICK, pick, (imp, jnp.zeros((SEL_TOPK, LANES), I32)), unroll=True)
    idx_ref[0] = ids


def _nsa_cmp(page_table, rel_bias, nq_s, pool_wt, cache3):
    dec_b, n_pages = page_table.shape
    past_len = n_pages * PAGE_SIZE
    assert n_pages % CMP_PAGES == 0 and past_len // SEL_BLOCK >= N_PICK
    n_chunks = n_pages // CMP_PAGES
    chunk_rows = CMP_PAGES * PAGE_SIZE
    chunk_blocks = chunk_rows // CMP_BLOCK
    assert chunk_blocks == LANES
    grid_spec = pltpu.PrefetchScalarGridSpec(
        num_scalar_prefetch=1,
        grid=(dec_b,),
        in_specs=[pl.BlockSpec(memory_space=pltpu.SMEM),
                  pl.BlockSpec((1, SEQ_PAD, NSA_W), lambda b, pt: (b, 0, 0)),
                  pl.BlockSpec((2 * LANES, PAGE_SIZE), lambda b, pt: (0, 0)),
                  pl.BlockSpec(memory_space=pl.ANY)],
        out_specs=[pl.BlockSpec((1, N_QROWS, LANES), lambda b, pt: (b, 0, 0)),
                   pl.BlockSpec((1, SEL_TOPK, LANES), lambda b, pt: (b, 0, 0))],
        scratch_shapes=[pltpu.VMEM((CMP_SLOTS, CMP_PAGES, 2 * LANES, PAGE_SIZE), F32),
                        pltpu.SemaphoreType.DMA((CMP_SLOTS,)),
                        pltpu.VMEM((chunk_rows, chunk_blocks), BF16),
                        pltpu.VMEM((n_chunks, LANES, chunk_blocks), BF16),
                        pltpu.VMEM((n_chunks, LANES, chunk_blocks), BF16)],
    )
    return pl.pallas_call(
        functools.partial(_nsa_cmp_body, n_pages=n_pages, past_len=past_len),
        grid_spec=grid_spec,
        out_shape=[jax.ShapeDtypeStruct((dec_b, N_QROWS, LANES), F32),
                   jax.ShapeDtypeStruct((dec_b, SEL_TOPK, LANES), I32)],
        compiler_params=pltpu.CompilerParams(dimension_semantics=("arbitrary",),
                                             vmem_limit_bytes=VMEM_LIMIT),
        name="nsa_sample_cmp",
    )(page_table.reshape(-1), rel_bias, nq_s, pool_wt, cache3)


def _nsa_sel_body(pt_ref, ids_ref, rb_ref, nq_ref, kvn_ref, winc_ref, winn_ref, zg_ref, oc_ref, idv_ref,
                  cache_ref, out_ref, buf, sem, *, n_pages, past_len):
    b = pl.program_id(0)
    nb = pl.num_programs(0)
    hi = lax.Precision.HIGHEST

    def block_copy(bb, n, slot):
        blk = ids_ref[bb * N_FETCH + n]
        page = pt_ref[bb * n_pages + blk // 2]
        return pltpu.make_async_copy(cache_ref.at[page, pl.ds(2 * LANES, 2 * LANES), :], buf.at[slot, n],
                                     sem.at[slot])

    def start_all(bb, slot):
        def go(n, carry):
            block_copy(bb, n, slot).start()
            return carry
        lax.fori_loop(0, N_FETCH, go, 0)

    def wait_all(bb, slot):
        def go(n, carry):
            block_copy(bb, n, slot).wait()
            return carry
        lax.fori_loop(0, N_FETCH, go, 0)

    slot = b % 2

    @pl.when(b == 0)
    def _prologue():
        start_all(0, 0)

    @pl.when(b + 1 < nb)
    def _prefetch():
        start_all(b + 1, 1 - slot)

    qbd = _block_diag_queries(nq_ref[0], SEQ_PAD)
    tok, head = _sample_row_ids()
    owner = (head // GQ) * N_NEW + tok

    wc = winc_ref[0]
    wn = winn_ref[0]
    r_i = lax.broadcasted_iota(I32, (N_QROWS, WINDOW), 1)
    dist1 = WINDOW + tok - r_i
    s1 = _dot(qbd, wc[0:LANES, :].astype(BF16)) + _sample_bias(dist1, rb_ref)
    s1 = jnp.where((dist1 >= 0) & (dist1 <= WINDOW), s1, NEG_INF)
    n_i = lax.broadcasted_iota(I32, (N_QROWS, SEQ_PAD), 1) - (SEQ_PAD - N_NEW)
    dist2 = tok - n_i
    new_ok = (n_i >= 0) & (dist2 >= 0)
    bias2 = _sample_bias(dist2, rb_ref)
    s2 = jnp.where(new_ok, _dot_nt(qbd, wn[:, 0:LANES].astype(BF16)) + bias2, NEG_INF)
    p1, p2 = _softmax_two(s1, s2)
    o_w = (_dot_nt(p1.astype(BF16), wc[LANES:2 * LANES, :].astype(BF16))
           + _dot(p2.astype(BF16), wn[:, LANES:2 * LANES].astype(BF16)))

    wait_all(b, slot)

    def owner_tiles(o, r0):
        return jnp.concatenate([buf[slot, o * N_PICK + k, r0:r0 + LANES, :] for k in range(N_PICK)],
                               axis=1).astype(BF16)

    s_f = jnp.full((N_QROWS, OWNER_KEYS), NEG_INF, F32)
    for o in range(N_OWNERS):
        s_f = jnp.where(owner == o, _dot(qbd, owner_tiles(o, 0)), s_f)
    spread = (lax.broadcasted_iota(I32, (SEL_TOPK, OWNER_KEYS), 0)
              == lax.broadcasted_iota(I32, (SEL_TOPK, OWNER_KEYS), 1) // PAGE_SIZE).astype(F32)
    blk_of_key = _dot_tn(idv_ref[0].astype(F32), spread, hi)
    blk_rows = jnp.concatenate([blk_of_key[(h // GQ) * SEQ_PAD:(h // GQ + 1) * SEQ_PAD]
                                for h in range(N_NSA_HEADS)], axis=0).astype(I32)
    blocks_per_page = PAGE_SIZE // SEL_BLOCK
    in_page = lax.broadcasted_iota(I32, (N_QROWS, OWNER_KEYS), 1) % PAGE_SIZE
    in_block = in_page // SEL_BLOCK == blk_rows % blocks_per_page
    dist_s = past_len + tok - ((blk_rows // blocks_per_page) * PAGE_SIZE + in_page)
    s_f = jnp.where((tok >= 0) & in_block & (dist_s >= 0), s_f + _sample_bias(dist_s, rb_ref), NEG_INF)
    kvn = kvn_ref[0]
    s_n = jnp.where(new_ok, _dot_nt(qbd, kvn[:, 2 * LANES:3 * LANES].astype(BF16)) + bias2, NEG_INF)
    p_f, p_n = _softmax_two(s_f, s_n)
    o_s = _dot(p_n.astype(BF16), kvn[:, 3 * LANES:4 * LANES].astype(BF16))
    for o in range(N_OWNERS):
        o_s = o_s + _dot_nt(jnp.where(owner == o, p_f, 0.0).astype(BF16), owner_tiles(o, LANES))

    o_c = oc_ref[0]
    gates = jax.nn.sigmoid(zg_ref[0])
    is_new = lax.broadcasted_iota(I32, (SEQ_PAD, HEAD_DIM), 0) >= SEQ_PAD - N_NEW
    for h in range(N_NSA_HEADS):
        mix = _gated_mix(gates, o_c, o_s, o_w, h, SEQ_PAD)
        out_ref[0, :, h * HEAD_DIM:(h + 1) * HEAD_DIM] = jnp.where(is_new, mix, 0.0)


def _nsa_sel(page_table, ids, ids_rows, rel_bias, nq_s, kv_s, win_cache, win_s, zg_s, o_c, cache3):
    dec_b, n_pages = page_table.shape
    past_len = n_pages * PAGE_SIZE
    seq = lambda w: pl.BlockSpec((1, SEQ_PAD, w), lambda b, pt, ix: (b, 0, 0))
    grid_spec = pltpu.PrefetchScalarGridSpec(
        num_scalar_prefetch=2,
        grid=(dec_b,),
        in_specs=[pl.BlockSpec(memory_space=pltpu.SMEM),
                  seq(NSA_W), seq(4 * KV_W),
                  pl.BlockSpec((1, 2 * KV_W, WINDOW), lambda b, pt, ix: (b, 0, 0)),
                  seq(2 * KV_W), seq(LANES),
                  pl.BlockSpec((1, N_QROWS, LANES), lambda b, pt, ix: (b, 0, 0)),
                  pl.BlockSpec((1, SEL_TOPK, LANES), lambda b, pt, ix: (b, 0, 0)),
                  pl.BlockSpec(memory_space=pl.ANY)],
        out_specs=seq(NSA_W),
        scratch_shapes=[pltpu.VMEM((2, N_FETCH, 2 * LANES, PAGE_SIZE), F32),
                        pltpu.SemaphoreType.DMA((2,))],
    )
    return pl.pallas_call(
        functools.partial(_nsa_sel_body, n_pages=n_pages, past_len=past_len),
        grid_spec=grid_spec,
        out_shape=jax.ShapeDtypeStruct((dec_b, SEQ_PAD, NSA_W), F32),
        compiler_params=pltpu.CompilerParams(dimension_semantics=("arbitrary",),
                                             vmem_limit_bytes=VMEM_LIMIT),
        name="nsa_sample_sel",
    )(page_table.reshape(-1), ids.reshape(-1), rel_bias, nq_s, kv_s, win_cache, win_s, zg_s, o_c, ids_rows, cache3)


MXU_DIM = 256


def _ff_chunks():
    sizes, left = [], D_FF
    while left > 0:
        sizes.append(min(4 * MXU_DIM, left))
        left -= sizes[-1]
    return tuple(sizes)


FF_CHUNKS = _ff_chunks()
FF_OFFSETS = tuple(sum(FF_CHUNKS[:c]) for c in range(len(FF_CHUNKS)))


def _ff_permute(a):
    parts = []
    for off, w in zip(FF_OFFSETS, FF_CHUNKS):
        parts += [a[..., off:off + w], a[..., D_FF + off:D_FF + off + w]]
    return jnp.concatenate(parts, axis=-1)


def _ff_unpermute(a):
    gate = [a[..., 2 * off:2 * off + w] for off, w in zip(FF_OFFSETS, FF_CHUNKS)]
    value = [a[..., 2 * off + w:2 * off + 2 * w] for off, w in zip(FF_OFFSETS, FF_CHUNKS)]
    return jnp.concatenate(gate + value, axis=-1)


def _ffn_body(*refs, tm, has_override):
    if has_override:
        (x_ref, mm_ref, mn_ref, ov_ref, woa_ref, wob_ref, l1w_ref, l1b_ref, wup_ref, bup_ref, cw_ref, cb_ref,
         wdn_ref, bdn_ref, l2w_ref, l2b_ref, y_ref, tail_ref, carry_sc, ubuf) = refs
    else:
        (x_ref, mm_ref, mn_ref, woa_ref, wob_ref, l1w_ref, l1b_ref, wup_ref, bup_ref, cw_ref, cb_ref,
         wdn_ref, bdn_ref, l2w_ref, l2b_ref, y_ref, tail_ref, carry_sc, ubuf) = refs
    j = pl.program_id(1)

    @pl.when(j == 0)
    def _init():
        carry_sc[...] = jnp.zeros(carry_sc.shape, F32)

    x = x_ref[0]
    h = ALPHA * x + (_dot(mm_ref[0].astype(BF16), woa_ref[...]) + _dot(mn_ref[0].astype(BF16), wob_ref[...]))
    x1 = _layer_norm(h, l1w_ref[...], l1b_ref[...])
    x1b = x1.astype(BF16)
    def conv_chunk(off, w):
        cols = slice(2 * off, 2 * off + 2 * w)
        u = _dot(x1b, wup_ref[:, cols]) + bup_ref[:, cols]
        if has_override:
            state_row = lax.broadcasted_iota(I32, (tm, 2 * w), 0) % SEQ_PAD < SEQ_PAD - N_NEW
            u = jnp.where(state_row, ov_ref[0, :, cols], u)
        ubuf[0:8, 0:2 * w] = carry_sc[:, cols]
        ubuf[8:8 + tm, 0:2 * w] = u
        out = cb_ref[:, cols] + ubuf[6:6 + tm, 0:2 * w] * cw_ref[0:1, cols]
        out = out + ubuf[7:7 + tm, 0:2 * w] * cw_ref[1:2, cols]
        out = out + u * cw_ref[2:3, cols]
        carry_sc[:, cols] = u[tm - 8:tm, :]
        tail_ref[0, 0, :, cols] = u if has_override else u[tm - 8:tm, :]
        return out[:, 0:w], out[:, w:2 * w]

    ff = jnp.zeros((tm, D_MODEL), F32)
    for off, w in zip(FF_OFFSETS, FF_CHUNKS):
        ga, gb = conv_chunk(off, w)
        gelu = ga * (0.5 * (1.0 + jnp.tanh(math.sqrt(2.0 / math.pi) * (ga + 0.044715 * (ga * ga * ga)))))
        ff = ff + _dot((gelu * gb).astype(BF16), wdn_ref[off:off + w, :])
    ff = ff + bdn_ref[...]
    y_ref[0] = _layer_norm(ALPHA * x1 + ff, l2w_ref[...], l2b_ref[...])


def _ffn(x, mix_m, mix_n, override, wts):
    bsz, t_len, _ = x.shape
    if override is not None:
        override = _ff_permute(override)
    tm = min(t_len, 512)
    nt = t_len // tm
    has_override = override is not None
    tail_rows = tm if has_override else 8
    row = lambda w: pl.BlockSpec((1, tm, w), lambda b, j: (b, j, 0))
    const = lambda shape: pl.BlockSpec(shape, lambda b, j: (0,) * len(shape), pipeline_mode=pl.Buffered(1))
    in_specs = [row(D_MODEL), row(MLSTM_W), row(NSA_W)]
    args = [x, mix_m, mix_n]
    if has_override:
        in_specs.append(row(2 * D_FF))
        args.append(override)
    in_specs += [const(w.shape) for w in wts]
    args += list(wts)
    y, tail = pl.pallas_call(
        functools.partial(_ffn_body, tm=tm, has_override=has_override),
        grid=(bsz, nt),
        in_specs=in_specs,
        out_specs=[row(D_MODEL), pl.BlockSpec((1, 1, tail_rows, 2 * D_FF), lambda b, j: (b, j, 0, 0))],
        out_shape=[jax.ShapeDtypeStruct((bsz, t_len, D_MODEL), F32),
                   jax.ShapeDtypeStruct((bsz, nt, tail_rows, 2 * D_FF), F32)],
        scratch_shapes=[pltpu.VMEM((8, 2 * D_FF), F32), pltpu.VMEM((8 + tm, 2 * max(FF_CHUNKS)), F32)],
        compiler_params=pltpu.CompilerParams(dimension_semantics=("arbitrary", "arbitrary"),
                                             vmem_limit_bytes=VMEM_LIMIT),
        name="ffn",
    )(*args)
    return y, _ff_unpermute(tail)


def _permute_in_proj(w_in, b_in):
    gates_a = slice(2 * MLSTM_W + 2 * MLSTM_W, 2 * MLSTM_W + 2 * MLSTM_W + 2 * N_MLSTM_HEADS)
    nsa0 = gates_a.stop
    nsa1 = nsa0 + NSA_W + 6 * KV_W

    def perm(a):
        pad = IN_PAD - a.shape[-1]
        parts = [a[..., :gates_a.start], a[..., nsa0:nsa1], a[..., gates_a], a[..., nsa1:]]
        parts.append(jnp.zeros(a.shape[:-1] + (pad,), a.dtype))
        return jnp.concatenate(parts, axis=-1)

    return perm(w_in).astype(BF16), perm(b_in).reshape(1, IN_PAD)


def _front_pad(a, n):
    return jnp.pad(a, ((0, 0), (n, 0), (0, 0)))


def kernel(x_prompt, x_sample, cache_nsa_kv, cache_win_kv, state_mlstm_c, state_mlstm_n, state_mlstm_m,
           state_mlstm_conv, state_ffn_conv, page_table,
           w_in, b_in, mlstm_conv_w, mlstm_conv_b, mlstm_norm_w, nsa_pool_w, rel_bias, w_out,
           ln1_w, ln1_b, w_up, b_up, ffn_conv_w, ffn_conv_b, w_down, b_down, ln2_w, ln2_b):
    bsz, t_len, _ = x_prompt.shape
    dec_b, n_new, _ = x_sample.shape
    assert n_new == N_NEW and cache_win_kv.shape[1] == WINDOW

    w_p, b_p = _permute_in_proj(w_in, b_in)
    pool_w2 = jnp.concatenate([jnp.repeat(nsa_pool_w[0], HEAD_DIM, axis=-1),
                               jnp.repeat(nsa_pool_w[1], HEAD_DIM, axis=-1)], axis=-1)
    row = lambda v: v.reshape(1, -1)
    ffn_w = (w_out[:MLSTM_W].astype(BF16), w_out[MLSTM_W:].astype(BF16), row(ln1_w), row(ln1_b),
             _ff_permute(w_up.astype(BF16)), row(_ff_permute(b_up)), _ff_permute(ffn_conv_w),
             row(_ff_permute(ffn_conv_b)), w_down.astype(BF16), row(b_down), row(ln2_w), row(ln2_b))

    zqk, zv, zo, nq, kv, win, zg, kv_t = _proj_in(x_prompt.reshape(bsz * t_len, D_MODEL), w_p, b_p, seq_len=t_len)
    seq3 = lambda a: a.reshape(bsz, t_len, a.shape[-1])
    zqk, zv, zo, nq, kv, win, zg = map(seq3, (zqk, zv, zo, nq, kv, win, zg))
    mix_m, c_p, n_p, m_p = _mlstm(zqk, zv, zo, zg, None, mlstm_conv_w, mlstm_conv_b, mlstm_norm_w, 0)
    mix_n = _nsa_prompt(nq, kv, win, zg, pool_w2, rel_bias)
    y_p, tail_p = _ffn(x_prompt, mix_m, mix_n, None, ffn_w)
    kv_p = jnp.transpose(kv_t.reshape(bsz, 4, N_KV_HEADS, HEAD_DIM, t_len), (0, 4, 1, 2, 3))
    win_p = win[:, t_len - min(WINDOW, t_len):].reshape(bsz, -1, 2, N_KV_HEADS, HEAD_DIM)
    mconv_p = zqk[:, t_len - (MLSTM_CONV - 1):]
    fconv_p = tail_p[:, -1, 8 - (FFN_CONV - 1):]

    pad = SEQ_PAD - N_NEW
    xs = _front_pad(x_sample, pad)
    szqk, szv, szo, snq, skv, swin, szg = _proj_in(xs.reshape(dec_b * SEQ_PAD, D_MODEL), w_p, b_p)
    sseq = lambda a: a.reshape(dec_b, SEQ_PAD, a.shape[-1])
    szqk, szv, szo, snq, skv, swin, szg = map(sseq, (szqk, szv, szo, snq, skv, swin, szg))
    lead = MLSTM_CHUNK - N_NEW
    new = lambda a: a[:, pad:]
    m_qk = jnp.concatenate([jnp.zeros((dec_b, lead - (MLSTM_CONV - 1), 2 * MLSTM_W), F32),
                            state_mlstm_conv.astype(F32), new(szqk)], axis=1)
    mix_ms, c_s, n_s, m_s = _mlstm(m_qk, _front_pad(new(szv), lead), _front_pad(new(szo), lead),
                                   _front_pad(new(szg), lead),
                                   (state_mlstm_c, state_mlstm_n, state_mlstm_m),
                                   mlstm_conv_w, mlstm_conv_b, mlstm_norm_w, lead)
    mix_ms = mix_ms[:, MLSTM_CHUNK - SEQ_PAD:]
    cache3 = jnp.transpose(cache_nsa_kv, (0, 2, 3, 4, 1)).reshape(cache_nsa_kv.shape[0], 4 * KV_W, PAGE_SIZE)
    pool_wt = jnp.tile(jnp.repeat(jnp.transpose(nsa_pool_w, (0, 2, 1)).reshape(2 * N_KV_HEADS, CMP_BLOCK),
                                  HEAD_DIM, axis=0), (1, PAGE_SIZE // CMP_BLOCK))
    o_c, ids_rows = _nsa_cmp(page_table, rel_bias, snq, pool_wt, cache3)
    ids = jnp.transpose(ids_rows[:, :N_PICK, :N_KV_HEADS * SEQ_PAD], (0, 2, 1))
    ids = ids.reshape(dec_b, N_KV_HEADS, SEQ_PAD, N_PICK)[:, :, pad:]
    win_cache = jnp.transpose(cache_win_kv, (0, 2, 3, 4, 1)).reshape(dec_b, 2 * KV_W, WINDOW)
    mix_ns = _nsa_sel(page_table, ids, ids_rows, rel_bias, snq, skv, win_cache, swin, szg, o_c, cache3)
    override = jnp.concatenate([jnp.zeros((dec_b, pad - (FFN_CONV - 1), 2 * D_FF), F32),
                                state_ffn_conv.astype(F32),
                                jnp.zeros((dec_b, N_NEW, 2 * D_FF), F32)], axis=1)
    flat = lambda a: a.reshape(1, dec_b * SEQ_PAD, a.shape[-1])
    y_s8, u_s = _ffn(flat(xs), flat(mix_ms), flat(mix_ns), flat(override), ffn_w)
    y_s = y_s8.reshape(dec_b, SEQ_PAD, D_MODEL)[:, pad:]
    kv_s = new(skv).reshape(dec_b, N_NEW, 4, N_KV_HEADS, HEAD_DIM)
    win_s = jnp.concatenate([cache_win_kv[:, N_NEW:], new(swin).reshape(dec_b, N_NEW, 2, N_KV_HEADS, HEAD_DIM)],
                            axis=1)
    mconv_s = szqk[:, SEQ_PAD - (MLSTM_CONV - 1):]
    fconv_s = u_s.reshape(dec_b, SEQ_PAD, 2 * D_FF)[:, SEQ_PAD - (FFN_CONV - 1):]
    return (y_p, y_s, kv_p, kv_s, win_p, win_s, c_p, c_s, n_p, n_s, m_p, m_s, mconv_p, mconv_s, fconv_p, fconv_s)
```

```python
import functools
import math

import jax
import jax.numpy as jnp
from jax import lax
from jax.experimental import pallas as pl
from jax.experimental.pallas import tpu as pltpu

F32 = jnp.float32
BF16 = jnp.bfloat16
I32 = jnp.int32

D_MODEL = 1024
HEAD_DIM = 64
N_MLSTM_HEADS = 8
N_NSA_HEADS = 8
N_KV_HEADS = 2
GQ = N_NSA_HEADS // N_KV_HEADS
MLSTM_W = N_MLSTM_HEADS * HEAD_DIM
NSA_W = N_NSA_HEADS * HEAD_DIM
KV_W = N_KV_HEADS * HEAD_DIM
MLSTM_CHUNK = 64
MLSTM_CONV = 4
CMP_BLOCK = 32
SEL_BLOCK = 64
SEL_TOPK = 16
WINDOW = 512
NSA_QBLOCK = 64
N_BUCKETS = 32
MAX_DISTANCE = 128
D_FF = 2688
FFN_CONV = 3
DEPTH = 1
ALPHA = (2.0 * DEPTH) ** 0.25
LN_EPS = 1e-5
PAGE_SIZE = 128

LANES = 128
VMEM_LIMIT = 56 * 1024 * 1024
NEG_INF = float("-inf")

SEQ_PAD = 8
N_NEW = 4
N_QROWS = N_NSA_HEADS * SEQ_PAD


def _bucket_thresholds():
    max_exact = N_BUCKETS // 2
    thr = list(range(max_exact + 1))
    for k in range(1, N_BUCKETS - max_exact):
        thr.append(int(math.ceil(max_exact * (MAX_DISTANCE / max_exact) ** (k / (N_BUCKETS - max_exact)))))
    return tuple(thr)


BUCKET_THR = _bucket_thresholds()
FAR_DIST = BUCKET_THR[N_BUCKETS - 1]


def _dot(a, b, precision=None):
    return jnp.dot(a, b, preferred_element_type=F32, precision=precision)


def _dot_nt(a, b, precision=None):
    return lax.dot_general(a, b, (((1,), (1,)), ((), ())), preferred_element_type=F32, precision=precision)


def _dot_tn(a, b, precision=None):
    return lax.dot_general(a, b, (((0,), (0,)), ((), ())), preferred_element_type=F32, precision=precision)


def _layer_norm(x, w, b):
    mu = jnp.mean(x, -1, keepdims=True)
    xc = x - mu
    var = jnp.mean(xc * xc, -1, keepdims=True)
    return xc * lax.rsqrt(var + LN_EPS) * w + b


def _log_sigmoid(x):
    return jnp.minimum(x, 0.0) - jnp.log1p(jnp.exp(-jnp.abs(x)))


def _bias_from_dist(dist, rb_ref, head):
    acc = jnp.full(dist.shape, rb_ref[0, head], F32)
    for k in range(1, N_BUCKETS):
        acc = jnp.where(dist >= BUCKET_THR[k], rb_ref[k, head], acc)
    return acc


def _masked_softmax(s):
    mx = jnp.max(s, -1, keepdims=True)
    mx = jnp.where(mx > NEG_INF, mx, 0.0)
    e = jnp.exp(s - mx)
    den = jnp.sum(e, -1, keepdims=True)
    return e / jnp.maximum(den, 1e-30)


_C_QK, _C_V, _C_O, _C_NQ, _C_KV, _C_WIN, _C_G = 0, 1024, 1536, 2048, 2560, 3072, 3328
IN_PAD = 3456
G_IGATE, G_FGATE, G_NSA = 0, N_MLSTM_HEADS, 2 * N_MLSTM_HEADS


def _proj_in_body(*refs, kv_transposed):
    if kv_transposed:
        x_ref, w_ref, b_ref, wkvt_ref, bkv_ref, qk_ref, v_ref, o_ref, nq_ref, kv_ref, win_ref, g_ref, kvt_ref = refs
    else:
        x_ref, w_ref, b_ref, qk_ref, v_ref, o_ref, nq_ref, kv_ref, win_ref, g_ref = refs
    xb = x_ref[...].astype(BF16)
    z = _dot(xb, w_ref[...]) + b_ref[...]
    qk_ref[...] = z[:, _C_QK:_C_V]
    v_ref[...] = z[:, _C_V:_C_O]
    o_ref[...] = z[:, _C_O:_C_NQ]
    nq_ref[...] = z[:, _C_NQ:_C_KV]
    kv_ref[...] = z[:, _C_KV:_C_WIN]
    win_ref[...] = z[:, _C_WIN:_C_G]
    g_ref[...] = z[:, _C_G:IN_PAD]
    if kv_transposed:
        kvt_ref[0] = _dot_nt(wkvt_ref[...], xb) + bkv_ref[...]


def _proj_in(x2d, w_p, b_p, seq_len=None):
    rows = x2d.shape[0]
    tm = min(rows, 256)
    widths = (1024, 512, 512, 512, 512, 256, 128)
    const = lambda i: (0, 0)
    const_spec = lambda shape: pl.BlockSpec(shape, const, pipeline_mode=pl.Buffered(1))
    in_specs = [pl.BlockSpec((tm, D_MODEL), lambda i: (i, 0)), const_spec((D_MODEL, IN_PAD)), const_spec((1, IN_PAD))]
    args = [x2d, w_p, b_p]
    out_specs = [pl.BlockSpec((tm, w), lambda i: (i, 0)) for w in widths]
    out_shape = [jax.ShapeDtypeStruct((rows, w), F32) for w in widths]
    if seq_len is not None:
        tiles = seq_len // tm
        in_specs += [const_spec((4 * KV_W, D_MODEL)), const_spec((4 * KV_W, 1))]
        args += [jnp.transpose(w_p[:, _C_KV:_C_WIN]), b_p[0, _C_KV:_C_WIN].reshape(4 * KV_W, 1)]
        out_specs.append(pl.BlockSpec((1, 4 * KV_W, tm), lambda i: (i // tiles, 0, i % tiles)))
        out_shape.append(jax.ShapeDtypeStruct((rows // seq_len, 4 * KV_W, seq_len), F32))
    return pl.pallas_call(
        functools.partial(_proj_in_body, kv_transposed=seq_len is not None),
        grid=(rows // tm,),
        in_specs=in_specs,
        out_specs=out_specs,
        out_shape=out_shape,
        compiler_params=pltpu.CompilerParams(dimension_semantics=("arbitrary",),
                                             vmem_limit_bytes=VMEM_LIMIT),
        name="proj_in",
    )(*args)


N_PAIRS = N_MLSTM_HEADS // 2
GATE_LANE0 = N_MLSTM_HEADS
MLSTM_SEQS_PER_STEP = 4


def _dot_split(x, onehot_bf16, pieces):
    acc = None
    rem = x
    for n in range(pieces):
        part = rem.astype(BF16)
        d = _dot(part, onehot_bf16)
        acc = d if acc is None else acc + d
        if n + 1 < pieces:
            rem = rem - part.astype(F32)
    return acc


def _mlstm_body(*refs, tb, nbg, n_dummy, has_state):
    if has_state:
        (zqk_ref, zv_ref, zo_ref, zg_ref, c0_ref, n0_ref, m0_ref, cw_ref, cb_ref, nw_ref,
         mix_ref, c_out, n_out, m_out, ubuf, qk_sc, ct_sc, n_sc, m_sc) = refs
    else:
        (zqk_ref, zv_ref, zo_ref, zg_ref, cw_ref, cb_ref, nw_ref,
         mix_ref, c_out, n_out, m_out, ubuf, qk_sc, ct_sc, n_sc, m_sc) = refs
    L = MLSTM_CHUNK
    D = HEAD_DIM
    j = pl.program_id(1)
    hi = lax.Precision.HIGHEST
    head_lanes = slice(GATE_LANE0, GATE_LANE0 + N_MLSTM_HEADS)
    pairs = [(bi, p) for bi in range(nbg) for p in range(N_PAIRS)]

    @pl.when(j == 0)
    def _init():
        ubuf[:, 0:8, :] = jnp.zeros((nbg, 8, 2 * MLSTM_W), F32)
        ct_sc[...] = jnp.zeros(ct_sc.shape, F32)
        n_sc[...] = jnp.zeros(n_sc.shape, F32)
        m_sc[...] = jnp.zeros(m_sc.shape, F32)
        if has_state:
            for bi in range(nbg):
                for h in range(N_MLSTM_HEADS):
                    p, hh = divmod(h, 2)
                    hs = slice(hh * D, (hh + 1) * D)
                    ct_sc[bi, p, hs, hs] = c0_ref[bi, h]
                    n_sc[bi, p, :, hs] = n0_ref[bi, h:h + 1, :]
                m_sc[bi, :, head_lanes] = m0_ref[bi]

    for bi in range(nbg):
        u = zqk_ref[bi]
        ubuf[bi, 8:8 + tb, :] = u
        conv = cb_ref[...] + ubuf[bi, 5:5 + tb, :] * cw_ref[0:1, :]
        conv = conv + ubuf[bi, 6:6 + tb, :] * cw_ref[1:2, :]
        conv = conv + ubuf[bi, 7:7 + tb, :] * cw_ref[2:3, :]
        conv = conv + u * cw_ref[3:4, :]
        ubuf[bi, 5:8, :] = u[tb - 3:tb, :]
        qk_sc[bi] = conv * jax.nn.sigmoid(conv)

    tri = (lax.broadcasted_iota(I32, (L, L), 1) <= lax.broadcasted_iota(I32, (L, L), 0)).astype(F32)
    row_l = lax.broadcasted_iota(I32, (L, LANES), 0)
    lane = lax.broadcasted_iota(I32, (L, LANES), 1)
    low = lane < D
    key_tok = lane % D
    causal_pair = key_tok <= row_l
    eye_pair = key_tok == row_l
    spread = (lax.broadcasted_iota(I32, (LANES, MLSTM_W), 0) - GATE_LANE0
              == lax.broadcasted_iota(I32, (LANES, MLSTM_W), 1) // D).astype(BF16)
    same_head = (lax.broadcasted_iota(I32, (LANES, LANES), 0) // D
                 == lax.broadcasted_iota(I32, (LANES, LANES), 1) // D)
    ones_bd = same_head.astype(BF16)
    slab_rows = 3 * L + 8

    def chunk(ci, carry):
        r0 = pl.multiple_of(ci * L, L)
        rows = pl.ds(r0, L)
        gates = []
        for bi in range(nbg):
            g = zg_ref[bi, rows, :]
            if n_dummy:
                dummy = (row_l + ci * L) < n_dummy
                g = jnp.where(dummy, jnp.where(lane < G_FGATE, -1e30, 1e30), g)
            gates.append(g)
        b_all = _dot(tri, jnp.concatenate([_log_sigmoid(g) for g in gates], axis=1), hi)
        stacks = []
        for bi in range(nbg):
            b = b_all[:, bi * LANES:(bi + 1) * LANES]
            r = pltpu.roll(gates[bi], GATE_LANE0 - G_IGATE, 1) - b
            cm = r
            for sh in (1, 2, 4, 8, 16, 32):
                cm = jnp.where(row_l >= sh, jnp.maximum(cm, pltpu.roll(cm, sh, 0)), cm)
            m_st = m_sc[bi]
            mx = jnp.maximum(m_st, cm)
            mx_end = mx[L - 1:L, :]
            m_sc[bi] = b[L - 1:L, :] + mx_end
            stacks += [mx, r, b + mx, m_st, mx_end, jnp.zeros((6, LANES), F32)]
        x_all = _dot_split(jnp.concatenate(stacks, axis=0), spread, 2)

        work = []
        den_terms = []
        for bi, p in pairs:
            cs = slice(p * LANES, (p + 1) * LANES)
            x0 = bi * slab_rows
            mx_p, rs_p, mt_p = x_all[x0:x0 + L, cs], x_all[x0 + L:x0 + 2 * L, cs], x_all[x0 + 2 * L:x0 + 3 * L, cs]
            mst_p, mxe_p = x_all[x0 + 3 * L:x0 + 3 * L + 1, cs], x_all[x0 + 3 * L + 1:x0 + 3 * L + 2, cs]
            wi_p = jnp.exp(mst_p - mx_p)
            fl_p = jnp.exp(-mt_p)
            wk_p = jnp.exp(rs_p - mxe_p)
            sc_p = jnp.exp(mst_p - mxe_p)
            r_row = jnp.sum(jnp.where(eye_pair, rs_p, 0.0), axis=0, keepdims=True)
            w_intra = jnp.exp(jnp.where(causal_pair, r_row - mx_p, NEG_INF))
            q_p = qk_sc[bi, rows, cs]
            k_p = qk_sc[bi, rows, MLSTM_W + p * LANES:MLSTM_W + (p + 1) * LANES] * (D ** -0.5)
            v_p = zv_ref[bi, rows, cs]
            qb = q_p.astype(BF16)
            k_bd = jnp.concatenate([jnp.where(low, k_p, 0.0), jnp.where(low, 0.0, k_p)], axis=0)
            v_bd = jnp.concatenate([jnp.where(low, v_p, 0.0), jnp.where(low, 0.0, v_p)], axis=0).astype(BF16)
            c_t = ct_sc[bi, p]
            both = _dot_nt(qb, jnp.concatenate([k_bd, c_t], axis=0).astype(BF16))
            a = w_intra * both[:, 0:LANES]
            num = _dot(a.astype(BF16), v_bd) + both[:, LANES:2 * LANES] * wi_p
            n_row = n_sc[bi, p]
            den_terms.append(a + wi_p * (q_p * n_row))
            work.append((num, fl_p))
            kw = k_p * wk_p
            upd = _dot_tn(v_p.astype(BF16), kw.astype(BF16))
            ct_sc[bi, p] = sc_p * c_t + jnp.where(same_head, upd, 0.0)
            n_sc[bi, p] = sc_p * n_row + jnp.sum(kw, axis=0, keepdims=True)
        den_all = _dot_split(jnp.concatenate(den_terms, axis=0), ones_bd, 2)

        hhs = []
        for n, (num, fl_p) in enumerate(work):
            hhs.append(num / jnp.maximum(jnp.abs(den_all[n * L:(n + 1) * L, :]), fl_p))
        mu_all = _dot_split(jnp.concatenate(hhs, axis=0), ones_bd, 1) * (1.0 / D)
        hcs = [hh - mu_all[n * L:(n + 1) * L, :] for n, hh in enumerate(hhs)]
        var_all = _dot_split(jnp.concatenate([hc * hc for hc in hcs], axis=0), ones_bd, 1) * (1.0 / D)
        for n, (bi, p) in enumerate(pairs):
            cs = slice(p * LANES, (p + 1) * LANES)
            hn = hcs[n] * lax.rsqrt(var_all[n * L:(n + 1) * L, :] + LN_EPS)
            og = jax.nn.sigmoid(zo_ref[bi, rows, cs])
            mix_ref[bi, rows, cs] = og * (hn * nw_ref[:, cs])
        return carry

    lax.fori_loop(0, tb // L, chunk, 0, unroll=2 if (tb // L) % 2 == 0 else 1)

    @pl.when(j == pl.num_programs(1) - 1)
    def _final():
        for bi in range(nbg):
            for h in range(N_MLSTM_HEADS):
                p, hh = divmod(h, 2)
                hs = slice(hh * D, (hh + 1) * D)
                c_out[bi, h] = ct_sc[bi, p, hs, hs]
                n_out[bi, h:h + 1, :] = n_sc[bi, p, :, hs]
            m_out[bi] = m_sc[bi, :, head_lanes]


def _mlstm(zqk, zv, zo, zg, state, conv_w, conv_b, norm_w, n_dummy):
    bsz, t_len, _ = zqk.shape
    tb = min(t_len, 256)
    nbg = MLSTM_SEQS_PER_STEP
    assert bsz % nbg == 0
    has_state = state is not None
    row = lambda w: pl.BlockSpec((nbg, tb, w), lambda b, j: (b, j, 0))
    per_b = lambda shape: pl.BlockSpec((nbg,) + shape, lambda b, j: (b,) + (0,) * len(shape))
    const = lambda shape: pl.BlockSpec(shape, lambda b, j: (0,) * len(shape))
    in_specs = [row(2 * MLSTM_W), row(MLSTM_W), row(MLSTM_W), row(LANES)]
    args = [zqk, zv, zo, zg]
    if has_state:
        c0, n0, m0 = state
        in_specs += [per_b((N_MLSTM_HEADS, HEAD_DIM, HEAD_DIM)), per_b((N_MLSTM_HEADS, HEAD_DIM)),
                     per_b((1, N_MLSTM_HEADS))]
        args += [c0, n0, m0.reshape(bsz, 1, N_MLSTM_HEADS)]
    in_specs += [const((MLSTM_CONV, 2 * MLSTM_W)), const((1, 2 * MLSTM_W)), const((1, MLSTM_W))]
    args += [conv_w, conv_b.reshape(1, -1), norm_w.reshape(1, -1)]
    mix, c_new, n_new, m_new = pl.pallas_call(
        functools.partial(_mlstm_body, tb=tb, nbg=nbg, n_dummy=n_dummy, has_state=has_state),
        grid=(bsz // nbg, t_len // tb),
        in_specs=in_specs,
        out_specs=[row(MLSTM_W), per_b((N_MLSTM_HEADS, HEAD_DIM, HEAD_DIM)),
                   per_b((N_MLSTM_HEADS, HEAD_DIM)), per_b((1, N_MLSTM_HEADS))],
        out_shape=[jax.ShapeDtypeStruct((bsz, t_len, MLSTM_W), F32),
                   jax.ShapeDtypeStruct((bsz, N_MLSTM_HEADS, HEAD_DIM, HEAD_DIM), F32),
                   jax.ShapeDtypeStruct((bsz, N_MLSTM_HEADS, HEAD_DIM), F32),
                   jax.ShapeDtypeStruct((bsz, 1, N_MLSTM_HEADS), F32)],
        scratch_shapes=[pltpu.VMEM((nbg, 8 + tb, 2 * MLSTM_W), F32),
                        pltpu.VMEM((nbg, tb, 2 * MLSTM_W), F32),
                        pltpu.VMEM((nbg, N_PAIRS, LANES, LANES), F32),
                        pltpu.VMEM((nbg, N_PAIRS, 1, LANES), F32),
                        pltpu.VMEM((nbg, 1, LANES), F32)],
        compiler_params=pltpu.CompilerParams(dimension_semantics=("arbitrary", "arbitrary"),
                                             vmem_limit_bytes=VMEM_LIMIT),
        name="mlstm",
    )(*args)
    return mix, c_new, n_new, m_new.reshape(bsz, N_MLSTM_HEADS)


def _block_diag_queries(x, n_rows, heads=tuple(range(N_NSA_HEADS)), scale=HEAD_DIM ** -0.5):
    lane = lax.broadcasted_iota(I32, (n_rows, LANES), 1)
    low = lane < HEAD_DIM
    pieces = []
    for h in heads:
        pair = x[:, (h // 2) * LANES:(h // 2 + 1) * LANES]
        src_low = h % 2 == 0
        dst_low = h // GQ == 0
        if src_low != dst_low:
            pair = pltpu.roll(pair, HEAD_DIM, 1)
        pieces.append(jnp.where(low if dst_low else ~low, pair, 0.0))
    return (jnp.concatenate(pieces, axis=0) * scale).astype(BF16)


def _group_lanes(o, g):
    return o[:, g * HEAD_DIM:(g + 1) * HEAD_DIM]


def _gated_mix(gates, o_c, o_s, o_w, h, n_rows):
    g = h // GQ
    rs = slice(h * n_rows, (h + 1) * n_rows)
    col = lambda branch: gates[:, G_NSA + branch * N_NSA_HEADS + h:G_NSA + branch * N_NSA_HEADS + h + 1]
    return (col(0) * _group_lanes(o_c[rs], g) + col(1) * _group_lanes(o_s[rs], g)
            + col(2) * _group_lanes(o_w[rs], g))


SEL_TILE = 256
SEL_TILE_BLOCKS = SEL_TILE // SEL_BLOCK
FAR_GROUP = 2
WIN_BAND = WINDOW + NSA_QBLOCK
N_COLS = N_NSA_HEADS * NSA_QBLOCK
PROMPT_HEADS = tuple(GQ * (jj % N_KV_HEADS) + jj // N_KV_HEADS for jj in range(N_NSA_HEADS))
SEL_REL0 = 2 * SEL_TILE - SEL_BLOCK
SEL_BIAS_ROWS = SEL_REL0 + SEL_TILE


LOG2E = math.log2(math.e)


def _softmax2_rows(s):
    mx = jnp.max(s, 0, keepdims=True)
    mx = jnp.where(mx > NEG_INF, mx, 0.0)
    e = jnp.exp2(s - mx)
    return e / jnp.maximum(jnp.sum(e, 0, keepdims=True), 1e-30)


def _nsa_prompt_body(rb_ref, nq_ref, nqn_ref, kv_ref, win_ref, zg_ref, pw_ref, out_ref,
                     ks_sc, vst_sc, wk_sc, wv_sc, kc_sc, pv_sc, vct_sc, bs_sc, bw_sc, fc_sc,
                     m_sc, l_sc, acc_sc, ch_sc, oc_sc, fs_sc, ft_sc, *, t_len):
    QB = NSA_QBLOCK
    n_cmp = t_len // CMP_BLOCK
    n_sel = t_len // SEL_BLOCK
    i = pl.program_id(1)
    hi = lax.Precision.HIGHEST

    def cmp_scores(x_q):
        return _dot_nt(kc_sc[...], _block_diag_queries(x_q, QB, PROMPT_HEADS, HEAD_DIM ** -0.5 * LOG2E))

    def cmp_and_select(raw, qi, out_buf):
        s_c = raw + pltpu.roll(fc_sc[...], 2 * qi, 0)
        c_i = lax.broadcasted_iota(I32, (n_cmp, N_COLS), 0)
        q_i = lax.broadcasted_iota(I32, (n_cmp, N_COLS), 1) % QB
        p_c = _softmax2_rows(jnp.where(c_i * CMP_BLOCK + (CMP_BLOCK - 1) <= qi * QB + q_i, s_c, NEG_INF))
        oc_sc[out_buf] = _dot(vct_sc[...], p_c.astype(BF16))
        p_sum = p_c[:, 0:LANES]
        for j in range(1, GQ):
            p_sum = p_sum + p_c[:, j * LANES:(j + 1) * LANES]
        ratio = SEL_BLOCK // CMP_BLOCK
        pair = (lax.broadcasted_iota(I32, (n_sel, n_cmp), 1) // ratio
                == lax.broadcasted_iota(I32, (n_sel, n_cmp), 0)).astype(F32)
        imp = _dot(pair, p_sum, hi)
        blk = lax.broadcasted_iota(I32, (n_sel, LANES), 0)
        score = jnp.where(blk == qi, jnp.inf, jnp.where(blk < qi, imp, NEG_INF))

        def pick(_, carry):
            work, chosen = carry
            best = jnp.max(work, 0, keepdims=True)
            first = jnp.min(jnp.where(work == best, blk, n_sel), 0, keepdims=True)
            hit = blk == first
            return jnp.where(hit, NEG_INF, work), jnp.where(hit, 1.0, chosen)

        _, chosen = lax.fori_loop(0, SEL_TOPK, pick, (score, jnp.zeros((n_sel, LANES), F32)), unroll=True)
        ch_sc[out_buf, 0:8, :] = jnp.full((8, LANES), NEG_INF, F32)
        ch_sc[out_buf, 8:8 + n_sel, :] = jnp.where((score > NEG_INF) & (chosen > 0.5), 0.0, NEG_INF)

    @pl.when(i == 0)
    def _init():
        ks_sc[0:SEL_TILE, :] = jnp.zeros((SEL_TILE, LANES), BF16)
        vst_sc[0] = jnp.zeros((LANES, SEL_TILE), BF16)
        wk_sc[0:WINDOW, :] = jnp.zeros((WINDOW, LANES), BF16)
        wv_sc[0:WINDOW, :] = jnp.zeros((WINDOW, LANES), BF16)
        rows_per = 2 * SEL_TILE
        blocks_per = rows_per // CMP_BLOCK

        def fill(c, carry):
            r0 = pl.multiple_of(c * rows_per, rows_per)
            kv = kv_ref[0, pl.ds(r0, rows_per), :]
            ks_sc[pl.ds(SEL_TILE + r0, rows_per), :] = kv[:, 2 * LANES:3 * LANES].astype(BF16)
            for half in range(2):
                v_t = kv[half * SEL_TILE:(half + 1) * SEL_TILE, 3 * LANES:4 * LANES]
                vst_sc[2 * c + half + 1] = jnp.transpose(v_t).astype(BF16)
            w = win_ref[0, pl.ds(r0, rows_per), :]
            wk_sc[pl.ds(WINDOW + r0, rows_per), :] = w[:, 0:LANES].astype(BF16)
            wv_sc[pl.ds(WINDOW + r0, rows_per), :] = w[:, LANES:2 * LANES].astype(BF16)
            pooled = jnp.sum(kv[:, 0:2 * LANES].reshape(blocks_per, CMP_BLOCK, 2 * LANES) * pw_ref[...][None],
                             axis=1)
            c0 = pl.multiple_of(c * blocks_per, blocks_per)
            kc_sc[pl.ds(c0, blocks_per), :] = pooled[:, 0:LANES].astype(BF16)
            pv_sc[pl.ds(c0, blocks_per), :] = pooled[:, LANES:2 * LANES]
            return carry

        lax.fori_loop(0, t_len // rows_per, fill, 0)
        vct_sc[...] = jnp.transpose(pv_sc[...]).astype(BF16)

        for jj, h in enumerate(PROMPT_HEADS):
            cs = slice(jj * QB, (jj + 1) * QB)
            u_i = lax.broadcasted_iota(I32, (SEL_BIAS_ROWS, QB), 0)
            q_i = lax.broadcasted_iota(I32, (SEL_BIAS_ROWS, QB), 1)
            dist = q_i - (u_i - SEL_REL0)
            near = (_bias_from_dist(dist, rb_ref, h) - rb_ref[N_BUCKETS - 1, h]) * LOG2E
            bs_sc[:, cs] = jnp.where(dist >= 0, near, NEG_INF)
            s_i = lax.broadcasted_iota(I32, (WIN_BAND, QB), 0)
            q_i = lax.broadcasted_iota(I32, (WIN_BAND, QB), 1)
            dist = q_i + WINDOW - s_i
            bw_sc[:, cs] = jnp.where((dist >= 0) & (dist <= WINDOW), _bias_from_dist(dist, rb_ref, h) * LOG2E,
                                     NEG_INF)
            c_i = lax.broadcasted_iota(I32, (n_cmp, QB), 0)
            q_i = lax.broadcasted_iota(I32, (n_cmp, QB), 1)
            rel = jnp.where(c_i < n_cmp // 2, c_i, c_i - n_cmp)
            dist = q_i - CMP_BLOCK * rel - (CMP_BLOCK - 1)
            dist = jnp.where((c_i >= 2) & (c_i < n_cmp // 2), FAR_DIST, dist)
            fc_sc[:, cs] = _bias_from_dist(dist, rb_ref, h) * LOG2E

        cmp_and_select(cmp_scores(nq_ref[0]), 0, 0)

    buf = i % 2
    o_c = oc_sc[buf]
    qbd = _block_diag_queries(nq_ref[0], QB, PROMPT_HEADS, HEAD_DIM ** -0.5 * LOG2E)
    last = i // SEL_TILE_BLOCKS

    def tile_raw(slot):
        return _dot_nt(ks_sc[pl.ds(pl.multiple_of(slot * SEL_TILE, SEL_TILE), SEL_TILE), :], qbd)

    raw_next = cmp_scores(nqn_ref[0])
    w0 = pl.multiple_of(i * QB, QB)
    raw_w = _dot_nt(wk_sc[pl.ds(w0, WIN_BAND), :], qbd)
    raw_near = [tile_raw(last + 1), tile_raw(last)]

    def tile_scores(slot, bias, s=None):
        if s is None:
            s = tile_raw(slot)
        if bias is not None:
            s = s + bias
        parts = []
        for r in range(SEL_TILE_BLOCKS):
            row = ch_sc[buf, pl.ds(8 - SEL_TILE_BLOCKS + slot * SEL_TILE_BLOCKS + r, 1), :]
            parts.append(s[r * SEL_BLOCK:(r + 1) * SEL_BLOCK, :] + jnp.concatenate([row] * GQ, axis=1))
        return jnp.concatenate(parts, axis=0)

    def group_max(scores):
        return functools.reduce(jnp.maximum, [jnp.max(s, 0, keepdims=True) for s in scores])

    def sel_group(slots, scores, t_max, first):
        m_new = t_max if first else jnp.maximum(m_sc[...], t_max)
        exps = [jnp.exp2(s - m_new) for s in scores]
        l_new = functools.reduce(jnp.add, [jnp.sum(e, 0, keepdims=True) for e in exps])
        acc_new = functools.reduce(jnp.add, [_dot(vst_sc[slot], e.astype(BF16)) for slot, e in zip(slots, exps)])
        if first:
            l_sc[...] = l_new
            acc_sc[...] = acc_new
        else:
            scale = jnp.exp2(m_sc[...] - m_new)
            l_sc[...] = scale * l_sc[...] + l_new
            acc_sc[...] = scale * acc_sc[...] + acc_new
        m_sc[...] = m_new

    def far_slots(p):
        return [jnp.where(FAR_GROUP * p + 1 + n < last, FAR_GROUP * p + 1 + n, 0) for n in range(FAR_GROUP)]

    def far_scores(p):
        scores = [tile_scores(slot, None) for slot in far_slots(p)]
        for n, s in enumerate(scores):
            fs_sc[p % 2, n] = s
        ft_sc[p % 2] = group_max(scores)

    far_scores(0)
    cmp_and_select(raw_next, i + 1, 1 - buf)

    in_seq = lax.broadcasted_iota(I32, (WIN_BAND, N_COLS), 0) >= WINDOW - i * QB
    p_w = _softmax2_rows(jnp.where(in_seq, raw_w + bw_sc[...], NEG_INF))
    o_w = _dot_tn(wv_sc[pl.ds(w0, WIN_BAND), :], p_w.astype(BF16))

    u0 = pl.multiple_of(SEL_REL0 - SEL_TILE - SEL_BLOCK * (i % SEL_TILE_BLOCKS), SEL_BLOCK)
    near = [tile_scores(last + 1, bs_sc[pl.ds(u0 + SEL_TILE, SEL_TILE), :], raw_near[0]),
            tile_scores(last, bs_sc[pl.ds(u0, SEL_TILE), :], raw_near[1])]
    sel_group((last + 1, last), near, group_max(near), True)

    def far_group(p, carry):
        sel_group(far_slots(p), [fs_sc[p % 2, n] for n in range(FAR_GROUP)], ft_sc[p % 2], False)
        far_scores(p + 1)
        return carry

    lax.fori_loop(0, (last - 1 + FAR_GROUP - 1) // FAR_GROUP, far_group, 0)
    o_s = acc_sc[...] / jnp.maximum(l_sc[...], 1e-30)

    eye = (lax.broadcasted_iota(I32, (3 * N_NSA_HEADS + 8, LANES), 0) + G_NSA
           == lax.broadcasted_iota(I32, (3 * N_NSA_HEADS + 8, LANES), 1)).astype(F32)
    gates = jax.nn.sigmoid(_dot_nt(eye, zg_ref[0], hi))
    pieces = []
    for h in range(N_NSA_HEADS):
        g = h // GQ
        jj = PROMPT_HEADS.index(h)
        part = lambda o: o[g * HEAD_DIM:(g + 1) * HEAD_DIM, jj * QB:(jj + 1) * QB]
        pieces.append(gates[h:h + 1, :] * part(o_c)
                      + gates[N_NSA_HEADS + h:N_NSA_HEADS + h + 1, :] * part(o_s)
                      + gates[2 * N_NSA_HEADS + h:2 * N_NSA_HEADS + h + 1, :] * part(o_w))
    mix_t = jnp.concatenate(pieces, axis=0)
    mix_t = jnp.concatenate([mix_t, jnp.zeros((NSA_W, LANES - QB), F32)], axis=1)
    out_ref[0] = jnp.transpose(mix_t)[0:QB, :]


def _nsa_prompt(nq, kv, win, zg, pool_w2, rel_bias):
    bsz, t_len, _ = nq.shape
    assert t_len // CMP_BLOCK == LANES and t_len % (2 * SEL_TILE) == 0
    n_cmp = t_len // CMP_BLOCK
    per_b = lambda w: pl.BlockSpec((1, t_len, w), lambda b, i: (b, 0, 0))
    blk = lambda w: pl.BlockSpec((1, NSA_QBLOCK, w), lambda b, i: (b, i, 0))
    n_qb = t_len // NSA_QBLOCK
    next_blk = pl.BlockSpec((1, NSA_QBLOCK, NSA_W), lambda b, i: (b, jnp.minimum(i + 1, n_qb - 1), 0))
    return pl.pallas_call(
        functools.partial(_nsa_prompt_body, t_len=t_len),
        grid=(bsz, n_qb),
        in_specs=[pl.BlockSpec(memory_space=pltpu.SMEM),
                  blk(NSA_W), next_blk, per_b(4 * KV_W), per_b(2 * KV_W), blk(LANES),
                  pl.BlockSpec((CMP_BLOCK, 2 * LANES), lambda b, i: (0, 0))],
        out_specs=blk(NSA_W),
        out_shape=jax.ShapeDtypeStruct((bsz, t_len, NSA_W), F32),
        scratch_shapes=[pltpu.VMEM((SEL_TILE + t_len, LANES), BF16),
                        pltpu.VMEM((1 + t_len // SEL_TILE, LANES, SEL_TILE), BF16),
                        pltpu.VMEM((WINDOW + t_len, LANES), BF16),
                        pltpu.VMEM((WINDOW + t_len, LANES), BF16),
                        pltpu.VMEM((n_cmp, LANES), BF16),
                        pltpu.VMEM((n_cmp, LANES), F32),
                        pltpu.VMEM((LANES, n_cmp), BF16),
                        pltpu.VMEM((SEL_BIAS_ROWS, N_COLS), F32),
                        pltpu.VMEM((WIN_BAND, N_COLS), F32),
                        pltpu.VMEM((n_cmp, N_COLS), F32),
                        pltpu.VMEM((1, N_COLS), F32),
                        pltpu.VMEM((1, N_COLS), F32),
                        pltpu.VMEM((LANES, N_COLS), F32),
                        pltpu.VMEM((2, 8 + t_len // SEL_BLOCK, LANES), F32),
                        pltpu.VMEM((2, LANES, N_COLS), F32),
                        pltpu.VMEM((2, FAR_GROUP, SEL_TILE, N_COLS), F32),
                        pltpu.VMEM((2, 1, N_COLS), F32)],
        compiler_params=pltpu.CompilerParams(dimension_semantics=("arbitrary", "arbitrary"),
                                             vmem_limit_bytes=VMEM_LIMIT),
        name="nsa_prompt",
    )(rel_bias, nq, nq, kv, win, zg, pool_w2)


CMP_PAGES = 32
CMP_SLOTS = 3
N_PICK = SEL_TOPK - 1
N_OWNERS = N_KV_HEADS * N_NEW
N_FETCH = N_OWNERS * N_PICK
OWNER_KEYS = N_PICK * PAGE_SIZE


def _sample_row_ids():
    r = lax.broadcasted_iota(I32, (N_QROWS, 1), 0)
    return r % SEQ_PAD - (SEQ_PAD - N_NEW), r // SEQ_PAD


def _sample_bias(dist, rb_ref):
    _, head = _sample_row_ids()
    col = lambda k: sum(jnp.where(head == h, rb_ref[k, h], 0.0) for h in range(N_NSA_HEADS))
    acc = jnp.broadcast_to(col(0), dist.shape)
    for k in range(1, N_BUCKETS):
        acc = jnp.where(dist >= BUCKET_THR[k], col(k), acc)
    return acc


def _sample_bias_table(rb_ref):
    assert FAR_DIST < LANES
    return _sample_bias(lax.broadcasted_iota(I32, (N_QROWS, LANES), 1), rb_ref)


def _sample_bias_lookup(dist, table):
    idx = jnp.clip(dist, 0, LANES - 1)
    return jnp.concatenate([jnp.take_along_axis(table, idx[:, c * LANES:(c + 1) * LANES], axis=1)
                            for c in range(dist.shape[1] // LANES)], axis=1)


def _softmax_two(s1, s2):
    mx = jnp.maximum(jnp.max(s1, -1, keepdims=True), jnp.max(s2, -1, keepdims=True))
    mx = jnp.where(mx > NEG_INF, mx, 0.0)
    e1 = jnp.exp(s1 - mx)
    e2 = jnp.exp(s2 - mx)
    den = jnp.maximum(jnp.sum(e1, -1, keepdims=True) + jnp.sum(e2, -1, keepdims=True), 1e-30)
    return e1 / den, e2 / den


def _nsa_cmp_body(pt_ref, rb_ref, nq_ref, pwt_ref, cache_ref, oc_ref, idx_ref,
                  buf, sem, seg_sc, kct_sc, vct_sc, *, n_pages, past_len):
    b = pl.program_id(0)
    nb = pl.num_programs(0)
    n_chunks = n_pages // CMP_PAGES
    chunk_rows = CMP_PAGES * PAGE_SIZE
    chunk_blocks = chunk_rows // CMP_BLOCK
    n_cmp = past_len // CMP_BLOCK
    n_sel_past = past_len // SEL_BLOCK
    n_prob = N_KV_HEADS * SEQ_PAD
    total = nb * n_chunks
    hi = lax.Precision.HIGHEST

    def page_copy(gi, p):
        page = pt_ref[(gi // n_chunks) * n_pages + (gi % n_chunks) * CMP_PAGES + p]
        slot = gi % CMP_SLOTS
        return pltpu.make_async_copy(cache_ref.at[page, pl.ds(0, 2 * LANES), :], buf.at[slot, p], sem.at[slot])

    def start_chunk(gi):
        for p in range(CMP_PAGES):
            page_copy(gi, p).start()

    def wait_chunk(gi):
        for p in range(CMP_PAGES):
            page_copy(gi, p).wait()

    @pl.when(b == 0)
    def _prologue():
        seg_sc[...] = (lax.broadcasted_iota(I32, (chunk_rows, chunk_blocks), 0) // CMP_BLOCK
                       == lax.broadcasted_iota(I32, (chunk_rows, chunk_blocks), 1)).astype(BF16)
        for gi in range(min(CMP_SLOTS - 1, total)):
            start_chunk(gi)

    def chunk_step(c, carry):
        gi = b * n_chunks + c
        ahead = gi + (CMP_SLOTS - 1)

        @pl.when(ahead < total)
        def _prefetch():
            start_chunk(ahead)

        wait_chunk(gi)
        slot = gi % CMP_SLOTS
        y = jnp.concatenate([(buf[slot, p] * pwt_ref[...]).astype(BF16) for p in range(CMP_PAGES)], axis=1)
        pooled = _dot(y, seg_sc[...])
        kct_sc[c] = pooled[0:LANES].astype(BF16)
        vct_sc[c] = pooled[LANES:2 * LANES].astype(BF16)
        return carry

    lax.fori_loop(0, n_chunks, chunk_step, 0)

    qbd = _block_diag_queries(nq_ref[0], SEQ_PAD)
    tok, _ = _sample_row_ids()
    s_c = jnp.concatenate([_dot(qbd, kct_sc[c]) for c in range(n_chunks)], axis=1)
    near0 = n_cmp - LANES
    assert past_len - (SEQ_PAD - N_NEW) - (CMP_BLOCK * (near0 - 1) + CMP_BLOCK - 1) >= FAR_DIST
    dist = past_len + tok - ((lax.broadcasted_iota(I32, (N_QROWS, LANES), 1) + near0) * CMP_BLOCK + CMP_BLOCK - 1)
    near = jnp.where(dist >= 0, s_c[:, near0:] + _sample_bias(dist, rb_ref), NEG_INF)
    far = s_c[:, :near0] + _sample_bias(jnp.full((N_QROWS, 1), FAR_DIST, I32), rb_ref)
    p_c = _masked_softmax(jnp.concatenate([far, near], axis=1))
    o_c = _dot_nt(p_c[:, 0:chunk_blocks].astype(BF16), vct_sc[0])
    for c in range(1, n_chunks):
        o_c = o_c + _dot_nt(p_c[:, c * chunk_blocks:(c + 1) * chunk_blocks].astype(BF16), vct_sc[c])
    oc_ref[0] = o_c

    p_sum = jnp.sum(p_c.reshape(N_KV_HEADS, GQ, SEQ_PAD, n_cmp), axis=1).reshape(n_prob, n_cmp)
    p_sum = jnp.concatenate([p_sum, jnp.zeros((LANES - n_prob, n_cmp), F32)], axis=0)
    ratio = SEL_BLOCK // CMP_BLOCK
    pair = (lax.broadcasted_iota(I32, (n_sel_past, n_cmp), 1) // ratio
            == lax.broadcasted_iota(I32, (n_sel_past, n_cmp), 0)).astype(F32)
    imp = _dot_nt(pair, p_sum, hi)
    blk = lax.broadcasted_iota(I32, (n_sel_past, LANES), 0)
    pick_i = lax.broadcasted_iota(I32, (SEL_TOPK, LANES), 0)

    def pick(k, carry):
        work, ids = carry
        best = jnp.max(work, 0, keepdims=True)
        first = jnp.min(jnp.where(work == best, blk, n_sel_past), 0, keepdims=True)
        return jnp.where(blk == first, NEG_INF, work), jnp.where(pick_i == k, first, ids)

    _, ids = lax.fori_loop(0, N_PICK, pick, (imp, jnp.zeros((SEL_TOPK, LANES), I32)), unroll=True)
    idx_ref[0] = ids


def _nsa_cmp(page_table, rel_bias, nq_s, pool_wt, cache3):
    dec_b, n_pages = page_table.shape
    past_len = n_pages * PAGE_SIZE
    assert n_pages % CMP_PAGES == 0 and past_len // SEL_BLOCK >= N_PICK
    n_chunks = n_pages // CMP_PAGES
    chunk_rows = CMP_PAGES * PAGE_SIZE
    chunk_blocks = chunk_rows // CMP_BLOCK
    assert chunk_blocks == LANES
    grid_spec = pltpu.PrefetchScalarGridSpec(
        num_scalar_prefetch=1,
        grid=(dec_b,),
        in_specs=[pl.BlockSpec(memory_space=pltpu.SMEM),
                  pl.BlockSpec((1, SEQ_PAD, NSA_W), lambda b, pt: (b, 0, 0)),
                  pl.BlockSpec((2 * LANES, PAGE_SIZE), lambda b, pt: (0, 0)),
                  pl.BlockSpec(memory_space=pl.ANY)],
        out_specs=[pl.BlockSpec((1, N_QROWS, LANES), lambda b, pt: (b, 0, 0)),
                   pl.BlockSpec((1, SEL_TOPK, LANES), lambda b, pt: (b, 0, 0))],
        scratch_shapes=[pltpu.VMEM((CMP_SLOTS, CMP_PAGES, 2 * LANES, PAGE_SIZE), F32),
                        pltpu.SemaphoreType.DMA((CMP_SLOTS,)),
                        pltpu.VMEM((chunk_rows, chunk_blocks), BF16),
                        pltpu.VMEM((n_chunks, LANES, chunk_blocks), BF16),
                        pltpu.VMEM((n_chunks, LANES, chunk_blocks), BF16)],
    )
    return pl.pallas_call(
        functools.partial(_nsa_cmp_body, n_pages=n_pages, past_len=past_len),
        grid_spec=grid_spec,
        out_shape=[jax.ShapeDtypeStruct((dec_b, N_QROWS, LANES), F32),
                   jax.ShapeDtypeStruct((dec_b, SEL_TOPK, LANES), I32)],
        compiler_params=pltpu.CompilerParams(dimension_semantics=("arbitrary",),
                                             vmem_limit_bytes=VMEM_LIMIT),
        name="nsa_sample_cmp",
    )(page_table.reshape(-1), rel_bias, nq_s, pool_wt, cache3)


def _nsa_sel_body(pt_ref, ids_ref, rb_ref, nq_ref, kvn_ref, winc_ref, winn_ref, zg_ref, oc_ref, idv_ref,
                  cache_ref, out_ref, buf, sem, *, n_pages, past_len):
    b = pl.program_id(0)
    nb = pl.num_programs(0)
    hi = lax.Precision.HIGHEST

    def block_copy(bb, n, slot):
        blk = ids_ref[bb * N_FETCH + n]
        page = pt_ref[bb * n_pages + blk // 2]
        return pltpu.make_async_copy(cache_ref.at[page, pl.ds(2 * LANES, 2 * LANES), :], buf.at[slot, n],
                                     sem.at[slot])

    def start_all(bb, slot):
        def go(n, carry):
            block_copy(bb, n, slot).start()
            return carry
        lax.fori_loop(0, N_FETCH, go, 0, unroll=8)

    def wait_all(bb, slot):
        def go(n, carry):
            block_copy(bb, n, slot).wait()
            return carry
        lax.fori_loop(0, N_FETCH, go, 0, unroll=8)

    slot = b % 2

    @pl.when(b == 0)
    def _prologue():
        start_all(0, 0)

    @pl.when(b + 1 < nb)
    def _prefetch():
        start_all(b + 1, 1 - slot)

    qbd = _block_diag_queries(nq_ref[0], SEQ_PAD)
    tok, head = _sample_row_ids()
    owner = (head // GQ) * N_NEW + tok

    wc = winc_ref[0]
    wn = winn_ref[0]
    r_i = lax.broadcasted_iota(I32, (N_QROWS, WINDOW), 1)
    dist1 = WINDOW + tok - r_i
    bias_table = _sample_bias_table(rb_ref)
    s1 = _dot(qbd, wc[0:LANES, :].astype(BF16)) + _sample_bias_lookup(dist1, bias_table)
    s1 = jnp.where((dist1 >= 0) & (dist1 <= WINDOW), s1, NEG_INF)
    n_i = lax.broadcasted_iota(I32, (N_QROWS, SEQ_PAD), 1) - (SEQ_PAD - N_NEW)
    dist2 = tok - n_i
    new_ok = (n_i >= 0) & (dist2 >= 0)
    bias2 = _sample_bias(dist2, rb_ref)
    s2 = jnp.where(new_ok, _dot_nt(qbd, wn[:, 0:LANES].astype(BF16)) + bias2, NEG_INF)
    p1, p2 = _softmax_two(s1, s2)
    o_w = (_dot_nt(p1.astype(BF16), wc[LANES:2 * LANES, :].astype(BF16))
           + _dot(p2.astype(BF16), wn[:, LANES:2 * LANES].astype(BF16)))

    wait_all(b, slot)

    def owner_tiles(o, r0):
        return jnp.concatenate([buf[slot, o * N_PICK + k, r0:r0 + LANES, :] for k in range(N_PICK)],
                               axis=1).astype(BF16)

    s_f = jnp.full((N_QROWS, OWNER_KEYS), NEG_INF, F32)
    for o in range(N_OWNERS):
        s_f = jnp.where(owner == o, _dot(qbd, owner_tiles(o, 0)), s_f)
    spread = (lax.broadcasted_iota(I32, (SEL_TOPK, OWNER_KEYS), 0)
              == lax.broadcasted_iota(I32, (SEL_TOPK, OWNER_KEYS), 1) // PAGE_SIZE).astype(F32)
    blk_of_key = _dot_tn(idv_ref[0].astype(F32), spread, hi)
    blk_rows = jnp.concatenate([blk_of_key[(h // GQ) * SEQ_PAD:(h // GQ + 1) * SEQ_PAD]
                                for h in range(N_NSA_HEADS)], axis=0).astype(I32)
    blocks_per_page = PAGE_SIZE // SEL_BLOCK
    in_page = lax.broadcasted_iota(I32, (N_QROWS, OWNER_KEYS), 1) % PAGE_SIZE
    in_block = in_page // SEL_BLOCK == blk_rows % blocks_per_page
    dist_s = past_len + tok - ((blk_rows // blocks_per_page) * PAGE_SIZE + in_page)
    s_f = jnp.where((tok >= 0) & in_block & (dist_s >= 0), s_f + _sample_bias_lookup(dist_s, bias_table), NEG_INF)
    kvn = kvn_ref[0]
    s_n = jnp.where(new_ok, _dot_nt(qbd, kvn[:, 2 * LANES:3 * LANES].astype(BF16)) + bias2, NEG_INF)
    p_f, p_n = _softmax_two(s_f, s_n)
    o_s = _dot(p_n.astype(BF16), kvn[:, 3 * LANES:4 * LANES].astype(BF16))
    for o in range(N_OWNERS):
        o_s = o_s + _dot_nt(jnp.where(owner == o, p_f, 0.0).astype(BF16), owner_tiles(o, LANES))

    o_c = oc_ref[0]
    gates = jax.nn.sigmoid(zg_ref[0])
    is_new = lax.broadcasted_iota(I32, (SEQ_PAD, HEAD_DIM), 0) >= SEQ_PAD - N_NEW
    for h in range(N_NSA_HEADS):
        mix = _gated_mix(gates, o_c, o_s, o_w, h, SEQ_PAD)
        out_ref[0, :, h * HEAD_DIM:(h + 1) * HEAD_DIM] = jnp.where(is_new, mix, 0.0)


def _nsa_sel(page_table, ids, ids_rows, rel_bias, nq_s, kv_s, win_cache, win_s, zg_s, o_c, cache3):
    dec_b, n_pages = page_table.shape
    past_len = n_pages * PAGE_SIZE
    seq = lambda w: pl.BlockSpec((1, SEQ_PAD, w), lambda b, pt, ix: (b, 0, 0))
    grid_spec = pltpu.PrefetchScalarGridSpec(
        num_scalar_prefetch=2,
        grid=(dec_b,),
        in_specs=[pl.BlockSpec(memory_space=pltpu.SMEM),
                  seq(NSA_W), seq(4 * KV_W),
                  pl.BlockSpec((1, 2 * KV_W, WINDOW), lambda b, pt, ix: (b, 0, 0)),
                  seq(2 * KV_W), seq(LANES),
                  pl.BlockSpec((1, N_QROWS, LANES), lambda b, pt, ix: (b, 0, 0)),
                  pl.BlockSpec((1, SEL_TOPK, LANES), lambda b, pt, ix: (b, 0, 0)),
                  pl.BlockSpec(memory_space=pl.ANY)],
        out_specs=seq(NSA_W),
        scratch_shapes=[pltpu.VMEM((2, N_FETCH, 2 * LANES, PAGE_SIZE), F32),
                        pltpu.SemaphoreType.DMA((2,))],
    )
    return pl.pallas_call(
        functools.partial(_nsa_sel_body, n_pages=n_pages, past_len=past_len),
        grid_spec=grid_spec,
        out_shape=jax.ShapeDtypeStruct((dec_b, SEQ_PAD, NSA_W), F32),
        compiler_params=pltpu.CompilerParams(dimension_semantics=("arbitrary",),
                                             vmem_limit_bytes=VMEM_LIMIT),
        name="nsa_sample_sel",
    )(page_table.reshape(-1), ids.reshape(-1), rel_bias, nq_s, kv_s, win_cache, win_s, zg_s, o_c, ids_rows, cache3)


FF_CHUNK = 896


def _ffn_body(*refs, tm, has_override):
    if has_override:
        (x_ref, mm_ref, mn_ref, ov_ref, woa_ref, wob_ref, l1w_ref, l1b_ref, wup_ref, bup_ref, cw_ref, cb_ref,
         wdn_ref, bdn_ref, l2w_ref, l2b_ref, y_ref, tail_ref, carry_sc, ubuf) = refs
    else:
        (x_ref, mm_ref, mn_ref, woa_ref, wob_ref, l1w_ref, l1b_ref, wup_ref, bup_ref, cw_ref, cb_ref,
         wdn_ref, bdn_ref, l2w_ref, l2b_ref, y_ref, tail_ref, carry_sc, ubuf) = refs
    j = pl.program_id(1)

    @pl.when(j == 0)
    def _init():
        carry_sc[...] = jnp.zeros(carry_sc.shape, F32)

    x = x_ref[0]
    h = ALPHA * x + (_dot(mm_ref[0].astype(BF16), woa_ref[...]) + _dot(mn_ref[0].astype(BF16), wob_ref[...]))
    x1 = _layer_norm(h, l1w_ref[...], l1b_ref[...])
    x1b = x1.astype(BF16)
    if has_override:
        state_row = lax.broadcasted_iota(I32, (tm, FF_CHUNK), 0) % SEQ_PAD < SEQ_PAD - N_NEW

    def conv_half(c0):
        cols = slice(c0, c0 + FF_CHUNK)
        u = _dot(x1b, wup_ref[:, cols]) + bup_ref[:, cols]
        if has_override:
            u = jnp.where(state_row, ov_ref[0, :, cols], u)
        ubuf[0:8, :] = carry_sc[:, cols]
        ubuf[8:8 + tm, :] = u
        out = cb_ref[:, cols] + ubuf[6:6 + tm, :] * cw_ref[0:1, cols]
        out = out + ubuf[7:7 + tm, :] * cw_ref[1:2, cols]
        out = out + u * cw_ref[2:3, cols]
        carry_sc[:, cols] = u[tm - 8:tm, :]
        tail_ref[0, 0, :, cols] = u if has_override else u[tm - 8:tm, :]
        return out

    ff = jnp.zeros((tm, D_MODEL), F32)
    for c in range(D_FF // FF_CHUNK):
        ga = conv_half(c * FF_CHUNK)
        gb = conv_half(D_FF + c * FF_CHUNK)
        gelu = ga * (0.5 * (1.0 + jnp.tanh(math.sqrt(2.0 / math.pi) * (ga + 0.044715 * (ga * ga * ga)))))
        ff = ff + _dot((gelu * gb).astype(BF16), wdn_ref[c * FF_CHUNK:(c + 1) * FF_CHUNK, :])
    ff = ff + bdn_ref[...]
    y_ref[0] = _layer_norm(ALPHA * x1 + ff, l2w_ref[...], l2b_ref[...])


def _ffn(x, mix_m, mix_n, override, wts):
    bsz, t_len, _ = x.shape
    tm = min(t_len, 512)
    nt = t_len // tm
    has_override = override is not None
    tail_rows = tm if has_override else 8
    row = lambda w: pl.BlockSpec((1, tm, w), lambda b, j: (b, j, 0))
    const = lambda shape: pl.BlockSpec(shape, lambda b, j: (0,) * len(shape), pipeline_mode=pl.Buffered(1))
    in_specs = [row(D_MODEL), row(MLSTM_W), row(NSA_W)]
    args = [x, mix_m, mix_n]
    if has_override:
        in_specs.append(row(2 * D_FF))
        args.append(override)
    in_specs += [const(w.shape) for w in wts]
    args += list(wts)
    y, tail = pl.pallas_call(
        functools.partial(_ffn_body, tm=tm, has_override=has_override),
        grid=(bsz, nt),
        in_specs=in_specs,
        out_specs=[row(D_MODEL), pl.BlockSpec((1, 1, tail_rows, 2 * D_FF), lambda b, j: (b, j, 0, 0))],
        out_shape=[jax.ShapeDtypeStruct((bsz, t_len, D_MODEL), F32),
                   jax.ShapeDtypeStruct((bsz, nt, tail_rows, 2 * D_FF), F32)],
        scratch_shapes=[pltpu.VMEM((8, 2 * D_FF), F32), pltpu.VMEM((8 + tm, FF_CHUNK), F32)],
        compiler_params=pltpu.CompilerParams(dimension_semantics=("arbitrary", "arbitrary"),
                                             vmem_limit_bytes=VMEM_LIMIT),
        name="ffn",
    )(*args)
    return y, tail


def _permute_in_proj(w_in, b_in):
    gates_a = slice(2 * MLSTM_W + 2 * MLSTM_W, 2 * MLSTM_W + 2 * MLSTM_W + 2 * N_MLSTM_HEADS)
    nsa0 = gates_a.stop
    nsa1 = nsa0 + NSA_W + 6 * KV_W

    def perm(a):
        pad = IN_PAD - a.shape[-1]
        parts = [a[..., :gates_a.start], a[..., nsa0:nsa1], a[..., gates_a], a[..., nsa1:]]
        parts.append(jnp.zeros(a.shape[:-1] + (pad,), a.dtype))
        return jnp.concatenate(parts, axis=-1)

    return perm(w_in).astype(BF16), perm(b_in).reshape(1, IN_PAD)


def _front_pad(a, n):
    return jnp.pad(a, ((0, 0), (n, 0), (0, 0)))


def kernel(x_prompt, x_sample, cache_nsa_kv, cache_win_kv, state_mlstm_c, state_mlstm_n, state_mlstm_m,
           state_mlstm_conv, state_ffn_conv, page_table,
           w_in, b_in, mlstm_conv_w, mlstm_conv_b, mlstm_norm_w, nsa_pool_w, rel_bias, w_out,
           ln1_w, ln1_b, w_up, b_up, ffn_conv_w, ffn_conv_b, w_down, b_down, ln2_w, ln2_b):
    bsz, t_len, _ = x_prompt.shape
    dec_b, n_new, _ = x_sample.shape
    assert n_new == N_NEW and cache_win_kv.shape[1] == WINDOW

    w_p, b_p = _permute_in_proj(w_in, b_in)
    pool_w2 = jnp.concatenate([jnp.repeat(nsa_pool_w[0], HEAD_DIM, axis=-1),
                               jnp.repeat(nsa_pool_w[1], HEAD_DIM, axis=-1)], axis=-1)
    row = lambda v: v.reshape(1, -1)
    ffn_w = (w_out[:MLSTM_W].astype(BF16), w_out[MLSTM_W:].astype(BF16), row(ln1_w), row(ln1_b),
             w_up.astype(BF16), row(b_up), ffn_conv_w, row(ffn_conv_b), w_down.astype(BF16), row(b_down),
             row(ln2_w), row(ln2_b))

    zqk, zv, zo, nq, kv, win, zg, kv_t = _proj_in(x_prompt.reshape(bsz * t_len, D_MODEL), w_p, b_p, seq_len=t_len)
    seq3 = lambda a: a.reshape(bsz, t_len, a.shape[-1])
    zqk, zv, zo, nq, kv, win, zg = map(seq3, (zqk, zv, zo, nq, kv, win, zg))
    mix_m, c_p, n_p, m_p = _mlstm(zqk, zv, zo, zg, None, mlstm_conv_w, mlstm_conv_b, mlstm_norm_w, 0)
    mix_n = _nsa_prompt(nq, kv, win, zg, pool_w2, rel_bias)
    y_p, tail_p = _ffn(x_prompt, mix_m, mix_n, None, ffn_w)
    kv_p = jnp.transpose(kv_t.reshape(bsz, 4, N_KV_HEADS, HEAD_DIM, t_len), (0, 4, 1, 2, 3))
    win_p = win[:, t_len - min(WINDOW, t_len):].reshape(bsz, -1, 2, N_KV_HEADS, HEAD_DIM)
    mconv_p = zqk[:, t_len - (MLSTM_CONV - 1):]
    fconv_p = tail_p[:, -1, 8 - (FFN_CONV - 1):]

    pad = SEQ_PAD - N_NEW
    xs = _front_pad(x_sample, pad)
    szqk, szv, szo, snq, skv, swin, szg = _proj_in(xs.reshape(dec_b * SEQ_PAD, D_MODEL), w_p, b_p)
    sseq = lambda a: a.reshape(dec_b, SEQ_PAD, a.shape[-1])
    szqk, szv, szo, snq, skv, swin, szg = map(sseq, (szqk, szv, szo, snq, skv, swin, szg))
    lead = MLSTM_CHUNK - N_NEW
    new = lambda a: a[:, pad:]
    m_qk = jnp.concatenate([jnp.zeros((dec_b, lead - (MLSTM_CONV - 1), 2 * MLSTM_W), F32),
                            state_mlstm_conv.astype(F32), new(szqk)], axis=1)
    mix_ms, c_s, n_s, m_s = _mlstm(m_qk, _front_pad(new(szv), lead), _front_pad(new(szo), lead),
                                   _front_pad(new(szg), lead),
                                   (state_mlstm_c, state_mlstm_n, state_mlstm_m),
                                   mlstm_conv_w, mlstm_conv_b, mlstm_norm_w, lead)
    mix_ms = mix_ms[:, MLSTM_CHUNK - SEQ_PAD:]
    cache3 = jnp.transpose(cache_nsa_kv, (0, 2, 3, 4, 1)).reshape(cache_nsa_kv.shape[0], 4 * KV_W, PAGE_SIZE)
    pool_wt = jnp.tile(jnp.repeat(jnp.transpose(nsa_pool_w, (0, 2, 1)).reshape(2 * N_KV_HEADS, CMP_BLOCK),
                                  HEAD_DIM, axis=0), (1, PAGE_SIZE // CMP_BLOCK))
    o_c, ids_rows = _nsa_cmp(page_table, rel_bias, snq, pool_wt, cache3)
    ids = jnp.transpose(ids_rows[:, :N_PICK, :N_KV_HEADS * SEQ_PAD], (0, 2, 1))
    ids = ids.reshape(dec_b, N_KV_HEADS, SEQ_PAD, N_PICK)[:, :, pad:]
    win_cache = jnp.transpose(cache_win_kv, (0, 2, 3, 4, 1)).reshape(dec_b, 2 * KV_W, WINDOW)
    mix_ns = _nsa_sel(page_table, ids, ids_rows, rel_bias, snq, skv, win_cache, swin, szg, o_c, cache3)
    override = jnp.concatenate([jnp.zeros((dec_b, pad - (FFN_CONV - 1), 2 * D_FF), F32),
                                state_ffn_conv.astype(F32),
                                jnp.zeros((dec_b, N_NEW, 2 * D_FF), F32)], axis=1)
    flat = lambda a: a.reshape(1, dec_b * SEQ_PAD, a.shape[-1])
    y_s8, u_s = _ffn(flat(xs), flat(mix_ms), flat(mix_ns), flat(override), ffn_w)
    y_s = y_s8.reshape(dec_b, SEQ_PAD, D_MODEL)[:, pad:]
    kv_s = new(skv).reshape(dec_b, N_NEW, 4, N_KV_HEADS, HEAD_DIM)
    win_s = jnp.concatenate([cache_win_kv[:, N_NEW:], new(swin).reshape(dec_b, N_NEW, 2, N_KV_HEADS, HEAD_DIM)],
                            axis=1)
    mconv_s = szqk[:, SEQ_PAD - (MLSTM_CONV - 1):]
    fconv_s = u_s.reshape(dec_b, SEQ_PAD, 2 * D_FF)[:, SEQ_PAD - (FFN_CONV - 1):]
    return (y_p, y_s, kv_p, kv_s, win_p, win_s, c_p, c_s, n_p, n_s, m_p, m_s, mconv_p, mconv_s, fconv_p, fconv_s)
```

```python
import functools
import math

import jax
import jax.numpy as jnp
from jax import lax
from jax.experimental import pallas as pl
from jax.experimental.pallas import tpu as pltpu

F32 = jnp.float32
BF16 = jnp.bfloat16
I32 = jnp.int32

D_MODEL = 1024
HEAD_DIM = 64
N_MLSTM_HEADS = 8
N_NSA_HEADS = 8
N_KV_HEADS = 2
GQ = N_NSA_HEADS // N_KV_HEADS
MLSTM_W = N_MLSTM_HEADS * HEAD_DIM
NSA_W = N_NSA_HEADS * HEAD_DIM
KV_W = N_KV_HEADS * HEAD_DIM
MLSTM_CHUNK = 64
MLSTM_CONV = 4
CMP_BLOCK = 32
SEL_BLOCK = 64
SEL_TOPK = 16
WINDOW = 512
NSA_QBLOCK = 64
N_BUCKETS = 32
MAX_DISTANCE = 128
D_FF = 2688
FFN_CONV = 3
DEPTH = 1
ALPHA = (2.0 * DEPTH) ** 0.25
LN_EPS = 1e-5
PAGE_SIZE = 128

LANES = 128
VMEM_LIMIT = 56 * 1024 * 1024
NEG_INF = float("-inf")

SEQ_PAD = 8
N_NEW = 4
N_QROWS = N_NSA_HEADS * SEQ_PAD


def _bucket_thresholds():
    max_exact = N_BUCKETS // 2
    thr = list(range(max_exact + 1))
    for k in range(1, N_BUCKETS - max_exact):
        thr.append(int(math.ceil(max_exact * (MAX_DISTANCE / max_exact) ** (k / (N_BUCKETS - max_exact)))))
    return tuple(thr)


BUCKET_THR = _bucket_thresholds()
FAR_DIST = BUCKET_THR[N_BUCKETS - 1]


def _dot(a, b, precision=None):
    return jnp.dot(a, b, preferred_element_type=F32, precision=precision)


def _dot_nt(a, b, precision=None):
    return lax.dot_general(a, b, (((1,), (1,)), ((), ())), preferred_element_type=F32, precision=precision)


def _dot_tn(a, b, precision=None):
    return lax.dot_general(a, b, (((0,), (0,)), ((), ())), preferred_element_type=F32, precision=precision)


def _layer_norm(x, w, b):
    mu = jnp.mean(x, -1, keepdims=True)
    xc = x - mu
    var = jnp.mean(xc * xc, -1, keepdims=True)
    return xc * lax.rsqrt(var + LN_EPS) * w + b


def _log_sigmoid(x):
    return jnp.minimum(x, 0.0) - jnp.log1p(jnp.exp(-jnp.abs(x)))


def _bias_from_dist(dist, rb_ref, head):
    acc = jnp.full(dist.shape, rb_ref[0, head], F32)
    for k in range(1, N_BUCKETS):
        acc = jnp.where(dist >= BUCKET_THR[k], rb_ref[k, head], acc)
    return acc


def _masked_softmax(s):
    mx = jnp.max(s, -1, keepdims=True)
    mx = jnp.where(mx > NEG_INF, mx, 0.0)
    e = jnp.exp(s - mx)
    den = jnp.sum(e, -1, keepdims=True)
    return e / jnp.maximum(den, 1e-30)


_C_QK, _C_V, _C_O, _C_NQ, _C_KV, _C_WIN, _C_G = 0, 1024, 1536, 2048, 2560, 3072, 3328
IN_PAD = 3456
G_IGATE, G_FGATE, G_NSA = 0, N_MLSTM_HEADS, 2 * N_MLSTM_HEADS


def _proj_in_body(*refs, kv_transposed):
    if kv_transposed:
        x_ref, w_ref, b_ref, wkvt_ref, bkv_ref, qk_ref, v_ref, o_ref, nq_ref, kv_ref, win_ref, g_ref, kvt_ref = refs
    else:
        x_ref, w_ref, b_ref, qk_ref, v_ref, o_ref, nq_ref, kv_ref, win_ref, g_ref = refs
    xb = x_ref[...].astype(BF16)
    z = _dot(xb, w_ref[...]) + b_ref[...]
    qk_ref[...] = z[:, _C_QK:_C_V]
    v_ref[...] = z[:, _C_V:_C_O]
    o_ref[...] = z[:, _C_O:_C_NQ]
    nq_ref[...] = z[:, _C_NQ:_C_KV]
    kv_ref[...] = z[:, _C_KV:_C_WIN]
    win_ref[...] = z[:, _C_WIN:_C_G]
    g_ref[...] = z[:, _C_G:IN_PAD]
    if kv_transposed:
        kvt_ref[0] = _dot_nt(wkvt_ref[...], xb) + bkv_ref[...]


def _proj_in(x2d, w_p, b_p, seq_len=None):
    rows = x2d.shape[0]
    tm = min(rows, 256)
    widths = (1024, 512, 512, 512, 512, 256, 128)
    const = lambda i: (0, 0)
    const_spec = lambda shape: pl.BlockSpec(shape, const, pipeline_mode=pl.Buffered(1))
    in_specs = [pl.BlockSpec((tm, D_MODEL), lambda i: (i, 0)), const_spec((D_MODEL, IN_PAD)), const_spec((1, IN_PAD))]
    args = [x2d, w_p, b_p]
    out_specs = [pl.BlockSpec((tm, w), lambda i: (i, 0)) for w in widths]
    out_shape = [jax.ShapeDtypeStruct((rows, w), F32) for w in widths]
    if seq_len is not None:
        tiles = seq_len // tm
        in_specs += [const_spec((4 * KV_W, D_MODEL)), const_spec((4 * KV_W, 1))]
        args += [jnp.transpose(w_p[:, _C_KV:_C_WIN]), b_p[0, _C_KV:_C_WIN].reshape(4 * KV_W, 1)]
        out_specs.append(pl.BlockSpec((1, 4 * KV_W, tm), lambda i: (i // tiles, 0, i % tiles)))
        out_shape.append(jax.ShapeDtypeStruct((rows // seq_len, 4 * KV_W, seq_len), F32))
    return pl.pallas_call(
        functools.partial(_proj_in_body, kv_transposed=seq_len is not None),
        grid=(rows // tm,),
        in_specs=in_specs,
        out_specs=out_specs,
        out_shape=out_shape,
        compiler_params=pltpu.CompilerParams(dimension_semantics=("arbitrary",),
                                             vmem_limit_bytes=VMEM_LIMIT),
        name="proj_in",
    )(*args)


N_PAIRS = N_MLSTM_HEADS // 2
GATE_LANE0 = N_MLSTM_HEADS
MLSTM_SEQS_PER_STEP = 4


def _dot_split(x, onehot_bf16, pieces):
    acc = None
    rem = x
    for n in range(pieces):
        part = rem.astype(BF16)
        d = _dot(part, onehot_bf16)
        acc = d if acc is None else acc + d
        if n + 1 < pieces:
            rem = rem - part.astype(F32)
    return acc


def _mlstm_body(*refs, tb, nbg, n_dummy, has_state):
    if has_state:
        (zqk_ref, zv_ref, zo_ref, zg_ref, c0_ref, n0_ref, m0_ref, cw_ref, cb_ref, nw_ref,
         mix_ref, c_out, n_out, m_out, ubuf, qk_sc, ct_sc, n_sc, m_sc) = refs
    else:
        (zqk_ref, zv_ref, zo_ref, zg_ref, cw_ref, cb_ref, nw_ref,
         mix_ref, c_out, n_out, m_out, ubuf, qk_sc, ct_sc, n_sc, m_sc) = refs
    L = MLSTM_CHUNK
    D = HEAD_DIM
    j = pl.program_id(1)
    hi = lax.Precision.HIGHEST
    head_lanes = slice(GATE_LANE0, GATE_LANE0 + N_MLSTM_HEADS)
    pairs = [(bi, p) for bi in range(nbg) for p in range(N_PAIRS)]

    @pl.when(j == 0)
    def _init():
        ubuf[:, 0:8, :] = jnp.zeros((nbg, 8, 2 * MLSTM_W), F32)
        ct_sc[...] = jnp.zeros(ct_sc.shape, F32)
        n_sc[...] = jnp.zeros(n_sc.shape, F32)
        m_sc[...] = jnp.zeros(m_sc.shape, F32)
        if has_state:
            for bi in range(nbg):
                for h in range(N_MLSTM_HEADS):
                    p, hh = divmod(h, 2)
                    hs = slice(hh * D, (hh + 1) * D)
                    ct_sc[bi, p, hs, hs] = c0_ref[bi, h]
                    n_sc[bi, p, :, hs] = n0_ref[bi, h:h + 1, :]
                m_sc[bi, :, head_lanes] = m0_ref[bi]

    for bi in range(nbg):
        u = zqk_ref[bi]
        ubuf[bi, 8:8 + tb, :] = u
        conv = cb_ref[...] + ubuf[bi, 5:5 + tb, :] * cw_ref[0:1, :]
        conv = conv + ubuf[bi, 6:6 + tb, :] * cw_ref[1:2, :]
        conv = conv + ubuf[bi, 7:7 + tb, :] * cw_ref[2:3, :]
        conv = conv + u * cw_ref[3:4, :]
        ubuf[bi, 5:8, :] = u[tb - 3:tb, :]
        qk_sc[bi] = conv * jax.nn.sigmoid(conv)

    tri = (lax.broadcasted_iota(I32, (L, L), 1) <= lax.broadcasted_iota(I32, (L, L), 0)).astype(F32)
    row_l = lax.broadcasted_iota(I32, (L, LANES), 0)
    lane = lax.broadcasted_iota(I32, (L, LANES), 1)
    low = lane < D
    key_tok = lane % D
    causal_pair = key_tok <= row_l
    eye_pair = key_tok == row_l
    spread = (lax.broadcasted_iota(I32, (LANES, MLSTM_W), 0) - GATE_LANE0
              == lax.broadcasted_iota(I32, (LANES, MLSTM_W), 1) // D).astype(BF16)
    same_head = (lax.broadcasted_iota(I32, (LANES, LANES), 0) // D
                 == lax.broadcasted_iota(I32, (LANES, LANES), 1) // D)
    ones_bd = same_head.astype(BF16)
    slab_rows = 3 * L + 8

    def chunk(ci, carry):
        r0 = pl.multiple_of(ci * L, L)
        rows = pl.ds(r0, L)
        gates = []
        for bi in range(nbg):
            g = zg_ref[bi, rows, :]
            if n_dummy:
                dummy = (row_l + ci * L) < n_dummy
                g = jnp.where(dummy, jnp.where(lane < G_FGATE, -1e30, 1e30), g)
            gates.append(g)
        b_all = _dot(tri, jnp.concatenate([_log_sigmoid(g) for g in gates], axis=1), hi)
        stacks = []
        for bi in range(nbg):
            b = b_all[:, bi * LANES:(bi + 1) * LANES]
            r = pltpu.roll(gates[bi], GATE_LANE0 - G_IGATE, 1) - b
            cm = r
            for sh in (1, 2, 4, 8, 16, 32):
                cm = jnp.where(row_l >= sh, jnp.maximum(cm, pltpu.roll(cm, sh, 0)), cm)
            m_st = m_sc[bi]
            mx = jnp.maximum(m_st, cm)
            mx_end = mx[L - 1:L, :]
            m_sc[bi] = b[L - 1:L, :] + mx_end
            stacks += [mx, r, b + mx, m_st, mx_end, jnp.zeros((6, LANES), F32)]
        x_all = _dot_split(jnp.concatenate(stacks, axis=0), spread, 2)

        work = []
        den_terms = []
        for bi, p in pairs:
            cs = slice(p * LANES, (p + 1) * LANES)
            x0 = bi * slab_rows
            mx_p, rs_p, mt_p = x_all[x0:x0 + L, cs], x_all[x0 + L:x0 + 2 * L, cs], x_all[x0 + 2 * L:x0 + 3 * L, cs]
            mst_p, mxe_p = x_all[x0 + 3 * L:x0 + 3 * L + 1, cs], x_all[x0 + 3 * L + 1:x0 + 3 * L + 2, cs]
            wi_p = jnp.exp(mst_p - mx_p)
            fl_p = jnp.exp(-mt_p)
            wk_p = jnp.exp(rs_p - mxe_p)
            sc_p = jnp.exp(mst_p - mxe_p)
            r_row = jnp.sum(jnp.where(eye_pair, rs_p, 0.0), axis=0, keepdims=True)
            w_intra = jnp.exp(jnp.where(causal_pair, r_row - mx_p, NEG_INF))
            q_p = qk_sc[bi, rows, cs]
            k_p = qk_sc[bi, rows, MLSTM_W + p * LANES:MLSTM_W + (p + 1) * LANES] * (D ** -0.5)
            v_p = zv_ref[bi, rows, cs]
            qb = q_p.astype(BF16)
            k_bd = jnp.concatenate([jnp.where(low, k_p, 0.0), jnp.where(low, 0.0, k_p)], axis=0)
            v_bd = jnp.concatenate([jnp.where(low, v_p, 0.0), jnp.where(low, 0.0, v_p)], axis=0).astype(BF16)
            c_t = ct_sc[bi, p]
            both = _dot_nt(qb, jnp.concatenate([k_bd, c_t], axis=0).astype(BF16))
            a = w_intra * both[:, 0:LANES]
            num = _dot(a.astype(BF16), v_bd) + both[:, LANES:2 * LANES] * wi_p
            n_row = n_sc[bi, p]
            den_terms.append(a + wi_p * (q_p * n_row))
            work.append((num, fl_p))
            kw = k_p * wk_p
            upd = _dot_tn(v_p.astype(BF16), kw.astype(BF16))
            ct_sc[bi, p] = sc_p * c_t + jnp.where(same_head, upd, 0.0)
            n_sc[bi, p] = sc_p * n_row + jnp.sum(kw, axis=0, keepdims=True)
        den_all = _dot_split(jnp.concatenate(den_terms, axis=0), ones_bd, 2)

        hhs = []
        for n, (num, fl_p) in enumerate(work):
            hhs.append(num / jnp.maximum(jnp.abs(den_all[n * L:(n + 1) * L, :]), fl_p))
        mu_all = _dot_split(jnp.concatenate(hhs, axis=0), ones_bd, 1) * (1.0 / D)
        hcs = [hh - mu_all[n * L:(n + 1) * L, :] for n, hh in enumerate(hhs)]
        var_all = _dot_split(jnp.concatenate([hc * hc for hc in hcs], axis=0), ones_bd, 1) * (1.0 / D)
        for n, (bi, p) in enumerate(pairs):
            cs = slice(p * LANES, (p + 1) * LANES)
            hn = hcs[n] * lax.rsqrt(var_all[n * L:(n + 1) * L, :] + LN_EPS)
            og = jax.nn.sigmoid(zo_ref[bi, rows, cs])
            mix_ref[bi, rows, cs] = og * (hn * nw_ref[:, cs])
        return carry

    lax.fori_loop(0, tb // L, chunk, 0, unroll=2 if (tb // L) % 2 == 0 else 1)

    @pl.when(j == pl.num_programs(1) - 1)
    def _final():
        for bi in range(nbg):
            for h in range(N_MLSTM_HEADS):
                p, hh = divmod(h, 2)
                hs = slice(hh * D, (hh + 1) * D)
                c_out[bi, h] = ct_sc[bi, p, hs, hs]
                n_out[bi, h:h + 1, :] = n_sc[bi, p, :, hs]
            m_out[bi] = m_sc[bi, :, head_lanes]


def _mlstm(zqk, zv, zo, zg, state, conv_w, conv_b, norm_w, n_dummy):
    bsz, t_len, _ = zqk.shape
    tb = min(t_len, 256)
    nbg = MLSTM_SEQS_PER_STEP
    assert bsz % nbg == 0
    has_state = state is not None
    row = lambda w: pl.BlockSpec((nbg, tb, w), lambda b, j: (b, j, 0))
    per_b = lambda shape: pl.BlockSpec((nbg,) + shape, lambda b, j: (b,) + (0,) * len(shape))
    const = lambda shape: pl.BlockSpec(shape, lambda b, j: (0,) * len(shape))
    in_specs = [row(2 * MLSTM_W), row(MLSTM_W), row(MLSTM_W), row(LANES)]
    args = [zqk, zv, zo, zg]
    if has_state:
        c0, n0, m0 = state
        in_specs += [per_b((N_MLSTM_HEADS, HEAD_DIM, HEAD_DIM)), per_b((N_MLSTM_HEADS, HEAD_DIM)),
                     per_b((1, N_MLSTM_HEADS))]
        args += [c0, n0, m0.reshape(bsz, 1, N_MLSTM_HEADS)]
    in_specs += [const((MLSTM_CONV, 2 * MLSTM_W)), const((1, 2 * MLSTM_W)), const((1, MLSTM_W))]
    args += [conv_w, conv_b.reshape(1, -1), norm_w.reshape(1, -1)]
    mix, c_new, n_new, m_new = pl.pallas_call(
        functools.partial(_mlstm_body, tb=tb, nbg=nbg, n_dummy=n_dummy, has_state=has_state),
        grid=(bsz // nbg, t_len // tb),
        in_specs=in_specs,
        out_specs=[row(MLSTM_W), per_b((N_MLSTM_HEADS, HEAD_DIM, HEAD_DIM)),
                   per_b((N_MLSTM_HEADS, HEAD_DIM)), per_b((1, N_MLSTM_HEADS))],
        out_shape=[jax.ShapeDtypeStruct((bsz, t_len, MLSTM_W), F32),
                   jax.ShapeDtypeStruct((bsz, N_MLSTM_HEADS, HEAD_DIM, HEAD_DIM), F32),
                   jax.ShapeDtypeStruct((bsz, N_MLSTM_HEADS, HEAD_DIM), F32),
                   jax.ShapeDtypeStruct((bsz, 1, N_MLSTM_HEADS), F32)],
        scratch_shapes=[pltpu.VMEM((nbg, 8 + tb, 2 * MLSTM_W), F32),
                        pltpu.VMEM((nbg, tb, 2 * MLSTM_W), F32),
                        pltpu.VMEM((nbg, N_PAIRS, LANES, LANES), F32),
                        pltpu.VMEM((nbg, N_PAIRS, 1, LANES), F32),
                        pltpu.VMEM((nbg, 1, LANES), F32)],
        compiler_params=pltpu.CompilerParams(dimension_semantics=("arbitrary", "arbitrary"),
                                             vmem_limit_bytes=VMEM_LIMIT),
        name="mlstm",
    )(*args)
    return mix, c_new, n_new, m_new.reshape(bsz, N_MLSTM_HEADS)


def _block_diag_queries(x, n_rows, heads=tuple(range(N_NSA_HEADS)), scale=HEAD_DIM ** -0.5):
    lane = lax.broadcasted_iota(I32, (n_rows, LANES), 1)
    low = lane < HEAD_DIM
    pieces = []
    for h in heads:
        pair = x[:, (h // 2) * LANES:(h // 2 + 1) * LANES]
        src_low = h % 2 == 0
        dst_low = h // GQ == 0
        if src_low != dst_low:
            pair = pltpu.roll(pair, HEAD_DIM, 1)
        pieces.append(jnp.where(low if dst_low else ~low, pair, 0.0))
    return (jnp.concatenate(pieces, axis=0) * scale).astype(BF16)


def _group_lanes(o, g):
    return o[:, g * HEAD_DIM:(g + 1) * HEAD_DIM]


def _gated_mix(gates, o_c, o_s, o_w, h, n_rows):
    g = h // GQ
    rs = slice(h * n_rows, (h + 1) * n_rows)
    col = lambda branch: gates[:, G_NSA + branch * N_NSA_HEADS + h:G_NSA + branch * N_NSA_HEADS + h + 1]
    return (col(0) * _group_lanes(o_c[rs], g) + col(1) * _group_lanes(o_s[rs], g)
            + col(2) * _group_lanes(o_w[rs], g))


SEL_TILE = 256
SEL_TILE_BLOCKS = SEL_TILE // SEL_BLOCK
FAR_GROUP = 2
Q_PER_STEP = 2
WIN_BAND = WINDOW + NSA_QBLOCK
N_COLS = N_NSA_HEADS * NSA_QBLOCK
PROMPT_HEADS = tuple(GQ * (jj % N_KV_HEADS) + jj // N_KV_HEADS for jj in range(N_NSA_HEADS))
SEL_REL0 = 2 * SEL_TILE - SEL_BLOCK
SEL_BIAS_ROWS = SEL_REL0 + SEL_TILE


LOG2E = math.log2(math.e)


def _softmax2_rows(s):
    mx = jnp.max(s, 0, keepdims=True)
    mx = jnp.where(mx > NEG_INF, mx, 0.0)
    e = jnp.exp2(s - mx)
    return e / jnp.maximum(jnp.sum(e, 0, keepdims=True), 1e-30)


def _nsa_prompt_body(rb_ref, nq_ref, nqn_ref, kv_ref, win_ref, zg_ref, pw_ref, out_ref,
                     ks_sc, vst_sc, wk_sc, wv_sc, kc_sc, pv_sc, vct_sc, bs_sc, bw_sc, fc_sc,
                     m_sc, l_sc, acc_sc, ch_sc, oc_sc, fs_sc, ft_sc, *, t_len):
    QB = NSA_QBLOCK
    n_cmp = t_len // CMP_BLOCK
    n_sel = t_len // SEL_BLOCK
    pid = pl.program_id(1)
    hi = lax.Precision.HIGHEST

    def cmp_scores(x_q):
        return _dot_nt(kc_sc[...], _block_diag_queries(x_q, QB, PROMPT_HEADS, HEAD_DIM ** -0.5 * LOG2E))

    def cmp_and_select(raw, qi, out_buf):
        s_c = raw + pltpu.roll(fc_sc[...], 2 * qi, 0)
        c_i = lax.broadcasted_iota(I32, (n_cmp, N_COLS), 0)
        q_i = lax.broadcasted_iota(I32, (n_cmp, N_COLS), 1) % QB
        p_c = _softmax2_rows(jnp.where(c_i * CMP_BLOCK + (CMP_BLOCK - 1) <= qi * QB + q_i, s_c, NEG_INF))
        oc_sc[out_buf] = _dot(vct_sc[...], p_c.astype(BF16))
        p_sum = p_c[:, 0:LANES]
        for j in range(1, GQ):
            p_sum = p_sum + p_c[:, j * LANES:(j + 1) * LANES]
        ratio = SEL_BLOCK // CMP_BLOCK
        pair = (lax.broadcasted_iota(I32, (n_sel, n_cmp), 1) // ratio
                == lax.broadcasted_iota(I32, (n_sel, n_cmp), 0)).astype(F32)
        imp = _dot(pair, p_sum, hi)
        blk = lax.broadcasted_iota(I32, (n_sel, LANES), 0)
        score = jnp.where(blk == qi, jnp.inf, jnp.where(blk < qi, imp, NEG_INF))

        def pick(_, carry):
            work, chosen = carry
            best = jnp.max(work, 0, keepdims=True)
            first = jnp.min(jnp.where(work == best, blk, n_sel), 0, keepdims=True)
            hit = blk == first
            return jnp.where(hit, NEG_INF, work), jnp.where(hit, 1.0, chosen)

        _, chosen = lax.fori_loop(0, SEL_TOPK, pick, (score, jnp.zeros((n_sel, LANES), F32)), unroll=True)
        ch_sc[out_buf, 0:8, :] = jnp.full((8, LANES), NEG_INF, F32)
        ch_sc[out_buf, 8:8 + n_sel, :] = jnp.where((score > NEG_INF) & (chosen > 0.5), 0.0, NEG_INF)

    @pl.when(pid == 0)
    def _init():
        ks_sc[0:SEL_TILE, :] = jnp.zeros((SEL_TILE, LANES), BF16)
        vst_sc[0] = jnp.zeros((LANES, SEL_TILE), BF16)
        wk_sc[0:WINDOW, :] = jnp.zeros((WINDOW, LANES), BF16)
        wv_sc[0:WINDOW, :] = jnp.zeros((WINDOW, LANES), BF16)
        rows_per = 2 * SEL_TILE
        blocks_per = rows_per // CMP_BLOCK

        def fill(c, carry):
            r0 = pl.multiple_of(c * rows_per, rows_per)
            kv = kv_ref[0, pl.ds(r0, rows_per), :]
            ks_sc[pl.ds(SEL_TILE + r0, rows_per), :] = kv[:, 2 * LANES:3 * LANES].astype(BF16)
            for half in range(2):
                v_t = kv[half * SEL_TILE:(half + 1) * SEL_TILE, 3 * LANES:4 * LANES]
                vst_sc[2 * c + half + 1] = jnp.transpose(v_t).astype(BF16)
            w = win_ref[0, pl.ds(r0, rows_per), :]
            wk_sc[pl.ds(WINDOW + r0, rows_per), :] = w[:, 0:LANES].astype(BF16)
            wv_sc[pl.ds(WINDOW + r0, rows_per), :] = w[:, LANES:2 * LANES].astype(BF16)
            pooled = jnp.sum(kv[:, 0:2 * LANES].reshape(blocks_per, CMP_BLOCK, 2 * LANES) * pw_ref[...][None],
                             axis=1)
            c0 = pl.multiple_of(c * blocks_per, blocks_per)
            kc_sc[pl.ds(c0, blocks_per), :] = pooled[:, 0:LANES].astype(BF16)
            pv_sc[pl.ds(c0, blocks_per), :] = pooled[:, LANES:2 * LANES]
            return carry

        lax.fori_loop(0, t_len // rows_per, fill, 0)
        vct_sc[...] = jnp.transpose(pv_sc[...]).astype(BF16)

        for jj, h in enumerate(PROMPT_HEADS):
            cs = slice(jj * QB, (jj + 1) * QB)
            u_i = lax.broadcasted_iota(I32, (SEL_BIAS_ROWS, QB), 0)
            q_i = lax.broadcasted_iota(I32, (SEL_BIAS_ROWS, QB), 1)
            dist = q_i - (u_i - SEL_REL0)
            near = (_bias_from_dist(dist, rb_ref, h) - rb_ref[N_BUCKETS - 1, h]) * LOG2E
            bs_sc[:, cs] = jnp.where(dist >= 0, near, NEG_INF)
            s_i = lax.broadcasted_iota(I32, (WIN_BAND, QB), 0)
            q_i = lax.broadcasted_iota(I32, (WIN_BAND, QB), 1)
            dist = q_i + WINDOW - s_i
            bw_sc[:, cs] = jnp.where((dist >= 0) & (dist <= WINDOW), _bias_from_dist(dist, rb_ref, h) * LOG2E,
                                     NEG_INF)
            c_i = lax.broadcasted_iota(I32, (n_cmp, QB), 0)
            q_i = lax.broadcasted_iota(I32, (n_cmp, QB), 1)
            rel = jnp.where(c_i < n_cmp // 2, c_i, c_i - n_cmp)
            dist = q_i - CMP_BLOCK * rel - (CMP_BLOCK - 1)
            dist = jnp.where((c_i >= 2) & (c_i < n_cmp // 2), FAR_DIST, dist)
            fc_sc[:, cs] = _bias_from_dist(dist, rb_ref, h) * LOG2E

        cmp_and_select(cmp_scores(nq_ref[0, 0:QB, :]), 0, 0)

    def q_block(t):
        i = Q_PER_STEP * pid + t
        rows = slice(t * QB, (t + 1) * QB)
        x_q = nq_ref[0, rows, :]
        x_next = nq_ref[0, (t + 1) * QB:(t + 2) * QB, :] if t + 1 < Q_PER_STEP else nqn_ref[0, 0:QB, :]
        buf = t
        o_c = oc_sc[buf]
        qbd = _block_diag_queries(x_q, QB, PROMPT_HEADS, HEAD_DIM ** -0.5 * LOG2E)
        last = i // SEL_TILE_BLOCKS

        def tile_raw(slot):
            return _dot_nt(ks_sc[pl.ds(pl.multiple_of(slot * SEL_TILE, SEL_TILE), SEL_TILE), :], qbd)

        raw_next = cmp_scores(x_next)
        w0 = pl.multiple_of(i * QB, QB)
        raw_w = _dot_nt(wk_sc[pl.ds(w0, WIN_BAND), :], qbd)
        raw_near = [tile_raw(last + 1), tile_raw(last)]

        def tile_scores(slot, bias, s=None):
            if s is None:
                s = tile_raw(slot)
            if bias is not None:
                s = s + bias
            parts = []
            for r in range(SEL_TILE_BLOCKS):
                row = ch_sc[buf, pl.ds(8 - SEL_TILE_BLOCKS + slot * SEL_TILE_BLOCKS + r, 1), :]
                parts.append(s[r * SEL_BLOCK:(r + 1) * SEL_BLOCK, :] + jnp.concatenate([row] * GQ, axis=1))
            return jnp.concatenate(parts, axis=0)

        def group_max(scores):
            return functools.reduce(jnp.maximum, [jnp.max(s, 0, keepdims=True) for s in scores])

        def sel_group(slots, scores, t_max, first):
            m_new = t_max if first else jnp.maximum(m_sc[...], t_max)
            exps = [jnp.exp2(s - m_new) for s in scores]
            l_new = functools.reduce(jnp.add, [jnp.sum(e, 0, keepdims=True) for e in exps])
            acc_new = functools.reduce(jnp.add, [_dot(vst_sc[slot], e.astype(BF16)) for slot, e in zip(slots, exps)])
            if first:
                l_sc[...] = l_new
                acc_sc[...] = acc_new
            else:
                scale = jnp.exp2(m_sc[...] - m_new)
                l_sc[...] = scale * l_sc[...] + l_new
                acc_sc[...] = scale * acc_sc[...] + acc_new
            m_sc[...] = m_new

        def far_slots(p):
            return [jnp.where(FAR_GROUP * p + 1 + n < last, FAR_GROUP * p + 1 + n, 0) for n in range(FAR_GROUP)]

        def far_scores(p):
            scores = [tile_scores(slot, None) for slot in far_slots(p)]
            for n, s in enumerate(scores):
                fs_sc[p % 2, n] = s
            ft_sc[p % 2] = group_max(scores)

        far_scores(0)
        cmp_and_select(raw_next, i + 1, 1 - buf)

        in_seq = lax.broadcasted_iota(I32, (WIN_BAND, N_COLS), 0) >= WINDOW - i * QB
        p_w = _softmax2_rows(jnp.where(in_seq, raw_w + bw_sc[...], NEG_INF))
        o_w = _dot_tn(wv_sc[pl.ds(w0, WIN_BAND), :], p_w.astype(BF16))

        u0 = pl.multiple_of(SEL_REL0 - SEL_TILE - SEL_BLOCK * (i % SEL_TILE_BLOCKS), SEL_BLOCK)
        near = [tile_scores(last + 1, bs_sc[pl.ds(u0 + SEL_TILE, SEL_TILE), :], raw_near[0]),
                tile_scores(last, bs_sc[pl.ds(u0, SEL_TILE), :], raw_near[1])]
        sel_group((last + 1, last), near, group_max(near), True)

        def far_group(p, carry):
            sel_group(far_slots(p), [fs_sc[p % 2, n] for n in range(FAR_GROUP)], ft_sc[p % 2], False)
            far_scores(p + 1)
            return carry

        lax.fori_loop(0, (last - 1 + FAR_GROUP - 1) // FAR_GROUP, far_group, 0)
        o_s = acc_sc[...] / jnp.maximum(l_sc[...], 1e-30)

        eye = (lax.broadcasted_iota(I32, (3 * N_NSA_HEADS + 8, LANES), 0) + G_NSA
               == lax.broadcasted_iota(I32, (3 * N_NSA_HEADS + 8, LANES), 1)).astype(F32)
        gates = jax.nn.sigmoid(_dot_nt(eye, zg_ref[0, rows, :], hi))
        pieces = []
        for h in range(N_NSA_HEADS):
            g = h // GQ
            jj = PROMPT_HEADS.index(h)
            part = lambda o: o[g * HEAD_DIM:(g + 1) * HEAD_DIM, jj * QB:(jj + 1) * QB]
            pieces.append(gates[h:h + 1, :] * part(o_c)
                          + gates[N_NSA_HEADS + h:N_NSA_HEADS + h + 1, :] * part(o_s)
                          + gates[2 * N_NSA_HEADS + h:2 * N_NSA_HEADS + h + 1, :] * part(o_w))
        mix_t = jnp.concatenate(pieces, axis=0)
        mix_t = jnp.concatenate([mix_t, jnp.zeros((NSA_W, LANES - QB), F32)], axis=1)
        out_ref[0, rows, :] = jnp.transpose(mix_t)[0:QB, :]

    for t in range(Q_PER_STEP):
        q_block(t)


def _nsa_prompt(nq, kv, win, zg, pool_w2, rel_bias):
    bsz, t_len, _ = nq.shape
    assert t_len // CMP_BLOCK == LANES and t_len % (2 * SEL_TILE) == 0
    n_cmp = t_len // CMP_BLOCK
    per_b = lambda w: pl.BlockSpec((1, t_len, w), lambda b, i: (b, 0, 0))
    step_rows = Q_PER_STEP * NSA_QBLOCK
    blk = lambda w: pl.BlockSpec((1, step_rows, w), lambda b, i: (b, i, 0))
    n_qb = t_len // step_rows
    next_blk = pl.BlockSpec((1, step_rows, NSA_W), lambda b, i: (b, jnp.minimum(i + 1, n_qb - 1), 0))
    return pl.pallas_call(
        functools.partial(_nsa_prompt_body, t_len=t_len),
        grid=(bsz, n_qb),
        in_specs=[pl.BlockSpec(memory_space=pltpu.SMEM),
                  blk(NSA_W), next_blk, per_b(4 * KV_W), per_b(2 * KV_W), blk(LANES),
                  pl.BlockSpec((CMP_BLOCK, 2 * LANES), lambda b, i: (0, 0))],
        out_specs=blk(NSA_W),
        out_shape=jax.ShapeDtypeStruct((bsz, t_len, NSA_W), F32),
        scratch_shapes=[pltpu.VMEM((SEL_TILE + t_len, LANES), BF16),
                        pltpu.VMEM((1 + t_len // SEL_TILE, LANES, SEL_TILE), BF16),
                        pltpu.VMEM((WINDOW + t_len, LANES), BF16),
                        pltpu.VMEM((WINDOW + t_len, LANES), BF16),
                        pltpu.VMEM((n_cmp, LANES), BF16),
                        pltpu.VMEM((n_cmp, LANES), F32),
                        pltpu.VMEM((LANES, n_cmp), BF16),
                        pltpu.VMEM((SEL_BIAS_ROWS, N_COLS), F32),
                        pltpu.VMEM((WIN_BAND, N_COLS), F32),
                        pltpu.VMEM((n_cmp, N_COLS), F32),
                        pltpu.VMEM((1, N_COLS), F32),
                        pltpu.VMEM((1, N_COLS), F32),
                        pltpu.VMEM((LANES, N_COLS), F32),
                        pltpu.VMEM((2, 8 + t_len // SEL_BLOCK, LANES), F32),
                        pltpu.VMEM((2, LANES, N_COLS), F32),
                        pltpu.VMEM((2, FAR_GROUP, SEL_TILE, N_COLS), F32),
                        pltpu.VMEM((2, 1, N_COLS), F32)],
        compiler_params=pltpu.CompilerParams(dimension_semantics=("arbitrary", "arbitrary"),
                                             vmem_limit_bytes=VMEM_LIMIT),
        name="nsa_prompt",
    )(rel_bias, nq, nq, kv, win, zg, pool_w2)


CMP_PAGES = 32
CMP_SLOTS = 3
N_PICK = SEL_TOPK - 1
N_OWNERS = N_KV_HEADS * N_NEW
N_FETCH = N_OWNERS * N_PICK
OWNER_KEYS = N_PICK * PAGE_SIZE


def _sample_row_ids():
    r = lax.broadcasted_iota(I32, (N_QROWS, 1), 0)
    return r % SEQ_PAD - (SEQ_PAD - N_NEW), r // SEQ_PAD


def _sample_bias(dist, rb_ref):
    _, head = _sample_row_ids()
    col = lambda k: sum(jnp.where(head == h, rb_ref[k, h], 0.0) for h in range(N_NSA_HEADS))
    acc = jnp.broadcast_to(col(0), dist.shape)
    for k in range(1, N_BUCKETS):
        acc = jnp.where(dist >= BUCKET_THR[k], col(k), acc)
    return acc


def _sample_bias_table(rb_ref):
    assert FAR_DIST < LANES
    return _sample_bias(lax.broadcasted_iota(I32, (N_QROWS, LANES), 1), rb_ref)


def _sample_bias_lookup(dist, table):
    idx = jnp.clip(dist, 0, LANES - 1)
    return jnp.concatenate([jnp.take_along_axis(table, idx[:, c * LANES:(c + 1) * LANES], axis=1)
                            for c in range(dist.shape[1] // LANES)], axis=1)


def _softmax_two(s1, s2):
    mx = jnp.maximum(jnp.max(s1, -1, keepdims=True), jnp.max(s2, -1, keepdims=True))
    mx = jnp.where(mx > NEG_INF, mx, 0.0)
    e1 = jnp.exp(s1 - mx)
    e2 = jnp.exp(s2 - mx)
    den = jnp.maximum(jnp.sum(e1, -1, keepdims=True) + jnp.sum(e2, -1, keepdims=True), 1e-30)
    return e1 / den, e2 / den


def _nsa_cmp_body(pt_ref, rb_ref, nq_ref, pwt_ref, cache_ref, oc_ref, idx_ref,
                  buf, sem, seg_sc, kct_sc, vct_sc, *, n_pages, past_len):
    b = pl.program_id(0)
    nb = pl.num_programs(0)
    n_chunks = n_pages // CMP_PAGES
    chunk_rows = CMP_PAGES * PAGE_SIZE
    chunk_blocks = chunk_rows // CMP_BLOCK
    n_cmp = past_len // CMP_BLOCK
    n_sel_past = past_len // SEL_BLOCK
    n_prob = N_KV_HEADS * SEQ_PAD
    total = nb * n_chunks
    hi = lax.Precision.HIGHEST

    def page_copy(gi, p):
        page = pt_ref[(gi // n_chunks) * n_pages + (gi % n_chunks) * CMP_PAGES + p]
        slot = gi % CMP_SLOTS
        return pltpu.make_async_copy(cache_ref.at[page, pl.ds(0, 2 * LANES), :], buf.at[slot, p], sem.at[slot])

    def start_chunk(gi):
        for p in range(CMP_PAGES):
            page_copy(gi, p).start()

    def wait_chunk(gi):
        for p in range(CMP_PAGES):
            page_copy(gi, p).wait()

    @pl.when(b == 0)
    def _prologue():
        seg_sc[...] = (lax.broadcasted_iota(I32, (chunk_rows, chunk_blocks), 0) // CMP_BLOCK
                       == lax.broadcasted_iota(I32, (chunk_rows, chunk_blocks), 1)).astype(BF16)
        for gi in range(min(CMP_SLOTS - 1, total)):
            start_chunk(gi)

    def chunk_step(c, carry):
        gi = b * n_chunks + c
        ahead = gi + (CMP_SLOTS - 1)

        @pl.when(ahead < total)
        def _prefetch():
            start_chunk(ahead)

        wait_chunk(gi)
        slot = gi % CMP_SLOTS
        y = jnp.concatenate([(buf[slot, p] * pwt_ref[...]).astype(BF16) for p in range(CMP_PAGES)], axis=1)
        pooled = _dot(y, seg_sc[...])
        kct_sc[c] = pooled[0:LANES].astype(BF16)
        vct_sc[c] = pooled[LANES:2 * LANES].astype(BF16)
        return carry

    lax.fori_loop(0, n_chunks, chunk_step, 0)

    qbd = _block_diag_queries(nq_ref[0], SEQ_PAD)
    tok, _ = _sample_row_ids()
    s_c = jnp.concatenate([_dot(qbd, kct_sc[c]) for c in range(n_chunks)], axis=1)
    near0 = n_cmp - LANES
    assert past_len - (SEQ_PAD - N_NEW) - (CMP_BLOCK * (near0 - 1) + CMP_BLOCK - 1) >= FAR_DIST
    dist = past_len + tok - ((lax.broadcasted_iota(I32, (N_QROWS, LANES), 1) + near0) * CMP_BLOCK + CMP_BLOCK - 1)
    near = jnp.where(dist >= 0, s_c[:, near0:] + _sample_bias(dist, rb_ref), NEG_INF)
    far = s_c[:, :near0] + _sample_bias(jnp.full((N_QROWS, 1), FAR_DIST, I32), rb_ref)
    p_c = _masked_softmax(jnp.concatenate([far, near], axis=1))
    o_c = _dot_nt(p_c[:, 0:chunk_blocks].astype(BF16), vct_sc[0])
    for c in range(1, n_chunks):
        o_c = o_c + _dot_nt(p_c[:, c * chunk_blocks:(c + 1) * chunk_blocks].astype(BF16), vct_sc[c])
    oc_ref[0] = o_c

    p_sum = jnp.sum(p_c.reshape(N_KV_HEADS, GQ, SEQ_PAD, n_cmp), axis=1).reshape(n_prob, n_cmp)
    p_sum = jnp.concatenate([p_sum, jnp.zeros((LANES - n_prob, n_cmp), F32)], axis=0)
    ratio = SEL_BLOCK // CMP_BLOCK
    pair = (lax.broadcasted_iota(I32, (n_sel_past, n_cmp), 1) // ratio
            == lax.broadcasted_iota(I32, (n_sel_past, n_cmp), 0)).astype(F32)
    imp = _dot_nt(pair, p_sum, hi)
    blk = lax.broadcasted_iota(I32, (n_sel_past, LANES), 0)
    pick_i = lax.broadcasted_iota(I32, (SEL_TOPK, LANES), 0)

    def pick(k, carry):
        work, ids = carry
        best = jnp.max(work, 0, keepdims=True)
        first = jnp.min(jnp.where(work == best, blk, n_sel_past), 0, keepdims=True)
        return jnp.where(blk == first, NEG_INF, work), jnp.where(pick_i == k, first, ids)

    _, ids = lax.fori_loop(0, N_PICK, pick, (imp, jnp.zeros((SEL_TOPK, LANES), I32)), unroll=True)
    idx_ref[0] = ids


def _nsa_cmp(page_table, rel_bias, nq_s, pool_wt, cache3):
    dec_b, n_pages = page_table.shape
    past_len = n_pages * PAGE_SIZE
    assert n_pages % CMP_PAGES == 0 and past_len // SEL_BLOCK >= N_PICK
    n_chunks = n_pages // CMP_PAGES
    chunk_rows = CMP_PAGES * PAGE_SIZE
    chunk_blocks = chunk_rows // CMP_BLOCK
    assert chunk_blocks == LANES
    grid_spec = pltpu.PrefetchScalarGridSpec(
        num_scalar_prefetch=1,
        grid=(dec_b,),
        in_specs=[pl.BlockSpec(memory_space=pltpu.SMEM),
                  pl.BlockSpec((1, SEQ_PAD, NSA_W), lambda b, pt: (b, 0, 0)),
                  pl.BlockSpec((2 * LANES, PAGE_SIZE), lambda b, pt: (0, 0)),
                  pl.BlockSpec(memory_space=pl.ANY)],
        out_specs=[pl.BlockSpec((1, N_QROWS, LANES), lambda b, pt: (b, 0, 0)),
                   pl.BlockSpec((1, SEL_TOPK, LANES), lambda b, pt: (b, 0, 0))],
        scratch_shapes=[pltpu.VMEM((CMP_SLOTS, CMP_PAGES, 2 * LANES, PAGE_SIZE), F32),
                        pltpu.SemaphoreType.DMA((CMP_SLOTS,)),
                        pltpu.VMEM((chunk_rows, chunk_blocks), BF16),
                        pltpu.VMEM((n_chunks, LANES, chunk_blocks), BF16),
                        pltpu.VMEM((n_chunks, LANES, chunk_blocks), BF16)],
    )
    return pl.pallas_call(
        functools.partial(_nsa_cmp_body, n_pages=n_pages, past_len=past_len),
        grid_spec=grid_spec,
        out_shape=[jax.ShapeDtypeStruct((dec_b, N_QROWS, LANES), F32),
                   jax.ShapeDtypeStruct((dec_b, SEL_TOPK, LANES), I32)],
        compiler_params=pltpu.CompilerParams(dimension_semantics=("arbitrary",),
                                             vmem_limit_bytes=VMEM_LIMIT),
        name="nsa_sample_cmp",
    )(page_table.reshape(-1), rel_bias, nq_s, pool_wt, cache3)


def _nsa_sel_body(pt_ref, ids_ref, rb_ref, nq_ref, kvn_ref, winc_ref, winn_ref, zg_ref, oc_ref, idv_ref,
                  cache_ref, out_ref, buf, sem, *, n_pages, past_len):
    b = pl.program_id(0)
    nb = pl.num_programs(0)
    hi = lax.Precision.HIGHEST

    def block_copy(bb, n, slot):
        blk = ids_ref[bb * N_FETCH + n]
        page = pt_ref[bb * n_pages + blk // 2]
        return pltpu.make_async_copy(cache_ref.at[page, pl.ds(2 * LANES, 2 * LANES), :], buf.at[slot, n],
                                     sem.at[slot])

    def start_all(bb, slot):
        def go(n, carry):
            block_copy(bb, n, slot).start()
            return carry
        lax.fori_loop(0, N_FETCH, go, 0, unroll=8)

    def wait_all(bb, slot):
        def go(n, carry):
            block_copy(bb, n, slot).wait()
            return carry
        lax.fori_loop(0, N_FETCH, go, 0, unroll=8)

    slot = b % 2

    @pl.when(b == 0)
    def _prologue():
        start_all(0, 0)

    @pl.when(b + 1 < nb)
    def _prefetch():
        start_all(b + 1, 1 - slot)

    qbd = _block_diag_queries(nq_ref[0], SEQ_PAD)
    tok, head = _sample_row_ids()
    owner = (head // GQ) * N_NEW + tok

    wc = winc_ref[0]
    wn = winn_ref[0]
    r_i = lax.broadcasted_iota(I32, (N_QROWS, WINDOW), 1)
    dist1 = WINDOW + tok - r_i
    bias_table = _sample_bias_table(rb_ref)
    s1 = _dot(qbd, wc[0:LANES, :].astype(BF16)) + _sample_bias_lookup(dist1, bias_table)
    s1 = jnp.where((dist1 >= 0) & (dist1 <= WINDOW), s1, NEG_INF)
    n_i = lax.broadcasted_iota(I32, (N_QROWS, SEQ_PAD), 1) - (SEQ_PAD - N_NEW)
    dist2 = tok - n_i
    new_ok = (n_i >= 0) & (dist2 >= 0)
    bias2 = _sample_bias(dist2, rb_ref)
    s2 = jnp.where(new_ok, _dot_nt(qbd, wn[:, 0:LANES].astype(BF16)) + bias2, NEG_INF)
    p1, p2 = _softmax_two(s1, s2)
    o_w = (_dot_nt(p1.astype(BF16), wc[LANES:2 * LANES, :].astype(BF16))
           + _dot(p2.astype(BF16), wn[:, LANES:2 * LANES].astype(BF16)))

    wait_all(b, slot)

    def owner_tiles(o, r0):
        return jnp.concatenate([buf[slot, o * N_PICK + k, r0:r0 + LANES, :] for k in range(N_PICK)],
                               axis=1).astype(BF16)

    s_f = jnp.full((N_QROWS, OWNER_KEYS), NEG_INF, F32)
    for o in range(N_OWNERS):
        s_f = jnp.where(owner == o, _dot(qbd, owner_tiles(o, 0)), s_f)
    spread = (lax.broadcasted_iota(I32, (SEL_TOPK, OWNER_KEYS), 0)
              == lax.broadcasted_iota(I32, (SEL_TOPK, OWNER_KEYS), 1) // PAGE_SIZE).astype(F32)
    blk_of_key = _dot_tn(idv_ref[0].astype(F32), spread, hi)
    blk_rows = jnp.concatenate([blk_of_key[(h // GQ) * SEQ_PAD:(h // GQ + 1) * SEQ_PAD]
                                for h in range(N_NSA_HEADS)], axis=0).astype(I32)
    blocks_per_page = PAGE_SIZE // SEL_BLOCK
    in_page = lax.broadcasted_iota(I32, (N_QROWS, OWNER_KEYS), 1) % PAGE_SIZE
    in_block = in_page // SEL_BLOCK == blk_rows % blocks_per_page
    dist_s = past_len + tok - ((blk_rows // blocks_per_page) * PAGE_SIZE + in_page)
    s_f = jnp.where((tok >= 0) & in_block & (dist_s >= 0), s_f + _sample_bias_lookup(dist_s, bias_table), NEG_INF)
    kvn = kvn_ref[0]
    s_n = jnp.where(new_ok, _dot_nt(qbd, kvn[:, 2 * LANES:3 * LANES].astype(BF16)) + bias2, NEG_INF)
    p_f, p_n = _softmax_two(s_f, s_n)
    o_s = _dot(p_n.astype(BF16), kvn[:, 3 * LANES:4 * LANES].astype(BF16))
    for o in range(N_OWNERS):
        o_s = o_s + _dot_nt(jnp.where(owner == o, p_f, 0.0).astype(BF16), owner_tiles(o, LANES))

    o_c = oc_ref[0]
    gates = jax.nn.sigmoid(zg_ref[0])
    is_new = lax.broadcasted_iota(I32, (SEQ_PAD, HEAD_DIM), 0) >= SEQ_PAD - N_NEW
    for h in range(N_NSA_HEADS):
        mix = _gated_mix(gates, o_c, o_s, o_w, h, SEQ_PAD)
        out_ref[0, :, h * HEAD_DIM:(h + 1) * HEAD_DIM] = jnp.where(is_new, mix, 0.0)


def _nsa_sel(page_table, ids, ids_rows, rel_bias, nq_s, kv_s, win_cache, win_s, zg_s, o_c, cache3):
    dec_b, n_pages = page_table.shape
    past_len = n_pages * PAGE_SIZE
    seq = lambda w: pl.BlockSpec((1, SEQ_PAD, w), lambda b, pt, ix: (b, 0, 0))
    grid_spec = pltpu.PrefetchScalarGridSpec(
        num_scalar_prefetch=2,
        grid=(dec_b,),
        in_specs=[pl.BlockSpec(memory_space=pltpu.SMEM),
                  seq(NSA_W), seq(4 * KV_W),
                  pl.BlockSpec((1, 2 * KV_W, WINDOW), lambda b, pt, ix: (b, 0, 0)),
                  seq(2 * KV_W), seq(LANES),
                  pl.BlockSpec((1, N_QROWS, LANES), lambda b, pt, ix: (b, 0, 0)),
                  pl.BlockSpec((1, SEL_TOPK, LANES), lambda b, pt, ix: (b, 0, 0)),
                  pl.BlockSpec(memory_space=pl.ANY)],
        out_specs=seq(NSA_W),
        scratch_shapes=[pltpu.VMEM((2, N_FETCH, 2 * LANES, PAGE_SIZE), F32),
                        pltpu.SemaphoreType.DMA((2,))],
    )
    return pl.pallas_call(
        functools.partial(_nsa_sel_body, n_pages=n_pages, past_len=past_len),
        grid_spec=grid_spec,
        out_shape=jax.ShapeDtypeStruct((dec_b, SEQ_PAD, NSA_W), F32),
        compiler_params=pltpu.CompilerParams(dimension_semantics=("arbitrary",),
                                             vmem_limit_bytes=VMEM_LIMIT),
        name="nsa_sample_sel",
    )(page_table.reshape(-1), ids.reshape(-1), rel_bias, nq_s, kv_s, win_cache, win_s, zg_s, o_c, ids_rows, cache3)


FF_CHUNK = 896


def _ffn_body(*refs, tm, has_override):
    if has_override:
        (x_ref, mm_ref, mn_ref, ov_ref, woa_ref, wob_ref, l1w_ref, l1b_ref, wup_ref, bup_ref, cw_ref, cb_ref,
         wdn_ref, bdn_ref, l2w_ref, l2b_ref, y_ref, tail_ref, carry_sc, ubuf) = refs
    else:
        (x_ref, mm_ref, mn_ref, woa_ref, wob_ref, l1w_ref, l1b_ref, wup_ref, bup_ref, cw_ref, cb_ref,
         wdn_ref, bdn_ref, l2w_ref, l2b_ref, y_ref, tail_ref, carry_sc, ubuf) = refs
    j = pl.program_id(1)

    @pl.when(j == 0)
    def _init():
        carry_sc[...] = jnp.zeros(carry_sc.shape, F32)

    x = x_ref[0]
    h = ALPHA * x + (_dot(mm_ref[0].astype(BF16), woa_ref[...]) + _dot(mn_ref[0].astype(BF16), wob_ref[...]))
    x1 = _layer_norm(h, l1w_ref[...], l1b_ref[...])
    x1b = x1.astype(BF16)
    if has_override:
        state_row = lax.broadcasted_iota(I32, (tm, FF_CHUNK), 0) % SEQ_PAD < SEQ_PAD - N_NEW

    def conv_half(c0):
        cols = slice(c0, c0 + FF_CHUNK)
        u = _dot(x1b, wup_ref[:, cols]) + bup_ref[:, cols]
        if has_override:
            u = jnp.where(state_row, ov_ref[0, :, cols], u)
        ubuf[0:8, :] = carry_sc[:, cols]
        ubuf[8:8 + tm, :] = u
        out = cb_ref[:, cols] + ubuf[6:6 + tm, :] * cw_ref[0:1, cols]
        out = out + ubuf[7:7 + tm, :] * cw_ref[1:2, cols]
        out = out + u * cw_ref[2:3, cols]
        carry_sc[:, cols] = u[tm - 8:tm, :]
        tail_ref[0, 0, :, cols] = u if has_override else u[tm - 8:tm, :]
        return out

    ff = jnp.zeros((tm, D_MODEL), F32)
    for c in range(D_FF // FF_CHUNK):
        ga = conv_half(c * FF_CHUNK)
        gb = conv_half(D_FF + c * FF_CHUNK)
        gelu = ga * (0.5 * (1.0 + jnp.tanh(math.sqrt(2.0 / math.pi) * (ga + 0.044715 * (ga * ga * ga)))))
        ff = ff + _dot((gelu * gb).astype(BF16), wdn_ref[c * FF_CHUNK:(c + 1) * FF_CHUNK, :])
    ff = ff + bdn_ref[...]
    y_ref[0] = _layer_norm(ALPHA * x1 + ff, l2w_ref[...], l2b_ref[...])


def _ffn(x, mix_m, mix_n, override, wts):
    bsz, t_len, _ = x.shape
    tm = min(t_len, 512)
    nt = t_len // tm
    has_override = override is not None
    tail_rows = tm if has_override else 8
    row = lambda w: pl.BlockSpec((1, tm, w), lambda b, j: (b, j, 0))
    const = lambda shape: pl.BlockSpec(shape, lambda b, j: (0,) * len(shape), pipeline_mode=pl.Buffered(1))
    in_specs = [row(D_MODEL), row(MLSTM_W), row(NSA_W)]
    args = [x, mix_m, mix_n]
    if has_override:
        in_specs.append(row(2 * D_FF))
        args.append(override)
    in_specs += [const(w.shape) for w in wts]
    args += list(wts)
    y, tail = pl.pallas_call(
        functools.partial(_ffn_body, tm=tm, has_override=has_override),
        grid=(bsz, nt),
        in_specs=in_specs,
        out_specs=[row(D_MODEL), pl.BlockSpec((1, 1, tail_rows, 2 * D_FF), lambda b, j: (b, j, 0, 0))],
        out_shape=[jax.ShapeDtypeStruct((bsz, t_len, D_MODEL), F32),
                   jax.ShapeDtypeStruct((bsz, nt, tail_rows, 2 * D_FF), F32)],
        scratch_shapes=[pltpu.VMEM((8, 2 * D_FF), F32), pltpu.VMEM((8 + tm, FF_CHUNK), F32)],
        compiler_params=pltpu.CompilerParams(dimension_semantics=("arbitrary", "arbitrary"),
                                             vmem_limit_bytes=VMEM_LIMIT),
        name="ffn",
    )(*args)
    return y, tail


def _permute_in_proj(w_in, b_in):
    gates_a = slice(2 * MLSTM_W + 2 * MLSTM_W, 2 * MLSTM_W + 2 * MLSTM_W + 2 * N_MLSTM_HEADS)
    nsa0 = gates_a.stop
    nsa1 = nsa0 + NSA_W + 6 * KV_W

    def perm(a):
        pad = IN_PAD - a.shape[-1]
        parts = [a[..., :gates_a.start], a[..., nsa0:nsa1], a[..., gates_a], a[..., nsa1:]]
        parts.append(jnp.zeros(a.shape[:-1] + (pad,), a.dtype))
        return jnp.concatenate(parts, axis=-1)

    return perm(w_in).astype(BF16), perm(b_in).reshape(1, IN_PAD)


def _front_pad(a, n):
    return jnp.pad(a, ((0, 0), (n, 0), (0, 0)))


def kernel(x_prompt, x_sample, cache_nsa_kv, cache_win_kv, state_mlstm_c, state_mlstm_n, state_mlstm_m,
           state_mlstm_conv, state_ffn_conv, page_table,
           w_in, b_in, mlstm_conv_w, mlstm_conv_b, mlstm_norm_w, nsa_pool_w, rel_bias, w_out,
           ln1_w, ln1_b, w_up, b_up, ffn_conv_w, ffn_conv_b, w_down, b_down, ln2_w, ln2_b):
    bsz, t_len, _ = x_prompt.shape
    dec_b, n_new, _ = x_sample.shape
    assert n_new == N_NEW and cache_win_kv.shape[1] == WINDOW

    w_p, b_p = _permute_in_proj(w_in, b_in)
    pool_w2 = jnp.concatenate([jnp.repeat(nsa_pool_w[0], HEAD_DIM, axis=-1),
                               jnp.repeat(nsa_pool_w[1], HEAD_DIM, axis=-1)], axis=-1)
    row = lambda v: v.reshape(1, -1)
    ffn_w = (w_out[:MLSTM_W].astype(BF16), w_out[MLSTM_W:].astype(BF16), row(ln1_w), row(ln1_b),
             w_up.astype(BF16), row(b_up), ffn_conv_w, row(ffn_conv_b), w_down.astype(BF16), row(b_down),
             row(ln2_w), row(ln2_b))

    zqk, zv, zo, nq, kv, win, zg, kv_t = _proj_in(x_prompt.reshape(bsz * t_len, D_MODEL), w_p, b_p, seq_len=t_len)
    seq3 = lambda a: a.reshape(bsz, t_len, a.shape[-1])
    zqk, zv, zo, nq, kv, win, zg = map(seq3, (zqk, zv, zo, nq, kv, win, zg))
    mix_m, c_p, n_p, m_p = _mlstm(zqk, zv, zo, zg, None, mlstm_conv_w, mlstm_conv_b, mlstm_norm_w, 0)
    mix_n = _nsa_prompt(nq, kv, win, zg, pool_w2, rel_bias)
    y_p, tail_p = _ffn(x_prompt, mix_m, mix_n, None, ffn_w)
    kv_p = jnp.transpose(kv_t.reshape(bsz, 4, N_KV_HEADS, HEAD_DIM, t_len), (0, 4, 1, 2, 3))
    win_p = win[:, t_len - min(WINDOW, t_len):].reshape(bsz, -1, 2, N_KV_HEADS, HEAD_DIM)
    mconv_p = zqk[:, t_len - (MLSTM_CONV - 1):]
    fconv_p = tail_p[:, -1, 8 - (FFN_CONV - 1):]

    pad = SEQ_PAD - N_NEW
    xs = _front_pad(x_sample, pad)
    szqk, szv, szo, snq, skv, swin, szg = _proj_in(xs.reshape(dec_b * SEQ_PAD, D_MODEL), w_p, b_p)
    sseq = lambda a: a.reshape(dec_b, SEQ_PAD, a.shape[-1])
    szqk, szv, szo, snq, skv, swin, szg = map(sseq, (szqk, szv, szo, snq, skv, swin, szg))
    lead = MLSTM_CHUNK - N_NEW
    new = lambda a: a[:, pad:]
    m_qk = jnp.concatenate([jnp.zeros((dec_b, lead - (MLSTM_CONV - 1), 2 * MLSTM_W), F32),
                            state_mlstm_conv.astype(F32), new(szqk)], axis=1)
    mix_ms, c_s, n_s, m_s = _mlstm(m_qk, _front_pad(new(szv), lead), _front_pad(new(szo), lead),
                                   _front_pad(new(szg), lead),
                                   (state_mlstm_c, state_mlstm_n, state_mlstm_m),
                                   mlstm_conv_w, mlstm_conv_b, mlstm_norm_w, lead)
    mix_ms = mix_ms[:, MLSTM_CHUNK - SEQ_PAD:]
    cache3 = jnp.transpose(cache_nsa_kv, (0, 2, 3, 4, 1)).reshape(cache_nsa_kv.shape[0], 4 * KV_W, PAGE_SIZE)
    pool_wt = jnp.tile(jnp.repeat(jnp.transpose(nsa_pool_w, (0, 2, 1)).reshape(2 * N_KV_HEADS, CMP_BLOCK),
                                  HEAD_DIM, axis=0), (1, PAGE_SIZE // CMP_BLOCK))
    o_c, ids_rows = _nsa_cmp(page_table, rel_bias, snq, pool_wt, cache3)
    ids = jnp.transpose(ids_rows[:, :N_PICK, :N_KV_HEADS * SEQ_PAD], (0, 2, 1))
    ids = ids.reshape(dec_b, N_KV_HEADS, SEQ_PAD, N_PICK)[:, :, pad:]
    win_cache = jnp.transpose(cache_win_kv, (0, 2, 3, 4, 1)).reshape(dec_b, 2 * KV_W, WINDOW)
    mix_ns = _nsa_sel(page_table, ids, ids_rows, rel_bias, snq, skv, win_cache, swin, szg, o_c, cache3)
    override = jnp.concatenate([jnp.zeros((dec_b, pad - (FFN_CONV - 1), 2 * D_FF), F32),
                                state_ffn_conv.astype(F32),
                                jnp.zeros((dec_b, N_NEW, 2 * D_FF), F32)], axis=1)
    flat = lambda a: a.reshape(1, dec_b * SEQ_PAD, a.shape[-1])
    y_s8, u_s = _ffn(flat(xs), flat(mix_ms), flat(mix_ns), flat(override), ffn_w)
    y_s = y_s8.reshape(dec_b, SEQ_PAD, D_MODEL)[:, pad:]
    kv_s = new(skv).reshape(dec_b, N_NEW, 4, N_KV_HEADS, HEAD_DIM)
    win_s = jnp.concatenate([cache_win_kv[:, N_NEW:], new(swin).reshape(dec_b, N_NEW, 2, N_KV_HEADS, HEAD_DIM)],
                            axis=1)
    mconv_s = szqk[:, SEQ_PAD - (MLSTM_CONV - 1):]
    fconv_s = u_s.reshape(dec_b, SEQ_PAD, 2 * D_FF)[:, SEQ_PAD - (FFN_CONV - 1):]
    return (y_p, y_s, kv_p, kv_s, win_p, win_s, c_p, c_s, n_p, n_s, m_p, m_s, mconv_p, mconv_s, fconv_p, fconv_s)
```

```python
import functools
import math

import jax
import jax.numpy as jnp
from jax import lax
from jax.experimental import pallas as pl
from jax.experimental.pallas import tpu as pltpu

F32 = jnp.float32
BF16 = jnp.bfloat16
I32 = jnp.int32

D_MODEL = 1024
HEAD_DIM = 64
N_MLSTM_HEADS = 8
N_NSA_HEADS = 8
N_KV_HEADS = 2
GQ = N_NSA_HEADS // N_KV_HEADS
MLSTM_W = N_MLSTM_HEADS * HEAD_DIM
NSA_W = N_NSA_HEADS * HEAD_DIM
KV_W = N_KV_HEADS * HEAD_DIM
MLSTM_CHUNK = 64
MLSTM_CONV = 4
CMP_BLOCK = 32
SEL_BLOCK = 64
SEL_TOPK = 16
WINDOW = 512
NSA_QBLOCK = 64
N_BUCKETS = 32
MAX_DISTANCE = 128
D_FF = 2688
FFN_CONV = 3
DEPTH = 1
ALPHA = (2.0 * DEPTH) ** 0.25
LN_EPS = 1e-5
PAGE_SIZE = 128

LANES = 128
VMEM_LIMIT = 56 * 1024 * 1024
NEG_INF = float("-inf")

SEQ_PAD = 8
N_NEW = 4
N_QROWS = N_NSA_HEADS * SEQ_PAD


def _bucket_thresholds():
    max_exact = N_BUCKETS // 2
    thr = list(range(max_exact + 1))
    for k in range(1, N_BUCKETS - max_exact):
        thr.append(int(math.ceil(max_exact * (MAX_DISTANCE / max_exact) ** (k / (N_BUCKETS - max_exact)))))
    return tuple(thr)


BUCKET_THR = _bucket_thresholds()
FAR_DIST = BUCKET_THR[N_BUCKETS - 1]


def _dot(a, b, precision=None):
    return jnp.dot(a, b, preferred_element_type=F32, precision=precision)


def _dot_nt(a, b, precision=None):
    return lax.dot_general(a, b, (((1,), (1,)), ((), ())), preferred_element_type=F32, precision=precision)


def _dot_tn(a, b, precision=None):
    return lax.dot_general(a, b, (((0,), (0,)), ((), ())), preferred_element_type=F32, precision=precision)


def _layer_norm(x, w, b):
    mu = jnp.mean(x, -1, keepdims=True)
    xc = x - mu
    var = jnp.mean(xc * xc, -1, keepdims=True)
    return xc * lax.rsqrt(var + LN_EPS) * w + b


def _log_sigmoid(x):
    return jnp.minimum(x, 0.0) - jnp.log1p(jnp.exp(-jnp.abs(x)))


def _bias_from_dist(dist, rb_ref, head):
    acc = jnp.full(dist.shape, rb_ref[0, head], F32)
    for k in range(1, N_BUCKETS):
        acc = jnp.where(dist >= BUCKET_THR[k], rb_ref[k, head], acc)
    return acc


def _masked_softmax(s):
    mx = jnp.max(s, -1, keepdims=True)
    mx = jnp.where(mx > NEG_INF, mx, 0.0)
    e = jnp.exp(s - mx)
    den = jnp.sum(e, -1, keepdims=True)
    return e / jnp.maximum(den, 1e-30)


_C_QK, _C_V, _C_O, _C_NQ, _C_KV, _C_WIN, _C_G = 0, 1024, 1536, 2048, 2560, 3072, 3328
IN_PAD = 3456
G_IGATE, G_FGATE, G_NSA = 0, N_MLSTM_HEADS, 2 * N_MLSTM_HEADS


def _proj_in_body(*refs, kv_transposed):
    if kv_transposed:
        x_ref, w_ref, b_ref, wkvt_ref, bkv_ref, qk_ref, v_ref, o_ref, nq_ref, kv_ref, win_ref, g_ref, kvt_ref = refs
    else:
        x_ref, w_ref, b_ref, qk_ref, v_ref, o_ref, nq_ref, kv_ref, win_ref, g_ref = refs
    xb = x_ref[...].astype(BF16)
    z = _dot(xb, w_ref[...]) + b_ref[...]
    qk_ref[...] = z[:, _C_QK:_C_V]
    v_ref[...] = z[:, _C_V:_C_O]
    o_ref[...] = z[:, _C_O:_C_NQ]
    nq_ref[...] = z[:, _C_NQ:_C_KV]
    kv_ref[...] = z[:, _C_KV:_C_WIN]
    win_ref[...] = z[:, _C_WIN:_C_G]
    g_ref[...] = z[:, _C_G:IN_PAD]
    if kv_transposed:
        kvt_ref[0] = _dot_nt(wkvt_ref[...], xb) + bkv_ref[...]


def _proj_in(x2d, w_p, b_p, seq_len=None):
    rows = x2d.shape[0]
    tm = min(rows, 512)
    widths = (1024, 512, 512, 512, 512, 256, 128)
    const = lambda i: (0, 0)
    const_spec = lambda shape: pl.BlockSpec(shape, const, pipeline_mode=pl.Buffered(1))
    in_specs = [pl.BlockSpec((tm, D_MODEL), lambda i: (i, 0)), const_spec((D_MODEL, IN_PAD)), const_spec((1, IN_PAD))]
    args = [x2d, w_p, b_p]
    out_specs = [pl.BlockSpec((tm, w), lambda i: (i, 0)) for w in widths]
    out_shape = [jax.ShapeDtypeStruct((rows, w), F32) for w in widths]
    if seq_len is not None:
        tiles = seq_len // tm
        in_specs += [const_spec((4 * KV_W, D_MODEL)), const_spec((4 * KV_W, 1))]
        args += [jnp.transpose(w_p[:, _C_KV:_C_WIN]), b_p[0, _C_KV:_C_WIN].reshape(4 * KV_W, 1)]
        out_specs.append(pl.BlockSpec((1, 4 * KV_W, tm), lambda i: (i // tiles, 0, i % tiles)))
        out_shape.append(jax.ShapeDtypeStruct((rows // seq_len, 4 * KV_W, seq_len), F32))
    return pl.pallas_call(
        functools.partial(_proj_in_body, kv_transposed=seq_len is not None),
        grid=(rows // tm,),
        in_specs=in_specs,
        out_specs=out_specs,
        out_shape=out_shape,
        compiler_params=pltpu.CompilerParams(dimension_semantics=("arbitrary",),
                                             vmem_limit_bytes=VMEM_LIMIT),
        name="proj_in",
    )(*args)


N_PAIRS = N_MLSTM_HEADS // 2
GATE_LANE0 = N_MLSTM_HEADS
MLSTM_SEQS_PER_STEP = 4


def _dot_split(x, onehot_bf16, pieces):
    acc = None
    rem = x
    for n in range(pieces):
        part = rem.astype(BF16)
        d = _dot(part, onehot_bf16)
        acc = d if acc is None else acc + d
        if n + 1 < pieces:
            rem = rem - part.astype(F32)
    return acc


def _mlstm_body(*refs, tb, nbg, n_dummy, has_state):
    if has_state:
        (zqk_ref, zv_ref, zo_ref, zg_ref, c0_ref, n0_ref, m0_ref, cw_ref, cb_ref, nw_ref,
         mix_ref, c_out, n_out, m_out, ubuf, qk_sc, ct_sc, n_sc, m_sc) = refs
    else:
        (zqk_ref, zv_ref, zo_ref, zg_ref, cw_ref, cb_ref, nw_ref,
         mix_ref, c_out, n_out, m_out, ubuf, qk_sc, ct_sc, n_sc, m_sc) = refs
    L = MLSTM_CHUNK
    D = HEAD_DIM
    j = pl.program_id(1)
    hi = lax.Precision.HIGHEST
    head_lanes = slice(GATE_LANE0, GATE_LANE0 + N_MLSTM_HEADS)
    pairs = [(bi, p) for bi in range(nbg) for p in range(N_PAIRS)]

    @pl.when(j == 0)
    def _init():
        ubuf[:, 0:8, :] = jnp.zeros((nbg, 8, 2 * MLSTM_W), F32)
        ct_sc[...] = jnp.zeros(ct_sc.shape, F32)
        n_sc[...] = jnp.zeros(n_sc.shape, F32)
        m_sc[...] = jnp.zeros(m_sc.shape, F32)
        if has_state:
            for bi in range(nbg):
                for h in range(N_MLSTM_HEADS):
                    p, hh = divmod(h, 2)
                    hs = slice(hh * D, (hh + 1) * D)
                    ct_sc[bi, p, hs, hs] = c0_ref[bi, h]
                    n_sc[bi, p, :, hs] = n0_ref[bi, h:h + 1, :]
                m_sc[bi, :, head_lanes] = m0_ref[bi]

    for bi in range(nbg):
        u = zqk_ref[bi]
        ubuf[bi, 8:8 + tb, :] = u
        conv = cb_ref[...] + ubuf[bi, 5:5 + tb, :] * cw_ref[0:1, :]
        conv = conv + ubuf[bi, 6:6 + tb, :] * cw_ref[1:2, :]
        conv = conv + ubuf[bi, 7:7 + tb, :] * cw_ref[2:3, :]
        conv = conv + u * cw_ref[3:4, :]
        ubuf[bi, 5:8, :] = u[tb - 3:tb, :]
        qk_sc[bi] = conv * jax.nn.sigmoid(conv)

    tri = (lax.broadcasted_iota(I32, (L, L), 1) <= lax.broadcasted_iota(I32, (L, L), 0)).astype(F32)
    row_l = lax.broadcasted_iota(I32, (L, LANES), 0)
    lane = lax.broadcasted_iota(I32, (L, LANES), 1)
    low = lane < D
    key_tok = lane % D
    causal_pair = key_tok <= row_l
    eye_pair = key_tok == row_l
    spread = (lax.broadcasted_iota(I32, (LANES, MLSTM_W), 0) - GATE_LANE0
              == lax.broadcasted_iota(I32, (LANES, MLSTM_W), 1) // D).astype(BF16)
    same_head = (lax.broadcasted_iota(I32, (LANES, LANES), 0) // D
                 == lax.broadcasted_iota(I32, (LANES, LANES), 1) // D)
    ones_bd = same_head.astype(BF16)
    slab_rows = 3 * L + 8

    def chunk(ci, carry):
        r0 = pl.multiple_of(ci * L, L)
        rows = pl.ds(r0, L)
        gates = []
        for bi in range(nbg):
            g = zg_ref[bi, rows, :]
            if n_dummy:
                dummy = (row_l + ci * L) < n_dummy
                g = jnp.where(dummy, jnp.where(lane < G_FGATE, -1e30, 1e30), g)
            gates.append(g)
        b_all = _dot(tri, jnp.concatenate([_log_sigmoid(g) for g in gates], axis=1), hi)
        stacks = []
        for bi in range(nbg):
            b = b_all[:, bi * LANES:(bi + 1) * LANES]
            r = pltpu.roll(gates[bi], GATE_LANE0 - G_IGATE, 1) - b
            cm = r
            for sh in (1, 2, 4, 8, 16, 32):
                cm = jnp.where(row_l >= sh, jnp.maximum(cm, pltpu.roll(cm, sh, 0)), cm)
            m_st = m_sc[bi]
            mx = jnp.maximum(m_st, cm)
            mx_end = mx[L - 1:L, :]
            m_sc[bi] = b[L - 1:L, :] + mx_end
            stacks += [mx, r, b + mx, m_st, mx_end, jnp.zeros((6, LANES), F32)]
        x_all = _dot_split(jnp.concatenate(stacks, axis=0), spread, 2)

        work = []
        den_terms = []
        for bi, p in pairs:
            cs = slice(p * LANES, (p + 1) * LANES)
            x0 = bi * slab_rows
            mx_p, rs_p, mt_p = x_all[x0:x0 + L, cs], x_all[x0 + L:x0 + 2 * L, cs], x_all[x0 + 2 * L:x0 + 3 * L, cs]
            mst_p, mxe_p = x_all[x0 + 3 * L:x0 + 3 * L + 1, cs], x_all[x0 + 3 * L + 1:x0 + 3 * L + 2, cs]
            wi_p = jnp.exp(mst_p - mx_p)
            fl_p = jnp.exp(-mt_p)
            wk_p = jnp.exp(rs_p - mxe_p)
            sc_p = jnp.exp(mst_p - mxe_p)
            r_row = jnp.sum(jnp.where(eye_pair, rs_p, 0.0), axis=0, keepdims=True)
            w_intra = jnp.exp(jnp.where(causal_pair, r_row - mx_p, NEG_INF))
            q_p = qk_sc[bi, rows, cs]
            k_p = qk_sc[bi, rows, MLSTM_W + p * LANES:MLSTM_W + (p + 1) * LANES] * (D ** -0.5)
            v_p = zv_ref[bi, rows, cs]
            qb = q_p.astype(BF16)
            k_bd = jnp.concatenate([jnp.where(low, k_p, 0.0), jnp.where(low, 0.0, k_p)], axis=0)
            v_bd = jnp.concatenate([jnp.where(low, v_p, 0.0), jnp.where(low, 0.0, v_p)], axis=0).astype(BF16)
            c_t = ct_sc[bi, p]
            both = _dot_nt(qb, jnp.concatenate([k_bd, c_t], axis=0).astype(BF16))
            a = w_intra * both[:, 0:LANES]
            num = _dot(a.astype(BF16), v_bd) + both[:, LANES:2 * LANES] * wi_p
            n_row = n_sc[bi, p]
            den_terms.append(a + wi_p * (q_p * n_row))
            work.append((num, fl_p))
            kw = k_p * wk_p
            upd = _dot_tn(v_p.astype(BF16), kw.astype(BF16))
            ct_sc[bi, p] = sc_p * c_t + jnp.where(same_head, upd, 0.0)
            n_sc[bi, p] = sc_p * n_row + jnp.sum(kw, axis=0, keepdims=True)
        den_all = _dot_split(jnp.concatenate(den_terms, axis=0), ones_bd, 2)

        hhs = []
        for n, (num, fl_p) in enumerate(work):
            hhs.append(num / jnp.maximum(jnp.abs(den_all[n * L:(n + 1) * L, :]), fl_p))
        mu_all = _dot_split(jnp.concatenate(hhs, axis=0), ones_bd, 1) * (1.0 / D)
        hcs = [hh - mu_all[n * L:(n + 1) * L, :] for n, hh in enumerate(hhs)]
        var_all = _dot_split(jnp.concatenate([hc * hc for hc in hcs], axis=0), ones_bd, 1) * (1.0 / D)
        for n, (bi, p) in enumerate(pairs):
            cs = slice(p * LANES, (p + 1) * LANES)
            hn = hcs[n] * lax.rsqrt(var_all[n * L:(n + 1) * L, :] + LN_EPS)
            og = jax.nn.sigmoid(zo_ref[bi, rows, cs])
            mix_ref[bi, rows, cs] = og * (hn * nw_ref[:, cs])
        return carry

    lax.fori_loop(0, tb // L, chunk, 0, unroll=2 if (tb // L) % 2 == 0 else 1)

    @pl.when(j == pl.num_programs(1) - 1)
    def _final():
        for bi in range(nbg):
            for h in range(N_MLSTM_HEADS):
                p, hh = divmod(h, 2)
                hs = slice(hh * D, (hh + 1) * D)
                c_out[bi, h] = ct_sc[bi, p, hs, hs]
                n_out[bi, h:h + 1, :] = n_sc[bi, p, :, hs]
            m_out[bi] = m_sc[bi, :, head_lanes]


def _mlstm(zqk, zv, zo, zg, state, conv_w, conv_b, norm_w, n_dummy):
    bsz, t_len, _ = zqk.shape
    tb = min(t_len, 256)
    nbg = MLSTM_SEQS_PER_STEP
    assert bsz % nbg == 0
    has_state = state is not None
    row = lambda w: pl.BlockSpec((nbg, tb, w), lambda b, j: (b, j, 0))
    per_b = lambda shape: pl.BlockSpec((nbg,) + shape, lambda b, j: (b,) + (0,) * len(shape))
    const = lambda shape: pl.BlockSpec(shape, lambda b, j: (0,) * len(shape))
    in_specs = [row(2 * MLSTM_W), row(MLSTM_W), row(MLSTM_W), row(LANES)]
    args = [zqk, zv, zo, zg]
    if has_state:
        c0, n0, m0 = state
        in_specs += [per_b((N_MLSTM_HEADS, HEAD_DIM, HEAD_DIM)), per_b((N_MLSTM_HEADS, HEAD_DIM)),
                     per_b((1, N_MLSTM_HEADS))]
        args += [c0, n0, m0.reshape(bsz, 1, N_MLSTM_HEADS)]
    in_specs += [const((MLSTM_CONV, 2 * MLSTM_W)), const((1, 2 * MLSTM_W)), const((1, MLSTM_W))]
    args += [conv_w, conv_b.reshape(1, -1), norm_w.reshape(1, -1)]
    mix, c_new, n_new, m_new = pl.pallas_call(
        functools.partial(_mlstm_body, tb=tb, nbg=nbg, n_dummy=n_dummy, has_state=has_state),
        grid=(bsz // nbg, t_len // tb),
        in_specs=in_specs,
        out_specs=[row(MLSTM_W), per_b((N_MLSTM_HEADS, HEAD_DIM, HEAD_DIM)),
                   per_b((N_MLSTM_HEADS, HEAD_DIM)), per_b((1, N_MLSTM_HEADS))],
        out_shape=[jax.ShapeDtypeStruct((bsz, t_len, MLSTM_W), F32),
                   jax.ShapeDtypeStruct((bsz, N_MLSTM_HEADS, HEAD_DIM, HEAD_DIM), F32),
                   jax.ShapeDtypeStruct((bsz, N_MLSTM_HEADS, HEAD_DIM), F32),
                   jax.ShapeDtypeStruct((bsz, 1, N_MLSTM_HEADS), F32)],
        scratch_shapes=[pltpu.VMEM((nbg, 8 + tb, 2 * MLSTM_W), F32),
                        pltpu.VMEM((nbg, tb, 2 * MLSTM_W), F32),
                        pltpu.VMEM((nbg, N_PAIRS, LANES, LANES), F32),
                        pltpu.VMEM((nbg, N_PAIRS, 1, LANES), F32),
                        pltpu.VMEM((nbg, 1, LANES), F32)],
        compiler_params=pltpu.CompilerParams(dimension_semantics=("arbitrary", "arbitrary"),
                                             vmem_limit_bytes=VMEM_LIMIT),
        name="mlstm",
    )(*args)
    return mix, c_new, n_new, m_new.reshape(bsz, N_MLSTM_HEADS)


def _block_diag_queries(x, n_rows, heads=tuple(range(N_NSA_HEADS)), scale=HEAD_DIM ** -0.5):
    lane = lax.broadcasted_iota(I32, (n_rows, LANES), 1)
    low = lane < HEAD_DIM
    pieces = []
    for h in heads:
        pair = x[:, (h // 2) * LANES:(h // 2 + 1) * LANES]
        src_low = h % 2 == 0
        dst_low = h // GQ == 0
        if src_low != dst_low:
            pair = pltpu.roll(pair, HEAD_DIM, 1)
        pieces.append(jnp.where(low if dst_low else ~low, pair, 0.0))
    return (jnp.concatenate(pieces, axis=0) * scale).astype(BF16)


def _group_lanes(o, g):
    return o[:, g * HEAD_DIM:(g + 1) * HEAD_DIM]


def _gated_mix(gates, o_c, o_s, o_w, h, n_rows):
    g = h // GQ
    rs = slice(h * n_rows, (h + 1) * n_rows)
    col = lambda branch: gates[:, G_NSA + branch * N_NSA_HEADS + h:G_NSA + branch * N_NSA_HEADS + h + 1]
    return (col(0) * _group_lanes(o_c[rs], g) + col(1) * _group_lanes(o_s[rs], g)
            + col(2) * _group_lanes(o_w[rs], g))


SEL_TILE = 256
SEL_TILE_BLOCKS = SEL_TILE // SEL_BLOCK
FAR_GROUP = 2
Q_PER_STEP = 4
WIN_BAND = WINDOW + NSA_QBLOCK
N_COLS = N_NSA_HEADS * NSA_QBLOCK
PROMPT_HEADS = tuple(GQ * (jj % N_KV_HEADS) + jj // N_KV_HEADS for jj in range(N_NSA_HEADS))
SEL_REL0 = 2 * SEL_TILE - SEL_BLOCK
SEL_BIAS_ROWS = SEL_REL0 + SEL_TILE


LOG2E = math.log2(math.e)


def _softmax2_rows(s):
    mx = jnp.max(s, 0, keepdims=True)
    mx = jnp.where(mx > NEG_INF, mx, 0.0)
    e = jnp.exp2(s - mx)
    return e / jnp.maximum(jnp.sum(e, 0, keepdims=True), 1e-30)


def _nsa_prompt_body(rb_ref, nq_ref, nqn_ref, kv_ref, win_ref, zg_ref, pw_ref, out_ref,
                     ks_sc, vst_sc, wk_sc, wv_sc, kc_sc, pv_sc, vct_sc, bs_sc, bw_sc, fc_sc,
                     m_sc, l_sc, acc_sc, ch_sc, oc_sc, fs_sc, ft_sc, *, t_len):
    QB = NSA_QBLOCK
    n_cmp = t_len // CMP_BLOCK
    n_sel = t_len // SEL_BLOCK
    pid = pl.program_id(1)
    hi = lax.Precision.HIGHEST

    def cmp_scores(x_q):
        return _dot_nt(kc_sc[...], _block_diag_queries(x_q, QB, PROMPT_HEADS, HEAD_DIM ** -0.5 * LOG2E))

    def cmp_and_select(raw, qi, out_buf):
        s_c = raw + pltpu.roll(fc_sc[...], 2 * qi, 0)
        c_i = lax.broadcasted_iota(I32, (n_cmp, N_COLS), 0)
        q_i = lax.broadcasted_iota(I32, (n_cmp, N_COLS), 1) % QB
        p_c = _softmax2_rows(jnp.where(c_i * CMP_BLOCK + (CMP_BLOCK - 1) <= qi * QB + q_i, s_c, NEG_INF))
        oc_sc[out_buf] = _dot(vct_sc[...], p_c.astype(BF16))
        p_sum = p_c[:, 0:LANES]
        for j in range(1, GQ):
            p_sum = p_sum + p_c[:, j * LANES:(j + 1) * LANES]
        ratio = SEL_BLOCK // CMP_BLOCK
        pair = (lax.broadcasted_iota(I32, (n_sel, n_cmp), 1) // ratio
                == lax.broadcasted_iota(I32, (n_sel, n_cmp), 0)).astype(F32)
        imp = _dot(pair, p_sum, hi)
        blk = lax.broadcasted_iota(I32, (n_sel, LANES), 0)
        score = jnp.where(blk == qi, jnp.inf, jnp.where(blk < qi, imp, NEG_INF))

        def pick(_, carry):
            work, chosen = carry
            best = jnp.max(work, 0, keepdims=True)
            first = jnp.min(jnp.where(work == best, blk, n_sel), 0, keepdims=True)
            hit = blk == first
            return jnp.where(hit, NEG_INF, work), jnp.where(hit, 1.0, chosen)

        _, chosen = lax.fori_loop(0, SEL_TOPK, pick, (score, jnp.zeros((n_sel, LANES), F32)), unroll=True)
        ch_sc[out_buf, 0:8, :] = jnp.full((8, LANES), NEG_INF, F32)
        ch_sc[out_buf, 8:8 + n_sel, :] = jnp.where((score > NEG_INF) & (chosen > 0.5), 0.0, NEG_INF)

    @pl.when(pid == 0)
    def _init():
        ks_sc[0:SEL_TILE, :] = jnp.zeros((SEL_TILE, LANES), BF16)
        vst_sc[0] = jnp.zeros((LANES, SEL_TILE), BF16)
        wk_sc[0:WINDOW, :] = jnp.zeros((WINDOW, LANES), BF16)
        wv_sc[0:WINDOW, :] = jnp.zeros((WINDOW, LANES), BF16)
        rows_per = 2 * SEL_TILE
        blocks_per = rows_per // CMP_BLOCK

        def fill(c, carry):
            r0 = pl.multiple_of(c * rows_per, rows_per)
            kv = kv_ref[0, pl.ds(r0, rows_per), :]
            ks_sc[pl.ds(SEL_TILE + r0, rows_per), :] = kv[:, 2 * LANES:3 * LANES].astype(BF16)
            for half in range(2):
                v_t = kv[half * SEL_TILE:(half + 1) * SEL_TILE, 3 * LANES:4 * LANES]
                vst_sc[2 * c + half + 1] = jnp.transpose(v_t).astype(BF16)
            w = win_ref[0, pl.ds(r0, rows_per), :]
            wk_sc[pl.ds(WINDOW + r0, rows_per), :] = w[:, 0:LANES].astype(BF16)
            wv_sc[pl.ds(WINDOW + r0, rows_per), :] = w[:, LANES:2 * LANES].astype(BF16)
            pooled = jnp.sum(kv[:, 0:2 * LANES].reshape(blocks_per, CMP_BLOCK, 2 * LANES) * pw_ref[...][None],
                             axis=1)
            c0 = pl.multiple_of(c * blocks_per, blocks_per)
            kc_sc[pl.ds(c0, blocks_per), :] = pooled[:, 0:LANES].astype(BF16)
            pv_sc[pl.ds(c0, blocks_per), :] = pooled[:, LANES:2 * LANES]
            return carry

        lax.fori_loop(0, t_len // rows_per, fill, 0)
        vct_sc[...] = jnp.transpose(pv_sc[...]).astype(BF16)

        for jj, h in enumerate(PROMPT_HEADS):
            cs = slice(jj * QB, (jj + 1) * QB)
            u_i = lax.broadcasted_iota(I32, (SEL_BIAS_ROWS, QB), 0)
            q_i = lax.broadcasted_iota(I32, (SEL_BIAS_ROWS, QB), 1)
            dist = q_i - (u_i - SEL_REL0)
            near = (_bias_from_dist(dist, rb_ref, h) - rb_ref[N_BUCKETS - 1, h]) * LOG2E
            bs_sc[:, cs] = jnp.where(dist >= 0, near, NEG_INF)
            s_i = lax.broadcasted_iota(I32, (WIN_BAND, QB), 0)
            q_i = lax.broadcasted_iota(I32, (WIN_BAND, QB), 1)
            dist = q_i + WINDOW - s_i
            bw_sc[:, cs] = jnp.where((dist >= 0) & (dist <= WINDOW), _bias_from_dist(dist, rb_ref, h) * LOG2E,
                                     NEG_INF)
            c_i = lax.broadcasted_iota(I32, (n_cmp, QB), 0)
            q_i = lax.broadcasted_iota(I32, (n_cmp, QB), 1)
            rel = jnp.where(c_i < n_cmp // 2, c_i, c_i - n_cmp)
            dist = q_i - CMP_BLOCK * rel - (CMP_BLOCK - 1)
            dist = jnp.where((c_i >= 2) & (c_i < n_cmp // 2), FAR_DIST, dist)
            fc_sc[:, cs] = _bias_from_dist(dist, rb_ref, h) * LOG2E

        cmp_and_select(cmp_scores(nq_ref[0, 0:QB, :]), 0, 0)

    def q_block(t):
        i = Q_PER_STEP * pid + t
        rows = slice(t * QB, (t + 1) * QB)
        x_q = nq_ref[0, rows, :]
        x_next = nq_ref[0, (t + 1) * QB:(t + 2) * QB, :] if t + 1 < Q_PER_STEP else nqn_ref[0, 0:QB, :]
        buf = t % 2
        o_c = oc_sc[buf]
        qbd = _block_diag_queries(x_q, QB, PROMPT_HEADS, HEAD_DIM ** -0.5 * LOG2E)
        last = i // SEL_TILE_BLOCKS

        def tile_raw(slot):
            return _dot_nt(ks_sc[pl.ds(pl.multiple_of(slot * SEL_TILE, SEL_TILE), SEL_TILE), :], qbd)

        raw_next = cmp_scores(x_next)
        w0 = pl.multiple_of(i * QB, QB)
        raw_w = _dot_nt(wk_sc[pl.ds(w0, WIN_BAND), :], qbd)
        raw_near = [tile_raw(last + 1), tile_raw(last)]

        def tile_scores(slot, bias, s=None):
            if s is None:
                s = tile_raw(slot)
            if bias is not None:
                s = s + bias
            parts = []
            for r in range(SEL_TILE_BLOCKS):
                row = ch_sc[buf, pl.ds(8 - SEL_TILE_BLOCKS + slot * SEL_TILE_BLOCKS + r, 1), :]
                parts.append(s[r * SEL_BLOCK:(r + 1) * SEL_BLOCK, :] + jnp.concatenate([row] * GQ, axis=1))
            return jnp.concatenate(parts, axis=0)

        def group_max(scores):
            return functools.reduce(jnp.maximum, [jnp.max(s, 0, keepdims=True) for s in scores])

        def sel_group(slots, scores, t_max, first):
            m_new = t_max if first else jnp.maximum(m_sc[...], t_max)
            exps = [jnp.exp2(s - m_new) for s in scores]
            l_new = functools.reduce(jnp.add, [jnp.sum(e, 0, keepdims=True) for e in exps])
            acc_new = functools.reduce(jnp.add, [_dot(vst_sc[slot], e.astype(BF16)) for slot, e in zip(slots, exps)])
            if first:
                l_sc[...] = l_new
                acc_sc[...] = acc_new
            else:
                scale = jnp.exp2(m_sc[...] - m_new)
                l_sc[...] = scale * l_sc[...] + l_new
                acc_sc[...] = scale * acc_sc[...] + acc_new
            m_sc[...] = m_new

        def far_slots(p):
            return [jnp.where(FAR_GROUP * p + 1 + n < last, FAR_GROUP * p + 1 + n, 0) for n in range(FAR_GROUP)]

        def far_scores(p):
            scores = [tile_scores(slot, None) for slot in far_slots(p)]
            for n, s in enumerate(scores):
                fs_sc[p % 2, n] = s
            ft_sc[p % 2] = group_max(scores)

        far_scores(0)
        cmp_and_select(raw_next, i + 1, 1 - buf)

        in_seq = lax.broadcasted_iota(I32, (WIN_BAND, N_COLS), 0) >= WINDOW - i * QB
        p_w = _softmax2_rows(jnp.where(in_seq, raw_w + bw_sc[...], NEG_INF))
        o_w = _dot_tn(wv_sc[pl.ds(w0, WIN_BAND), :], p_w.astype(BF16))

        u0 = pl.multiple_of(SEL_REL0 - SEL_TILE - SEL_BLOCK * (i % SEL_TILE_BLOCKS), SEL_BLOCK)
        near = [tile_scores(last + 1, bs_sc[pl.ds(u0 + SEL_TILE, SEL_TILE), :], raw_near[0]),
                tile_scores(last, bs_sc[pl.ds(u0, SEL_TILE), :], raw_near[1])]
        sel_group((last + 1, last), near, group_max(near), True)

        def far_group(p, carry):
            sel_group(far_slots(p), [fs_sc[p % 2, n] for n in range(FAR_GROUP)], ft_sc[p % 2], False)
            far_scores(p + 1)
            return carry

        lax.fori_loop(0, (last - 1 + FAR_GROUP - 1) // FAR_GROUP, far_group, 0)
        o_s = acc_sc[...] / jnp.maximum(l_sc[...], 1e-30)

        eye = (lax.broadcasted_iota(I32, (3 * N_NSA_HEADS + 8, LANES), 0) + G_NSA
               == lax.broadcasted_iota(I32, (3 * N_NSA_HEADS + 8, LANES), 1)).astype(F32)
        gates = jax.nn.sigmoid(_dot_nt(eye, zg_ref[0, rows, :], hi))
        pieces = []
        for h in range(N_NSA_HEADS):
            g = h // GQ
            jj = PROMPT_HEADS.index(h)
            part = lambda o: o[g * HEAD_DIM:(g + 1) * HEAD_DIM, jj * QB:(jj + 1) * QB]
            pieces.append(gates[h:h + 1, :] * part(o_c)
                          + gates[N_NSA_HEADS + h:N_NSA_HEADS + h + 1, :] * part(o_s)
                          + gates[2 * N_NSA_HEADS + h:2 * N_NSA_HEADS + h + 1, :] * part(o_w))
        mix_t = jnp.concatenate(pieces, axis=0)
        mix_t = jnp.concatenate([mix_t, jnp.zeros((NSA_W, LANES - QB), F32)], axis=1)
        out_ref[0, rows, :] = jnp.transpose(mix_t)[0:QB, :]

    for t in range(Q_PER_STEP):
        q_block(t)


def _nsa_prompt(nq, kv, win, zg, pool_w2, rel_bias):
    bsz, t_len, _ = nq.shape
    assert t_len // CMP_BLOCK == LANES and t_len % (2 * SEL_TILE) == 0
    n_cmp = t_len // CMP_BLOCK
    per_b = lambda w: pl.BlockSpec((1, t_len, w), lambda b, i: (b, 0, 0))
    step_rows = Q_PER_STEP * NSA_QBLOCK
    blk = lambda w: pl.BlockSpec((1, step_rows, w), lambda b, i: (b, i, 0))
    n_qb = t_len // step_rows
    next_blk = pl.BlockSpec((1, step_rows, NSA_W), lambda b, i: (b, jnp.minimum(i + 1, n_qb - 1), 0))
    return pl.pallas_call(
        functools.partial(_nsa_prompt_body, t_len=t_len),
        grid=(bsz, n_qb),
        in_specs=[pl.BlockSpec(memory_space=pltpu.SMEM),
                  blk(NSA_W), next_blk, per_b(4 * KV_W), per_b(2 * KV_W), blk(LANES),
                  pl.BlockSpec((CMP_BLOCK, 2 * LANES), lambda b, i: (0, 0))],
        out_specs=blk(NSA_W),
        out_shape=jax.ShapeDtypeStruct((bsz, t_len, NSA_W), F32),
        scratch_shapes=[pltpu.VMEM((SEL_TILE + t_len, LANES), BF16),
                        pltpu.VMEM((1 + t_len // SEL_TILE, LANES, SEL_TILE), BF16),
                        pltpu.VMEM((WINDOW + t_len, LANES), BF16),
                        pltpu.VMEM((WINDOW + t_len, LANES), BF16),
                        pltpu.VMEM((n_cmp, LANES), BF16),
                        pltpu.VMEM((n_cmp, LANES), F32),
                        pltpu.VMEM((LANES, n_cmp), BF16),
                        pltpu.VMEM((SEL_BIAS_ROWS, N_COLS), F32),
                        pltpu.VMEM((WIN_BAND, N_COLS), F32),
                        pltpu.VMEM((n_cmp, N_COLS), F32),
                        pltpu.VMEM((1, N_COLS), F32),
                        pltpu.VMEM((1, N_COLS), F32),
                        pltpu.VMEM((LANES, N_COLS), F32),
                        pltpu.VMEM((2, 8 + t_len // SEL_BLOCK, LANES), F32),
                        pltpu.VMEM((2, LANES, N_COLS), F32),
                        pltpu.VMEM((2, FAR_GROUP, SEL_TILE, N_COLS), F32),
                        pltpu.VMEM((2, 1, N_COLS), F32)],
        compiler_params=pltpu.CompilerParams(dimension_semantics=("arbitrary", "arbitrary"),
                                             vmem_limit_bytes=VMEM_LIMIT),
        name="nsa_prompt",
    )(rel_bias, nq, nq, kv, win, zg, pool_w2)


CMP_PAGES = 32
CMP_SLOTS = 3
N_PICK = SEL_TOPK - 1
N_OWNERS = N_KV_HEADS * N_NEW
N_FETCH = N_OWNERS * N_PICK
OWNER_KEYS = N_PICK * PAGE_SIZE


def _sample_row_ids():
    r = lax.broadcasted_iota(I32, (N_QROWS, 1), 0)
    return r % SEQ_PAD - (SEQ_PAD - N_NEW), r // SEQ_PAD


def _sample_bias(dist, rb_ref):
    _, head = _sample_row_ids()
    col = lambda k: sum(jnp.where(head == h, rb_ref[k, h], 0.0) for h in range(N_NSA_HEADS))
    acc = jnp.broadcast_to(col(0), dist.shape)
    for k in range(1, N_BUCKETS):
        acc = jnp.where(dist >= BUCKET_THR[k], col(k), acc)
    return acc


def _sample_bias_table(rb_ref):
    assert FAR_DIST < LANES
    return _sample_bias(lax.broadcasted_iota(I32, (N_QROWS, LANES), 1), rb_ref)


def _sample_bias_lookup(dist, table):
    idx = jnp.clip(dist, 0, LANES - 1)
    return jnp.concatenate([jnp.take_along_axis(table, idx[:, c * LANES:(c + 1) * LANES], axis=1)
                            for c in range(dist.shape[1] // LANES)], axis=1)


def _softmax_two(s1, s2):
    mx = jnp.maximum(jnp.max(s1, -1, keepdims=True), jnp.max(s2, -1, keepdims=True))
    mx = jnp.where(mx > NEG_INF, mx, 0.0)
    e1 = jnp.exp(s1 - mx)
    e2 = jnp.exp(s2 - mx)
    den = jnp.maximum(jnp.sum(e1, -1, keepdims=True) + jnp.sum(e2, -1, keepdims=True), 1e-30)
    return e1 / den, e2 / den


def _nsa_cmp_body(pt_ref, rb_ref, nq_ref, pwt_ref, cache_ref, oc_ref, idx_ref,
                  buf, sem, seg_sc, kct_sc, vct_sc, *, n_pages, past_len):
    b = pl.program_id(0)
    nb = pl.num_programs(0)
    n_chunks = n_pages // CMP_PAGES
    chunk_rows = CMP_PAGES * PAGE_SIZE
    chunk_blocks = chunk_rows // CMP_BLOCK
    n_cmp = past_len // CMP_BLOCK
    n_sel_past = past_len // SEL_BLOCK
    n_prob = N_KV_HEADS * SEQ_PAD
    total = nb * n_chunks
    hi = lax.Precision.HIGHEST

    def page_copy(gi, p):
        page = pt_ref[(gi // n_chunks) * n_pages + (gi % n_chunks) * CMP_PAGES + p]
        slot = gi % CMP_SLOTS
        return pltpu.make_async_copy(cache_ref.at[page, pl.ds(0, 2 * LANES), :], buf.at[slot, p], sem.at[slot])

    def start_chunk(gi):
        for p in range(CMP_PAGES):
            page_copy(gi, p).start()

    def wait_chunk(gi):
        for p in range(CMP_PAGES):
            page_copy(gi, p).wait()

    @pl.when(b == 0)
    def _prologue():
        seg_sc[...] = (lax.broadcasted_iota(I32, (chunk_rows, chunk_blocks), 0) // CMP_BLOCK
                       == lax.broadcasted_iota(I32, (chunk_rows, chunk_blocks), 1)).astype(BF16)
        for gi in range(min(CMP_SLOTS - 1, total)):
            start_chunk(gi)

    def chunk_step(c, carry):
        gi = b * n_chunks + c
        ahead = gi + (CMP_SLOTS - 1)

        @pl.when(ahead < total)
        def _prefetch():
            start_chunk(ahead)

        wait_chunk(gi)
        slot = gi % CMP_SLOTS
        y = jnp.concatenate([(buf[slot, p] * pwt_ref[...]).astype(BF16) for p in range(CMP_PAGES)], axis=1)
        pooled = _dot(y, seg_sc[...])
        kct_sc[c] = pooled[0:LANES].astype(BF16)
        vct_sc[c] = pooled[LANES:2 * LANES].astype(BF16)
        return carry

    lax.fori_loop(0, n_chunks, chunk_step, 0)

    qbd = _block_diag_queries(nq_ref[0], SEQ_PAD)
    tok, _ = _sample_row_ids()
    s_c = jnp.concatenate([_dot(qbd, kct_sc[c]) for c in range(n_chunks)], axis=1)
    near0 = n_cmp - LANES
    assert past_len - (SEQ_PAD - N_NEW) - (CMP_BLOCK * (near0 - 1) + CMP_BLOCK - 1) >= FAR_DIST
    dist = past_len + tok - ((lax.broadcasted_iota(I32, (N_QROWS, LANES), 1) + near0) * CMP_BLOCK + CMP_BLOCK - 1)
    near = jnp.where(dist >= 0, s_c[:, near0:] + _sample_bias(dist, rb_ref), NEG_INF)
    far = s_c[:, :near0] + _sample_bias(jnp.full((N_QROWS, 1), FAR_DIST, I32), rb_ref)
    p_c = _masked_softmax(jnp.concatenate([far, near], axis=1))
    o_c = _dot_nt(p_c[:, 0:chunk_blocks].astype(BF16), vct_sc[0])
    for c in range(1, n_chunks):
        o_c = o_c + _dot_nt(p_c[:, c * chunk_blocks:(c + 1) * chunk_blocks].astype(BF16), vct_sc[c])
    oc_ref[0] = o_c

    p_sum = jnp.sum(p_c.reshape(N_KV_HEADS, GQ, SEQ_PAD, n_cmp), axis=1).reshape(n_prob, n_cmp)
    p_sum = jnp.concatenate([p_sum, jnp.zeros((LANES - n_prob, n_cmp), F32)], axis=0)
    ratio = SEL_BLOCK // CMP_BLOCK
    pair = (lax.broadcasted_iota(I32, (n_sel_past, n_cmp), 1) // ratio
            == lax.broadcasted_iota(I32, (n_sel_past, n_cmp), 0)).astype(F32)
    imp = _dot_nt(pair, p_sum, hi)
    blk = lax.broadcasted_iota(I32, (n_sel_past, LANES), 0)
    pick_i = lax.broadcasted_iota(I32, (SEL_TOPK, LANES), 0)

    def pick(k, carry):
        work, ids = carry
        best = jnp.max(work, 0, keepdims=True)
        first = jnp.min(jnp.where(work == best, blk, n_sel_past), 0, keepdims=True)
        return jnp.where(blk == first, NEG_INF, work), jnp.where(pick_i == k, first, ids)

    _, ids = lax.fori_loop(0, N_PICK, pick, (imp, jnp.zeros((SEL_TOPK, LANES), I32)), unroll=True)
    idx_ref[0] = ids


def _nsa_cmp(page_table, rel_bias, nq_s, pool_wt, cache3):
    dec_b, n_pages = page_table.shape
    past_len = n_pages * PAGE_SIZE
    assert n_pages % CMP_PAGES == 0 and past_len // SEL_BLOCK >= N_PICK
    n_chunks = n_pages // CMP_PAGES
    chunk_rows = CMP_PAGES * PAGE_SIZE
    chunk_blocks = chunk_rows // CMP_BLOCK
    assert chunk_blocks == LANES
    grid_spec = pltpu.PrefetchScalarGridSpec(
        num_scalar_prefetch=1,
        grid=(dec_b,),
        in_specs=[pl.BlockSpec(memory_space=pltpu.SMEM),
                  pl.BlockSpec((1, SEQ_PAD, NSA_W), lambda b, pt: (b, 0, 0)),
                  pl.BlockSpec((2 * LANES, PAGE_SIZE), lambda b, pt: (0, 0)),
                  pl.BlockSpec(memory_space=pl.ANY)],
        out_specs=[pl.BlockSpec((1, N_QROWS, LANES), lambda b, pt: (b, 0, 0)),
                   pl.BlockSpec((1, SEL_TOPK, LANES), lambda b, pt: (b, 0, 0))],
        scratch_shapes=[pltpu.VMEM((CMP_SLOTS, CMP_PAGES, 2 * LANES, PAGE_SIZE), F32),
                        pltpu.SemaphoreType.DMA((CMP_SLOTS,)),
                        pltpu.VMEM((chunk_rows, chunk_blocks), BF16),
                        pltpu.VMEM((n_chunks, LANES, chunk_blocks), BF16),
                        pltpu.VMEM((n_chunks, LANES, chunk_blocks), BF16)],
    )
    return pl.pallas_call(
        functools.partial(_nsa_cmp_body, n_pages=n_pages, past_len=past_len),
        grid_spec=grid_spec,
        out_shape=[jax.ShapeDtypeStruct((dec_b, N_QROWS, LANES), F32),
                   jax.ShapeDtypeStruct((dec_b, SEL_TOPK, LANES), I32)],
        compiler_params=pltpu.CompilerParams(dimension_semantics=("arbitrary",),
                                             vmem_limit_bytes=VMEM_LIMIT),
        name="nsa_sample_cmp",
    )(page_table.reshape(-1), rel_bias, nq_s, pool_wt, cache3)


def _nsa_sel_body(pt_ref, ids_ref, rb_ref, nq_ref, kvn_ref, winc_ref, winn_ref, zg_ref, oc_ref, idv_ref,
                  cache_ref, out_ref, buf, sem, *, n_pages, past_len):
    b = pl.program_id(0)
    nb = pl.num_programs(0)
    hi = lax.Precision.HIGHEST

    def block_copy(bb, n, slot):
        blk = ids_ref[bb * N_FETCH + n]
        page = pt_ref[bb * n_pages + blk // 2]
        return pltpu.make_async_copy(cache_ref.at[page, pl.ds(2 * LANES, 2 * LANES), :], buf.at[slot, n],
                                     sem.at[slot])

    def start_all(bb, slot):
        def go(n, carry):
            block_copy(bb, n, slot).start()
            return carry
        lax.fori_loop(0, N_FETCH, go, 0, unroll=8)

    def wait_all(bb, slot):
        def go(n, carry):
            block_copy(bb, n, slot).wait()
            return carry
        lax.fori_loop(0, N_FETCH, go, 0, unroll=8)

    slot = b % 2

    @pl.when(b == 0)
    def _prologue():
        start_all(0, 0)

    @pl.when(b + 1 < nb)
    def _prefetch():
        start_all(b + 1, 1 - slot)

    qbd = _block_diag_queries(nq_ref[0], SEQ_PAD)
    tok, head = _sample_row_ids()
    owner = (head // GQ) * N_NEW + tok

    wc = winc_ref[0]
    wn = winn_ref[0]
    r_i = lax.broadcasted_iota(I32, (N_QROWS, WINDOW), 1)
    dist1 = WINDOW + tok - r_i
    bias_table = _sample_bias_table(rb_ref)
    s1 = _dot(qbd, wc[0:LANES, :].astype(BF16)) + _sample_bias_lookup(dist1, bias_table)
    s1 = jnp.where((dist1 >= 0) & (dist1 <= WINDOW), s1, NEG_INF)
    n_i = lax.broadcasted_iota(I32, (N_QROWS, SEQ_PAD), 1) - (SEQ_PAD - N_NEW)
    dist2 = tok - n_i
    new_ok = (n_i >= 0) & (dist2 >= 0)
    bias2 = _sample_bias(dist2, rb_ref)
    s2 = jnp.where(new_ok, _dot_nt(qbd, wn[:, 0:LANES].astype(BF16)) + bias2, NEG_INF)
    p1, p2 = _softmax_two(s1, s2)
    o_w = (_dot_nt(p1.astype(BF16), wc[LANES:2 * LANES, :].astype(BF16))
           + _dot(p2.astype(BF16), wn[:, LANES:2 * LANES].astype(BF16)))

    wait_all(b, slot)

    def owner_tiles(o, r0):
        return jnp.concatenate([buf[slot, o * N_PICK + k, r0:r0 + LANES, :] for k in range(N_PICK)],
                               axis=1).astype(BF16)

    s_f = jnp.full((N_QROWS, OWNER_KEYS), NEG_INF, F32)
    for o in range(N_OWNERS):
        s_f = jnp.where(owner == o, _dot(qbd, owner_tiles(o, 0)), s_f)
    spread = (lax.broadcasted_iota(I32, (SEL_TOPK, OWNER_KEYS), 0)
              == lax.broadcasted_iota(I32, (SEL_TOPK, OWNER_KEYS), 1) // PAGE_SIZE).astype(F32)
    blk_of_key = _dot_tn(idv_ref[0].astype(F32), spread, hi)
    blk_rows = jnp.concatenate([blk_of_key[(h // GQ) * SEQ_PAD:(h // GQ + 1) * SEQ_PAD]
                                for h in range(N_NSA_HEADS)], axis=0).astype(I32)
    blocks_per_page = PAGE_SIZE // SEL_BLOCK
    in_page = lax.broadcasted_iota(I32, (N_QROWS, OWNER_KEYS), 1) % PAGE_SIZE
    in_block = in_page // SEL_BLOCK == blk_rows % blocks_per_page
    dist_s = past_len + tok - ((blk_rows // blocks_per_page) * PAGE_SIZE + in_page)
    s_f = jnp.where((tok >= 0) & in_block & (dist_s >= 0), s_f + _sample_bias_lookup(dist_s, bias_table), NEG_INF)
    kvn = kvn_ref[0]
    s_n = jnp.where(new_ok, _dot_nt(qbd, kvn[:, 2 * LANES:3 * LANES].astype(BF16)) + bias2, NEG_INF)
    p_f, p_n = _softmax_two(s_f, s_n)
    o_s = _dot(p_n.astype(BF16), kvn[:, 3 * LANES:4 * LANES].astype(BF16))
    for o in range(N_OWNERS):
        o_s = o_s + _dot_nt(jnp.where(owner == o, p_f, 0.0).astype(BF16), owner_tiles(o, LANES))

    o_c = oc_ref[0]
    gates = jax.nn.sigmoid(zg_ref[0])
    is_new = lax.broadcasted_iota(I32, (SEQ_PAD, HEAD_DIM), 0) >= SEQ_PAD - N_NEW
    for h in range(N_NSA_HEADS):
        mix = _gated_mix(gates, o_c, o_s, o_w, h, SEQ_PAD)
        out_ref[0, :, h * HEAD_DIM:(h + 1) * HEAD_DIM] = jnp.where(is_new, mix, 0.0)


def _nsa_sel(page_table, ids, ids_rows, rel_bias, nq_s, kv_s, win_cache, win_s, zg_s, o_c, cache3):
    dec_b, n_pages = page_table.shape
    past_len = n_pages * PAGE_SIZE
    seq = lambda w: pl.BlockSpec((1, SEQ_PAD, w), lambda b, pt, ix: (b, 0, 0))
    grid_spec = pltpu.PrefetchScalarGridSpec(
        num_scalar_prefetch=2,
        grid=(dec_b,),
        in_specs=[pl.BlockSpec(memory_space=pltpu.SMEM),
                  seq(NSA_W), seq(4 * KV_W),
                  pl.BlockSpec((1, 2 * KV_W, WINDOW), lambda b, pt, ix: (b, 0, 0)),
                  seq(2 * KV_W), seq(LANES),
                  pl.BlockSpec((1, N_QROWS, LANES), lambda b, pt, ix: (b, 0, 0)),
                  pl.BlockSpec((1, SEL_TOPK, LANES), lambda b, pt, ix: (b, 0, 0)),
                  pl.BlockSpec(memory_space=pl.ANY)],
        out_specs=seq(NSA_W),
        scratch_shapes=[pltpu.VMEM((2, N_FETCH, 2 * LANES, PAGE_SIZE), F32),
                        pltpu.SemaphoreType.DMA((2,))],
    )
    return pl.pallas_call(
        functools.partial(_nsa_sel_body, n_pages=n_pages, past_len=past_len),
        grid_spec=grid_spec,
        out_shape=jax.ShapeDtypeStruct((dec_b, SEQ_PAD, NSA_W), F32),
        compiler_params=pltpu.CompilerParams(dimension_semantics=("arbitrary",),
                                             vmem_limit_bytes=VMEM_LIMIT),
        name="nsa_sample_sel",
    )(page_table.reshape(-1), ids.reshape(-1), rel_bias, nq_s, kv_s, win_cache, win_s, zg_s, o_c, ids_rows, cache3)


FF_CHUNK = 896


def _ffn_body(*refs, tm, has_override):
    if has_override:
        (x_ref, mm_ref, mn_ref, ov_ref, woa_ref, wob_ref, l1w_ref, l1b_ref, wup_ref, bup_ref, cw_ref, cb_ref,
         wdn_ref, bdn_ref, l2w_ref, l2b_ref, y_ref, tail_ref, carry_sc, ubuf) = refs
    else:
        (x_ref, mm_ref, mn_ref, woa_ref, wob_ref, l1w_ref, l1b_ref, wup_ref, bup_ref, cw_ref, cb_ref,
         wdn_ref, bdn_ref, l2w_ref, l2b_ref, y_ref, tail_ref, carry_sc, ubuf) = refs
    j = pl.program_id(1)

    @pl.when(j == 0)
    def _init():
        carry_sc[...] = jnp.zeros(carry_sc.shape, F32)

    x = x_ref[0]
    h = ALPHA * x + (_dot(mm_ref[0].astype(BF16), woa_ref[...]) + _dot(mn_ref[0].astype(BF16), wob_ref[...]))
    x1 = _layer_norm(h, l1w_ref[...], l1b_ref[...])
    x1b = x1.astype(BF16)
    if has_override:
        state_row = lax.broadcasted_iota(I32, (tm, FF_CHUNK), 0) % SEQ_PAD < SEQ_PAD - N_NEW

    def conv_half(c0):
        cols = slice(c0, c0 + FF_CHUNK)
        u = _dot(x1b, wup_ref[:, cols]) + bup_ref[:, cols]
        if has_override:
            u = jnp.where(state_row, ov_ref[0, :, cols], u)
        ubuf[0:8, :] = carry_sc[:, cols]
        ubuf[8:8 + tm, :] = u
        out = cb_ref[:, cols] + ubuf[6:6 + tm, :] * cw_ref[0:1, cols]
        out = out + ubuf[7:7 + tm, :] * cw_ref[1:2, cols]
        out = out + u * cw_ref[2:3, cols]
        carry_sc[:, cols] = u[tm - 8:tm, :]
        tail_ref[0, 0, :, cols] = u if has_override else u[tm - 8:tm, :]
        return out

    ff = jnp.zeros((tm, D_MODEL), F32)
    for c in range(D_FF // FF_CHUNK):
        ga = conv_half(c * FF_CHUNK)
        gb = conv_half(D_FF + c * FF_CHUNK)
        gelu = ga * (0.5 * (1.0 + jnp.tanh(math.sqrt(2.0 / math.pi) * (ga + 0.044715 * (ga * ga * ga)))))
        ff = ff + _dot((gelu * gb).astype(BF16), wdn_ref[c * FF_CHUNK:(c + 1) * FF_CHUNK, :])
    ff = ff + bdn_ref[...]
    y_ref[0] = _layer_norm(ALPHA * x1 + ff, l2w_ref[...], l2b_ref[...])


def _ffn(x, mix_m, mix_n, override, wts):
    bsz, t_len, _ = x.shape
    tm = min(t_len, 512)
    nt = t_len // tm
    has_override = override is not None
    tail_rows = tm if has_override else 8
    row = lambda w: pl.BlockSpec((1, tm, w), lambda b, j: (b, j, 0))
    const = lambda shape: pl.BlockSpec(shape, lambda b, j: (0,) * len(shape), pipeline_mode=pl.Buffered(1))
    in_specs = [row(D_MODEL), row(MLSTM_W), row(NSA_W)]
    args = [x, mix_m, mix_n]
    if has_override:
        in_specs.append(row(2 * D_FF))
        args.append(override)
    in_specs += [const(w.shape) for w in wts]
    args += list(wts)
    y, tail = pl.pallas_call(
        functools.partial(_ffn_body, tm=tm, has_override=has_override),
        grid=(bsz, nt),
        in_specs=in_specs,
        out_specs=[row(D_MODEL), pl.BlockSpec((1, 1, tail_rows, 2 * D_FF), lambda b, j: (b, j, 0, 0))],
        out_shape=[jax.ShapeDtypeStruct((bsz, t_len, D_MODEL), F32),
                   jax.ShapeDtypeStruct((bsz, nt, tail_rows, 2 * D_FF), F32)],
        scratch_shapes=[pltpu.VMEM((8, 2 * D_FF), F32), pltpu.VMEM((8 + tm, FF_CHUNK), F32)],
        compiler_params=pltpu.CompilerParams(dimension_semantics=("arbitrary", "arbitrary"),
                                             vmem_limit_bytes=VMEM_LIMIT),
        name="ffn",
    )(*args)
    return y, tail


def _permute_in_proj(w_in, b_in):
    gates_a = slice(2 * MLSTM_W + 2 * MLSTM_W, 2 * MLSTM_W + 2 * MLSTM_W + 2 * N_MLSTM_HEADS)
    nsa0 = gates_a.stop
    nsa1 = nsa0 + NSA_W + 6 * KV_W

    def perm(a):
        pad = IN_PAD - a.shape[-1]
        parts = [a[..., :gates_a.start], a[..., nsa0:nsa1], a[..., gates_a], a[..., nsa1:]]
        parts.append(jnp.zeros(a.shape[:-1] + (pad,), a.dtype))
        return jnp.concatenate(parts, axis=-1)

    return perm(w_in).astype(BF16), perm(b_in).reshape(1, IN_PAD)


def _front_pad(a, n):
    return jnp.pad(a, ((0, 0), (n, 0), (0, 0)))


def kernel(x_prompt, x_sample, cache_nsa_kv, cache_win_kv, state_mlstm_c, state_mlstm_n, state_mlstm_m,
           state_mlstm_conv, state_ffn_conv, page_table,
           w_in, b_in, mlstm_conv_w, mlstm_conv_b, mlstm_norm_w, nsa_pool_w, rel_bias, w_out,
           ln1_w, ln1_b, w_up, b_up, ffn_conv_w, ffn_conv_b, w_down, b_down, ln2_w, ln2_b):
    bsz, t_len, _ = x_prompt.shape
    dec_b, n_new, _ = x_sample.shape
    assert n_new == N_NEW and cache_win_kv.shape[1] == WINDOW

    w_p, b_p = _permute_in_proj(w_in, b_in)
    pool_w2 = jnp.concatenate([jnp.repeat(nsa_pool_w[0], HEAD_DIM, axis=-1),
                               jnp.repeat(nsa_pool_w[1], HEAD_DIM, axis=-1)], axis=-1)
    row = lambda v: v.reshape(1, -1)
    ffn_w = (w_out[:MLSTM_W].astype(BF16), w_out[MLSTM_W:].astype(BF16), row(ln1_w), row(ln1_b),
             w_up.astype(BF16), row(b_up), ffn_conv_w, row(ffn_conv_b), w_down.astype(BF16), row(b_down),
             row(ln2_w), row(ln2_b))

    zqk, zv, zo, nq, kv, win, zg, kv_t = _proj_in(x_prompt.reshape(bsz * t_len, D_MODEL), w_p, b_p, seq_len=t_len)
    seq3 = lambda a: a.reshape(bsz, t_len, a.shape[-1])
    zqk, zv, zo, nq, kv, win, zg = map(seq3, (zqk, zv, zo, nq, kv, win, zg))
    mix_m, c_p, n_p, m_p = _mlstm(zqk, zv, zo, zg, None, mlstm_conv_w, mlstm_conv_b, mlstm_norm_w, 0)
    mix_n = _nsa_prompt(nq, kv, win, zg, pool_w2, rel_bias)
    y_p, tail_p = _ffn(x_prompt, mix_m, mix_n, None, ffn_w)
    kv_p = jnp.transpose(kv_t.reshape(bsz, 4, N_KV_HEADS, HEAD_DIM, t_len), (0, 4, 1, 2, 3))
    win_p = win[:, t_len - min(WINDOW, t_len):].reshape(bsz, -1, 2, N_KV_HEADS, HEAD_DIM)
    mconv_p = zqk[:, t_len - (MLSTM_CONV - 1):]
    fconv_p = tail_p[:, -1, 8 - (FFN_CONV - 1):]

    pad = SEQ_PAD - N_NEW
    xs = _front_pad(x_sample, pad)
    szqk, szv, szo, snq, skv, swin, szg = _proj_in(xs.reshape(dec_b * SEQ_PAD, D_MODEL), w_p, b_p)
    sseq = lambda a: a.reshape(dec_b, SEQ_PAD, a.shape[-1])
    szqk, szv, szo, snq, skv, swin, szg = map(sseq, (szqk, szv, szo, snq, skv, swin, szg))
    lead = MLSTM_CHUNK - N_NEW
    new = lambda a: a[:, pad:]
    m_qk = jnp.concatenate([jnp.zeros((dec_b, lead - (MLSTM_CONV - 1), 2 * MLSTM_W), F32),
                            state_mlstm_conv.astype(F32), new(szqk)], axis=1)
    mix_ms, c_s, n_s, m_s = _mlstm(m_qk, _front_pad(new(szv), lead), _front_pad(new(szo), lead),
                                   _front_pad(new(szg), lead),
                                   (state_mlstm_c, state_mlstm_n, state_mlstm_m),
                                   mlstm_conv_w, mlstm_conv_b, mlstm_norm_w, lead)
    mix_ms = mix_ms[:, MLSTM_CHUNK - SEQ_PAD:]
    cache3 = jnp.transpose(cache_nsa_kv, (0, 2, 3, 4, 1)).reshape(cache_nsa_kv.shape[0], 4 * KV_W, PAGE_SIZE)
    pool_wt = jnp.tile(jnp.repeat(jnp.transpose(nsa_pool_w, (0, 2, 1)).reshape(2 * N_KV_HEADS, CMP_BLOCK),
                                  HEAD_DIM, axis=0), (1, PAGE_SIZE // CMP_BLOCK))
    o_c, ids_rows = _nsa_cmp(page_table, rel_bias, snq, pool_wt, cache3)
    ids = jnp.transpose(ids_rows[:, :N_PICK, :N_KV_HEADS * SEQ_PAD], (0, 2, 1))
    ids = ids.reshape(dec_b, N_KV_HEADS, SEQ_PAD, N_PICK)[:, :, pad:]
    win_cache = jnp.transpose(cache_win_kv, (0, 2, 3, 4, 1)).reshape(dec_b, 2 * KV_W, WINDOW)
    mix_ns = _nsa_sel(page_table, ids, ids_rows, rel_bias, snq, skv, win_cache, swin, szg, o_c, cache3)
    override = jnp.concatenate([jnp.zeros((dec_b, pad - (FFN_CONV - 1), 2 * D_FF), F32),
                                state_ffn_conv.astype(F32),
                                jnp.zeros((dec_b, N_NEW, 2 * D_FF), F32)], axis=1)
    flat = lambda a: a.reshape(1, dec_b * SEQ_PAD, a.shape[-1])
    y_s8, u_s = _ffn(flat(xs), flat(mix_ms), flat(mix_ns), flat(override), ffn_w)
    y_s = y_s8.reshape(dec_b, SEQ_PAD, D_MODEL)[:, pad:]
    kv_s = new(skv).reshape(dec_b, N_NEW, 4, N_KV_HEADS, HEAD_DIM)
    win_s = jnp.concatenate([cache_win_kv[:, N_NEW:], new(swin).reshape(dec_b, N_NEW, 2, N_KV_HEADS, HEAD_DIM)],
                            axis=1)
    mconv_s = szqk[:, SEQ_PAD - (MLSTM_CONV - 1):]
    fconv_s = u_s.reshape(dec_b, SEQ_PAD, 2 * D_FF)[:, SEQ_PAD - (FFN_CONV - 1):]
    return (y_p, y_s, kv_p, kv_s, win_p, win_s, c_p, c_s, n_p, n_s, m_p, m_s, mconv_p, mconv_s, fconv_p, fconv_s)
```

```python
import functools
import math

import jax
import jax.numpy as jnp
from jax import lax
from jax.experimental import pallas as pl
from jax.experimental.pallas import tpu as pltpu

F32 = jnp.float32
BF16 = jnp.bfloat16
I32 = jnp.int32

D_MODEL = 1024
HEAD_DIM = 64
N_MLSTM_HEADS = 8
N_NSA_HEADS = 8
N_KV_HEADS = 2
GQ = N_NSA_HEADS // N_KV_HEADS
MLSTM_W = N_MLSTM_HEADS * HEAD_DIM
NSA_W = N_NSA_HEADS * HEAD_DIM
KV_W = N_KV_HEADS * HEAD_DIM
MLSTM_CHUNK = 64
MLSTM_CONV = 4
CMP_BLOCK = 32
SEL_BLOCK = 64
SEL_TOPK = 16
WINDOW = 512
NSA_QBLOCK = 64
N_BUCKETS = 32
MAX_DISTANCE = 128
D_FF = 2688
FFN_CONV = 3
DEPTH = 1
ALPHA = (2.0 * DEPTH) ** 0.25
LN_EPS = 1e-5
PAGE_SIZE = 128

LANES = 128
VMEM_LIMIT = 56 * 1024 * 1024
NEG_INF = float("-inf")

SEQ_PAD = 8
N_NEW = 4
N_QROWS = N_NSA_HEADS * SEQ_PAD


def _bucket_thresholds():
    max_exact = N_BUCKETS // 2
    thr = list(range(max_exact + 1))
    for k in range(1, N_BUCKETS - max_exact):
        thr.append(int(math.ceil(max_exact * (MAX_DISTANCE / max_exact) ** (k / (N_BUCKETS - max_exact)))))
    return tuple(thr)


BUCKET_THR = _bucket_thresholds()
FAR_DIST = BUCKET_THR[N_BUCKETS - 1]


def _dot(a, b, precision=None):
    return jnp.dot(a, b, preferred_element_type=F32, precision=precision)


def _dot_nt(a, b, precision=None):
    return lax.dot_general(a, b, (((1,), (1,)), ((), ())), preferred_element_type=F32, precision=precision)


def _dot_tn(a, b, precision=None):
    return lax.dot_general(a, b, (((0,), (0,)), ((), ())), preferred_element_type=F32, precision=precision)


def _layer_norm(x, w, b):
    mu = jnp.mean(x, -1, keepdims=True)
    xc = x - mu
    var = jnp.mean(xc * xc, -1, keepdims=True)
    return xc * lax.rsqrt(var + LN_EPS) * w + b


def _log_sigmoid(x):
    return jnp.minimum(x, 0.0) - jnp.log1p(jnp.exp(-jnp.abs(x)))


def _bias_from_dist(dist, rb_ref, head):
    acc = jnp.full(dist.shape, rb_ref[0, head], F32)
    for k in range(1, N_BUCKETS):
        acc = jnp.where(dist >= BUCKET_THR[k], rb_ref[k, head], acc)
    return acc


def _masked_softmax(s):
    mx = jnp.max(s, -1, keepdims=True)
    mx = jnp.where(mx > NEG_INF, mx, 0.0)
    e = jnp.exp(s - mx)
    den = jnp.sum(e, -1, keepdims=True)
    return e / jnp.maximum(den, 1e-30)


_C_QK, _C_V, _C_O, _C_NQ, _C_KV, _C_WIN, _C_G = 0, 1024, 1536, 2048, 2560, 3072, 3328
IN_PAD = 3456
G_IGATE, G_FGATE, G_NSA = 0, N_MLSTM_HEADS, 2 * N_MLSTM_HEADS


def _proj_in_body(*refs, kv_transposed):
    if kv_transposed:
        x_ref, w_ref, b_ref, wkvt_ref, bkv_ref, qk_ref, v_ref, o_ref, nq_ref, kv_ref, win_ref, g_ref, kvt_ref = refs
    else:
        x_ref, w_ref, b_ref, qk_ref, v_ref, o_ref, nq_ref, kv_ref, win_ref, g_ref = refs
    xb = x_ref[...].astype(BF16)
    z = _dot(xb, w_ref[...]) + b_ref[...]
    qk_ref[...] = z[:, _C_QK:_C_V]
    v_ref[...] = z[:, _C_V:_C_O]
    o_ref[...] = z[:, _C_O:_C_NQ]
    nq_ref[...] = z[:, _C_NQ:_C_KV]
    kv_ref[...] = z[:, _C_KV:_C_WIN]
    win_ref[...] = z[:, _C_WIN:_C_G]
    g_ref[...] = z[:, _C_G:IN_PAD]
    if kv_transposed:
        kvt_ref[0] = _dot_nt(wkvt_ref[...], xb) + bkv_ref[...]


def _proj_in(x2d, w_p, b_p, seq_len=None):
    rows = x2d.shape[0]
    tm = min(rows, 512)
    widths = (1024, 512, 512, 512, 512, 256, 128)
    const = lambda i: (0, 0)
    const_spec = lambda shape: pl.BlockSpec(shape, const, pipeline_mode=pl.Buffered(1))
    in_specs = [pl.BlockSpec((tm, D_MODEL), lambda i: (i, 0)), const_spec((D_MODEL, IN_PAD)), const_spec((1, IN_PAD))]
    args = [x2d, w_p, b_p]
    out_specs = [pl.BlockSpec((tm, w), lambda i: (i, 0)) for w in widths]
    out_shape = [jax.ShapeDtypeStruct((rows, w), F32) for w in widths]
    if seq_len is not None:
        tiles = seq_len // tm
        in_specs += [const_spec((4 * KV_W, D_MODEL)), const_spec((4 * KV_W, 1))]
        args += [jnp.transpose(w_p[:, _C_KV:_C_WIN]), b_p[0, _C_KV:_C_WIN].reshape(4 * KV_W, 1)]
        out_specs.append(pl.BlockSpec((1, 4 * KV_W, tm), lambda i: (i // tiles, 0, i % tiles)))
        out_shape.append(jax.ShapeDtypeStruct((rows // seq_len, 4 * KV_W, seq_len), F32))
    return pl.pallas_call(
        functools.partial(_proj_in_body, kv_transposed=seq_len is not None),
        grid=(rows // tm,),
        in_specs=in_specs,
        out_specs=out_specs,
        out_shape=out_shape,
        compiler_params=pltpu.CompilerParams(dimension_semantics=("arbitrary",),
                                             vmem_limit_bytes=VMEM_LIMIT),
        name="proj_in",
    )(*args)


N_PAIRS = N_MLSTM_HEADS // 2
GATE_LANE0 = N_MLSTM_HEADS
MLSTM_SEQS_PER_STEP = 4


def _dot_split(x, onehot_bf16, pieces):
    acc = None
    rem = x
    for n in range(pieces):
        part = rem.astype(BF16)
        d = _dot(part, onehot_bf16)
        acc = d if acc is None else acc + d
        if n + 1 < pieces:
            rem = rem - part.astype(F32)
    return acc


def _mlstm_body(*refs, tb, nbg, n_dummy, has_state):
    if has_state:
        (zqk_ref, zv_ref, zo_ref, zg_ref, c0_ref, n0_ref, m0_ref, cw_ref, cb_ref, nw_ref,
         mix_ref, c_out, n_out, m_out, ubuf, qk_sc, ct_sc, n_sc, m_sc) = refs
    else:
        (zqk_ref, zv_ref, zo_ref, zg_ref, cw_ref, cb_ref, nw_ref,
         mix_ref, c_out, n_out, m_out, ubuf, qk_sc, ct_sc, n_sc, m_sc) = refs
    L = MLSTM_CHUNK
    D = HEAD_DIM
    j = pl.program_id(1)
    hi = lax.Precision.HIGHEST
    head_lanes = slice(GATE_LANE0, GATE_LANE0 + N_MLSTM_HEADS)
    pairs = [(bi, p) for bi in range(nbg) for p in range(N_PAIRS)]

    @pl.when(j == 0)
    def _init():
        ubuf[:, 0:8, :] = jnp.zeros((nbg, 8, 2 * MLSTM_W), F32)
        ct_sc[...] = jnp.zeros(ct_sc.shape, F32)
        n_sc[...] = jnp.zeros(n_sc.shape, F32)
        m_sc[...] = jnp.zeros(m_sc.shape, F32)
        if has_state:
            for bi in range(nbg):
                for h in range(N_MLSTM_HEADS):
                    p, hh = divmod(h, 2)
                    hs = slice(hh * D, (hh + 1) * D)
                    ct_sc[bi, p, hs, hs] = c0_ref[bi, h]
                    n_sc[bi, p, :, hs] = n0_ref[bi, h:h + 1, :]
                m_sc[bi, :, head_lanes] = m0_ref[bi]

    for bi in range(nbg):
        u = zqk_ref[bi]
        ubuf[bi, 8:8 + tb, :] = u
        conv = cb_ref[...] + ubuf[bi, 5:5 + tb, :] * cw_ref[0:1, :]
        conv = conv + ubuf[bi, 6:6 + tb, :] * cw_ref[1:2, :]
        conv = conv + ubuf[bi, 7:7 + tb, :] * cw_ref[2:3, :]
        conv = conv + u * cw_ref[3:4, :]
        ubuf[bi, 5:8, :] = u[tb - 3:tb, :]
        qk_sc[bi] = conv * jax.nn.sigmoid(conv)

    tri = (lax.broadcasted_iota(I32, (L, L), 1) <= lax.broadcasted_iota(I32, (L, L), 0)).astype(F32)
    row_l = lax.broadcasted_iota(I32, (L, LANES), 0)
    lane = lax.broadcasted_iota(I32, (L, LANES), 1)
    low = lane < D
    key_tok = lane % D
    causal_pair = key_tok <= row_l
    eye_pair = key_tok == row_l
    spread = (lax.broadcasted_iota(I32, (LANES, MLSTM_W), 0) - GATE_LANE0
              == lax.broadcasted_iota(I32, (LANES, MLSTM_W), 1) // D).astype(BF16)
    same_head = (lax.broadcasted_iota(I32, (LANES, LANES), 0) // D
                 == lax.broadcasted_iota(I32, (LANES, LANES), 1) // D)
    ones_bd = same_head.astype(BF16)
    slab_rows = 3 * L + 8

    def chunk(ci, carry):
        r0 = pl.multiple_of(ci * L, L)
        rows = pl.ds(r0, L)
        gates = []
        for bi in range(nbg):
            g = zg_ref[bi, rows, :]
            if n_dummy:
                dummy = (row_l + ci * L) < n_dummy
                g = jnp.where(dummy, jnp.where(lane < G_FGATE, -1e30, 1e30), g)
            gates.append(g)
        b_all = _dot(tri, jnp.concatenate([_log_sigmoid(g) for g in gates], axis=1), hi)
        stacks = []
        for bi in range(nbg):
            b = b_all[:, bi * LANES:(bi + 1) * LANES]
            r = pltpu.roll(gates[bi], GATE_LANE0 - G_IGATE, 1) - b
            cm = r
            for sh in (1, 2, 4, 8, 16, 32):
                cm = jnp.where(row_l >= sh, jnp.maximum(cm, pltpu.roll(cm, sh, 0)), cm)
            m_st = m_sc[bi]
            mx = jnp.maximum(m_st, cm)
            mx_end = mx[L - 1:L, :]
            m_sc[bi] = b[L - 1:L, :] + mx_end
            stacks += [mx, r, b + mx, m_st, mx_end, jnp.zeros((6, LANES), F32)]
        x_all = _dot_split(jnp.concatenate(stacks, axis=0), spread, 2)

        work = []
        den_terms = []
        for bi, p in pairs:
            cs = slice(p * LANES, (p + 1) * LANES)
            x0 = bi * slab_rows
            mx_p, rs_p, mt_p = x_all[x0:x0 + L, cs], x_all[x0 + L:x0 + 2 * L, cs], x_all[x0 + 2 * L:x0 + 3 * L, cs]
            mst_p, mxe_p = x_all[x0 + 3 * L:x0 + 3 * L + 1, cs], x_all[x0 + 3 * L + 1:x0 + 3 * L + 2, cs]
            wi_p = jnp.exp(mst_p - mx_p)
            fl_p = jnp.exp(-mt_p)
            wk_p = jnp.exp(rs_p - mxe_p)
            sc_p = jnp.exp(mst_p - mxe_p)
            r_row = jnp.sum(jnp.where(eye_pair, rs_p, 0.0), axis=0, keepdims=True)
            w_intra = jnp.exp(jnp.where(causal_pair, r_row - mx_p, NEG_INF))
            q_p = qk_sc[bi, rows, cs]
            k_p = qk_sc[bi, rows, MLSTM_W + p * LANES:MLSTM_W + (p + 1) * LANES] * (D ** -0.5)
            v_p = zv_ref[bi, rows, cs]
            qb = q_p.astype(BF16)
            k_bd = jnp.concatenate([jnp.where(low, k_p, 0.0), jnp.where(low, 0.0, k_p)], axis=0)
            v_bd = jnp.concatenate([jnp.where(low, v_p, 0.0), jnp.where(low, 0.0, v_p)], axis=0).astype(BF16)
            c_t = ct_sc[bi, p]
            both = _dot_nt(qb, jnp.concatenate([k_bd, c_t], axis=0).astype(BF16))
            a = w_intra * both[:, 0:LANES]
            num = _dot(a.astype(BF16), v_bd) + both[:, LANES:2 * LANES] * wi_p
            n_row = n_sc[bi, p]
            den_terms.append(a + wi_p * (q_p * n_row))
            work.append((num, fl_p))
            kw = k_p * wk_p
            upd = _dot_tn(v_p.astype(BF16), kw.astype(BF16))
            ct_sc[bi, p] = sc_p * c_t + jnp.where(same_head, upd, 0.0)
            n_sc[bi, p] = sc_p * n_row + jnp.sum(kw, axis=0, keepdims=True)
        den_all = _dot_split(jnp.concatenate(den_terms, axis=0), ones_bd, 2)

        hhs = []
        for n, (num, fl_p) in enumerate(work):
            hhs.append(num / jnp.maximum(jnp.abs(den_all[n * L:(n + 1) * L, :]), fl_p))
        mu_all = _dot_split(jnp.concatenate(hhs, axis=0), ones_bd, 1) * (1.0 / D)
        hcs = [hh - mu_all[n * L:(n + 1) * L, :] for n, hh in enumerate(hhs)]
        var_all = _dot_split(jnp.concatenate([hc * hc for hc in hcs], axis=0), ones_bd, 1) * (1.0 / D)
        for n, (bi, p) in enumerate(pairs):
            cs = slice(p * LANES, (p + 1) * LANES)
            hn = hcs[n] * lax.rsqrt(var_all[n * L:(n + 1) * L, :] + LN_EPS)
            og = jax.nn.sigmoid(zo_ref[bi, rows, cs])
            mix_ref[bi, rows, cs] = og * (hn * nw_ref[:, cs])
        return carry

    lax.fori_loop(0, tb // L, chunk, 0, unroll=2 if (tb // L) % 2 == 0 else 1)

    @pl.when(j == pl.num_programs(1) - 1)
    def _final():
        for bi in range(nbg):
            for h in range(N_MLSTM_HEADS):
                p, hh = divmod(h, 2)
                hs = slice(hh * D, (hh + 1) * D)
                c_out[bi, h] = ct_sc[bi, p, hs, hs]
                n_out[bi, h:h + 1, :] = n_sc[bi, p, :, hs]
            m_out[bi] = m_sc[bi, :, head_lanes]


def _mlstm(zqk, zv, zo, zg, state, conv_w, conv_b, norm_w, n_dummy):
    bsz, t_len, _ = zqk.shape
    tb = min(t_len, 256)
    nbg = MLSTM_SEQS_PER_STEP
    assert bsz % nbg == 0
    has_state = state is not None
    row = lambda w: pl.BlockSpec((nbg, tb, w), lambda b, j: (b, j, 0))
    per_b = lambda shape: pl.BlockSpec((nbg,) + shape, lambda b, j: (b,) + (0,) * len(shape))
    const = lambda shape: pl.BlockSpec(shape, lambda b, j: (0,) * len(shape))
    in_specs = [row(2 * MLSTM_W), row(MLSTM_W), row(MLSTM_W), row(LANES)]
    args = [zqk, zv, zo, zg]
    if has_state:
        c0, n0, m0 = state
        in_specs += [per_b((N_MLSTM_HEADS, HEAD_DIM, HEAD_DIM)), per_b((N_MLSTM_HEADS, HEAD_DIM)),
                     per_b((1, N_MLSTM_HEADS))]
        args += [c0, n0, m0.reshape(bsz, 1, N_MLSTM_HEADS)]
    in_specs += [const((MLSTM_CONV, 2 * MLSTM_W)), const((1, 2 * MLSTM_W)), const((1, MLSTM_W))]
    args += [conv_w, conv_b.reshape(1, -1), norm_w.reshape(1, -1)]
    mix, c_new, n_new, m_new = pl.pallas_call(
        functools.partial(_mlstm_body, tb=tb, nbg=nbg, n_dummy=n_dummy, has_state=has_state),
        grid=(bsz // nbg, t_len // tb),
        in_specs=in_specs,
        out_specs=[row(MLSTM_W), per_b((N_MLSTM_HEADS, HEAD_DIM, HEAD_DIM)),
                   per_b((N_MLSTM_HEADS, HEAD_DIM)), per_b((1, N_MLSTM_HEADS))],
        out_shape=[jax.ShapeDtypeStruct((bsz, t_len, MLSTM_W), F32),
                   jax.ShapeDtypeStruct((bsz, N_MLSTM_HEADS, HEAD_DIM, HEAD_DIM), F32),
                   jax.ShapeDtypeStruct((bsz, N_MLSTM_HEADS, HEAD_DIM), F32),
                   jax.ShapeDtypeStruct((bsz, 1, N_MLSTM_HEADS), F32)],
        scratch_shapes=[pltpu.VMEM((nbg, 8 + tb, 2 * MLSTM_W), F32),
                        pltpu.VMEM((nbg, tb, 2 * MLSTM_W), F32),
                        pltpu.VMEM((nbg, N_PAIRS, LANES, LANES), F32),
                        pltpu.VMEM((nbg, N_PAIRS, 1, LANES), F32),
                        pltpu.VMEM((nbg, 1, LANES), F32)],
        compiler_params=pltpu.CompilerParams(dimension_semantics=("arbitrary", "arbitrary"),
                                             vmem_limit_bytes=VMEM_LIMIT),
        name="mlstm",
    )(*args)
    return mix, c_new, n_new, m_new.reshape(bsz, N_MLSTM_HEADS)


def _block_diag_queries(x, n_rows, heads=tuple(range(N_NSA_HEADS)), scale=HEAD_DIM ** -0.5):
    lane = lax.broadcasted_iota(I32, (n_rows, LANES), 1)
    low = lane < HEAD_DIM
    pieces = []
    for h in heads:
        pair = x[:, (h // 2) * LANES:(h // 2 + 1) * LANES]
        src_low = h % 2 == 0
        dst_low = h // GQ == 0
        if src_low != dst_low:
            pair = pltpu.roll(pair, HEAD_DIM, 1)
        pieces.append(jnp.where(low if dst_low else ~low, pair, 0.0))
    return (jnp.concatenate(pieces, axis=0) * scale).astype(BF16)


def _group_lanes(o, g):
    return o[:, g * HEAD_DIM:(g + 1) * HEAD_DIM]


def _gated_mix(gates, o_c, o_s, o_w, h, n_rows):
    g = h // GQ
    rs = slice(h * n_rows, (h + 1) * n_rows)
    col = lambda branch: gates[:, G_NSA + branch * N_NSA_HEADS + h:G_NSA + branch * N_NSA_HEADS + h + 1]
    return (col(0) * _group_lanes(o_c[rs], g) + col(1) * _group_lanes(o_s[rs], g)
            + col(2) * _group_lanes(o_w[rs], g))


SEL_TILE = 256
SEL_TILE_BLOCKS = SEL_TILE // SEL_BLOCK
FAR_GROUP = 2
Q_PER_STEP = 4
WIN_BAND = WINDOW + NSA_QBLOCK
N_COLS = N_NSA_HEADS * NSA_QBLOCK
PROMPT_HEADS = tuple(GQ * (jj % N_KV_HEADS) + jj // N_KV_HEADS for jj in range(N_NSA_HEADS))
SEL_REL0 = 2 * SEL_TILE - SEL_BLOCK
SEL_BIAS_ROWS = SEL_REL0 + SEL_TILE


LOG2E = math.log2(math.e)


def _softmax2_rows(s):
    mx = jnp.max(s, 0, keepdims=True)
    mx = jnp.where(mx > NEG_INF, mx, 0.0)
    e = jnp.exp2(s - mx)
    return e / jnp.maximum(jnp.sum(e, 0, keepdims=True), 1e-30)


def _nsa_prompt_body(rb_ref, nq_ref, nqn_ref, kv_ref, win_ref, zg_ref, pw_ref, out_ref,
                     ks_sc, vst_sc, wk_sc, wv_sc, kc_sc, pv_sc, vct_sc, bs_sc, bw_sc, fc_sc,
                     m_sc, l_sc, acc_sc, ch_sc, oc_sc, fs_sc, ft_sc, *, t_len):
    QB = NSA_QBLOCK
    n_cmp = t_len // CMP_BLOCK
    n_sel = t_len // SEL_BLOCK
    pid = pl.program_id(1)
    hi = lax.Precision.HIGHEST

    def cmp_scores(x_q):
        return _dot_nt(kc_sc[...], _block_diag_queries(x_q, QB, PROMPT_HEADS, HEAD_DIM ** -0.5 * LOG2E))

    def cmp_and_select(raw, qi, out_buf):
        s_c = raw + pltpu.roll(fc_sc[...], 2 * qi, 0)
        c_i = lax.broadcasted_iota(I32, (n_cmp, N_COLS), 0)
        q_i = lax.broadcasted_iota(I32, (n_cmp, N_COLS), 1) % QB
        p_c = _softmax2_rows(jnp.where(c_i * CMP_BLOCK + (CMP_BLOCK - 1) <= qi * QB + q_i, s_c, NEG_INF))
        oc_sc[out_buf] = _dot(vct_sc[...], p_c.astype(BF16))
        p_sum = p_c[:, 0:LANES]
        for j in range(1, GQ):
            p_sum = p_sum + p_c[:, j * LANES:(j + 1) * LANES]
        ratio = SEL_BLOCK // CMP_BLOCK
        pair = (lax.broadcasted_iota(I32, (n_sel, n_cmp), 1) // ratio
                == lax.broadcasted_iota(I32, (n_sel, n_cmp), 0)).astype(F32)
        imp = _dot(pair, p_sum, hi)
        blk = lax.broadcasted_iota(I32, (n_sel, LANES), 0)
        score = jnp.where(blk == qi, jnp.inf, jnp.where(blk < qi, imp, NEG_INF))

        def pick(_, carry):
            work, chosen = carry
            best = jnp.max(work, 0, keepdims=True)
            first = jnp.min(jnp.where(work == best, blk, n_sel), 0, keepdims=True)
            hit = blk == first
            return jnp.where(hit, NEG_INF, work), jnp.where(hit, 1.0, chosen)

        _, chosen = lax.fori_loop(0, SEL_TOPK, pick, (score, jnp.zeros((n_sel, LANES), F32)), unroll=True)
        ch_sc[out_buf, 0:8, :] = jnp.full((8, LANES), NEG_INF, F32)
        ch_sc[out_buf, 8:8 + n_sel, :] = jnp.where((score > NEG_INF) & (chosen > 0.5), 0.0, NEG_INF)

    @pl.when(pid == 0)
    def _init():
        ks_sc[0:SEL_TILE, :] = jnp.zeros((SEL_TILE, LANES), BF16)
        vst_sc[0] = jnp.zeros((LANES, SEL_TILE), BF16)
        wk_sc[0:WINDOW, :] = jnp.zeros((WINDOW, LANES), BF16)
        wv_sc[0:WINDOW, :] = jnp.zeros((WINDOW, LANES), BF16)
        rows_per = 2 * SEL_TILE
        blocks_per = rows_per // CMP_BLOCK

        def fill(c, carry):
            r0 = pl.multiple_of(c * rows_per, rows_per)
            kv = kv_ref[0, pl.ds(r0, rows_per), :]
            ks_sc[pl.ds(SEL_TILE + r0, rows_per), :] = kv[:, 2 * LANES:3 * LANES].astype(BF16)
            for half in range(2):
                v_t = kv[half * SEL_TILE:(half + 1) * SEL_TILE, 3 * LANES:4 * LANES]
                vst_sc[2 * c + half + 1] = jnp.transpose(v_t).astype(BF16)
            w = win_ref[0, pl.ds(r0, rows_per), :]
            wk_sc[pl.ds(WINDOW + r0, rows_per), :] = w[:, 0:LANES].astype(BF16)
            wv_sc[pl.ds(WINDOW + r0, rows_per), :] = w[:, LANES:2 * LANES].astype(BF16)
            pooled = jnp.sum(kv[:, 0:2 * LANES].reshape(blocks_per, CMP_BLOCK, 2 * LANES) * pw_ref[...][None],
                             axis=1)
            c0 = pl.multiple_of(c * blocks_per, blocks_per)
            kc_sc[pl.ds(c0, blocks_per), :] = pooled[:, 0:LANES].astype(BF16)
            pv_sc[pl.ds(c0, blocks_per), :] = pooled[:, LANES:2 * LANES]
            return carry

        lax.fori_loop(0, t_len // rows_per, fill, 0)
        vct_sc[...] = jnp.transpose(pv_sc[...]).astype(BF16)

        for jj, h in enumerate(PROMPT_HEADS):
            cs = slice(jj * QB, (jj + 1) * QB)
            u_i = lax.broadcasted_iota(I32, (SEL_BIAS_ROWS, QB), 0)
            q_i = lax.broadcasted_iota(I32, (SEL_BIAS_ROWS, QB), 1)
            dist = q_i - (u_i - SEL_REL0)
            near = (_bias_from_dist(dist, rb_ref, h) - rb_ref[N_BUCKETS - 1, h]) * LOG2E
            bs_sc[:, cs] = jnp.where(dist >= 0, near, NEG_INF)
            s_i = lax.broadcasted_iota(I32, (WIN_BAND, QB), 0)
            q_i = lax.broadcasted_iota(I32, (WIN_BAND, QB), 1)
            dist = q_i + WINDOW - s_i
            bw_sc[:, cs] = jnp.where((dist >= 0) & (dist <= WINDOW), _bias_from_dist(dist, rb_ref, h) * LOG2E,
                                     NEG_INF)
            c_i = lax.broadcasted_iota(I32, (n_cmp, QB), 0)
            q_i = lax.broadcasted_iota(I32, (n_cmp, QB), 1)
            rel = jnp.where(c_i < n_cmp // 2, c_i, c_i - n_cmp)
            dist = q_i - CMP_BLOCK * rel - (CMP_BLOCK - 1)
            dist = jnp.where((c_i >= 2) & (c_i < n_cmp // 2), FAR_DIST, dist)
            fc_sc[:, cs] = _bias_from_dist(dist, rb_ref, h) * LOG2E

        cmp_and_select(cmp_scores(nq_ref[0, 0:QB, :]), 0, 0)

    def q_block(t):
        i = Q_PER_STEP * pid + t
        rows = slice(t * QB, (t + 1) * QB)
        x_q = nq_ref[0, rows, :]
        x_next = nq_ref[0, (t + 1) * QB:(t + 2) * QB, :] if t + 1 < Q_PER_STEP else nqn_ref[0, 0:QB, :]
        buf = t % 2
        o_c = oc_sc[buf]
        qbd = _block_diag_queries(x_q, QB, PROMPT_HEADS, HEAD_DIM ** -0.5 * LOG2E)
        last = i // SEL_TILE_BLOCKS

        def tile_raw(slot):
            return _dot_nt(ks_sc[pl.ds(pl.multiple_of(slot * SEL_TILE, SEL_TILE), SEL_TILE), :], qbd)

        raw_next = cmp_scores(x_next)
        w0 = pl.multiple_of(i * QB, QB)
        raw_w = _dot_nt(wk_sc[pl.ds(w0, WIN_BAND), :], qbd)
        raw_near = [tile_raw(last + 1), tile_raw(last)]

        def tile_scores(slot, bias, s=None):
            if s is None:
                s = tile_raw(slot)
            if bias is not None:
                s = s + bias
            parts = []
            for r in range(SEL_TILE_BLOCKS):
                row = ch_sc[buf, pl.ds(8 - SEL_TILE_BLOCKS + slot * SEL_TILE_BLOCKS + r, 1), :]
                parts.append(s[r * SEL_BLOCK:(r + 1) * SEL_BLOCK, :] + jnp.concatenate([row] * GQ, axis=1))
            return jnp.concatenate(parts, axis=0)

        def group_max(scores):
            return functools.reduce(jnp.maximum, [jnp.max(s, 0, keepdims=True) for s in scores])

        def sel_group(slots, scores, t_max, first):
            m_new = t_max if first else jnp.maximum(m_sc[...], t_max)
            exps = [jnp.exp2(s - m_new) for s in scores]
            l_new = functools.reduce(jnp.add, [jnp.sum(e, 0, keepdims=True) for e in exps])
            acc_new = functools.reduce(jnp.add, [_dot(vst_sc[slot], e.astype(BF16)) for slot, e in zip(slots, exps)])
            if first:
                l_sc[...] = l_new
                acc_sc[...] = acc_new
            else:
                scale = jnp.exp2(m_sc[...] - m_new)
                l_sc[...] = scale * l_sc[...] + l_new
                acc_sc[...] = scale * acc_sc[...] + acc_new
            m_sc[...] = m_new

        def far_slots(p):
            return [jnp.where(FAR_GROUP * p + 1 + n < last, FAR_GROUP * p + 1 + n, 0) for n in range(FAR_GROUP)]

        def far_scores(p):
            scores = [tile_scores(slot, None) for slot in far_slots(p)]
            for n, s in enumerate(scores):
                fs_sc[p % 2, n] = s
            ft_sc[p % 2] = group_max(scores)

        far_scores(0)
        cmp_and_select(raw_next, i + 1, 1 - buf)

        in_seq = lax.broadcasted_iota(I32, (WIN_BAND, N_COLS), 0) >= WINDOW - i * QB
        s_w = jnp.where(in_seq, raw_w + bw_sc[...], NEG_INF)
        e_w = jnp.exp2(s_w - jnp.max(s_w, 0, keepdims=True))
        o_w = (_dot_tn(wv_sc[pl.ds(w0, WIN_BAND), :], e_w.astype(BF16))
               / jnp.maximum(jnp.sum(e_w, 0, keepdims=True), 1e-30))

        u0 = pl.multiple_of(SEL_REL0 - SEL_TILE - SEL_BLOCK * (i % SEL_TILE_BLOCKS), SEL_BLOCK)
        near = [tile_scores(last + 1, bs_sc[pl.ds(u0 + SEL_TILE, SEL_TILE), :], raw_near[0]),
                tile_scores(last, bs_sc[pl.ds(u0, SEL_TILE), :], raw_near[1])]
        sel_group((last + 1, last), near, group_max(near), True)

        def far_group(p, carry):
            sel_group(far_slots(p), [fs_sc[p % 2, n] for n in range(FAR_GROUP)], ft_sc[p % 2], False)
            far_scores(p + 1)
            return carry

        lax.fori_loop(0, (last - 1 + FAR_GROUP - 1) // FAR_GROUP, far_group, 0)
        o_s = acc_sc[...] / jnp.maximum(l_sc[...], 1e-30)

        eye = (lax.broadcasted_iota(I32, (3 * N_NSA_HEADS + 8, LANES), 0) + G_NSA
               == lax.broadcasted_iota(I32, (3 * N_NSA_HEADS + 8, LANES), 1)).astype(F32)
        gates = jax.nn.sigmoid(_dot_nt(eye, zg_ref[0, rows, :], hi))
        pieces = []
        for h in range(N_NSA_HEADS):
            g = h // GQ
            jj = PROMPT_HEADS.index(h)
            part = lambda o: o[g * HEAD_DIM:(g + 1) * HEAD_DIM, jj * QB:(jj + 1) * QB]
            pieces.append(gates[h:h + 1, :] * part(o_c)
                          + gates[N_NSA_HEADS + h:N_NSA_HEADS + h + 1, :] * part(o_s)
                          + gates[2 * N_NSA_HEADS + h:2 * N_NSA_HEADS + h + 1, :] * part(o_w))
        mix_t = jnp.concatenate(pieces, axis=0)
        mix_t = jnp.concatenate([mix_t, jnp.zeros((NSA_W, LANES - QB), F32)], axis=1)
        out_ref[0, rows, :] = jnp.transpose(mix_t)[0:QB, :]

    for t in range(Q_PER_STEP):
        q_block(t)


def _nsa_prompt(nq, kv, win, zg, pool_w2, rel_bias):
    bsz, t_len, _ = nq.shape
    assert t_len // CMP_BLOCK == LANES and t_len % (2 * SEL_TILE) == 0
    n_cmp = t_len // CMP_BLOCK
    per_b = lambda w: pl.BlockSpec((1, t_len, w), lambda b, i: (b, 0, 0))
    step_rows = Q_PER_STEP * NSA_QBLOCK
    blk = lambda w: pl.BlockSpec((1, step_rows, w), lambda b, i: (b, i, 0))
    n_qb = t_len // step_rows
    next_blk = pl.BlockSpec((1, step_rows, NSA_W), lambda b, i: (b, jnp.minimum(i + 1, n_qb - 1), 0))
    return pl.pallas_call(
        functools.partial(_nsa_prompt_body, t_len=t_len),
        grid=(bsz, n_qb),
        in_specs=[pl.BlockSpec(memory_space=pltpu.SMEM),
                  blk(NSA_W), next_blk, per_b(4 * KV_W), per_b(2 * KV_W), blk(LANES),
                  pl.BlockSpec((CMP_BLOCK, 2 * LANES), lambda b, i: (0, 0))],
        out_specs=blk(NSA_W),
        out_shape=jax.ShapeDtypeStruct((bsz, t_len, NSA_W), F32),
        scratch_shapes=[pltpu.VMEM((SEL_TILE + t_len, LANES), BF16),
                        pltpu.VMEM((1 + t_len // SEL_TILE, LANES, SEL_TILE), BF16),
                        pltpu.VMEM((WINDOW + t_len, LANES), BF16),
                        pltpu.VMEM((WINDOW + t_len, LANES), BF16),
                        pltpu.VMEM((n_cmp, LANES), BF16),
                        pltpu.VMEM((n_cmp, LANES), F32),
                        pltpu.VMEM((LANES, n_cmp), BF16),
                        pltpu.VMEM((SEL_BIAS_ROWS, N_COLS), F32),
                        pltpu.VMEM((WIN_BAND, N_COLS), F32),
                        pltpu.VMEM((n_cmp, N_COLS), F32),
                        pltpu.VMEM((1, N_COLS), F32),
                        pltpu.VMEM((1, N_COLS), F32),
                        pltpu.VMEM((LANES, N_COLS), F32),
                        pltpu.VMEM((2, 8 + t_len // SEL_BLOCK, LANES), F32),
                        pltpu.VMEM((2, LANES, N_COLS), F32),
                        pltpu.VMEM((2, FAR_GROUP, SEL_TILE, N_COLS), F32),
                        pltpu.VMEM((2, 1, N_COLS), F32)],
        compiler_params=pltpu.CompilerParams(dimension_semantics=("arbitrary", "arbitrary"),
                                             vmem_limit_bytes=VMEM_LIMIT),
        name="nsa_prompt",
    )(rel_bias, nq, nq, kv, win, zg, pool_w2)


CMP_PAGES = 32
CMP_SLOTS = 3
N_PICK = SEL_TOPK - 1
N_OWNERS = N_KV_HEADS * N_NEW
N_FETCH = N_OWNERS * N_PICK
OWNER_KEYS = N_PICK * PAGE_SIZE


def _sample_row_ids():
    r = lax.broadcasted_iota(I32, (N_QROWS, 1), 0)
    return r % SEQ_PAD - (SEQ_PAD - N_NEW), r // SEQ_PAD


def _sample_bias(dist, rb_ref):
    _, head = _sample_row_ids()
    col = lambda k: sum(jnp.where(head == h, rb_ref[k, h], 0.0) for h in range(N_NSA_HEADS))
    acc = jnp.broadcast_to(col(0), dist.shape)
    for k in range(1, N_BUCKETS):
        acc = jnp.where(dist >= BUCKET_THR[k], col(k), acc)
    return acc


def _sample_bias_table(rb_ref):
    assert FAR_DIST < LANES
    return _sample_bias(lax.broadcasted_iota(I32, (N_QROWS, LANES), 1), rb_ref)


def _sample_bias_lookup(dist, table):
    idx = jnp.clip(dist, 0, LANES - 1)
    return jnp.concatenate([jnp.take_along_axis(table, idx[:, c * LANES:(c + 1) * LANES], axis=1)
                            for c in range(dist.shape[1] // LANES)], axis=1)


def _softmax_two(s1, s2):
    mx = jnp.maximum(jnp.max(s1, -1, keepdims=True), jnp.max(s2, -1, keepdims=True))
    mx = jnp.where(mx > NEG_INF, mx, 0.0)
    e1 = jnp.exp(s1 - mx)
    e2 = jnp.exp(s2 - mx)
    den = jnp.maximum(jnp.sum(e1, -1, keepdims=True) + jnp.sum(e2, -1, keepdims=True), 1e-30)
    return e1 / den, e2 / den


def _nsa_cmp_body(pt_ref, rb_ref, nq_ref, pwt_ref, cache_ref, oc_ref, idx_ref,
                  buf, sem, seg_sc, kct_sc, vct_sc, *, n_pages, past_len):
    b = pl.program_id(0)
    nb = pl.num_programs(0)
    n_chunks = n_pages // CMP_PAGES
    chunk_rows = CMP_PAGES * PAGE_SIZE
    chunk_blocks = chunk_rows // CMP_BLOCK
    n_cmp = past_len // CMP_BLOCK
    n_sel_past = past_len // SEL_BLOCK
    n_prob = N_KV_HEADS * SEQ_PAD
    total = nb * n_chunks
    hi = lax.Precision.HIGHEST

    def page_copy(gi, p):
        page = pt_ref[(gi // n_chunks) * n_pages + (gi % n_chunks) * CMP_PAGES + p]
        slot = gi % CMP_SLOTS
        return pltpu.make_async_copy(cache_ref.at[page, pl.ds(0, 2 * LANES), :], buf.at[slot, p], sem.at[slot])

    def start_chunk(gi):
        for p in range(CMP_PAGES):
            page_copy(gi, p).start()

    def wait_chunk(gi):
        for p in range(CMP_PAGES):
            page_copy(gi, p).wait()

    @pl.when(b == 0)
    def _prologue():
        seg_sc[...] = (lax.broadcasted_iota(I32, (chunk_rows, chunk_blocks), 0) // CMP_BLOCK
                       == lax.broadcasted_iota(I32, (chunk_rows, chunk_blocks), 1)).astype(BF16)
        for gi in range(min(CMP_SLOTS - 1, total)):
            start_chunk(gi)

    def chunk_step(c, carry):
        gi = b * n_chunks + c
        ahead = gi + (CMP_SLOTS - 1)

        @pl.when(ahead < total)
        def _prefetch():
            start_chunk(ahead)

        wait_chunk(gi)
        slot = gi % CMP_SLOTS
        y = jnp.concatenate([(buf[slot, p] * pwt_ref[...]).astype(BF16) for p in range(CMP_PAGES)], axis=1)
        pooled = _dot(y, seg_sc[...])
        kct_sc[c] = pooled[0:LANES].astype(BF16)
        vct_sc[c] = pooled[LANES:2 * LANES].astype(BF16)
        return carry

    lax.fori_loop(0, n_chunks, chunk_step, 0)

    qbd = _block_diag_queries(nq_ref[0], SEQ_PAD)
    tok, _ = _sample_row_ids()
    s_c = jnp.concatenate([_dot(qbd, kct_sc[c]) for c in range(n_chunks)], axis=1)
    near0 = n_cmp - LANES
    assert past_len - (SEQ_PAD - N_NEW) - (CMP_BLOCK * (near0 - 1) + CMP_BLOCK - 1) >= FAR_DIST
    dist = past_len + tok - ((lax.broadcasted_iota(I32, (N_QROWS, LANES), 1) + near0) * CMP_BLOCK + CMP_BLOCK - 1)
    near = jnp.where(dist >= 0, s_c[:, near0:] + _sample_bias(dist, rb_ref), NEG_INF)
    far = s_c[:, :near0] + _sample_bias(jnp.full((N_QROWS, 1), FAR_DIST, I32), rb_ref)
    p_c = _masked_softmax(jnp.concatenate([far, near], axis=1))
    o_c = _dot_nt(p_c[:, 0:chunk_blocks].astype(BF16), vct_sc[0])
    for c in range(1, n_chunks):
        o_c = o_c + _dot_nt(p_c[:, c * chunk_blocks:(c + 1) * chunk_blocks].astype(BF16), vct_sc[c])
    oc_ref[0] = o_c

    p_sum = jnp.sum(p_c.reshape(N_KV_HEADS, GQ, SEQ_PAD, n_cmp), axis=1).reshape(n_prob, n_cmp)
    p_sum = jnp.concatenate([p_sum, jnp.zeros((LANES - n_prob, n_cmp), F32)], axis=0)
    ratio = SEL_BLOCK // CMP_BLOCK
    pair = (lax.broadcasted_iota(I32, (n_sel_past, n_cmp), 1) // ratio
            == lax.broadcasted_iota(I32, (n_sel_past, n_cmp), 0)).astype(F32)
    imp = _dot_nt(pair, p_sum, hi)
    blk = lax.broadcasted_iota(I32, (n_sel_past, LANES), 0)
    pick_i = lax.broadcasted_iota(I32, (SEL_TOPK, LANES), 0)

    def pick(k, carry):
        work, ids = carry
        best = jnp.max(work, 0, keepdims=True)
        first = jnp.min(jnp.where(work == best, blk, n_sel_past), 0, keepdims=True)
        return jnp.where(blk == first, NEG_INF, work), jnp.where(pick_i == k, first, ids)

    _, ids = lax.fori_loop(0, N_PICK, pick, (imp, jnp.zeros((SEL_TOPK, LANES), I32)), unroll=True)
    idx_ref[0] = ids


def _nsa_cmp(page_table, rel_bias, nq_s, pool_wt, cache3):
    dec_b, n_pages = page_table.shape
    past_len = n_pages * PAGE_SIZE
    assert n_pages % CMP_PAGES == 0 and past_len // SEL_BLOCK >= N_PICK
    n_chunks = n_pages // CMP_PAGES
    chunk_rows = CMP_PAGES * PAGE_SIZE
    chunk_blocks = chunk_rows // CMP_BLOCK
    assert chunk_blocks == LANES
    grid_spec = pltpu.PrefetchScalarGridSpec(
        num_scalar_prefetch=1,
        grid=(dec_b,),
        in_specs=[pl.BlockSpec(memory_space=pltpu.SMEM),
                  pl.BlockSpec((1, SEQ_PAD, NSA_W), lambda b, pt: (b, 0, 0)),
                  pl.BlockSpec((2 * LANES, PAGE_SIZE), lambda b, pt: (0, 0)),
                  pl.BlockSpec(memory_space=pl.ANY)],
        out_specs=[pl.BlockSpec((1, N_QROWS, LANES), lambda b, pt: (b, 0, 0)),
                   pl.BlockSpec((1, SEL_TOPK, LANES), lambda b, pt: (b, 0, 0))],
        scratch_shapes=[pltpu.VMEM((CMP_SLOTS, CMP_PAGES, 2 * LANES, PAGE_SIZE), F32),
                        pltpu.SemaphoreType.DMA((CMP_SLOTS,)),
                        pltpu.VMEM((chunk_rows, chunk_blocks), BF16),
                        pltpu.VMEM((n_chunks, LANES, chunk_blocks), BF16),
                        pltpu.VMEM((n_chunks, LANES, chunk_blocks), BF16)],
    )
    return pl.pallas_call(
        functools.partial(_nsa_cmp_body, n_pages=n_pages, past_len=past_len),
        grid_spec=grid_spec,
        out_shape=[jax.ShapeDtypeStruct((dec_b, N_QROWS, LANES), F32),
                   jax.ShapeDtypeStruct((dec_b, SEL_TOPK, LANES), I32)],
        compiler_params=pltpu.CompilerParams(dimension_semantics=("arbitrary",),
                                             vmem_limit_bytes=VMEM_LIMIT),
        name="nsa_sample_cmp",
    )(page_table.reshape(-1), rel_bias, nq_s, pool_wt, cache3)


def _nsa_sel_body(pt_ref, ids_ref, rb_ref, nq_ref, kvn_ref, winc_ref, winn_ref, zg_ref, oc_ref, idv_ref,
                  cache_ref, out_ref, buf, sem, *, n_pages, past_len):
    b = pl.program_id(0)
    nb = pl.num_programs(0)
    hi = lax.Precision.HIGHEST

    def block_copy(bb, n, slot):
        blk = ids_ref[bb * N_FETCH + n]
        page = pt_ref[bb * n_pages + blk // 2]
        return pltpu.make_async_copy(cache_ref.at[page, pl.ds(2 * LANES, 2 * LANES), :], buf.at[slot, n],
                                     sem.at[slot])

    def start_all(bb, slot):
        def go(n, carry):
            block_copy(bb, n, slot).start()
            return carry
        lax.fori_loop(0, N_FETCH, go, 0, unroll=8)

    def wait_all(bb, slot):
        def go(n, carry):
            block_copy(bb, n, slot).wait()
            return carry
        lax.fori_loop(0, N_FETCH, go, 0, unroll=8)

    slot = b % 2

    @pl.when(b == 0)
    def _prologue():
        start_all(0, 0)

    @pl.when(b + 1 < nb)
    def _prefetch():
        start_all(b + 1, 1 - slot)

    qbd = _block_diag_queries(nq_ref[0], SEQ_PAD)
    tok, head = _sample_row_ids()
    owner = (head // GQ) * N_NEW + tok

    wc = winc_ref[0]
    wn = winn_ref[0]
    r_i = lax.broadcasted_iota(I32, (N_QROWS, WINDOW), 1)
    dist1 = WINDOW + tok - r_i
    bias_table = _sample_bias_table(rb_ref)
    s1 = _dot(qbd, wc[0:LANES, :].astype(BF16)) + _sample_bias_lookup(dist1, bias_table)
    s1 = jnp.where((dist1 >= 0) & (dist1 <= WINDOW), s1, NEG_INF)
    n_i = lax.broadcasted_iota(I32, (N_QROWS, SEQ_PAD), 1) - (SEQ_PAD - N_NEW)
    dist2 = tok - n_i
    new_ok = (n_i >= 0) & (dist2 >= 0)
    bias2 = _sample_bias(dist2, rb_ref)
    s2 = jnp.where(new_ok, _dot_nt(qbd, wn[:, 0:LANES].astype(BF16)) + bias2, NEG_INF)
    p1, p2 = _softmax_two(s1, s2)
    o_w = (_dot_nt(p1.astype(BF16), wc[LANES:2 * LANES, :].astype(BF16))
           + _dot(p2.astype(BF16), wn[:, LANES:2 * LANES].astype(BF16)))

    wait_all(b, slot)

    def owner_tiles(o, r0):
        return jnp.concatenate([buf[slot, o * N_PICK + k, r0:r0 + LANES, :] for k in range(N_PICK)],
                               axis=1).astype(BF16)

    s_f = jnp.full((N_QROWS, OWNER_KEYS), NEG_INF, F32)
    for o in range(N_OWNERS):
        s_f = jnp.where(owner == o, _dot(qbd, owner_tiles(o, 0)), s_f)
    spread = (lax.broadcasted_iota(I32, (SEL_TOPK, OWNER_KEYS), 0)
              == lax.broadcasted_iota(I32, (SEL_TOPK, OWNER_KEYS), 1) // PAGE_SIZE).astype(F32)
    blk_of_key = _dot_tn(idv_ref[0].astype(F32), spread, hi)
    blk_rows = jnp.concatenate([blk_of_key[(h // GQ) * SEQ_PAD:(h // GQ + 1) * SEQ_PAD]
                                for h in range(N_NSA_HEADS)], axis=0).astype(I32)
    blocks_per_page = PAGE_SIZE // SEL_BLOCK
    in_page = lax.broadcasted_iota(I32, (N_QROWS, OWNER_KEYS), 1) % PAGE_SIZE
    in_block = in_page // SEL_BLOCK == blk_rows % blocks_per_page
    dist_s = past_len + tok - ((blk_rows // blocks_per_page) * PAGE_SIZE + in_page)
    s_f = jnp.where((tok >= 0) & in_block & (dist_s >= 0), s_f + _sample_bias_lookup(dist_s, bias_table), NEG_INF)
    kvn = kvn_ref[0]
    s_n = jnp.where(new_ok, _dot_nt(qbd, kvn[:, 2 * LANES:3 * LANES].astype(BF16)) + bias2, NEG_INF)
    p_f, p_n = _softmax_two(s_f, s_n)
    o_s = _dot(p_n.astype(BF16), kvn[:, 3 * LANES:4 * LANES].astype(BF16))
    for o in range(N_OWNERS):
        o_s = o_s + _dot_nt(jnp.where(owner == o, p_f, 0.0).astype(BF16), owner_tiles(o, LANES))

    o_c = oc_ref[0]
    gates = jax.nn.sigmoid(zg_ref[0])
    is_new = lax.broadcasted_iota(I32, (SEQ_PAD, HEAD_DIM), 0) >= SEQ_PAD - N_NEW
    for h in range(N_NSA_HEADS):
        mix = _gated_mix(gates, o_c, o_s, o_w, h, SEQ_PAD)
        out_ref[0, :, h * HEAD_DIM:(h + 1) * HEAD_DIM] = jnp.where(is_new, mix, 0.0)


def _nsa_sel(page_table, ids, ids_rows, rel_bias, nq_s, kv_s, win_cache, win_s, zg_s, o_c, cache3):
    dec_b, n_pages = page_table.shape
    past_len = n_pages * PAGE_SIZE
    seq = lambda w: pl.BlockSpec((1, SEQ_PAD, w), lambda b, pt, ix: (b, 0, 0))
    grid_spec = pltpu.PrefetchScalarGridSpec(
        num_scalar_prefetch=2,
        grid=(dec_b,),
        in_specs=[pl.BlockSpec(memory_space=pltpu.SMEM),
                  seq(NSA_W), seq(4 * KV_W),
                  pl.BlockSpec((1, 2 * KV_W, WINDOW), lambda b, pt, ix: (b, 0, 0)),
                  seq(2 * KV_W), seq(LANES),
                  pl.BlockSpec((1, N_QROWS, LANES), lambda b, pt, ix: (b, 0, 0)),
                  pl.BlockSpec((1, SEL_TOPK, LANES), lambda b, pt, ix: (b, 0, 0)),
                  pl.BlockSpec(memory_space=pl.ANY)],
        out_specs=seq(NSA_W),
        scratch_shapes=[pltpu.VMEM((2, N_FETCH, 2 * LANES, PAGE_SIZE), F32),
                        pltpu.SemaphoreType.DMA((2,))],
    )
    return pl.pallas_call(
        functools.partial(_nsa_sel_body, n_pages=n_pages, past_len=past_len),
        grid_spec=grid_spec,
        out_shape=jax.ShapeDtypeStruct((dec_b, SEQ_PAD, NSA_W), F32),
        compiler_params=pltpu.CompilerParams(dimension_semantics=("arbitrary",),
                                             vmem_limit_bytes=VMEM_LIMIT),
        name="nsa_sample_sel",
    )(page_table.reshape(-1), ids.reshape(-1), rel_bias, nq_s, kv_s, win_cache, win_s, zg_s, o_c, ids_rows, cache3)


FF_CHUNK = 896


def _ffn_body(*refs, tm, has_override):
    if has_override:
        (x_ref, mm_ref, mn_ref, ov_ref, woa_ref, wob_ref, l1w_ref, l1b_ref, wup_ref, bup_ref, cw_ref, cb_ref,
         wdn_ref, bdn_ref, l2w_ref, l2b_ref, y_ref, tail_ref, carry_sc, ubuf) = refs
    else:
        (x_ref, mm_ref, mn_ref, woa_ref, wob_ref, l1w_ref, l1b_ref, wup_ref, bup_ref, cw_ref, cb_ref,
         wdn_ref, bdn_ref, l2w_ref, l2b_ref, y_ref, tail_ref, carry_sc, ubuf) = refs
    j = pl.program_id(1)

    @pl.when(j == 0)
    def _init():
        carry_sc[...] = jnp.zeros(carry_sc.shape, F32)

    x = x_ref[0]
    h = ALPHA * x + (_dot(mm_ref[0].astype(BF16), woa_ref[...]) + _dot(mn_ref[0].astype(BF16), wob_ref[...]))
    x1 = _layer_norm(h, l1w_ref[...], l1b_ref[...])
    x1b = x1.astype(BF16)
    if has_override:
        state_row = lax.broadcasted_iota(I32, (tm, FF_CHUNK), 0) % SEQ_PAD < SEQ_PAD - N_NEW

    def conv_half(c0):
        cols = slice(c0, c0 + FF_CHUNK)
        u = _dot(x1b, wup_ref[:, cols]) + bup_ref[:, cols]
        if has_override:
            u = jnp.where(state_row, ov_ref[0, :, cols], u)
        ubuf[0:8, :] = carry_sc[:, cols]
        ubuf[8:8 + tm, :] = u
        out = cb_ref[:, cols] + ubuf[6:6 + tm, :] * cw_ref[0:1, cols]
        out = out + ubuf[7:7 + tm, :] * cw_ref[1:2, cols]
        out = out + u * cw_ref[2:3, cols]
        carry_sc[:, cols] = u[tm - 8:tm, :]
        tail_ref[0, 0, :, cols] = u if has_override else u[tm - 8:tm, :]
        return out

    ff = jnp.zeros((tm, D_MODEL), F32)
    for c in range(D_FF // FF_CHUNK):
        ga = conv_half(c * FF_CHUNK)
        gb = conv_half(D_FF + c * FF_CHUNK)
        gelu = ga * (0.5 * (1.0 + jnp.tanh(math.sqrt(2.0 / math.pi) * (ga + 0.044715 * (ga * ga * ga)))))
        ff = ff + _dot((gelu * gb).astype(BF16), wdn_ref[c * FF_CHUNK:(c + 1) * FF_CHUNK, :])
    ff = ff + bdn_ref[...]
    y_ref[0] = _layer_norm(ALPHA * x1 + ff, l2w_ref[...], l2b_ref[...])


def _ffn(x, mix_m, mix_n, override, wts):
    bsz, t_len, _ = x.shape
    tm = min(t_len, 512)
    nt = t_len // tm
    has_override = override is not None
    tail_rows = tm if has_override else 8
    row = lambda w: pl.BlockSpec((1, tm, w), lambda b, j: (b, j, 0))
    const = lambda shape: pl.BlockSpec(shape, lambda b, j: (0,) * len(shape), pipeline_mode=pl.Buffered(1))
    in_specs = [row(D_MODEL), row(MLSTM_W), row(NSA_W)]
    args = [x, mix_m, mix_n]
    if has_override:
        in_specs.append(row(2 * D_FF))
        args.append(override)
    in_specs += [const(w.shape) for w in wts]
    args += list(wts)
    y, tail = pl.pallas_call(
        functools.partial(_ffn_body, tm=tm, has_override=has_override),
        grid=(bsz, nt),
        in_specs=in_specs,
        out_specs=[row(D_MODEL), pl.BlockSpec((1, 1, tail_rows, 2 * D_FF), lambda b, j: (b, j, 0, 0))],
        out_shape=[jax.ShapeDtypeStruct((bsz, t_len, D_MODEL), F32),
                   jax.ShapeDtypeStruct((bsz, nt, tail_rows, 2 * D_FF), F32)],
        scratch_shapes=[pltpu.VMEM((8, 2 * D_FF), F32), pltpu.VMEM((8 + tm, FF_CHUNK), F32)],
        compiler_params=pltpu.CompilerParams(dimension_semantics=("arbitrary", "arbitrary"),
                                             vmem_limit_bytes=VMEM_LIMIT),
        name="ffn",
    )(*args)
    return y, tail


def _permute_in_proj(w_in, b_in):
    gates_a = slice(2 * MLSTM_W + 2 * MLSTM_W, 2 * MLSTM_W + 2 * MLSTM_W + 2 * N_MLSTM_HEADS)
    nsa0 = gates_a.stop
    nsa1 = nsa0 + NSA_W + 6 * KV_W

    def perm(a):
        pad = IN_PAD - a.shape[-1]
        parts = [a[..., :gates_a.start], a[..., nsa0:nsa1], a[..., gates_a], a[..., nsa1:]]
        parts.append(jnp.zeros(a.shape[:-1] + (pad,), a.dtype))
        return jnp.concatenate(parts, axis=-1)

    return perm(w_in).astype(BF16), perm(b_in).reshape(1, IN_PAD)


def _front_pad(a, n):
    return jnp.pad(a, ((0, 0), (n, 0), (0, 0)))


def kernel(x_prompt, x_sample, cache_nsa_kv, cache_win_kv, state_mlstm_c, state_mlstm_n, state_mlstm_m,
           state_mlstm_conv, state_ffn_conv, page_table,
           w_in, b_in, mlstm_conv_w, mlstm_conv_b, mlstm_norm_w, nsa_pool_w, rel_bias, w_out,
           ln1_w, ln1_b, w_up, b_up, ffn_conv_w, ffn_conv_b, w_down, b_down, ln2_w, ln2_b):
    bsz, t_len, _ = x_prompt.shape
    dec_b, n_new, _ = x_sample.shape
    assert n_new == N_NEW and cache_win_kv.shape[1] == WINDOW

    w_p, b_p = _permute_in_proj(w_in, b_in)
    pool_w2 = jnp.concatenate([jnp.repeat(nsa_pool_w[0], HEAD_DIM, axis=-1),
                               jnp.repeat(nsa_pool_w[1], HEAD_DIM, axis=-1)], axis=-1)
    row = lambda v: v.reshape(1, -1)
    ffn_w = (w_out[:MLSTM_W].astype(BF16), w_out[MLSTM_W:].astype(BF16), row(ln1_w), row(ln1_b),
             w_up.astype(BF16), row(b_up), ffn_conv_w, row(ffn_conv_b), w_down.astype(BF16), row(b_down),
             row(ln2_w), row(ln2_b))

    zqk, zv, zo, nq, kv, win, zg, kv_t = _proj_in(x_prompt.reshape(bsz * t_len, D_MODEL), w_p, b_p, seq_len=t_len)
    seq3 = lambda a: a.reshape(bsz, t_len, a.shape[-1])
    zqk, zv, zo, nq, kv, win, zg = map(seq3, (zqk, zv, zo, nq, kv, win, zg))
    mix_m, c_p, n_p, m_p = _mlstm(zqk, zv, zo, zg, None, mlstm_conv_w, mlstm_conv_b, mlstm_norm_w, 0)
    mix_n = _nsa_prompt(nq, kv, win, zg, pool_w2, rel_bias)
    y_p, tail_p = _ffn(x_prompt, mix_m, mix_n, None, ffn_w)
    kv_p = jnp.transpose(kv_t.reshape(bsz, 4, N_KV_HEADS, HEAD_DIM, t_len), (0, 4, 1, 2, 3))
    win_p = win[:, t_len - min(WINDOW, t_len):].reshape(bsz, -1, 2, N_KV_HEADS, HEAD_DIM)
    mconv_p = zqk[:, t_len - (MLSTM_CONV - 1):]
    fconv_p = tail_p[:, -1, 8 - (FFN_CONV - 1):]

    pad = SEQ_PAD - N_NEW
    xs = _front_pad(x_sample, pad)
    szqk, szv, szo, snq, skv, swin, szg = _proj_in(xs.reshape(dec_b * SEQ_PAD, D_MODEL), w_p, b_p)
    sseq = lambda a: a.reshape(dec_b, SEQ_PAD, a.shape[-1])
    szqk, szv, szo, snq, skv, swin, szg = map(sseq, (szqk, szv, szo, snq, skv, swin, szg))
    lead = MLSTM_CHUNK - N_NEW
    new = lambda a: a[:, pad:]
    m_qk = jnp.concatenate([jnp.zeros((dec_b, lead - (MLSTM_CONV - 1), 2 * MLSTM_W), F32),
                            state_mlstm_conv.astype(F32), new(szqk)], axis=1)
    mix_ms, c_s, n_s, m_s = _mlstm(m_qk, _front_pad(new(szv), lead), _front_pad(new(szo), lead),
                                   _front_pad(new(szg), lead),
                                   (state_mlstm_c, state_mlstm_n, state_mlstm_m),
                                   mlstm_conv_w, mlstm_conv_b, mlstm_norm_w, lead)
    mix_ms = mix_ms[:, MLSTM_CHUNK - SEQ_PAD:]
    cache3 = jnp.transpose(cache_nsa_kv, (0, 2, 3, 4, 1)).reshape(cache_nsa_kv.shape[0], 4 * KV_W, PAGE_SIZE)
    pool_wt = jnp.tile(jnp.repeat(jnp.transpose(nsa_pool_w, (0, 2, 1)).reshape(2 * N_KV_HEADS, CMP_BLOCK),
                                  HEAD_DIM, axis=0), (1, PAGE_SIZE // CMP_BLOCK))
    o_c, ids_rows = _nsa_cmp(page_table, rel_bias, snq, pool_wt, cache3)
    ids = jnp.transpose(ids_rows[:, :N_PICK, :N_KV_HEADS * SEQ_PAD], (0, 2, 1))
    ids = ids.reshape(dec_b, N_KV_HEADS, SEQ_PAD, N_PICK)[:, :, pad:]
    win_cache = jnp.transpose(cache_win_kv, (0, 2, 3, 4, 1)).reshape(dec_b, 2 * KV_W, WINDOW)
    mix_ns = _nsa_sel(page_table, ids, ids_rows, rel_bias, snq, skv, win_cache, swin, szg, o_c, cache3)
    override = jnp.concatenate([jnp.zeros((dec_b, pad - (FFN_CONV - 1), 2 * D_FF), F32),
                                state_ffn_conv.astype(F32),
                                jnp.zeros((dec_b, N_NEW, 2 * D_FF), F32)], axis=1)
    flat = lambda a: a.reshape(1, dec_b * SEQ_PAD, a.shape[-1])
    y_s8, u_s = _ffn(flat(xs), flat(mix_ms), flat(mix_ns), flat(override), ffn_w)
    y_s = y_s8.reshape(dec_b, SEQ_PAD, D_MODEL)[:, pad:]
    kv_s = new(skv).reshape(dec_b, N_NEW, 4, N_KV_HEADS, HEAD_DIM)
    win_s = jnp.concatenate([cache_win_kv[:, N_NEW:], new(swin).reshape(dec_b, N_NEW, 2, N_KV_HEADS, HEAD_DIM)],
                            axis=1)
    mconv_s = szqk[:, SEQ_PAD - (MLSTM_CONV - 1):]
    fconv_s = u_s.reshape(dec_b, SEQ_PAD, 2 * D_FF)[:, SEQ_PAD - (FFN_CONV - 1):]
    return (y_p, y_s, kv_p, kv_s, win_p, win_s, c_p, c_s, n_p, n_s, m_p, m_s, mconv_p, mconv_s, fconv_p, fconv_s)
```

```python
import functools
import math

import jax
import jax.numpy as jnp
from jax import lax
from jax.experimental import pallas as pl
from jax.experimental.pallas import tpu as pltpu

F32 = jnp.float32
BF16 = jnp.bfloat16
I32 = jnp.int32

D_MODEL = 1024
HEAD_DIM = 64
N_MLSTM_HEADS = 8
N_NSA_HEADS = 8
N_KV_HEADS = 2
GQ = N_NSA_HEADS // N_KV_HEADS
MLSTM_W = N_MLSTM_HEADS * HEAD_DIM
NSA_W = N_NSA_HEADS * HEAD_DIM
KV_W = N_KV_HEADS * HEAD_DIM
MLSTM_CHUNK = 64
MLSTM_CONV = 4
CMP_BLOCK = 32
SEL_BLOCK = 64
SEL_TOPK = 16
WINDOW = 512
NSA_QBLOCK = 64
N_BUCKETS = 32
MAX_DISTANCE = 128
D_FF = 2688
FFN_CONV = 3
DEPTH = 1
ALPHA = (2.0 * DEPTH) ** 0.25
LN_EPS = 1e-5
PAGE_SIZE = 128

LANES = 128
VMEM_LIMIT = 56 * 1024 * 1024
NEG_INF = float("-inf")

SEQ_PAD = 8
N_NEW = 4
N_QROWS = N_NSA_HEADS * SEQ_PAD


def _bucket_thresholds():
    max_exact = N_BUCKETS // 2
    thr = list(range(max_exact + 1))
    for k in range(1, N_BUCKETS - max_exact):
        thr.append(int(math.ceil(max_exact * (MAX_DISTANCE / max_exact) ** (k / (N_BUCKETS - max_exact)))))
    return tuple(thr)


BUCKET_THR = _bucket_thresholds()
FAR_DIST = BUCKET_THR[N_BUCKETS - 1]


def _dot(a, b, precision=None):
    return jnp.dot(a, b, preferred_element_type=F32, precision=precision)


def _dot_nt(a, b, precision=None):
    return lax.dot_general(a, b, (((1,), (1,)), ((), ())), preferred_element_type=F32, precision=precision)


def _dot_tn(a, b, precision=None):
    return lax.dot_general(a, b, (((0,), (0,)), ((), ())), preferred_element_type=F32, precision=precision)


def _layer_norm(x, w, b):
    mu = jnp.mean(x, -1, keepdims=True)
    xc = x - mu
    var = jnp.mean(xc * xc, -1, keepdims=True)
    return xc * lax.rsqrt(var + LN_EPS) * w + b


def _log_sigmoid(x):
    return jnp.minimum(x, 0.0) - jnp.log1p(jnp.exp(-jnp.abs(x)))


def _bias_from_dist(dist, rb_ref, head):
    acc = jnp.full(dist.shape, rb_ref[0, head], F32)
    for k in range(1, N_BUCKETS):
        acc = jnp.where(dist >= BUCKET_THR[k], rb_ref[k, head], acc)
    return acc


def _masked_softmax(s):
    mx = jnp.max(s, -1, keepdims=True)
    mx = jnp.where(mx > NEG_INF, mx, 0.0)
    e = jnp.exp(s - mx)
    den = jnp.sum(e, -1, keepdims=True)
    return e / jnp.maximum(den, 1e-30)


_C_QK, _C_V, _C_O, _C_NQ, _C_KV, _C_WIN, _C_G = 0, 1024, 1536, 2048, 2560, 3072, 3328
IN_PAD = 3456
G_IGATE, G_FGATE, G_NSA = 0, N_MLSTM_HEADS, 2 * N_MLSTM_HEADS


def _proj_in_body(*refs, kv_transposed):
    if kv_transposed:
        x_ref, w_ref, b_ref, wkvt_ref, bkv_ref, qk_ref, v_ref, o_ref, nq_ref, kv_ref, win_ref, g_ref, kvt_ref = refs
    else:
        x_ref, w_ref, b_ref, qk_ref, v_ref, o_ref, nq_ref, kv_ref, win_ref, g_ref = refs
    xb = x_ref[...].astype(BF16)
    z = _dot(xb, w_ref[...]) + b_ref[...]
    qk_ref[...] = z[:, _C_QK:_C_V]
    v_ref[...] = z[:, _C_V:_C_O]
    o_ref[...] = z[:, _C_O:_C_NQ]
    nq_ref[...] = z[:, _C_NQ:_C_KV]
    kv_ref[...] = z[:, _C_KV:_C_WIN]
    win_ref[...] = z[:, _C_WIN:_C_G]
    g_ref[...] = z[:, _C_G:IN_PAD]
    if kv_transposed:
        kvt_ref[0] = _dot_nt(wkvt_ref[...], xb) + bkv_ref[...]


def _proj_in(x2d, w_p, b_p, seq_len=None):
    rows = x2d.shape[0]
    tm = min(rows, 512)
    widths = (1024, 512, 512, 512, 512, 256, 128)
    const = lambda i: (0, 0)
    const_spec = lambda shape: pl.BlockSpec(shape, const, pipeline_mode=pl.Buffered(1))
    in_specs = [pl.BlockSpec((tm, D_MODEL), lambda i: (i, 0)), const_spec((D_MODEL, IN_PAD)), const_spec((1, IN_PAD))]
    args = [x2d, w_p, b_p]
    out_specs = [pl.BlockSpec((tm, w), lambda i: (i, 0)) for w in widths]
    out_shape = [jax.ShapeDtypeStruct((rows, w), F32) for w in widths]
    if seq_len is not None:
        tiles = seq_len // tm
        in_specs += [const_spec((4 * KV_W, D_MODEL)), const_spec((4 * KV_W, 1))]
        args += [jnp.transpose(w_p[:, _C_KV:_C_WIN]), b_p[0, _C_KV:_C_WIN].reshape(4 * KV_W, 1)]
        out_specs.append(pl.BlockSpec((1, 4 * KV_W, tm), lambda i: (i // tiles, 0, i % tiles)))
        out_shape.append(jax.ShapeDtypeStruct((rows // seq_len, 4 * KV_W, seq_len), F32))
    return pl.pallas_call(
        functools.partial(_proj_in_body, kv_transposed=seq_len is not None),
        grid=(rows // tm,),
        in_specs=in_specs,
        out_specs=out_specs,
        out_shape=out_shape,
        compiler_params=pltpu.CompilerParams(dimension_semantics=("arbitrary",),
                                             vmem_limit_bytes=VMEM_LIMIT),
        name="proj_in",
    )(*args)


N_PAIRS = N_MLSTM_HEADS // 2
GATE_LANE0 = N_MLSTM_HEADS
MLSTM_SEQS_PER_STEP = 4


def _dot_split(x, onehot_bf16, pieces):
    acc = None
    rem = x
    for n in range(pieces):
        part = rem.astype(BF16)
        d = _dot(part, onehot_bf16)
        acc = d if acc is None else acc + d
        if n + 1 < pieces:
            rem = rem - part.astype(F32)
    return acc


def _mlstm_body(*refs, tb, nbg, n_dummy, has_state):
    if has_state:
        (zqk_ref, zv_ref, zo_ref, zg_ref, c0_ref, n0_ref, m0_ref, cw_ref, cb_ref, nw_ref,
         mix_ref, c_out, n_out, m_out, ubuf, qk_sc, ct_sc, n_sc, m_sc) = refs
    else:
        (zqk_ref, zv_ref, zo_ref, zg_ref, cw_ref, cb_ref, nw_ref,
         mix_ref, c_out, n_out, m_out, ubuf, qk_sc, ct_sc, n_sc, m_sc) = refs
    L = MLSTM_CHUNK
    D = HEAD_DIM
    j = pl.program_id(1)
    hi = lax.Precision.HIGHEST
    head_lanes = slice(GATE_LANE0, GATE_LANE0 + N_MLSTM_HEADS)
    pairs = [(bi, p) for bi in range(nbg) for p in range(N_PAIRS)]

    @pl.when(j == 0)
    def _init():
        ubuf[:, 0:8, :] = jnp.zeros((nbg, 8, 2 * MLSTM_W), F32)
        ct_sc[...] = jnp.zeros(ct_sc.shape, F32)
        n_sc[...] = jnp.zeros(n_sc.shape, F32)
        m_sc[...] = jnp.zeros(m_sc.shape, F32)
        if has_state:
            for bi in range(nbg):
                for h in range(N_MLSTM_HEADS):
                    p, hh = divmod(h, 2)
                    hs = slice(hh * D, (hh + 1) * D)
                    ct_sc[bi, p, hs, hs] = c0_ref[bi, h]
                    n_sc[bi, p, :, hs] = n0_ref[bi, h:h + 1, :]
                m_sc[bi, :, head_lanes] = m0_ref[bi]

    for bi in range(nbg):
        u = zqk_ref[bi]
        ubuf[bi, 8:8 + tb, :] = u
        conv = cb_ref[...] + ubuf[bi, 5:5 + tb, :] * cw_ref[0:1, :]
        conv = conv + ubuf[bi, 6:6 + tb, :] * cw_ref[1:2, :]
        conv = conv + ubuf[bi, 7:7 + tb, :] * cw_ref[2:3, :]
        conv = conv + u * cw_ref[3:4, :]
        ubuf[bi, 5:8, :] = u[tb - 3:tb, :]
        qk_sc[bi] = conv * jax.nn.sigmoid(conv)

    tri = (lax.broadcasted_iota(I32, (L, L), 1) <= lax.broadcasted_iota(I32, (L, L), 0)).astype(F32)
    row_l = lax.broadcasted_iota(I32, (L, LANES), 0)
    lane = lax.broadcasted_iota(I32, (L, LANES), 1)
    low = lane < D
    key_tok = lane % D
    causal_pair = key_tok <= row_l
    eye_pair = key_tok == row_l
    spread = (lax.broadcasted_iota(I32, (LANES, MLSTM_W), 0) - GATE_LANE0
              == lax.broadcasted_iota(I32, (LANES, MLSTM_W), 1) // D).astype(BF16)
    same_head = (lax.broadcasted_iota(I32, (LANES, LANES), 0) // D
                 == lax.broadcasted_iota(I32, (LANES, LANES), 1) // D)
    ones_bd = same_head.astype(BF16)
    slab_rows = 3 * L + 8

    def chunk(ci, carry):
        r0 = pl.multiple_of(ci * L, L)
        rows = pl.ds(r0, L)
        gates = []
        for bi in range(nbg):
            g = zg_ref[bi, rows, :]
            if n_dummy:
                dummy = (row_l + ci * L) < n_dummy
                g = jnp.where(dummy, jnp.where(lane < G_FGATE, -1e30, 1e30), g)
            gates.append(g)
        b_all = _dot(tri, jnp.concatenate([_log_sigmoid(g) for g in gates], axis=1), hi)
        stacks = []
        for bi in range(nbg):
            b = b_all[:, bi * LANES:(bi + 1) * LANES]
            r = pltpu.roll(gates[bi], GATE_LANE0 - G_IGATE, 1) - b
            cm = r
            for sh in (1, 2, 4, 8, 16, 32):
                cm = jnp.where(row_l >= sh, jnp.maximum(cm, pltpu.roll(cm, sh, 0)), cm)
            m_st = m_sc[bi]
            mx = jnp.maximum(m_st, cm)
            mx_end = mx[L - 1:L, :]
            m_sc[bi] = b[L - 1:L, :] + mx_end
            stacks += [mx, r, b + mx, m_st, mx_end, jnp.zeros((6, LANES), F32)]
        x_all = _dot_split(jnp.concatenate(stacks, axis=0), spread, 2)

        work = []
        den_terms = []
        for bi, p in pairs:
            cs = slice(p * LANES, (p + 1) * LANES)
            x0 = bi * slab_rows
            mx_p, rs_p, mt_p = x_all[x0:x0 + L, cs], x_all[x0 + L:x0 + 2 * L, cs], x_all[x0 + 2 * L:x0 + 3 * L, cs]
            mst_p, mxe_p = x_all[x0 + 3 * L:x0 + 3 * L + 1, cs], x_all[x0 + 3 * L + 1:x0 + 3 * L + 2, cs]
            wi_p = jnp.exp(mst_p - mx_p)
            fl_p = jnp.exp(-mt_p)
            wk_p = jnp.exp(rs_p - mxe_p)
            sc_p = jnp.exp(mst_p - mxe_p)
            r_row = jnp.sum(jnp.where(eye_pair, rs_p, 0.0), axis=0, keepdims=True)
            w_intra = jnp.exp(jnp.where(causal_pair, r_row - mx_p, NEG_INF))
            q_p = qk_sc[bi, rows, cs]
            k_p = qk_sc[bi, rows, MLSTM_W + p * LANES:MLSTM_W + (p + 1) * LANES] * (D ** -0.5)
            v_p = zv_ref[bi, rows, cs]
            qb = q_p.astype(BF16)
            k_bd = jnp.concatenate([jnp.where(low, k_p, 0.0), jnp.where(low, 0.0, k_p)], axis=0)
            v_bd = jnp.concatenate([jnp.where(low, v_p, 0.0), jnp.where(low, 0.0, v_p)], axis=0).astype(BF16)
            c_t = ct_sc[bi, p]
            both = _dot_nt(qb, jnp.concatenate([k_bd, c_t], axis=0).astype(BF16))
            a = w_intra * both[:, 0:LANES]
            num = _dot(a.astype(BF16), v_bd) + both[:, LANES:2 * LANES] * wi_p
            n_row = n_sc[bi, p]
            den_terms.append(a + wi_p * (q_p * n_row))
            work.append((num, fl_p))
            kw = k_p * wk_p
            upd = _dot_tn(v_p.astype(BF16), kw.astype(BF16))
            ct_sc[bi, p] = sc_p * c_t + jnp.where(same_head, upd, 0.0)
            n_sc[bi, p] = sc_p * n_row + jnp.sum(kw, axis=0, keepdims=True)
        den_all = _dot_split(jnp.concatenate(den_terms, axis=0), ones_bd, 2)

        hhs = []
        for n, (num, fl_p) in enumerate(work):
            hhs.append(num / jnp.maximum(jnp.abs(den_all[n * L:(n + 1) * L, :]), fl_p))
        mu_all = _dot_split(jnp.concatenate(hhs, axis=0), ones_bd, 1) * (1.0 / D)
        hcs = [hh - mu_all[n * L:(n + 1) * L, :] for n, hh in enumerate(hhs)]
        var_all = _dot_split(jnp.concatenate([hc * hc for hc in hcs], axis=0), ones_bd, 1) * (1.0 / D)
        for n, (bi, p) in enumerate(pairs):
            cs = slice(p * LANES, (p + 1) * LANES)
            hn = hcs[n] * lax.rsqrt(var_all[n * L:(n + 1) * L, :] + LN_EPS)
            og = jax.nn.sigmoid(zo_ref[bi, rows, cs])
            mix_ref[bi, rows, cs] = og * (hn * nw_ref[:, cs])
        return carry

    lax.fori_loop(0, tb // L, chunk, 0, unroll=2 if (tb // L) % 2 == 0 else 1)

    @pl.when(j == pl.num_programs(1) - 1)
    def _final():
        for bi in range(nbg):
            for h in range(N_MLSTM_HEADS):
                p, hh = divmod(h, 2)
                hs = slice(hh * D, (hh + 1) * D)
                c_out[bi, h] = ct_sc[bi, p, hs, hs]
                n_out[bi, h:h + 1, :] = n_sc[bi, p, :, hs]
            m_out[bi] = m_sc[bi, :, head_lanes]


def _mlstm(zqk, zv, zo, zg, state, conv_w, conv_b, norm_w, n_dummy):
    bsz, t_len, _ = zqk.shape
    tb = min(t_len, 256)
    nbg = MLSTM_SEQS_PER_STEP
    assert bsz % nbg == 0
    has_state = state is not None
    row = lambda w: pl.BlockSpec((nbg, tb, w), lambda b, j: (b, j, 0))
    per_b = lambda shape: pl.BlockSpec((nbg,) + shape, lambda b, j: (b,) + (0,) * len(shape))
    const = lambda shape: pl.BlockSpec(shape, lambda b, j: (0,) * len(shape))
    in_specs = [row(2 * MLSTM_W), row(MLSTM_W), row(MLSTM_W), row(LANES)]
    args = [zqk, zv, zo, zg]
    if has_state:
        c0, n0, m0 = state
        in_specs += [per_b((N_MLSTM_HEADS, HEAD_DIM, HEAD_DIM)), per_b((N_MLSTM_HEADS, HEAD_DIM)),
                     per_b((1, N_MLSTM_HEADS))]
        args += [c0, n0, m0.reshape(bsz, 1, N_MLSTM_HEADS)]
    in_specs += [const((MLSTM_CONV, 2 * MLSTM_W)), const((1, 2 * MLSTM_W)), const((1, MLSTM_W))]
    args += [conv_w, conv_b.reshape(1, -1), norm_w.reshape(1, -1)]
    mix, c_new, n_new, m_new = pl.pallas_call(
        functools.partial(_mlstm_body, tb=tb, nbg=nbg, n_dummy=n_dummy, has_state=has_state),
        grid=(bsz // nbg, t_len // tb),
        in_specs=in_specs,
        out_specs=[row(MLSTM_W), per_b((N_MLSTM_HEADS, HEAD_DIM, HEAD_DIM)),
                   per_b((N_MLSTM_HEADS, HEAD_DIM)), per_b((1, N_MLSTM_HEADS))],
        out_shape=[jax.ShapeDtypeStruct((bsz, t_len, MLSTM_W), F32),
                   jax.ShapeDtypeStruct((bsz, N_MLSTM_HEADS, HEAD_DIM, HEAD_DIM), F32),
                   jax.ShapeDtypeStruct((bsz, N_MLSTM_HEADS, HEAD_DIM), F32),
                   jax.ShapeDtypeStruct((bsz, 1, N_MLSTM_HEADS), F32)],
        scratch_shapes=[pltpu.VMEM((nbg, 8 + tb, 2 * MLSTM_W), F32),
                        pltpu.VMEM((nbg, tb, 2 * MLSTM_W), F32),
                        pltpu.VMEM((nbg, N_PAIRS, LANES, LANES), F32),
                        pltpu.VMEM((nbg, N_PAIRS, 1, LANES), F32),
                        pltpu.VMEM((nbg, 1, LANES), F32)],
        compiler_params=pltpu.CompilerParams(dimension_semantics=("arbitrary", "arbitrary"),
                                             vmem_limit_bytes=VMEM_LIMIT),
        name="mlstm",
    )(*args)
    return mix, c_new, n_new, m_new.reshape(bsz, N_MLSTM_HEADS)


def _block_diag_queries(x, n_rows, heads=tuple(range(N_NSA_HEADS)), scale=HEAD_DIM ** -0.5):
    lane = lax.broadcasted_iota(I32, (n_rows, LANES), 1)
    low = lane < HEAD_DIM
    pieces = []
    for h in heads:
        pair = x[:, (h // 2) * LANES:(h // 2 + 1) * LANES]
        src_low = h % 2 == 0
        dst_low = h // GQ == 0
        if src_low != dst_low:
            pair = pltpu.roll(pair, HEAD_DIM, 1)
        pieces.append(jnp.where(low if dst_low else ~low, pair, 0.0))
    return (jnp.concatenate(pieces, axis=0) * scale).astype(BF16)


def _group_lanes(o, g):
    return o[:, g * HEAD_DIM:(g + 1) * HEAD_DIM]


def _gated_mix(gates, o_c, o_s, o_w, h, n_rows):
    g = h // GQ
    rs = slice(h * n_rows, (h + 1) * n_rows)
    col = lambda branch: gates[:, G_NSA + branch * N_NSA_HEADS + h:G_NSA + branch * N_NSA_HEADS + h + 1]
    return (col(0) * _group_lanes(o_c[rs], g) + col(1) * _group_lanes(o_s[rs], g)
            + col(2) * _group_lanes(o_w[rs], g))


SEL_TILE = 256
SEL_TILE_BLOCKS = SEL_TILE // SEL_BLOCK
FAR_GROUP = 2
Q_PER_STEP = 4
WIN_BAND = WINDOW + NSA_QBLOCK
N_COLS = N_NSA_HEADS * NSA_QBLOCK
PROMPT_HEADS = tuple(GQ * (jj % N_KV_HEADS) + jj // N_KV_HEADS for jj in range(N_NSA_HEADS))
SEL_REL0 = 2 * SEL_TILE - SEL_BLOCK
SEL_BIAS_ROWS = SEL_REL0 + SEL_TILE


LOG2E = math.log2(math.e)


def _softmax2_rows(s):
    mx = jnp.max(s, 0, keepdims=True)
    mx = jnp.where(mx > NEG_INF, mx, 0.0)
    e = jnp.exp2(s - mx)
    return e / jnp.maximum(jnp.sum(e, 0, keepdims=True), 1e-30)


def _nsa_prompt_body(rb_ref, nq_ref, nqn_ref, kv_ref, win_ref, zg_ref, pw_ref, out_ref,
                     ks_sc, vst_sc, wk_sc, wv_sc, kc_sc, pv_sc, vct_sc, bs_sc, bw_sc, fc_sc,
                     m_sc, l_sc, acc_sc, ch_sc, oc_sc, fs_sc, ft_sc, *, t_len):
    QB = NSA_QBLOCK
    n_cmp = t_len // CMP_BLOCK
    n_sel = t_len // SEL_BLOCK
    pid = pl.program_id(1)
    hi = lax.Precision.HIGHEST

    def cmp_scores(x_q):
        return _dot_nt(kc_sc[...], _block_diag_queries(x_q, QB, PROMPT_HEADS, HEAD_DIM ** -0.5 * LOG2E))

    def cmp_and_select(raw, qi, out_buf):
        s_c = raw + pltpu.roll(fc_sc[...], 2 * qi, 0)
        c_i = lax.broadcasted_iota(I32, (n_cmp, N_COLS), 0)
        q_i = lax.broadcasted_iota(I32, (n_cmp, N_COLS), 1) % QB
        p_c = _softmax2_rows(jnp.where(c_i * CMP_BLOCK + (CMP_BLOCK - 1) <= qi * QB + q_i, s_c, NEG_INF))
        oc_sc[out_buf] = _dot(vct_sc[...], p_c.astype(BF16))
        p_sum = p_c[:, 0:LANES]
        for j in range(1, GQ):
            p_sum = p_sum + p_c[:, j * LANES:(j + 1) * LANES]
        ratio = SEL_BLOCK // CMP_BLOCK
        pair = (lax.broadcasted_iota(I32, (n_sel, n_cmp), 1) // ratio
                == lax.broadcasted_iota(I32, (n_sel, n_cmp), 0)).astype(F32)
        imp = _dot(pair, p_sum, hi)
        blk = lax.broadcasted_iota(I32, (n_sel, LANES), 0)
        score = jnp.where(blk == qi, jnp.inf, jnp.where(blk < qi, imp, NEG_INF))

        def pick(_, carry):
            work, chosen = carry
            best = jnp.max(work, 0, keepdims=True)
            first = jnp.min(jnp.where(work == best, blk, n_sel), 0, keepdims=True)
            hit = blk == first
            return jnp.where(hit, NEG_INF, work), jnp.where(hit, 1.0, chosen)

        _, chosen = lax.fori_loop(0, SEL_TOPK, pick, (score, jnp.zeros((n_sel, LANES), F32)), unroll=True)
        ch_sc[out_buf, 0:8, :] = jnp.full((8, LANES), NEG_INF, F32)
        ch_sc[out_buf, 8:8 + n_sel, :] = jnp.where((score > NEG_INF) & (chosen > 0.5), 0.0, NEG_INF)

    @pl.when(pid == 0)
    def _init():
        ks_sc[0:SEL_TILE, :] = jnp.zeros((SEL_TILE, LANES), BF16)
        vst_sc[0] = jnp.zeros((LANES, SEL_TILE), BF16)
        wk_sc[0:WINDOW, :] = jnp.zeros((WINDOW, LANES), BF16)
        wv_sc[0:WINDOW, :] = jnp.zeros((WINDOW, LANES), BF16)
        rows_per = 2 * SEL_TILE
        blocks_per = rows_per // CMP_BLOCK

        def fill(c, carry):
            r0 = pl.multiple_of(c * rows_per, rows_per)
            kv = kv_ref[0, pl.ds(r0, rows_per), :]
            ks_sc[pl.ds(SEL_TILE + r0, rows_per), :] = kv[:, 2 * LANES:3 * LANES].astype(BF16)
            for half in range(2):
                v_t = kv[half * SEL_TILE:(half + 1) * SEL_TILE, 3 * LANES:4 * LANES]
                vst_sc[2 * c + half + 1] = jnp.transpose(v_t).astype(BF16)
            w = win_ref[0, pl.ds(r0, rows_per), :]
            wk_sc[pl.ds(WINDOW + r0, rows_per), :] = w[:, 0:LANES].astype(BF16)
            wv_sc[pl.ds(WINDOW + r0, rows_per), :] = w[:, LANES:2 * LANES].astype(BF16)
            pooled = jnp.sum(kv[:, 0:2 * LANES].reshape(blocks_per, CMP_BLOCK, 2 * LANES) * pw_ref[...][None],
                             axis=1)
            c0 = pl.multiple_of(c * blocks_per, blocks_per)
            kc_sc[pl.ds(c0, blocks_per), :] = pooled[:, 0:LANES].astype(BF16)
            pv_sc[pl.ds(c0, blocks_per), :] = pooled[:, LANES:2 * LANES]
            return carry

        lax.fori_loop(0, t_len // rows_per, fill, 0)
        vct_sc[...] = jnp.transpose(pv_sc[...]).astype(BF16)

        for jj, h in enumerate(PROMPT_HEADS):
            cs = slice(jj * QB, (jj + 1) * QB)
            u_i = lax.broadcasted_iota(I32, (SEL_BIAS_ROWS, QB), 0)
            q_i = lax.broadcasted_iota(I32, (SEL_BIAS_ROWS, QB), 1)
            dist = q_i - (u_i - SEL_REL0)
            near = (_bias_from_dist(dist, rb_ref, h) - rb_ref[N_BUCKETS - 1, h]) * LOG2E
            bs_sc[:, cs] = jnp.where(dist >= 0, near, NEG_INF)
            s_i = lax.broadcasted_iota(I32, (WIN_BAND, QB), 0)
            q_i = lax.broadcasted_iota(I32, (WIN_BAND, QB), 1)
            dist = q_i + WINDOW - s_i
            bw_sc[:, cs] = jnp.where((dist >= 0) & (dist <= WINDOW), _bias_from_dist(dist, rb_ref, h) * LOG2E,
                                     NEG_INF)
            c_i = lax.broadcasted_iota(I32, (n_cmp, QB), 0)
            q_i = lax.broadcasted_iota(I32, (n_cmp, QB), 1)
            rel = jnp.where(c_i < n_cmp // 2, c_i, c_i - n_cmp)
            dist = q_i - CMP_BLOCK * rel - (CMP_BLOCK - 1)
            dist = jnp.where((c_i >= 2) & (c_i < n_cmp // 2), FAR_DIST, dist)
            fc_sc[:, cs] = _bias_from_dist(dist, rb_ref, h) * LOG2E

        cmp_and_select(cmp_scores(nq_ref[0, 0:QB, :]), 0, 0)

    def q_block(t):
        i = Q_PER_STEP * pid + t
        rows = slice(t * QB, (t + 1) * QB)
        x_q = nq_ref[0, rows, :]
        x_next = nq_ref[0, (t + 1) * QB:(t + 2) * QB, :] if t + 1 < Q_PER_STEP else nqn_ref[0, 0:QB, :]
        buf = t % 2
        o_c = oc_sc[buf]
        qbd = _block_diag_queries(x_q, QB, PROMPT_HEADS, HEAD_DIM ** -0.5 * LOG2E)
        last = i // SEL_TILE_BLOCKS

        def tile_raw(slot):
            return _dot_nt(ks_sc[pl.ds(pl.multiple_of(slot * SEL_TILE, SEL_TILE), SEL_TILE), :], qbd)

        raw_next = cmp_scores(x_next)
        w0 = pl.multiple_of(i * QB, QB)
        raw_w = _dot_nt(wk_sc[pl.ds(w0, WIN_BAND), :], qbd)
        raw_near = [tile_raw(last + 1), tile_raw(last)]

        def tile_scores(slot, bias, s=None):
            if s is None:
                s = tile_raw(slot)
            if bias is not None:
                s = s + bias
            parts = []
            for r in range(SEL_TILE_BLOCKS):
                row = ch_sc[buf, pl.ds(8 - SEL_TILE_BLOCKS + slot * SEL_TILE_BLOCKS + r, 1), :]
                parts.append(s[r * SEL_BLOCK:(r + 1) * SEL_BLOCK, :] + jnp.concatenate([row] * GQ, axis=1))
            return jnp.concatenate(parts, axis=0)

        def group_max(scores):
            return functools.reduce(jnp.maximum, [jnp.max(s, 0, keepdims=True) for s in scores])

        def sel_group(slots, scores, t_max, first):
            m_new = t_max if first else jnp.maximum(m_sc[...], t_max)
            exps = [jnp.exp2(s - m_new) for s in scores]
            l_new = functools.reduce(jnp.add, [jnp.sum(e, 0, keepdims=True) for e in exps])
            acc_new = functools.reduce(jnp.add, [_dot(vst_sc[slot], e.astype(BF16)) for slot, e in zip(slots, exps)])
            if first:
                l_sc[...] = l_new
                acc_sc[...] = acc_new
            else:
                scale = jnp.exp2(m_sc[...] - m_new)
                l_sc[...] = scale * l_sc[...] + l_new
                acc_sc[...] = scale * acc_sc[...] + acc_new
            m_sc[...] = m_new

        def far_slots(p):
            return [jnp.where(FAR_GROUP * p + 1 + n < last, FAR_GROUP * p + 1 + n, 0) for n in range(FAR_GROUP)]

        def far_scores(p):
            scores = [tile_scores(slot, None) for slot in far_slots(p)]
            for n, s in enumerate(scores):
                fs_sc[p % 2, n] = s
            ft_sc[p % 2] = group_max(scores)

        far_scores(0)
        cmp_and_select(raw_next, i + 1, 1 - buf)

        in_seq = lax.broadcasted_iota(I32, (WIN_BAND, N_COLS), 0) >= WINDOW - i * QB
        s_w = jnp.where(in_seq, raw_w + bw_sc[...], NEG_INF)
        e_w = jnp.exp2(s_w - jnp.max(s_w, 0, keepdims=True))
        o_w = (_dot_tn(wv_sc[pl.ds(w0, WIN_BAND), :], e_w.astype(BF16))
               / jnp.maximum(jnp.sum(e_w, 0, keepdims=True), 1e-30))

        u0 = pl.multiple_of(SEL_REL0 - SEL_TILE - SEL_BLOCK * (i % SEL_TILE_BLOCKS), SEL_BLOCK)
        near = [tile_scores(last + 1, bs_sc[pl.ds(u0 + SEL_TILE, SEL_TILE), :], raw_near[0]),
                tile_scores(last, bs_sc[pl.ds(u0, SEL_TILE), :], raw_near[1])]
        sel_group((last + 1, last), near, group_max(near), True)

        def far_group(p, carry):
            sel_group(far_slots(p), [fs_sc[p % 2, n] for n in range(FAR_GROUP)], ft_sc[p % 2], False)
            far_scores(p + 1)
            return carry

        lax.fori_loop(0, (last - 1 + FAR_GROUP - 1) // FAR_GROUP, far_group, 0)
        o_s = acc_sc[...] / jnp.maximum(l_sc[...], 1e-30)

        eye = (lax.broadcasted_iota(I32, (3 * N_NSA_HEADS + 8, LANES), 0) + G_NSA
               == lax.broadcasted_iota(I32, (3 * N_NSA_HEADS + 8, LANES), 1)).astype(F32)
        gates = jax.nn.sigmoid(_dot_nt(eye, zg_ref[0, rows, :], hi))
        pieces = []
        for h in range(N_NSA_HEADS):
            g = h // GQ
            jj = PROMPT_HEADS.index(h)
            part = lambda o: o[g * HEAD_DIM:(g + 1) * HEAD_DIM, jj * QB:(jj + 1) * QB]
            pieces.append(gates[h:h + 1, :] * part(o_c)
                          + gates[N_NSA_HEADS + h:N_NSA_HEADS + h + 1, :] * part(o_s)
                          + gates[2 * N_NSA_HEADS + h:2 * N_NSA_HEADS + h + 1, :] * part(o_w))
        mix_t = jnp.concatenate(pieces, axis=0)
        mix_t = jnp.concatenate([mix_t, jnp.zeros((NSA_W, LANES - QB), F32)], axis=1)
        out_ref[0, rows, :] = jnp.transpose(mix_t)[0:QB, :]

    for t in range(Q_PER_STEP):
        q_block(t)


def _nsa_prompt(nq, kv, win, zg, pool_w2, rel_bias):
    bsz, t_len, _ = nq.shape
    assert t_len // CMP_BLOCK == LANES and t_len % (2 * SEL_TILE) == 0
    n_cmp = t_len // CMP_BLOCK
    per_b = lambda w: pl.BlockSpec((1, t_len, w), lambda b, i: (b, 0, 0))
    step_rows = Q_PER_STEP * NSA_QBLOCK
    blk = lambda w: pl.BlockSpec((1, step_rows, w), lambda b, i: (b, i, 0))
    n_qb = t_len // step_rows
    next_blk = pl.BlockSpec((1, step_rows, NSA_W), lambda b, i: (b, jnp.minimum(i + 1, n_qb - 1), 0))
    return pl.pallas_call(
        functools.partial(_nsa_prompt_body, t_len=t_len),
        grid=(bsz, n_qb),
        in_specs=[pl.BlockSpec(memory_space=pltpu.SMEM),
                  blk(NSA_W), next_blk, per_b(4 * KV_W), per_b(2 * KV_W), blk(LANES),
                  pl.BlockSpec((CMP_BLOCK, 2 * LANES), lambda b, i: (0, 0))],
        out_specs=blk(NSA_W),
        out_shape=jax.ShapeDtypeStruct((bsz, t_len, NSA_W), F32),
        scratch_shapes=[pltpu.VMEM((SEL_TILE + t_len, LANES), BF16),
                        pltpu.VMEM((1 + t_len // SEL_TILE, LANES, SEL_TILE), BF16),
                        pltpu.VMEM((WINDOW + t_len, LANES), BF16),
                        pltpu.VMEM((WINDOW + t_len, LANES), BF16),
                        pltpu.VMEM((n_cmp, LANES), BF16),
                        pltpu.VMEM((n_cmp, LANES), F32),
                        pltpu.VMEM((LANES, n_cmp), BF16),
                        pltpu.VMEM((SEL_BIAS_ROWS, N_COLS), F32),
                        pltpu.VMEM((WIN_BAND, N_COLS), F32),
                        pltpu.VMEM((n_cmp, N_COLS), F32),
                        pltpu.VMEM((1, N_COLS), F32),
                        pltpu.VMEM((1, N_COLS), F32),
                        pltpu.VMEM((LANES, N_COLS), F32),
                        pltpu.VMEM((2, 8 + t_len // SEL_BLOCK, LANES), F32),
                        pltpu.VMEM((2, LANES, N_COLS), F32),
                        pltpu.VMEM((2, FAR_GROUP, SEL_TILE, N_COLS), F32),
                        pltpu.VMEM((2, 1, N_COLS), F32)],
        compiler_params=pltpu.CompilerParams(dimension_semantics=("arbitrary", "arbitrary"),
                                             vmem_limit_bytes=VMEM_LIMIT),
        name="nsa_prompt",
    )(rel_bias, nq, nq, kv, win, zg, pool_w2)


CMP_PAGES = 32
CMP_SLOTS = 4
N_PICK = SEL_TOPK - 1
N_OWNERS = N_KV_HEADS * N_NEW
N_FETCH = N_OWNERS * N_PICK
OWNER_KEYS = N_PICK * PAGE_SIZE


def _sample_row_ids():
    r = lax.broadcasted_iota(I32, (N_QROWS, 1), 0)
    return r % SEQ_PAD - (SEQ_PAD - N_NEW), r // SEQ_PAD


def _sample_bias(dist, rb_ref):
    _, head = _sample_row_ids()
    col = lambda k: sum(jnp.where(head == h, rb_ref[k, h], 0.0) for h in range(N_NSA_HEADS))
    acc = jnp.broadcast_to(col(0), dist.shape)
    for k in range(1, N_BUCKETS):
        acc = jnp.where(dist >= BUCKET_THR[k], col(k), acc)
    return acc


def _sample_bias_table(rb_ref):
    assert FAR_DIST < LANES
    return _sample_bias(lax.broadcasted_iota(I32, (N_QROWS, LANES), 1), rb_ref)


def _sample_bias_lookup(dist, table):
    idx = jnp.clip(dist, 0, LANES - 1)
    return jnp.concatenate([jnp.take_along_axis(table, idx[:, c * LANES:(c + 1) * LANES], axis=1)
                            for c in range(dist.shape[1] // LANES)], axis=1)


def _softmax_two(s1, s2):
    mx = jnp.maximum(jnp.max(s1, -1, keepdims=True), jnp.max(s2, -1, keepdims=True))
    mx = jnp.where(mx > NEG_INF, mx, 0.0)
    e1 = jnp.exp(s1 - mx)
    e2 = jnp.exp(s2 - mx)
    den = jnp.maximum(jnp.sum(e1, -1, keepdims=True) + jnp.sum(e2, -1, keepdims=True), 1e-30)
    return e1 / den, e2 / den


def _nsa_cmp_body(pt_ref, rb_ref, nq_ref, pwt_ref, cache_ref, oc_ref, idx_ref,
                  buf, sem, seg_sc, kct_sc, vct_sc, *, n_pages, past_len):
    b = pl.program_id(0)
    nb = pl.num_programs(0)
    n_chunks = n_pages // CMP_PAGES
    chunk_rows = CMP_PAGES * PAGE_SIZE
    chunk_blocks = chunk_rows // CMP_BLOCK
    n_cmp = past_len // CMP_BLOCK
    n_sel_past = past_len // SEL_BLOCK
    n_prob = N_KV_HEADS * SEQ_PAD
    total = nb * n_chunks
    hi = lax.Precision.HIGHEST

    def page_copy(gi, p):
        page = pt_ref[(gi // n_chunks) * n_pages + (gi % n_chunks) * CMP_PAGES + p]
        slot = gi % CMP_SLOTS
        return pltpu.make_async_copy(cache_ref.at[page, pl.ds(0, 2 * LANES), :], buf.at[slot, p], sem.at[slot])

    def start_chunk(gi):
        for p in range(CMP_PAGES):
            page_copy(gi, p).start()

    def wait_chunk(gi):
        for p in range(CMP_PAGES):
            page_copy(gi, p).wait()

    @pl.when(b == 0)
    def _prologue():
        seg_sc[...] = (lax.broadcasted_iota(I32, (chunk_rows, chunk_blocks), 0) // CMP_BLOCK
                       == lax.broadcasted_iota(I32, (chunk_rows, chunk_blocks), 1)).astype(BF16)
        for gi in range(min(CMP_SLOTS - 1, total)):
            start_chunk(gi)

    def chunk_step(c, carry):
        gi = b * n_chunks + c
        ahead = gi + (CMP_SLOTS - 1)

        @pl.when(ahead < total)
        def _prefetch():
            start_chunk(ahead)

        wait_chunk(gi)
        slot = gi % CMP_SLOTS
        y = jnp.concatenate([(buf[slot, p] * pwt_ref[...]).astype(BF16) for p in range(CMP_PAGES)], axis=1)
        pooled = _dot(y, seg_sc[...])
        kct_sc[c] = pooled[0:LANES].astype(BF16)
        vct_sc[c] = pooled[LANES:2 * LANES].astype(BF16)
        return carry

    lax.fori_loop(0, n_chunks, chunk_step, 0)

    qbd = _block_diag_queries(nq_ref[0], SEQ_PAD)
    tok, _ = _sample_row_ids()
    s_c = jnp.concatenate([_dot(qbd, kct_sc[c]) for c in range(n_chunks)], axis=1)
    near0 = n_cmp - LANES
    assert past_len - (SEQ_PAD - N_NEW) - (CMP_BLOCK * (near0 - 1) + CMP_BLOCK - 1) >= FAR_DIST
    dist = past_len + tok - ((lax.broadcasted_iota(I32, (N_QROWS, LANES), 1) + near0) * CMP_BLOCK + CMP_BLOCK - 1)
    near = jnp.where(dist >= 0, s_c[:, near0:] + _sample_bias(dist, rb_ref), NEG_INF)
    far = s_c[:, :near0] + _sample_bias(jnp.full((N_QROWS, 1), FAR_DIST, I32), rb_ref)
    p_c = _masked_softmax(jnp.concatenate([far, near], axis=1))
    o_c = _dot_nt(p_c[:, 0:chunk_blocks].astype(BF16), vct_sc[0])
    for c in range(1, n_chunks):
        o_c = o_c + _dot_nt(p_c[:, c * chunk_blocks:(c + 1) * chunk_blocks].astype(BF16), vct_sc[c])
    oc_ref[0] = o_c

    p_sum = jnp.sum(p_c.reshape(N_KV_HEADS, GQ, SEQ_PAD, n_cmp), axis=1).reshape(n_prob, n_cmp)
    p_sum = jnp.concatenate([p_sum, jnp.zeros((LANES - n_prob, n_cmp), F32)], axis=0)
    ratio = SEL_BLOCK // CMP_BLOCK
    pair = (lax.broadcasted_iota(I32, (n_sel_past, n_cmp), 1) // ratio
            == lax.broadcasted_iota(I32, (n_sel_past, n_cmp), 0)).astype(F32)
    imp = _dot_nt(pair, p_sum, hi)
    blk = lax.broadcasted_iota(I32, (n_sel_past, LANES), 0)
    pick_i = lax.broadcasted_iota(I32, (SEL_TOPK, LANES), 0)

    def pick(k, carry):
        work, ids = carry
        best = jnp.max(work, 0, keepdims=True)
        first = jnp.min(jnp.where(work == best, blk, n_sel_past), 0, keepdims=True)
        return jnp.where(blk == first, NEG_INF, work), jnp.where(pick_i == k, first, ids)

    _, ids = lax.fori_loop(0, N_PICK, pick, (imp, jnp.zeros((SEL_TOPK, LANES), I32)), unroll=True)
    idx_ref[0] = ids


def _nsa_cmp(page_table, rel_bias, nq_s, pool_wt, cache3):
    dec_b, n_pages = page_table.shape
    past_len = n_pages * PAGE_SIZE
    assert n_pages % CMP_PAGES == 0 and past_len // SEL_BLOCK >= N_PICK
    n_chunks = n_pages // CMP_PAGES
    chunk_rows = CMP_PAGES * PAGE_SIZE
    chunk_blocks = chunk_rows // CMP_BLOCK
    assert chunk_blocks == LANES
    grid_spec = pltpu.PrefetchScalarGridSpec(
        num_scalar_prefetch=1,
        grid=(dec_b,),
        in_specs=[pl.BlockSpec(memory_space=pltpu.SMEM),
                  pl.BlockSpec((1, SEQ_PAD, NSA_W), lambda b, pt: (b, 0, 0)),
                  pl.BlockSpec((2 * LANES, PAGE_SIZE), lambda b, pt: (0, 0)),
                  pl.BlockSpec(memory_space=pl.ANY)],
        out_specs=[pl.BlockSpec((1, N_QROWS, LANES), lambda b, pt: (b, 0, 0)),
                   pl.BlockSpec((1, SEL_TOPK, LANES), lambda b, pt: (b, 0, 0))],
        scratch_shapes=[pltpu.VMEM((CMP_SLOTS, CMP_PAGES, 2 * LANES, PAGE_SIZE), F32),
                        pltpu.SemaphoreType.DMA((CMP_SLOTS,)),
                        pltpu.VMEM((chunk_rows, chunk_blocks), BF16),
                        pltpu.VMEM((n_chunks, LANES, chunk_blocks), BF16),
                        pltpu.VMEM((n_chunks, LANES, chunk_blocks), BF16)],
    )
    return pl.pallas_call(
        functools.partial(_nsa_cmp_body, n_pages=n_pages, past_len=past_len),
        grid_spec=grid_spec,
        out_shape=[jax.ShapeDtypeStruct((dec_b, N_QROWS, LANES), F32),
                   jax.ShapeDtypeStruct((dec_b, SEL_TOPK, LANES), I32)],
        compiler_params=pltpu.CompilerParams(dimension_semantics=("arbitrary",),
                                             vmem_limit_bytes=VMEM_LIMIT),
        name="nsa_sample_cmp",
    )(page_table.reshape(-1), rel_bias, nq_s, pool_wt, cache3)


def _nsa_sel_body(pt_ref, ids_ref, rb_ref, nq_ref, kvn_ref, winc_ref, winn_ref, zg_ref, oc_ref, idv_ref,
                  cache_ref, out_ref, buf, sem, *, n_pages, past_len):
    b = pl.program_id(0)
    nb = pl.num_programs(0)
    hi = lax.Precision.HIGHEST

    def block_copy(bb, n, slot):
        blk = ids_ref[bb * N_FETCH + n]
        page = pt_ref[bb * n_pages + blk // 2]
        return pltpu.make_async_copy(cache_ref.at[page, pl.ds(2 * LANES, 2 * LANES), :], buf.at[slot, n],
                                     sem.at[slot])

    def start_all(bb, slot):
        def go(n, carry):
            block_copy(bb, n, slot).start()
            return carry
        lax.fori_loop(0, N_FETCH, go, 0, unroll=8)

    def wait_all(bb, slot):
        def go(n, carry):
            block_copy(bb, n, slot).wait()
            return carry
        lax.fori_loop(0, N_FETCH, go, 0, unroll=8)

    slot = b % 2

    @pl.when(b == 0)
    def _prologue():
        start_all(0, 0)

    @pl.when(b + 1 < nb)
    def _prefetch():
        start_all(b + 1, 1 - slot)

    qbd = _block_diag_queries(nq_ref[0], SEQ_PAD)
    tok, head = _sample_row_ids()
    owner = (head // GQ) * N_NEW + tok

    wc = winc_ref[0]
    wn = winn_ref[0]
    r_i = lax.broadcasted_iota(I32, (N_QROWS, WINDOW), 1)
    dist1 = WINDOW + tok - r_i
    bias_table = _sample_bias_table(rb_ref)
    s1 = _dot(qbd, wc[0:LANES, :].astype(BF16)) + _sample_bias_lookup(dist1, bias_table)
    s1 = jnp.where((dist1 >= 0) & (dist1 <= WINDOW), s1, NEG_INF)
    n_i = lax.broadcasted_iota(I32, (N_QROWS, SEQ_PAD), 1) - (SEQ_PAD - N_NEW)
    dist2 = tok - n_i
    new_ok = (n_i >= 0) & (dist2 >= 0)
    bias2 = _sample_bias(dist2, rb_ref)
    s2 = jnp.where(new_ok, _dot_nt(qbd, wn[:, 0:LANES].astype(BF16)) + bias2, NEG_INF)
    p1, p2 = _softmax_two(s1, s2)
    o_w = (_dot_nt(p1.astype(BF16), wc[LANES:2 * LANES, :].astype(BF16))
           + _dot(p2.astype(BF16), wn[:, LANES:2 * LANES].astype(BF16)))

    wait_all(b, slot)

    def owner_tiles(o, r0):
        return jnp.concatenate([buf[slot, o * N_PICK + k, r0:r0 + LANES, :] for k in range(N_PICK)],
                               axis=1).astype(BF16)

    s_f = jnp.full((N_QROWS, OWNER_KEYS), NEG_INF, F32)
    for o in range(N_OWNERS):
        s_f = jnp.where(owner == o, _dot(qbd, owner_tiles(o, 0)), s_f)
    spread = (lax.broadcasted_iota(I32, (SEL_TOPK, OWNER_KEYS), 0)
              == lax.broadcasted_iota(I32, (SEL_TOPK, OWNER_KEYS), 1) // PAGE_SIZE).astype(F32)
    blk_of_key = _dot_tn(idv_ref[0].astype(F32), spread, hi)
    blk_rows = jnp.concatenate([blk_of_key[(h // GQ) * SEQ_PAD:(h // GQ + 1) * SEQ_PAD]
                                for h in range(N_NSA_HEADS)], axis=0).astype(I32)
    blocks_per_page = PAGE_SIZE // SEL_BLOCK
    in_page = lax.broadcasted_iota(I32, (N_QROWS, OWNER_KEYS), 1) % PAGE_SIZE
    in_block = in_page // SEL_BLOCK == blk_rows % blocks_per_page
    dist_s = past_len + tok - ((blk_rows // blocks_per_page) * PAGE_SIZE + in_page)
    s_f = jnp.where((tok >= 0) & in_block & (dist_s >= 0), s_f + _sample_bias_lookup(dist_s, bias_table), NEG_INF)
    kvn = kvn_ref[0]
    s_n = jnp.where(new_ok, _dot_nt(qbd, kvn[:, 2 * LANES:3 * LANES].astype(BF16)) + bias2, NEG_INF)
    p_f, p_n = _softmax_two(s_f, s_n)
    o_s = _dot(p_n.astype(BF16), kvn[:, 3 * LANES:4 * LANES].astype(BF16))
    for o in range(N_OWNERS):
        o_s = o_s + _dot_nt(jnp.where(owner == o, p_f, 0.0).astype(BF16), owner_tiles(o, LANES))

    o_c = oc_ref[0]
    gates = jax.nn.sigmoid(zg_ref[0])
    is_new = lax.broadcasted_iota(I32, (SEQ_PAD, HEAD_DIM), 0) >= SEQ_PAD - N_NEW
    for h in range(N_NSA_HEADS):
        mix = _gated_mix(gates, o_c, o_s, o_w, h, SEQ_PAD)
        out_ref[0, :, h * HEAD_DIM:(h + 1) * HEAD_DIM] = jnp.where(is_new, mix, 0.0)


def _nsa_sel(page_table, ids, ids_rows, rel_bias, nq_s, kv_s, win_cache, win_s, zg_s, o_c, cache3):
    dec_b, n_pages = page_table.shape
    past_len = n_pages * PAGE_SIZE
    seq = lambda w: pl.BlockSpec((1, SEQ_PAD, w), lambda b, pt, ix: (b, 0, 0))
    grid_spec = pltpu.PrefetchScalarGridSpec(
        num_scalar_prefetch=2,
        grid=(dec_b,),
        in_specs=[pl.BlockSpec(memory_space=pltpu.SMEM),
                  seq(NSA_W), seq(4 * KV_W),
                  pl.BlockSpec((1, 2 * KV_W, WINDOW), lambda b, pt, ix: (b, 0, 0)),
                  seq(2 * KV_W), seq(LANES),
                  pl.BlockSpec((1, N_QROWS, LANES), lambda b, pt, ix: (b, 0, 0)),
                  pl.BlockSpec((1, SEL_TOPK, LANES), lambda b, pt, ix: (b, 0, 0)),
                  pl.BlockSpec(memory_space=pl.ANY)],
        out_specs=seq(NSA_W),
        scratch_shapes=[pltpu.VMEM((2, N_FETCH, 2 * LANES, PAGE_SIZE), F32),
                        pltpu.SemaphoreType.DMA((2,))],
    )
    return pl.pallas_call(
        functools.partial(_nsa_sel_body, n_pages=n_pages, past_len=past_len),
        grid_spec=grid_spec,
        out_shape=jax.ShapeDtypeStruct((dec_b, SEQ_PAD, NSA_W), F32),
        compiler_params=pltpu.CompilerParams(dimension_semantics=("arbitrary",),
                                             vmem_limit_bytes=VMEM_LIMIT),
        name="nsa_sample_sel",
    )(page_table.reshape(-1), ids.reshape(-1), rel_bias, nq_s, kv_s, win_cache, win_s, zg_s, o_c, ids_rows, cache3)


FF_CHUNK = 896


def _ffn_body(*refs, tm, has_override):
    if has_override:
        (x_ref, mm_ref, mn_ref, ov_ref, woa_ref, wob_ref, l1w_ref, l1b_ref, wup_ref, bup_ref, cw_ref, cb_ref,
         wdn_ref, bdn_ref, l2w_ref, l2b_ref, y_ref, tail_ref, carry_sc, ubuf) = refs
    else:
        (x_ref, mm_ref, mn_ref, woa_ref, wob_ref, l1w_ref, l1b_ref, wup_ref, bup_ref, cw_ref, cb_ref,
         wdn_ref, bdn_ref, l2w_ref, l2b_ref, y_ref, tail_ref, carry_sc, ubuf) = refs
    j = pl.program_id(1)

    @pl.when(j == 0)
    def _init():
        carry_sc[...] = jnp.zeros(carry_sc.shape, F32)

    x = x_ref[0]
    h = ALPHA * x + (_dot(mm_ref[0].astype(BF16), woa_ref[...]) + _dot(mn_ref[0].astype(BF16), wob_ref[...]))
    x1 = _layer_norm(h, l1w_ref[...], l1b_ref[...])
    x1b = x1.astype(BF16)
    if has_override:
        state_row = lax.broadcasted_iota(I32, (tm, FF_CHUNK), 0) % SEQ_PAD < SEQ_PAD - N_NEW

    def conv_half(c0):
        cols = slice(c0, c0 + FF_CHUNK)
        u = _dot(x1b, wup_ref[:, cols]) + bup_ref[:, cols]
        if has_override:
            u = jnp.where(state_row, ov_ref[0, :, cols], u)
        ubuf[0:8, :] = carry_sc[:, cols]
        ubuf[8:8 + tm, :] = u
        out = cb_ref[:, cols] + ubuf[6:6 + tm, :] * cw_ref[0:1, cols]
        out = out + ubuf[7:7 + tm, :] * cw_ref[1:2, cols]
        out = out + u * cw_ref[2:3, cols]
        carry_sc[:, cols] = u[tm - 8:tm, :]
        tail_ref[0, 0, :, cols] = u if has_override else u[tm - 8:tm, :]
        return out

    ff = jnp.zeros((tm, D_MODEL), F32)
    for c in range(D_FF // FF_CHUNK):
        ga = conv_half(c * FF_CHUNK)
        gb = conv_half(D_FF + c * FF_CHUNK)
        gelu = ga * (0.5 * (1.0 + jnp.tanh(math.sqrt(2.0 / math.pi) * (ga + 0.044715 * (ga * ga * ga)))))
        ff = ff + _dot((gelu * gb).astype(BF16), wdn_ref[c * FF_CHUNK:(c + 1) * FF_CHUNK, :])
    ff = ff + bdn_ref[...]
    y_ref[0] = _layer_norm(ALPHA * x1 + ff, l2w_ref[...], l2b_ref[...])


def _ffn(x, mix_m, mix_n, override, wts):
    bsz, t_len, _ = x.shape
    tm = min(t_len, 512)
    nt = t_len // tm
    has_override = override is not None
    tail_rows = tm if has_override else 8
    row = lambda w: pl.BlockSpec((1, tm, w), lambda b, j: (b, j, 0))
    const = lambda shape: pl.BlockSpec(shape, lambda b, j: (0,) * len(shape), pipeline_mode=pl.Buffered(1))
    in_specs = [row(D_MODEL), row(MLSTM_W), row(NSA_W)]
    args = [x, mix_m, mix_n]
    if has_override:
        in_specs.append(row(2 * D_FF))
        args.append(override)
    in_specs += [const(w.shape) for w in wts]
    args += list(wts)
    y, tail = pl.pallas_call(
        functools.partial(_ffn_body, tm=tm, has_override=has_override),
        grid=(bsz, nt),
        in_specs=in_specs,
        out_specs=[row(D_MODEL), pl.BlockSpec((1, 1, tail_rows, 2 * D_FF), lambda b, j: (b, j, 0, 0))],
        out_shape=[jax.ShapeDtypeStruct((bsz, t_len, D_MODEL), F32),
                   jax.ShapeDtypeStruct((bsz, nt, tail_rows, 2 * D_FF), F32)],
        scratch_shapes=[pltpu.VMEM((8, 2 * D_FF), F32), pltpu.VMEM((8 + tm, FF_CHUNK), F32)],
        compiler_params=pltpu.CompilerParams(dimension_semantics=("arbitrary", "arbitrary"),
                                             vmem_limit_bytes=VMEM_LIMIT),
        name="ffn",
    )(*args)
    return y, tail


def _permute_in_proj(w_in, b_in):
    gates_a = slice(2 * MLSTM_W + 2 * MLSTM_W, 2 * MLSTM_W + 2 * MLSTM_W + 2 * N_MLSTM_HEADS)
    nsa0 = gates_a.stop
    nsa1 = nsa0 + NSA_W + 6 * KV_W

    def perm(a):
        pad = IN_PAD - a.shape[-1]
        parts = [a[..., :gates_a.start], a[..., nsa0:nsa1], a[..., gates_a], a[..., nsa1:]]
        parts.append(jnp.zeros(a.shape[:-1] + (pad,), a.dtype))
        return jnp.concatenate(parts, axis=-1)

    return perm(w_in).astype(BF16), perm(b_in).reshape(1, IN_PAD)


def _front_pad(a, n):
    return jnp.pad(a, ((0, 0), (n, 0), (0, 0)))


def kernel(x_prompt, x_sample, cache_nsa_kv, cache_win_kv, state_mlstm_c, state_mlstm_n, state_mlstm_m,
           state_mlstm_conv, state_ffn_conv, page_table,
           w_in, b_in, mlstm_conv_w, mlstm_conv_b, mlstm_norm_w, nsa_pool_w, rel_bias, w_out,
           ln1_w, ln1_b, w_up, b_up, ffn_conv_w, ffn_conv_b, w_down, b_down, ln2_w, ln2_b):
    bsz, t_len, _ = x_prompt.shape
    dec_b, n_new, _ = x_sample.shape
    assert n_new == N_NEW and cache_win_kv.shape[1] == WINDOW

    w_p, b_p = _permute_in_proj(w_in, b_in)
    pool_w2 = jnp.concatenate([jnp.repeat(nsa_pool_w[0], HEAD_DIM, axis=-1),
                               jnp.repeat(nsa_pool_w[1], HEAD_DIM, axis=-1)], axis=-1)
    row = lambda v: v.reshape(1, -1)
    ffn_w = (w_out[:MLSTM_W].astype(BF16), w_out[MLSTM_W:].astype(BF16), row(ln1_w), row(ln1_b),
             w_up.astype(BF16), row(b_up), ffn_conv_w, row(ffn_conv_b), w_down.astype(BF16), row(b_down),
             row(ln2_w), row(ln2_b))

    zqk, zv, zo, nq, kv, win, zg, kv_t = _proj_in(x_prompt.reshape(bsz * t_len, D_MODEL), w_p, b_p, seq_len=t_len)
    seq3 = lambda a: a.reshape(bsz, t_len, a.shape[-1])
    zqk, zv, zo, nq, kv, win, zg = map(seq3, (zqk, zv, zo, nq, kv, win, zg))
    mix_m, c_p, n_p, m_p = _mlstm(zqk, zv, zo, zg, None, mlstm_conv_w, mlstm_conv_b, mlstm_norm_w, 0)
    mix_n = _nsa_prompt(nq, kv, win, zg, pool_w2, rel_bias)
    y_p, tail_p = _ffn(x_prompt, mix_m, mix_n, None, ffn_w)
    kv_p = jnp.transpose(kv_t.reshape(bsz, 4, N_KV_HEADS, HEAD_DIM, t_len), (0, 4, 1, 2, 3))
    win_p = win[:, t_len - min(WINDOW, t_len):].reshape(bsz, -1, 2, N_KV_HEADS, HEAD_DIM)
    mconv_p = zqk[:, t_len - (MLSTM_CONV - 1):]
    fconv_p = tail_p[:, -1, 8 - (FFN_CONV - 1):]

    pad = SEQ_PAD - N_NEW
    xs = _front_pad(x_sample, pad)
    szqk, szv, szo, snq, skv, swin, szg = _proj_in(xs.reshape(dec_b * SEQ_PAD, D_MODEL), w_p, b_p)
    sseq = lambda a: a.reshape(dec_b, SEQ_PAD, a.shape[-1])
    szqk, szv, szo, snq, skv, swin, szg = map(sseq, (szqk, szv, szo, snq, skv, swin, szg))
    lead = MLSTM_CHUNK - N_NEW
    new = lambda a: a[:, pad:]
    m_qk = jnp.concatenate([jnp.zeros((dec_b, lead - (MLSTM_CONV - 1), 2 * MLSTM_W), F32),
                            state_mlstm_conv.astype(F32), new(szqk)], axis=1)
    mix_ms, c_s, n_s, m_s = _mlstm(m_qk, _front_pad(new(szv), lead), _front_pad(new(szo), lead),
                                   _front_pad(new(szg), lead),
                                   (state_mlstm_c, state_mlstm_n, state_mlstm_m),
                                   mlstm_conv_w, mlstm_conv_b, mlstm_norm_w, lead)
    mix_ms = mix_ms[:, MLSTM_CHUNK - SEQ_PAD:]
    cache3 = jnp.transpose(cache_nsa_kv, (0, 2, 3, 4, 1)).reshape(cache_nsa_kv.shape[0], 4 * KV_W, PAGE_SIZE)
    pool_wt = jnp.tile(jnp.repeat(jnp.transpose(nsa_pool_w, (0, 2, 1)).reshape(2 * N_KV_HEADS, CMP_BLOCK),
                                  HEAD_DIM, axis=0), (1, PAGE_SIZE // CMP_BLOCK))
    o_c, ids_rows = _nsa_cmp(page_table, rel_bias, snq, pool_wt, cache3)
    ids = jnp.transpose(ids_rows[:, :N_PICK, :N_KV_HEADS * SEQ_PAD], (0, 2, 1))
    ids = ids.reshape(dec_b, N_KV_HEADS, SEQ_PAD, N_PICK)[:, :, pad:]
    win_cache = jnp.transpose(cache_win_kv, (0, 2, 3, 4, 1)).reshape(dec_b, 2 * KV_W, WINDOW)
    mix_ns = _nsa_sel(page_table, ids, ids_rows, rel_bias, snq, skv, win_cache, swin, szg, o_c, cache3)
    override = jnp.concatenate([jnp.zeros((dec_b, pad - (FFN_CONV - 1), 2 * D_FF), F32),
                                state_ffn_conv.astype(F32),
                                jnp.zeros((dec_b, N_NEW, 2 * D_FF), F32)], axis=1)
    flat = lambda a: a.reshape(1, dec_b * SEQ_PAD, a.shape[-1])
    y_s8, u_s = _ffn(flat(xs), flat(mix_ms), flat(mix_ns), flat(override), ffn_w)
    y_s = y_s8.reshape(dec_b, SEQ_PAD, D_MODEL)[:, pad:]
    kv_s = new(skv).reshape(dec_b, N_NEW, 4, N_KV_HEADS, HEAD_DIM)
    win_s = jnp.concatenate([cache_win_kv[:, N_NEW:], new(swin).reshape(dec_b, N_NEW, 2, N_KV_HEADS, HEAD_DIM)],
                            axis=1)
    mconv_s = szqk[:, SEQ_PAD - (MLSTM_CONV - 1):]
    fconv_s = u_s.reshape(dec_b, SEQ_PAD, 2 * D_FF)[:, SEQ_PAD - (FFN_CONV - 1):]
    return (y_p, y_s, kv_p, kv_s, win_p, win_s, c_p, c_s, n_p, n_s, m_p, m_s, mconv_p, mconv_s, fconv_p, fconv_s)
```

```python
import functools
import math

import jax
import jax.numpy as jnp
from jax import lax
from jax.experimental import pallas as pl
from jax.experimental.pallas import tpu as pltpu

F32 = jnp.float32
BF16 = jnp.bfloat16
I32 = jnp.int32

D_MODEL = 1024
HEAD_DIM = 64
N_MLSTM_HEADS = 8
N_NSA_HEADS = 8
N_KV_HEADS = 2
GQ = N_NSA_HEADS // N_KV_HEADS
MLSTM_W = N_MLSTM_HEADS * HEAD_DIM
NSA_W = N_NSA_HEADS * HEAD_DIM
KV_W = N_KV_HEADS * HEAD_DIM
MLSTM_CHUNK = 64
MLSTM_CONV = 4
CMP_BLOCK = 32
SEL_BLOCK = 64
SEL_TOPK = 16
WINDOW = 512
NSA_QBLOCK = 64
N_BUCKETS = 32
MAX_DISTANCE = 128
D_FF = 2688
FFN_CONV = 3
DEPTH = 1
ALPHA = (2.0 * DEPTH) ** 0.25
LN_EPS = 1e-5
PAGE_SIZE = 128

LANES = 128
VMEM_LIMIT = 56 * 1024 * 1024
NEG_INF = float("-inf")

SEQ_PAD = 8
N_NEW = 4
N_QROWS = N_NSA_HEADS * SEQ_PAD


def _bucket_thresholds():
    max_exact = N_BUCKETS // 2
    thr = list(range(max_exact + 1))
    for k in range(1, N_BUCKETS - max_exact):
        thr.append(int(math.ceil(max_exact * (MAX_DISTANCE / max_exact) ** (k / (N_BUCKETS - max_exact)))))
    return tuple(thr)


BUCKET_THR = _bucket_thresholds()
FAR_DIST = BUCKET_THR[N_BUCKETS - 1]


def _dot(a, b, precision=None):
    return jnp.dot(a, b, preferred_element_type=F32, precision=precision)


def _dot_nt(a, b, precision=None):
    return lax.dot_general(a, b, (((1,), (1,)), ((), ())), preferred_element_type=F32, precision=precision)


def _dot_tn(a, b, precision=None):
    return lax.dot_general(a, b, (((0,), (0,)), ((), ())), preferred_element_type=F32, precision=precision)


def _layer_norm(x, w, b):
    mu = jnp.mean(x, -1, keepdims=True)
    xc = x - mu
    var = jnp.mean(xc * xc, -1, keepdims=True)
    return xc * lax.rsqrt(var + LN_EPS) * w + b


def _log_sigmoid(x):
    return jnp.minimum(x, 0.0) - jnp.log1p(jnp.exp(-jnp.abs(x)))


def _bias_from_dist(dist, rb_ref, head):
    acc = jnp.full(dist.shape, rb_ref[0, head], F32)
    for k in range(1, N_BUCKETS):
        acc = jnp.where(dist >= BUCKET_THR[k], rb_ref[k, head], acc)
    return acc


def _masked_softmax(s):
    mx = jnp.max(s, -1, keepdims=True)
    mx = jnp.where(mx > NEG_INF, mx, 0.0)
    e = jnp.exp(s - mx)
    den = jnp.sum(e, -1, keepdims=True)
    return e / jnp.maximum(den, 1e-30)


_C_QK, _C_V, _C_O, _C_NQ, _C_KV, _C_WIN, _C_G = 0, 1024, 1536, 2048, 2560, 3072, 3328
IN_PAD = 3456
G_IGATE, G_FGATE, G_NSA = 0, N_MLSTM_HEADS, 2 * N_MLSTM_HEADS


def _proj_in_body(*refs, kv_transposed):
    if kv_transposed:
        x_ref, w_ref, b_ref, wkvt_ref, bkv_ref, qk_ref, v_ref, o_ref, nq_ref, kv_ref, win_ref, g_ref, kvt_ref = refs
    else:
        x_ref, w_ref, b_ref, qk_ref, v_ref, o_ref, nq_ref, kv_ref, win_ref, g_ref = refs
    xb = x_ref[...].astype(BF16)
    z = _dot(xb, w_ref[...]) + b_ref[...]
    qk_ref[...] = z[:, _C_QK:_C_V]
    v_ref[...] = z[:, _C_V:_C_O]
    o_ref[...] = z[:, _C_O:_C_NQ]
    nq_ref[...] = z[:, _C_NQ:_C_KV]
    kv_ref[...] = z[:, _C_KV:_C_WIN]
    win_ref[...] = z[:, _C_WIN:_C_G]
    g_ref[...] = z[:, _C_G:IN_PAD]
    if kv_transposed:
        kvt_ref[0] = _dot_nt(wkvt_ref[...], xb) + bkv_ref[...]


def _proj_in(x2d, w_p, b_p, seq_len=None):
    rows = x2d.shape[0]
    tm = min(rows, 512)
    widths = (1024, 512, 512, 512, 512, 256, 128)
    const = lambda i: (0, 0)
    const_spec = lambda shape: pl.BlockSpec(shape, const, pipeline_mode=pl.Buffered(1))
    in_specs = [pl.BlockSpec((tm, D_MODEL), lambda i: (i, 0)), const_spec((D_MODEL, IN_PAD)), const_spec((1, IN_PAD))]
    args = [x2d, w_p, b_p]
    out_specs = [pl.BlockSpec((tm, w), lambda i: (i, 0)) for w in widths]
    out_shape = [jax.ShapeDtypeStruct((rows, w), F32) for w in widths]
    if seq_len is not None:
        tiles = seq_len // tm
        in_specs += [const_spec((4 * KV_W, D_MODEL)), const_spec((4 * KV_W, 1))]
        args += [jnp.transpose(w_p[:, _C_KV:_C_WIN]), b_p[0, _C_KV:_C_WIN].reshape(4 * KV_W, 1)]
        out_specs.append(pl.BlockSpec((1, 4 * KV_W, tm), lambda i: (i // tiles, 0, i % tiles)))
        out_shape.append(jax.ShapeDtypeStruct((rows // seq_len, 4 * KV_W, seq_len), F32))
    return pl.pallas_call(
        functools.partial(_proj_in_body, kv_transposed=seq_len is not None),
        grid=(rows // tm,),
        in_specs=in_specs,
        out_specs=out_specs,
        out_shape=out_shape,
        compiler_params=pltpu.CompilerParams(dimension_semantics=("arbitrary",),
                                             vmem_limit_bytes=VMEM_LIMIT),
        name="proj_in",
    )(*args)


N_PAIRS = N_MLSTM_HEADS // 2
GATE_LANE0 = N_MLSTM_HEADS
MLSTM_SEQS_PER_STEP = 4


def _dot_split(x, onehot_bf16, pieces):
    acc = None
    rem = x
    for n in range(pieces):
        part = rem.astype(BF16)
        d = _dot(part, onehot_bf16)
        acc = d if acc is None else acc + d
        if n + 1 < pieces:
            rem = rem - part.astype(F32)
    return acc


def _mlstm_body(*refs, tb, nbg, n_dummy, has_state):
    if has_state:
        (zqk_ref, zv_ref, zo_ref, zg_ref, c0_ref, n0_ref, m0_ref, cw_ref, cb_ref, nw_ref,
         mix_ref, c_out, n_out, m_out, ubuf, qk_sc, ct_sc, n_sc, m_sc) = refs
    else:
        (zqk_ref, zv_ref, zo_ref, zg_ref, cw_ref, cb_ref, nw_ref,
         mix_ref, c_out, n_out, m_out, ubuf, qk_sc, ct_sc, n_sc, m_sc) = refs
    L = MLSTM_CHUNK
    D = HEAD_DIM
    j = pl.program_id(1)
    hi = lax.Precision.HIGHEST
    head_lanes = slice(GATE_LANE0, GATE_LANE0 + N_MLSTM_HEADS)
    pairs = [(bi, p) for bi in range(nbg) for p in range(N_PAIRS)]

    @pl.when(j == 0)
    def _init():
        ubuf[:, 0:8, :] = jnp.zeros((nbg, 8, 2 * MLSTM_W), F32)
        ct_sc[...] = jnp.zeros(ct_sc.shape, F32)
        n_sc[...] = jnp.zeros(n_sc.shape, F32)
        m_sc[...] = jnp.zeros(m_sc.shape, F32)
        if has_state:
            for bi in range(nbg):
                for h in range(N_MLSTM_HEADS):
                    p, hh = divmod(h, 2)
                    hs = slice(hh * D, (hh + 1) * D)
                    ct_sc[bi, p, hs, hs] = c0_ref[bi, h]
                    n_sc[bi, p, :, hs] = n0_ref[bi, h:h + 1, :]
                m_sc[bi, :, head_lanes] = m0_ref[bi]

    for bi in range(nbg):
        u = zqk_ref[bi]
        ubuf[bi, 8:8 + tb, :] = u
        conv = cb_ref[...] + ubuf[bi, 5:5 + tb, :] * cw_ref[0:1, :]
        conv = conv + ubuf[bi, 6:6 + tb, :] * cw_ref[1:2, :]
        conv = conv + ubuf[bi, 7:7 + tb, :] * cw_ref[2:3, :]
        conv = conv + u * cw_ref[3:4, :]
        ubuf[bi, 5:8, :] = u[tb - 3:tb, :]
        qk_sc[bi] = conv * jax.nn.sigmoid(conv)

    tri = (lax.broadcasted_iota(I32, (L, L), 1) <= lax.broadcasted_iota(I32, (L, L), 0)).astype(F32)
    row_l = lax.broadcasted_iota(I32, (L, LANES), 0)
    lane = lax.broadcasted_iota(I32, (L, LANES), 1)
    low = lane < D
    key_tok = lane % D
    causal_pair = key_tok <= row_l
    eye_pair = key_tok == row_l
    spread = (lax.broadcasted_iota(I32, (LANES, MLSTM_W), 0) - GATE_LANE0
              == lax.broadcasted_iota(I32, (LANES, MLSTM_W), 1) // D).astype(BF16)
    same_head = (lax.broadcasted_iota(I32, (LANES, LANES), 0) // D
                 == lax.broadcasted_iota(I32, (LANES, LANES), 1) // D)
    ones_bd = same_head.astype(BF16)
    slab_rows = 3 * L + 8

    def chunk(ci, carry):
        r0 = pl.multiple_of(ci * L, L)
        rows = pl.ds(r0, L)
        gates = []
        for bi in range(nbg):
            g = zg_ref[bi, rows, :]
            if n_dummy:
                dummy = (row_l + ci * L) < n_dummy
                g = jnp.where(dummy, jnp.where(lane < G_FGATE, -1e30, 1e30), g)
            gates.append(g)
        b_all = _dot(tri, jnp.concatenate([_log_sigmoid(g) for g in gates], axis=1), hi)
        stacks = []
        for bi in range(nbg):
            b = b_all[:, bi * LANES:(bi + 1) * LANES]
            r = pltpu.roll(gates[bi], GATE_LANE0 - G_IGATE, 1) - b
            cm = r
            for sh in (1, 2, 4, 8, 16, 32):
                cm = jnp.where(row_l >= sh, jnp.maximum(cm, pltpu.roll(cm, sh, 0)), cm)
            m_st = m_sc[bi]
            mx = jnp.maximum(m_st, cm)
            mx_end = mx[L - 1:L, :]
            m_sc[bi] = b[L - 1:L, :] + mx_end
            stacks += [mx, r, b + mx, m_st, mx_end, jnp.zeros((6, LANES), F32)]
        x_all = _dot_split(jnp.concatenate(stacks, axis=0), spread, 2)

        work = []
        den_terms = []
        for bi, p in pairs:
            cs = slice(p * LANES, (p + 1) * LANES)
            x0 = bi * slab_rows
            mx_p, rs_p, mt_p = x_all[x0:x0 + L, cs], x_all[x0 + L:x0 + 2 * L, cs], x_all[x0 + 2 * L:x0 + 3 * L, cs]
            mst_p, mxe_p = x_all[x0 + 3 * L:x0 + 3 * L + 1, cs], x_all[x0 + 3 * L + 1:x0 + 3 * L + 2, cs]
            wi_p = jnp.exp(mst_p - mx_p)
            fl_p = jnp.exp(-mt_p)
            wk_p = jnp.exp(rs_p - mxe_p)
            sc_p = jnp.exp(mst_p - mxe_p)
            r_row = jnp.sum(jnp.where(eye_pair, rs_p, 0.0), axis=0, keepdims=True)
            w_intra = jnp.exp(jnp.where(causal_pair, r_row - mx_p, NEG_INF))
            q_p = qk_sc[bi, rows, cs]
            k_p = qk_sc[bi, rows, MLSTM_W + p * LANES:MLSTM_W + (p + 1) * LANES] * (D ** -0.5)
            v_p = zv_ref[bi, rows, cs]
            qb = q_p.astype(BF16)
            k_bd = jnp.concatenate([jnp.where(low, k_p, 0.0), jnp.where(low, 0.0, k_p)], axis=0)
            v_bd = jnp.concatenate([jnp.where(low, v_p, 0.0), jnp.where(low, 0.0, v_p)], axis=0).astype(BF16)
            c_t = ct_sc[bi, p]
            both = _dot_nt(qb, jnp.concatenate([k_bd, c_t], axis=0).astype(BF16))
            a = w_intra * both[:, 0:LANES]
            num = _dot(a.astype(BF16), v_bd) + both[:, LANES:2 * LANES] * wi_p
            n_row = n_sc[bi, p]
            den_terms.append(a + wi_p * (q_p * n_row))
            work.append((num, fl_p))
            kw = k_p * wk_p
            upd = _dot_tn(v_p.astype(BF16), kw.astype(BF16))
            ct_sc[bi, p] = sc_p * c_t + jnp.where(same_head, upd, 0.0)
            n_sc[bi, p] = sc_p * n_row + jnp.sum(kw, axis=0, keepdims=True)
        den_all = _dot_split(jnp.concatenate(den_terms, axis=0), ones_bd, 2)

        hhs = []
        for n, (num, fl_p) in enumerate(work):
            hhs.append(num / jnp.maximum(jnp.abs(den_all[n * L:(n + 1) * L, :]), fl_p))
        mu_all = _dot_split(jnp.concatenate(hhs, axis=0), ones_bd, 1) * (1.0 / D)
        hcs = [hh - mu_all[n * L:(n + 1) * L, :] for n, hh in enumerate(hhs)]
        var_all = _dot_split(jnp.concatenate([hc * hc for hc in hcs], axis=0), ones_bd, 1) * (1.0 / D)
        for n, (bi, p) in enumerate(pairs):
            cs = slice(p * LANES, (p + 1) * LANES)
            hn = hcs[n] * lax.rsqrt(var_all[n * L:(n + 1) * L, :] + LN_EPS)
            og = jax.nn.sigmoid(zo_ref[bi, rows, cs])
            mix_ref[bi, rows, cs] = og * (hn * nw_ref[:, cs])
        return carry

    lax.fori_loop(0, tb // L, chunk, 0, unroll=2 if (tb // L) % 2 == 0 else 1)

    @pl.when(j == pl.num_programs(1) - 1)
    def _final():
        for bi in range(nbg):
            for h in range(N_MLSTM_HEADS):
                p, hh = divmod(h, 2)
                hs = slice(hh * D, (hh + 1) * D)
                c_out[bi, h] = ct_sc[bi, p, hs, hs]
                n_out[bi, h:h + 1, :] = n_sc[bi, p, :, hs]
            m_out[bi] = m_sc[bi, :, head_lanes]


def _mlstm(zqk, zv, zo, zg, state, conv_w, conv_b, norm_w, n_dummy):
    bsz, t_len, _ = zqk.shape
    tb = min(t_len, 256)
    nbg = MLSTM_SEQS_PER_STEP
    assert bsz % nbg == 0
    has_state = state is not None
    row = lambda w: pl.BlockSpec((nbg, tb, w), lambda b, j: (b, j, 0))
    per_b = lambda shape: pl.BlockSpec((nbg,) + shape, lambda b, j: (b,) + (0,) * len(shape))
    const = lambda shape: pl.BlockSpec(shape, lambda b, j: (0,) * len(shape))
    in_specs = [row(2 * MLSTM_W), row(MLSTM_W), row(MLSTM_W), row(LANES)]
    args = [zqk, zv, zo, zg]
    if has_state:
        c0, n0, m0 = state
        in_specs += [per_b((N_MLSTM_HEADS, HEAD_DIM, HEAD_DIM)), per_b((N_MLSTM_HEADS, HEAD_DIM)),
                     per_b((1, N_MLSTM_HEADS))]
        args += [c0, n0, m0.reshape(bsz, 1, N_MLSTM_HEADS)]
    in_specs += [const((MLSTM_CONV, 2 * MLSTM_W)), const((1, 2 * MLSTM_W)), const((1, MLSTM_W))]
    args += [conv_w, conv_b.reshape(1, -1), norm_w.reshape(1, -1)]
    mix, c_new, n_new, m_new = pl.pallas_call(
        functools.partial(_mlstm_body, tb=tb, nbg=nbg, n_dummy=n_dummy, has_state=has_state),
        grid=(bsz // nbg, t_len // tb),
        in_specs=in_specs,
        out_specs=[row(MLSTM_W), per_b((N_MLSTM_HEADS, HEAD_DIM, HEAD_DIM)),
                   per_b((N_MLSTM_HEADS, HEAD_DIM)), per_b((1, N_MLSTM_HEADS))],
        out_shape=[jax.ShapeDtypeStruct((bsz, t_len, MLSTM_W), F32),
                   jax.ShapeDtypeStruct((bsz, N_MLSTM_HEADS, HEAD_DIM, HEAD_DIM), F32),
                   jax.ShapeDtypeStruct((bsz, N_MLSTM_HEADS, HEAD_DIM), F32),
                   jax.ShapeDtypeStruct((bsz, 1, N_MLSTM_HEADS), F32)],
        scratch_shapes=[pltpu.VMEM((nbg, 8 + tb, 2 * MLSTM_W), F32),
                        pltpu.VMEM((nbg, tb, 2 * MLSTM_W), F32),
                        pltpu.VMEM((nbg, N_PAIRS, LANES, LANES), F32),
                        pltpu.VMEM((nbg, N_PAIRS, 1, LANES), F32),
                        pltpu.VMEM((nbg, 1, LANES), F32)],
        compiler_params=pltpu.CompilerParams(dimension_semantics=("arbitrary", "arbitrary"),
                                             vmem_limit_bytes=VMEM_LIMIT),
        name="mlstm",
    )(*args)
    return mix, c_new, n_new, m_new.reshape(bsz, N_MLSTM_HEADS)


def _block_diag_queries(x, n_rows, heads=tuple(range(N_NSA_HEADS)), scale=HEAD_DIM ** -0.5):
    lane = lax.broadcasted_iota(I32, (n_rows, LANES), 1)
    low = lane < HEAD_DIM
    pieces = []
    for h in heads:
        pair = x[:, (h // 2) * LANES:(h // 2 + 1) * LANES]
        src_low = h % 2 == 0
        dst_low = h // GQ == 0
        if src_low != dst_low:
            pair = pltpu.roll(pair, HEAD_DIM, 1)
        pieces.append(jnp.where(low if dst_low else ~low, pair, 0.0))
    return (jnp.concatenate(pieces, axis=0) * scale).astype(BF16)


def _group_lanes(o, g):
    return o[:, g * HEAD_DIM:(g + 1) * HEAD_DIM]


def _gated_mix(gates, o_c, o_s, o_w, h, n_rows):
    g = h // GQ
    rs = slice(h * n_rows, (h + 1) * n_rows)
    col = lambda branch: gates[:, G_NSA + branch * N_NSA_HEADS + h:G_NSA + branch * N_NSA_HEADS + h + 1]
    return (col(0) * _group_lanes(o_c[rs], g) + col(1) * _group_lanes(o_s[rs], g)
            + col(2) * _group_lanes(o_w[rs], g))


SEL_TILE = 256
SEL_TILE_BLOCKS = SEL_TILE // SEL_BLOCK
FAR_GROUP = 2
Q_PER_STEP = 4
WIN_BAND = WINDOW + NSA_QBLOCK
N_COLS = N_NSA_HEADS * NSA_QBLOCK
PROMPT_HEADS = tuple(GQ * (jj % N_KV_HEADS) + jj // N_KV_HEADS for jj in range(N_NSA_HEADS))
SEL_REL0 = 2 * SEL_TILE - SEL_BLOCK
SEL_BIAS_ROWS = SEL_REL0 + SEL_TILE


LOG2E = math.log2(math.e)


def _softmax2_rows(s):
    mx = jnp.max(s, 0, keepdims=True)
    mx = jnp.where(mx > NEG_INF, mx, 0.0)
    e = jnp.exp2(s - mx)
    return e / jnp.maximum(jnp.sum(e, 0, keepdims=True), 1e-30)


def _nsa_prompt_body(rb_ref, nq_ref, nqn_ref, kv_ref, win_ref, zg_ref, pw_ref, out_ref,
                     ks_sc, vst_sc, wk_sc, wv_sc, kc_sc, pv_sc, vct_sc, bs_sc, bw_sc, fc_sc,
                     m_sc, l_sc, acc_sc, ch_sc, oc_sc, fs_sc, ft_sc, *, t_len):
    QB = NSA_QBLOCK
    n_cmp = t_len // CMP_BLOCK
    n_sel = t_len // SEL_BLOCK
    pid = pl.program_id(1)
    hi = lax.Precision.HIGHEST

    def cmp_scores(x_q):
        return _dot_nt(kc_sc[...], _block_diag_queries(x_q, QB, PROMPT_HEADS, HEAD_DIM ** -0.5 * LOG2E))

    def cmp_and_select(raw, qi, out_buf):
        s_c = raw + pltpu.roll(fc_sc[...], 2 * qi, 0)
        c_i = lax.broadcasted_iota(I32, (n_cmp, N_COLS), 0)
        q_i = lax.broadcasted_iota(I32, (n_cmp, N_COLS), 1) % QB
        p_c = _softmax2_rows(jnp.where(c_i * CMP_BLOCK + (CMP_BLOCK - 1) <= qi * QB + q_i, s_c, NEG_INF))
        oc_sc[out_buf] = _dot(vct_sc[...], p_c.astype(BF16))
        p_sum = p_c[:, 0:LANES]
        for j in range(1, GQ):
            p_sum = p_sum + p_c[:, j * LANES:(j + 1) * LANES]
        ratio = SEL_BLOCK // CMP_BLOCK
        pair = (lax.broadcasted_iota(I32, (n_sel, n_cmp), 1) // ratio
                == lax.broadcasted_iota(I32, (n_sel, n_cmp), 0)).astype(F32)
        imp = _dot(pair, p_sum, hi)
        blk = lax.broadcasted_iota(I32, (n_sel, LANES), 0)
        score = jnp.where(blk == qi, jnp.inf, jnp.where(blk < qi, imp, NEG_INF))

        def pick(_, carry):
            work, chosen = carry
            best = jnp.max(work, 0, keepdims=True)
            first = jnp.min(jnp.where(work == best, blk, n_sel), 0, keepdims=True)
            hit = blk == first
            return jnp.where(hit, NEG_INF, work), jnp.where(hit, 1.0, chosen)

        _, chosen = lax.fori_loop(0, SEL_TOPK, pick, (score, jnp.zeros((n_sel, LANES), F32)), unroll=True)
        ch_sc[out_buf, 0:8, :] = jnp.full((8, LANES), NEG_INF, F32)
        ch_sc[out_buf, 8:8 + n_sel, :] = jnp.where((score > NEG_INF) & (chosen > 0.5), 0.0, NEG_INF)

    @pl.when(pid == 0)
    def _init():
        ks_sc[0:SEL_TILE, :] = jnp.zeros((SEL_TILE, LANES), BF16)
        vst_sc[0] = jnp.zeros((LANES, SEL_TILE), BF16)
        wk_sc[0:WINDOW, :] = jnp.zeros((WINDOW, LANES), BF16)
        wv_sc[0:WINDOW, :] = jnp.zeros((WINDOW, LANES), BF16)
        rows_per = 2 * SEL_TILE
        blocks_per = rows_per // CMP_BLOCK

        def fill(c, carry):
            r0 = pl.multiple_of(c * rows_per, rows_per)
            kv = kv_ref[0, pl.ds(r0, rows_per), :]
            ks_sc[pl.ds(SEL_TILE + r0, rows_per), :] = kv[:, 2 * LANES:3 * LANES].astype(BF16)
            for half in range(2):
                v_t = kv[half * SEL_TILE:(half + 1) * SEL_TILE, 3 * LANES:4 * LANES]
                vst_sc[2 * c + half + 1] = jnp.transpose(v_t).astype(BF16)
            w = win_ref[0, pl.ds(r0, rows_per), :]
            wk_sc[pl.ds(WINDOW + r0, rows_per), :] = w[:, 0:LANES].astype(BF16)
            wv_sc[pl.ds(WINDOW + r0, rows_per), :] = w[:, LANES:2 * LANES].astype(BF16)
            pooled = jnp.sum(kv[:, 0:2 * LANES].reshape(blocks_per, CMP_BLOCK, 2 * LANES) * pw_ref[...][None],
                             axis=1)
            c0 = pl.multiple_of(c * blocks_per, blocks_per)
            kc_sc[pl.ds(c0, blocks_per), :] = pooled[:, 0:LANES].astype(BF16)
            pv_sc[pl.ds(c0, blocks_per), :] = pooled[:, LANES:2 * LANES]
            return carry

        lax.fori_loop(0, t_len // rows_per, fill, 0)
        vct_sc[...] = jnp.transpose(pv_sc[...]).astype(BF16)

        for jj, h in enumerate(PROMPT_HEADS):
            cs = slice(jj * QB, (jj + 1) * QB)
            u_i = lax.broadcasted_iota(I32, (SEL_BIAS_ROWS, QB), 0)
            q_i = lax.broadcasted_iota(I32, (SEL_BIAS_ROWS, QB), 1)
            dist = q_i - (u_i - SEL_REL0)
            near = (_bias_from_dist(dist, rb_ref, h) - rb_ref[N_BUCKETS - 1, h]) * LOG2E
            bs_sc[:, cs] = jnp.where(dist >= 0, near, NEG_INF)
            s_i = lax.broadcasted_iota(I32, (WIN_BAND, QB), 0)
            q_i = lax.broadcasted_iota(I32, (WIN_BAND, QB), 1)
            dist = q_i + WINDOW - s_i
            bw_sc[:, cs] = jnp.where((dist >= 0) & (dist <= WINDOW), _bias_from_dist(dist, rb_ref, h) * LOG2E,
                                     NEG_INF)
            c_i = lax.broadcasted_iota(I32, (n_cmp, QB), 0)
            q_i = lax.broadcasted_iota(I32, (n_cmp, QB), 1)
            rel = jnp.where(c_i < n_cmp // 2, c_i, c_i - n_cmp)
            dist = q_i - CMP_BLOCK * rel - (CMP_BLOCK - 1)
            dist = jnp.where((c_i >= 2) & (c_i < n_cmp // 2), FAR_DIST, dist)
            fc_sc[:, cs] = _bias_from_dist(dist, rb_ref, h) * LOG2E

        cmp_and_select(cmp_scores(nq_ref[0, 0:QB, :]), 0, 0)

    def q_block(t):
        i = Q_PER_STEP * pid + t
        rows = slice(t * QB, (t + 1) * QB)
        x_q = nq_ref[0, rows, :]
        x_next = nq_ref[0, (t + 1) * QB:(t + 2) * QB, :] if t + 1 < Q_PER_STEP else nqn_ref[0, 0:QB, :]
        buf = t % 2
        o_c = oc_sc[buf]
        qbd = _block_diag_queries(x_q, QB, PROMPT_HEADS, HEAD_DIM ** -0.5 * LOG2E)
        last = i // SEL_TILE_BLOCKS

        def tile_raw(slot):
            return _dot_nt(ks_sc[pl.ds(pl.multiple_of(slot * SEL_TILE, SEL_TILE), SEL_TILE), :], qbd)

        raw_next = cmp_scores(x_next)
        w0 = pl.multiple_of(i * QB, QB)
        raw_w = _dot_nt(wk_sc[pl.ds(w0, WIN_BAND), :], qbd)
        raw_near = [tile_raw(last + 1), tile_raw(last)]

        def tile_scores(slot, bias, s=None):
            if s is None:
                s = tile_raw(slot)
            if bias is not None:
                s = s + bias
            parts = []
            for r in range(SEL_TILE_BLOCKS):
                row = ch_sc[buf, pl.ds(8 - SEL_TILE_BLOCKS + slot * SEL_TILE_BLOCKS + r, 1), :]
                parts.append(s[r * SEL_BLOCK:(r + 1) * SEL_BLOCK, :] + jnp.concatenate([row] * GQ, axis=1))
            return jnp.concatenate(parts, axis=0)

        def group_max(scores):
            return functools.reduce(jnp.maximum, [jnp.max(s, 0, keepdims=True) for s in scores])

        def sel_group(slots, scores, t_max, first):
            m_new = t_max if first else jnp.maximum(m_sc[...], t_max)
            exps = [jnp.exp2(s - m_new) for s in scores]
            l_new = functools.reduce(jnp.add, [jnp.sum(e, 0, keepdims=True) for e in exps])
            acc_new = functools.reduce(jnp.add, [_dot(vst_sc[slot], e.astype(BF16)) for slot, e in zip(slots, exps)])
            if first:
                l_sc[...] = l_new
                acc_sc[...] = acc_new
            else:
                scale = jnp.exp2(m_sc[...] - m_new)
                l_sc[...] = scale * l_sc[...] + l_new
                acc_sc[...] = scale * acc_sc[...] + acc_new
            m_sc[...] = m_new

        def far_slots(p):
            return [jnp.where(FAR_GROUP * p + 1 + n < last, FAR_GROUP * p + 1 + n, 0) for n in range(FAR_GROUP)]

        def far_scores(p):
            scores = [tile_scores(slot, None) for slot in far_slots(p)]
            for n, s in enumerate(scores):
                fs_sc[p % 2, n] = s
            ft_sc[p % 2] = group_max(scores)

        far_scores(0)
        cmp_and_select(raw_next, i + 1, 1 - buf)

        in_seq = lax.broadcasted_iota(I32, (WIN_BAND, N_COLS), 0) >= WINDOW - i * QB
        s_w = jnp.where(in_seq, raw_w + bw_sc[...], NEG_INF)
        e_w = jnp.exp2(s_w - jnp.max(s_w, 0, keepdims=True))
        o_w = (_dot_tn(wv_sc[pl.ds(w0, WIN_BAND), :], e_w.astype(BF16))
               / jnp.maximum(jnp.sum(e_w, 0, keepdims=True), 1e-30))

        u0 = pl.multiple_of(SEL_REL0 - SEL_TILE - SEL_BLOCK * (i % SEL_TILE_BLOCKS), SEL_BLOCK)
        near = [tile_scores(last + 1, bs_sc[pl.ds(u0 + SEL_TILE, SEL_TILE), :], raw_near[0]),
                tile_scores(last, bs_sc[pl.ds(u0, SEL_TILE), :], raw_near[1])]
        sel_group((last + 1, last), near, group_max(near), True)

        def far_group(p, carry):
            sel_group(far_slots(p), [fs_sc[p % 2, n] for n in range(FAR_GROUP)], ft_sc[p % 2], False)
            far_scores(p + 1)
            return carry

        lax.fori_loop(0, (last - 1 + FAR_GROUP - 1) // FAR_GROUP, far_group, 0)
        o_s = acc_sc[...] / jnp.maximum(l_sc[...], 1e-30)

        eye = (lax.broadcasted_iota(I32, (3 * N_NSA_HEADS + 8, LANES), 0) + G_NSA
               == lax.broadcasted_iota(I32, (3 * N_NSA_HEADS + 8, LANES), 1)).astype(F32)
        gates = jax.nn.sigmoid(_dot_nt(eye, zg_ref[0, rows, :], hi))
        pieces = []
        for h in range(N_NSA_HEADS):
            g = h // GQ
            jj = PROMPT_HEADS.index(h)
            part = lambda o: o[g * HEAD_DIM:(g + 1) * HEAD_DIM, jj * QB:(jj + 1) * QB]
            pieces.append(gates[h:h + 1, :] * part(o_c)
                          + gates[N_NSA_HEADS + h:N_NSA_HEADS + h + 1, :] * part(o_s)
                          + gates[2 * N_NSA_HEADS + h:2 * N_NSA_HEADS + h + 1, :] * part(o_w))
        mix_t = jnp.concatenate(pieces, axis=0)
        mix_t = jnp.concatenate([mix_t, jnp.zeros((NSA_W, LANES - QB), F32)], axis=1)
        out_ref[0, rows, :] = jnp.transpose(mix_t)[0:QB, :]

    for t in range(Q_PER_STEP):
        q_block(t)


def _nsa_prompt(nq, kv, win, zg, pool_w2, rel_bias):
    bsz, t_len, _ = nq.shape
    assert t_len // CMP_BLOCK == LANES and t_len % (2 * SEL_TILE) == 0
    n_cmp = t_len // CMP_BLOCK
    per_b = lambda w: pl.BlockSpec((1, t_len, w), lambda b, i: (b, 0, 0))
    step_rows = Q_PER_STEP * NSA_QBLOCK
    blk = lambda w: pl.BlockSpec((1, step_rows, w), lambda b, i: (b, i, 0))
    n_qb = t_len // step_rows
    next_blk = pl.BlockSpec((1, step_rows, NSA_W), lambda b, i: (b, jnp.minimum(i + 1, n_qb - 1), 0))
    return pl.pallas_call(
        functools.partial(_nsa_prompt_body, t_len=t_len),
        grid=(bsz, n_qb),
        in_specs=[pl.BlockSpec(memory_space=pltpu.SMEM),
                  blk(NSA_W), next_blk, per_b(4 * KV_W), per_b(2 * KV_W), blk(LANES),
                  pl.BlockSpec((CMP_BLOCK, 2 * LANES), lambda b, i: (0, 0))],
        out_specs=blk(NSA_W),
        out_shape=jax.ShapeDtypeStruct((bsz, t_len, NSA_W), F32),
        scratch_shapes=[pltpu.VMEM((SEL_TILE + t_len, LANES), BF16),
                        pltpu.VMEM((1 + t_len // SEL_TILE, LANES, SEL_TILE), BF16),
                        pltpu.VMEM((WINDOW + t_len, LANES), BF16),
                        pltpu.VMEM((WINDOW + t_len, LANES), BF16),
                        pltpu.VMEM((n_cmp, LANES), BF16),
                        pltpu.VMEM((n_cmp, LANES), F32),
                        pltpu.VMEM((LANES, n_cmp), BF16),
                        pltpu.VMEM((SEL_BIAS_ROWS, N_COLS), F32),
                        pltpu.VMEM((WIN_BAND, N_COLS), F32),
                        pltpu.VMEM((n_cmp, N_COLS), F32),
                        pltpu.VMEM((1, N_COLS), F32),
                        pltpu.VMEM((1, N_COLS), F32),
                        pltpu.VMEM((LANES, N_COLS), F32),
                        pltpu.VMEM((2, 8 + t_len // SEL_BLOCK, LANES), F32),
                        pltpu.VMEM((2, LANES, N_COLS), F32),
                        pltpu.VMEM((2, FAR_GROUP, SEL_TILE, N_COLS), F32),
                        pltpu.VMEM((2, 1, N_COLS), F32)],
        compiler_params=pltpu.CompilerParams(dimension_semantics=("arbitrary", "arbitrary"),
                                             vmem_limit_bytes=VMEM_LIMIT),
        name="nsa_prompt",
    )(rel_bias, nq, nq, kv, win, zg, pool_w2)


CMP_PAGES = 32
CMP_SLOTS = 6
N_PICK = SEL_TOPK - 1
N_OWNERS = N_KV_HEADS * N_NEW
N_FETCH = N_OWNERS * N_PICK
OWNER_KEYS = N_PICK * PAGE_SIZE


def _sample_row_ids():
    r = lax.broadcasted_iota(I32, (N_QROWS, 1), 0)
    return r % SEQ_PAD - (SEQ_PAD - N_NEW), r // SEQ_PAD


def _sample_bias(dist, rb_ref):
    _, head = _sample_row_ids()
    col = lambda k: sum(jnp.where(head == h, rb_ref[k, h], 0.0) for h in range(N_NSA_HEADS))
    acc = jnp.broadcast_to(col(0), dist.shape)
    for k in range(1, N_BUCKETS):
        acc = jnp.where(dist >= BUCKET_THR[k], col(k), acc)
    return acc


def _sample_bias_table(rb_ref):
    assert FAR_DIST < LANES
    return _sample_bias(lax.broadcasted_iota(I32, (N_QROWS, LANES), 1), rb_ref)


def _sample_bias_lookup(dist, table):
    idx = jnp.clip(dist, 0, LANES - 1)
    return jnp.concatenate([jnp.take_along_axis(table, idx[:, c * LANES:(c + 1) * LANES], axis=1)
                            for c in range(dist.shape[1] // LANES)], axis=1)


def _softmax_two(s1, s2):
    mx = jnp.maximum(jnp.max(s1, -1, keepdims=True), jnp.max(s2, -1, keepdims=True))
    mx = jnp.where(mx > NEG_INF, mx, 0.0)
    e1 = jnp.exp(s1 - mx)
    e2 = jnp.exp(s2 - mx)
    den = jnp.maximum(jnp.sum(e1, -1, keepdims=True) + jnp.sum(e2, -1, keepdims=True), 1e-30)
    return e1 / den, e2 / den


def _nsa_cmp_body(pt_ref, rb_ref, nq_ref, pwt_ref, cache_ref, oc_ref, idx_ref,
                  buf, sem, seg_sc, kct_sc, vct_sc, *, n_pages, past_len):
    b = pl.program_id(0)
    nb = pl.num_programs(0)
    n_chunks = n_pages // CMP_PAGES
    chunk_rows = CMP_PAGES * PAGE_SIZE
    chunk_blocks = chunk_rows // CMP_BLOCK
    n_cmp = past_len // CMP_BLOCK
    n_sel_past = past_len // SEL_BLOCK
    n_prob = N_KV_HEADS * SEQ_PAD
    total = nb * n_chunks
    hi = lax.Precision.HIGHEST

    def page_copy(gi, p):
        page = pt_ref[(gi // n_chunks) * n_pages + (gi % n_chunks) * CMP_PAGES + p]
        slot = gi % CMP_SLOTS
        return pltpu.make_async_copy(cache_ref.at[page, pl.ds(0, 2 * LANES), :], buf.at[slot, p], sem.at[slot])

    def start_chunk(gi):
        for p in range(CMP_PAGES):
            page_copy(gi, p).start()

    def wait_chunk(gi):
        for p in range(CMP_PAGES):
            page_copy(gi, p).wait()

    @pl.when(b == 0)
    def _prologue():
        seg_sc[...] = (lax.broadcasted_iota(I32, (chunk_rows, chunk_blocks), 0) // CMP_BLOCK
                       == lax.broadcasted_iota(I32, (chunk_rows, chunk_blocks), 1)).astype(BF16)
        for gi in range(min(CMP_SLOTS - 1, total)):
            start_chunk(gi)

    def chunk_step(c, carry):
        gi = b * n_chunks + c
        ahead = gi + (CMP_SLOTS - 1)

        @pl.when(ahead < total)
        def _prefetch():
            start_chunk(ahead)

        wait_chunk(gi)
        slot = gi % CMP_SLOTS
        y = jnp.concatenate([(buf[slot, p] * pwt_ref[...]).astype(BF16) for p in range(CMP_PAGES)], axis=1)
        pooled = _dot(y, seg_sc[...])
        kct_sc[c] = pooled[0:LANES].astype(BF16)
        vct_sc[c] = pooled[LANES:2 * LANES].astype(BF16)
        return carry

    lax.fori_loop(0, n_chunks, chunk_step, 0)

    qbd = _block_diag_queries(nq_ref[0], SEQ_PAD)
    tok, _ = _sample_row_ids()
    s_c = jnp.concatenate([_dot(qbd, kct_sc[c]) for c in range(n_chunks)], axis=1)
    near0 = n_cmp - LANES
    assert past_len - (SEQ_PAD - N_NEW) - (CMP_BLOCK * (near0 - 1) + CMP_BLOCK - 1) >= FAR_DIST
    dist = past_len + tok - ((lax.broadcasted_iota(I32, (N_QROWS, LANES), 1) + near0) * CMP_BLOCK + CMP_BLOCK - 1)
    near = jnp.where(dist >= 0, s_c[:, near0:] + _sample_bias(dist, rb_ref), NEG_INF)
    far = s_c[:, :near0] + _sample_bias(jnp.full((N_QROWS, 1), FAR_DIST, I32), rb_ref)
    p_c = _masked_softmax(jnp.concatenate([far, near], axis=1))
    o_c = _dot_nt(p_c[:, 0:chunk_blocks].astype(BF16), vct_sc[0])
    for c in range(1, n_chunks):
        o_c = o_c + _dot_nt(p_c[:, c * chunk_blocks:(c + 1) * chunk_blocks].astype(BF16), vct_sc[c])
    oc_ref[0] = o_c

    p_sum = jnp.sum(p_c.reshape(N_KV_HEADS, GQ, SEQ_PAD, n_cmp), axis=1).reshape(n_prob, n_cmp)
    p_sum = jnp.concatenate([p_sum, jnp.zeros((LANES - n_prob, n_cmp), F32)], axis=0)
    ratio = SEL_BLOCK // CMP_BLOCK
    pair = (lax.broadcasted_iota(I32, (n_sel_past, n_cmp), 1) // ratio
            == lax.broadcasted_iota(I32, (n_sel_past, n_cmp), 0)).astype(F32)
    imp = _dot_nt(pair, p_sum, hi)
    blk = lax.broadcasted_iota(I32, (n_sel_past, LANES), 0)
    pick_i = lax.broadcasted_iota(I32, (SEL_TOPK, LANES), 0)

    def pick(k, carry):
        work, ids = carry
        best = jnp.max(work, 0, keepdims=True)
        first = jnp.min(jnp.where(work == best, blk, n_sel_past), 0, keepdims=True)
        return jnp.where(blk == first, NEG_INF, work), jnp.where(pick_i == k, first, ids)

    _, ids = lax.fori_loop(0, N_PICK, pick, (imp, jnp.zeros((SEL_TOPK, LANES), I32)), unroll=True)
    idx_ref[0] = ids


def _nsa_cmp(page_table, rel_bias, nq_s, pool_wt, cache3):
    dec_b, n_pages = page_table.shape
    past_len = n_pages * PAGE_SIZE
    assert n_pages % CMP_PAGES == 0 and past_len // SEL_BLOCK >= N_PICK
    n_chunks = n_pages // CMP_PAGES
    chunk_rows = CMP_PAGES * PAGE_SIZE
    chunk_blocks = chunk_rows // CMP_BLOCK
    assert chunk_blocks == LANES
    grid_spec = pltpu.PrefetchScalarGridSpec(
        num_scalar_prefetch=1,
        grid=(dec_b,),
        in_specs=[pl.BlockSpec(memory_space=pltpu.SMEM),
                  pl.BlockSpec((1, SEQ_PAD, NSA_W), lambda b, pt: (b, 0, 0)),
                  pl.BlockSpec((2 * LANES, PAGE_SIZE), lambda b, pt: (0, 0)),
                  pl.BlockSpec(memory_space=pl.ANY)],
        out_specs=[pl.BlockSpec((1, N_QROWS, LANES), lambda b, pt: (b, 0, 0)),
                   pl.BlockSpec((1, SEL_TOPK, LANES), lambda b, pt: (b, 0, 0))],
        scratch_shapes=[pltpu.VMEM((CMP_SLOTS, CMP_PAGES, 2 * LANES, PAGE_SIZE), F32),
                        pltpu.SemaphoreType.DMA((CMP_SLOTS,)),
                        pltpu.VMEM((chunk_rows, chunk_blocks), BF16),
                        pltpu.VMEM((n_chunks, LANES, chunk_blocks), BF16),
                        pltpu.VMEM((n_chunks, LANES, chunk_blocks), BF16)],
    )
    return pl.pallas_call(
        functools.partial(_nsa_cmp_body, n_pages=n_pages, past_len=past_len),
        grid_spec=grid_spec,
        out_shape=[jax.ShapeDtypeStruct((dec_b, N_QROWS, LANES), F32),
                   jax.ShapeDtypeStruct((dec_b, SEL_TOPK, LANES), I32)],
        compiler_params=pltpu.CompilerParams(dimension_semantics=("arbitrary",),
                                             vmem_limit_bytes=VMEM_LIMIT),
        name="nsa_sample_cmp",
    )(page_table.reshape(-1), rel_bias, nq_s, pool_wt, cache3)


def _nsa_sel_body(pt_ref, ids_ref, rb_ref, nq_ref, kvn_ref, winc_ref, winn_ref, zg_ref, oc_ref, idv_ref,
                  cache_ref, out_ref, buf, sem, *, n_pages, past_len):
    b = pl.program_id(0)
    nb = pl.num_programs(0)
    hi = lax.Precision.HIGHEST

    def block_copy(bb, n, slot):
        blk = ids_ref[bb * N_FETCH + n]
        page = pt_ref[bb * n_pages + blk // 2]
        return pltpu.make_async_copy(cache_ref.at[page, pl.ds(2 * LANES, 2 * LANES), :], buf.at[slot, n],
                                     sem.at[slot])

    def start_all(bb, slot):
        def go(n, carry):
            block_copy(bb, n, slot).start()
            return carry
        lax.fori_loop(0, N_FETCH, go, 0, unroll=8)

    def wait_all(bb, slot):
        def go(n, carry):
            block_copy(bb, n, slot).wait()
            return carry
        lax.fori_loop(0, N_FETCH, go, 0, unroll=8)

    slot = b % 2

    @pl.when(b == 0)
    def _prologue():
        start_all(0, 0)

    @pl.when(b + 1 < nb)
    def _prefetch():
        start_all(b + 1, 1 - slot)

    qbd = _block_diag_queries(nq_ref[0], SEQ_PAD)
    tok, head = _sample_row_ids()
    owner = (head // GQ) * N_NEW + tok

    wc = winc_ref[0]
    wn = winn_ref[0]
    r_i = lax.broadcasted_iota(I32, (N_QROWS, WINDOW), 1)
    dist1 = WINDOW + tok - r_i
    bias_table = _sample_bias_table(rb_ref)
    s1 = _dot(qbd, wc[0:LANES, :].astype(BF16)) + _sample_bias_lookup(dist1, bias_table)
    s1 = jnp.where((dist1 >= 0) & (dist1 <= WINDOW), s1, NEG_INF)
    n_i = lax.broadcasted_iota(I32, (N_QROWS, SEQ_PAD), 1) - (SEQ_PAD - N_NEW)
    dist2 = tok - n_i
    new_ok = (n_i >= 0) & (dist2 >= 0)
    bias2 = _sample_bias(dist2, rb_ref)
    s2 = jnp.where(new_ok, _dot_nt(qbd, wn[:, 0:LANES].astype(BF16)) + bias2, NEG_INF)
    p1, p2 = _softmax_two(s1, s2)
    o_w = (_dot_nt(p1.astype(BF16), wc[LANES:2 * LANES, :].astype(BF16))
           + _dot(p2.astype(BF16), wn[:, LANES:2 * LANES].astype(BF16)))

    wait_all(b, slot)

    def owner_tiles(o, r0):
        return jnp.concatenate([buf[slot, o * N_PICK + k, r0:r0 + LANES, :] for k in range(N_PICK)],
                               axis=1).astype(BF16)

    s_f = jnp.full((N_QROWS, OWNER_KEYS), NEG_INF, F32)
    for o in range(N_OWNERS):
        s_f = jnp.where(owner == o, _dot(qbd, owner_tiles(o, 0)), s_f)
    spread = (lax.broadcasted_iota(I32, (SEL_TOPK, OWNER_KEYS), 0)
              == lax.broadcasted_iota(I32, (SEL_TOPK, OWNER_KEYS), 1) // PAGE_SIZE).astype(F32)
    blk_of_key = _dot_tn(idv_ref[0].astype(F32), spread, hi)
    blk_rows = jnp.concatenate([blk_of_key[(h // GQ) * SEQ_PAD:(h // GQ + 1) * SEQ_PAD]
                                for h in range(N_NSA_HEADS)], axis=0).astype(I32)
    blocks_per_page = PAGE_SIZE // SEL_BLOCK
    in_page = lax.broadcasted_iota(I32, (N_QROWS, OWNER_KEYS), 1) % PAGE_SIZE
    in_block = in_page // SEL_BLOCK == blk_rows % blocks_per_page
    dist_s = past_len + tok - ((blk_rows // blocks_per_page) * PAGE_SIZE + in_page)
    s_f = jnp.where((tok >= 0) & in_block & (dist_s >= 0), s_f + _sample_bias_lookup(dist_s, bias_table), NEG_INF)
    kvn = kvn_ref[0]
    s_n = jnp.where(new_ok, _dot_nt(qbd, kvn[:, 2 * LANES:3 * LANES].astype(BF16)) + bias2, NEG_INF)
    p_f, p_n = _softmax_two(s_f, s_n)
    o_s = _dot(p_n.astype(BF16), kvn[:, 3 * LANES:4 * LANES].astype(BF16))
    for o in range(N_OWNERS):
        o_s = o_s + _dot_nt(jnp.where(owner == o, p_f, 0.0).astype(BF16), owner_tiles(o, LANES))

    o_c = oc_ref[0]
    gates = jax.nn.sigmoid(zg_ref[0])
    is_new = lax.broadcasted_iota(I32, (SEQ_PAD, HEAD_DIM), 0) >= SEQ_PAD - N_NEW
    for h in range(N_NSA_HEADS):
        mix = _gated_mix(gates, o_c, o_s, o_w, h, SEQ_PAD)
        out_ref[0, :, h * HEAD_DIM:(h + 1) * HEAD_DIM] = jnp.where(is_new, mix, 0.0)


def _nsa_sel(page_table, ids, ids_rows, rel_bias, nq_s, kv_s, win_cache, win_s, zg_s, o_c, cache3):
    dec_b, n_pages = page_table.shape
    past_len = n_pages * PAGE_SIZE
    seq = lambda w: pl.BlockSpec((1, SEQ_PAD, w), lambda b, pt, ix: (b, 0, 0))
    grid_spec = pltpu.PrefetchScalarGridSpec(
        num_scalar_prefetch=2,
        grid=(dec_b,),
        in_specs=[pl.BlockSpec(memory_space=pltpu.SMEM),
                  seq(NSA_W), seq(4 * KV_W),
                  pl.BlockSpec((1, 2 * KV_W, WINDOW), lambda b, pt, ix: (b, 0, 0)),
                  seq(2 * KV_W), seq(LANES),
                  pl.BlockSpec((1, N_QROWS, LANES), lambda b, pt, ix: (b, 0, 0)),
                  pl.BlockSpec((1, SEL_TOPK, LANES), lambda b, pt, ix: (b, 0, 0)),
                  pl.BlockSpec(memory_space=pl.ANY)],
        out_specs=seq(NSA_W),
        scratch_shapes=[pltpu.VMEM((2, N_FETCH, 2 * LANES, PAGE_SIZE), F32),
                        pltpu.SemaphoreType.DMA((2,))],
    )
    return pl.pallas_call(
        functools.partial(_nsa_sel_body, n_pages=n_pages, past_len=past_len),
        grid_spec=grid_spec,
        out_shape=jax.ShapeDtypeStruct((dec_b, SEQ_PAD, NSA_W), F32),
        compiler_params=pltpu.CompilerParams(dimension_semantics=("arbitrary",),
                                             vmem_limit_bytes=VMEM_LIMIT),
        name="nsa_sample_sel",
    )(page_table.reshape(-1), ids.reshape(-1), rel_bias, nq_s, kv_s, win_cache, win_s, zg_s, o_c, ids_rows, cache3)


FF_CHUNK = 896


def _ffn_body(*refs, tm, has_override):
    if has_override:
        (x_ref, mm_ref, mn_ref, ov_ref, woa_ref, wob_ref, l1w_ref, l1b_ref, wup_ref, bup_ref, cw_ref, cb_ref,
         wdn_ref, bdn_ref, l2w_ref, l2b_ref, y_ref, tail_ref, carry_sc, ubuf) = refs
    else:
        (x_ref, mm_ref, mn_ref, woa_ref, wob_ref, l1w_ref, l1b_ref, wup_ref, bup_ref, cw_ref, cb_ref,
         wdn_ref, bdn_ref, l2w_ref, l2b_ref, y_ref, tail_ref, carry_sc, ubuf) = refs
    j = pl.program_id(1)

    @pl.when(j == 0)
    def _init():
        carry_sc[...] = jnp.zeros(carry_sc.shape, F32)

    x = x_ref[0]
    h = ALPHA * x + (_dot(mm_ref[0].astype(BF16), woa_ref[...]) + _dot(mn_ref[0].astype(BF16), wob_ref[...]))
    x1 = _layer_norm(h, l1w_ref[...], l1b_ref[...])
    x1b = x1.astype(BF16)
    if has_override:
        state_row = lax.broadcasted_iota(I32, (tm, FF_CHUNK), 0) % SEQ_PAD < SEQ_PAD - N_NEW

    def conv_half(c0):
        cols = slice(c0, c0 + FF_CHUNK)
        u = _dot(x1b, wup_ref[:, cols]) + bup_ref[:, cols]
        if has_override:
            u = jnp.where(state_row, ov_ref[0, :, cols], u)
        ubuf[0:8, :] = carry_sc[:, cols]
        ubuf[8:8 + tm, :] = u
        out = cb_ref[:, cols] + ubuf[6:6 + tm, :] * cw_ref[0:1, cols]
        out = out + ubuf[7:7 + tm, :] * cw_ref[1:2, cols]
        out = out + u * cw_ref[2:3, cols]
        carry_sc[:, cols] = u[tm - 8:tm, :]
        tail_ref[0, 0, :, cols] = u if has_override else u[tm - 8:tm, :]
        return out

    ff = jnp.zeros((tm, D_MODEL), F32)
    for c in range(D_FF // FF_CHUNK):
        ga = conv_half(c * FF_CHUNK)
        gb = conv_half(D_FF + c * FF_CHUNK)
        gelu = ga * (0.5 * (1.0 + jnp.tanh(math.sqrt(2.0 / math.pi) * (ga + 0.044715 * (ga * ga * ga)))))
        ff = ff + _dot((gelu * gb).astype(BF16), wdn_ref[c * FF_CHUNK:(c + 1) * FF_CHUNK, :])
    ff = ff + bdn_ref[...]
    y_ref[0] = _layer_norm(ALPHA * x1 + ff, l2w_ref[...], l2b_ref[...])


def _ffn(x, mix_m, mix_n, override, wts):
    bsz, t_len, _ = x.shape
    tm = min(t_len, 512)
    nt = t_len // tm
    has_override = override is not None
    tail_rows = tm if has_override else 8
    row = lambda w: pl.BlockSpec((1, tm, w), lambda b, j: (b, j, 0))
    const = lambda shape: pl.BlockSpec(shape, lambda b, j: (0,) * len(shape), pipeline_mode=pl.Buffered(1))
    in_specs = [row(D_MODEL), row(MLSTM_W), row(NSA_W)]
    args = [x, mix_m, mix_n]
    if has_override:
        in_specs.append(row(2 * D_FF))
        args.append(override)
    in_specs += [const(w.shape) for w in wts]
    args += list(wts)
    y, tail = pl.pallas_call(
        functools.partial(_ffn_body, tm=tm, has_override=has_override),
        grid=(bsz, nt),
        in_specs=in_specs,
        out_specs=[row(D_MODEL), pl.BlockSpec((1, 1, tail_rows, 2 * D_FF), lambda b, j: (b, j, 0, 0))],
        out_shape=[jax.ShapeDtypeStruct((bsz, t_len, D_MODEL), F32),
                   jax.ShapeDtypeStruct((bsz, nt, tail_rows, 2 * D_FF), F32)],
        scratch_shapes=[pltpu.VMEM((8, 2 * D_FF), F32), pltpu.VMEM((8 + tm, FF_CHUNK), F32)],
        compiler_params=pltpu.CompilerParams(dimension_semantics=("arbitrary", "arbitrary"),
                                             vmem_limit_bytes=VMEM_LIMIT),
        name="ffn",
    )(*args)
    return y, tail


def _permute_in_proj(w_in, b_in):
    gates_a = slice(2 * MLSTM_W + 2 * MLSTM_W, 2 * MLSTM_W + 2 * MLSTM_W + 2 * N_MLSTM_HEADS)
    nsa0 = gates_a.stop
    nsa1 = nsa0 + NSA_W + 6 * KV_W

    def perm(a):
        pad = IN_PAD - a.shape[-1]
        parts = [a[..., :gates_a.start], a[..., nsa0:nsa1], a[..., gates_a], a[..., nsa1:]]
        parts.append(jnp.zeros(a.shape[:-1] + (pad,), a.dtype))
        return jnp.concatenate(parts, axis=-1)

    return perm(w_in).astype(BF16), perm(b_in).reshape(1, IN_PAD)


def _front_pad(a, n):
    return jnp.pad(a, ((0, 0), (n, 0), (0, 0)))


def kernel(x_prompt, x_sample, cache_nsa_kv, cache_win_kv, state_mlstm_c, state_mlstm_n, state_mlstm_m,
           state_mlstm_conv, state_ffn_conv, page_table,
           w_in, b_in, mlstm_conv_w, mlstm_conv_b, mlstm_norm_w, nsa_pool_w, rel_bias, w_out,
           ln1_w, ln1_b, w_up, b_up, ffn_conv_w, ffn_conv_b, w_down, b_down, ln2_w, ln2_b):
    bsz, t_len, _ = x_prompt.shape
    dec_b, n_new, _ = x_sample.shape
    assert n_new == N_NEW and cache_win_kv.shape[1] == WINDOW

    w_p, b_p = _permute_in_proj(w_in, b_in)
    pool_w2 = jnp.concatenate([jnp.repeat(nsa_pool_w[0], HEAD_DIM, axis=-1),
                               jnp.repeat(nsa_pool_w[1], HEAD_DIM, axis=-1)], axis=-1)
    row = lambda v: v.reshape(1, -1)
    ffn_w = (w_out[:MLSTM_W].astype(BF16), w_out[MLSTM_W:].astype(BF16), row(ln1_w), row(ln1_b),
             w_up.astype(BF16), row(b_up), ffn_conv_w, row(ffn_conv_b), w_down.astype(BF16), row(b_down),
             row(ln2_w), row(ln2_b))

    zqk, zv, zo, nq, kv, win, zg, kv_t = _proj_in(x_prompt.reshape(bsz * t_len, D_MODEL), w_p, b_p, seq_len=t_len)
    seq3 = lambda a: a.reshape(bsz, t_len, a.shape[-1])
    zqk, zv, zo, nq, kv, win, zg = map(seq3, (zqk, zv, zo, nq, kv, win, zg))
    mix_m, c_p, n_p, m_p = _mlstm(zqk, zv, zo, zg, None, mlstm_conv_w, mlstm_conv_b, mlstm_norm_w, 0)
    mix_n = _nsa_prompt(nq, kv, win, zg, pool_w2, rel_bias)
    y_p, tail_p = _ffn(x_prompt, mix_m, mix_n, None, ffn_w)
    kv_p = jnp.transpose(kv_t.reshape(bsz, 4, N_KV_HEADS, HEAD_DIM, t_len), (0, 4, 1, 2, 3))
    win_p = win[:, t_len - min(WINDOW, t_len):].reshape(bsz, -1, 2, N_KV_HEADS, HEAD_DIM)
    mconv_p = zqk[:, t_len - (MLSTM_CONV - 1):]
    fconv_p = tail_p[:, -1, 8 - (FFN_CONV - 1):]

    pad = SEQ_PAD - N_NEW
    xs = _front_pad(x_sample, pad)
    szqk, szv, szo, snq, skv, swin, szg = _proj_in(xs.reshape(dec_b * SEQ_PAD, D_MODEL), w_p, b_p)
    sseq = lambda a: a.reshape(dec_b, SEQ_PAD, a.shape[-1])
    szqk, szv, szo, snq, skv, swin, szg = map(sseq, (szqk, szv, szo, snq, skv, swin, szg))
    lead = MLSTM_CHUNK - N_NEW
    new = lambda a: a[:, pad:]
    m_qk = jnp.concatenate([jnp.zeros((dec_b, lead - (MLSTM_CONV - 1), 2 * MLSTM_W), F32),
                            state_mlstm_conv.astype(F32), new(szqk)], axis=1)
    mix_ms, c_s, n_s, m_s = _mlstm(m_qk, _front_pad(new(szv), lead), _front_pad(new(szo), lead),
                                   _front_pad(new(szg), lead),
                                   (state_mlstm_c, state_mlstm_n, state_mlstm_m),
                                   mlstm_conv_w, mlstm_conv_b, mlstm_norm_w, lead)
    mix_ms = mix_ms[:, MLSTM_CHUNK - SEQ_PAD:]
    cache3 = jnp.transpose(cache_nsa_kv, (0, 2, 3, 4, 1)).reshape(cache_nsa_kv.shape[0], 4 * KV_W, PAGE_SIZE)
    pool_wt = jnp.tile(jnp.repeat(jnp.transpose(nsa_pool_w, (0, 2, 1)).reshape(2 * N_KV_HEADS, CMP_BLOCK),
                                  HEAD_DIM, axis=0), (1, PAGE_SIZE // CMP_BLOCK))
    o_c, ids_rows = _nsa_cmp(page_table, rel_bias, snq, pool_wt, cache3)
    ids = jnp.transpose(ids_rows[:, :N_PICK, :N_KV_HEADS * SEQ_PAD], (0, 2, 1))
    ids = ids.reshape(dec_b, N_KV_HEADS, SEQ_PAD, N_PICK)[:, :, pad:]
    win_cache = jnp.transpose(cache_win_kv, (0, 2, 3, 4, 1)).reshape(dec_b, 2 * KV_W, WINDOW)
    mix_ns = _nsa_sel(page_table, ids, ids_rows, rel_bias, snq, skv, win_cache, swin, szg, o_c, cache3)
    override = jnp.concatenate([jnp.zeros((dec_b, pad - (FFN_CONV - 1), 2 * D_FF), F32),
                                state_ffn_conv.astype(F32),
                                jnp.zeros((dec_b, N_NEW, 2 * D_FF), F32)], axis=1)
    flat = lambda a: a.reshape(1, dec_b * SEQ_PAD, a.shape[-1])
    y_s8, u_s = _ffn(flat(xs), flat(mix_ms), flat(mix_ns), flat(override), ffn_w)
    y_s = y_s8.reshape(dec_b, SEQ_PAD, D_MODEL)[:, pad:]
    kv_s = new(skv).reshape(dec_b, N_NEW, 4, N_KV_HEADS, HEAD_DIM)
    win_s = jnp.concatenate([cache_win_kv[:, N_NEW:], new(swin).reshape(dec_b, N_NEW, 2, N_KV_HEADS, HEAD_DIM)],
                            axis=1)
    mconv_s = szqk[:, SEQ_PAD - (MLSTM_CONV - 1):]
    fconv_s = u_s.reshape(dec_b, SEQ_PAD, 2 * D_FF)[:, SEQ_PAD - (FFN_CONV - 1):]
    return (y_p, y_s, kv_p, kv_s, win_p, win_s, c_p, c_s, n_p, n_s, m_p, m_s, mconv_p, mconv_s, fconv_p, fconv_s)
```

```python
import functools
import math

import jax
import jax.numpy as jnp
from jax import lax
from jax.experimental import pallas as pl
from jax.experimental.pallas import tpu as pltpu

F32 = jnp.float32
BF16 = jnp.bfloat16
I32 = jnp.int32

D_MODEL = 1024
HEAD_DIM = 64
N_MLSTM_HEADS = 8
N_NSA_HEADS = 8
N_KV_HEADS = 2
GQ = N_NSA_HEADS // N_KV_HEADS
MLSTM_W = N_MLSTM_HEADS * HEAD_DIM
NSA_W = N_NSA_HEADS * HEAD_DIM
KV_W = N_KV_HEADS * HEAD_DIM
MLSTM_CHUNK = 64
MLSTM_CONV = 4
CMP_BLOCK = 32
SEL_BLOCK = 64
SEL_TOPK = 16
WINDOW = 512
NSA_QBLOCK = 64
N_BUCKETS = 32
MAX_DISTANCE = 128
D_FF = 2688
FFN_CONV = 3
DEPTH = 1
ALPHA = (2.0 * DEPTH) ** 0.25
LN_EPS = 1e-5
PAGE_SIZE = 128

LANES = 128
VMEM_LIMIT = 56 * 1024 * 1024
NEG_INF = float("-inf")

SEQ_PAD = 8
N_NEW = 4
N_QROWS = N_NSA_HEADS * SEQ_PAD


def _bucket_thresholds():
    max_exact = N_BUCKETS // 2
    thr = list(range(max_exact + 1))
    for k in range(1, N_BUCKETS - max_exact):
        thr.append(int(math.ceil(max_exact * (MAX_DISTANCE / max_exact) ** (k / (N_BUCKETS - max_exact)))))
    return tuple(thr)


BUCKET_THR = _bucket_thresholds()
FAR_DIST = BUCKET_THR[N_BUCKETS - 1]


def _dot(a, b, precision=None):
    return jnp.dot(a, b, preferred_element_type=F32, precision=precision)


def _dot_nt(a, b, precision=None):
    return lax.dot_general(a, b, (((1,), (1,)), ((), ())), preferred_element_type=F32, precision=precision)


def _dot_tn(a, b, precision=None):
    return lax.dot_general(a, b, (((0,), (0,)), ((), ())), preferred_element_type=F32, precision=precision)


def _layer_norm(x, w, b):
    mu = jnp.mean(x, -1, keepdims=True)
    xc = x - mu
    var = jnp.mean(xc * xc, -1, keepdims=True)
    return xc * lax.rsqrt(var + LN_EPS) * w + b


def _log_sigmoid(x):
    return jnp.minimum(x, 0.0) - jnp.log1p(jnp.exp(-jnp.abs(x)))


def _bias_from_dist(dist, rb_ref, head):
    acc = jnp.full(dist.shape, rb_ref[0, head], F32)
    for k in range(1, N_BUCKETS):
        acc = jnp.where(dist >= BUCKET_THR[k], rb_ref[k, head], acc)
    return acc


def _masked_softmax(s):
    mx = jnp.max(s, -1, keepdims=True)
    mx = jnp.where(mx > NEG_INF, mx, 0.0)
    e = jnp.exp(s - mx)
    den = jnp.sum(e, -1, keepdims=True)
    return e / jnp.maximum(den, 1e-30)


_C_QK, _C_V, _C_O, _C_NQ, _C_KV, _C_WIN, _C_G = 0, 1024, 1536, 2048, 2560, 3072, 3328
IN_PAD = 3456
G_IGATE, G_FGATE, G_NSA = 0, N_MLSTM_HEADS, 2 * N_MLSTM_HEADS


def _proj_in_body(*refs, kv_transposed):
    if kv_transposed:
        x_ref, w_ref, b_ref, wkvt_ref, bkv_ref, qk_ref, v_ref, o_ref, nq_ref, kv_ref, win_ref, g_ref, kvt_ref = refs
    else:
        x_ref, w_ref, b_ref, qk_ref, v_ref, o_ref, nq_ref, kv_ref, win_ref, g_ref = refs
    xb = x_ref[...].astype(BF16)
    z = _dot(xb, w_ref[...]) + b_ref[...]
    qk_ref[...] = z[:, _C_QK:_C_V]
    v_ref[...] = z[:, _C_V:_C_O]
    o_ref[...] = z[:, _C_O:_C_NQ]
    nq_ref[...] = z[:, _C_NQ:_C_KV]
    kv_ref[...] = z[:, _C_KV:_C_WIN]
    win_ref[...] = z[:, _C_WIN:_C_G]
    g_ref[...] = z[:, _C_G:IN_PAD]
    if kv_transposed:
        kvt_ref[0] = _dot_nt(wkvt_ref[...], xb) + bkv_ref[...]


def _proj_in(x2d, w_p, b_p, seq_len=None):
    rows = x2d.shape[0]
    tm = min(rows, 512)
    widths = (1024, 512, 512, 512, 512, 256, 128)
    const = lambda i: (0, 0)
    const_spec = lambda shape: pl.BlockSpec(shape, const, pipeline_mode=pl.Buffered(1))
    in_specs = [pl.BlockSpec((tm, D_MODEL), lambda i: (i, 0)), const_spec((D_MODEL, IN_PAD)), const_spec((1, IN_PAD))]
    args = [x2d, w_p, b_p]
    out_specs = [pl.BlockSpec((tm, w), lambda i: (i, 0)) for w in widths]
    out_shape = [jax.ShapeDtypeStruct((rows, w), F32) for w in widths]
    if seq_len is not None:
        tiles = seq_len // tm
        in_specs += [const_spec((4 * KV_W, D_MODEL)), const_spec((4 * KV_W, 1))]
        args += [jnp.transpose(w_p[:, _C_KV:_C_WIN]), b_p[0, _C_KV:_C_WIN].reshape(4 * KV_W, 1)]
        out_specs.append(pl.BlockSpec((1, 4 * KV_W, tm), lambda i: (i // tiles, 0, i % tiles)))
        out_shape.append(jax.ShapeDtypeStruct((rows // seq_len, 4 * KV_W, seq_len), F32))
    return pl.pallas_call(
        functools.partial(_proj_in_body, kv_transposed=seq_len is not None),
        grid=(rows // tm,),
        in_specs=in_specs,
        out_specs=out_specs,
        out_shape=out_shape,
        compiler_params=pltpu.CompilerParams(dimension_semantics=("arbitrary",),
                                             vmem_limit_bytes=VMEM_LIMIT),
        name="proj_in",
    )(*args)


N_PAIRS = N_MLSTM_HEADS // 2
GATE_LANE0 = N_MLSTM_HEADS
MLSTM_SEQS_PER_STEP = 4


def _dot_split(x, onehot_bf16, pieces):
    acc = None
    rem = x
    for n in range(pieces):
        part = rem.astype(BF16)
        d = _dot(part, onehot_bf16)
        acc = d if acc is None else acc + d
        if n + 1 < pieces:
            rem = rem - part.astype(F32)
    return acc


def _mlstm_body(*refs, tb, nbg, n_dummy, has_state):
    if has_state:
        (zqk_ref, zv_ref, zo_ref, zg_ref, c0_ref, n0_ref, m0_ref, cw_ref, cb_ref, nw_ref,
         mix_ref, c_out, n_out, m_out, ubuf, qk_sc, ct_sc, n_sc, m_sc) = refs
    else:
        (zqk_ref, zv_ref, zo_ref, zg_ref, cw_ref, cb_ref, nw_ref,
         mix_ref, c_out, n_out, m_out, ubuf, qk_sc, ct_sc, n_sc, m_sc) = refs
    L = MLSTM_CHUNK
    D = HEAD_DIM
    j = pl.program_id(1)
    hi = lax.Precision.HIGHEST
    head_lanes = slice(GATE_LANE0, GATE_LANE0 + N_MLSTM_HEADS)
    pairs = [(bi, p) for bi in range(nbg) for p in range(N_PAIRS)]

    @pl.when(j == 0)
    def _init():
        ubuf[:, 0:8, :] = jnp.zeros((nbg, 8, 2 * MLSTM_W), F32)
        ct_sc[...] = jnp.zeros(ct_sc.shape, F32)
        n_sc[...] = jnp.zeros(n_sc.shape, F32)
        m_sc[...] = jnp.zeros(m_sc.shape, F32)
        if has_state:
            for bi in range(nbg):
                for h in range(N_MLSTM_HEADS):
                    p, hh = divmod(h, 2)
                    hs = slice(hh * D, (hh + 1) * D)
                    ct_sc[bi, p, hs, hs] = c0_ref[bi, h]
                    n_sc[bi, p, :, hs] = n0_ref[bi, h:h + 1, :]
                m_sc[bi, :, head_lanes] = m0_ref[bi]

    for bi in range(nbg):
        u = zqk_ref[bi]
        ubuf[bi, 8:8 + tb, :] = u
        conv = cb_ref[...] + ubuf[bi, 5:5 + tb, :] * cw_ref[0:1, :]
        conv = conv + ubuf[bi, 6:6 + tb, :] * cw_ref[1:2, :]
        conv = conv + ubuf[bi, 7:7 + tb, :] * cw_ref[2:3, :]
        conv = conv + u * cw_ref[3:4, :]
        ubuf[bi, 5:8, :] = u[tb - 3:tb, :]
        qk_sc[bi] = conv * jax.nn.sigmoid(conv)

    tri = (lax.broadcasted_iota(I32, (L, L), 1) <= lax.broadcasted_iota(I32, (L, L), 0)).astype(F32)
    row_l = lax.broadcasted_iota(I32, (L, LANES), 0)
    lane = lax.broadcasted_iota(I32, (L, LANES), 1)
    low = lane < D
    key_tok = lane % D
    causal_pair = key_tok <= row_l
    eye_pair = key_tok == row_l
    spread = (lax.broadcasted_iota(I32, (LANES, MLSTM_W), 0) - GATE_LANE0
              == lax.broadcasted_iota(I32, (LANES, MLSTM_W), 1) // D).astype(BF16)
    same_head = (lax.broadcasted_iota(I32, (LANES, LANES), 0) // D
                 == lax.broadcasted_iota(I32, (LANES, LANES), 1) // D)
    ones_bd = same_head.astype(BF16)
    slab_rows = 3 * L + 8

    def chunk(ci, carry):
        r0 = pl.multiple_of(ci * L, L)
        rows = pl.ds(r0, L)
        gates = []
        for bi in range(nbg):
            g = zg_ref[bi, rows, :]
            if n_dummy:
                dummy = (row_l + ci * L) < n_dummy
                g = jnp.where(dummy, jnp.where(lane < G_FGATE, -1e30, 1e30), g)
            gates.append(g)
        b_all = _dot(tri, jnp.concatenate([_log_sigmoid(g) for g in gates], axis=1), hi)
        stacks = []
        for bi in range(nbg):
            b = b_all[:, bi * LANES:(bi + 1) * LANES]
            r = pltpu.roll(gates[bi], GATE_LANE0 - G_IGATE, 1) - b
            cm = r
            for sh in (1, 2, 4, 8, 16, 32):
                cm = jnp.where(row_l >= sh, jnp.maximum(cm, pltpu.roll(cm, sh, 0)), cm)
            m_st = m_sc[bi]
            mx = jnp.maximum(m_st, cm)
            mx_end = mx[L - 1:L, :]
            m_sc[bi] = b[L - 1:L, :] + mx_end
            stacks += [mx, r, b + mx, m_st, mx_end, jnp.zeros((6, LANES), F32)]
        x_all = _dot_split(jnp.concatenate(stacks, axis=0), spread, 2)

        work = []
        den_terms = []
        for bi, p in pairs:
            cs = slice(p * LANES, (p + 1) * LANES)
            x0 = bi * slab_rows
            mx_p, rs_p, mt_p = x_all[x0:x0 + L, cs], x_all[x0 + L:x0 + 2 * L, cs], x_all[x0 + 2 * L:x0 + 3 * L, cs]
            mst_p, mxe_p = x_all[x0 + 3 * L:x0 + 3 * L + 1, cs], x_all[x0 + 3 * L + 1:x0 + 3 * L + 2, cs]
            wi_p = jnp.exp(mst_p - mx_p)
            fl_p = jnp.exp(-mt_p)
            wk_p = jnp.exp(rs_p - mxe_p)
            sc_p = jnp.exp(mst_p - mxe_p)
            r_row = jnp.sum(jnp.where(eye_pair, rs_p, 0.0), axis=0, keepdims=True)
            w_intra = jnp.exp(jnp.where(causal_pair, r_row - mx_p, NEG_INF))
            q_p = qk_sc[bi, rows, cs]
            k_p = qk_sc[bi, rows, MLSTM_W + p * LANES:MLSTM_W + (p + 1) * LANES] * (D ** -0.5)
            v_p = zv_ref[bi, rows, cs]
            qb = q_p.astype(BF16)
            k_bd = jnp.concatenate([jnp.where(low, k_p, 0.0), jnp.where(low, 0.0, k_p)], axis=0)
            v_bd = jnp.concatenate([jnp.where(low, v_p, 0.0), jnp.where(low, 0.0, v_p)], axis=0).astype(BF16)
            c_t = ct_sc[bi, p]
            both = _dot_nt(qb, jnp.concatenate([k_bd, c_t], axis=0).astype(BF16))
            a = w_intra * both[:, 0:LANES]
            num = _dot(a.astype(BF16), v_bd) + both[:, LANES:2 * LANES] * wi_p
            n_row = n_sc[bi, p]
            den_terms.append(a + wi_p * (q_p * n_row))
            work.append((num, fl_p))
            kw = k_p * wk_p
            upd = _dot_tn(v_p.astype(BF16), kw.astype(BF16))
            ct_sc[bi, p] = sc_p * c_t + jnp.where(same_head, upd, 0.0)
            n_sc[bi, p] = sc_p * n_row + jnp.sum(kw, axis=0, keepdims=True)
        den_all = _dot_split(jnp.concatenate(den_terms, axis=0), ones_bd, 2)

        hhs = []
        for n, (num, fl_p) in enumerate(work):
            hhs.append(num / jnp.maximum(jnp.abs(den_all[n * L:(n + 1) * L, :]), fl_p))
        mu_all = _dot_split(jnp.concatenate(hhs, axis=0), ones_bd, 1) * (1.0 / D)
        hcs = [hh - mu_all[n * L:(n + 1) * L, :] for n, hh in enumerate(hhs)]
        var_all = _dot_split(jnp.concatenate([hc * hc for hc in hcs], axis=0), ones_bd, 1) * (1.0 / D)
        for n, (bi, p) in enumerate(pairs):
            cs = slice(p * LANES, (p + 1) * LANES)
            hn = hcs[n] * lax.rsqrt(var_all[n * L:(n + 1) * L, :] + LN_EPS)
            og = jax.nn.sigmoid(zo_ref[bi, rows, cs])
            mix_ref[bi, rows, cs] = og * (hn * nw_ref[:, cs])
        return carry

    lax.fori_loop(0, tb // L, chunk, 0, unroll=2 if (tb // L) % 2 == 0 else 1)

    @pl.when(j == pl.num_programs(1) - 1)
    def _final():
        for bi in range(nbg):
            for h in range(N_MLSTM_HEADS):
                p, hh = divmod(h, 2)
                hs = slice(hh * D, (hh + 1) * D)
                c_out[bi, h] = ct_sc[bi, p, hs, hs]
                n_out[bi, h:h + 1, :] = n_sc[bi, p, :, hs]
            m_out[bi] = m_sc[bi, :, head_lanes]


def _mlstm(zqk, zv, zo, zg, state, conv_w, conv_b, norm_w, n_dummy):
    bsz, t_len, _ = zqk.shape
    tb = min(t_len, 256)
    nbg = MLSTM_SEQS_PER_STEP
    assert bsz % nbg == 0
    has_state = state is not None
    row = lambda w: pl.BlockSpec((nbg, tb, w), lambda b, j: (b, j, 0))
    per_b = lambda shape: pl.BlockSpec((nbg,) + shape, lambda b, j: (b,) + (0,) * len(shape))
    const = lambda shape: pl.BlockSpec(shape, lambda b, j: (0,) * len(shape))
    in_specs = [row(2 * MLSTM_W), row(MLSTM_W), row(MLSTM_W), row(LANES)]
    args = [zqk, zv, zo, zg]
    if has_state:
        c0, n0, m0 = state
        in_specs += [per_b((N_MLSTM_HEADS, HEAD_DIM, HEAD_DIM)), per_b((N_MLSTM_HEADS, HEAD_DIM)),
                     per_b((1, N_MLSTM_HEADS))]
        args += [c0, n0, m0.reshape(bsz, 1, N_MLSTM_HEADS)]
    in_specs += [const((MLSTM_CONV, 2 * MLSTM_W)), const((1, 2 * MLSTM_W)), const((1, MLSTM_W))]
    args += [conv_w, conv_b.reshape(1, -1), norm_w.reshape(1, -1)]
    mix, c_new, n_new, m_new = pl.pallas_call(
        functools.partial(_mlstm_body, tb=tb, nbg=nbg, n_dummy=n_dummy, has_state=has_state),
        grid=(bsz // nbg, t_len // tb),
        in_specs=in_specs,
        out_specs=[row(MLSTM_W), per_b((N_MLSTM_HEADS, HEAD_DIM, HEAD_DIM)),
                   per_b((N_MLSTM_HEADS, HEAD_DIM)), per_b((1, N_MLSTM_HEADS))],
        out_shape=[jax.ShapeDtypeStruct((bsz, t_len, MLSTM_W), F32),
                   jax.ShapeDtypeStruct((bsz, N_MLSTM_HEADS, HEAD_DIM, HEAD_DIM), F32),
                   jax.ShapeDtypeStruct((bsz, N_MLSTM_HEADS, HEAD_DIM), F32),
                   jax.ShapeDtypeStruct((bsz, 1, N_MLSTM_HEADS), F32)],
        scratch_shapes=[pltpu.VMEM((nbg, 8 + tb, 2 * MLSTM_W), F32),
                        pltpu.VMEM((nbg, tb, 2 * MLSTM_W), F32),
                        pltpu.VMEM((nbg, N_PAIRS, LANES, LANES), F32),
                        pltpu.VMEM((nbg, N_PAIRS, 1, LANES), F32),
                        pltpu.VMEM((nbg, 1, LANES), F32)],
        compiler_params=pltpu.CompilerParams(dimension_semantics=("arbitrary", "arbitrary"),
                                             vmem_limit_bytes=VMEM_LIMIT),
        name="mlstm",
    )(*args)
    return mix, c_new, n_new, m_new.reshape(bsz, N_MLSTM_HEADS)


def _block_diag_queries(x, n_rows, heads=tuple(range(N_NSA_HEADS)), scale=HEAD_DIM ** -0.5):
    lane = lax.broadcasted_iota(I32, (n_rows, LANES), 1)
    low = lane < HEAD_DIM
    pieces = []
    for h in heads:
        pair = x[:, (h // 2) * LANES:(h // 2 + 1) * LANES]
        src_low = h % 2 == 0
        dst_low = h // GQ == 0
        if src_low != dst_low:
            pair = pltpu.roll(pair, HEAD_DIM, 1)
        pieces.append(jnp.where(low if dst_low else ~low, pair, 0.0))
    return (jnp.concatenate(pieces, axis=0) * scale).astype(BF16)


def _group_lanes(o, g):
    return o[:, g * HEAD_DIM:(g + 1) * HEAD_DIM]


def _gated_mix(gates, o_c, o_s, o_w, h, n_rows):
    g = h // GQ
    rs = slice(h * n_rows, (h + 1) * n_rows)
    col = lambda branch: gates[:, G_NSA + branch * N_NSA_HEADS + h:G_NSA + branch * N_NSA_HEADS + h + 1]
    return (col(0) * _group_lanes(o_c[rs], g) + col(1) * _group_lanes(o_s[rs], g)
            + col(2) * _group_lanes(o_w[rs], g))


SEL_TILE = 256
SEL_TILE_BLOCKS = SEL_TILE // SEL_BLOCK
FAR_GROUP = 2
Q_PER_STEP = 4
WIN_BAND = WINDOW + NSA_QBLOCK
N_COLS = N_NSA_HEADS * NSA_QBLOCK
PROMPT_HEADS = tuple(GQ * (jj % N_KV_HEADS) + jj // N_KV_HEADS for jj in range(N_NSA_HEADS))
SEL_REL0 = 2 * SEL_TILE - SEL_BLOCK
SEL_BIAS_ROWS = SEL_REL0 + SEL_TILE


LOG2E = math.log2(math.e)


def _softmax2_rows(s):
    mx = jnp.max(s, 0, keepdims=True)
    mx = jnp.where(mx > NEG_INF, mx, 0.0)
    e = jnp.exp2(s - mx)
    return e / jnp.maximum(jnp.sum(e, 0, keepdims=True), 1e-30)


def _nsa_prompt_body(rb_ref, nq_ref, nqn_ref, kv_ref, win_ref, zg_ref, pw_ref, out_ref,
                     ks_sc, vst_sc, wk_sc, wv_sc, kc_sc, pv_sc, vct_sc, bs_sc, bw_sc, fc_sc,
                     m_sc, l_sc, acc_sc, ch_sc, oc_sc, fs_sc, ft_sc, *, t_len):
    QB = NSA_QBLOCK
    n_cmp = t_len // CMP_BLOCK
    n_sel = t_len // SEL_BLOCK
    pid = pl.program_id(1)
    hi = lax.Precision.HIGHEST

    def cmp_scores(x_q):
        return _dot_nt(kc_sc[...], _block_diag_queries(x_q, QB, PROMPT_HEADS, HEAD_DIM ** -0.5 * LOG2E))

    def cmp_and_select(raw, qi, out_buf):
        s_c = raw + pltpu.roll(fc_sc[...], 2 * qi, 0)
        c_i = lax.broadcasted_iota(I32, (n_cmp, N_COLS), 0)
        q_i = lax.broadcasted_iota(I32, (n_cmp, N_COLS), 1) % QB
        p_c = _softmax2_rows(jnp.where(c_i * CMP_BLOCK + (CMP_BLOCK - 1) <= qi * QB + q_i, s_c, NEG_INF))
        oc_sc[out_buf] = _dot(vct_sc[...], p_c.astype(BF16))
        p_sum = p_c[:, 0:LANES]
        for j in range(1, GQ):
            p_sum = p_sum + p_c[:, j * LANES:(j + 1) * LANES]
        ratio = SEL_BLOCK // CMP_BLOCK
        pair = (lax.broadcasted_iota(I32, (n_sel, n_cmp), 1) // ratio
                == lax.broadcasted_iota(I32, (n_sel, n_cmp), 0)).astype(F32)
        imp = _dot(pair, p_sum, hi)
        blk = lax.broadcasted_iota(I32, (n_sel, LANES), 0)
        score = jnp.where(blk == qi, jnp.inf, jnp.where(blk < qi, imp, NEG_INF))

        def pick(_, carry):
            work, chosen = carry
            best = jnp.max(work, 0, keepdims=True)
            first = jnp.min(jnp.where(work == best, blk, n_sel), 0, keepdims=True)
            hit = blk == first
            return jnp.where(hit, NEG_INF, work), jnp.where(hit, 1.0, chosen)

        _, chosen = lax.fori_loop(0, SEL_TOPK, pick, (score, jnp.zeros((n_sel, LANES), F32)), unroll=True)
        ch_sc[out_buf, 0:8, :] = jnp.full((8, LANES), NEG_INF, F32)
        ch_sc[out_buf, 8:8 + n_sel, :] = jnp.where((score > NEG_INF) & (chosen > 0.5), 0.0, NEG_INF)

    @pl.when(pid == 0)
    def _init():
        ks_sc[0:SEL_TILE, :] = jnp.zeros((SEL_TILE, LANES), BF16)
        vst_sc[0] = jnp.zeros((LANES, SEL_TILE), BF16)
        wk_sc[0:WINDOW, :] = jnp.zeros((WINDOW, LANES), BF16)
        wv_sc[0:WINDOW, :] = jnp.zeros((WINDOW, LANES), BF16)
        rows_per = 2 * SEL_TILE
        blocks_per = rows_per // CMP_BLOCK

        def fill(c, carry):
            r0 = pl.multiple_of(c * rows_per, rows_per)
            kv = kv_ref[0, pl.ds(r0, rows_per), :]
            ks_sc[pl.ds(SEL_TILE + r0, rows_per), :] = kv[:, 2 * LANES:3 * LANES].astype(BF16)
            for half in range(2):
                v_t = kv[half * SEL_TILE:(half + 1) * SEL_TILE, 3 * LANES:4 * LANES]
                vst_sc[2 * c + half + 1] = jnp.transpose(v_t).astype(BF16)
            w = win_ref[0, pl.ds(r0, rows_per), :]
            wk_sc[pl.ds(WINDOW + r0, rows_per), :] = w[:, 0:LANES].astype(BF16)
            wv_sc[pl.ds(WINDOW + r0, rows_per), :] = w[:, LANES:2 * LANES].astype(BF16)
            pooled = jnp.sum(kv[:, 0:2 * LANES].reshape(blocks_per, CMP_BLOCK, 2 * LANES) * pw_ref[...][None],
                             axis=1)
            c0 = pl.multiple_of(c * blocks_per, blocks_per)
            kc_sc[pl.ds(c0, blocks_per), :] = pooled[:, 0:LANES].astype(BF16)
            pv_sc[pl.ds(c0, blocks_per), :] = pooled[:, LANES:2 * LANES]
            return carry

        lax.fori_loop(0, t_len // rows_per, fill, 0)
        vct_sc[...] = jnp.transpose(pv_sc[...]).astype(BF16)

        for jj, h in enumerate(PROMPT_HEADS):
            cs = slice(jj * QB, (jj + 1) * QB)
            u_i = lax.broadcasted_iota(I32, (SEL_BIAS_ROWS, QB), 0)
            q_i = lax.broadcasted_iota(I32, (SEL_BIAS_ROWS, QB), 1)
            dist = q_i - (u_i - SEL_REL0)
            near = (_bias_from_dist(dist, rb_ref, h) - rb_ref[N_BUCKETS - 1, h]) * LOG2E
            bs_sc[:, cs] = jnp.where(dist >= 0, near, NEG_INF)
            s_i = lax.broadcasted_iota(I32, (WIN_BAND, QB), 0)
            q_i = lax.broadcasted_iota(I32, (WIN_BAND, QB), 1)
            dist = q_i + WINDOW - s_i
            bw_sc[:, cs] = jnp.where((dist >= 0) & (dist <= WINDOW), _bias_from_dist(dist, rb_ref, h) * LOG2E,
                                     NEG_INF)
            c_i = lax.broadcasted_iota(I32, (n_cmp, QB), 0)
            q_i = lax.broadcasted_iota(I32, (n_cmp, QB), 1)
            rel = jnp.where(c_i < n_cmp // 2, c_i, c_i - n_cmp)
            dist = q_i - CMP_BLOCK * rel - (CMP_BLOCK - 1)
            dist = jnp.where((c_i >= 2) & (c_i < n_cmp // 2), FAR_DIST, dist)
            fc_sc[:, cs] = _bias_from_dist(dist, rb_ref, h) * LOG2E

        cmp_and_select(cmp_scores(nq_ref[0, 0:QB, :]), 0, 0)

    def q_block(t):
        i = Q_PER_STEP * pid + t
        rows = slice(t * QB, (t + 1) * QB)
        x_q = nq_ref[0, rows, :]
        x_next = nq_ref[0, (t + 1) * QB:(t + 2) * QB, :] if t + 1 < Q_PER_STEP else nqn_ref[0, 0:QB, :]
        buf = t % 2
        o_c = oc_sc[buf]
        qbd = _block_diag_queries(x_q, QB, PROMPT_HEADS, HEAD_DIM ** -0.5 * LOG2E)
        last = i // SEL_TILE_BLOCKS

        def tile_raw(slot):
            return _dot_nt(ks_sc[pl.ds(pl.multiple_of(slot * SEL_TILE, SEL_TILE), SEL_TILE), :], qbd)

        raw_next = cmp_scores(x_next)
        w0 = pl.multiple_of(i * QB, QB)
        raw_w = _dot_nt(wk_sc[pl.ds(w0, WIN_BAND), :], qbd)
        raw_near = [tile_raw(last + 1), tile_raw(last)]

        def tile_scores(slot, bias, s=None):
            if s is None:
                s = tile_raw(slot)
            if bias is not None:
                s = s + bias
            parts = []
            for r in range(SEL_TILE_BLOCKS):
                row = ch_sc[buf, pl.ds(8 - SEL_TILE_BLOCKS + slot * SEL_TILE_BLOCKS + r, 1), :]
                parts.append(s[r * SEL_BLOCK:(r + 1) * SEL_BLOCK, :] + jnp.concatenate([row] * GQ, axis=1))
            return jnp.concatenate(parts, axis=0)

        def group_max(scores):
            return functools.reduce(jnp.maximum, [jnp.max(s, 0, keepdims=True) for s in scores])

        def sel_group(slots, scores, t_max, first):
            m_new = t_max if first else jnp.maximum(m_sc[...], t_max)
            exps = [jnp.exp2(s - m_new) for s in scores]
            l_new = functools.reduce(jnp.add, [jnp.sum(e, 0, keepdims=True) for e in exps])
            acc_new = functools.reduce(jnp.add, [_dot(vst_sc[slot], e.astype(BF16)) for slot, e in zip(slots, exps)])
            if first:
                l_sc[...] = l_new
                acc_sc[...] = acc_new
            else:
                scale = jnp.exp2(m_sc[...] - m_new)
                l_sc[...] = scale * l_sc[...] + l_new
                acc_sc[...] = scale * acc_sc[...] + acc_new
            m_sc[...] = m_new

        def far_slots(p):
            return [jnp.where(FAR_GROUP * p + 1 + n < last, FAR_GROUP * p + 1 + n, 0) for n in range(FAR_GROUP)]

        def far_scores(p):
            scores = [tile_scores(slot, None) for slot in far_slots(p)]
            for n, s in enumerate(scores):
                fs_sc[p % 2, n] = s
            ft_sc[p % 2] = group_max(scores)

        far_scores(0)
        cmp_and_select(raw_next, i + 1, 1 - buf)

        in_seq = lax.broadcasted_iota(I32, (WIN_BAND, N_COLS), 0) >= WINDOW - i * QB
        s_w = jnp.where(in_seq, raw_w + bw_sc[...], NEG_INF)
        e_w = jnp.exp2(s_w - jnp.max(s_w, 0, keepdims=True))
        o_w = (_dot_tn(wv_sc[pl.ds(w0, WIN_BAND), :], e_w.astype(BF16))
               / jnp.maximum(jnp.sum(e_w, 0, keepdims=True), 1e-30))

        u0 = pl.multiple_of(SEL_REL0 - SEL_TILE - SEL_BLOCK * (i % SEL_TILE_BLOCKS), SEL_BLOCK)
        near = [tile_scores(last + 1, bs_sc[pl.ds(u0 + SEL_TILE, SEL_TILE), :], raw_near[0]),
                tile_scores(last, bs_sc[pl.ds(u0, SEL_TILE), :], raw_near[1])]
        sel_group((last + 1, last), near, group_max(near), True)

        def far_group(p, carry):
            sel_group(far_slots(p), [fs_sc[p % 2, n] for n in range(FAR_GROUP)], ft_sc[p % 2], False)
            far_scores(p + 1)
            return carry

        lax.fori_loop(0, (last - 1 + FAR_GROUP - 1) // FAR_GROUP, far_group, 0)
        o_s = acc_sc[...] / jnp.maximum(l_sc[...], 1e-30)

        eye = (lax.broadcasted_iota(I32, (3 * N_NSA_HEADS + 8, LANES), 0) + G_NSA
               == lax.broadcasted_iota(I32, (3 * N_NSA_HEADS + 8, LANES), 1)).astype(F32)
        gates = jax.nn.sigmoid(_dot_nt(eye, zg_ref[0, rows, :], hi))
        pieces = []
        for h in range(N_NSA_HEADS):
            g = h // GQ
            jj = PROMPT_HEADS.index(h)
            part = lambda o: o[g * HEAD_DIM:(g + 1) * HEAD_DIM, jj * QB:(jj + 1) * QB]
            pieces.append(gates[h:h + 1, :] * part(o_c)
                          + gates[N_NSA_HEADS + h:N_NSA_HEADS + h + 1, :] * part(o_s)
                          + gates[2 * N_NSA_HEADS + h:2 * N_NSA_HEADS + h + 1, :] * part(o_w))
        mix_t = jnp.concatenate(pieces, axis=0)
        mix_t = jnp.concatenate([mix_t, jnp.zeros((NSA_W, LANES - QB), F32)], axis=1)
        out_ref[0, rows, :] = jnp.transpose(mix_t)[0:QB, :]

    for t in range(Q_PER_STEP):
        q_block(t)


def _nsa_prompt(nq, kv, win, zg, pool_w2, rel_bias):
    bsz, t_len, _ = nq.shape
    assert t_len // CMP_BLOCK == LANES and t_len % (2 * SEL_TILE) == 0
    n_cmp = t_len // CMP_BLOCK
    per_b = lambda w: pl.BlockSpec((1, t_len, w), lambda b, i: (b, 0, 0))
    step_rows = Q_PER_STEP * NSA_QBLOCK
    blk = lambda w: pl.BlockSpec((1, step_rows, w), lambda b, i: (b, i, 0))
    n_qb = t_len // step_rows
    next_blk = pl.BlockSpec((1, step_rows, NSA_W), lambda b, i: (b, jnp.minimum(i + 1, n_qb - 1), 0))
    return pl.pallas_call(
        functools.partial(_nsa_prompt_body, t_len=t_len),
        grid=(bsz, n_qb),
        in_specs=[pl.BlockSpec(memory_space=pltpu.SMEM),
                  blk(NSA_W), next_blk, per_b(4 * KV_W), per_b(2 * KV_W), blk(LANES),
                  pl.BlockSpec((CMP_BLOCK, 2 * LANES), lambda b, i: (0, 0))],
        out_specs=blk(NSA_W),
        out_shape=jax.ShapeDtypeStruct((bsz, t_len, NSA_W), F32),
        scratch_shapes=[pltpu.VMEM((SEL_TILE + t_len, LANES), BF16),
                        pltpu.VMEM((1 + t_len // SEL_TILE, LANES, SEL_TILE), BF16),
                        pltpu.VMEM((WINDOW + t_len, LANES), BF16),
                        pltpu.VMEM((WINDOW + t_len, LANES), BF16),
                        pltpu.VMEM((n_cmp, LANES), BF16),
                        pltpu.VMEM((n_cmp, LANES), F32),
                        pltpu.VMEM((LANES, n_cmp), BF16),
                        pltpu.VMEM((SEL_BIAS_ROWS, N_COLS), F32),
                        pltpu.VMEM((WIN_BAND, N_COLS), F32),
                        pltpu.VMEM((n_cmp, N_COLS), F32),
                        pltpu.VMEM((1, N_COLS), F32),
                        pltpu.VMEM((1, N_COLS), F32),
                        pltpu.VMEM((LANES, N_COLS), F32),
                        pltpu.VMEM((2, 8 + t_len // SEL_BLOCK, LANES), F32),
                        pltpu.VMEM((2, LANES, N_COLS), F32),
                        pltpu.VMEM((2, FAR_GROUP, SEL_TILE, N_COLS), F32),
                        pltpu.VMEM((2, 1, N_COLS), F32)],
        compiler_params=pltpu.CompilerParams(dimension_semantics=("arbitrary", "arbitrary"),
                                             vmem_limit_bytes=VMEM_LIMIT),
        name="nsa_prompt",
    )(rel_bias, nq, nq, kv, win, zg, pool_w2)


CMP_PAGES = 32
CMP_SLOTS = 6
N_PICK = SEL_TOPK - 1
N_OWNERS = N_KV_HEADS * N_NEW
N_FETCH = N_OWNERS * N_PICK
OWNER_KEYS = N_PICK * PAGE_SIZE
DMA_BURST = 8


def _sample_row_ids():
    r = lax.broadcasted_iota(I32, (N_QROWS, 1), 0)
    return r % SEQ_PAD - (SEQ_PAD - N_NEW), r // SEQ_PAD


def _sample_bias(dist, rb_ref):
    _, head = _sample_row_ids()
    col = lambda k: sum(jnp.where(head == h, rb_ref[k, h], 0.0) for h in range(N_NSA_HEADS))
    acc = jnp.broadcast_to(col(0), dist.shape)
    for k in range(1, N_BUCKETS):
        acc = jnp.where(dist >= BUCKET_THR[k], col(k), acc)
    return acc


def _sample_bias_table(rb_ref):
    assert FAR_DIST < LANES
    return _sample_bias(lax.broadcasted_iota(I32, (N_QROWS, LANES), 1), rb_ref)


def _sample_bias_lookup(dist, table):
    idx = jnp.clip(dist, 0, LANES - 1)
    return jnp.concatenate([jnp.take_along_axis(table, idx[:, c * LANES:(c + 1) * LANES], axis=1)
                            for c in range(dist.shape[1] // LANES)], axis=1)


def _softmax_two(s1, s2):
    mx = jnp.maximum(jnp.max(s1, -1, keepdims=True), jnp.max(s2, -1, keepdims=True))
    mx = jnp.where(mx > NEG_INF, mx, 0.0)
    e1 = jnp.exp(s1 - mx)
    e2 = jnp.exp(s2 - mx)
    den = jnp.maximum(jnp.sum(e1, -1, keepdims=True) + jnp.sum(e2, -1, keepdims=True), 1e-30)
    return e1 / den, e2 / den


def _nsa_cmp_body(pt_ref, rb_ref, nq_ref, pwt_ref, cache_ref, oc_ref, idx_ref,
                  buf, sem, seg_sc, kct_sc, vct_sc, *, n_pages, past_len):
    b = pl.program_id(0)
    nb = pl.num_programs(0)
    n_chunks = n_pages // CMP_PAGES
    chunk_rows = CMP_PAGES * PAGE_SIZE
    chunk_blocks = chunk_rows // CMP_BLOCK
    n_cmp = past_len // CMP_BLOCK
    n_sel_past = past_len // SEL_BLOCK
    n_prob = N_KV_HEADS * SEQ_PAD
    total = nb * n_chunks
    hi = lax.Precision.HIGHEST

    def page_copy(gi, p):
        page = pt_ref[(gi // n_chunks) * n_pages + (gi % n_chunks) * CMP_PAGES + p]
        slot = gi % CMP_SLOTS
        return pltpu.make_async_copy(cache_ref.at[page, pl.ds(0, 2 * LANES), :], buf.at[slot, p], sem.at[slot])

    def start_chunk(gi):
        for p in range(CMP_PAGES):
            page_copy(gi, p).start()

    def wait_chunk(gi):
        for p in range(CMP_PAGES):
            page_copy(gi, p).wait()

    @pl.when(b == 0)
    def _prologue():
        seg_sc[...] = (lax.broadcasted_iota(I32, (chunk_rows, chunk_blocks), 0) // CMP_BLOCK
                       == lax.broadcasted_iota(I32, (chunk_rows, chunk_blocks), 1)).astype(BF16)
        for gi in range(min(CMP_SLOTS - 1, total)):
            start_chunk(gi)

    def chunk_step(c, carry):
        gi = b * n_chunks + c
        ahead = gi + (CMP_SLOTS - 1)

        @pl.when(ahead < total)
        def _prefetch():
            start_chunk(ahead)

        wait_chunk(gi)
        slot = gi % CMP_SLOTS
        y = jnp.concatenate([(buf[slot, p] * pwt_ref[...]).astype(BF16) for p in range(CMP_PAGES)], axis=1)
        pooled = _dot(y, seg_sc[...])
        kct_sc[c] = pooled[0:LANES].astype(BF16)
        vct_sc[c] = pooled[LANES:2 * LANES].astype(BF16)
        return carry

    lax.fori_loop(0, n_chunks, chunk_step, 0)

    qbd = _block_diag_queries(nq_ref[0], SEQ_PAD)
    tok, _ = _sample_row_ids()
    s_c = jnp.concatenate([_dot(qbd, kct_sc[c]) for c in range(n_chunks)], axis=1)
    near0 = n_cmp - LANES
    assert past_len - (SEQ_PAD - N_NEW) - (CMP_BLOCK * (near0 - 1) + CMP_BLOCK - 1) >= FAR_DIST
    dist = past_len + tok - ((lax.broadcasted_iota(I32, (N_QROWS, LANES), 1) + near0) * CMP_BLOCK + CMP_BLOCK - 1)
    near = jnp.where(dist >= 0, s_c[:, near0:] + _sample_bias(dist, rb_ref), NEG_INF)
    far = s_c[:, :near0] + _sample_bias(jnp.full((N_QROWS, 1), FAR_DIST, I32), rb_ref)
    p_c = _masked_softmax(jnp.concatenate([far, near], axis=1))
    o_c = _dot_nt(p_c[:, 0:chunk_blocks].astype(BF16), vct_sc[0])
    for c in range(1, n_chunks):
        o_c = o_c + _dot_nt(p_c[:, c * chunk_blocks:(c + 1) * chunk_blocks].astype(BF16), vct_sc[c])
    oc_ref[0] = o_c

    p_sum = jnp.sum(p_c.reshape(N_KV_HEADS, GQ, SEQ_PAD, n_cmp), axis=1).reshape(n_prob, n_cmp)
    p_sum = jnp.concatenate([p_sum, jnp.zeros((LANES - n_prob, n_cmp), F32)], axis=0)
    ratio = SEL_BLOCK // CMP_BLOCK
    pair = (lax.broadcasted_iota(I32, (n_sel_past, n_cmp), 1) // ratio
            == lax.broadcasted_iota(I32, (n_sel_past, n_cmp), 0)).astype(F32)
    imp = _dot_nt(pair, p_sum, hi)
    blk = lax.broadcasted_iota(I32, (n_sel_past, LANES), 0)
    pick_i = lax.broadcasted_iota(I32, (SEL_TOPK, LANES), 0)

    def pick(k, carry):
        work, ids = carry
        best = jnp.max(work, 0, keepdims=True)
        first = jnp.min(jnp.where(work == best, blk, n_sel_past), 0, keepdims=True)
        return jnp.where(blk == first, NEG_INF, work), jnp.where(pick_i == k, first, ids)

    _, ids = lax.fori_loop(0, N_PICK, pick, (imp, jnp.zeros((SEL_TOPK, LANES), I32)), unroll=True)
    idx_ref[0] = ids


def _nsa_cmp(page_table, rel_bias, nq_s, pool_wt, cache3):
    dec_b, n_pages = page_table.shape
    past_len = n_pages * PAGE_SIZE
    assert n_pages % CMP_PAGES == 0 and past_len // SEL_BLOCK >= N_PICK
    n_chunks = n_pages // CMP_PAGES
    chunk_rows = CMP_PAGES * PAGE_SIZE
    chunk_blocks = chunk_rows // CMP_BLOCK
    assert chunk_blocks == LANES
    grid_spec = pltpu.PrefetchScalarGridSpec(
        num_scalar_prefetch=1,
        grid=(dec_b,),
        in_specs=[pl.BlockSpec(memory_space=pltpu.SMEM),
                  pl.BlockSpec((1, SEQ_PAD, NSA_W), lambda b, pt: (b, 0, 0)),
                  pl.BlockSpec((2 * LANES, PAGE_SIZE), lambda b, pt: (0, 0)),
                  pl.BlockSpec(memory_space=pl.ANY)],
        out_specs=[pl.BlockSpec((1, N_QROWS, LANES), lambda b, pt: (b, 0, 0)),
                   pl.BlockSpec((1, SEL_TOPK, LANES), lambda b, pt: (b, 0, 0))],
        scratch_shapes=[pltpu.VMEM((CMP_SLOTS, CMP_PAGES, 2 * LANES, PAGE_SIZE), F32),
                        pltpu.SemaphoreType.DMA((CMP_SLOTS,)),
                        pltpu.VMEM((chunk_rows, chunk_blocks), BF16),
                        pltpu.VMEM((n_chunks, LANES, chunk_blocks), BF16),
                        pltpu.VMEM((n_chunks, LANES, chunk_blocks), BF16)],
    )
    return pl.pallas_call(
        functools.partial(_nsa_cmp_body, n_pages=n_pages, past_len=past_len),
        grid_spec=grid_spec,
        out_shape=[jax.ShapeDtypeStruct((dec_b, N_QROWS, LANES), F32),
                   jax.ShapeDtypeStruct((dec_b, SEL_TOPK, LANES), I32)],
        compiler_params=pltpu.CompilerParams(dimension_semantics=("arbitrary",),
                                             vmem_limit_bytes=VMEM_LIMIT),
        name="nsa_sample_cmp",
    )(page_table.reshape(-1), rel_bias, nq_s, pool_wt, cache3)


def _nsa_sel_body(pt_ref, ids_ref, rb_ref, nq_ref, kvn_ref, winc_ref, winn_ref, zg_ref, oc_ref, idv_ref,
                  cache_ref, out_ref, buf, sem, *, n_pages, past_len):
    b = pl.program_id(0)
    nb = pl.num_programs(0)
    hi = lax.Precision.HIGHEST

    def block_copy(bb, n, slot):
        blk = ids_ref[bb * N_FETCH + n]
        page = pt_ref[bb * n_pages + blk // 2]
        return pltpu.make_async_copy(cache_ref.at[page, pl.ds(2 * LANES, 2 * LANES), :], buf.at[slot, n],
                                     sem.at[slot])

    def start_all(bb, slot):
        def go(n8, carry):
            for k in range(DMA_BURST):
                block_copy(bb, n8 * DMA_BURST + k, slot).start(priority=k % 2)
            return carry
        lax.fori_loop(0, N_FETCH // DMA_BURST, go, 0)

    def wait_all(bb, slot):
        def go(n, carry):
            block_copy(bb, n, slot).wait()
            return carry
        lax.fori_loop(0, N_FETCH, go, 0, unroll=8)

    slot = b % 2

    @pl.when(b == 0)
    def _prologue():
        start_all(0, 0)

    @pl.when(b + 1 < nb)
    def _prefetch():
        start_all(b + 1, 1 - slot)

    qbd = _block_diag_queries(nq_ref[0], SEQ_PAD)
    tok, head = _sample_row_ids()
    owner = (head // GQ) * N_NEW + tok

    wc = winc_ref[0]
    wn = winn_ref[0]
    r_i = lax.broadcasted_iota(I32, (N_QROWS, WINDOW), 1)
    dist1 = WINDOW + tok - r_i
    bias_table = _sample_bias_table(rb_ref)
    s1 = _dot(qbd, wc[0:LANES, :].astype(BF16)) + _sample_bias_lookup(dist1, bias_table)
    s1 = jnp.where((dist1 >= 0) & (dist1 <= WINDOW), s1, NEG_INF)
    n_i = lax.broadcasted_iota(I32, (N_QROWS, SEQ_PAD), 1) - (SEQ_PAD - N_NEW)
    dist2 = tok - n_i
    new_ok = (n_i >= 0) & (dist2 >= 0)
    bias2 = _sample_bias(dist2, rb_ref)
    s2 = jnp.where(new_ok, _dot_nt(qbd, wn[:, 0:LANES].astype(BF16)) + bias2, NEG_INF)
    p1, p2 = _softmax_two(s1, s2)
    o_w = (_dot_nt(p1.astype(BF16), wc[LANES:2 * LANES, :].astype(BF16))
           + _dot(p2.astype(BF16), wn[:, LANES:2 * LANES].astype(BF16)))

    wait_all(b, slot)

    def owner_tiles(o, r0):
        return jnp.concatenate([buf[slot, o * N_PICK + k, r0:r0 + LANES, :] for k in range(N_PICK)],
                               axis=1).astype(BF16)

    s_f = jnp.full((N_QROWS, OWNER_KEYS), NEG_INF, F32)
    for o in range(N_OWNERS):
        s_f = jnp.where(owner == o, _dot(qbd, owner_tiles(o, 0)), s_f)
    spread = (lax.broadcasted_iota(I32, (SEL_TOPK, OWNER_KEYS), 0)
              == lax.broadcasted_iota(I32, (SEL_TOPK, OWNER_KEYS), 1) // PAGE_SIZE).astype(F32)
    blk_of_key = _dot_tn(idv_ref[0].astype(F32), spread, hi)
    blk_rows = jnp.concatenate([blk_of_key[(h // GQ) * SEQ_PAD:(h // GQ + 1) * SEQ_PAD]
                                for h in range(N_NSA_HEADS)], axis=0).astype(I32)
    blocks_per_page = PAGE_SIZE // SEL_BLOCK
    in_page = lax.broadcasted_iota(I32, (N_QROWS, OWNER_KEYS), 1) % PAGE_SIZE
    in_block = in_page // SEL_BLOCK == blk_rows % blocks_per_page
    dist_s = past_len + tok - ((blk_rows // blocks_per_page) * PAGE_SIZE + in_page)
    s_f = jnp.where((tok >= 0) & in_block & (dist_s >= 0), s_f + _sample_bias_lookup(dist_s, bias_table), NEG_INF)
    kvn = kvn_ref[0]
    s_n = jnp.where(new_ok, _dot_nt(qbd, kvn[:, 2 * LANES:3 * LANES].astype(BF16)) + bias2, NEG_INF)
    p_f, p_n = _softmax_two(s_f, s_n)
    o_s = _dot(p_n.astype(BF16), kvn[:, 3 * LANES:4 * LANES].astype(BF16))
    for o in range(N_OWNERS):
        o_s = o_s + _dot_nt(jnp.where(owner == o, p_f, 0.0).astype(BF16), owner_tiles(o, LANES))

    o_c = oc_ref[0]
    gates = jax.nn.sigmoid(zg_ref[0])
    is_new = lax.broadcasted_iota(I32, (SEQ_PAD, HEAD_DIM), 0) >= SEQ_PAD - N_NEW
    for h in range(N_NSA_HEADS):
        mix = _gated_mix(gates, o_c, o_s, o_w, h, SEQ_PAD)
        out_ref[0, :, h * HEAD_DIM:(h + 1) * HEAD_DIM] = jnp.where(is_new, mix, 0.0)


def _nsa_sel(page_table, ids, ids_rows, rel_bias, nq_s, kv_s, win_cache, win_s, zg_s, o_c, cache3):
    dec_b, n_pages = page_table.shape
    past_len = n_pages * PAGE_SIZE
    seq = lambda w: pl.BlockSpec((1, SEQ_PAD, w), lambda b, pt, ix: (b, 0, 0))
    grid_spec = pltpu.PrefetchScalarGridSpec(
        num_scalar_prefetch=2,
        grid=(dec_b,),
        in_specs=[pl.BlockSpec(memory_space=pltpu.SMEM),
                  seq(NSA_W), seq(4 * KV_W),
                  pl.BlockSpec((1, 2 * KV_W, WINDOW), lambda b, pt, ix: (b, 0, 0)),
                  seq(2 * KV_W), seq(LANES),
                  pl.BlockSpec((1, N_QROWS, LANES), lambda b, pt, ix: (b, 0, 0)),
                  pl.BlockSpec((1, SEL_TOPK, LANES), lambda b, pt, ix: (b, 0, 0)),
                  pl.BlockSpec(memory_space=pl.ANY)],
        out_specs=seq(NSA_W),
        scratch_shapes=[pltpu.VMEM((2, N_FETCH, 2 * LANES, PAGE_SIZE), F32),
                        pltpu.SemaphoreType.DMA((2,))],
    )
    return pl.pallas_call(
        functools.partial(_nsa_sel_body, n_pages=n_pages, past_len=past_len),
        grid_spec=grid_spec,
        out_shape=jax.ShapeDtypeStruct((dec_b, SEQ_PAD, NSA_W), F32),
        compiler_params=pltpu.CompilerParams(dimension_semantics=("arbitrary",),
                                             vmem_limit_bytes=VMEM_LIMIT),
        name="nsa_sample_sel",
    )(page_table.reshape(-1), ids.reshape(-1), rel_bias, nq_s, kv_s, win_cache, win_s, zg_s, o_c, ids_rows, cache3)


FF_CHUNK = 896


def _ffn_body(*refs, tm, has_override):
    if has_override:
        (x_ref, mm_ref, mn_ref, ov_ref, woa_ref, wob_ref, l1w_ref, l1b_ref, wup_ref, bup_ref, cw_ref, cb_ref,
         wdn_ref, bdn_ref, l2w_ref, l2b_ref, y_ref, tail_ref, carry_sc, ubuf) = refs
    else:
        (x_ref, mm_ref, mn_ref, woa_ref, wob_ref, l1w_ref, l1b_ref, wup_ref, bup_ref, cw_ref, cb_ref,
         wdn_ref, bdn_ref, l2w_ref, l2b_ref, y_ref, tail_ref, carry_sc, ubuf) = refs
    j = pl.program_id(1)

    @pl.when(j == 0)
    def _init():
        carry_sc[...] = jnp.zeros(carry_sc.shape, F32)

    x = x_ref[0]
    h = ALPHA * x + (_dot(mm_ref[0].astype(BF16), woa_ref[...]) + _dot(mn_ref[0].astype(BF16), wob_ref[...]))
    x1 = _layer_norm(h, l1w_ref[...], l1b_ref[...])
    x1b = x1.astype(BF16)
    if has_override:
        state_row = lax.broadcasted_iota(I32, (tm, FF_CHUNK), 0) % SEQ_PAD < SEQ_PAD - N_NEW

    def conv_half(c0):
        cols = slice(c0, c0 + FF_CHUNK)
        u = _dot(x1b, wup_ref[:, cols]) + bup_ref[:, cols]
        if has_override:
            u = jnp.where(state_row, ov_ref[0, :, cols], u)
        ubuf[0:8, :] = carry_sc[:, cols]
        ubuf[8:8 + tm, :] = u
        out = cb_ref[:, cols] + ubuf[6:6 + tm, :] * cw_ref[0:1, cols]
        out = out + ubuf[7:7 + tm, :] * cw_ref[1:2, cols]
        out = out + u * cw_ref[2:3, cols]
        carry_sc[:, cols] = u[tm - 8:tm, :]
        tail_ref[0, 0, :, cols] = u if has_override else u[tm - 8:tm, :]
        return out

    ff = jnp.zeros((tm, D_MODEL), F32)
    for c in range(D_FF // FF_CHUNK):
        ga = conv_half(c * FF_CHUNK)
        gb = conv_half(D_FF + c * FF_CHUNK)
        gelu = ga * (0.5 * (1.0 + jnp.tanh(math.sqrt(2.0 / math.pi) * (ga + 0.044715 * (ga * ga * ga)))))
        ff = ff + _dot((gelu * gb).astype(BF16), wdn_ref[c * FF_CHUNK:(c + 1) * FF_CHUNK, :])
    ff = ff + bdn_ref[...]
    y_ref[0] = _layer_norm(ALPHA * x1 + ff, l2w_ref[...], l2b_ref[...])


def _ffn(x, mix_m, mix_n, override, wts):
    bsz, t_len, _ = x.shape
    tm = min(t_len, 512)
    nt = t_len // tm
    has_override = override is not None
    tail_rows = tm if has_override else 8
    row = lambda w: pl.BlockSpec((1, tm, w), lambda b, j: (b, j, 0))
    const = lambda shape: pl.BlockSpec(shape, lambda b, j: (0,) * len(shape), pipeline_mode=pl.Buffered(1))
    in_specs = [row(D_MODEL), row(MLSTM_W), row(NSA_W)]
    args = [x, mix_m, mix_n]
    if has_override:
        in_specs.append(row(2 * D_FF))
        args.append(override)
    in_specs += [const(w.shape) for w in wts]
    args += list(wts)
    y, tail = pl.pallas_call(
        functools.partial(_ffn_body, tm=tm, has_override=has_override),
        grid=(bsz, nt),
        in_specs=in_specs,
        out_specs=[row(D_MODEL), pl.BlockSpec((1, 1, tail_rows, 2 * D_FF), lambda b, j: (b, j, 0, 0))],
        out_shape=[jax.ShapeDtypeStruct((bsz, t_len, D_MODEL), F32),
                   jax.ShapeDtypeStruct((bsz, nt, tail_rows, 2 * D_FF), F32)],
        scratch_shapes=[pltpu.VMEM((8, 2 * D_FF), F32), pltpu.VMEM((8 + tm, FF_CHUNK), F32)],
        compiler_params=pltpu.CompilerParams(dimension_semantics=("arbitrary", "arbitrary"),
                                             vmem_limit_bytes=VMEM_LIMIT),
        name="ffn",
    )(*args)
    return y, tail


def _permute_in_proj(w_in, b_in):
    gates_a = slice(2 * MLSTM_W + 2 * MLSTM_W, 2 * MLSTM_W + 2 * MLSTM_W + 2 * N_MLSTM_HEADS)
    nsa0 = gates_a.stop
    nsa1 = nsa0 + NSA_W + 6 * KV_W

    def perm(a):
        pad = IN_PAD - a.shape[-1]
        parts = [a[..., :gates_a.start], a[..., nsa0:nsa1], a[..., gates_a], a[..., nsa1:]]
        parts.append(jnp.zeros(a.shape[:-1] + (pad,), a.dtype))
        return jnp.concatenate(parts, axis=-1)

    return perm(w_in).astype(BF16), perm(b_in).reshape(1, IN_PAD)


def _front_pad(a, n):
    return jnp.pad(a, ((0, 0), (n, 0), (0, 0)))


def kernel(x_prompt, x_sample, cache_nsa_kv, cache_win_kv, state_mlstm_c, state_mlstm_n, state_mlstm_m,
           state_mlstm_conv, state_ffn_conv, page_table,
           w_in, b_in, mlstm_conv_w, mlstm_conv_b, mlstm_norm_w, nsa_pool_w, rel_bias, w_out,
           ln1_w, ln1_b, w_up, b_up, ffn_conv_w, ffn_conv_b, w_down, b_down, ln2_w, ln2_b):
    bsz, t_len, _ = x_prompt.shape
    dec_b, n_new, _ = x_sample.shape
    assert n_new == N_NEW and cache_win_kv.shape[1] == WINDOW

    w_p, b_p = _permute_in_proj(w_in, b_in)
    pool_w2 = jnp.concatenate([jnp.repeat(nsa_pool_w[0], HEAD_DIM, axis=-1),
                               jnp.repeat(nsa_pool_w[1], HEAD_DIM, axis=-1)], axis=-1)
    row = lambda v: v.reshape(1, -1)
    ffn_w = (w_out[:MLSTM_W].astype(BF16), w_out[MLSTM_W:].astype(BF16), row(ln1_w), row(ln1_b),
             w_up.astype(BF16), row(b_up), ffn_conv_w, row(ffn_conv_b), w_down.astype(BF16), row(b_down),
             row(ln2_w), row(ln2_b))

    zqk, zv, zo, nq, kv, win, zg, kv_t = _proj_in(x_prompt.reshape(bsz * t_len, D_MODEL), w_p, b_p, seq_len=t_len)
    seq3 = lambda a: a.reshape(bsz, t_len, a.shape[-1])
    zqk, zv, zo, nq, kv, win, zg = map(seq3, (zqk, zv, zo, nq, kv, win, zg))
    mix_m, c_p, n_p, m_p = _mlstm(zqk, zv, zo, zg, None, mlstm_conv_w, mlstm_conv_b, mlstm_norm_w, 0)
    mix_n = _nsa_prompt(nq, kv, win, zg, pool_w2, rel_bias)
    y_p, tail_p = _ffn(x_prompt, mix_m, mix_n, None, ffn_w)
    kv_p = jnp.transpose(kv_t.reshape(bsz, 4, N_KV_HEADS, HEAD_DIM, t_len), (0, 4, 1, 2, 3))
    win_p = win[:, t_len - min(WINDOW, t_len):].reshape(bsz, -1, 2, N_KV_HEADS, HEAD_DIM)
    mconv_p = zqk[:, t_len - (MLSTM_CONV - 1):]
    fconv_p = tail_p[:, -1, 8 - (FFN_CONV - 1):]

    pad = SEQ_PAD - N_NEW
    xs = _front_pad(x_sample, pad)
    szqk, szv, szo, snq, skv, swin, szg = _proj_in(xs.reshape(dec_b * SEQ_PAD, D_MODEL), w_p, b_p)
    sseq = lambda a: a.reshape(dec_b, SEQ_PAD, a.shape[-1])
    szqk, szv, szo, snq, skv, swin, szg = map(sseq, (szqk, szv, szo, snq, skv, swin, szg))
    lead = MLSTM_CHUNK - N_NEW
    new = lambda a: a[:, pad:]
    m_qk = jnp.concatenate([jnp.zeros((dec_b, lead - (MLSTM_CONV - 1), 2 * MLSTM_W), F32),
                            state_mlstm_conv.astype(F32), new(szqk)], axis=1)
    mix_ms, c_s, n_s, m_s = _mlstm(m_qk, _front_pad(new(szv), lead), _front_pad(new(szo), lead),
                                   _front_pad(new(szg), lead),
                                   (state_mlstm_c, state_mlstm_n, state_mlstm_m),
                                   mlstm_conv_w, mlstm_conv_b, mlstm_norm_w, lead)
    mix_ms = mix_ms[:, MLSTM_CHUNK - SEQ_PAD:]
    cache3 = jnp.transpose(cache_nsa_kv, (0, 2, 3, 4, 1)).reshape(cache_nsa_kv.shape[0], 4 * KV_W, PAGE_SIZE)
    pool_wt = jnp.tile(jnp.repeat(jnp.transpose(nsa_pool_w, (0, 2, 1)).reshape(2 * N_KV_HEADS, CMP_BLOCK),
                                  HEAD_DIM, axis=0), (1, PAGE_SIZE // CMP_BLOCK))
    o_c, ids_rows = _nsa_cmp(page_table, rel_bias, snq, pool_wt, cache3)
    ids = jnp.transpose(ids_rows[:, :N_PICK, :N_KV_HEADS * SEQ_PAD], (0, 2, 1))
    ids = ids.reshape(dec_b, N_KV_HEADS, SEQ_PAD, N_PICK)[:, :, pad:]
    win_cache = jnp.transpose(cache_win_kv, (0, 2, 3, 4, 1)).reshape(dec_b, 2 * KV_W, WINDOW)
    mix_ns = _nsa_sel(page_table, ids, ids_rows, rel_bias, snq, skv, win_cache, swin, szg, o_c, cache3)
    override = jnp.concatenate([jnp.zeros((dec_b, pad - (FFN_CONV - 1), 2 * D_FF), F32),
                                state_ffn_conv.astype(F32),
                                jnp.zeros((dec_b, N_NEW, 2 * D_FF), F32)], axis=1)
    flat = lambda a: a.reshape(1, dec_b * SEQ_PAD, a.shape[-1])
    y_s8, u_s = _ffn(flat(xs), flat(mix_ms), flat(mix_ns), flat(override), ffn_w)
    y_s = y_s8.reshape(dec_b, SEQ_PAD, D_MODEL)[:, pad:]
    kv_s = new(skv).reshape(dec_b, N_NEW, 4, N_KV_HEADS, HEAD_DIM)
    win_s = jnp.concatenate([cache_win_kv[:, N_NEW:], new(swin).reshape(dec_b, N_NEW, 2, N_KV_HEADS, HEAD_DIM)],
                            axis=1)
    mconv_s = szqk[:, SEQ_PAD - (MLSTM_CONV - 1):]
    fconv_s = u_s.reshape(dec_b, SEQ_PAD, 2 * D_FF)[:, SEQ_PAD - (FFN_CONV - 1):]
    return (y_p, y_s, kv_p, kv_s, win_p, win_s, c_p, c_s, n_p, n_s, m_p, m_s, mconv_p, mconv_s, fconv_p, fconv_s)
```
